```python
import math
import jax, jax.numpy as jnp
from jax import lax
import numpy as np


D_MODEL = 1024
BATCH = 8
SEQ = 4096
DEPTH = 4

N_MIXERS = 2
N_SSD = (DEPTH + 1) // 2
N_GMLP = DEPTH // 2

SSD_EXPAND = 2
SSD_INNER = SSD_EXPAND * D_MODEL
SSD_HEADDIM = 64
SSD_HEADS = SSD_INNER // SSD_HEADDIM
SSD_GROUPS = 8
SSD_STATE = 128
SSD_CONV_DIM = SSD_INNER + 2 * SSD_GROUPS * SSD_STATE
SSD_IN_DIM = 2 * SSD_INNER + 2 * SSD_GROUPS * SSD_STATE + SSD_HEADS
CONV_K = 4
CHUNK = 128
DT_MIN = 0.001
DT_MAX = 0.1

GMLP_INNER = 2 * D_MODEL
GMLP_GROUPS = 16
GMLP_GROUP_DIM = GMLP_INNER // GMLP_GROUPS
GMLP_CHUNK = 128

FFN_DIM = int(math.ceil((8 * D_MODEL / 3) / 256) * 256)

PLE_DIM = 256

RMS_EPS = 1e-6
LN_EPS = 1e-5

kernel_name = 'hybrid_ssd_gmlp_trunk'


def rmsnorm(x, w, eps=RMS_EPS):
    xf = x.astype(jnp.float32)
    y = xf * lax.rsqrt(jnp.mean(xf * xf, axis=-1, keepdims=True) + eps)
    return (y * w.astype(jnp.float32)).astype(x.dtype)


def layernorm(x, w, b, eps=LN_EPS):
    xf = x.astype(jnp.float32)
    mu = jnp.mean(xf, axis=-1, keepdims=True)
    xc = xf - mu
    y = xc * lax.rsqrt(jnp.mean(xc * xc, axis=-1, keepdims=True) + eps)
    return (y * w.astype(jnp.float32) + b.astype(jnp.float32)).astype(x.dtype)


def gated_rmsnorm(y, z, w, eps=LN_EPS):
    g = (y * jax.nn.silu(z)).astype(jnp.float32)
    shp = g.shape
    g = g.reshape(shp[:-1] + (SSD_GROUPS, shp[-1] // SSD_GROUPS))
    g = g * lax.rsqrt(jnp.mean(g * g, axis=-1, keepdims=True) + eps)
    return (g.reshape(shp) * w.astype(jnp.float32)).astype(y.dtype)


def causal_dwconv(x, w, b):
    k, c = w.shape
    y = lax.conv_general_dilated(
        x, w[:, None, :].astype(x.dtype), window_strides=(1,), padding=[(k - 1, 0)],
        dimension_numbers=('NWC', 'WIO', 'NWC'), feature_group_count=c)
    return y + b.astype(x.dtype)


def segsum(a):
    t = a.shape[-1]
    cs = jnp.cumsum(a, axis=-1)
    diff = cs[..., :, None] - cs[..., None, :]
    mask = jnp.tril(jnp.ones((t, t), dtype=bool))
    return jnp.where(mask, diff, -jnp.inf)


def ssd_scan(x, dt, a, bm, cm):
    b, s, h, p = x.shape
    g, n = bm.shape[-2], bm.shape[-1]
    r = h // g
    c = s // CHUNK
    dtype = x.dtype
    xr = (x * dt[..., None]).reshape(b, c, CHUNK, g, r, p)
    da = (dt.astype(jnp.float32) * a).reshape(b, c, CHUNK, g, r).transpose(0, 1, 3, 4, 2)
    da_cs = jnp.cumsum(da, axis=-1)
    br = bm.reshape(b, c, CHUNK, g, n)
    cr = cm.reshape(b, c, CHUNK, g, n)
    lmat = jnp.exp(segsum(da)).astype(dtype)
    cb = jnp.einsum('bclgn,bcsgn->bcgls', cr, br)
    wmat = cb[:, :, :, None] * lmat
    y_diag = jnp.einsum('bcgrls,bcsgrp->bclgrp', wmat, xr)
    dec_states = jnp.exp(da_cs[..., -1:] - da_cs).astype(dtype).transpose(0, 1, 4, 2, 3)
    states = jnp.einsum('bclgn,bclgrp->bcgrpn', br, xr * dec_states[..., None])
    chunk_decay = jnp.exp(da_cs[..., -1]).astype(dtype)

    def step(carry, inp):
        dec_c, st_c = inp
        new = dec_c[..., None, None] * carry + st_c
        return new, carry

    init = jnp.zeros((b, g, r, p, n), dtype=states.dtype)
    _, prev = lax.scan(step, init, (jnp.moveaxis(chunk_decay, 1, 0), jnp.moveaxis(states, 1, 0)))
    prev = jnp.moveaxis(prev, 0, 1)
    dec_out = jnp.exp(da_cs).astype(dtype).transpose(0, 1, 4, 2, 3)
    y_off = jnp.einsum('bclgn,bcgrpn->bclgrp', cr, prev) * dec_out[..., None]
    return (y_diag + y_off).reshape(b, s, h, p)


def ssd_mixer(u, w_in, conv_w, conv_b, dt_bias, a_log, d_skip, norm_w, w_out):
    b, s, _ = u.shape
    zxbcdt = u @ w_in
    z = zxbcdt[..., :SSD_INNER]
    xbc = zxbcdt[..., SSD_INNER:SSD_INNER + SSD_CONV_DIM]
    dt = zxbcdt[..., SSD_INNER + SSD_CONV_DIM:]
    xbc = jax.nn.silu(causal_dwconv(xbc, conv_w, conv_b))
    xs = xbc[..., :SSD_INNER].reshape(b, s, SSD_HEADS, SSD_HEADDIM)
    bm = xbc[..., SSD_INNER:SSD_INNER + SSD_GROUPS * SSD_STATE].reshape(b, s, SSD_GROUPS, SSD_STATE)
    cm = xbc[..., SSD_INNER + SSD_GROUPS * SSD_STATE:].reshape(b, s, SSD_GROUPS, SSD_STATE)
    dt = jax.nn.softplus(dt + dt_bias)
    a = -jnp.exp(a_log.astype(jnp.float32))
    y = ssd_scan(xs, dt, a, bm, cm) + xs * d_skip[:, None]
    y = gated_rmsnorm(y.reshape(b, s, SSD_INNER), z, norm_w)
    return y @ w_out


def gmlp_mixer(u, w_in, b_in, ln_w, ln_b, w_s, b_s, w_out):
    b, s, _ = u.shape
    c = s // GMLP_CHUNK
    hp = jax.nn.gelu(u @ w_in + b_in, approximate=False)
    uu = hp[..., :GMLP_INNER]
    vv = layernorm(hp[..., GMLP_INNER:], ln_w, ln_b)
    vv = vv.reshape(b, c, GMLP_CHUNK, GMLP_GROUPS, GMLP_GROUP_DIM)
    ws = jnp.tril(w_s)
    mixed = jnp.einsum('gts,bcsgd->bctgd', ws, vv) + b_s.T[None, None, :, :, None]
    return (uu * mixed.reshape(b, s, GMLP_INNER)) @ w_out


def swiglu(u, w_gate, w_up, w_down):
    return (jax.nn.silu(u @ w_gate) * (u @ w_up)) @ w_down


def _fwd_setup_inputs(seed: int = 0) -> dict:
    key = jax.random.key(seed)
    ks = jax.random.split(key, 32)
    f32 = jnp.float32

    def nrm(k, shape, scale):
        return jax.random.normal(k, shape, f32) * scale

    x = nrm(ks[0], (BATCH, SEQ, D_MODEL), 1.0)
    p = nrm(ks[1], (DEPTH, BATCH, SEQ, PLE_DIM), 1.0)
    norm_mix = 1.0 + nrm(ks[2], (DEPTH, D_MODEL), 0.02)
    norm_ffn = 1.0 + nrm(ks[3], (DEPTH, D_MODEL), 0.02)
    ssd_w_in = nrm(ks[4], (N_SSD, D_MODEL, SSD_IN_DIM), D_MODEL ** -0.5)
    ssd_conv_w = nrm(ks[5], (N_SSD, CONV_K, SSD_CONV_DIM), CONV_K ** -0.5)
    ssd_conv_b = nrm(ks[6], (N_SSD, SSD_CONV_DIM), 0.02)
    dt0 = jnp.exp(jax.random.uniform(ks[7], (N_SSD, SSD_HEADS), f32)
                  * (math.log(DT_MAX) - math.log(DT_MIN)) + math.log(DT_MIN))
    ssd_dt_bias = dt0 + jnp.log(-jnp.expm1(-dt0))
    ssd_a_log = jnp.log(jax.random.uniform(ks[8], (N_SSD, SSD_HEADS), f32, minval=1.0, maxval=16.0))
    ssd_d = 1.0 + nrm(ks[9], (N_SSD, SSD_HEADS), 0.02)
    ssd_norm_w = 1.0 + nrm(ks[10], (N_SSD, SSD_INNER), 0.02)
    ssd_w_out = nrm(ks[11], (N_SSD, SSD_INNER, D_MODEL), SSD_INNER ** -0.5)
    gmlp_w_in = nrm(ks[12], (N_GMLP, D_MODEL, 2 * GMLP_INNER), D_MODEL ** -0.5)
    gmlp_b_in = nrm(ks[13], (N_GMLP, 2 * GMLP_INNER), 0.02)
    gmlp_ln_w = 1.0 + nrm(ks[14], (N_GMLP, GMLP_INNER), 0.02)
    gmlp_ln_b = nrm(ks[15], (N_GMLP, GMLP_INNER), 0.02)
    gmlp_w_s = nrm(ks[16], (N_GMLP, GMLP_GROUPS, GMLP_CHUNK, GMLP_CHUNK), GMLP_CHUNK ** -0.5)
    gmlp_b_s = 1.0 + nrm(ks[17], (N_GMLP, GMLP_GROUPS, GMLP_CHUNK), 0.02)
    gmlp_w_out = nrm(ks[18], (N_GMLP, GMLP_INNER, D_MODEL), GMLP_INNER ** -0.5)
    ffn_w_gate = nrm(ks[19], (DEPTH, D_MODEL, FFN_DIM), D_MODEL ** -0.5)
    ffn_w_up = nrm(ks[20], (DEPTH, D_MODEL, FFN_DIM), D_MODEL ** -0.5)
    ffn_w_down = nrm(ks[21], (DEPTH, FFN_DIM, D_MODEL), FFN_DIM ** -0.5)
    ple_w_proj = nrm(ks[22], (DEPTH, PLE_DIM, D_MODEL), PLE_DIM ** -0.5)
    ple_norm = 1.0 + nrm(ks[23], (DEPTH, D_MODEL), 0.02)
    ple_gate_norm = 1.0 + nrm(ks[24], (DEPTH, D_MODEL), 0.02)
    ple_w_gate = nrm(ks[25], (DEPTH, D_MODEL, D_MODEL), D_MODEL ** -0.5)
    final_norm = 1.0 + nrm(ks[26], (D_MODEL,), 0.02)
    return {'x': x, 'p': p, 'norm_mix': norm_mix, 'norm_ffn': norm_ffn,
            'ssd_w_in': ssd_w_in, 'ssd_conv_w': ssd_conv_w, 'ssd_conv_b': ssd_conv_b,
            'ssd_dt_bias': ssd_dt_bias, 'ssd_a_log': ssd_a_log, 'ssd_d': ssd_d,
            'ssd_norm_w': ssd_norm_w, 'ssd_w_out': ssd_w_out,
            'gmlp_w_in': gmlp_w_in, 'gmlp_b_in': gmlp_b_in, 'gmlp_ln_w': gmlp_ln_w,
            'gmlp_ln_b': gmlp_ln_b, 'gmlp_w_s': gmlp_w_s, 'gmlp_b_s': gmlp_b_s,
            'gmlp_w_out': gmlp_w_out,
            'ffn_w_gate': ffn_w_gate, 'ffn_w_up': ffn_w_up, 'ffn_w_down': ffn_w_down,
            'ple_w_proj': ple_w_proj, 'ple_norm': ple_norm, 'ple_gate_norm': ple_gate_norm,
            'ple_w_gate': ple_w_gate, 'final_norm': final_norm}


def _fwd_reference(x, p, norm_mix, norm_ffn,
              ssd_w_in, ssd_conv_w, ssd_conv_b, ssd_dt_bias, ssd_a_log, ssd_d, ssd_norm_w, ssd_w_out,
              gmlp_w_in, gmlp_b_in, gmlp_ln_w, gmlp_ln_b, gmlp_w_s, gmlp_b_s, gmlp_w_out,
              ffn_w_gate, ffn_w_up, ffn_w_down,
              ple_w_proj, ple_norm, ple_gate_norm, ple_w_gate, final_norm):
    h = x
    for i in range(DEPTH):
        j = i // N_MIXERS
        hn = rmsnorm(h, norm_mix[i])
        if i % N_MIXERS == 0:
            mix = ssd_mixer(hn, ssd_w_in[j], ssd_conv_w[j], ssd_conv_b[j], ssd_dt_bias[j],
                            ssd_a_log[j], ssd_d[j], ssd_norm_w[j], ssd_w_out[j])
        else:
            mix = gmlp_mixer(hn, gmlp_w_in[j], gmlp_b_in[j], gmlp_ln_w[j], gmlp_ln_b[j],
                             gmlp_w_s[j], gmlp_b_s[j], gmlp_w_out[j])
        h = h + mix
        h = h + swiglu(rmsnorm(h, norm_ffn[i]), ffn_w_gate[i], ffn_w_up[i], ffn_w_down[i])
        e = rmsnorm(p[i] @ ple_w_proj[i], ple_norm[i])
        gate = jax.nn.sigmoid(rmsnorm(h, ple_gate_norm[i]) @ ple_w_gate[i])
        h = h + gate * e
    return rmsnorm(h, final_norm)


import jax as _jax
import jax.numpy as _jnp

TWIN_FORMAT = 'train_step'
FWD_PARAMS = ['x', 'p', 'norm_mix', 'norm_ffn', 'ssd_w_in', 'ssd_conv_w', 'ssd_conv_b', 'ssd_dt_bias', 'ssd_a_log', 'ssd_d', 'ssd_norm_w', 'ssd_w_out', 'gmlp_w_in', 'gmlp_b_in', 'gmlp_ln_w', 'gmlp_ln_b', 'gmlp_w_s', 'gmlp_b_s', 'gmlp_w_out', 'ffn_w_gate', 'ffn_w_up', 'ffn_w_down', 'ple_w_proj', 'ple_norm', 'ple_gate_norm', 'ple_w_gate', 'final_norm']
TWIN_WEIGHTS = ['norm_mix', 'norm_ffn', 'ssd_w_in', 'ssd_conv_w', 'ssd_conv_b', 'ssd_dt_bias', 'ssd_a_log', 'ssd_d', 'ssd_norm_w', 'ssd_w_out', 'gmlp_w_in', 'gmlp_b_in', 'gmlp_ln_w', 'gmlp_ln_b', 'gmlp_w_s', 'gmlp_b_s', 'gmlp_w_out', 'ffn_w_gate', 'ffn_w_up', 'ffn_w_down', 'ple_w_proj', 'ple_norm', 'ple_gate_norm', 'ple_w_gate', 'final_norm']
TWIN_DIFF_INPUT = 'x'
TWIN_INPUTS = ['x', 'p', 'norm_mix', 'norm_ffn', 'ssd_w_in', 'ssd_conv_w', 'ssd_conv_b', 'ssd_dt_bias', 'ssd_a_log', 'ssd_d', 'ssd_norm_w', 'ssd_w_out', 'gmlp_w_in', 'gmlp_b_in', 'gmlp_ln_w', 'gmlp_ln_b', 'gmlp_w_s', 'gmlp_b_s', 'gmlp_w_out', 'ffn_w_gate', 'ffn_w_up', 'ffn_w_down', 'ple_w_proj', 'ple_norm', 'ple_gate_norm', 'ple_w_gate', 'final_norm', 'loss_target', 'm_norm_mix', 'm_norm_ffn', 'm_ssd_w_in', 'm_ssd_conv_w', 'm_ssd_conv_b', 'm_ssd_dt_bias', 'm_ssd_a_log', 'm_ssd_d', 'm_ssd_norm_w', 'm_ssd_w_out', 'm_gmlp_w_in', 'm_gmlp_b_in', 'm_gmlp_ln_w', 'm_gmlp_ln_b', 'm_gmlp_w_s', 'm_gmlp_b_s', 'm_gmlp_w_out', 'm_ffn_w_gate', 'm_ffn_w_up', 'm_ffn_w_down', 'm_ple_w_proj', 'm_ple_norm', 'm_ple_gate_norm', 'm_ple_w_gate', 'm_final_norm', 'v_norm_mix', 'v_norm_ffn', 'v_ssd_w_in', 'v_ssd_conv_w', 'v_ssd_conv_b', 'v_ssd_dt_bias', 'v_ssd_a_log', 'v_ssd_d', 'v_ssd_norm_w', 'v_ssd_w_out', 'v_gmlp_w_in', 'v_gmlp_b_in', 'v_gmlp_ln_w', 'v_gmlp_ln_b', 'v_gmlp_w_s', 'v_gmlp_b_s', 'v_gmlp_w_out', 'v_ffn_w_gate', 'v_ffn_w_up', 'v_ffn_w_down', 'v_ple_w_proj', 'v_ple_norm', 'v_ple_gate_norm', 'v_ple_w_gate', 'v_final_norm']
TWIN_OUTPUTS = ['loss', 'grad_x', 'grad_norm_mix', 'grad_norm_ffn', 'grad_ssd_w_in', 'grad_ssd_conv_w', 'grad_ssd_conv_b', 'grad_ssd_dt_bias', 'grad_ssd_a_log', 'grad_ssd_d', 'grad_ssd_norm_w', 'grad_ssd_w_out', 'grad_gmlp_w_in', 'grad_gmlp_b_in', 'grad_gmlp_ln_w', 'grad_gmlp_ln_b', 'grad_gmlp_w_s', 'grad_gmlp_b_s', 'grad_gmlp_w_out', 'grad_ffn_w_gate', 'grad_ffn_w_up', 'grad_ffn_w_down', 'grad_ple_w_proj', 'grad_ple_norm', 'grad_ple_gate_norm', 'grad_ple_w_gate', 'grad_final_norm', 'delta_norm_mix', 'delta_norm_ffn', 'delta_ssd_w_in', 'delta_ssd_conv_w', 'delta_ssd_conv_b', 'delta_ssd_dt_bias', 'delta_ssd_a_log', 'delta_ssd_d', 'delta_ssd_norm_w', 'delta_ssd_w_out', 'delta_gmlp_w_in', 'delta_gmlp_b_in', 'delta_gmlp_ln_w', 'delta_gmlp_ln_b', 'delta_gmlp_w_s', 'delta_gmlp_b_s', 'delta_gmlp_w_out', 'delta_ffn_w_gate', 'delta_ffn_w_up', 'delta_ffn_w_down', 'delta_ple_w_proj', 'delta_ple_norm', 'delta_ple_gate_norm', 'delta_ple_w_gate', 'delta_final_norm', 'new_m_norm_mix', 'new_m_norm_ffn', 'new_m_ssd_w_in', 'new_m_ssd_conv_w', 'new_m_ssd_conv_b', 'new_m_ssd_dt_bias', 'new_m_ssd_a_log', 'new_m_ssd_d', 'new_m_ssd_norm_w', 'new_m_ssd_w_out', 'new_m_gmlp_w_in', 'new_m_gmlp_b_in', 'new_m_gmlp_ln_w', 'new_m_gmlp_ln_b', 'new_m_gmlp_w_s', 'new_m_gmlp_b_s', 'new_m_gmlp_w_out', 'new_m_ffn_w_gate', 'new_m_ffn_w_up', 'new_m_ffn_w_down', 'new_m_ple_w_proj', 'new_m_ple_norm', 'new_m_ple_gate_norm', 'new_m_ple_w_gate', 'new_m_final_norm', 'new_v_norm_mix', 'new_v_norm_ffn', 'new_v_ssd_w_in', 'new_v_ssd_conv_w', 'new_v_ssd_conv_b', 'new_v_ssd_dt_bias', 'new_v_ssd_a_log', 'new_v_ssd_d', 'new_v_ssd_norm_w', 'new_v_ssd_w_out', 'new_v_gmlp_w_in', 'new_v_gmlp_b_in', 'new_v_gmlp_ln_w', 'new_v_gmlp_ln_b', 'new_v_gmlp_w_s', 'new_v_gmlp_b_s', 'new_v_gmlp_w_out', 'new_v_ffn_w_gate', 'new_v_ffn_w_up', 'new_v_ffn_w_down', 'new_v_ple_w_proj', 'new_v_ple_norm', 'new_v_ple_gate_norm', 'new_v_ple_w_gate', 'new_v_final_norm']
TWIN_LEAF_KINDS = {'loss': 'loss', 'grad_x': 'grad_x', 'grad_norm_mix': 'grad_w', 'grad_norm_ffn': 'grad_w', 'grad_ssd_w_in': 'grad_w', 'grad_ssd_conv_w': 'grad_w', 'grad_ssd_conv_b': 'grad_w', 'grad_ssd_dt_bias': 'grad_w', 'grad_ssd_a_log': 'grad_w', 'grad_ssd_d': 'grad_w', 'grad_ssd_norm_w': 'grad_w', 'grad_ssd_w_out': 'grad_w', 'grad_gmlp_w_in': 'grad_w', 'grad_gmlp_b_in': 'grad_w', 'grad_gmlp_ln_w': 'grad_w', 'grad_gmlp_ln_b': 'grad_w', 'grad_gmlp_w_s': 'grad_w', 'grad_gmlp_b_s': 'grad_w', 'grad_gmlp_w_out': 'grad_w', 'grad_ffn_w_gate': 'grad_w', 'grad_ffn_w_up': 'grad_w', 'grad_ffn_w_down': 'grad_w', 'grad_ple_w_proj': 'grad_w', 'grad_ple_norm': 'grad_w', 'grad_ple_gate_norm': 'grad_w', 'grad_ple_w_gate': 'grad_w', 'grad_final_norm': 'grad_w', 'delta_norm_mix': 'delta_w', 'delta_norm_ffn': 'delta_w', 'delta_ssd_w_in': 'delta_w', 'delta_ssd_conv_w': 'delta_w', 'delta_ssd_conv_b': 'delta_w', 'delta_ssd_dt_bias': 'delta_w', 'delta_ssd_a_log': 'delta_w', 'delta_ssd_d': 'delta_w', 'delta_ssd_norm_w': 'delta_w', 'delta_ssd_w_out': 'delta_w', 'delta_gmlp_w_in': 'delta_w', 'delta_gmlp_b_in': 'delta_w', 'delta_gmlp_ln_w': 'delta_w', 'delta_gmlp_ln_b': 'delta_w', 'delta_gmlp_w_s': 'delta_w', 'delta_gmlp_b_s': 'delta_w', 'delta_gmlp_w_out': 'delta_w', 'delta_ffn_w_gate': 'delta_w', 'delta_ffn_w_up': 'delta_w', 'delta_ffn_w_down': 'delta_w', 'delta_ple_w_proj': 'delta_w', 'delta_ple_norm': 'delta_w', 'delta_ple_gate_norm': 'delta_w', 'delta_ple_w_gate': 'delta_w', 'delta_final_norm': 'delta_w', 'new_m_norm_mix': 'new_m', 'new_m_norm_ffn': 'new_m', 'new_m_ssd_w_in': 'new_m', 'new_m_ssd_conv_w': 'new_m', 'new_m_ssd_conv_b': 'new_m', 'new_m_ssd_dt_bias': 'new_m', 'new_m_ssd_a_log': 'new_m', 'new_m_ssd_d': 'new_m', 'new_m_ssd_norm_w': 'new_m', 'new_m_ssd_w_out': 'new_m', 'new_m_gmlp_w_in': 'new_m', 'new_m_gmlp_b_in': 'new_m', 'new_m_gmlp_ln_w': 'new_m', 'new_m_gmlp_ln_b': 'new_m', 'new_m_gmlp_w_s': 'new_m', 'new_m_gmlp_b_s': 'new_m', 'new_m_gmlp_w_out': 'new_m', 'new_m_ffn_w_gate': 'new_m', 'new_m_ffn_w_up': 'new_m', 'new_m_ffn_w_down': 'new_m', 'new_m_ple_w_proj': 'new_m', 'new_m_ple_norm': 'new_m', 'new_m_ple_gate_norm': 'new_m', 'new_m_ple_w_gate': 'new_m', 'new_m_final_norm': 'new_m', 'new_v_norm_mix': 'new_v', 'new_v_norm_ffn': 'new_v', 'new_v_ssd_w_in': 'new_v', 'new_v_ssd_conv_w': 'new_v', 'new_v_ssd_conv_b': 'new_v', 'new_v_ssd_dt_bias': 'new_v', 'new_v_ssd_a_log': 'new_v', 'new_v_ssd_d': 'new_v', 'new_v_ssd_norm_w': 'new_v', 'new_v_ssd_w_out': 'new_v', 'new_v_gmlp_w_in': 'new_v', 'new_v_gmlp_b_in': 'new_v', 'new_v_gmlp_ln_w': 'new_v', 'new_v_gmlp_ln_b': 'new_v', 'new_v_gmlp_w_s': 'new_v', 'new_v_gmlp_b_s': 'new_v', 'new_v_gmlp_w_out': 'new_v', 'new_v_ffn_w_gate': 'new_v', 'new_v_ffn_w_up': 'new_v', 'new_v_ffn_w_down': 'new_v', 'new_v_ple_w_proj': 'new_v', 'new_v_ple_norm': 'new_v', 'new_v_ple_gate_norm': 'new_v', 'new_v_ple_w_gate': 'new_v', 'new_v_final_norm': 'new_v'}


def _forward(args):
    return _fwd_reference(*[args[k] for k in FWD_PARAMS])


def _output_shape():
    out = _jax.eval_shape(lambda: _forward(_fwd_setup_inputs(0)))
    return out.shape, out.dtype

N_MICROBATCH = 1
ADAM_LR = 0.001
ADAM_B1 = 0.9
ADAM_B2 = 0.999
ADAM_EPS = 1e-08
ADAM_WD = 0.01
ADAM_STEP = 10
PER_EXAMPLE_BATCH_AXIS = {'x': 0, 'p': 1, 'loss_target': 0}
SHARED_INPUTS = []
_WEIGHT_DTYPES = {'norm_mix': _jnp.float32, 'norm_ffn': _jnp.float32, 'ssd_w_in': _jnp.float32, 'ssd_conv_w': _jnp.float32, 'ssd_conv_b': _jnp.float32, 'ssd_dt_bias': _jnp.float32, 'ssd_a_log': _jnp.float32, 'ssd_d': _jnp.float32, 'ssd_norm_w': _jnp.float32, 'ssd_w_out': _jnp.float32, 'gmlp_w_in': _jnp.float32, 'gmlp_b_in': _jnp.float32, 'gmlp_ln_w': _jnp.float32, 'gmlp_ln_b': _jnp.float32, 'gmlp_w_s': _jnp.float32, 'gmlp_b_s': _jnp.float32, 'gmlp_w_out': _jnp.float32, 'ffn_w_gate': _jnp.float32, 'ffn_w_up': _jnp.float32, 'ffn_w_down': _jnp.float32, 'ple_w_proj': _jnp.float32, 'ple_norm': _jnp.float32, 'ple_gate_norm': _jnp.float32, 'ple_w_gate': _jnp.float32, 'final_norm': _jnp.float32}
MOMENT_SCALE = {'norm_mix': 1.584787e-01, 'norm_ffn': 8.615879e-02, 'ssd_w_in': 7.762106e-02, 'ssd_conv_w': 6.854438e-02, 'ssd_conv_b': 1.037003e-01, 'ssd_dt_bias': 1.952119e-01, 'ssd_a_log': 3.865906e-01, 'ssd_d': 4.236039e-01, 'ssd_norm_w': 9.560398e-02, 'ssd_w_out': 1.295488e-01, 'gmlp_w_in': 4.749765e-02, 'gmlp_b_in': 5.453172e-02, 'gmlp_ln_w': 3.265159e-02, 'gmlp_ln_b': 3.228630e-02, 'gmlp_w_s': 3.125731e-02, 'gmlp_b_s': 4.443068e-02, 'gmlp_w_out': 8.017988e-02, 'ffn_w_gate': 3.738649e-02, 'ffn_w_up': 3.630412e-02, 'ffn_w_down': 6.029297e-02, 'ple_w_proj': 5.419864e-02, 'ple_norm': 6.211145e-02, 'ple_gate_norm': 2.158702e-02, 'ple_w_gate': 2.131147e-02, 'final_norm': 3.207915e+01}


def _to_microbatches(a, axis):
    t = _jnp.moveaxis(a, axis, 0)
    t = t.reshape((N_MICROBATCH, t.shape[0] // N_MICROBATCH) + t.shape[1:])
    return _jnp.moveaxis(t, 1, axis + 1)


def setup_inputs(seed: int = 0) -> dict:
    inp = _fwd_setup_inputs(seed)
    key = _jax.random.fold_in(_jax.random.key(seed), 7919)
    shape, _ = _output_shape()
    out = dict(inp)
    out["loss_target"] = _jax.random.normal(_jax.random.fold_in(key, 0), shape, _jnp.float32)
    for i, name in enumerate(TWIN_WEIGHTS):
        w = inp[name].astype(_jnp.float32)
        if MOMENT_SCALE is None:
            s = _jnp.sqrt(_jnp.mean(_jnp.square(w)) + 1e-30)
        else:
            s = MOMENT_SCALE[name]
        km, kv = _jax.random.split(_jax.random.fold_in(key, i + 1))
        out[name] = w
        out["m_" + name] = s * _jax.random.normal(km, w.shape, _jnp.float32)
        out["v_" + name] = (s * s) * _jax.random.uniform(kv, w.shape, _jnp.float32, 0.5, 1.5)
    if N_MICROBATCH > 1:
        for name, axis in PER_EXAMPLE_BATCH_AXIS.items():
            out[name] = _to_microbatches(out[name], axis)
    return {'x': out['x'], 'p': out['p'], 'norm_mix': out['norm_mix'], 'norm_ffn': out['norm_ffn'], 'ssd_w_in': out['ssd_w_in'], 'ssd_conv_w': out['ssd_conv_w'], 'ssd_conv_b': out['ssd_conv_b'], 'ssd_dt_bias': out['ssd_dt_bias'], 'ssd_a_log': out['ssd_a_log'], 'ssd_d': out['ssd_d'], 'ssd_norm_w': out['ssd_norm_w'], 'ssd_w_out': out['ssd_w_out'], 'gmlp_w_in': out['gmlp_w_in'], 'gmlp_b_in': out['gmlp_b_in'], 'gmlp_ln_w': out['gmlp_ln_w'], 'gmlp_ln_b': out['gmlp_ln_b'], 'gmlp_w_s': out['gmlp_w_s'], 'gmlp_b_s': out['gmlp_b_s'], 'gmlp_w_out': out['gmlp_w_out'], 'ffn_w_gate': out['ffn_w_gate'], 'ffn_w_up': out['ffn_w_up'], 'ffn_w_down': out['ffn_w_down'], 'ple_w_proj': out['ple_w_proj'], 'ple_norm': out['ple_norm'], 'ple_gate_norm': out['ple_gate_norm'], 'ple_w_gate': out['ple_w_gate'], 'final_norm': out['final_norm'], 'loss_target': out['loss_target'], 'm_norm_mix': out['m_norm_mix'], 'm_norm_ffn': out['m_norm_ffn'], 'm_ssd_w_in': out['m_ssd_w_in'], 'm_ssd_conv_w': out['m_ssd_conv_w'], 'm_ssd_conv_b': out['m_ssd_conv_b'], 'm_ssd_dt_bias': out['m_ssd_dt_bias'], 'm_ssd_a_log': out['m_ssd_a_log'], 'm_ssd_d': out['m_ssd_d'], 'm_ssd_norm_w': out['m_ssd_norm_w'], 'm_ssd_w_out': out['m_ssd_w_out'], 'm_gmlp_w_in': out['m_gmlp_w_in'], 'm_gmlp_b_in': out['m_gmlp_b_in'], 'm_gmlp_ln_w': out['m_gmlp_ln_w'], 'm_gmlp_ln_b': out['m_gmlp_ln_b'], 'm_gmlp_w_s': out['m_gmlp_w_s'], 'm_gmlp_b_s': out['m_gmlp_b_s'], 'm_gmlp_w_out': out['m_gmlp_w_out'], 'm_ffn_w_gate': out['m_ffn_w_gate'], 'm_ffn_w_up': out['m_ffn_w_up'], 'm_ffn_w_down': out['m_ffn_w_down'], 'm_ple_w_proj': out['m_ple_w_proj'], 'm_ple_norm': out['m_ple_norm'], 'm_ple_gate_norm': out['m_ple_gate_norm'], 'm_ple_w_gate': out['m_ple_w_gate'], 'm_final_norm': out['m_final_norm'], 'v_norm_mix': out['v_norm_mix'], 'v_norm_ffn': out['v_norm_ffn'], 'v_ssd_w_in': out['v_ssd_w_in'], 'v_ssd_conv_w': out['v_ssd_conv_w'], 'v_ssd_conv_b': out['v_ssd_conv_b'], 'v_ssd_dt_bias': out['v_ssd_dt_bias'], 'v_ssd_a_log': out['v_ssd_a_log'], 'v_ssd_d': out['v_ssd_d'], 'v_ssd_norm_w': out['v_ssd_norm_w'], 'v_ssd_w_out': out['v_ssd_w_out'], 'v_gmlp_w_in': out['v_gmlp_w_in'], 'v_gmlp_b_in': out['v_gmlp_b_in'], 'v_gmlp_ln_w': out['v_gmlp_ln_w'], 'v_gmlp_ln_b': out['v_gmlp_ln_b'], 'v_gmlp_w_s': out['v_gmlp_w_s'], 'v_gmlp_b_s': out['v_gmlp_b_s'], 'v_gmlp_w_out': out['v_gmlp_w_out'], 'v_ffn_w_gate': out['v_ffn_w_gate'], 'v_ffn_w_up': out['v_ffn_w_up'], 'v_ffn_w_down': out['v_ffn_w_down'], 'v_ple_w_proj': out['v_ple_w_proj'], 'v_ple_norm': out['v_ple_norm'], 'v_ple_gate_norm': out['v_ple_gate_norm'], 'v_ple_w_gate': out['v_ple_w_gate'], 'v_final_norm': out['v_final_norm']}


def _loss(weights, diff, rest, loss_target):
    with _jax.named_scope("forward"):
        args = {**rest, TWIN_DIFF_INPUT: diff, **{k: w.astype(_WEIGHT_DTYPES[k]) for k, w in weights.items()}}
        y = _forward(args)
    with _jax.named_scope("loss_head"):
        err = _jnp.square(y.astype(_jnp.float32) - loss_target)
        return 0.5 * _jnp.sum(_jnp.mean(err, axis=-1)) if err.ndim else 0.5 * err


def _adamw(w, g, m, v):
    m = ADAM_B1 * m + (1.0 - ADAM_B1) * g
    v = ADAM_B2 * v + (1.0 - ADAM_B2) * _jnp.square(g)
    m_hat = m / (1.0 - ADAM_B1 ** ADAM_STEP)
    v_hat = v / (1.0 - ADAM_B2 ** ADAM_STEP)
    delta = -ADAM_LR * (m_hat / (_jnp.sqrt(v_hat) + ADAM_EPS) + ADAM_WD * w)
    return delta, m, v


def reference(x, p, norm_mix, norm_ffn, ssd_w_in, ssd_conv_w, ssd_conv_b, ssd_dt_bias, ssd_a_log, ssd_d, ssd_norm_w, ssd_w_out, gmlp_w_in, gmlp_b_in, gmlp_ln_w, gmlp_ln_b, gmlp_w_s, gmlp_b_s, gmlp_w_out, ffn_w_gate, ffn_w_up, ffn_w_down, ple_w_proj, ple_norm, ple_gate_norm, ple_w_gate, final_norm, loss_target, m_norm_mix, m_norm_ffn, m_ssd_w_in, m_ssd_conv_w, m_ssd_conv_b, m_ssd_dt_bias, m_ssd_a_log, m_ssd_d, m_ssd_norm_w, m_ssd_w_out, m_gmlp_w_in, m_gmlp_b_in, m_gmlp_ln_w, m_gmlp_ln_b, m_gmlp_w_s, m_gmlp_b_s, m_gmlp_w_out, m_ffn_w_gate, m_ffn_w_up, m_ffn_w_down, m_ple_w_proj, m_ple_norm, m_ple_gate_norm, m_ple_w_gate, m_final_norm, v_norm_mix, v_norm_ffn, v_ssd_w_in, v_ssd_conv_w, v_ssd_conv_b, v_ssd_dt_bias, v_ssd_a_log, v_ssd_d, v_ssd_norm_w, v_ssd_w_out, v_gmlp_w_in, v_gmlp_b_in, v_gmlp_ln_w, v_gmlp_ln_b, v_gmlp_w_s, v_gmlp_b_s, v_gmlp_w_out, v_ffn_w_gate, v_ffn_w_up, v_ffn_w_down, v_ple_w_proj, v_ple_norm, v_ple_gate_norm, v_ple_w_gate, v_final_norm):
    given = dict(x=x, p=p, norm_mix=norm_mix, norm_ffn=norm_ffn, ssd_w_in=ssd_w_in, ssd_conv_w=ssd_conv_w, ssd_conv_b=ssd_conv_b, ssd_dt_bias=ssd_dt_bias, ssd_a_log=ssd_a_log, ssd_d=ssd_d, ssd_norm_w=ssd_norm_w, ssd_w_out=ssd_w_out, gmlp_w_in=gmlp_w_in, gmlp_b_in=gmlp_b_in, gmlp_ln_w=gmlp_ln_w, gmlp_ln_b=gmlp_ln_b, gmlp_w_s=gmlp_w_s, gmlp_b_s=gmlp_b_s, gmlp_w_out=gmlp_w_out, ffn_w_gate=ffn_w_gate, ffn_w_up=ffn_w_up, ffn_w_down=ffn_w_down, ple_w_proj=ple_w_proj, ple_norm=ple_norm, ple_gate_norm=ple_gate_norm, ple_w_gate=ple_w_gate, final_norm=final_norm, loss_target=loss_target, m_norm_mix=m_norm_mix, m_norm_ffn=m_norm_ffn, m_ssd_w_in=m_ssd_w_in, m_ssd_conv_w=m_ssd_conv_w, m_ssd_conv_b=m_ssd_conv_b, m_ssd_dt_bias=m_ssd_dt_bias, m_ssd_a_log=m_ssd_a_log, m_ssd_d=m_ssd_d, m_ssd_norm_w=m_ssd_norm_w, m_ssd_w_out=m_ssd_w_out, m_gmlp_w_in=m_gmlp_w_in, m_gmlp_b_in=m_gmlp_b_in, m_gmlp_ln_w=m_gmlp_ln_w, m_gmlp_ln_b=m_gmlp_ln_b, m_gmlp_w_s=m_gmlp_w_s, m_gmlp_b_s=m_gmlp_b_s, m_gmlp_w_out=m_gmlp_w_out, m_ffn_w_gate=m_ffn_w_gate, m_ffn_w_up=m_ffn_w_up, m_ffn_w_down=m_ffn_w_down, m_ple_w_proj=m_ple_w_proj, m_ple_norm=m_ple_norm, m_ple_gate_norm=m_ple_gate_norm, m_ple_w_gate=m_ple_w_gate, m_final_norm=m_final_norm, v_norm_mix=v_norm_mix, v_norm_ffn=v_norm_ffn, v_ssd_w_in=v_ssd_w_in, v_ssd_conv_w=v_ssd_conv_w, v_ssd_conv_b=v_ssd_conv_b, v_ssd_dt_bias=v_ssd_dt_bias, v_ssd_a_log=v_ssd_a_log, v_ssd_d=v_ssd_d, v_ssd_norm_w=v_ssd_norm_w, v_ssd_w_out=v_ssd_w_out, v_gmlp_w_in=v_gmlp_w_in, v_gmlp_b_in=v_gmlp_b_in, v_gmlp_ln_w=v_gmlp_ln_w, v_gmlp_ln_b=v_gmlp_ln_b, v_gmlp_w_s=v_gmlp_w_s, v_gmlp_b_s=v_gmlp_b_s, v_gmlp_w_out=v_gmlp_w_out, v_ffn_w_gate=v_ffn_w_gate, v_ffn_w_up=v_ffn_w_up, v_ffn_w_down=v_ffn_w_down, v_ple_w_proj=v_ple_w_proj, v_ple_norm=v_ple_norm, v_ple_gate_norm=v_ple_gate_norm, v_ple_w_gate=v_ple_w_gate, v_final_norm=v_final_norm)
    weights = {n: given[n] for n in TWIN_WEIGHTS}
    shared = {n: given[n] for n in SHARED_INPUTS}
    per_example = {n: given[n] for n in ['x', 'p']}
    grad_fn = _jax.value_and_grad(_loss, argnums=(0, 1))

    def one_microbatch(ex, loss_target):
        ex = dict(ex)
        diff = ex.pop(TWIN_DIFF_INPUT)
        return grad_fn(weights, diff, {**shared, **ex}, loss_target)

    if N_MICROBATCH == 1:
        loss, (grad_w, grad_x) = one_microbatch(per_example, given["loss_target"])
    else:
        def body(carry, xs):
            loss_sum, grad_sum = carry
            l_k, (gw_k, gx_k) = one_microbatch(xs[0], xs[1])
            with _jax.named_scope("update"):
                return (loss_sum + l_k, _jax.tree.map(_jnp.add, grad_sum, gw_k)), gx_k

        init = (_jnp.zeros((), _jnp.float32), _jax.tree.map(_jnp.zeros_like, weights))
        (loss, grad_w), grad_x = _jax.lax.scan(body, init, (per_example, given["loss_target"]))
    with _jax.named_scope("update"):
        delta_w, new_m, new_v = {}, {}, {}
        for n in TWIN_WEIGHTS:
            delta_w[n], new_m[n], new_v[n] = _adamw(weights[n], grad_w[n], given["m_" + n], given["v_" + n])
    return (loss, grad_x, *[grad_w[n] for n in TWIN_WEIGHTS], *[delta_w[n] for n in TWIN_WEIGHTS],
            *[new_m[n] for n in TWIN_WEIGHTS], *[new_v[n] for n in TWIN_WEIGHTS])
```

```python
import functools
import math

import jax
import jax.numpy as jnp
from jax import lax
from jax.experimental import pallas as pl
from jax.experimental.pallas import tpu as pltpu

F32 = jnp.float32
BF16 = jnp.bfloat16

N_DEV = 8
D_MODEL = 1024
DEPTH = 4
SSD_INNER = 2048
SSD_HEADS = 32
SSD_HEADDIM = 64
SSD_GROUPS = 8
SSD_STATE = 128
SSD_GROUP_W = SSD_INNER // SSD_GROUPS
SSD_CONV_DIM = SSD_INNER + 2 * SSD_GROUPS * SSD_STATE
SSD_IN_DIM = 2 * SSD_INNER + SSD_CONV_DIM - SSD_INNER + SSD_HEADS
SSD_ZX = SSD_INNER + SSD_CONV_DIM
CONV_K = 4
CHUNK = 128
GMLP_INNER = 2048
GMLP_GROUPS = 16
FFN_DIM = 2816
PLE_DIM = 256
RMS_EPS = 1e-6
LN_EPS = 1e-5
LANES = 128
VMEM_LIMIT = 56 * 1024 * 1024

ADAM_LR = 0.001
ADAM_B1 = 0.9
ADAM_B2 = 0.999
ADAM_EPS = 1e-08
ADAM_WD = 0.01
ADAM_STEP = 10

HI = lax.Precision.HIGHEST
MESH_ID = pl.DeviceIdType.MESH


def _pick(n, cands):
    for c in cands:
        if c <= n and n % c == 0:
            return c
    return n


def _params(dims):
    return pltpu.CompilerParams(dimension_semantics=dims, vmem_limit_bytes=VMEM_LIMIT)


def _dot(a, b, dims=(((1,), (0,)), ((), ())), precision=None):
    return lax.dot_general(a, b, dims, precision=precision, preferred_element_type=F32)


NN = (((1,), (0,)), ((), ()))
NT = (((1,), (1,)), ((), ()))
TN = (((0,), (0,)), ((), ()))


def _sigmoid(x):
    return 1.0 / (1.0 + jnp.exp(-x))


def _mm(a, b, mode, out_dtype, add=None):
    if mode == "nn":
        m, k = a.shape
        n = b.shape[1]
    elif mode == "nt":
        m, k = a.shape
        n = b.shape[0]
    else:
        k, m = a.shape
        n = b.shape[1]
    if mode == "tn":
        tm = _pick(m, (1408, 1024, 512, 256, 128))
        tn = _pick(n, tuple(c for c in (1024, 512, 256, 128) if tm * c * 4 <= 4 * 1024 * 1024))
        tk = _pick(k, (512, 256, 128))
    else:
        tm = _pick(m, (512, 256, 128))
        tn = _pick(n, (2816, 1024, 512, 256, 128))
        tk = k if k <= 2816 else _pick(k, (2816, 2048, 1024, 512))
    nk = k // tk
    dims = {"nn": NN, "nt": NT, "tn": TN}[mode]

    def body(*refs):
        if add is None:
            a_ref, b_ref, o_ref = refs[:3]
            add_ref = None
            rest = refs[3:]
        else:
            a_ref, b_ref, add_ref, o_ref = refs[:4]
            rest = refs[4:]
        part = _dot(a_ref[...].astype(BF16), b_ref[...].astype(BF16), dims)

        def finish(acc):
            if add_ref is not None:
                acc = acc + add_ref[...]
            o_ref[...] = acc.astype(o_ref.dtype)

        if nk == 1:
            finish(part)
        else:
            acc_ref = rest[0]
            kk = pl.program_id(2)

            @pl.when(kk == 0)
            def _():
                acc_ref[...] = part

            @pl.when(kk > 0)
            def _():
                acc_ref[...] += part

            @pl.when(kk == nk - 1)
            def _():
                finish(acc_ref[...])

    if mode == "nn":
        a_spec = pl.BlockSpec((tm, tk), lambda i, j, kk: (i, kk))
        b_spec = pl.BlockSpec((tk, tn), lambda i, j, kk: (kk, j))
    elif mode == "nt":
        a_spec = pl.BlockSpec((tm, tk), lambda i, j, kk: (i, kk))
        b_spec = pl.BlockSpec((tn, tk), lambda i, j, kk: (j, kk))
    else:
        a_spec = pl.BlockSpec((tk, tm), lambda i, j, kk: (kk, i))
        b_spec = pl.BlockSpec((tk, tn), lambda i, j, kk: (kk, j))
    o_spec = pl.BlockSpec((tm, tn), lambda i, j, kk: (i, j))
    in_specs = [a_spec, b_spec] + ([o_spec] if add is not None else [])
    args = (a, b) + ((add,) if add is not None else ())
    return pl.pallas_call(
        body,
        grid=(m // tm, n // tn, nk),
        in_specs=in_specs,
        out_specs=o_spec,
        out_shape=jax.ShapeDtypeStruct((m, n), out_dtype),
        scratch_shapes=[pltpu.VMEM((tm, tn), F32)] if nk > 1 else [],
        compiler_params=_params(("parallel", "parallel", "arbitrary")),
        name=f"mm_{mode}_{m}x{k}x{n}",
    )(*args)


def _rms_fwd(x, w):
    s, d = x.shape
    tr = _pick(s, (512, 256, 128))

    def body(x_ref, w_ref, o_ref):
        xv = x_ref[...]
        r = lax.rsqrt(jnp.mean(xv * xv, axis=-1, keepdims=True) + RMS_EPS)
        o_ref[...] = (xv * r * w_ref[...]).astype(o_ref.dtype)

    return pl.pallas_call(
        body,
        grid=(s // tr,),
        in_specs=[pl.BlockSpec((tr, d), lambda i: (i, 0)), pl.BlockSpec((1, d), lambda i: (0, 0))],
        out_specs=pl.BlockSpec((tr, d), lambda i: (i, 0)),
        out_shape=jax.ShapeDtypeStruct((s, d), BF16),
        compiler_params=_params(("parallel",)),
        name="rms_fwd",
    )(x, w)


def _rms_bwd(dyn, x, w, add):
    s, d = x.shape
    tr = _pick(s, (512, 256, 128))

    def body(dy_ref, x_ref, w_ref, add_ref, dx_ref, dw_ref):
        xv = x_ref[...]
        dy = dy_ref[...].astype(F32)
        r = lax.rsqrt(jnp.mean(xv * xv, axis=-1, keepdims=True) + RMS_EPS)
        xn = xv * r
        dxh = dy * w_ref[...]
        dx = r * (dxh - xn * jnp.mean(dxh * xn, axis=-1, keepdims=True))
        dx_ref[...] = add_ref[...] + dx
        part = jnp.sum(dy * xn, axis=0, keepdims=True)

        @pl.when(pl.program_id(0) == 0)
        def _():
            dw_ref[...] = part

        @pl.when(pl.program_id(0) > 0)
        def _():
            dw_ref[...] += part

    row = pl.BlockSpec((tr, d), lambda i: (i, 0))
    vec = pl.BlockSpec((1, d), lambda i: (0, 0))
    return pl.pallas_call(
        body,
        grid=(s // tr,),
        in_specs=[row, row, vec, row],
        out_specs=[row, vec],
        out_shape=[jax.ShapeDtypeStruct((s, d), F32), jax.ShapeDtypeStruct((1, d), F32)],
        compiler_params=_params(("arbitrary",)),
        name="rms_bwd",
    )(dyn, x, w, add)


CONV_ROWS = 256
CONV_COLS = 256
CONV_HALO = 16


def _conv_taps(ext, w, base, rows):
    acc = w[0:1, :] * ext[base:base + rows]
    for k in range(1, CONV_K):
        acc = acc + w[k:k + 1, :] * ext[base + k:base + k + rows]
    return acc


def _ssd_conv_fwd(zx, conv_w, conv_b):
    s = zx.shape[0]
    c = SSD_CONV_DIM
    nsteps = s // CONV_ROWS
    off = SSD_INNER // CONV_COLS

    def body(x_ref, w_ref, b_ref, o_ref):
        w = w_ref[...]
        b = b_ref[...]

        def step(i, carry):
            r0 = pl.multiple_of(i * CONV_ROWS, CONV_ROWS)
            cur = x_ref[pl.ds(r0, CONV_ROWS), :].astype(F32)
            p0 = pl.multiple_of(jnp.maximum(r0 - CONV_HALO, 0), CONV_HALO)
            prev = x_ref[pl.ds(p0, CONV_HALO), :].astype(F32)
            prev = jnp.where(i == 0, 0.0, prev)
            ext = jnp.concatenate([prev, cur], axis=0)
            acc = _conv_taps(ext, w, CONV_HALO - (CONV_K - 1), CONV_ROWS) + b
            o_ref[pl.ds(r0, CONV_ROWS), :] = (acc * _sigmoid(acc)).astype(o_ref.dtype)
            return carry

        lax.fori_loop(0, nsteps, step, 0)

    return pl.pallas_call(
        body,
        grid=(c // CONV_COLS,),
        in_specs=[pl.BlockSpec((s, CONV_COLS), lambda j: (0, j + off)),
                  pl.BlockSpec((8, CONV_COLS), lambda j: (0, j)),
                  pl.BlockSpec((1, CONV_COLS), lambda j: (0, j))],
        out_specs=pl.BlockSpec((s, CONV_COLS), lambda j: (0, j)),
        out_shape=jax.ShapeDtypeStruct((s, c), BF16),
        compiler_params=_params(("parallel",)),
        name="ssd_conv_fwd",
    )(zx, conv_w, conv_b)


def _ssd_conv_bwd(zx, dxbc, conv_w, conv_b, dzx):
    s = zx.shape[0]
    c = SSD_CONV_DIM
    nsteps = s // CONV_ROWS
    off = SSD_INNER // CONV_COLS

    def body(x_ref, dy_ref, w_ref, b_ref, dzx_in_ref, dx_ref, dw_ref, db_ref, dc_ref):
        w = w_ref[...]
        b = b_ref[...]
        dc_ref[pl.ds(s, CONV_HALO), :] = jnp.zeros((CONV_HALO, CONV_COLS), F32)

        def step1(i, carry):
            dw0, dw1, dw2, dw3, dbs = carry
            r0 = pl.multiple_of(i * CONV_ROWS, CONV_ROWS)
            cur = x_ref[pl.ds(r0, CONV_ROWS), :].astype(F32)
            p0 = pl.multiple_of(jnp.maximum(r0 - CONV_HALO, 0), CONV_HALO)
            prev = x_ref[pl.ds(p0, CONV_HALO), :].astype(F32)
            prev = jnp.where(i == 0, 0.0, prev)
            ext = jnp.concatenate([prev, cur], axis=0)
            base = CONV_HALO - (CONV_K - 1)
            acc = _conv_taps(ext, w, base, CONV_ROWS) + b
            sg = _sigmoid(acc)
            dcv = dy_ref[pl.ds(r0, CONV_ROWS), :].astype(F32) * (sg * (1.0 + acc * (1.0 - sg)))
            dc_ref[pl.ds(r0, CONV_ROWS), :] = dcv
            dws = [jnp.sum(dcv * ext[base + k:base + k + CONV_ROWS], axis=0, keepdims=True) for k in range(CONV_K)]
            return (dw0 + dws[0], dw1 + dws[1], dw2 + dws[2], dw3 + dws[3], dbs + jnp.sum(dcv, axis=0, keepdims=True))

        z = jnp.zeros((1, CONV_COLS), F32)
        dw0, dw1, dw2, dw3, dbs = lax.fori_loop(0, nsteps, step1, (z, z, z, z, z))
        dw_ref[...] = jnp.concatenate([dw0, dw1, dw2, dw3, z, z, z, z], axis=0)
        db_ref[...] = dbs

        def step2(i, carry):
            r0 = pl.multiple_of(i * CONV_ROWS, CONV_ROWS)
            ext = dc_ref[pl.ds(r0, CONV_ROWS + CONV_HALO), :]
            acc = w[0:1, :] * ext[CONV_K - 1:CONV_K - 1 + CONV_ROWS]
            for k in range(1, CONV_K):
                acc = acc + w[k:k + 1, :] * ext[CONV_K - 1 - k:CONV_K - 1 - k + CONV_ROWS]
            dx_ref[pl.ds(r0, CONV_ROWS), :] = acc.astype(dx_ref.dtype)
            return carry

        lax.fori_loop(0, nsteps, step2, 0)

    col = pl.BlockSpec((s, CONV_COLS), lambda j: (0, j))
    shifted = pl.BlockSpec((s, CONV_COLS), lambda j: (0, j + off))
    return pl.pallas_call(
        body,
        grid=(c // CONV_COLS,),
        in_specs=[shifted, col,
                  pl.BlockSpec((8, CONV_COLS), lambda j: (0, j)),
                  pl.BlockSpec((1, CONV_COLS), lambda j: (0, j)),
                  pl.BlockSpec(memory_space=pl.ANY)],
        out_specs=[shifted, pl.BlockSpec((8, CONV_COLS), lambda j: (0, j)), pl.BlockSpec((1, CONV_COLS), lambda j: (0, j))],
        out_shape=[jax.ShapeDtypeStruct((s, SSD_ZX), BF16), jax.ShapeDtypeStruct((8, c), F32),
                   jax.ShapeDtypeStruct((1, c), F32)],
        scratch_shapes=[pltpu.VMEM((s + CONV_HALO, CONV_COLS), F32)],
        input_output_aliases={4: 0},
        compiler_params=_params(("parallel",)),
        name="ssd_conv_bwd",
    )(zx, dxbc, conv_w, conv_b, dzx)


def _ssd_consts():
    li = lax.broadcasted_iota(jnp.int32, (CHUNK, CHUNK), 0)
    si = lax.broadcasted_iota(jnp.int32, (CHUNK, CHUNK), 1)
    tril = li >= si
    hrow = lax.broadcasted_iota(jnp.int32, (LANES, SSD_INNER), 0)
    hcol = lax.broadcasted_iota(jnp.int32, (LANES, SSD_INNER), 1) // SSD_HEADDIM
    expand = (hrow == hcol).astype(F32)
    return tril, expand


def _ssd_chunk_common(dtp_ref, bias_ref, alog_ref, tril, expand):
    lane = lax.broadcasted_iota(jnp.int32, (1, LANES), 1)
    valid = lane < SSD_HEADS
    pre = dtp_ref[...] + bias_ref[...]
    dt = jnp.where(valid, jnp.maximum(pre, 0.0) + jnp.log1p(jnp.exp(-jnp.abs(pre))), 0.0)
    a = jnp.where(valid, -jnp.exp(alog_ref[...]), 0.0)
    da = dt * a
    cs = _dot(tril.astype(F32), da, NN, HI)
    cs_x = _dot(cs, expand, NN, HI)
    dt_x = _dot(dt, expand, NN, HI)
    return pre, dt, a, cs, cs_x, dt_x


def _ssd_scan_fwd(xbc, dtp, dt_bias, a_log, d_skip):
    s = xbc.shape[0]
    nc = s // CHUNK
    gw = SSD_GROUP_W

    def body(xbc_ref, dtp_ref, bias_ref, alog_ref, d_ref, y_ref, prev_ref, state_ref):
        c = pl.program_id(0)

        @pl.when(c == 0)
        def _():
            state_ref[...] = jnp.zeros_like(state_ref)

        tril, expand = _ssd_consts()
        pre, dt, a, cs, cs_x, dt_x = _ssd_chunk_common(dtp_ref, bias_ref, alog_ref, tril, expand)
        cs_t = cs.T
        d_x = _dot(jnp.broadcast_to(d_ref[...], (8, LANES)), expand, NN, HI)[0:1, :]
        cs_last = cs_x[CHUNK - 1:CHUNK, :]
        dec_out = jnp.exp(cs_x)
        dec_st = jnp.exp(cs_last - cs_x)
        dec_ch = jnp.exp(cs_last)
        x = xbc_ref[:, 0:SSD_INNER].astype(F32)
        xr = x * dt_x
        xrs = xr * dec_st
        lane_g = lax.broadcasted_iota(jnp.int32, (1, gw), 1) // SSD_HEADDIM
        for g in range(SSD_GROUPS):
            sl = slice(g * gw, (g + 1) * gw)
            bg = xbc_ref[:, SSD_INNER + g * SSD_STATE:SSD_INNER + (g + 1) * SSD_STATE]
            cg = xbc_ref[:, SSD_INNER + (SSD_GROUPS + g) * SSD_STATE:SSD_INNER + (SSD_GROUPS + g + 1) * SSD_STATE]
            cb = _dot(cg, bg, NT)
            prev_g = state_ref[:, sl]
            prev_ref[0, :, sl] = prev_g
            yo = _dot(cg, prev_g.astype(BF16), NN) * dec_out[:, sl]
            xr_g = xr[:, sl]
            yd = jnp.zeros((CHUNK, gw), F32)
            for r in range(SSD_HEADS // SSD_GROUPS):
                h = g * (SSD_HEADS // SSD_GROUPS) + r
                diff = cs[:, h:h + 1] - cs_t[h:h + 1, :]
                lmat = jnp.exp(jnp.where(tril, diff, -1e30))
                wmat = (cb * lmat).astype(BF16)
                xr_h = jnp.where(lane_g == r, xr_g, 0.0).astype(BF16)
                yd = yd + _dot(wmat, xr_h, NN)
            y_ref[:, sl] = yd + yo + x[:, sl] * d_x[:, sl]
            sc = _dot(bg, xrs[:, sl].astype(BF16), TN)
            state_ref[:, sl] = prev_g * dec_ch[:, sl] + sc

    vec = pl.BlockSpec((1, LANES), lambda c: (0, 0))
    return pl.pallas_call(
        body,
        grid=(nc,),
        in_specs=[pl.BlockSpec((CHUNK, SSD_CONV_DIM), lambda c: (c, 0)),
                  pl.BlockSpec((CHUNK, LANES), lambda c: (c, 0)), vec, vec, vec],
        out_specs=[pl.BlockSpec((CHUNK, SSD_INNER), lambda c: (c, 0)),
                   pl.BlockSpec((1, SSD_STATE, SSD_INNER), lambda c: (c, 0, 0))],
        out_shape=[jax.ShapeDtypeStruct((s, SSD_INNER), F32), jax.ShapeDtypeStruct((nc, SSD_STATE, SSD_INNER), F32)],
        scratch_shapes=[pltpu.VMEM((SSD_STATE, SSD_INNER), F32)],
        compiler_params=_params(("arbitrary",)),
        name="ssd_scan_fwd",
    )(xbc, dtp, dt_bias, a_log, d_skip)


def _ssd_scan_bwd(xbc, dtp, prev, dy, dt_bias, a_log, d_skip):
    s = xbc.shape[0]
    nc = s // CHUNK
    gw = SSD_GROUP_W
    hpg = SSD_HEADS // SSD_GROUPS

    def body(xbc_ref, dtp_ref, prev_ref, dy_ref, bias_ref, alog_ref, d_ref,
             dxbc_ref, ddtp_ref, dbias_ref, dalog_ref, dd_ref, dp_ref, ddx_ref):
        step = pl.program_id(0)

        @pl.when(step == 0)
        def _():
            dp_ref[...] = jnp.zeros_like(dp_ref)
            ddx_ref[...] = jnp.zeros_like(ddx_ref)
            dbias_ref[...] = jnp.zeros_like(dbias_ref)
            dalog_ref[...] = jnp.zeros_like(dalog_ref)

        tril, expand = _ssd_consts()
        pre, dt, a, cs, cs_x, dt_x = _ssd_chunk_common(dtp_ref, bias_ref, alog_ref, tril, expand)
        cs_t = cs.T
        d_x = _dot(jnp.broadcast_to(d_ref[...], (8, LANES)), expand, NN, HI)[0:1, :]
        cs_last = cs_x[CHUNK - 1:CHUNK, :]
        dec_out = jnp.exp(cs_x)
        dec_st = jnp.exp(cs_last - cs_x)
        dec_ch = jnp.exp(cs_last)
        x = xbc_ref[:, 0:SSD_INNER].astype(F32)
        dyv = dy_ref[...]
        xr = x * dt_x
        xrs = xr * dec_st
        lane_g = lax.broadcasted_iota(jnp.int32, (1, gw), 1) // SSD_HEADDIM
        hsel = lax.broadcasted_iota(jnp.int32, (CHUNK, LANES), 1)
        dcs = jnp.zeros((CHUNK, LANES), F32)
        last_parts = []
        t_parts = []
        dxr_parts = []
        for g in range(SSD_GROUPS):
            sl = slice(g * gw, (g + 1) * gw)
            bsl = slice(SSD_INNER + g * SSD_STATE, SSD_INNER + (g + 1) * SSD_STATE)
            csl = slice(SSD_INNER + (SSD_GROUPS + g) * SSD_STATE, SSD_INNER + (SSD_GROUPS + g + 1) * SSD_STATE)
            bg = xbc_ref[:, bsl]
            cg = xbc_ref[:, csl]
            cb = _dot(cg, bg, NT)
            prev_g = prev_ref[0, :, sl]
            prev_b = prev_g.astype(BF16)
            dp_g = dp_ref[:, sl]
            dp_b = dp_g.astype(BF16)
            dy_g = dyv[:, sl]
            xr_g = xr[:, sl]
            gmat = _dot(cg, prev_b, NN)
            dgm = (dy_g * dec_out[:, sl]).astype(BF16)
            dc_g = _dot(dgm, prev_b, NT)
            dprev = _dot(cg, dgm, TN)
            t1 = dy_g * gmat * dec_out[:, sl]
            mm_ = _dot(bg, dp_b, NN)
            db_g = _dot(xrs[:, sl].astype(BF16), dp_b, NT)
            dxr_g = mm_ * dec_st[:, sl]
            t2 = dxr_g * xr_g
            last = jnp.sum(t2, axis=0, keepdims=True) + jnp.sum(dp_g * prev_g, axis=0, keepdims=True) * dec_ch[:, sl]
            dp_ref[:, sl] = dp_g * dec_ch[:, sl] + dprev
            dcb = jnp.zeros((CHUNK, CHUNK), F32)
            for r in range(hpg):
                h = g * hpg + r
                diff = cs[:, h:h + 1] - cs_t[h:h + 1, :]
                lmat = jnp.exp(jnp.where(tril, diff, -1e30))
                wmat = cb * lmat
                dy_h = jnp.where(lane_g == r, dy_g, 0.0).astype(BF16)
                dw = _dot(dy_h, xr_g.astype(BF16), NT)
                dxr_g = dxr_g + _dot(wmat.astype(BF16), dy_h, TN)
                dcb = dcb + dw * lmat
                q = (dw * wmat).astype(BF16)
                onehot = (hsel == h).astype(BF16)
                dcs = dcs + _dot(q, onehot, NN) - _dot(q, onehot, TN)
            dcb_b = dcb.astype(BF16)
            dc_g = dc_g + _dot(dcb_b, bg, NN)
            db_g = db_g + _dot(dcb_b, cg, TN)
            dxbc_ref[:, bsl] = db_g.astype(dxbc_ref.dtype)
            dxbc_ref[:, csl] = dc_g.astype(dxbc_ref.dtype)
            t_parts.append(t1 - t2)
            last_parts.append(last)
            dxr_parts.append(dxr_g)
        dxr = jnp.concatenate(dxr_parts, axis=1)
        tt = jnp.concatenate(t_parts, axis=1)
        last_x = jnp.concatenate(last_parts, axis=1)
        dxbc_ref[:, 0:SSD_INNER] = (dxr * dt_x + dyv * d_x).astype(dxbc_ref.dtype)
        dcs = dcs + _dot(tt, expand, NT, HI)
        last_h = _dot(jnp.broadcast_to(last_x, (8, SSD_INNER)), expand, NT, HI)[0:1, :]
        rowi = lax.broadcasted_iota(jnp.int32, (CHUNK, LANES), 0)
        dcs = dcs + jnp.where(rowi == CHUNK - 1, last_h, 0.0)
        dda = _dot(tril.astype(F32), dcs, TN, HI)
        ddt = dda * a + _dot(dxr * x, expand, NT, HI)
        dpre = ddt * _sigmoid(pre)
        ddtp_ref[...] = dpre
        dbias_ref[...] += jnp.sum(dpre, axis=0, keepdims=True)
        dalog_ref[...] += jnp.sum(dda * dt, axis=0, keepdims=True) * a
        ddx_ref[...] += jnp.broadcast_to(jnp.sum(dyv * x, axis=0, keepdims=True), (8, SSD_INNER))

        @pl.when(step == nc - 1)
        def _():
            dd_ref[...] = _dot(ddx_ref[...], expand, NT, HI)[0:1, :]

    rev = lambda c: (nc - 1 - c, 0)
    vec = pl.BlockSpec((1, LANES), lambda c: (0, 0))
    return pl.pallas_call(
        body,
        grid=(nc,),
        in_specs=[pl.BlockSpec((CHUNK, SSD_CONV_DIM), rev), pl.BlockSpec((CHUNK, LANES), rev),
                  pl.BlockSpec((1, SSD_STATE, SSD_INNER), lambda c: (nc - 1 - c, 0, 0)),
                  pl.BlockSpec((CHUNK, SSD_INNER), rev), vec, vec, vec],
        out_specs=[pl.BlockSpec((CHUNK, SSD_CONV_DIM), rev), pl.BlockSpec((CHUNK, LANES), rev), vec, vec, vec],
        out_shape=[jax.ShapeDtypeStruct((s, SSD_CONV_DIM), BF16), jax.ShapeDtypeStruct((s, LANES), F32),
                   jax.ShapeDtypeStruct((1, LANES), F32), jax.ShapeDtypeStruct((1, LANES), F32),
                   jax.ShapeDtypeStruct((1, LANES), F32)],
        scratch_shapes=[pltpu.VMEM((SSD_STATE, SSD_INNER), F32), pltpu.VMEM((8, SSD_INNER), F32)],
        compiler_params=_params(("arbitrary",)),
        name="ssd_scan_bwd",
    )(xbc, dtp, prev, dy, dt_bias, a_log, d_skip)


def _ssd_gate_fwd(y, zx, norm_w):
    s = y.shape[0]
    tr = _pick(s, (256, 128))
    gw = SSD_GROUP_W

    def body(y_ref, z_ref, w_ref, o_ref):
        for g in range(SSD_GROUPS):
            sl = slice(g * gw, (g + 1) * gw)
            z = z_ref[:, sl].astype(F32)
            gv = y_ref[:, sl] * (z * _sigmoid(z))
            r = lax.rsqrt(jnp.mean(gv * gv, axis=-1, keepdims=True) + LN_EPS)
            o_ref[:, sl] = (gv * r * w_ref[:, sl]).astype(o_ref.dtype)

    row = pl.BlockSpec((tr, SSD_INNER), lambda i: (i, 0))
    return pl.pallas_call(
        body,
        grid=(s // tr,),
        in_specs=[row, row, pl.BlockSpec((1, SSD_INNER), lambda i: (0, 0))],
        out_specs=row,
        out_shape=jax.ShapeDtypeStruct((s, SSD_INNER), BF16),
        compiler_params=_params(("parallel",)),
        name="ssd_gate_fwd",
    )(y, zx, norm_w)


def _ssd_gate_bwd(dgn, y, zx, norm_w):
    s = y.shape[0]
    tr = _pick(s, (256, 128))
    gw = SSD_GROUP_W

    def body(dg_ref, y_ref, z_ref, w_ref, dy_ref, dz_ref, dw_ref):
        parts = []
        for g in range(SSD_GROUPS):
            sl = slice(g * gw, (g + 1) * gw)
            z = z_ref[:, sl].astype(F32)
            yv = y_ref[:, sl]
            sg = _sigmoid(z)
            sz = z * sg
            gv = yv * sz
            r = lax.rsqrt(jnp.mean(gv * gv, axis=-1, keepdims=True) + LN_EPS)
            gn = gv * r
            dout = dg_ref[:, sl].astype(F32)
            parts.append(jnp.sum(dout * gn, axis=0, keepdims=True))
            dgn_ = dout * w_ref[:, sl]
            dgv = r * (dgn_ - gn * jnp.mean(dgn_ * gn, axis=-1, keepdims=True))
            dy_ref[:, sl] = dgv * sz
            dz_ref[:, sl] = (dgv * yv * (sg * (1.0 + z * (1.0 - sg)))).astype(dz_ref.dtype)
        part = jnp.concatenate(parts, axis=1)

        @pl.when(pl.program_id(0) == 0)
        def _():
            dw_ref[...] = part

        @pl.when(pl.program_id(0) > 0)
        def _():
            dw_ref[...] += part

    row = pl.BlockSpec((tr, SSD_INNER), lambda i: (i, 0))
    vec = pl.BlockSpec((1, SSD_INNER), lambda i: (0, 0))
    return pl.pallas_call(
        body,
        grid=(s // tr,),
        in_specs=[row, row, row, vec],
        out_specs=[row, row, vec],
        out_shape=[jax.ShapeDtypeStruct((s, SSD_INNER), F32), jax.ShapeDtypeStruct((s, SSD_ZX), BF16),
                   jax.ShapeDtypeStruct((1, SSD_INNER), F32)],
        compiler_params=_params(("arbitrary",)),
        name="ssd_gate_bwd",
    )(dgn, y, zx, norm_w)


INV_SQRT2 = 1.0 / math.sqrt(2.0)
INV_SQRT2PI = 1.0 / math.sqrt(2.0 * math.pi)


def _gelu(x):
    return 0.5 * x * (1.0 + lax.erf(x * INV_SQRT2))


def _gelu_grad(x):
    return 0.5 * (1.0 + lax.erf(x * INV_SQRT2)) + x * INV_SQRT2PI * jnp.exp(-0.5 * x * x)


def _gmlp_act_fwd(pre, b_in, ln_w, ln_b):
    s = pre.shape[0]
    tr = _pick(s, (256, 128))
    n = GMLP_INNER

    def body(p_ref, b_ref, w_ref, lb_ref, u_ref, v_ref):
        u_ref[...] = _gelu(p_ref[:, 0:n].astype(F32) + b_ref[:, 0:n]).astype(u_ref.dtype)
        hv = _gelu(p_ref[:, n:2 * n].astype(F32) + b_ref[:, n:2 * n])
        mu = jnp.mean(hv, axis=-1, keepdims=True)
        xc = hv - mu
        r = lax.rsqrt(jnp.mean(xc * xc, axis=-1, keepdims=True) + LN_EPS)
        v_ref[...] = (xc * r * w_ref[...] + lb_ref[...]).astype(v_ref.dtype)

    half = pl.BlockSpec((tr, n), lambda i: (i, 0))
    vec = pl.BlockSpec((1, n), lambda i: (0, 0))
    return pl.pallas_call(
        body,
        grid=(s // tr,),
        in_specs=[pl.BlockSpec((tr, 2 * n), lambda i: (i, 0)), pl.BlockSpec((1, 2 * n), lambda i: (0, 0)), vec, vec],
        out_specs=[half, half],
        out_shape=[jax.ShapeDtypeStruct((s, n), BF16), jax.ShapeDtypeStruct((s, n), BF16)],
        compiler_params=_params(("parallel",)),
        name="gmlp_act_fwd",
    )(pre, b_in, ln_w, ln_b)


def _gmlp_act_bwd(pre, b_in, ln_w, du, dv):
    s = pre.shape[0]
    tr = _pick(s, (256, 128))
    n = GMLP_INNER

    def body(p_ref, b_ref, w_ref, du_ref, dv_ref, dp_ref, db_ref, dw_ref, dlb_ref):
        xu = p_ref[:, 0:n].astype(F32) + b_ref[:, 0:n]
        dpu = du_ref[...].astype(F32) * _gelu_grad(xu)
        xv = p_ref[:, n:2 * n].astype(F32) + b_ref[:, n:2 * n]
        hv = _gelu(xv)
        mu = jnp.mean(hv, axis=-1, keepdims=True)
        xc = hv - mu
        r = lax.rsqrt(jnp.mean(xc * xc, axis=-1, keepdims=True) + LN_EPS)
        vh = xc * r
        dvv = dv_ref[...].astype(F32)
        dvh = dvv * w_ref[...]
        dh = r * (dvh - jnp.mean(dvh, axis=-1, keepdims=True) - vh * jnp.mean(dvh * vh, axis=-1, keepdims=True))
        dpv = dh * _gelu_grad(xv)
        dp_ref[:, 0:n] = dpu.astype(dp_ref.dtype)
        dp_ref[:, n:2 * n] = dpv.astype(dp_ref.dtype)
        pb = jnp.concatenate([jnp.sum(dpu, axis=0, keepdims=True), jnp.sum(dpv, axis=0, keepdims=True)], axis=1)
        pw = jnp.sum(dvv * vh, axis=0, keepdims=True)
        plb = jnp.sum(dvv, axis=0, keepdims=True)

        @pl.when(pl.program_id(0) == 0)
        def _():
            db_ref[...] = pb
            dw_ref[...] = pw
            dlb_ref[...] = plb

        @pl.when(pl.program_id(0) > 0)
        def _():
            db_ref[...] += pb
            dw_ref[...] += pw
            dlb_ref[...] += plb

    half = pl.BlockSpec((tr, n), lambda i: (i, 0))
    full = pl.BlockSpec((tr, 2 * n), lambda i: (i, 0))
    vec = pl.BlockSpec((1, n), lambda i: (0, 0))
    vec2 = pl.BlockSpec((1, 2 * n), lambda i: (0, 0))
    return pl.pallas_call(
        body,
        grid=(s // tr,),
        in_specs=[full, vec2, vec, half, half],
        out_specs=[full, vec2, vec, vec],
        out_shape=[jax.ShapeDtypeStruct((s, 2 * n), BF16), jax.ShapeDtypeStruct((1, 2 * n), F32),
                   jax.ShapeDtypeStruct((1, n), F32), jax.ShapeDtypeStruct((1, n), F32)],
        compiler_params=_params(("arbitrary",)),
        name="gmlp_act_bwd",
    )(pre, b_in, ln_w, du, dv)


def _gmlp_mix_fwd(u, v, w_s, b_st):
    s = u.shape[0]
    gd = GMLP_INNER // GMLP_GROUPS

    def body(u_ref, v_ref, w_ref, b_ref, o_ref):
        li = lax.broadcasted_iota(jnp.int32, (CHUNK, CHUNK), 0)
        si = lax.broadcasted_iota(jnp.int32, (CHUNK, CHUNK), 1)
        tril = li >= si
        for g in range(GMLP_GROUPS):
            sl = slice(g * gd, (g + 1) * gd)
            wm = jnp.where(tril, w_ref[g], 0.0).astype(BF16)
            mixed = _dot(wm, v_ref[:, sl], NN) + b_ref[:, g:g + 1]
            o_ref[:, sl] = (u_ref[:, sl].astype(F32) * mixed).astype(o_ref.dtype)

    row = pl.BlockSpec((CHUNK, GMLP_INNER), lambda c: (c, 0))
    return pl.pallas_call(
        body,
        grid=(s // CHUNK,),
        in_specs=[row, row, pl.BlockSpec((GMLP_GROUPS, CHUNK, CHUNK), lambda c: (0, 0, 0)),
                  pl.BlockSpec((CHUNK, LANES), lambda c: (0, 0))],
        out_specs=row,
        out_shape=jax.ShapeDtypeStruct((s, GMLP_INNER), BF16),
        compiler_params=_params(("parallel",)),
        name="gmlp_mix_fwd",
    )(u, v, w_s, b_st)


def _gmlp_mix_bwd(dgated, u, v, w_s, b_st):
    s = u.shape[0]
    nc = s // CHUNK
    gd = GMLP_INNER // GMLP_GROUPS

    def body(dg_ref, u_ref, v_ref, w_ref, b_ref, du_ref, dv_ref, dw_ref, db_ref):
        c = pl.program_id(0)

        @pl.when(c == 0)
        def _():
            dw_ref[...] = jnp.zeros_like(dw_ref)
            db_ref[...] = jnp.zeros_like(db_ref)

        li = lax.broadcasted_iota(jnp.int32, (CHUNK, CHUNK), 0)
        si = lax.broadcasted_iota(jnp.int32, (CHUNK, CHUNK), 1)
        tril = li >= si
        lane = lax.broadcasted_iota(jnp.int32, (CHUNK, LANES), 1)
        dbacc = jnp.zeros((CHUNK, LANES), F32)
        for g in range(GMLP_GROUPS):
            sl = slice(g * gd, (g + 1) * gd)
            wm = jnp.where(tril, w_ref[g], 0.0).astype(BF16)
            vg = v_ref[:, sl]
            mixed = _dot(wm, vg, NN) + b_ref[:, g:g + 1]
            dgv = dg_ref[:, sl].astype(F32)
            du_ref[:, sl] = (dgv * mixed).astype(du_ref.dtype)
            dm = dgv * u_ref[:, sl].astype(F32)
            dm_b = dm.astype(BF16)
            dv_ref[:, sl] = _dot(wm, dm_b, TN).astype(dv_ref.dtype)
            dw_ref[g] += jnp.where(tril, _dot(dm_b, vg, NT), 0.0)
            dbacc = dbacc + jnp.where(lane == g, jnp.sum(dm, axis=1, keepdims=True), 0.0)
        db_ref[...] += dbacc

    row = pl.BlockSpec((CHUNK, GMLP_INNER), lambda c: (c, 0))
    wspec = pl.BlockSpec((GMLP_GROUPS, CHUNK, CHUNK), lambda c: (0, 0, 0))
    bspec = pl.BlockSpec((CHUNK, LANES), lambda c: (0, 0))
    return pl.pallas_call(
        body,
        grid=(nc,),
        in_specs=[row, row, row, wspec, bspec],
        out_specs=[row, row, wspec, bspec],
        out_shape=[jax.ShapeDtypeStruct((s, GMLP_INNER), BF16), jax.ShapeDtypeStruct((s, GMLP_INNER), BF16),
                   jax.ShapeDtypeStruct((GMLP_GROUPS, CHUNK, CHUNK), F32), jax.ShapeDtypeStruct((CHUNK, LANES), F32)],
        compiler_params=_params(("arbitrary",)),
        name="gmlp_mix_bwd",
    )(dgated, u, v, w_s, b_st)


def _swiglu_fwd(gu):
    s = gu.shape[0]
    f = FFN_DIM
    tr = _pick(s, (512, 256, 128))

    def body(gu_ref, o_ref):
        gt = gu_ref[:, 0:f].astype(F32)
        o_ref[...] = (gt * _sigmoid(gt) * gu_ref[:, f:2 * f].astype(F32)).astype(o_ref.dtype)

    return pl.pallas_call(
        body,
        grid=(s // tr,),
        in_specs=[pl.BlockSpec((tr, 2 * f), lambda i: (i, 0))],
        out_specs=pl.BlockSpec((tr, f), lambda i: (i, 0)),
        out_shape=jax.ShapeDtypeStruct((s, f), BF16),
        compiler_params=_params(("parallel",)),
        name="swiglu_fwd",
    )(gu)


def _swiglu_bwd(gu, dhid):
    s = gu.shape[0]
    f = FFN_DIM
    tr = _pick(s, (512, 256, 128))

    def body(gu_ref, dh_ref, dgu_ref):
        gt = gu_ref[:, 0:f].astype(F32)
        up = gu_ref[:, f:2 * f].astype(F32)
        dh = dh_ref[...].astype(F32)
        sg = _sigmoid(gt)
        dgu_ref[:, 0:f] = (dh * up * (sg * (1.0 + gt * (1.0 - sg)))).astype(dgu_ref.dtype)
        dgu_ref[:, f:2 * f] = (dh * gt * sg).astype(dgu_ref.dtype)

    wide = pl.BlockSpec((tr, 2 * f), lambda i: (i, 0))
    return pl.pallas_call(
        body,
        grid=(s // tr,),
        in_specs=[wide, pl.BlockSpec((tr, f), lambda i: (i, 0))],
        out_specs=wide,
        out_shape=jax.ShapeDtypeStruct((s, 2 * f), BF16),
        compiler_params=_params(("parallel",)),
        name="swiglu_bwd",
    )(gu, dhid)


def _ple_fwd(pe, gl, h, ple_norm):
    s, d = h.shape
    tr = _pick(s, (512, 256, 128))

    def body(pe_ref, gl_ref, h_ref, w_ref, o_ref):
        pe_ = pe_ref[...]
        r = lax.rsqrt(jnp.mean(pe_ * pe_, axis=-1, keepdims=True) + RMS_EPS)
        o_ref[...] = h_ref[...] + _sigmoid(gl_ref[...]) * (pe_ * r * w_ref[...])

    row = pl.BlockSpec((tr, d), lambda i: (i, 0))
    return pl.pallas_call(
        body,
        grid=(s // tr,),
        in_specs=[row, row, row, pl.BlockSpec((1, d), lambda i: (0, 0))],
        out_specs=row,
        out_shape=jax.ShapeDtypeStruct((s, d), F32),
        compiler_params=_params(("parallel",)),
        name="ple_fwd",
    )(pe, gl, h, ple_norm)


def _ple_bwd(dh, pe, gl, ple_norm):
    s, d = dh.shape
    tr = _pick(s, (512, 256, 128))

    def body(dh_ref, pe_ref, gl_ref, w_ref, dgl_ref, dpe_ref, dw_ref):
        pe_ = pe_ref[...]
        dhv = dh_ref[...]
        r = lax.rsqrt(jnp.mean(pe_ * pe_, axis=-1, keepdims=True) + RMS_EPS)
        pn = pe_ * r
        gate = _sigmoid(gl_ref[...])
        dgl_ref[...] = (dhv * (pn * w_ref[...]) * gate * (1.0 - gate)).astype(dgl_ref.dtype)
        de = dhv * gate
        dxh = de * w_ref[...]
        dpe_ref[...] = (r * (dxh - pn * jnp.mean(dxh * pn, axis=-1, keepdims=True))).astype(dpe_ref.dtype)
        part = jnp.sum(de * pn, axis=0, keepdims=True)

        @pl.when(pl.program_id(0) == 0)
        def _():
            dw_ref[...] = part

        @pl.when(pl.program_id(0) > 0)
        def _():
            dw_ref[...] += part

    row = pl.BlockSpec((tr, d), lambda i: (i, 0))
    vec = pl.BlockSpec((1, d), lambda i: (0, 0))
    return pl.pallas_call(
        body,
        grid=(s // tr,),
        in_specs=[row, row, row, vec],
        out_specs=[row, row, vec],
        out_shape=[jax.ShapeDtypeStruct((s, d), BF16), jax.ShapeDtypeStruct((s, d), BF16),
                   jax.ShapeDtypeStruct((1, d), F32)],
        compiler_params=_params(("arbitrary",)),
        name="ple_bwd",
    )(dh, pe, gl, ple_norm)


def _loss_head(h, w, target):
    s, d = h.shape
    tr = _pick(s, (512, 256, 128))

    def body(h_ref, w_ref, t_ref, l_ref, dh_ref, dw_ref):
        hv = h_ref[...]
        r = lax.rsqrt(jnp.mean(hv * hv, axis=-1, keepdims=True) + RMS_EPS)
        hn = hv * r
        diff = hn * w_ref[...] - t_ref[...]
        lpart = jnp.zeros((8, LANES), F32) + (0.5 / d) * jnp.sum(jnp.sum(diff * diff, axis=1, keepdims=True), axis=0, keepdims=True)
        dy = diff * (1.0 / d)
        dxh = dy * w_ref[...]
        dh_ref[...] = r * (dxh - hn * jnp.mean(dxh * hn, axis=-1, keepdims=True))
        part = jnp.sum(dy * hn, axis=0, keepdims=True)

        @pl.when(pl.program_id(0) == 0)
        def _():
            l_ref[...] = lpart
            dw_ref[...] = part

        @pl.when(pl.program_id(0) > 0)
        def _():
            l_ref[...] += lpart
            dw_ref[...] += part

    row = pl.BlockSpec((tr, d), lambda i: (i, 0))
    vec = pl.BlockSpec((1, d), lambda i: (0, 0))
    return pl.pallas_call(
        body,
        grid=(s // tr,),
        in_specs=[row, vec, row],
        out_specs=[pl.BlockSpec((8, LANES), lambda i: (0, 0)), row, vec],
        out_shape=[jax.ShapeDtypeStruct((8, LANES), F32), jax.ShapeDtypeStruct((s, d), F32),
                   jax.ShapeDtypeStruct((1, d), F32)],
        compiler_params=_params(("arbitrary",)),
        name="loss_head",
    )(h, w, target)


PER_LAYER = ("norm_mix", "norm_ffn", "ffn_w_gu", "ffn_w_down", "ple_w_proj", "ple_norm", "ple_gate_norm", "ple_w_gate")


def _pad_lanes(v):
    return jnp.pad(v.astype(F32), (0, LANES - v.shape[0]))[None, :]


def _kernel_layouts(full):
    w = {}
    for k in ("norm_mix", "norm_ffn", "ple_norm", "ple_gate_norm", "ssd_conv_b", "ssd_norm_w", "gmlp_b_in", "gmlp_ln_w",
              "gmlp_ln_b", "gmlp_w_s"):
        w[k] = [full[k][i].astype(F32) for i in range(full[k].shape[0])]
    w["final_norm"] = full["final_norm"].astype(F32)
    n_ssd = full["ssd_w_in"].shape[0]
    w["ssd_w_zx"] = [full["ssd_w_in"][j][:, :SSD_ZX].astype(BF16) for j in range(n_ssd)]
    w["ssd_w_dt"] = [jnp.pad(full["ssd_w_in"][j][:, SSD_ZX:].astype(BF16), ((0, 0), (0, LANES - SSD_HEADS)))
                     for j in range(n_ssd)]
    w["ssd_conv_w"] = [jnp.pad(full["ssd_conv_w"][j].astype(F32), ((0, 8 - CONV_K), (0, 0))) for j in range(n_ssd)]
    for k in ("ssd_dt_bias", "ssd_a_log", "ssd_d"):
        w[k] = [_pad_lanes(full[k][j]) for j in range(n_ssd)]
    w["ssd_w_out"] = [full["ssd_w_out"][j].astype(BF16) for j in range(n_ssd)]
    n_g = full["gmlp_w_in"].shape[0]
    w["gmlp_w_in"] = [full["gmlp_w_in"][j].astype(BF16) for j in range(n_g)]
    w["gmlp_w_out"] = [full["gmlp_w_out"][j].astype(BF16) for j in range(n_g)]
    w["gmlp_b_st"] = [jnp.pad(full["gmlp_b_s"][j].astype(F32).T, ((0, 0), (0, LANES - GMLP_GROUPS))) for j in range(n_g)]
    w["ffn_w_gu"] = [jnp.concatenate([full["ffn_w_gate"][i], full["ffn_w_up"][i]], axis=1).astype(BF16)
                     for i in range(DEPTH)]
    w["ffn_w_down"] = [full["ffn_w_down"][i].astype(BF16) for i in range(DEPTH)]
    w["ple_w_proj"] = [full["ple_w_proj"][i].astype(BF16) for i in range(DEPTH)]
    w["ple_w_gate"] = [full["ple_w_gate"][i].astype(BF16) for i in range(DEPTH)]
    return w


def _reference_layouts(g):
    out = {}
    for k in ("norm_mix", "norm_ffn", "ple_norm", "ple_gate_norm", "ssd_conv_b", "ssd_norm_w", "gmlp_b_in", "gmlp_ln_w",
              "gmlp_ln_b", "gmlp_w_s", "ssd_conv_w", "ssd_dt_bias", "ssd_a_log", "ssd_d", "ssd_w_out", "gmlp_w_in",
              "gmlp_w_out", "ffn_w_down", "ple_w_proj", "ple_w_gate"):
        out[k] = jnp.stack(g[k])
    out["final_norm"] = g["final_norm"]
    out["ssd_w_in"] = jnp.stack([jnp.concatenate([zx, dt[:, :SSD_HEADS]], axis=1)
                                 for zx, dt in zip(g["ssd_w_zx"], g["ssd_w_dt"])])
    out["gmlp_b_s"] = jnp.stack([b[:, :GMLP_GROUPS].T for b in g["gmlp_b_st"]])
    out["ffn_w_gate"] = jnp.stack([gu[:, :FFN_DIM] for gu in g["ffn_w_gu"]])
    out["ffn_w_up"] = jnp.stack([gu[:, FFN_DIM:] for gu in g["ffn_w_gu"]])
    return out


def _local_step(x, p, target, w):
    saved = []
    h = x
    for i in range(DEPTH):
        j = i // 2
        sv = {"h0": h}
        hn = _rms_fwd(h, w["norm_mix"][i][None, :])
        sv["hn"] = hn
        if i % 2 == 0:
            zx = _mm(hn, w["ssd_w_zx"][j], "nn", BF16)
            dtp = _mm(hn, w["ssd_w_dt"][j], "nn", F32)
            xbc = _ssd_conv_fwd(zx, w["ssd_conv_w"][j], w["ssd_conv_b"][j][None, :])
            y, prev = _ssd_scan_fwd(xbc, dtp, w["ssd_dt_bias"][j], w["ssd_a_log"][j], w["ssd_d"][j])
            gn = _ssd_gate_fwd(y, zx, w["ssd_norm_w"][j][None, :])
            h = _mm(gn, w["ssd_w_out"][j], "nn", F32, add=h)
            sv.update(zx=zx, dtp=dtp, xbc=xbc, y=y, prev=prev, gn=gn)
        else:
            pre = _mm(hn, w["gmlp_w_in"][j], "nn", BF16)
            u, v = _gmlp_act_fwd(pre, w["gmlp_b_in"][j][None, :], w["gmlp_ln_w"][j][None, :], w["gmlp_ln_b"][j][None, :])
            gated = _gmlp_mix_fwd(u, v, w["gmlp_w_s"][j], w["gmlp_b_st"][j])
            h = _mm(gated, w["gmlp_w_out"][j], "nn", F32, add=h)
            sv.update(pre=pre, u=u, v=v, gated=gated)
        sv["h1"] = h
        un = _rms_fwd(h, w["norm_ffn"][i][None, :])
        gu = _mm(un, w["ffn_w_gu"][i], "nn", BF16)
        hid = _swiglu_fwd(gu)
        h = _mm(hid, w["ffn_w_down"][i], "nn", F32, add=h)
        sv.update(un=un, gu=gu, hid=hid, h2=h)
        pe = _mm(p[i], w["ple_w_proj"][i], "nn", F32)
        hg = _rms_fwd(h, w["ple_gate_norm"][i][None, :])
        gl = _mm(hg, w["ple_w_gate"][i], "nn", F32)
        h = _ple_fwd(pe, gl, h, w["ple_norm"][i][None, :])
        sv.update(pe=pe, hg=hg, gl=gl)
        saved.append(sv)

    lpart, dh, d_final = _loss_head(h, w["final_norm"][None, :], target)
    g = {k: [None] * (DEPTH if k in PER_LAYER else DEPTH // 2) for k in w if k != "final_norm"}
    g["final_norm"] = d_final[0]

    for i in reversed(range(DEPTH)):
        j = i // 2
        sv = saved[i]
        dgl, dpe, d_ple_norm = _ple_bwd(dh, sv["pe"], sv["gl"], w["ple_norm"][i][None, :])
        g["ple_norm"][i] = d_ple_norm[0]
        g["ple_w_gate"][i] = _mm(sv["hg"], dgl, "tn", F32)
        g["ple_w_proj"][i] = _mm(p[i], dpe, "tn", F32)
        dhg = _mm(dgl, w["ple_w_gate"][i], "nt", F32)
        dh, d_gate_norm = _rms_bwd(dhg, sv["h2"], w["ple_gate_norm"][i][None, :], dh)
        g["ple_gate_norm"][i] = d_gate_norm[0]
        dhid = _mm(dh, w["ffn_w_down"][i], "nt", BF16)
        g["ffn_w_down"][i] = _mm(sv["hid"], dh, "tn", F32)
        dgu = _swiglu_bwd(sv["gu"], dhid)
        g["ffn_w_gu"][i] = _mm(sv["un"], dgu, "tn", F32)
        dun = _mm(dgu, w["ffn_w_gu"][i], "nt", F32)
        dh, d_norm_ffn = _rms_bwd(dun, sv["h1"], w["norm_ffn"][i][None, :], dh)
        g["norm_ffn"][i] = d_norm_ffn[0]
        if i % 2 == 0:
            dgn = _mm(dh, w["ssd_w_out"][j], "nt", F32)
            g["ssd_w_out"][j] = _mm(sv["gn"], dh, "tn", F32)
            dy, dzx, d_norm_w = _ssd_gate_bwd(dgn, sv["y"], sv["zx"], w["ssd_norm_w"][j][None, :])
            g["ssd_norm_w"][j] = d_norm_w[0]
            dxbc, ddtp, d_bias, d_alog, d_d = _ssd_scan_bwd(sv["xbc"], sv["dtp"], sv["prev"], dy, w["ssd_dt_bias"][j],
                                                            w["ssd_a_log"][j], w["ssd_d"][j])
            g["ssd_dt_bias"][j] = d_bias[0, :SSD_HEADS]
            g["ssd_a_log"][j] = d_alog[0, :SSD_HEADS]
            g["ssd_d"][j] = d_d[0, :SSD_HEADS]
            dzx, d_conv_w, d_conv_b = _ssd_conv_bwd(sv["zx"], dxbc, w["ssd_conv_w"][j], w["ssd_conv_b"][j][None, :], dzx)
            g["ssd_conv_w"][j] = d_conv_w[:CONV_K]
            g["ssd_conv_b"][j] = d_conv_b[0]
            g["ssd_w_zx"][j] = _mm(sv["hn"], dzx, "tn", F32)
            g["ssd_w_dt"][j] = _mm(sv["hn"], ddtp, "tn", F32)
            dhn = _mm(dzx, w["ssd_w_zx"][j], "nt", F32)
            dhn = _mm(ddtp, w["ssd_w_dt"][j], "nt", F32, add=dhn)
        else:
            dgated = _mm(dh, w["gmlp_w_out"][j], "nt", BF16)
            g["gmlp_w_out"][j] = _mm(sv["gated"], dh, "tn", F32)
            du, dv, d_ws, d_bst = _gmlp_mix_bwd(dgated, sv["u"], sv["v"], w["gmlp_w_s"][j], w["gmlp_b_st"][j])
            g["gmlp_w_s"][j] = d_ws
            g["gmlp_b_st"][j] = d_bst
            dpre, d_bin, d_lnw, d_lnb = _gmlp_act_bwd(sv["pre"], w["gmlp_b_in"][j][None, :], w["gmlp_ln_w"][j][None, :],
                                                     du, dv)
            g["gmlp_b_in"][j] = d_bin[0]
            g["gmlp_ln_w"][j] = d_lnw[0]
            g["gmlp_ln_b"][j] = d_lnb[0]
            g["gmlp_w_in"][j] = _mm(sv["hn"], dpre, "tn", F32)
            dhn = _mm(dpre, w["gmlp_w_in"][j], "nt", F32)
        dh, d_norm_mix = _rms_bwd(dhn, sv["h0"], w["norm_mix"][i][None, :], dh)
        g["norm_mix"][i] = d_norm_mix[0]
    return lpart[0, 0], dh, g


PACK_COLS = 1024
ANY = pl.BlockSpec(memory_space=pl.ANY)


def _mesh_pos():
    return lax.axis_index("x"), lax.axis_index("y"), lax.axis_index("c")


def _all_gather(xs, name):
    r, cdim = xs.shape

    def body(x_ref, out_ref, send_sems, recv_sems, local_sem):
        x, y, c = _mesh_pos()
        me, sibling = (x, y, c), (x, y, 1 - c)
        chips = [(1 - x, y), (x, 1 - y), (1 - x, 1 - y)]

        def slot(px, py, pc):
            return out_ref.at[4 * px + 2 * py + pc]

        def copy(k, block, to, src=None):
            return pltpu.make_async_remote_copy(
                src_ref=slot(*block) if src is None else src, dst_ref=slot(*block),
                send_sem=send_sems.at[k], recv_sem=recv_sems.at[k], device_id=to, device_id_type=MESH_ID)

        mine = pltpu.make_async_copy(x_ref, slot(*me), local_sem)
        mine.start()
        first = [copy(0, me, sibling, src=x_ref)]
        first += [copy(1 + j, me, (*chip, c), src=x_ref) for j, chip in enumerate(chips)]
        for cp in first:
            cp.start()
        passed = [copy(4 + j, (*chip, c), sibling) for j, chip in enumerate(chips)]
        for j, chip in enumerate(chips):
            copy(1 + j, (*chip, c), me).wait_recv()
            passed[j].start()
        copy(0, sibling, me).wait_recv()
        for j, chip in enumerate(chips):
            copy(4 + j, (*chip, 1 - c), me).wait_recv()
        for cp in first + passed:
            cp.wait_send()
        mine.wait()

    return pl.pallas_call(
        body,
        out_shape=jax.ShapeDtypeStruct((N_DEV, r, cdim), xs.dtype),
        in_specs=[ANY],
        out_specs=ANY,
        scratch_shapes=[pltpu.SemaphoreType.DMA((7,)), pltpu.SemaphoreType.DMA((7,)), pltpu.SemaphoreType.DMA],
        name=name,
    )(xs)


def _exchange_sibling(send):
    def body(s_ref, land_ref, send_sem, recv_sem):
        x, y, c = _mesh_pos()
        cp = pltpu.make_async_remote_copy(src_ref=s_ref, dst_ref=land_ref, send_sem=send_sem, recv_sem=recv_sem,
                                          device_id=(x, y, 1 - c), device_id_type=MESH_ID)
        cp.start()
        cp.wait()

    return pl.pallas_call(
        body,
        out_shape=jax.ShapeDtypeStruct(send.shape, send.dtype),
        in_specs=[ANY],
        out_specs=ANY,
        scratch_shapes=[pltpu.SemaphoreType.DMA, pltpu.SemaphoreType.DMA],
        name="rs_exchange_sibling",
    )(send)


def _exchange_chips(partial):
    def body(p_ref, land_ref, send_sems, recv_sems):
        x, y, c = _mesh_pos()
        chips = [(1 - x, y), (x, 1 - y), (1 - x, 1 - y)]
        cps = [pltpu.make_async_remote_copy(src_ref=p_ref.at[2 * cx + cy], dst_ref=land_ref.at[j],
                                            send_sem=send_sems.at[j], recv_sem=recv_sems.at[j],
                                            device_id=(cx, cy, c), device_id_type=MESH_ID)
               for j, (cx, cy) in enumerate(chips)]
        for cp in cps:
            cp.start()
        for cp in cps:
            cp.wait()

    return pl.pallas_call(
        body,
        out_shape=jax.ShapeDtypeStruct((3,) + partial.shape[1:], partial.dtype),
        in_specs=[ANY],
        out_specs=ANY,
        scratch_shapes=[pltpu.SemaphoreType.DMA((3,)), pltpu.SemaphoreType.DMA((3,))],
        name="rs_exchange_chips",
    )(partial)


def _sum_pairs(a, b):
    n, r, cdim = a.shape
    tr = _pick(r, (256, 128, 64))

    def body(a_ref, b_ref, o_ref):
        o_ref[...] = (a_ref[...].astype(F32) + b_ref[...].astype(F32)).astype(o_ref.dtype)

    blk = pl.BlockSpec((1, tr, cdim), lambda i, j: (i, j, 0))
    return pl.pallas_call(
        body, grid=(n, r // tr), in_specs=[blk, blk], out_specs=blk,
        out_shape=jax.ShapeDtypeStruct(a.shape, a.dtype),
        compiler_params=_params(("parallel", "parallel")), name="rs_sum_pairs",
    )(a, b)


def _sum_final(own, land):
    r, cdim = own.shape
    tr = _pick(r, (256, 128, 64))

    def body(o_ref, l_ref, out_ref):
        acc = o_ref[...].astype(F32)
        for j in range(3):
            acc = acc + l_ref[j].astype(F32)
        out_ref[...] = acc

    return pl.pallas_call(
        body, grid=(r // tr,),
        in_specs=[pl.BlockSpec((tr, cdim), lambda i: (i, 0)), pl.BlockSpec((3, tr, cdim), lambda i: (0, i, 0))],
        out_specs=pl.BlockSpec((tr, cdim), lambda i: (i, 0)),
        out_shape=jax.ShapeDtypeStruct((r, cdim), F32),
        compiler_params=_params(("parallel",)), name="rs_sum_final",
    )(own, land)


def _sum_devices(gathered):
    n, r, cdim = gathered.shape
    tr = _pick(r, (64, 32, 16, 8))

    def body(g_ref, out_ref):
        acc = g_ref[0]
        for q in range(1, n):
            acc = acc + g_ref[q]
        out_ref[...] = acc

    return pl.pallas_call(
        body, grid=(r // tr,),
        in_specs=[pl.BlockSpec((n, tr, cdim), lambda i: (0, i, 0))],
        out_specs=pl.BlockSpec((tr, cdim), lambda i: (i, 0)),
        out_shape=jax.ShapeDtypeStruct((r, cdim), F32),
        compiler_params=_params(("parallel",)), name="sum_devices",
    )(gathered)


def _adamw(w, g, m, v):
    shape = w.shape
    cols = shape[-1]
    rows = w.size // cols
    tr = _pick(rows, (512, 256, 128, 64, 32, 16, 8))
    c1 = 1.0 - ADAM_B1 ** ADAM_STEP
    c2 = 1.0 - ADAM_B2 ** ADAM_STEP

    def body(w_ref, g_ref, m_ref, v_ref, d_ref, nm_ref, nv_ref):
        gv = g_ref[...]
        m2 = ADAM_B1 * m_ref[...] + (1.0 - ADAM_B1) * gv
        v2 = ADAM_B2 * v_ref[...] + (1.0 - ADAM_B2) * (gv * gv)
        d_ref[...] = -ADAM_LR * ((m2 / c1) / (jnp.sqrt(v2 / c2) + ADAM_EPS) + ADAM_WD * w_ref[...])
        nm_ref[...] = m2
        nv_ref[...] = v2

    blk = pl.BlockSpec((tr, cols), lambda i: (i, 0))
    sds = jax.ShapeDtypeStruct((rows, cols), F32)
    outs = pl.pallas_call(
        body, grid=(rows // tr,), in_specs=[blk] * 4, out_specs=[blk] * 3, out_shape=[sds] * 3,
        compiler_params=_params(("parallel",)), name=f"adamw_{rows}x{cols}",
    )(*(t.reshape(rows, cols) for t in (w, g, m, v)))
    return tuple(o.reshape(shape) for o in outs)


WEIGHTS = ("norm_mix", "norm_ffn", "ssd_w_in", "ssd_conv_w", "ssd_conv_b", "ssd_dt_bias", "ssd_a_log", "ssd_d",
           "ssd_norm_w", "ssd_w_out", "gmlp_w_in", "gmlp_b_in", "gmlp_ln_w", "gmlp_ln_b", "gmlp_w_s", "gmlp_b_s",
           "gmlp_w_out", "ffn_w_gate", "ffn_w_up", "ffn_w_down", "ple_w_proj", "ple_norm", "ple_gate_norm",
           "ple_w_gate", "final_norm")
ARG_NAMES = ("x", "p") + WEIGHTS + ("loss_target",) + tuple("m_" + n for n in WEIGHTS) + tuple("v_" + n for n in WEIGHTS)
SHARD_AXIS = {"ssd_w_in": 2, "ssd_conv_w": 2, "ssd_w_out": 1, "gmlp_w_in": 2, "gmlp_b_in": 1, "gmlp_ln_w": 1,
              "gmlp_ln_b": 1, "gmlp_w_out": 1, "ffn_w_gate": 2, "ffn_w_up": 2, "ffn_w_down": 1, "ple_w_proj": 2,
              "ple_w_gate": 1}
GATHER_BF16 = ("ssd_w_in", "ssd_w_out", "gmlp_w_in", "gmlp_w_out", "ffn_w_gate", "ffn_w_up", "ffn_w_down",
               "ple_w_proj", "ple_w_gate")
GATHER_F32 = ("ssd_conv_w", "gmlp_b_in", "gmlp_ln_w", "gmlp_ln_b")
SHARDED = GATHER_BF16 + GATHER_F32
REPLICATED = tuple(n for n in WEIGHTS if n not in SHARD_AXIS)


def _pack(arrs, dtype, row_mult, lead=0):
    flat = jnp.concatenate([t.reshape(t.shape[:lead] + (-1,)).astype(dtype) for t in arrs], axis=lead)
    n = flat.shape[-1]
    unit = row_mult * PACK_COLS
    total = -(-n // unit) * unit
    flat = jnp.pad(flat, [(0, 0)] * lead + [(0, total - n)])
    return flat.reshape(flat.shape[:lead] + (total // PACK_COLS, PACK_COLS))


def _unpack(buf, names, shapes, lead=0):
    flat = buf.reshape(buf.shape[:lead] + (-1,))
    out, off = {}, 0
    for n in names:
        size = math.prod(shapes[n])
        out[n] = lax.slice_in_dim(flat, off, off + size, axis=lead).reshape(buf.shape[:lead] + tuple(shapes[n]))
        off += size
    return out


def _merge_shards(seg, ax):
    t = jnp.moveaxis(seg, 0, ax)
    return t.reshape(t.shape[:ax] + (t.shape[ax] * t.shape[ax + 1],) + t.shape[ax + 2:])


def _split_for_cores(gfull, ax, c):
    shp = gfull.shape
    t = gfull.reshape(shp[:ax] + (2, 2, 2, shp[ax] // N_DEV) + shp[ax + 1:])

    def take(core):
        u = lax.dynamic_index_in_dim(t, core, axis=ax + 2, keepdims=False)
        u = jnp.moveaxis(u, (ax, ax + 1), (0, 1))
        return u.reshape((4,) + u.shape[2:])

    return take(c), take(1 - c)


def kernel(x, p, norm_mix, norm_ffn, ssd_w_in, ssd_conv_w, ssd_conv_b, ssd_dt_bias, ssd_a_log, ssd_d,
           ssd_norm_w, ssd_w_out, gmlp_w_in, gmlp_b_in, gmlp_ln_w, gmlp_ln_b, gmlp_w_s, gmlp_b_s,
           gmlp_w_out, ffn_w_gate, ffn_w_up, ffn_w_down, ple_w_proj, ple_norm, ple_gate_norm, ple_w_gate,
           final_norm, loss_target, m_norm_mix, m_norm_ffn, m_ssd_w_in, m_ssd_conv_w, m_ssd_conv_b,
           m_ssd_dt_bias, m_ssd_a_log, m_ssd_d, m_ssd_norm_w, m_ssd_w_out, m_gmlp_w_in, m_gmlp_b_in,
           m_gmlp_ln_w, m_gmlp_ln_b, m_gmlp_w_s, m_gmlp_b_s, m_gmlp_w_out, m_ffn_w_gate, m_ffn_w_up,
           m_ffn_w_down, m_ple_w_proj, m_ple_norm, m_ple_gate_norm, m_ple_w_gate, m_final_norm, v_norm_mix,
           v_norm_ffn, v_ssd_w_in, v_ssd_conv_w, v_ssd_conv_b, v_ssd_dt_bias, v_ssd_a_log, v_ssd_d,
           v_ssd_norm_w, v_ssd_w_out, v_gmlp_w_in, v_gmlp_b_in, v_gmlp_ln_w, v_gmlp_ln_b, v_gmlp_w_s,
           v_gmlp_b_s, v_gmlp_w_out, v_ffn_w_gate, v_ffn_w_up, v_ffn_w_down, v_ple_w_proj, v_ple_norm,
           v_ple_gate_norm, v_ple_w_gate, v_final_norm):
    given = locals()
    a = {n: given[n] for n in ARG_NAMES}
    mx, my, c = _mesh_pos()
    xs = a["x"][0]
    ps = a["p"][:, 0]
    target = a["loss_target"][0]
    shard_shapes = {n: a[n].shape for n in WEIGHTS}

    full = {n: a[n] for n in REPLICATED}
    got = _all_gather(_pack([a[n] for n in GATHER_BF16], BF16, 256), "ag_weights")
    for n, seg in _unpack(got, GATHER_BF16, shard_shapes, lead=1).items():
        full[n] = _merge_shards(seg, SHARD_AXIS[n])
    got = _all_gather(_pack([a[n] for n in GATHER_F32], F32, 8), "ag_small")
    for n, seg in _unpack(got, GATHER_F32, shard_shapes, lead=1).items():
        full[n] = _merge_shards(seg, SHARD_AXIS[n])

    lpart, dx, g = _local_step(xs, ps, target, _kernel_layouts(full))
    gfull = _reference_layouts(g)
    loss = lax.psum(lpart, ("x", "y", "c"))

    halves = [_split_for_cores(gfull[n], SHARD_AXIS[n], c) for n in SHARDED]
    keep = _pack([h[0] for h in halves], BF16, 256, lead=1)
    send = _pack([h[1] for h in halves], BF16, 256, lead=1)
    partial = _sum_pairs(keep, _exchange_sibling(send))
    own = lax.dynamic_index_in_dim(partial, 2 * mx + my, axis=0, keepdims=False)
    gshard = _unpack(_sum_final(own, _exchange_chips(partial)), SHARDED, shard_shapes)
    rep = _all_gather(_pack([gfull[n] for n in REPLICATED], F32, 64), "ag_replicated_grads")
    grep = _unpack(_sum_devices(rep), REPLICATED, shard_shapes)
    grads = {**gshard, **grep}

    upd = {n: _adamw(a[n], grads[n], a["m_" + n], a["v_" + n]) for n in WEIGHTS}
    return (loss, dx[None], *[grads[n] for n in WEIGHTS], *[upd[n][0] for n in WEIGHTS],
            *[upd[n][1] for n in WEIGHTS], *[upd[n][2] for n in WEIGHTS])
```

```python
import functools
import math

import jax
import jax.numpy as jnp
from jax import lax
from jax.experimental import pallas as pl
from jax.experimental.pallas import tpu as pltpu

F32 = jnp.float32
BF16 = jnp.bfloat16

N_DEV = 8
D_MODEL = 1024
DEPTH = 4
SSD_INNER = 2048
SSD_HEADS = 32
SSD_HEADDIM = 64
SSD_GROUPS = 8
SSD_STATE = 128
SSD_GROUP_W = SSD_INNER // SSD_GROUPS
SSD_CONV_DIM = SSD_INNER + 2 * SSD_GROUPS * SSD_STATE
SSD_IN_DIM = 2 * SSD_INNER + SSD_CONV_DIM - SSD_INNER + SSD_HEADS
SSD_ZX = SSD_INNER + SSD_CONV_DIM
CONV_K = 4
CHUNK = 128
GMLP_INNER = 2048
GMLP_GROUPS = 16
FFN_DIM = 2816
PLE_DIM = 256
RMS_EPS = 1e-6
LN_EPS = 1e-5
LANES = 128
VMEM_LIMIT = 56 * 1024 * 1024

ADAM_LR = 0.001
ADAM_B1 = 0.9
ADAM_B2 = 0.999
ADAM_EPS = 1e-08
ADAM_WD = 0.01
ADAM_STEP = 10

HI = lax.Precision.HIGHEST
MESH_ID = pl.DeviceIdType.MESH


def _pick(n, cands):
    for c in cands:
        if c <= n and n % c == 0:
            return c
    return n


def _params(dims):
    return pltpu.CompilerParams(dimension_semantics=dims, vmem_limit_bytes=VMEM_LIMIT)


def _dot(a, b, dims=(((1,), (0,)), ((), ())), precision=None):
    return lax.dot_general(a, b, dims, precision=precision, preferred_element_type=F32)


NN = (((1,), (0,)), ((), ()))
NT = (((1,), (1,)), ((), ()))
TN = (((0,), (0,)), ((), ()))


def _sigmoid(x):
    return 1.0 / (1.0 + jnp.exp(-x))


def _mm(a, b, mode, out_dtype, add=None):
    if mode == "nn":
        m, k = a.shape
        n = b.shape[1]
    elif mode == "nt":
        m, k = a.shape
        n = b.shape[0]
    else:
        k, m = a.shape
        n = b.shape[1]
    if mode == "tn":
        tm = _pick(m, (1408, 1024, 512, 256, 128))
        tn = _pick(n, tuple(c for c in (1024, 512, 256, 128) if tm * c * 4 <= 4 * 1024 * 1024))
        tk = _pick(k, (512, 256, 128))
    else:
        tm = _pick(m, (512, 256, 128))
        tn = _pick(n, (2816, 1024, 512, 256, 128))
        tk = k if k <= 2816 else _pick(k, (2816, 2048, 1024, 512))
    nk = k // tk
    dims = {"nn": NN, "nt": NT, "tn": TN}[mode]

    def body(*refs):
        if add is None:
            a_ref, b_ref, o_ref = refs[:3]
            add_ref = None
            rest = refs[3:]
        else:
            a_ref, b_ref, add_ref, o_ref = refs[:4]
            rest = refs[4:]
        part = _dot(a_ref[...].astype(BF16), b_ref[...].astype(BF16), dims)

        def finish(acc):
            if add_ref is not None:
                acc = acc + add_ref[...]
            o_ref[...] = acc.astype(o_ref.dtype)

        if nk == 1:
            finish(part)
        else:
            acc_ref = rest[0]
            kk = pl.program_id(2)

            @pl.when(kk == 0)
            def _():
                acc_ref[...] = part

            @pl.when(kk > 0)
            def _():
                acc_ref[...] += part

            @pl.when(kk == nk - 1)
            def _():
                finish(acc_ref[...])

    if mode == "nn":
        a_spec = pl.BlockSpec((tm, tk), lambda i, j, kk: (i, kk))
        b_spec = pl.BlockSpec((tk, tn), lambda i, j, kk: (kk, j))
    elif mode == "nt":
        a_spec = pl.BlockSpec((tm, tk), lambda i, j, kk: (i, kk))
        b_spec = pl.BlockSpec((tn, tk), lambda i, j, kk: (j, kk))
    else:
        a_spec = pl.BlockSpec((tk, tm), lambda i, j, kk: (kk, i))
        b_spec = pl.BlockSpec((tk, tn), lambda i, j, kk: (kk, j))
    o_spec = pl.BlockSpec((tm, tn), lambda i, j, kk: (i, j))
    in_specs = [a_spec, b_spec] + ([o_spec] if add is not None else [])
    args = (a, b) + ((add,) if add is not None else ())
    return pl.pallas_call(
        body,
        grid=(m // tm, n // tn, nk),
        in_specs=in_specs,
        out_specs=o_spec,
        out_shape=jax.ShapeDtypeStruct((m, n), out_dtype),
        scratch_shapes=[pltpu.VMEM((tm, tn), F32)] if nk > 1 else [],
        compiler_params=_params(("parallel", "parallel", "arbitrary")),
        name=f"mm_{mode}_{m}x{k}x{n}",
    )(*args)


def _rms_fwd(x, w):
    s, d = x.shape
    tr = _pick(s, (512, 256, 128))

    def body(x_ref, w_ref, o_ref):
        xv = x_ref[...]
        r = lax.rsqrt(jnp.mean(xv * xv, axis=-1, keepdims=True) + RMS_EPS)
        o_ref[...] = (xv * r * w_ref[...]).astype(o_ref.dtype)

    return pl.pallas_call(
        body,
        grid=(s // tr,),
        in_specs=[pl.BlockSpec((tr, d), lambda i: (i, 0)), pl.BlockSpec((1, d), lambda i: (0, 0))],
        out_specs=pl.BlockSpec((tr, d), lambda i: (i, 0)),
        out_shape=jax.ShapeDtypeStruct((s, d), BF16),
        compiler_params=_params(("parallel",)),
        name="rms_fwd",
    )(x, w)


def _rms_bwd(dyn, x, w, add):
    s, d = x.shape
    tr = _pick(s, (512, 256, 128))

    def body(dy_ref, x_ref, w_ref, add_ref, dx_ref, dw_ref):
        xv = x_ref[...]
        dy = dy_ref[...].astype(F32)
        r = lax.rsqrt(jnp.mean(xv * xv, axis=-1, keepdims=True) + RMS_EPS)
        xn = xv * r
        dxh = dy * w_ref[...]
        dx = r * (dxh - xn * jnp.mean(dxh * xn, axis=-1, keepdims=True))
        dx_ref[...] = add_ref[...] + dx
        part = jnp.sum(dy * xn, axis=0, keepdims=True)

        @pl.when(pl.program_id(0) == 0)
        def _():
            dw_ref[...] = part

        @pl.when(pl.program_id(0) > 0)
        def _():
            dw_ref[...] += part

    row = pl.BlockSpec((tr, d), lambda i: (i, 0))
    vec = pl.BlockSpec((1, d), lambda i: (0, 0))
    return pl.pallas_call(
        body,
        grid=(s // tr,),
        in_specs=[row, row, vec, row],
        out_specs=[row, vec],
        out_shape=[jax.ShapeDtypeStruct((s, d), F32), jax.ShapeDtypeStruct((1, d), F32)],
        compiler_params=_params(("arbitrary",)),
        name="rms_bwd",
    )(dyn, x, w, add)


CONV_ROWS = 256
CONV_COLS = 256
CONV_HALO = 16


def _conv_taps(ext, w, base, rows):
    acc = w[0:1, :] * ext[base:base + rows]
    for k in range(1, CONV_K):
        acc = acc + w[k:k + 1, :] * ext[base + k:base + k + rows]
    return acc


def _ssd_conv_fwd(zx, conv_w, conv_b):
    s = zx.shape[0]
    c = SSD_CONV_DIM
    nsteps = s // CONV_ROWS
    off = SSD_INNER // CONV_COLS

    def body(x_ref, w_ref, b_ref, o_ref):
        w = w_ref[...]
        b = b_ref[...]

        def step(i, carry):
            r0 = pl.multiple_of(i * CONV_ROWS, CONV_ROWS)
            cur = x_ref[pl.ds(r0, CONV_ROWS), :].astype(F32)
            p0 = pl.multiple_of(jnp.maximum(r0 - CONV_HALO, 0), CONV_HALO)
            prev = x_ref[pl.ds(p0, CONV_HALO), :].astype(F32)
            prev = jnp.where(i == 0, 0.0, prev)
            ext = jnp.concatenate([prev, cur], axis=0)
            acc = _conv_taps(ext, w, CONV_HALO - (CONV_K - 1), CONV_ROWS) + b
            o_ref[pl.ds(r0, CONV_ROWS), :] = (acc * _sigmoid(acc)).astype(o_ref.dtype)
            return carry

        lax.fori_loop(0, nsteps, step, 0)

    return pl.pallas_call(
        body,
        grid=(c // CONV_COLS,),
        in_specs=[pl.BlockSpec((s, CONV_COLS), lambda j: (0, j + off)),
                  pl.BlockSpec((8, CONV_COLS), lambda j: (0, j)),
                  pl.BlockSpec((1, CONV_COLS), lambda j: (0, j))],
        out_specs=pl.BlockSpec((s, CONV_COLS), lambda j: (0, j)),
        out_shape=jax.ShapeDtypeStruct((s, c), BF16),
        compiler_params=_params(("parallel",)),
        name="ssd_conv_fwd",
    )(zx, conv_w, conv_b)


def _ssd_conv_bwd(zx, dxbc, conv_w, conv_b, dzx):
    s = zx.shape[0]
    c = SSD_CONV_DIM
    nsteps = s // CONV_ROWS
    off = SSD_INNER // CONV_COLS

    def body(x_ref, dy_ref, w_ref, b_ref, dzx_in_ref, dx_ref, dw_ref, db_ref, dc_ref):
        w = w_ref[...]
        b = b_ref[...]
        dc_ref[pl.ds(s, CONV_HALO), :] = jnp.zeros((CONV_HALO, CONV_COLS), F32)

        def step1(i, carry):
            dw0, dw1, dw2, dw3, dbs = carry
            r0 = pl.multiple_of(i * CONV_ROWS, CONV_ROWS)
            cur = x_ref[pl.ds(r0, CONV_ROWS), :].astype(F32)
            p0 = pl.multiple_of(jnp.maximum(r0 - CONV_HALO, 0), CONV_HALO)
            prev = x_ref[pl.ds(p0, CONV_HALO), :].astype(F32)
            prev = jnp.where(i == 0, 0.0, prev)
            ext = jnp.concatenate([prev, cur], axis=0)
            base = CONV_HALO - (CONV_K - 1)
            acc = _conv_taps(ext, w, base, CONV_ROWS) + b
            sg = _sigmoid(acc)
            dcv = dy_ref[pl.ds(r0, CONV_ROWS), :].astype(F32) * (sg * (1.0 + acc * (1.0 - sg)))
            dc_ref[pl.ds(r0, CONV_ROWS), :] = dcv
            dws = [jnp.sum(dcv * ext[base + k:base + k + CONV_ROWS], axis=0, keepdims=True) for k in range(CONV_K)]
            return (dw0 + dws[0], dw1 + dws[1], dw2 + dws[2], dw3 + dws[3], dbs + jnp.sum(dcv, axis=0, keepdims=True))

        z = jnp.zeros((1, CONV_COLS), F32)
        dw0, dw1, dw2, dw3, dbs = lax.fori_loop(0, nsteps, step1, (z, z, z, z, z))
        dw_ref[...] = jnp.concatenate([dw0, dw1, dw2, dw3, z, z, z, z], axis=0)
        db_ref[...] = dbs

        def step2(i, carry):
            r0 = pl.multiple_of(i * CONV_ROWS, CONV_ROWS)
            ext = dc_ref[pl.ds(r0, CONV_ROWS + CONV_HALO), :]
            acc = w[0:1, :] * ext[CONV_K - 1:CONV_K - 1 + CONV_ROWS]
            for k in range(1, CONV_K):
                acc = acc + w[k:k + 1, :] * ext[CONV_K - 1 - k:CONV_K - 1 - k + CONV_ROWS]
            dx_ref[pl.ds(r0, CONV_ROWS), :] = acc.astype(dx_ref.dtype)
            return carry

        lax.fori_loop(0, nsteps, step2, 0)

    col = pl.BlockSpec((s, CONV_COLS), lambda j: (0, j))
    shifted = pl.BlockSpec((s, CONV_COLS), lambda j: (0, j + off))
    return pl.pallas_call(
        body,
        grid=(c // CONV_COLS,),
        in_specs=[shifted, col,
                  pl.BlockSpec((8, CONV_COLS), lambda j: (0, j)),
                  pl.BlockSpec((1, CONV_COLS), lambda j: (0, j)),
                  pl.BlockSpec(memory_space=pl.ANY)],
        out_specs=[shifted, pl.BlockSpec((8, CONV_COLS), lambda j: (0, j)), pl.BlockSpec((1, CONV_COLS), lambda j: (0, j))],
        out_shape=[jax.ShapeDtypeStruct((s, SSD_ZX), BF16), jax.ShapeDtypeStruct((8, c), F32),
                   jax.ShapeDtypeStruct((1, c), F32)],
        scratch_shapes=[pltpu.VMEM((s + CONV_HALO, CONV_COLS), F32)],
        input_output_aliases={4: 0},
        compiler_params=_params(("parallel",)),
        name="ssd_conv_bwd",
    )(zx, dxbc, conv_w, conv_b, dzx)


def _ssd_consts():
    li = lax.broadcasted_iota(jnp.int32, (CHUNK, CHUNK), 0)
    si = lax.broadcasted_iota(jnp.int32, (CHUNK, CHUNK), 1)
    tril = li >= si
    hrow = lax.broadcasted_iota(jnp.int32, (LANES, SSD_INNER), 0)
    hcol = lax.broadcasted_iota(jnp.int32, (LANES, SSD_INNER), 1) // SSD_HEADDIM
    expand = (hrow == hcol).astype(F32)
    return tril, expand


def _ssd_chunk_common(dtp_ref, bias_ref, alog_ref, tril, expand):
    lane = lax.broadcasted_iota(jnp.int32, (1, LANES), 1)
    valid = lane < SSD_HEADS
    pre = dtp_ref[...] + bias_ref[...]
    dt = jnp.where(valid, jnp.maximum(pre, 0.0) + jnp.log1p(jnp.exp(-jnp.abs(pre))), 0.0)
    a = jnp.where(valid, -jnp.exp(alog_ref[...]), 0.0)
    da = dt * a
    cs = _dot(tril.astype(F32), da, NN, HI)
    cs_x = _dot(cs, expand, NN, HI)
    dt_x = _dot(dt, expand, NN, HI)
    return pre, dt, a, cs, cs_x, dt_x


def _ssd_scan_fwd(xbc, dtp, dt_bias, a_log, d_skip):
    s = xbc.shape[0]
    nc = s // CHUNK
    gw = SSD_GROUP_W

    def body(xbc_ref, dtp_ref, bias_ref, alog_ref, d_ref, y_ref, prev_ref, state_ref):
        c = pl.program_id(0)

        @pl.when(c == 0)
        def _():
            state_ref[...] = jnp.zeros_like(state_ref)

        tril, expand = _ssd_consts()
        pre, dt, a, cs, cs_x, dt_x = _ssd_chunk_common(dtp_ref, bias_ref, alog_ref, tril, expand)
        cs_t = cs.T
        d_x = _dot(jnp.broadcast_to(d_ref[...], (8, LANES)), expand, NN, HI)[0:1, :]
        cs_last = cs_x[CHUNK - 1:CHUNK, :]
        dec_out = jnp.exp(cs_x)
        dec_st = jnp.exp(cs_last - cs_x)
        dec_ch = jnp.exp(cs_last)
        x = xbc_ref[:, 0:SSD_INNER].astype(F32)
        xr = x * dt_x
        xrs = xr * dec_st
        lane_g = lax.broadcasted_iota(jnp.int32, (1, gw), 1) // SSD_HEADDIM
        for g in range(SSD_GROUPS):
            sl = slice(g * gw, (g + 1) * gw)
            bg = xbc_ref[:, SSD_INNER + g * SSD_STATE:SSD_INNER + (g + 1) * SSD_STATE]
            cg = xbc_ref[:, SSD_INNER + (SSD_GROUPS + g) * SSD_STATE:SSD_INNER + (SSD_GROUPS + g + 1) * SSD_STATE]
            cb = _dot(cg, bg, NT)
            prev_g = state_ref[:, sl]
            prev_ref[0, :, sl] = prev_g
            yo = _dot(cg, prev_g.astype(BF16), NN) * dec_out[:, sl]
            xr_g = xr[:, sl]
            yd = jnp.zeros((CHUNK, gw), F32)
            for r in range(SSD_HEADS // SSD_GROUPS):
                h = g * (SSD_HEADS // SSD_GROUPS) + r
                diff = cs[:, h:h + 1] - cs_t[h:h + 1, :]
                lmat = jnp.exp(jnp.where(tril, diff, -1e30))
                wmat = (cb * lmat).astype(BF16)
                xr_h = jnp.where(lane_g == r, xr_g, 0.0).astype(BF16)
                yd = yd + _dot(wmat, xr_h, NN)
            y_ref[:, sl] = yd + yo + x[:, sl] * d_x[:, sl]
            sc = _dot(bg, xrs[:, sl].astype(BF16), TN)
            state_ref[:, sl] = prev_g * dec_ch[:, sl] + sc

    vec = pl.BlockSpec((1, LANES), lambda c: (0, 0))
    return pl.pallas_call(
        body,
        grid=(nc,),
        in_specs=[pl.BlockSpec((CHUNK, SSD_CONV_DIM), lambda c: (c, 0)),
                  pl.BlockSpec((CHUNK, LANES), lambda c: (c, 0)), vec, vec, vec],
        out_specs=[pl.BlockSpec((CHUNK, SSD_INNER), lambda c: (c, 0)),
                   pl.BlockSpec((1, SSD_STATE, SSD_INNER), lambda c: (c, 0, 0))],
        out_shape=[jax.ShapeDtypeStruct((s, SSD_INNER), F32), jax.ShapeDtypeStruct((nc, SSD_STATE, SSD_INNER), F32)],
        scratch_shapes=[pltpu.VMEM((SSD_STATE, SSD_INNER), F32)],
        compiler_params=_params(("arbitrary",)),
        name="ssd_scan_fwd",
    )(xbc, dtp, dt_bias, a_log, d_skip)


def _ssd_scan_bwd(xbc, dtp, prev, dy, dt_bias, a_log, d_skip):
    s = xbc.shape[0]
    nc = s // CHUNK
    gw = SSD_GROUP_W
    hpg = SSD_HEADS // SSD_GROUPS

    def body(xbc_ref, dtp_ref, prev_ref, dy_ref, bias_ref, alog_ref, d_ref,
             dxbc_ref, ddtp_ref, dbias_ref, dalog_ref, dd_ref, dp_ref, ddx_ref):
        step = pl.program_id(0)

        @pl.when(step == 0)
        def _():
            dp_ref[...] = jnp.zeros_like(dp_ref)
            ddx_ref[...] = jnp.zeros_like(ddx_ref)
            dbias_ref[...] = jnp.zeros_like(dbias_ref)
            dalog_ref[...] = jnp.zeros_like(dalog_ref)

        tril, expand = _ssd_consts()
        pre, dt, a, cs, cs_x, dt_x = _ssd_chunk_common(dtp_ref, bias_ref, alog_ref, tril, expand)
        cs_t = cs.T
        d_x = _dot(jnp.broadcast_to(d_ref[...], (8, LANES)), expand, NN, HI)[0:1, :]
        cs_last = cs_x[CHUNK - 1:CHUNK, :]
        dec_out = jnp.exp(cs_x)
        dec_st = jnp.exp(cs_last - cs_x)
        dec_ch = jnp.exp(cs_last)
        x = xbc_ref[:, 0:SSD_INNER].astype(F32)
        dyv = dy_ref[...]
        xr = x * dt_x
        xrs = xr * dec_st
        lane_g = lax.broadcasted_iota(jnp.int32, (1, gw), 1) // SSD_HEADDIM
        hsel = lax.broadcasted_iota(jnp.int32, (CHUNK, LANES), 1)
        dcs = jnp.zeros((CHUNK, LANES), F32)
        last_parts = []
        t_parts = []
        dxr_parts = []
        for g in range(SSD_GROUPS):
            sl = slice(g * gw, (g + 1) * gw)
            bsl = slice(SSD_INNER + g * SSD_STATE, SSD_INNER + (g + 1) * SSD_STATE)
            csl = slice(SSD_INNER + (SSD_GROUPS + g) * SSD_STATE, SSD_INNER + (SSD_GROUPS + g + 1) * SSD_STATE)
            bg = xbc_ref[:, bsl]
            cg = xbc_ref[:, csl]
            cb = _dot(cg, bg, NT)
            prev_g = prev_ref[0, :, sl]
            prev_b = prev_g.astype(BF16)
            dp_g = dp_ref[:, sl]
            dp_b = dp_g.astype(BF16)
            dy_g = dyv[:, sl]
            xr_g = xr[:, sl]
            gmat = _dot(cg, prev_b, NN)
            dgm = (dy_g * dec_out[:, sl]).astype(BF16)
            dc_g = _dot(dgm, prev_b, NT)
            dprev = _dot(cg, dgm, TN)
            t1 = dy_g * gmat * dec_out[:, sl]
            mm_ = _dot(bg, dp_b, NN)
            db_g = _dot(xrs[:, sl].astype(BF16), dp_b, NT)
            dxr_g = mm_ * dec_st[:, sl]
            t2 = dxr_g * xr_g
            last = jnp.sum(t2, axis=0, keepdims=True) + jnp.sum(dp_g * prev_g, axis=0, keepdims=True) * dec_ch[:, sl]
            dp_ref[:, sl] = dp_g * dec_ch[:, sl] + dprev
            dcb = jnp.zeros((CHUNK, CHUNK), F32)
            for r in range(hpg):
                h = g * hpg + r
                diff = cs[:, h:h + 1] - cs_t[h:h + 1, :]
                lmat = jnp.exp(jnp.where(tril, diff, -1e30))
                wmat = cb * lmat
                dy_h = jnp.where(lane_g == r, dy_g, 0.0).astype(BF16)
                dw = _dot(dy_h, xr_g.astype(BF16), NT)
                dxr_g = dxr_g + _dot(wmat.astype(BF16), dy_h, TN)
                dcb = dcb + dw * lmat
                q = (dw * wmat).astype(BF16)
                onehot = (hsel == h).astype(BF16)
                dcs = dcs + _dot(q, onehot, NN) - _dot(q, onehot, TN)
            dcb_b = dcb.astype(BF16)
            dc_g = dc_g + _dot(dcb_b, bg, NN)
            db_g = db_g + _dot(dcb_b, cg, TN)
            dxbc_ref[:, bsl] = db_g.astype(dxbc_ref.dtype)
            dxbc_ref[:, csl] = dc_g.astype(dxbc_ref.dtype)
            t_parts.append(t1 - t2)
            last_parts.append(last)
            dxr_parts.append(dxr_g)
        dxr = jnp.concatenate(dxr_parts, axis=1)
        tt = jnp.concatenate(t_parts, axis=1)
        last_x = jnp.concatenate(last_parts, axis=1)
        dxbc_ref[:, 0:SSD_INNER] = (dxr * dt_x + dyv * d_x).astype(dxbc_ref.dtype)
        dcs = dcs + _dot(tt, expand, NT, HI)
        last_h = _dot(jnp.broadcast_to(last_x, (8, SSD_INNER)), expand, NT, HI)[0:1, :]
        rowi = lax.broadcasted_iota(jnp.int32, (CHUNK, LANES), 0)
        dcs = dcs + jnp.where(rowi == CHUNK - 1, last_h, 0.0)
        dda = _dot(tril.astype(F32), dcs, TN, HI)
        ddt = dda * a + _dot(dxr * x, expand, NT, HI)
        dpre = ddt * _sigmoid(pre)
        ddtp_ref[...] = dpre
        dbias_ref[...] += jnp.sum(dpre, axis=0, keepdims=True)
        dalog_ref[...] += jnp.sum(dda * dt, axis=0, keepdims=True) * a
        ddx_ref[...] += jnp.broadcast_to(jnp.sum(dyv * x, axis=0, keepdims=True), (8, SSD_INNER))

        @pl.when(step == nc - 1)
        def _():
            dd_ref[...] = _dot(ddx_ref[...], expand, NT, HI)[0:1, :]

    rev = lambda c: (nc - 1 - c, 0)
    vec = pl.BlockSpec((1, LANES), lambda c: (0, 0))
    return pl.pallas_call(
        body,
        grid=(nc,),
        in_specs=[pl.BlockSpec((CHUNK, SSD_CONV_DIM), rev), pl.BlockSpec((CHUNK, LANES), rev),
                  pl.BlockSpec((1, SSD_STATE, SSD_INNER), lambda c: (nc - 1 - c, 0, 0)),
                  pl.BlockSpec((CHUNK, SSD_INNER), rev), vec, vec, vec],
        out_specs=[pl.BlockSpec((CHUNK, SSD_CONV_DIM), rev), pl.BlockSpec((CHUNK, LANES), rev), vec, vec, vec],
        out_shape=[jax.ShapeDtypeStruct((s, SSD_CONV_DIM), BF16), jax.ShapeDtypeStruct((s, LANES), F32),
                   jax.ShapeDtypeStruct((1, LANES), F32), jax.ShapeDtypeStruct((1, LANES), F32),
                   jax.ShapeDtypeStruct((1, LANES), F32)],
        scratch_shapes=[pltpu.VMEM((SSD_STATE, SSD_INNER), F32), pltpu.VMEM((8, SSD_INNER), F32)],
        compiler_params=_params(("arbitrary",)),
        name="ssd_scan_bwd",
    )(xbc, dtp, prev, dy, dt_bias, a_log, d_skip)


def _ssd_gate_fwd(y, zx, norm_w):
    s = y.shape[0]
    tr = _pick(s, (256, 128))
    gw = SSD_GROUP_W

    def body(y_ref, z_ref, w_ref, o_ref):
        for g in range(SSD_GROUPS):
            sl = slice(g * gw, (g + 1) * gw)
            z = z_ref[:, sl].astype(F32)
            gv = y_ref[:, sl] * (z * _sigmoid(z))
            r = lax.rsqrt(jnp.mean(gv * gv, axis=-1, keepdims=True) + LN_EPS)
            o_ref[:, sl] = (gv * r * w_ref[:, sl]).astype(o_ref.dtype)

    row = pl.BlockSpec((tr, SSD_INNER), lambda i: (i, 0))
    return pl.pallas_call(
        body,
        grid=(s // tr,),
        in_specs=[row, row, pl.BlockSpec((1, SSD_INNER), lambda i: (0, 0))],
        out_specs=row,
        out_shape=jax.ShapeDtypeStruct((s, SSD_INNER), BF16),
        compiler_params=_params(("parallel",)),
        name="ssd_gate_fwd",
    )(y, zx, norm_w)


def _ssd_gate_bwd(dgn, y, zx, norm_w):
    s = y.shape[0]
    tr = _pick(s, (256, 128))
    gw = SSD_GROUP_W

    def body(dg_ref, y_ref, z_ref, w_ref, dy_ref, dz_ref, dw_ref):
        parts = []
        for g in range(SSD_GROUPS):
            sl = slice(g * gw, (g + 1) * gw)
            z = z_ref[:, sl].astype(F32)
            yv = y_ref[:, sl]
            sg = _sigmoid(z)
            sz = z * sg
            gv = yv * sz
            r = lax.rsqrt(jnp.mean(gv * gv, axis=-1, keepdims=True) + LN_EPS)
            gn = gv * r
            dout = dg_ref[:, sl].astype(F32)
            parts.append(jnp.sum(dout * gn, axis=0, keepdims=True))
            dgn_ = dout * w_ref[:, sl]
            dgv = r * (dgn_ - gn * jnp.mean(dgn_ * gn, axis=-1, keepdims=True))
            dy_ref[:, sl] = dgv * sz
            dz_ref[:, sl] = (dgv * yv * (sg * (1.0 + z * (1.0 - sg)))).astype(dz_ref.dtype)
        part = jnp.concatenate(parts, axis=1)

        @pl.when(pl.program_id(0) == 0)
        def _():
            dw_ref[...] = part

        @pl.when(pl.program_id(0) > 0)
        def _():
            dw_ref[...] += part

    row = pl.BlockSpec((tr, SSD_INNER), lambda i: (i, 0))
    vec = pl.BlockSpec((1, SSD_INNER), lambda i: (0, 0))
    return pl.pallas_call(
        body,
        grid=(s // tr,),
        in_specs=[row, row, row, vec],
        out_specs=[row, row, vec],
        out_shape=[jax.ShapeDtypeStruct((s, SSD_INNER), F32), jax.ShapeDtypeStruct((s, SSD_ZX), BF16),
                   jax.ShapeDtypeStruct((1, SSD_INNER), F32)],
        compiler_params=_params(("arbitrary",)),
        name="ssd_gate_bwd",
    )(dgn, y, zx, norm_w)


INV_SQRT2 = 1.0 / math.sqrt(2.0)
INV_SQRT2PI = 1.0 / math.sqrt(2.0 * math.pi)


def _gelu(x):
    return 0.5 * x * (1.0 + lax.erf(x * INV_SQRT2))


def _gelu_grad(x):
    return 0.5 * (1.0 + lax.erf(x * INV_SQRT2)) + x * INV_SQRT2PI * jnp.exp(-0.5 * x * x)


def _gmlp_act_fwd(pre, b_in, ln_w, ln_b):
    s = pre.shape[0]
    tr = _pick(s, (256, 128))
    n = GMLP_INNER

    def body(p_ref, b_ref, w_ref, lb_ref, u_ref, v_ref):
        u_ref[...] = _gelu(p_ref[:, 0:n].astype(F32) + b_ref[:, 0:n]).astype(u_ref.dtype)
        hv = _gelu(p_ref[:, n:2 * n].astype(F32) + b_ref[:, n:2 * n])
        mu = jnp.mean(hv, axis=-1, keepdims=True)
        xc = hv - mu
        r = lax.rsqrt(jnp.mean(xc * xc, axis=-1, keepdims=True) + LN_EPS)
        v_ref[...] = (xc * r * w_ref[...] + lb_ref[...]).astype(v_ref.dtype)

    half = pl.BlockSpec((tr, n), lambda i: (i, 0))
    vec = pl.BlockSpec((1, n), lambda i: (0, 0))
    return pl.pallas_call(
        body,
        grid=(s // tr,),
        in_specs=[pl.BlockSpec((tr, 2 * n), lambda i: (i, 0)), pl.BlockSpec((1, 2 * n), lambda i: (0, 0)), vec, vec],
        out_specs=[half, half],
        out_shape=[jax.ShapeDtypeStruct((s, n), BF16), jax.ShapeDtypeStruct((s, n), BF16)],
        compiler_params=_params(("parallel",)),
        name="gmlp_act_fwd",
    )(pre, b_in, ln_w, ln_b)


def _gmlp_act_bwd(pre, b_in, ln_w, du, dv):
    s = pre.shape[0]
    tr = _pick(s, (256, 128))
    n = GMLP_INNER

    def body(p_ref, b_ref, w_ref, du_ref, dv_ref, dp_ref, db_ref, dw_ref, dlb_ref):
        xu = p_ref[:, 0:n].astype(F32) + b_ref[:, 0:n]
        dpu = du_ref[...].astype(F32) * _gelu_grad(xu)
        xv = p_ref[:, n:2 * n].astype(F32) + b_ref[:, n:2 * n]
        hv = _gelu(xv)
        mu = jnp.mean(hv, axis=-1, keepdims=True)
        xc = hv - mu
        r = lax.rsqrt(jnp.mean(xc * xc, axis=-1, keepdims=True) + LN_EPS)
        vh = xc * r
        dvv = dv_ref[...].astype(F32)
        dvh = dvv * w_ref[...]
        dh = r * (dvh - jnp.mean(dvh, axis=-1, keepdims=True) - vh * jnp.mean(dvh * vh, axis=-1, keepdims=True))
        dpv = dh * _gelu_grad(xv)
        dp_ref[:, 0:n] = dpu.astype(dp_ref.dtype)
        dp_ref[:, n:2 * n] = dpv.astype(dp_ref.dtype)
        pb = jnp.concatenate([jnp.sum(dpu, axis=0, keepdims=True), jnp.sum(dpv, axis=0, keepdims=True)], axis=1)
        pw = jnp.sum(dvv * vh, axis=0, keepdims=True)
        plb = jnp.sum(dvv, axis=0, keepdims=True)

        @pl.when(pl.program_id(0) == 0)
        def _():
            db_ref[...] = pb
            dw_ref[...] = pw
            dlb_ref[...] = plb

        @pl.when(pl.program_id(0) > 0)
        def _():
            db_ref[...] += pb
            dw_ref[...] += pw
            dlb_ref[...] += plb

    half = pl.BlockSpec((tr, n), lambda i: (i, 0))
    full = pl.BlockSpec((tr, 2 * n), lambda i: (i, 0))
    vec = pl.BlockSpec((1, n), lambda i: (0, 0))
    vec2 = pl.BlockSpec((1, 2 * n), lambda i: (0, 0))
    return pl.pallas_call(
        body,
        grid=(s // tr,),
        in_specs=[full, vec2, vec, half, half],
        out_specs=[full, vec2, vec, vec],
        out_shape=[jax.ShapeDtypeStruct((s, 2 * n), BF16), jax.ShapeDtypeStruct((1, 2 * n), F32),
                   jax.ShapeDtypeStruct((1, n), F32), jax.ShapeDtypeStruct((1, n), F32)],
        compiler_params=_params(("arbitrary",)),
        name="gmlp_act_bwd",
    )(pre, b_in, ln_w, du, dv)


def _gmlp_mix_fwd(u, v, w_s, b_st):
    s = u.shape[0]
    gd = GMLP_INNER // GMLP_GROUPS

    def body(u_ref, v_ref, w_ref, b_ref, o_ref):
        li = lax.broadcasted_iota(jnp.int32, (CHUNK, CHUNK), 0)
        si = lax.broadcasted_iota(jnp.int32, (CHUNK, CHUNK), 1)
        tril = li >= si
        for g in range(GMLP_GROUPS):
            sl = slice(g * gd, (g + 1) * gd)
            wm = jnp.where(tril, w_ref[g], 0.0).astype(BF16)
            mixed = _dot(wm, v_ref[:, sl], NN) + b_ref[:, g:g + 1]
            o_ref[:, sl] = (u_ref[:, sl].astype(F32) * mixed).astype(o_ref.dtype)

    row = pl.BlockSpec((CHUNK, GMLP_INNER), lambda c: (c, 0))
    return pl.pallas_call(
        body,
        grid=(s // CHUNK,),
        in_specs=[row, row, pl.BlockSpec((GMLP_GROUPS, CHUNK, CHUNK), lambda c: (0, 0, 0)),
                  pl.BlockSpec((CHUNK, LANES), lambda c: (0, 0))],
        out_specs=row,
        out_shape=jax.ShapeDtypeStruct((s, GMLP_INNER), BF16),
        compiler_params=_params(("parallel",)),
        name="gmlp_mix_fwd",
    )(u, v, w_s, b_st)


def _gmlp_mix_bwd(dgated, u, v, w_s, b_st):
    s = u.shape[0]
    nc = s // CHUNK
    gd = GMLP_INNER // GMLP_GROUPS

    def body(dg_ref, u_ref, v_ref, w_ref, b_ref, du_ref, dv_ref, dw_ref, db_ref):
        c = pl.program_id(0)

        @pl.when(c == 0)
        def _():
            dw_ref[...] = jnp.zeros_like(dw_ref)
            db_ref[...] = jnp.zeros_like(db_ref)

        li = lax.broadcasted_iota(jnp.int32, (CHUNK, CHUNK), 0)
        si = lax.broadcasted_iota(jnp.int32, (CHUNK, CHUNK), 1)
        tril = li >= si
        lane = lax.broadcasted_iota(jnp.int32, (CHUNK, LANES), 1)
        dbacc = jnp.zeros((CHUNK, LANES), F32)
        for g in range(GMLP_GROUPS):
            sl = slice(g * gd, (g + 1) * gd)
            wm = jnp.where(tril, w_ref[g], 0.0).astype(BF16)
            vg = v_ref[:, sl]
            mixed = _dot(wm, vg, NN) + b_ref[:, g:g + 1]
            dgv = dg_ref[:, sl].astype(F32)
            du_ref[:, sl] = (dgv * mixed).astype(du_ref.dtype)
            dm = dgv * u_ref[:, sl].astype(F32)
            dm_b = dm.astype(BF16)
            dv_ref[:, sl] = _dot(wm, dm_b, TN).astype(dv_ref.dtype)
            dw_ref[g] += jnp.where(tril, _dot(dm_b, vg, NT), 0.0)
            dbacc = dbacc + jnp.where(lane == g, jnp.sum(dm, axis=1, keepdims=True), 0.0)
        db_ref[...] += dbacc

    row = pl.BlockSpec((CHUNK, GMLP_INNER), lambda c: (c, 0))
    wspec = pl.BlockSpec((GMLP_GROUPS, CHUNK, CHUNK), lambda c: (0, 0, 0))
    bspec = pl.BlockSpec((CHUNK, LANES), lambda c: (0, 0))
    return pl.pallas_call(
        body,
        grid=(nc,),
        in_specs=[row, row, row, wspec, bspec],
        out_specs=[row, row, wspec, bspec],
        out_shape=[jax.ShapeDtypeStruct((s, GMLP_INNER), BF16), jax.ShapeDtypeStruct((s, GMLP_INNER), BF16),
                   jax.ShapeDtypeStruct((GMLP_GROUPS, CHUNK, CHUNK), F32), jax.ShapeDtypeStruct((CHUNK, LANES), F32)],
        compiler_params=_params(("arbitrary",)),
        name="gmlp_mix_bwd",
    )(dgated, u, v, w_s, b_st)


def _swiglu_fwd(gu):
    s = gu.shape[0]
    f = FFN_DIM
    tr = _pick(s, (512, 256, 128))

    def body(gu_ref, o_ref):
        gt = gu_ref[:, 0:f].astype(F32)
        o_ref[...] = (gt * _sigmoid(gt) * gu_ref[:, f:2 * f].astype(F32)).astype(o_ref.dtype)

    return pl.pallas_call(
        body,
        grid=(s // tr,),
        in_specs=[pl.BlockSpec((tr, 2 * f), lambda i: (i, 0))],
        out_specs=pl.BlockSpec((tr, f), lambda i: (i, 0)),
        out_shape=jax.ShapeDtypeStruct((s, f), BF16),
        compiler_params=_params(("parallel",)),
        name="swiglu_fwd",
    )(gu)


def _swiglu_bwd(gu, dhid):
    s = gu.shape[0]
    f = FFN_DIM
    tr = _pick(s, (512, 256, 128))

    def body(gu_ref, dh_ref, dgu_ref):
        gt = gu_ref[:, 0:f].astype(F32)
        up = gu_ref[:, f:2 * f].astype(F32)
        dh = dh_ref[...].astype(F32)
        sg = _sigmoid(gt)
        dgu_ref[:, 0:f] = (dh * up * (sg * (1.0 + gt * (1.0 - sg)))).astype(dgu_ref.dtype)
        dgu_ref[:, f:2 * f] = (dh * gt * sg).astype(dgu_ref.dtype)

    wide = pl.BlockSpec((tr, 2 * f), lambda i: (i, 0))
    return pl.pallas_call(
        body,
        grid=(s // tr,),
        in_specs=[wide, pl.BlockSpec((tr, f), lambda i: (i, 0))],
        out_specs=wide,
        out_shape=jax.ShapeDtypeStruct((s, 2 * f), BF16),
        compiler_params=_params(("parallel",)),
        name="swiglu_bwd",
    )(gu, dhid)


def _ple_fwd(pe, gl, h, ple_norm):
    s, d = h.shape
    tr = _pick(s, (512, 256, 128))

    def body(pe_ref, gl_ref, h_ref, w_ref, o_ref):
        pe_ = pe_ref[...]
        r = lax.rsqrt(jnp.mean(pe_ * pe_, axis=-1, keepdims=True) + RMS_EPS)
        o_ref[...] = h_ref[...] + _sigmoid(gl_ref[...]) * (pe_ * r * w_ref[...])

    row = pl.BlockSpec((tr, d), lambda i: (i, 0))
    return pl.pallas_call(
        body,
        grid=(s // tr,),
        in_specs=[row, row, row, pl.BlockSpec((1, d), lambda i: (0, 0))],
        out_specs=row,
        out_shape=jax.ShapeDtypeStruct((s, d), F32),
        compiler_params=_params(("parallel",)),
        name="ple_fwd",
    )(pe, gl, h, ple_norm)


def _ple_bwd(dh, pe, gl, ple_norm):
    s, d = dh.shape
    tr = _pick(s, (512, 256, 128))

    def body(dh_ref, pe_ref, gl_ref, w_ref, dgl_ref, dpe_ref, dw_ref):
        pe_ = pe_ref[...]
        dhv = dh_ref[...]
        r = lax.rsqrt(jnp.mean(pe_ * pe_, axis=-1, keepdims=True) + RMS_EPS)
        pn = pe_ * r
        gate = _sigmoid(gl_ref[...])
        dgl_ref[...] = (dhv * (pn * w_ref[...]) * gate * (1.0 - gate)).astype(dgl_ref.dtype)
        de = dhv * gate
        dxh = de * w_ref[...]
        dpe_ref[...] = (r * (dxh - pn * jnp.mean(dxh * pn, axis=-1, keepdims=True))).astype(dpe_ref.dtype)
        part = jnp.sum(de * pn, axis=0, keepdims=True)

        @pl.when(pl.program_id(0) == 0)
        def _():
            dw_ref[...] = part

        @pl.when(pl.program_id(0) > 0)
        def _():
            dw_ref[...] += part

    row = pl.BlockSpec((tr, d), lambda i: (i, 0))
    vec = pl.BlockSpec((1, d), lambda i: (0, 0))
    return pl.pallas_call(
        body,
        grid=(s // tr,),
        in_specs=[row, row, row, vec],
        out_specs=[row, row, vec],
        out_shape=[jax.ShapeDtypeStruct((s, d), BF16), jax.ShapeDtypeStruct((s, d), BF16),
                   jax.ShapeDtypeStruct((1, d), F32)],
        compiler_params=_params(("arbitrary",)),
        name="ple_bwd",
    )(dh, pe, gl, ple_norm)


def _loss_head(h, w, target):
    s, d = h.shape
    tr = _pick(s, (512, 256, 128))

    def body(h_ref, w_ref, t_ref, l_ref, dh_ref, dw_ref):
        hv = h_ref[...]
        r = lax.rsqrt(jnp.mean(hv * hv, axis=-1, keepdims=True) + RMS_EPS)
        hn = hv * r
        diff = hn * w_ref[...] - t_ref[...]
        lpart = jnp.zeros((8, LANES), F32) + (0.5 / d) * jnp.sum(jnp.sum(diff * diff, axis=1, keepdims=True), axis=0, keepdims=True)
        dy = diff * (1.0 / d)
        dxh = dy * w_ref[...]
        dh_ref[...] = r * (dxh - hn * jnp.mean(dxh * hn, axis=-1, keepdims=True))
        part = jnp.sum(dy * hn, axis=0, keepdims=True)

        @pl.when(pl.program_id(0) == 0)
        def _():
            l_ref[...] = lpart
            dw_ref[...] = part

        @pl.when(pl.program_id(0) > 0)
        def _():
            l_ref[...] += lpart
            dw_ref[...] += part

    row = pl.BlockSpec((tr, d), lambda i: (i, 0))
    vec = pl.BlockSpec((1, d), lambda i: (0, 0))
    return pl.pallas_call(
        body,
        grid=(s // tr,),
        in_specs=[row, vec, row],
        out_specs=[pl.BlockSpec((8, LANES), lambda i: (0, 0)), row, vec],
        out_shape=[jax.ShapeDtypeStruct((8, LANES), F32), jax.ShapeDtypeStruct((s, d), F32),
                   jax.ShapeDtypeStruct((1, d), F32)],
        compiler_params=_params(("arbitrary",)),
        name="loss_head",
    )(h, w, target)


PER_LAYER = ("norm_mix", "norm_ffn", "ffn_w_gu", "ffn_w_down", "ple_w_proj", "ple_norm", "ple_gate_norm", "ple_w_gate")


def _pad_lanes(v):
    return jnp.pad(v.astype(F32), (0, LANES - v.shape[0]))[None, :]


def _kernel_layouts(full):
    w = {}
    for k in ("norm_mix", "norm_ffn", "ple_norm", "ple_gate_norm", "ssd_conv_b", "ssd_norm_w", "gmlp_b_in", "gmlp_ln_w",
              "gmlp_ln_b", "gmlp_w_s"):
        w[k] = [full[k][i].astype(F32) for i in range(full[k].shape[0])]
    w["final_norm"] = full["final_norm"].astype(F32)
    n_ssd = full["ssd_w_out"].shape[0]
    if "ssd_w_in" in full:
        w["ssd_w_zx"] = [full["ssd_w_in"][j][:, :SSD_ZX].astype(BF16) for j in range(n_ssd)]
        w["ssd_w_dt"] = [jnp.pad(full["ssd_w_in"][j][:, SSD_ZX:].astype(BF16), ((0, 0), (0, LANES - SSD_HEADS)))
                         for j in range(n_ssd)]
        w["ffn_w_gu"] = [jnp.concatenate([full["ffn_w_gate"][i], full["ffn_w_up"][i]], axis=1).astype(BF16)
                         for i in range(DEPTH)]
    else:
        for k in ("ssd_w_zx", "ssd_w_dt", "ffn_w_gu"):
            w[k] = full[k]
    w["ssd_conv_w"] = [jnp.pad(full["ssd_conv_w"][j].astype(F32), ((0, 8 - CONV_K), (0, 0))) for j in range(n_ssd)]
    for k in ("ssd_dt_bias", "ssd_a_log", "ssd_d"):
        w[k] = [_pad_lanes(full[k][j]) for j in range(n_ssd)]
    w["ssd_w_out"] = [full["ssd_w_out"][j].astype(BF16) for j in range(n_ssd)]
    n_g = full["gmlp_w_in"].shape[0]
    w["gmlp_w_in"] = [full["gmlp_w_in"][j].astype(BF16) for j in range(n_g)]
    w["gmlp_w_out"] = [full["gmlp_w_out"][j].astype(BF16) for j in range(n_g)]
    w["gmlp_b_st"] = [jnp.pad(full["gmlp_b_s"][j].astype(F32).T, ((0, 0), (0, LANES - GMLP_GROUPS))) for j in range(n_g)]
    w["ffn_w_down"] =[full["ffn_w_down"][i].astype(BF16) for i in range(DEPTH)]
    w["ple_w_proj"] = [full["ple_w_proj"][i].astype(BF16) for i in range(DEPTH)]
    w["ple_w_gate"] = [full["ple_w_gate"][i].astype(BF16) for i in range(DEPTH)]
    return w


def _reference_layouts(g, wide=True):
    out = {}
    for k in ("norm_mix", "norm_ffn", "ple_norm", "ple_gate_norm", "ssd_conv_b", "ssd_norm_w", "gmlp_b_in", "gmlp_ln_w",
              "gmlp_ln_b", "gmlp_w_s", "ssd_conv_w", "ssd_dt_bias", "ssd_a_log", "ssd_d", "ssd_w_out", "gmlp_w_in",
              "gmlp_w_out", "ffn_w_down", "ple_w_proj", "ple_w_gate"):
        out[k] = jnp.stack(g[k])
    out["final_norm"] = g["final_norm"]
    out["gmlp_b_s"] = jnp.stack([b[:, :GMLP_GROUPS].T for b in g["gmlp_b_st"]])
    if wide:
        out["ssd_w_in"] = jnp.stack([jnp.concatenate([zx, dt[:, :SSD_HEADS]], axis=1)
                                     for zx, dt in zip(g["ssd_w_zx"], g["ssd_w_dt"])])
        out["ffn_w_gate"] = jnp.stack([gu[:, :FFN_DIM] for gu in g["ffn_w_gu"]])
        out["ffn_w_up"] = jnp.stack([gu[:, FFN_DIM:] for gu in g["ffn_w_gu"]])
    return out


RELAYOUT_ROWS = 128
SSD_SHARD = SSD_IN_DIM // N_DEV
FFN_SHARD = FFN_DIM // N_DEV


def _cat_ssd_in(gathered):
    _, nl, rows, n = gathered.shape
    tr = RELAYOUT_ROWS

    def body(g_ref, *o_refs):
        for j in range(nl):
            full = jnp.concatenate([g_ref[d, j] for d in range(N_DEV)], axis=1)
            o_refs[2 * j][...] = full[:, :SSD_ZX]
            o_refs[2 * j + 1][...] = jnp.concatenate(
                [full[:, SSD_ZX:], jnp.zeros((tr, LANES - SSD_HEADS), full.dtype)], axis=1)

    outs = pl.pallas_call(
        body, grid=(rows // tr,),
        in_specs=[pl.BlockSpec((N_DEV, nl, tr, n), lambda i: (0, 0, i, 0))],
        out_specs=[pl.BlockSpec((tr, SSD_ZX), lambda i: (i, 0)), pl.BlockSpec((tr, LANES), lambda i: (i, 0))] * nl,
        out_shape=[jax.ShapeDtypeStruct((rows, SSD_ZX), BF16), jax.ShapeDtypeStruct((rows, LANES), BF16)] * nl,
        compiler_params=_params(("parallel",)), name="cat_ssd_in",
    )(gathered)
    return [outs[2 * j] for j in range(nl)], [outs[2 * j + 1] for j in range(nl)]


def _split_ssd_in(dzx_list, ddt_list):
    nl = len(dzx_list)
    rows = dzx_list[0].shape[0]
    tr = RELAYOUT_ROWS

    def body(*refs):
        o_ref = refs[2 * nl]
        for j in range(nl):
            full = jnp.concatenate([refs[2 * j][...], refs[2 * j + 1][:, 0:SSD_HEADS]], axis=1)
            for d in range(N_DEV):
                o_ref[d, j] = full[:, d * SSD_SHARD:(d + 1) * SSD_SHARD].astype(o_ref.dtype)

    ins = []
    for j in range(nl):
        ins += [dzx_list[j], ddt_list[j]]
    return pl.pallas_call(
        body, grid=(rows // tr,),
        in_specs=[pl.BlockSpec((tr, SSD_ZX), lambda i: (i, 0)), pl.BlockSpec((tr, LANES), lambda i: (i, 0))] * nl,
        out_specs=pl.BlockSpec((N_DEV, nl, tr, SSD_SHARD), lambda i: (0, 0, i, 0)),
        out_shape=jax.ShapeDtypeStruct((N_DEV, nl, rows, SSD_SHARD), BF16),
        compiler_params=_params(("parallel",)), name="split_ssd_in",
    )(*ins)


def _cat_ffn(g_gate, g_up):
    _, nl, rows, n = g_gate.shape
    tr = RELAYOUT_ROWS

    def body(gg_ref, gu_ref, *o_refs):
        for i in range(nl):
            o_refs[i][...] = jnp.concatenate([gg_ref[d, i] for d in range(N_DEV)] + [gu_ref[d, i] for d in range(N_DEV)],
                                             axis=1)

    blk = pl.BlockSpec((N_DEV, nl, tr, n), lambda i: (0, 0, i, 0))
    outs = pl.pallas_call(
        body, grid=(rows // tr,), in_specs=[blk, blk],
        out_specs=[pl.BlockSpec((tr, 2 * FFN_DIM), lambda i: (i, 0))] * nl,
        out_shape=[jax.ShapeDtypeStruct((rows, 2 * FFN_DIM), BF16)] * nl,
        compiler_params=_params(("parallel",)), name="cat_ffn",
    )(g_gate, g_up)
    return list(outs)


def _split_ffn(dgu_list):
    nl = len(dgu_list)
    rows = dgu_list[0].shape[0]
    tr = RELAYOUT_ROWS

    def body(*refs):
        og_ref, ou_ref = refs[nl], refs[nl + 1]
        for i in range(nl):
            full = refs[i][...]
            for d in range(N_DEV):
                og_ref[d, i] = full[:, d * FFN_SHARD:(d + 1) * FFN_SHARD].astype(og_ref.dtype)
                ou_ref[d, i] = full[:, FFN_DIM + d * FFN_SHARD:FFN_DIM + (d + 1) * FFN_SHARD].astype(ou_ref.dtype)

    blk = pl.BlockSpec((N_DEV, nl, tr, FFN_SHARD), lambda i: (0, 0, i, 0))
    sds = jax.ShapeDtypeStruct((N_DEV, nl, rows, FFN_SHARD), BF16)
    return pl.pallas_call(
        body, grid=(rows // tr,),
        in_specs=[pl.BlockSpec((tr, 2 * FFN_DIM), lambda i: (i, 0))] * nl,
        out_specs=[blk, blk], out_shape=[sds, sds],
        compiler_params=_params(("parallel",)), name="split_ffn",
    )(*dgu_list)


def _local_step(x, p, target, w):
    saved = []
    h = x
    for i in range(DEPTH):
        j = i // 2
        sv = {"h0": h}
        hn = _rms_fwd(h, w["norm_mix"][i][None, :])
        sv["hn"] = hn
        if i % 2 == 0:
            zx = _mm(hn, w["ssd_w_zx"][j], "nn", BF16)
            dtp = _mm(hn, w["ssd_w_dt"][j], "nn", F32)
            xbc = _ssd_conv_fwd(zx, w["ssd_conv_w"][j], w["ssd_conv_b"][j][None, :])
            y, prev = _ssd_scan_fwd(xbc, dtp, w["ssd_dt_bias"][j], w["ssd_a_log"][j], w["ssd_d"][j])
            gn = _ssd_gate_fwd(y, zx, w["ssd_norm_w"][j][None, :])
            h = _mm(gn, w["ssd_w_out"][j], "nn", F32, add=h)
            sv.update(zx=zx, dtp=dtp, xbc=xbc, y=y, prev=prev, gn=gn)
        else:
            pre = _mm(hn, w["gmlp_w_in"][j], "nn", BF16)
            u, v = _gmlp_act_fwd(pre, w["gmlp_b_in"][j][None, :], w["gmlp_ln_w"][j][None, :], w["gmlp_ln_b"][j][None, :])
            gated = _gmlp_mix_fwd(u, v, w["gmlp_w_s"][j], w["gmlp_b_st"][j])
            h = _mm(gated, w["gmlp_w_out"][j], "nn", F32, add=h)
            sv.update(pre=pre, u=u, v=v, gated=gated)
        sv["h1"] = h
        un = _rms_fwd(h, w["norm_ffn"][i][None, :])
        gu = _mm(un, w["ffn_w_gu"][i], "nn", BF16)
        hid = _swiglu_fwd(gu)
        h = _mm(hid, w["ffn_w_down"][i], "nn", F32, add=h)
        sv.update(un=un, gu=gu, hid=hid, h2=h)
        pe = _mm(p[i], w["ple_w_proj"][i], "nn", F32)
        hg = _rms_fwd(h, w["ple_gate_norm"][i][None, :])
        gl = _mm(hg, w["ple_w_gate"][i], "nn", F32)
        h = _ple_fwd(pe, gl, h, w["ple_norm"][i][None, :])
        sv.update(pe=pe, hg=hg, gl=gl)
        saved.append(sv)

    lpart, dh, d_final = _loss_head(h, w["final_norm"][None, :], target)
    g = {k: [None] * (DEPTH if k in PER_LAYER else DEPTH // 2) for k in w if k != "final_norm"}
    g["final_norm"] = d_final[0]

    for i in reversed(range(DEPTH)):
        j = i // 2
        sv = saved[i]
        dgl, dpe, d_ple_norm = _ple_bwd(dh, sv["pe"], sv["gl"], w["ple_norm"][i][None, :])
        g["ple_norm"][i] = d_ple_norm[0]
        g["ple_w_gate"][i] = _mm(sv["hg"], dgl, "tn", F32)
        g["ple_w_proj"][i] = _mm(p[i], dpe, "tn", F32)
        dhg = _mm(dgl, w["ple_w_gate"][i], "nt", F32)
        dh, d_gate_norm = _rms_bwd(dhg, sv["h2"], w["ple_gate_norm"][i][None, :], dh)
        g["ple_gate_norm"][i] = d_gate_norm[0]
        dhid = _mm(dh, w["ffn_w_down"][i], "nt", BF16)
        g["ffn_w_down"][i] = _mm(sv["hid"], dh, "tn", F32)
        dgu = _swiglu_bwd(sv["gu"], dhid)
        g["ffn_w_gu"][i] = _mm(sv["un"], dgu, "tn", F32)
        dun = _mm(dgu, w["ffn_w_gu"][i], "nt", F32)
        dh, d_norm_ffn = _rms_bwd(dun, sv["h1"], w["norm_ffn"][i][None, :], dh)
        g["norm_ffn"][i] = d_norm_ffn[0]
        if i % 2 == 0:
            dgn = _mm(dh, w["ssd_w_out"][j], "nt", F32)
            g["ssd_w_out"][j] = _mm(sv["gn"], dh, "tn", F32)
            dy, dzx, d_norm_w = _ssd_gate_bwd(dgn, sv["y"], sv["zx"], w["ssd_norm_w"][j][None, :])
            g["ssd_norm_w"][j] = d_norm_w[0]
            dxbc, ddtp, d_bias, d_alog, d_d = _ssd_scan_bwd(sv["xbc"], sv["dtp"], sv["prev"], dy, w["ssd_dt_bias"][j],
                                                            w["ssd_a_log"][j], w["ssd_d"][j])
            g["ssd_dt_bias"][j] = d_bias[0, :SSD_HEADS]
            g["ssd_a_log"][j] = d_alog[0, :SSD_HEADS]
            g["ssd_d"][j] = d_d[0, :SSD_HEADS]
            dzx, d_conv_w, d_conv_b = _ssd_conv_bwd(sv["zx"], dxbc, w["ssd_conv_w"][j], w["ssd_conv_b"][j][None, :], dzx)
            g["ssd_conv_w"][j] = d_conv_w[:CONV_K]
            g["ssd_conv_b"][j] = d_conv_b[0]
            g["ssd_w_zx"][j] = _mm(sv["hn"], dzx, "tn", F32)
            g["ssd_w_dt"][j] = _mm(sv["hn"], ddtp, "tn", F32)
            dhn = _mm(dzx, w["ssd_w_zx"][j], "nt", F32)
            dhn = _mm(ddtp, w["ssd_w_dt"][j], "nt", F32, add=dhn)
        else:
            dgated = _mm(dh, w["gmlp_w_out"][j], "nt", BF16)
            g["gmlp_w_out"][j] = _mm(sv["gated"], dh, "tn", F32)
            du, dv, d_ws, d_bst = _gmlp_mix_bwd(dgated, sv["u"], sv["v"], w["gmlp_w_s"][j], w["gmlp_b_st"][j])
            g["gmlp_w_s"][j] = d_ws
            g["gmlp_b_st"][j] = d_bst
            dpre, d_bin, d_lnw, d_lnb = _gmlp_act_bwd(sv["pre"], w["gmlp_b_in"][j][None, :], w["gmlp_ln_w"][j][None, :],
                                                     du, dv)
            g["gmlp_b_in"][j] = d_bin[0]
            g["gmlp_ln_w"][j] = d_lnw[0]
            g["gmlp_ln_b"][j] = d_lnb[0]
            g["gmlp_w_in"][j] = _mm(sv["hn"], dpre, "tn", F32)
            dhn = _mm(dpre, w["gmlp_w_in"][j], "nt", F32)
        dh, d_norm_mix = _rms_bwd(dhn, sv["h0"], w["norm_mix"][i][None, :], dh)
        g["norm_mix"][i] = d_norm_mix[0]
    return lpart[0, 0], dh, g


PACK_COLS = 1024
ANY = pl.BlockSpec(memory_space=pl.ANY)


def _mesh_pos():
    return lax.axis_index("x"), lax.axis_index("y"), lax.axis_index("c")


def _all_gather(xs_list, name):
    n = len(xs_list)

    def body(*refs):
        x_refs, out_refs = refs[:n], refs[n:2 * n]
        send_sems, recv_sems, local_sems = refs[2 * n:]
        x, y, c = _mesh_pos()
        me, sibling = (x, y, c), (x, y, 1 - c)
        chips = [(1 - x, y), (x, 1 - y), (1 - x, 1 - y)]

        def copy(a, k, block, to, from_input=False):
            px, py, pc = block
            dst = out_refs[a].at[4 * px + 2 * py + pc]
            return pltpu.make_async_remote_copy(
                src_ref=x_refs[a] if from_input else dst, dst_ref=dst,
                send_sem=send_sems.at[7 * a + k], recv_sem=recv_sems.at[7 * a + k], device_id=to,
                device_id_type=MESH_ID)

        mine = [pltpu.make_async_copy(x_refs[a], out_refs[a].at[4 * x + 2 * y + c], local_sems.at[a]) for a in range(n)]
        for cp in mine:
            cp.start()
        first = []
        for a in range(n):
            first += [copy(a, 1 + j, me, (*chip, c), from_input=True) for j, chip in enumerate(chips)]
            first.append(copy(a, 0, me, sibling, from_input=True))
        for cp in first:
            cp.start()
        passed = []
        for a in range(n):
            for j, chip in enumerate(chips):
                copy(a, 1 + j, (*chip, c), me).wait_recv()
                fwd = copy(a, 4 + j, (*chip, c), sibling)
                fwd.start()
                passed.append(fwd)
        for a in range(n):
            copy(a, 0, sibling, me).wait_recv()
            for j, chip in enumerate(chips):
                copy(a, 4 + j, (*chip, 1 - c), me).wait_recv()
        for cp in first + passed:
            cp.wait_send()
        for cp in mine:
            cp.wait()

    outs = pl.pallas_call(
        body,
        out_shape=[jax.ShapeDtypeStruct((N_DEV,) + t.shape, t.dtype) for t in xs_list],
        in_specs=[ANY] * n,
        out_specs=[ANY] * n,
        scratch_shapes=[pltpu.SemaphoreType.DMA((7 * n,)), pltpu.SemaphoreType.DMA((7 * n,)),
                        pltpu.SemaphoreType.DMA((n,))],
        name=name,
    )(*xs_list)
    return list(outs)


def _exchange_sibling(send_list):
    n = len(send_list)

    def body(*refs):
        s_refs, land_refs = refs[:n], refs[n:2 * n]
        send_sems, recv_sems = refs[2 * n:]
        x, y, c = _mesh_pos()
        cps = [pltpu.make_async_remote_copy(src_ref=s_refs[a], dst_ref=land_refs[a], send_sem=send_sems.at[a],
                                            recv_sem=recv_sems.at[a], device_id=(x, y, 1 - c), device_id_type=MESH_ID)
               for a in range(n)]
        for cp in cps:
            cp.start()
        for cp in cps:
            cp.wait()

    outs = pl.pallas_call(
        body,
        out_shape=[jax.ShapeDtypeStruct(t.shape, t.dtype) for t in send_list],
        in_specs=[ANY] * n,
        out_specs=[ANY] * n,
        scratch_shapes=[pltpu.SemaphoreType.DMA((n,)), pltpu.SemaphoreType.DMA((n,))],
        name="rs_exchange_sibling",
    )(*send_list)
    return list(outs)


def _exchange_chips(partial_list):
    n = len(partial_list)

    def body(*refs):
        p_refs, land_refs = refs[:n], refs[n:2 * n]
        send_sems, recv_sems = refs[2 * n:]
        x, y, c = _mesh_pos()
        chips = [(1 - x, y), (x, 1 - y), (1 - x, 1 - y)]
        cps = [pltpu.make_async_remote_copy(src_ref=p_refs[a].at[2 * cx + cy], dst_ref=land_refs[a].at[j],
                                            send_sem=send_sems.at[3 * a + j], recv_sem=recv_sems.at[3 * a + j],
                                            device_id=(cx, cy, c), device_id_type=MESH_ID)
               for a in range(n) for j, (cx, cy) in enumerate(chips)]
        for cp in cps:
            cp.start()
        for cp in cps:
            cp.wait()

    outs = pl.pallas_call(
        body,
        out_shape=[jax.ShapeDtypeStruct((3,) + t.shape[1:], t.dtype) for t in partial_list],
        in_specs=[ANY] * n,
        out_specs=[ANY] * n,
        scratch_shapes=[pltpu.SemaphoreType.DMA((3 * n,)), pltpu.SemaphoreType.DMA((3 * n,))],
        name="rs_exchange_chips",
    )(*partial_list)
    return list(outs)


def _sum_pairs(a, b):
    shape = a.shape
    a = a.reshape(shape[0], -1, shape[-1])
    b = b.reshape(a.shape)
    n, r, cdim = a.shape
    tr = _pick(r, (256, 128, 64))

    def body(a_ref, b_ref, o_ref):
        o_ref[...] = (a_ref[...].astype(F32) + b_ref[...].astype(F32)).astype(o_ref.dtype)

    blk = pl.BlockSpec((1, tr, cdim), lambda i, j: (i, j, 0))
    return pl.pallas_call(
        body, grid=(n, r // tr), in_specs=[blk, blk], out_specs=blk,
        out_shape=jax.ShapeDtypeStruct(a.shape, a.dtype),
        compiler_params=_params(("parallel", "parallel")), name="rs_sum_pairs",
    )(a, b).reshape(shape)


def _sum_final(own, land):
    shape = own.shape
    own = own.reshape(-1, shape[-1])
    land = land.reshape((3,) + own.shape)
    r, cdim = own.shape
    tr = _pick(r, (256, 128, 64))

    def body(o_ref, l_ref, out_ref):
        acc = o_ref[...].astype(F32)
        for j in range(3):
            acc = acc + l_ref[j].astype(F32)
        out_ref[...] = acc

    return pl.pallas_call(
        body, grid=(r // tr,),
        in_specs=[pl.BlockSpec((tr, cdim), lambda i: (i, 0)), pl.BlockSpec((3, tr, cdim), lambda i: (0, i, 0))],
        out_specs=pl.BlockSpec((tr, cdim), lambda i: (i, 0)),
        out_shape=jax.ShapeDtypeStruct((r, cdim), F32),
        compiler_params=_params(("parallel",)), name="rs_sum_final",
    )(own, land).reshape(shape)


def _sum_devices(gathered):
    n, r, cdim = gathered.shape
    tr = _pick(r, (64, 32, 16, 8))

    def body(g_ref, out_ref):
        acc = g_ref[0]
        for q in range(1, n):
            acc = acc + g_ref[q]
        out_ref[...] = acc

    return pl.pallas_call(
        body, grid=(r // tr,),
        in_specs=[pl.BlockSpec((n, tr, cdim), lambda i: (0, i, 0))],
        out_specs=pl.BlockSpec((tr, cdim), lambda i: (i, 0)),
        out_shape=jax.ShapeDtypeStruct((r, cdim), F32),
        compiler_params=_params(("parallel",)), name="sum_devices",
    )(gathered)


def _adamw(w, g, m, v):
    shape = w.shape
    cols = shape[-1]
    rows = w.size // cols
    tr = _pick(rows, (512, 256, 128, 64, 32, 16, 8))
    c1 = 1.0 - ADAM_B1 ** ADAM_STEP
    c2 = 1.0 - ADAM_B2 ** ADAM_STEP

    def body(w_ref, g_ref, m_ref, v_ref, d_ref, nm_ref, nv_ref):
        gv = g_ref[...]
        m2 = ADAM_B1 * m_ref[...] + (1.0 - ADAM_B1) * gv
        v2 = ADAM_B2 * v_ref[...] + (1.0 - ADAM_B2) * (gv * gv)
        d_ref[...] = -ADAM_LR * ((m2 / c1) / (jnp.sqrt(v2 / c2) + ADAM_EPS) + ADAM_WD * w_ref[...])
        nm_ref[...] = m2
        nv_ref[...] = v2

    blk = pl.BlockSpec((tr, cols), lambda i: (i, 0))
    sds = jax.ShapeDtypeStruct((rows, cols), F32)
    outs = pl.pallas_call(
        body, grid=(rows // tr,), in_specs=[blk] * 4, out_specs=[blk] * 3, out_shape=[sds] * 3,
        compiler_params=_params(("parallel",)), name=f"adamw_{rows}x{cols}",
    )(*(t.reshape(rows, cols) for t in (w, g, m, v)))
    return tuple(o.reshape(shape) for o in outs)


WEIGHTS = ("norm_mix", "norm_ffn", "ssd_w_in", "ssd_conv_w", "ssd_conv_b", "ssd_dt_bias", "ssd_a_log", "ssd_d",
           "ssd_norm_w", "ssd_w_out", "gmlp_w_in", "gmlp_b_in", "gmlp_ln_w", "gmlp_ln_b", "gmlp_w_s", "gmlp_b_s",
           "gmlp_w_out", "ffn_w_gate", "ffn_w_up", "ffn_w_down", "ple_w_proj", "ple_norm", "ple_gate_norm",
           "ple_w_gate", "final_norm")
ARG_NAMES = ("x", "p") + WEIGHTS + ("loss_target",) + tuple("m_" + n for n in WEIGHTS) + tuple("v_" + n for n in WEIGHTS)
SHARD_AXIS = {"ssd_w_in": 2, "ssd_conv_w": 2, "ssd_w_out": 1, "gmlp_w_in": 2, "gmlp_b_in": 1, "gmlp_ln_w": 1,
              "gmlp_ln_b": 1, "gmlp_w_out": 1, "ffn_w_gate": 2, "ffn_w_up": 2, "ffn_w_down": 1, "ple_w_proj": 2,
              "ple_w_gate": 1}
GATHER_BF16 = ("ssd_w_in", "ssd_w_out", "gmlp_w_in", "gmlp_w_out", "ffn_w_gate", "ffn_w_up", "ffn_w_down",
               "ple_w_proj", "ple_w_gate")
GATHER_F32 = ("ssd_conv_w", "gmlp_b_in", "gmlp_ln_w", "gmlp_ln_b")
SHARDED = GATHER_BF16 + GATHER_F32
WIDE = ("ssd_w_in", "ffn_w_gate", "ffn_w_up")
REPLICATED = tuple(n for n in WEIGHTS if n not in SHARD_AXIS)


def _pack(arrs, dtype, row_mult, lead=0):
    flat = jnp.concatenate([t.reshape(t.shape[:lead] + (-1,)).astype(dtype) for t in arrs], axis=lead)
    n = flat.shape[-1]
    unit = row_mult * PACK_COLS
    total = -(-n // unit) * unit
    flat = jnp.pad(flat, [(0, 0)] * lead + [(0, total - n)])
    return flat.reshape(flat.shape[:lead] + (total // PACK_COLS, PACK_COLS))


def _unpack(buf, names, shapes, lead=0):
    flat = buf.reshape(buf.shape[:lead] + (-1,))
    out, off = {}, 0
    for n in names:
        size = math.prod(shapes[n])
        out[n] = lax.slice_in_dim(flat, off, off + size, axis=lead).reshape(buf.shape[:lead] + tuple(shapes[n]))
        off += size
    return out


def _merge_shards(seg, ax):
    t = jnp.moveaxis(seg, 0, ax)
    return t.reshape(t.shape[:ax] + (t.shape[ax] * t.shape[ax + 1],) + t.shape[ax + 2:])


def _split_for_cores(gfull, ax, c):
    shp = gfull.shape
    t = gfull.reshape(shp[:ax] + (2, 2, 2, shp[ax] // N_DEV) + shp[ax + 1:])

    def take(core):
        u = lax.dynamic_index_in_dim(t, core, axis=ax + 2, keepdims=False)
        u = jnp.moveaxis(u, (ax, ax + 1), (0, 1))
        return u.reshape((4,) + u.shape[2:])

    return take(c), take(1 - c)


def kernel(x, p, norm_mix, norm_ffn, ssd_w_in, ssd_conv_w, ssd_conv_b, ssd_dt_bias, ssd_a_log, ssd_d,
           ssd_norm_w, ssd_w_out, gmlp_w_in, gmlp_b_in, gmlp_ln_w, gmlp_ln_b, gmlp_w_s, gmlp_b_s,
           gmlp_w_out, ffn_w_gate, ffn_w_up, ffn_w_down, ple_w_proj, ple_norm, ple_gate_norm, ple_w_gate,
           final_norm, loss_target, m_norm_mix, m_norm_ffn, m_ssd_w_in, m_ssd_conv_w, m_ssd_conv_b,
           m_ssd_dt_bias, m_ssd_a_log, m_ssd_d, m_ssd_norm_w, m_ssd_w_out, m_gmlp_w_in, m_gmlp_b_in,
           m_gmlp_ln_w, m_gmlp_ln_b, m_gmlp_w_s, m_gmlp_b_s, m_gmlp_w_out, m_ffn_w_gate, m_ffn_w_up,
           m_ffn_w_down, m_ple_w_proj, m_ple_norm, m_ple_gate_norm, m_ple_w_gate, m_final_norm, v_norm_mix,
           v_norm_ffn, v_ssd_w_in, v_ssd_conv_w, v_ssd_conv_b, v_ssd_dt_bias, v_ssd_a_log, v_ssd_d,
           v_ssd_norm_w, v_ssd_w_out, v_gmlp_w_in, v_gmlp_b_in, v_gmlp_ln_w, v_gmlp_ln_b, v_gmlp_w_s,
           v_gmlp_b_s, v_gmlp_w_out, v_ffn_w_gate, v_ffn_w_up, v_ffn_w_down, v_ple_w_proj, v_ple_norm,
           v_ple_gate_norm, v_ple_w_gate, v_final_norm):
    given = locals()
    a = {n: given[n] for n in ARG_NAMES}
    mx, my, c = _mesh_pos()
    xs = a["x"][0]
    ps = a["p"][:, 0]
    target = a["loss_target"][0]
    shard_shapes = {n: a[n].shape for n in WEIGHTS}

    full = {n: a[n] for n in REPLICATED}
    packed = tuple(n for n in GATHER_BF16 if n not in WIDE)
    rs_packed = tuple(n for n in SHARDED if n not in WIDE)
    got = _all_gather([_pack([a[n] for n in packed], BF16, 256), _pack([a[n] for n in GATHER_F32], F32, 8)]
                      + [a[n].astype(BF16) for n in WIDE], "ag_weights")
    for n, seg in _unpack(got[0], packed, shard_shapes, lead=1).items():
        full[n] = _merge_shards(seg, SHARD_AXIS[n])
    for n, seg in _unpack(got[1], GATHER_F32, shard_shapes, lead=1).items():
        full[n] = _merge_shards(seg, SHARD_AXIS[n])
    full["ssd_w_zx"], full["ssd_w_dt"] = _cat_ssd_in(got[2])
    full["ffn_w_gu"] = _cat_ffn(got[3], got[4])

    lpart, dx, g = _local_step(xs, ps, target, _kernel_layouts(full))
    gfull = _reference_layouts(g, wide=False)
    loss = lax.psum(lpart, ("x", "y", "c"))

    def by_core(t):
        u = t.reshape((4, 2) + t.shape[1:])
        return (lax.dynamic_index_in_dim(u, c, axis=1, keepdims=False),
                lax.dynamic_index_in_dim(u, 1 - c, axis=1, keepdims=False))

    halves = [_split_for_cores(gfull[n], SHARD_AXIS[n], c) for n in rs_packed]
    pairs = [(_pack([h[0] for h in halves], BF16, 256, lead=1), _pack([h[1] for h in halves], BF16, 256, lead=1))]
    pairs += [by_core(t) for t in (_split_ssd_in(g["ssd_w_zx"], g["ssd_w_dt"]),) + tuple(_split_ffn(g["ffn_w_gu"]))]
    landed = _exchange_sibling([s for _, s in pairs])
    partials = [_sum_pairs(k, l) for (k, _), l in zip(pairs, landed)]
    landed = _exchange_chips(partials)
    sums = [_sum_final(lax.dynamic_index_in_dim(t, 2 * mx + my, axis=0, keepdims=False), l)
            for t, l in zip(partials, landed)]
    gshard = _unpack(sums[0], rs_packed, shard_shapes)
    gshard.update(zip(WIDE, sums[1:]))
    rep = _all_gather([_pack([gfull[n] for n in REPLICATED], F32, 64)], "ag_replicated_grads")[0]
    grep = _unpack(_sum_devices(rep), REPLICATED, shard_shapes)
    grads = {**gshard, **grep}

    upd = {n: _adamw(a[n], grads[n], a["m_" + n], a["v_" + n]) for n in WEIGHTS}
    return (loss, dx[None], *[grads[n] for n in WEIGHTS], *[upd[n][0] for n in WEIGHTS],
            *[upd[n][1] for n in WEIGHTS], *[upd[n][2] for n in WEIGHTS])
```

```python
import functools
import math

import jax
import jax.numpy as jnp
from jax import lax
from jax.experimental import pallas as pl
from jax.experimental.pallas import tpu as pltpu

F32 = jnp.float32
BF16 = jnp.bfloat16

N_DEV = 8
D_MODEL = 1024
DEPTH = 4
SSD_INNER = 2048
SSD_HEADS = 32
SSD_HEADDIM = 64
SSD_GROUPS = 8
SSD_STATE = 128
SSD_GROUP_W = SSD_INNER // SSD_GROUPS
SSD_CONV_DIM = SSD_INNER + 2 * SSD_GROUPS * SSD_STATE
SSD_IN_DIM = 2 * SSD_INNER + SSD_CONV_DIM - SSD_INNER + SSD_HEADS
SSD_ZX = SSD_INNER + SSD_CONV_DIM
CONV_K = 4
CHUNK = 128
GMLP_INNER = 2048
GMLP_GROUPS = 16
FFN_DIM = 2816
PLE_DIM = 256
RMS_EPS = 1e-6
LN_EPS = 1e-5
LANES = 128
VMEM_LIMIT = 56 * 1024 * 1024

ADAM_LR = 0.001
ADAM_B1 = 0.9
ADAM_B2 = 0.999
ADAM_EPS = 1e-08
ADAM_WD = 0.01
ADAM_STEP = 10

MESH_ID = pl.DeviceIdType.MESH


def _pick(n, cands):
    for c in cands:
        if c <= n and n % c == 0:
            return c
    return n


def _params(dims):
    return pltpu.CompilerParams(dimension_semantics=dims, vmem_limit_bytes=VMEM_LIMIT)


def _dot(a, b, dims=(((1,), (0,)), ((), ())), precision=None):
    return lax.dot_general(a, b, dims, precision=precision, preferred_element_type=F32)


NN = (((1,), (0,)), ((), ()))
NT = (((1,), (1,)), ((), ()))
TN = (((0,), (0,)), ((), ()))


def _sigmoid(x):
    return 1.0 / (1.0 + jnp.exp(-x))


def _dot01(a, b, dims, split):
    v = (a, b)[split]
    hi = v.astype(BF16)
    r1 = v - hi.astype(F32)
    mid = r1.astype(BF16)
    lo = (r1 - mid.astype(F32)).astype(BF16)
    ones = (a, b)[1 - split].astype(BF16)
    terms = [(t, ones) if split == 0 else (ones, t) for t in (hi, mid, lo)]
    return _dot(*terms[0], dims) + _dot(*terms[1], dims) + _dot(*terms[2], dims)


MM_VMEM_BUDGET = 36 * 1024 * 1024


def _mm_tiles(mode, m, n, k, a_bytes, b_bytes, out_bytes, has_add):
    tm = _pick(m, (1408, 1024, 512, 256, 128))
    tn_cands = [c for c in (2816, 1024, 512, 256, 128) if c <= n and n % c == 0] or [n]
    tk_cands = [k] + [c for c in (2816, 2048, 1024, 512, 256, 128) if c < k and k % c == 0]
    for tk in tk_cands:
        for tn in tn_cands:
            blocks = tm * tk * a_bytes + tk * tn * b_bytes + tm * tn * (out_bytes + (4 if has_add else 0))
            if 2 * blocks + (tm * tn * 4 if tk < k else 0) <= MM_VMEM_BUDGET:
                return tm, tn, tk
    return tm, tn_cands[-1], tk_cands[-1]


def _mm(a, b, mode, out_dtype, add=None):
    if mode == "nn":
        m, k = a.shape
        n = b.shape[1]
    elif mode == "nt":
        m, k = a.shape
        n = b.shape[0]
    else:
        k, m = a.shape
        n = b.shape[1]
    tm, tn, tk = _mm_tiles(mode, m, n, k, a.dtype.itemsize, b.dtype.itemsize, jnp.dtype(out_dtype).itemsize,
                           add is not None)
    nk = k // tk
    dims = {"nn": NN, "nt": NT, "tn": TN}[mode]

    def body(*refs):
        if add is None:
            a_ref, b_ref, o_ref = refs[:3]
            add_ref = None
            rest = refs[3:]
        else:
            a_ref, b_ref, add_ref, o_ref = refs[:4]
            rest = refs[4:]
        part = _dot(a_ref[...].astype(BF16), b_ref[...].astype(BF16), dims)

        def finish(acc):
            if add_ref is not None:
                acc = acc + add_ref[...]
            o_ref[...] = acc.astype(o_ref.dtype)

        if nk == 1:
            finish(part)
        else:
            acc_ref = rest[0]
            kk = pl.program_id(2)

            @pl.when(kk == 0)
            def _():
                acc_ref[...] = part

            @pl.when(kk > 0)
            def _():
                acc_ref[...] += part

            @pl.when(kk == nk - 1)
            def _():
                finish(acc_ref[...])

    if mode == "nn":
        a_spec = pl.BlockSpec((tm, tk), lambda i, j, kk: (i, kk))
        b_spec = pl.BlockSpec((tk, tn), lambda i, j, kk: (kk, j))
    elif mode == "nt":
        a_spec = pl.BlockSpec((tm, tk), lambda i, j, kk: (i, kk))
        b_spec = pl.BlockSpec((tn, tk), lambda i, j, kk: (j, kk))
    else:
        a_spec = pl.BlockSpec((tk, tm), lambda i, j, kk: (kk, i))
        b_spec = pl.BlockSpec((tk, tn), lambda i, j, kk: (kk, j))
    o_spec = pl.BlockSpec((tm, tn), lambda i, j, kk: (i, j))
    in_specs = [a_spec, b_spec] + ([o_spec] if add is not None else [])
    args = (a, b) + ((add,) if add is not None else ())
    return pl.pallas_call(
        body,
        grid=(m // tm, n // tn, nk),
        in_specs=in_specs,
        out_specs=o_spec,
        out_shape=jax.ShapeDtypeStruct((m, n), out_dtype),
        scratch_shapes=[pltpu.VMEM((tm, tn), F32)] if nk > 1 else [],
        compiler_params=_params(("parallel", "parallel", "arbitrary")),
        name=f"mm_{mode}_{m}x{k}x{n}",
    )(*args)


def _rms_fwd(x, w):
    s, d = x.shape
    tr = _pick(s, (512, 256, 128))

    def body(x_ref, w_ref, o_ref):
        xv = x_ref[...]
        r = lax.rsqrt(jnp.mean(xv * xv, axis=-1, keepdims=True) + RMS_EPS)
        o_ref[...] = (xv * r * w_ref[...]).astype(o_ref.dtype)

    return pl.pallas_call(
        body,
        grid=(s // tr,),
        in_specs=[pl.BlockSpec((tr, d), lambda i: (i, 0)), pl.BlockSpec((1, d), lambda i: (0, 0))],
        out_specs=pl.BlockSpec((tr, d), lambda i: (i, 0)),
        out_shape=jax.ShapeDtypeStruct((s, d), BF16),
        compiler_params=_params(("parallel",)),
        name="rms_fwd",
    )(x, w)


def _rms_bwd(dyn, x, w, add):
    s, d = x.shape
    tr = _pick(s, (512, 256, 128))

    def body(dy_ref, x_ref, w_ref, add_ref, dx_ref, dw_ref):
        xv = x_ref[...]
        dy = dy_ref[...].astype(F32)
        r = lax.rsqrt(jnp.mean(xv * xv, axis=-1, keepdims=True) + RMS_EPS)
        xn = xv * r
        dxh = dy * w_ref[...]
        dx = r * (dxh - xn * jnp.mean(dxh * xn, axis=-1, keepdims=True))
        dx_ref[...] = add_ref[...] + dx
        part = jnp.sum(dy * xn, axis=0, keepdims=True)

        @pl.when(pl.program_id(0) == 0)
        def _():
            dw_ref[...] = part

        @pl.when(pl.program_id(0) > 0)
        def _():
            dw_ref[...] += part

    row = pl.BlockSpec((tr, d), lambda i: (i, 0))
    vec = pl.BlockSpec((1, d), lambda i: (0, 0))
    return pl.pallas_call(
        body,
        grid=(s // tr,),
        in_specs=[row, row, vec, row],
        out_specs=[row, vec],
        out_shape=[jax.ShapeDtypeStruct((s, d), F32), jax.ShapeDtypeStruct((1, d), F32)],
        compiler_params=_params(("arbitrary",)),
        name="rms_bwd",
    )(dyn, x, w, add)


CONV_ROWS = 256
CONV_COLS = 256
CONV_HALO = 16


def _conv_taps(ext, w, base, rows):
    acc = w[0:1, :] * ext[base:base + rows]
    for k in range(1, CONV_K):
        acc = acc + w[k:k + 1, :] * ext[base + k:base + k + rows]
    return acc


def _ssd_conv_fwd(zx, conv_w, conv_b):
    s = zx.shape[0]
    c = SSD_CONV_DIM
    nsteps = s // CONV_ROWS
    off = SSD_INNER // CONV_COLS

    def body(x_ref, w_ref, b_ref, o_ref):
        w = w_ref[...]
        b = b_ref[...]

        def step(i, carry):
            r0 = pl.multiple_of(i * CONV_ROWS, CONV_ROWS)
            cur = x_ref[pl.ds(r0, CONV_ROWS), :].astype(F32)
            p0 = pl.multiple_of(jnp.maximum(r0 - CONV_HALO, 0), CONV_HALO)
            prev = x_ref[pl.ds(p0, CONV_HALO), :].astype(F32)
            prev = jnp.where(i == 0, 0.0, prev)
            ext = jnp.concatenate([prev, cur], axis=0)
            acc = _conv_taps(ext, w, CONV_HALO - (CONV_K - 1), CONV_ROWS) + b
            o_ref[pl.ds(r0, CONV_ROWS), :] = (acc * _sigmoid(acc)).astype(o_ref.dtype)
            return carry

        lax.fori_loop(0, nsteps, step, 0)

    return pl.pallas_call(
        body,
        grid=(c // CONV_COLS,),
        in_specs=[pl.BlockSpec((s, CONV_COLS), lambda j: (0, j + off)),
                  pl.BlockSpec((8, CONV_COLS), lambda j: (0, j)),
                  pl.BlockSpec((1, CONV_COLS), lambda j: (0, j))],
        out_specs=pl.BlockSpec((s, CONV_COLS), lambda j: (0, j)),
        out_shape=jax.ShapeDtypeStruct((s, c), BF16),
        compiler_params=_params(("parallel",)),
        name="ssd_conv_fwd",
    )(zx, conv_w, conv_b)


def _ssd_conv_bwd(zx, dxbc, conv_w, conv_b, dzx):
    s = zx.shape[0]
    c = SSD_CONV_DIM
    nsteps = s // CONV_ROWS
    off = SSD_INNER // CONV_COLS

    def body(x_ref, dy_ref, w_ref, b_ref, dzx_in_ref, dx_ref, dw_ref, db_ref, dc_ref):
        w = w_ref[...]
        b = b_ref[...]
        dc_ref[pl.ds(s, CONV_HALO), :] = jnp.zeros((CONV_HALO, CONV_COLS), F32)

        def step1(i, carry):
            dw0, dw1, dw2, dw3, dbs = carry
            r0 = pl.multiple_of(i * CONV_ROWS, CONV_ROWS)
            cur = x_ref[pl.ds(r0, CONV_ROWS), :].astype(F32)
            p0 = pl.multiple_of(jnp.maximum(r0 - CONV_HALO, 0), CONV_HALO)
            prev = x_ref[pl.ds(p0, CONV_HALO), :].astype(F32)
            prev = jnp.where(i == 0, 0.0, prev)
            ext = jnp.concatenate([prev, cur], axis=0)
            base = CONV_HALO - (CONV_K - 1)
            acc = _conv_taps(ext, w, base, CONV_ROWS) + b
            sg = _sigmoid(acc)
            dcv = dy_ref[pl.ds(r0, CONV_ROWS), :].astype(F32) * (sg * (1.0 + acc * (1.0 - sg)))
            dc_ref[pl.ds(r0, CONV_ROWS), :] = dcv
            dws = [jnp.sum(dcv * ext[base + k:base + k + CONV_ROWS], axis=0, keepdims=True) for k in range(CONV_K)]
            return (dw0 + dws[0], dw1 + dws[1], dw2 + dws[2], dw3 + dws[3], dbs + jnp.sum(dcv, axis=0, keepdims=True))

        z = jnp.zeros((1, CONV_COLS), F32)
        dw0, dw1, dw2, dw3, dbs = lax.fori_loop(0, nsteps, step1, (z, z, z, z, z))
        dw_ref[...] = jnp.concatenate([dw0, dw1, dw2, dw3, z, z, z, z], axis=0)
        db_ref[...] = dbs

        def step2(i, carry):
            r0 = pl.multiple_of(i * CONV_ROWS, CONV_ROWS)
            ext = dc_ref[pl.ds(r0, CONV_ROWS + CONV_HALO), :]
            acc = w[0:1, :] * ext[CONV_K - 1:CONV_K - 1 + CONV_ROWS]
            for k in range(1, CONV_K):
                acc = acc + w[k:k + 1, :] * ext[CONV_K - 1 - k:CONV_K - 1 - k + CONV_ROWS]
            dx_ref[pl.ds(r0, CONV_ROWS), :] = acc.astype(dx_ref.dtype)
            return carry

        lax.fori_loop(0, nsteps, step2, 0)

    col = pl.BlockSpec((s, CONV_COLS), lambda j: (0, j))
    shifted = pl.BlockSpec((s, CONV_COLS), lambda j: (0, j + off))
    return pl.pallas_call(
        body,
        grid=(c // CONV_COLS,),
        in_specs=[shifted, col,
                  pl.BlockSpec((8, CONV_COLS), lambda j: (0, j)),
                  pl.BlockSpec((1, CONV_COLS), lambda j: (0, j)),
                  pl.BlockSpec(memory_space=pl.ANY)],
        out_specs=[shifted, pl.BlockSpec((8, CONV_COLS), lambda j: (0, j)), pl.BlockSpec((1, CONV_COLS), lambda j: (0, j))],
        out_shape=[jax.ShapeDtypeStruct((s, SSD_ZX), BF16), jax.ShapeDtypeStruct((8, c), F32),
                   jax.ShapeDtypeStruct((1, c), F32)],
        scratch_shapes=[pltpu.VMEM((s + CONV_HALO, CONV_COLS), F32)],
        input_output_aliases={4: 0},
        compiler_params=_params(("parallel",)),
        name="ssd_conv_bwd",
    )(zx, dxbc, conv_w, conv_b, dzx)


def _ssd_consts():
    li = lax.broadcasted_iota(jnp.int32, (CHUNK, CHUNK), 0)
    si = lax.broadcasted_iota(jnp.int32, (CHUNK, CHUNK), 1)
    tril = li >= si
    hrow = lax.broadcasted_iota(jnp.int32, (LANES, SSD_INNER), 0)
    hcol = lax.broadcasted_iota(jnp.int32, (LANES, SSD_INNER), 1) // SSD_HEADDIM
    expand = (hrow == hcol).astype(F32)
    return tril, expand


def _ssd_chunk_common(dtp_ref, bias_ref, alog_ref, tril, expand):
    lane = lax.broadcasted_iota(jnp.int32, (1, LANES), 1)
    valid = lane < SSD_HEADS
    pre = dtp_ref[...] + bias_ref[...]
    dt = jnp.where(valid, jnp.maximum(pre, 0.0) + jnp.log1p(jnp.exp(-jnp.abs(pre))), 0.0)
    a = jnp.where(valid, -jnp.exp(alog_ref[...]), 0.0)
    da = dt * a
    cs = _dot01(tril.astype(F32), da, NN, 1)
    cs_x = _dot01(cs, expand, NN, 0)
    dt_x = _dot01(dt, expand, NN, 0)
    return pre, dt, a, cs, cs_x, dt_x


def _ssd_scan_fwd(xbc, dtp, dt_bias, a_log, d_skip):
    s = xbc.shape[0]
    nc = s // CHUNK
    gw = SSD_GROUP_W

    def body(xbc_ref, dtp_ref, bias_ref, alog_ref, d_ref, y_ref, prev_ref, state_ref):
        c = pl.program_id(0)

        @pl.when(c == 0)
        def _():
            state_ref[...] = jnp.zeros_like(state_ref)

        tril, expand = _ssd_consts()
        pre, dt, a, cs, cs_x, dt_x = _ssd_chunk_common(dtp_ref, bias_ref, alog_ref, tril, expand)
        cs_t = cs.T
        d_x = _dot01(jnp.broadcast_to(d_ref[...], (8, LANES)), expand, NN, 0)[0:1, :]
        cs_last = cs_x[CHUNK - 1:CHUNK, :]
        dec_out = jnp.exp(cs_x)
        dec_st = jnp.exp(cs_last - cs_x)
        dec_ch = jnp.exp(cs_last)
        x = xbc_ref[:, 0:SSD_INNER].astype(F32)
        xr = x * dt_x
        xrs = xr * dec_st
        lane_g = lax.broadcasted_iota(jnp.int32, (1, gw), 1) // SSD_HEADDIM
        for g in range(SSD_GROUPS):
            sl = slice(g * gw, (g + 1) * gw)
            bg = xbc_ref[:, SSD_INNER + g * SSD_STATE:SSD_INNER + (g + 1) * SSD_STATE]
            cg = xbc_ref[:, SSD_INNER + (SSD_GROUPS + g) * SSD_STATE:SSD_INNER + (SSD_GROUPS + g + 1) * SSD_STATE]
            cb = _dot(cg, bg, NT)
            prev_g = state_ref[:, sl]
            prev_ref[0, :, sl] = prev_g
            yo = _dot(cg, prev_g.astype(BF16), NN) * dec_out[:, sl]
            xr_g = xr[:, sl]
            yd = jnp.zeros((CHUNK, gw), F32)
            for r in range(SSD_HEADS // SSD_GROUPS):
                h = g * (SSD_HEADS // SSD_GROUPS) + r
                diff = cs[:, h:h + 1] - cs_t[h:h + 1, :]
                lmat = jnp.exp(jnp.where(tril, diff, -1e30))
                wmat = (cb * lmat).astype(BF16)
                xr_h = jnp.where(lane_g == r, xr_g, 0.0).astype(BF16)
                yd = yd + _dot(wmat, xr_h, NN)
            y_ref[:, sl] = yd + yo + x[:, sl] * d_x[:, sl]
            sc = _dot(bg, xrs[:, sl].astype(BF16), TN)
            state_ref[:, sl] = prev_g * dec_ch[:, sl] + sc

    vec = pl.BlockSpec((1, LANES), lambda c: (0, 0))
    return pl.pallas_call(
        body,
        grid=(nc,),
        in_specs=[pl.BlockSpec((CHUNK, SSD_CONV_DIM), lambda c: (c, 0)),
                  pl.BlockSpec((CHUNK, LANES), lambda c: (c, 0)), vec, vec, vec],
        out_specs=[pl.BlockSpec((CHUNK, SSD_INNER), lambda c: (c, 0)),
                   pl.BlockSpec((1, SSD_STATE, SSD_INNER), lambda c: (c, 0, 0))],
        out_shape=[jax.ShapeDtypeStruct((s, SSD_INNER), F32), jax.ShapeDtypeStruct((nc, SSD_STATE, SSD_INNER), F32)],
        scratch_shapes=[pltpu.VMEM((SSD_STATE, SSD_INNER), F32)],
        compiler_params=_params(("arbitrary",)),
        name="ssd_scan_fwd",
    )(xbc, dtp, dt_bias, a_log, d_skip)


def _ssd_scan_bwd(xbc, dtp, prev, dy, dt_bias, a_log, d_skip):
    s = xbc.shape[0]
    nc = s // CHUNK
    gw = SSD_GROUP_W
    hpg = SSD_HEADS // SSD_GROUPS

    def body(xbc_ref, dtp_ref, prev_ref, dy_ref, bias_ref, alog_ref, d_ref,
             dxbc_ref, ddtp_ref, dbias_ref, dalog_ref, dd_ref, dp_ref, ddx_ref):
        step = pl.program_id(0)

        @pl.when(step == 0)
        def _():
            dp_ref[...] = jnp.zeros_like(dp_ref)
            ddx_ref[...] = jnp.zeros_like(ddx_ref)
            dbias_ref[...] = jnp.zeros_like(dbias_ref)
            dalog_ref[...] = jnp.zeros_like(dalog_ref)

        tril, expand = _ssd_consts()
        pre, dt, a, cs, cs_x, dt_x = _ssd_chunk_common(dtp_ref, bias_ref, alog_ref, tril, expand)
        cs_t = cs.T
        d_x = _dot01(jnp.broadcast_to(d_ref[...], (8, LANES)), expand, NN, 0)[0:1, :]
        cs_last = cs_x[CHUNK - 1:CHUNK, :]
        dec_out = jnp.exp(cs_x)
        dec_st = jnp.exp(cs_last - cs_x)
        dec_ch = jnp.exp(cs_last)
        x = xbc_ref[:, 0:SSD_INNER].astype(F32)
        dyv = dy_ref[...]
        xr = x * dt_x
        xrs = xr * dec_st
        lane_g = lax.broadcasted_iota(jnp.int32, (1, gw), 1) // SSD_HEADDIM
        hsel = lax.broadcasted_iota(jnp.int32, (CHUNK, LANES), 1)
        dcs = jnp.zeros((CHUNK, LANES), F32)
        last_parts = []
        t_parts = []
        dxr_parts = []
        for g in range(SSD_GROUPS):
            sl = slice(g * gw, (g + 1) * gw)
            bsl = slice(SSD_INNER + g * SSD_STATE, SSD_INNER + (g + 1) * SSD_STATE)
            csl = slice(SSD_INNER + (SSD_GROUPS + g) * SSD_STATE, SSD_INNER + (SSD_GROUPS + g + 1) * SSD_STATE)
            bg = xbc_ref[:, bsl]
            cg = xbc_ref[:, csl]
            cb = _dot(cg, bg, NT)
            prev_g = prev_ref[0, :, sl]
            prev_b = prev_g.astype(BF16)
            dp_g = dp_ref[:, sl]
            dp_b = dp_g.astype(BF16)
            dy_g = dyv[:, sl]
            xr_g = xr[:, sl]
            gmat = _dot(cg, prev_b, NN)
            dgm = (dy_g * dec_out[:, sl]).astype(BF16)
            dc_g = _dot(dgm, prev_b, NT)
            dprev = _dot(cg, dgm, TN)
            t1 = dy_g * gmat * dec_out[:, sl]
            mm_ = _dot(bg, dp_b, NN)
            db_g = _dot(xrs[:, sl].astype(BF16), dp_b, NT)
            dxr_g = mm_ * dec_st[:, sl]
            t2 = dxr_g * xr_g
            last = jnp.sum(t2, axis=0, keepdims=True) + jnp.sum(dp_g * prev_g, axis=0, keepdims=True) * dec_ch[:, sl]
            dp_ref[:, sl] = dp_g * dec_ch[:, sl] + dprev
            dcb = jnp.zeros((CHUNK, CHUNK), F32)
            for r in range(hpg):
                h = g * hpg + r
                diff = cs[:, h:h + 1] - cs_t[h:h + 1, :]
                lmat = jnp.exp(jnp.where(tril, diff, -1e30))
                wmat = cb * lmat
                dy_h = jnp.where(lane_g == r, dy_g, 0.0).astype(BF16)
                dw = _dot(dy_h, xr_g.astype(BF16), NT)
                dxr_g = dxr_g + _dot(wmat.astype(BF16), dy_h, TN)
                dcb = dcb + dw * lmat
                q = (dw * wmat).astype(BF16)
                onehot = (hsel == h).astype(BF16)
                dcs = dcs + _dot(q, onehot, NN) - _dot(q, onehot, TN)
            dcb_b = dcb.astype(BF16)
            dc_g = dc_g + _dot(dcb_b, bg, NN)
            db_g = db_g + _dot(dcb_b, cg, TN)
            dxbc_ref[:, bsl] = db_g.astype(dxbc_ref.dtype)
            dxbc_ref[:, csl] = dc_g.astype(dxbc_ref.dtype)
            t_parts.append(t1 - t2)
            last_parts.append(last)
            dxr_parts.append(dxr_g)
        dxr = jnp.concatenate(dxr_parts, axis=1)
        tt = jnp.concatenate(t_parts, axis=1)
        last_x = jnp.concatenate(last_parts, axis=1)
        dxbc_ref[:, 0:SSD_INNER] = (dxr * dt_x + dyv * d_x).astype(dxbc_ref.dtype)
        dcs = dcs + _dot01(tt, expand, NT, 0)
        last_h = _dot01(jnp.broadcast_to(last_x, (8, SSD_INNER)), expand, NT, 0)[0:1, :]
        rowi = lax.broadcasted_iota(jnp.int32, (CHUNK, LANES), 0)
        dcs = dcs + jnp.where(rowi == CHUNK - 1, last_h, 0.0)
        dda = _dot01(tril.astype(F32), dcs, TN, 1)
        ddt = dda * a + _dot01(dxr * x, expand, NT, 0)
        dpre = ddt * _sigmoid(pre)
        ddtp_ref[...] = dpre
        dbias_ref[...] += jnp.sum(dpre, axis=0, keepdims=True)
        dalog_ref[...] += jnp.sum(dda * dt, axis=0, keepdims=True) * a
        ddx_ref[...] += jnp.broadcast_to(jnp.sum(dyv * x, axis=0, keepdims=True), (8, SSD_INNER))

        @pl.when(step == nc - 1)
        def _():
            dd_ref[...] = _dot01(ddx_ref[...], expand, NT, 0)[0:1, :]

    rev = lambda c: (nc - 1 - c, 0)
    vec = pl.BlockSpec((1, LANES), lambda c: (0, 0))
    return pl.pallas_call(
        body,
        grid=(nc,),
        in_specs=[pl.BlockSpec((CHUNK, SSD_CONV_DIM), rev), pl.BlockSpec((CHUNK, LANES), rev),
                  pl.BlockSpec((1, SSD_STATE, SSD_INNER), lambda c: (nc - 1 - c, 0, 0)),
                  pl.BlockSpec((CHUNK, SSD_INNER), rev), vec, vec, vec],
        out_specs=[pl.BlockSpec((CHUNK, SSD_CONV_DIM), rev), pl.BlockSpec((CHUNK, LANES), rev), vec, vec, vec],
        out_shape=[jax.ShapeDtypeStruct((s, SSD_CONV_DIM), BF16), jax.ShapeDtypeStruct((s, LANES), F32),
                   jax.ShapeDtypeStruct((1, LANES), F32), jax.ShapeDtypeStruct((1, LANES), F32),
                   jax.ShapeDtypeStruct((1, LANES), F32)],
        scratch_shapes=[pltpu.VMEM((SSD_STATE, SSD_INNER), F32), pltpu.VMEM((8, SSD_INNER), F32)],
        compiler_params=_params(("arbitrary",)),
        name="ssd_scan_bwd",
    )(xbc, dtp, prev, dy, dt_bias, a_log, d_skip)


def _ssd_gate_fwd(y, zx, norm_w):
    s = y.shape[0]
    tr = _pick(s, (256, 128))
    gw = SSD_GROUP_W

    def body(y_ref, z_ref, w_ref, o_ref):
        for g in range(SSD_GROUPS):
            sl = slice(g * gw, (g + 1) * gw)
            z = z_ref[:, sl].astype(F32)
            gv = y_ref[:, sl] * (z * _sigmoid(z))
            r = lax.rsqrt(jnp.mean(gv * gv, axis=-1, keepdims=True) + LN_EPS)
            o_ref[:, sl] = (gv * r * w_ref[:, sl]).astype(o_ref.dtype)

    row = pl.BlockSpec((tr, SSD_INNER), lambda i: (i, 0))
    return pl.pallas_call(
        body,
        grid=(s // tr,),
        in_specs=[row, row, pl.BlockSpec((1, SSD_INNER), lambda i: (0, 0))],
        out_specs=row,
        out_shape=jax.ShapeDtypeStruct((s, SSD_INNER), BF16),
        compiler_params=_params(("parallel",)),
        name="ssd_gate_fwd",
    )(y, zx, norm_w)


def _ssd_gate_bwd(dgn, y, zx, norm_w):
    s = y.shape[0]
    tr = _pick(s, (256, 128))
    gw = SSD_GROUP_W

    def body(dg_ref, y_ref, z_ref, w_ref, dy_ref, dz_ref, dw_ref):
        parts = []
        for g in range(SSD_GROUPS):
            sl = slice(g * gw, (g + 1) * gw)
            z = z_ref[:, sl].astype(F32)
            yv = y_ref[:, sl]
            sg = _sigmoid(z)
            sz = z * sg
            gv = yv * sz
            r = lax.rsqrt(jnp.mean(gv * gv, axis=-1, keepdims=True) + LN_EPS)
            gn = gv * r
            dout = dg_ref[:, sl].astype(F32)
            parts.append(jnp.sum(dout * gn, axis=0, keepdims=True))
            dgn_ = dout * w_ref[:, sl]
            dgv = r * (dgn_ - gn * jnp.mean(dgn_ * gn, axis=-1, keepdims=True))
            dy_ref[:, sl] = dgv * sz
            dz_ref[:, sl] = (dgv * yv * (sg * (1.0 + z * (1.0 - sg)))).astype(dz_ref.dtype)
        part = jnp.concatenate(parts, axis=1)

        @pl.when(pl.program_id(0) == 0)
        def _():
            dw_ref[...] = part

        @pl.when(pl.program_id(0) > 0)
        def _():
            dw_ref[...] += part

    row = pl.BlockSpec((tr, SSD_INNER), lambda i: (i, 0))
    vec = pl.BlockSpec((1, SSD_INNER), lambda i: (0, 0))
    return pl.pallas_call(
        body,
        grid=(s // tr,),
        in_specs=[row, row, row, vec],
        out_specs=[row, row, vec],
        out_shape=[jax.ShapeDtypeStruct((s, SSD_INNER), F32), jax.ShapeDtypeStruct((s, SSD_ZX), BF16),
                   jax.ShapeDtypeStruct((1, SSD_INNER), F32)],
        compiler_params=_params(("arbitrary",)),
        name="ssd_gate_bwd",
    )(dgn, y, zx, norm_w)


INV_SQRT2 = 1.0 / math.sqrt(2.0)
INV_SQRT2PI = 1.0 / math.sqrt(2.0 * math.pi)


def _gelu(x):
    return 0.5 * x * (1.0 + lax.erf(x * INV_SQRT2))


def _gelu_grad(x):
    return 0.5 * (1.0 + lax.erf(x * INV_SQRT2)) + x * INV_SQRT2PI * jnp.exp(-0.5 * x * x)


def _gmlp_act_fwd(pre, b_in, ln_w, ln_b):
    s = pre.shape[0]
    tr = _pick(s, (256, 128))
    n = GMLP_INNER

    def body(p_ref, b_ref, w_ref, lb_ref, u_ref, v_ref):
        u_ref[...] = _gelu(p_ref[:, 0:n].astype(F32) + b_ref[:, 0:n]).astype(u_ref.dtype)
        hv = _gelu(p_ref[:, n:2 * n].astype(F32) + b_ref[:, n:2 * n])
        mu = jnp.mean(hv, axis=-1, keepdims=True)
        xc = hv - mu
        r = lax.rsqrt(jnp.mean(xc * xc, axis=-1, keepdims=True) + LN_EPS)
        v_ref[...] = (xc * r * w_ref[...] + lb_ref[...]).astype(v_ref.dtype)

    half = pl.BlockSpec((tr, n), lambda i: (i, 0))
    vec = pl.BlockSpec((1, n), lambda i: (0, 0))
    return pl.pallas_call(
        body,
        grid=(s // tr,),
        in_specs=[pl.BlockSpec((tr, 2 * n), lambda i: (i, 0)), pl.BlockSpec((1, 2 * n), lambda i: (0, 0)), vec, vec],
        out_specs=[half, half],
        out_shape=[jax.ShapeDtypeStruct((s, n), BF16), jax.ShapeDtypeStruct((s, n), BF16)],
        compiler_params=_params(("parallel",)),
        name="gmlp_act_fwd",
    )(pre, b_in, ln_w, ln_b)


def _gmlp_act_bwd(pre, b_in, ln_w, du, dv):
    s = pre.shape[0]
    tr = _pick(s, (256, 128))
    n = GMLP_INNER

    def body(p_ref, b_ref, w_ref, du_ref, dv_ref, dp_ref, db_ref, dw_ref, dlb_ref):
        xu = p_ref[:, 0:n].astype(F32) + b_ref[:, 0:n]
        dpu = du_ref[...].astype(F32) * _gelu_grad(xu)
        xv = p_ref[:, n:2 * n].astype(F32) + b_ref[:, n:2 * n]
        hv = _gelu(xv)
        mu = jnp.mean(hv, axis=-1, keepdims=True)
        xc = hv - mu
        r = lax.rsqrt(jnp.mean(xc * xc, axis=-1, keepdims=True) + LN_EPS)
        vh = xc * r
        dvv = dv_ref[...].astype(F32)
        dvh = dvv * w_ref[...]
        dh = r * (dvh - jnp.mean(dvh, axis=-1, keepdims=True) - vh * jnp.mean(dvh * vh, axis=-1, keepdims=True))
        dpv = dh * _gelu_grad(xv)
        dp_ref[:, 0:n] = dpu.astype(dp_ref.dtype)
        dp_ref[:, n:2 * n] = dpv.astype(dp_ref.dtype)
        pb = jnp.concatenate([jnp.sum(dpu, axis=0, keepdims=True), jnp.sum(dpv, axis=0, keepdims=True)], axis=1)
        pw = jnp.sum(dvv * vh, axis=0, keepdims=True)
        plb = jnp.sum(dvv, axis=0, keepdims=True)

        @pl.when(pl.program_id(0) == 0)
        def _():
            db_ref[...] = pb
            dw_ref[...] = pw
            dlb_ref[...] = plb

        @pl.when(pl.program_id(0) > 0)
        def _():
            db_ref[...] += pb
            dw_ref[...] += pw
            dlb_ref[...] += plb

    half = pl.BlockSpec((tr, n), lambda i: (i, 0))
    full = pl.BlockSpec((tr, 2 * n), lambda i: (i, 0))
    vec = pl.BlockSpec((1, n), lambda i: (0, 0))
    vec2 = pl.BlockSpec((1, 2 * n), lambda i: (0, 0))
    return pl.pallas_call(
        body,
        grid=(s // tr,),
        in_specs=[full, vec2, vec, half, half],
        out_specs=[full, vec2, vec, vec],
        out_shape=[jax.ShapeDtypeStruct((s, 2 * n), BF16), jax.ShapeDtypeStruct((1, 2 * n), F32),
                   jax.ShapeDtypeStruct((1, n), F32), jax.ShapeDtypeStruct((1, n), F32)],
        compiler_params=_params(("arbitrary",)),
        name="gmlp_act_bwd",
    )(pre, b_in, ln_w, du, dv)


def _gmlp_mix_fwd(u, v, w_s, b_st):
    s = u.shape[0]
    gd = GMLP_INNER // GMLP_GROUPS

    def body(u_ref, v_ref, w_ref, b_ref, o_ref):
        li = lax.broadcasted_iota(jnp.int32, (CHUNK, CHUNK), 0)
        si = lax.broadcasted_iota(jnp.int32, (CHUNK, CHUNK), 1)
        tril = li >= si
        for g in range(GMLP_GROUPS):
            sl = slice(g * gd, (g + 1) * gd)
            wm = jnp.where(tril, w_ref[g], 0.0).astype(BF16)
            mixed = _dot(wm, v_ref[:, sl], NN) + b_ref[:, g:g + 1]
            o_ref[:, sl] = (u_ref[:, sl].astype(F32) * mixed).astype(o_ref.dtype)

    row = pl.BlockSpec((CHUNK, GMLP_INNER), lambda c: (c, 0))
    return pl.pallas_call(
        body,
        grid=(s // CHUNK,),
        in_specs=[row, row, pl.BlockSpec((GMLP_GROUPS, CHUNK, CHUNK), lambda c: (0, 0, 0)),
                  pl.BlockSpec((CHUNK, LANES), lambda c: (0, 0))],
        out_specs=row,
        out_shape=jax.ShapeDtypeStruct((s, GMLP_INNER), BF16),
        compiler_params=_params(("parallel",)),
        name="gmlp_mix_fwd",
    )(u, v, w_s, b_st)


def _gmlp_mix_bwd(dgated, u, v, w_s, b_st):
    s = u.shape[0]
    nc = s // CHUNK
    gd = GMLP_INNER // GMLP_GROUPS

    def body(dg_ref, u_ref, v_ref, w_ref, b_ref, du_ref, dv_ref, dw_ref, db_ref):
        c = pl.program_id(0)

        @pl.when(c == 0)
        def _():
            dw_ref[...] = jnp.zeros_like(dw_ref)
            db_ref[...] = jnp.zeros_like(db_ref)

        li = lax.broadcasted_iota(jnp.int32, (CHUNK, CHUNK), 0)
        si = lax.broadcasted_iota(jnp.int32, (CHUNK, CHUNK), 1)
        tril = li >= si
        lane = lax.broadcasted_iota(jnp.int32, (CHUNK, LANES), 1)
        dbacc = jnp.zeros((CHUNK, LANES), F32)
        for g in range(GMLP_GROUPS):
            sl = slice(g * gd, (g + 1) * gd)
            wm = jnp.where(tril, w_ref[g], 0.0).astype(BF16)
            vg = v_ref[:, sl]
            mixed = _dot(wm, vg, NN) + b_ref[:, g:g + 1]
            dgv = dg_ref[:, sl].astype(F32)
            du_ref[:, sl] = (dgv * mixed).astype(du_ref.dtype)
            dm = dgv * u_ref[:, sl].astype(F32)
            dm_b = dm.astype(BF16)
            dv_ref[:, sl] = _dot(wm, dm_b, TN).astype(dv_ref.dtype)
            dw_ref[g] += jnp.where(tril, _dot(dm_b, vg, NT), 0.0)
            dbacc = dbacc + jnp.where(lane == g, jnp.sum(dm, axis=1, keepdims=True), 0.0)
        db_ref[...] += dbacc

    row = pl.BlockSpec((CHUNK, GMLP_INNER), lambda c: (c, 0))
    wspec = pl.BlockSpec((GMLP_GROUPS, CHUNK, CHUNK), lambda c: (0, 0, 0))
    bspec = pl.BlockSpec((CHUNK, LANES), lambda c: (0, 0))
    return pl.pallas_call(
        body,
        grid=(nc,),
        in_specs=[row, row, row, wspec, bspec],
        out_specs=[row, row, wspec, bspec],
        out_shape=[jax.ShapeDtypeStruct((s, GMLP_INNER), BF16), jax.ShapeDtypeStruct((s, GMLP_INNER), BF16),
                   jax.ShapeDtypeStruct((GMLP_GROUPS, CHUNK, CHUNK), F32), jax.ShapeDtypeStruct((CHUNK, LANES), F32)],
        compiler_params=_params(("arbitrary",)),
        name="gmlp_mix_bwd",
    )(dgated, u, v, w_s, b_st)


def _swiglu_fwd(gu):
    s = gu.shape[0]
    f = FFN_DIM
    tr = _pick(s, (512, 256, 128))

    def body(gu_ref, o_ref):
        gt = gu_ref[:, 0:f].astype(F32)
        o_ref[...] = (gt * _sigmoid(gt) * gu_ref[:, f:2 * f].astype(F32)).astype(o_ref.dtype)

    return pl.pallas_call(
        body,
        grid=(s // tr,),
        in_specs=[pl.BlockSpec((tr, 2 * f), lambda i: (i, 0))],
        out_specs=pl.BlockSpec((tr, f), lambda i: (i, 0)),
        out_shape=jax.ShapeDtypeStruct((s, f), BF16),
        compiler_params=_params(("parallel",)),
        name="swiglu_fwd",
    )(gu)


def _swiglu_bwd(gu, dhid):
    s = gu.shape[0]
    f = FFN_DIM
    tr = _pick(s, (512, 256, 128))

    def body(gu_ref, dh_ref, dgu_ref):
        gt = gu_ref[:, 0:f].astype(F32)
        up = gu_ref[:, f:2 * f].astype(F32)
        dh = dh_ref[...].astype(F32)
        sg = _sigmoid(gt)
        dgu_ref[:, 0:f] = (dh * up * (sg * (1.0 + gt * (1.0 - sg)))).astype(dgu_ref.dtype)
        dgu_ref[:, f:2 * f] = (dh * gt * sg).astype(dgu_ref.dtype)

    wide = pl.BlockSpec((tr, 2 * f), lambda i: (i, 0))
    return pl.pallas_call(
        body,
        grid=(s // tr,),
        in_specs=[wide, pl.BlockSpec((tr, f), lambda i: (i, 0))],
        out_specs=wide,
        out_shape=jax.ShapeDtypeStruct((s, 2 * f), BF16),
        compiler_params=_params(("parallel",)),
        name="swiglu_bwd",
    )(gu, dhid)


def _ple_fwd(pe, gl, h, ple_norm):
    s, d = h.shape
    tr = _pick(s, (512, 256, 128))

    def body(pe_ref, gl_ref, h_ref, w_ref, o_ref):
        pe_ = pe_ref[...]
        r = lax.rsqrt(jnp.mean(pe_ * pe_, axis=-1, keepdims=True) + RMS_EPS)
        o_ref[...] = h_ref[...] + _sigmoid(gl_ref[...]) * (pe_ * r * w_ref[...])

    row = pl.BlockSpec((tr, d), lambda i: (i, 0))
    return pl.pallas_call(
        body,
        grid=(s // tr,),
        in_specs=[row, row, row, pl.BlockSpec((1, d), lambda i: (0, 0))],
        out_specs=row,
        out_shape=jax.ShapeDtypeStruct((s, d), F32),
        compiler_params=_params(("parallel",)),
        name="ple_fwd",
    )(pe, gl, h, ple_norm)


def _ple_bwd(dh, pe, gl, ple_norm):
    s, d = dh.shape
    tr = _pick(s, (512, 256, 128))

    def body(dh_ref, pe_ref, gl_ref, w_ref, dgl_ref, dpe_ref, dw_ref):
        pe_ = pe_ref[...]
        dhv = dh_ref[...]
        r = lax.rsqrt(jnp.mean(pe_ * pe_, axis=-1, keepdims=True) + RMS_EPS)
        pn = pe_ * r
        gate = _sigmoid(gl_ref[...])
        dgl_ref[...] = (dhv * (pn * w_ref[...]) * gate * (1.0 - gate)).astype(dgl_ref.dtype)
        de = dhv * gate
        dxh = de * w_ref[...]
        dpe_ref[...] = (r * (dxh - pn * jnp.mean(dxh * pn, axis=-1, keepdims=True))).astype(dpe_ref.dtype)
        part = jnp.sum(de * pn, axis=0, keepdims=True)

        @pl.when(pl.program_id(0) == 0)
        def _():
            dw_ref[...] = part

        @pl.when(pl.program_id(0) > 0)
        def _():
            dw_ref[...] += part

    row = pl.BlockSpec((tr, d), lambda i: (i, 0))
    vec = pl.BlockSpec((1, d), lambda i: (0, 0))
    return pl.pallas_call(
        body,
        grid=(s // tr,),
        in_specs=[row, row, row, vec],
        out_specs=[row, row, vec],
        out_shape=[jax.ShapeDtypeStruct((s, d), BF16), jax.ShapeDtypeStruct((s, d), BF16),
                   jax.ShapeDtypeStruct((1, d), F32)],
        compiler_params=_params(("arbitrary",)),
        name="ple_bwd",
    )(dh, pe, gl, ple_norm)


def _loss_head(h, w, target):
    s, d = h.shape
    tr = _pick(s, (512, 256, 128))

    def body(h_ref, w_ref, t_ref, l_ref, dh_ref, dw_ref):
        hv = h_ref[...]
        r = lax.rsqrt(jnp.mean(hv * hv, axis=-1, keepdims=True) + RMS_EPS)
        hn = hv * r
        diff = hn * w_ref[...] - t_ref[...]
        lpart = jnp.zeros((8, LANES), F32) + (0.5 / d) * jnp.sum(jnp.sum(diff * diff, axis=1, keepdims=True), axis=0, keepdims=True)
        dy = diff * (1.0 / d)
        dxh = dy * w_ref[...]
        dh_ref[...] = r * (dxh - hn * jnp.mean(dxh * hn, axis=-1, keepdims=True))
        part = jnp.sum(dy * hn, axis=0, keepdims=True)

        @pl.when(pl.program_id(0) == 0)
        def _():
            l_ref[...] = lpart
            dw_ref[...] = part

        @pl.when(pl.program_id(0) > 0)
        def _():
            l_ref[...] += lpart
            dw_ref[...] += part

    row = pl.BlockSpec((tr, d), lambda i: (i, 0))
    vec = pl.BlockSpec((1, d), lambda i: (0, 0))
    return pl.pallas_call(
        body,
        grid=(s // tr,),
        in_specs=[row, vec, row],
        out_specs=[pl.BlockSpec((8, LANES), lambda i: (0, 0)), row, vec],
        out_shape=[jax.ShapeDtypeStruct((8, LANES), F32), jax.ShapeDtypeStruct((s, d), F32),
                   jax.ShapeDtypeStruct((1, d), F32)],
        compiler_params=_params(("arbitrary",)),
        name="loss_head",
    )(h, w, target)


PER_LAYER = ("norm_mix", "norm_ffn", "ffn_w_gu", "ffn_w_down", "ple_w_proj", "ple_norm", "ple_gate_norm", "ple_w_gate")


def _pad_lanes(v):
    return jnp.pad(v.astype(F32), (0, LANES - v.shape[0]))[None, :]


def _kernel_layouts(full):
    w = {}
    for k in ("norm_mix", "norm_ffn", "ple_norm", "ple_gate_norm", "ssd_conv_b", "ssd_norm_w", "gmlp_b_in", "gmlp_ln_w",
              "gmlp_ln_b", "gmlp_w_s"):
        w[k] = [full[k][i].astype(F32) for i in range(full[k].shape[0])]
    w["final_norm"] = full["final_norm"].astype(F32)
    n_ssd = full["ssd_w_out"].shape[0]
    if "ssd_w_in" in full:
        w["ssd_w_zx"] = [full["ssd_w_in"][j][:, :SSD_ZX].astype(BF16) for j in range(n_ssd)]
        w["ssd_w_dt"] = [jnp.pad(full["ssd_w_in"][j][:, SSD_ZX:].astype(BF16), ((0, 0), (0, LANES - SSD_HEADS)))
                         for j in range(n_ssd)]
        w["ffn_w_gu"] = [jnp.concatenate([full["ffn_w_gate"][i], full["ffn_w_up"][i]], axis=1).astype(BF16)
                         for i in range(DEPTH)]
    else:
        for k in ("ssd_w_zx", "ssd_w_dt", "ffn_w_gu"):
            w[k] = full[k]
    w["ssd_conv_w"] = [jnp.pad(full["ssd_conv_w"][j].astype(F32), ((0, 8 - CONV_K), (0, 0))) for j in range(n_ssd)]
    for k in ("ssd_dt_bias", "ssd_a_log", "ssd_d"):
        w[k] = [_pad_lanes(full[k][j]) for j in range(n_ssd)]
    w["ssd_w_out"] = [full["ssd_w_out"][j].astype(BF16) for j in range(n_ssd)]
    n_g = full["gmlp_w_in"].shape[0]
    w["gmlp_w_in"] = [full["gmlp_w_in"][j].astype(BF16) for j in range(n_g)]
    w["gmlp_w_out"] = [full["gmlp_w_out"][j].astype(BF16) for j in range(n_g)]
    w["gmlp_b_st"] = [jnp.pad(full["gmlp_b_s"][j].astype(F32).T, ((0, 0), (0, LANES - GMLP_GROUPS))) for j in range(n_g)]
    w["ffn_w_down"] =[full["ffn_w_down"][i].astype(BF16) for i in range(DEPTH)]
    w["ple_w_proj"] = [full["ple_w_proj"][i].astype(BF16) for i in range(DEPTH)]
    w["ple_w_gate"] = [full["ple_w_gate"][i].astype(BF16) for i in range(DEPTH)]
    return w


def _reference_layouts(g, wide=True):
    out = {}
    for k in ("norm_mix", "norm_ffn", "ple_norm", "ple_gate_norm", "ssd_conv_b", "ssd_norm_w", "gmlp_b_in", "gmlp_ln_w",
              "gmlp_ln_b", "gmlp_w_s", "ssd_conv_w", "ssd_dt_bias", "ssd_a_log", "ssd_d", "ssd_w_out", "gmlp_w_in",
              "gmlp_w_out", "ffn_w_down", "ple_w_proj", "ple_w_gate"):
        out[k] = jnp.stack(g[k])
    out["final_norm"] = g["final_norm"]
    out["gmlp_b_s"] = jnp.stack([b[:, :GMLP_GROUPS].T for b in g["gmlp_b_st"]])
    if wide:
        out["ssd_w_in"] = jnp.stack([jnp.concatenate([zx, dt[:, :SSD_HEADS]], axis=1)
                                     for zx, dt in zip(g["ssd_w_zx"], g["ssd_w_dt"])])
        out["ffn_w_gate"] = jnp.stack([gu[:, :FFN_DIM] for gu in g["ffn_w_gu"]])
        out["ffn_w_up"] = jnp.stack([gu[:, FFN_DIM:] for gu in g["ffn_w_gu"]])
    return out


RELAYOUT_ROWS = 128
SSD_SHARD = SSD_IN_DIM // N_DEV
FFN_SHARD = FFN_DIM // N_DEV


def _cat_ssd_in(gathered):
    _, nl, rows, n = gathered.shape
    tr = RELAYOUT_ROWS

    def body(g_ref, *o_refs):
        for j in range(nl):
            full = jnp.concatenate([g_ref[d, j] for d in range(N_DEV)], axis=1)
            o_refs[2 * j][...] = full[:, :SSD_ZX]
            o_refs[2 * j + 1][...] = jnp.concatenate(
                [full[:, SSD_ZX:], jnp.zeros((tr, LANES - SSD_HEADS), full.dtype)], axis=1)

    outs = pl.pallas_call(
        body, grid=(rows // tr,),
        in_specs=[pl.BlockSpec((N_DEV, nl, tr, n), lambda i: (0, 0, i, 0))],
        out_specs=[pl.BlockSpec((tr, SSD_ZX), lambda i: (i, 0)), pl.BlockSpec((tr, LANES), lambda i: (i, 0))] * nl,
        out_shape=[jax.ShapeDtypeStruct((rows, SSD_ZX), BF16), jax.ShapeDtypeStruct((rows, LANES), BF16)] * nl,
        compiler_params=_params(("parallel",)), name="cat_ssd_in",
    )(gathered)
    return [outs[2 * j] for j in range(nl)], [outs[2 * j + 1] for j in range(nl)]


def _split_ssd_in(dzx_list, ddt_list):
    nl = len(dzx_list)
    rows = dzx_list[0].shape[0]
    tr = RELAYOUT_ROWS

    def body(*refs):
        o_ref = refs[2 * nl]
        for j in range(nl):
            full = jnp.concatenate([refs[2 * j][...], refs[2 * j + 1][:, 0:SSD_HEADS]], axis=1)
            for d in range(N_DEV):
                o_ref[d, j] = full[:, d * SSD_SHARD:(d + 1) * SSD_SHARD].astype(o_ref.dtype)

    ins = []
    for j in range(nl):
        ins += [dzx_list[j], ddt_list[j]]
    return pl.pallas_call(
        body, grid=(rows // tr,),
        in_specs=[pl.BlockSpec((tr, SSD_ZX), lambda i: (i, 0)), pl.BlockSpec((tr, LANES), lambda i: (i, 0))] * nl,
        out_specs=pl.BlockSpec((N_DEV, nl, tr, SSD_SHARD), lambda i: (0, 0, i, 0)),
        out_shape=jax.ShapeDtypeStruct((N_DEV, nl, rows, SSD_SHARD), BF16),
        compiler_params=_params(("parallel",)), name="split_ssd_in",
    )(*ins)


def _cat_ffn(g_gate, g_up):
    _, nl, rows, n = g_gate.shape
    tr = RELAYOUT_ROWS

    def body(gg_ref, gu_ref, *o_refs):
        for i in range(nl):
            o_refs[i][...] = jnp.concatenate([gg_ref[d, i] for d in range(N_DEV)] + [gu_ref[d, i] for d in range(N_DEV)],
                                             axis=1)

    blk = pl.BlockSpec((N_DEV, nl, tr, n), lambda i: (0, 0, i, 0))
    outs = pl.pallas_call(
        body, grid=(rows // tr,), in_specs=[blk, blk],
        out_specs=[pl.BlockSpec((tr, 2 * FFN_DIM), lambda i: (i, 0))] * nl,
        out_shape=[jax.ShapeDtypeStruct((rows, 2 * FFN_DIM), BF16)] * nl,
        compiler_params=_params(("parallel",)), name="cat_ffn",
    )(g_gate, g_up)
    return list(outs)


def _split_ffn(dgu_list):
    nl = len(dgu_list)
    rows = dgu_list[0].shape[0]
    tr = RELAYOUT_ROWS

    def body(*refs):
        og_ref, ou_ref = refs[nl], refs[nl + 1]
        for i in range(nl):
            full = refs[i][...]
            for d in range(N_DEV):
                og_ref[d, i] = full[:, d * FFN_SHARD:(d + 1) * FFN_SHARD].astype(og_ref.dtype)
                ou_ref[d, i] = full[:, FFN_DIM + d * FFN_SHARD:FFN_DIM + (d + 1) * FFN_SHARD].astype(ou_ref.dtype)

    blk = pl.BlockSpec((N_DEV, nl, tr, FFN_SHARD), lambda i: (0, 0, i, 0))
    sds = jax.ShapeDtypeStruct((N_DEV, nl, rows, FFN_SHARD), BF16)
    return pl.pallas_call(
        body, grid=(rows // tr,),
        in_specs=[pl.BlockSpec((tr, 2 * FFN_DIM), lambda i: (i, 0))] * nl,
        out_specs=[blk, blk], out_shape=[sds, sds],
        compiler_params=_params(("parallel",)), name="split_ffn",
    )(*dgu_list)


def _local_step(x, p, target, w):
    saved = []
    h = x
    for i in range(DEPTH):
        j = i // 2
        sv = {"h0": h}
        hn = _rms_fwd(h, w["norm_mix"][i][None, :])
        sv["hn"] = hn
        if i % 2 == 0:
            zx = _mm(hn, w["ssd_w_zx"][j], "nn", BF16)
            dtp = _mm(hn, w["ssd_w_dt"][j], "nn", F32)
            xbc = _ssd_conv_fwd(zx, w["ssd_conv_w"][j], w["ssd_conv_b"][j][None, :])
            y, prev = _ssd_scan_fwd(xbc, dtp, w["ssd_dt_bias"][j], w["ssd_a_log"][j], w["ssd_d"][j])
            gn = _ssd_gate_fwd(y, zx, w["ssd_norm_w"][j][None, :])
            h = _mm(gn, w["ssd_w_out"][j], "nn", F32, add=h)
            sv.update(zx=zx, dtp=dtp, xbc=xbc, y=y, prev=prev, gn=gn)
        else:
            pre = _mm(hn, w["gmlp_w_in"][j], "nn", BF16)
            u, v = _gmlp_act_fwd(pre, w["gmlp_b_in"][j][None, :], w["gmlp_ln_w"][j][None, :], w["gmlp_ln_b"][j][None, :])
            gated = _gmlp_mix_fwd(u, v, w["gmlp_w_s"][j], w["gmlp_b_st"][j])
            h = _mm(gated, w["gmlp_w_out"][j], "nn", F32, add=h)
            sv.update(pre=pre, u=u, v=v, gated=gated)
        sv["h1"] = h
        un = _rms_fwd(h, w["norm_ffn"][i][None, :])
        gu = _mm(un, w["ffn_w_gu"][i], "nn", BF16)
        hid = _swiglu_fwd(gu)
        h = _mm(hid, w["ffn_w_down"][i], "nn", F32, add=h)
        sv.update(un=un, gu=gu, hid=hid, h2=h)
        pe = _mm(p[i], w["ple_w_proj"][i], "nn", F32)
        hg = _rms_fwd(h, w["ple_gate_norm"][i][None, :])
        gl = _mm(hg, w["ple_w_gate"][i], "nn", F32)
        h = _ple_fwd(pe, gl, h, w["ple_norm"][i][None, :])
        sv.update(pe=pe, hg=hg, gl=gl)
        saved.append(sv)

    lpart, dh, d_final = _loss_head(h, w["final_norm"][None, :], target)
    g = {k: [None] * (DEPTH if k in PER_LAYER else DEPTH // 2) for k in w if k != "final_norm"}
    g["final_norm"] = d_final[0]

    for i in reversed(range(DEPTH)):
        j = i // 2
        sv = saved[i]
        dgl, dpe, d_ple_norm = _ple_bwd(dh, sv["pe"], sv["gl"], w["ple_norm"][i][None, :])
        g["ple_norm"][i] = d_ple_norm[0]
        g["ple_w_gate"][i] = _mm(sv["hg"], dgl, "tn", F32)
        g["ple_w_proj"][i] = _mm(p[i], dpe, "tn", F32)
        dhg = _mm(dgl, w["ple_w_gate"][i], "nt", F32)
        dh, d_gate_norm = _rms_bwd(dhg, sv["h2"], w["ple_gate_norm"][i][None, :], dh)
        g["ple_gate_norm"][i] = d_gate_norm[0]
        dhid = _mm(dh, w["ffn_w_down"][i], "nt", BF16)
        g["ffn_w_down"][i] = _mm(sv["hid"], dh, "tn", F32)
        dgu = _swiglu_bwd(sv["gu"], dhid)
        g["ffn_w_gu"][i] = _mm(sv["un"], dgu, "tn", F32)
        dun = _mm(dgu, w["ffn_w_gu"][i], "nt", F32)
        dh, d_norm_ffn = _rms_bwd(dun, sv["h1"], w["norm_ffn"][i][None, :], dh)
        g["norm_ffn"][i] = d_norm_ffn[0]
        if i % 2 == 0:
            dgn = _mm(dh, w["ssd_w_out"][j], "nt", F32)
            g["ssd_w_out"][j] = _mm(sv["gn"], dh, "tn", F32)
            dy, dzx, d_norm_w = _ssd_gate_bwd(dgn, sv["y"], sv["zx"], w["ssd_norm_w"][j][None, :])
            g["ssd_norm_w"][j] = d_norm_w[0]
            dxbc, ddtp, d_bias, d_alog, d_d = _ssd_scan_bwd(sv["xbc"], sv["dtp"], sv["prev"], dy, w["ssd_dt_bias"][j],
                                                            w["ssd_a_log"][j], w["ssd_d"][j])
            g["ssd_dt_bias"][j] = d_bias[0, :SSD_HEADS]
            g["ssd_a_log"][j] = d_alog[0, :SSD_HEADS]
            g["ssd_d"][j] = d_d[0, :SSD_HEADS]
            dzx, d_conv_w, d_conv_b = _ssd_conv_bwd(sv["zx"], dxbc, w["ssd_conv_w"][j], w["ssd_conv_b"][j][None, :], dzx)
            g["ssd_conv_w"][j] = d_conv_w[:CONV_K]
            g["ssd_conv_b"][j] = d_conv_b[0]
            g["ssd_w_zx"][j] = _mm(sv["hn"], dzx, "tn", F32)
            g["ssd_w_dt"][j] = _mm(sv["hn"], ddtp, "tn", F32)
            dhn = _mm(dzx, w["ssd_w_zx"][j], "nt", F32)
            dhn = _mm(ddtp, w["ssd_w_dt"][j], "nt", F32, add=dhn)
        else:
            dgated = _mm(dh, w["gmlp_w_out"][j], "nt", BF16)
            g["gmlp_w_out"][j] = _mm(sv["gated"], dh, "tn", F32)
            du, dv, d_ws, d_bst = _gmlp_mix_bwd(dgated, sv["u"], sv["v"], w["gmlp_w_s"][j], w["gmlp_b_st"][j])
            g["gmlp_w_s"][j] = d_ws
            g["gmlp_b_st"][j] = d_bst
            dpre, d_bin, d_lnw, d_lnb = _gmlp_act_bwd(sv["pre"], w["gmlp_b_in"][j][None, :], w["gmlp_ln_w"][j][None, :],
                                                     du, dv)
            g["gmlp_b_in"][j] = d_bin[0]
            g["gmlp_ln_w"][j] = d_lnw[0]
            g["gmlp_ln_b"][j] = d_lnb[0]
            g["gmlp_w_in"][j] = _mm(sv["hn"], dpre, "tn", F32)
            dhn = _mm(dpre, w["gmlp_w_in"][j], "nt", F32)
        dh, d_norm_mix = _rms_bwd(dhn, sv["h0"], w["norm_mix"][i][None, :], dh)
        g["norm_mix"][i] = d_norm_mix[0]
    return lpart[0, 0], dh, g


PACK_COLS = 1024
ANY = pl.BlockSpec(memory_space=pl.ANY)


def _mesh_pos():
    return lax.axis_index("x"), lax.axis_index("y"), lax.axis_index("c")


def _all_gather(xs_list, name):
    n = len(xs_list)

    def body(*refs):
        x_refs, out_refs = refs[:n], refs[n:2 * n]
        send_sems, recv_sems, local_sems = refs[2 * n:]
        x, y, c = _mesh_pos()
        me, sibling = (x, y, c), (x, y, 1 - c)
        chips = [(1 - x, y), (x, 1 - y), (1 - x, 1 - y)]

        def copy(a, k, block, to, from_input=False):
            px, py, pc = block
            dst = out_refs[a].at[4 * px + 2 * py + pc]
            return pltpu.make_async_remote_copy(
                src_ref=x_refs[a] if from_input else dst, dst_ref=dst,
                send_sem=send_sems.at[7 * a + k], recv_sem=recv_sems.at[7 * a + k], device_id=to,
                device_id_type=MESH_ID)

        mine = [pltpu.make_async_copy(x_refs[a], out_refs[a].at[4 * x + 2 * y + c], local_sems.at[a]) for a in range(n)]
        for cp in mine:
            cp.start()
        first = []
        for a in range(n):
            first += [copy(a, 1 + j, me, (*chip, c), from_input=True) for j, chip in enumerate(chips)]
            first.append(copy(a, 0, me, sibling, from_input=True))
        for cp in first:
            cp.start()
        passed = []
        for a in range(n):
            for j, chip in enumerate(chips):
                copy(a, 1 + j, (*chip, c), me).wait_recv()
                fwd = copy(a, 4 + j, (*chip, c), sibling)
                fwd.start()
                passed.append(fwd)
        for a in range(n):
            copy(a, 0, sibling, me).wait_recv()
            for j, chip in enumerate(chips):
                copy(a, 4 + j, (*chip, 1 - c), me).wait_recv()
        for cp in first + passed:
            cp.wait_send()
        for cp in mine:
            cp.wait()

    outs = pl.pallas_call(
        body,
        out_shape=[jax.ShapeDtypeStruct((N_DEV,) + t.shape, t.dtype) for t in xs_list],
        in_specs=[ANY] * n,
        out_specs=[ANY] * n,
        scratch_shapes=[pltpu.SemaphoreType.DMA((7 * n,)), pltpu.SemaphoreType.DMA((7 * n,)),
                        pltpu.SemaphoreType.DMA((n,))],
        name=name,
    )(*xs_list)
    return list(outs)


def _exchange_sibling(send_list):
    n = len(send_list)

    def body(*refs):
        s_refs, land_refs = refs[:n], refs[n:2 * n]
        send_sems, recv_sems = refs[2 * n:]
        x, y, c = _mesh_pos()
        cps = [pltpu.make_async_remote_copy(src_ref=s_refs[a], dst_ref=land_refs[a], send_sem=send_sems.at[a],
                                            recv_sem=recv_sems.at[a], device_id=(x, y, 1 - c), device_id_type=MESH_ID)
               for a in range(n)]
        for cp in cps:
            cp.start()
        for cp in cps:
            cp.wait()

    outs = pl.pallas_call(
        body,
        out_shape=[jax.ShapeDtypeStruct(t.shape, t.dtype) for t in send_list],
        in_specs=[ANY] * n,
        out_specs=[ANY] * n,
        scratch_shapes=[pltpu.SemaphoreType.DMA((n,)), pltpu.SemaphoreType.DMA((n,))],
        name="rs_exchange_sibling",
    )(*send_list)
    return list(outs)


def _exchange_chips(partial_list):
    n = len(partial_list)

    def body(*refs):
        p_refs, land_refs = refs[:n], refs[n:2 * n]
        send_sems, recv_sems = refs[2 * n:]
        x, y, c = _mesh_pos()
        chips = [(1 - x, y), (x, 1 - y), (1 - x, 1 - y)]
        cps = [pltpu.make_async_remote_copy(src_ref=p_refs[a].at[2 * cx + cy], dst_ref=land_refs[a].at[j],
                                            send_sem=send_sems.at[3 * a + j], recv_sem=recv_sems.at[3 * a + j],
                                            device_id=(cx, cy, c), device_id_type=MESH_ID)
               for a in range(n) for j, (cx, cy) in enumerate(chips)]
        for cp in cps:
            cp.start()
        for cp in cps:
            cp.wait()

    outs = pl.pallas_call(
        body,
        out_shape=[jax.ShapeDtypeStruct((3,) + t.shape[1:], t.dtype) for t in partial_list],
        in_specs=[ANY] * n,
        out_specs=[ANY] * n,
        scratch_shapes=[pltpu.SemaphoreType.DMA((3 * n,)), pltpu.SemaphoreType.DMA((3 * n,))],
        name="rs_exchange_chips",
    )(*partial_list)
    return list(outs)


def _sum_pairs(a, b):
    shape = a.shape
    a = a.reshape(shape[0], -1, shape[-1])
    b = b.reshape(a.shape)
    n, r, cdim = a.shape
    tr = _pick(r, (256, 128, 64))

    def body(a_ref, b_ref, o_ref):
        o_ref[...] = (a_ref[...].astype(F32) + b_ref[...].astype(F32)).astype(o_ref.dtype)

    blk = pl.BlockSpec((1, tr, cdim), lambda i, j: (i, j, 0))
    return pl.pallas_call(
        body, grid=(n, r // tr), in_specs=[blk, blk], out_specs=blk,
        out_shape=jax.ShapeDtypeStruct(a.shape, a.dtype),
        compiler_params=_params(("parallel", "parallel")), name="rs_sum_pairs",
    )(a, b).reshape(shape)


def _sum_final(own, land):
    shape = own.shape
    own = own.reshape(-1, shape[-1])
    land = land.reshape((3,) + own.shape)
    r, cdim = own.shape
    tr = _pick(r, (256, 128, 64))

    def body(o_ref, l_ref, out_ref):
        acc = o_ref[...].astype(F32)
        for j in range(3):
            acc = acc + l_ref[j].astype(F32)
        out_ref[...] = acc

    return pl.pallas_call(
        body, grid=(r // tr,),
        in_specs=[pl.BlockSpec((tr, cdim), lambda i: (i, 0)), pl.BlockSpec((3, tr, cdim), lambda i: (0, i, 0))],
        out_specs=pl.BlockSpec((tr, cdim), lambda i: (i, 0)),
        out_shape=jax.ShapeDtypeStruct((r, cdim), F32),
        compiler_params=_params(("parallel",)), name="rs_sum_final",
    )(own, land).reshape(shape)


def _sum_devices(gathered):
    n, r, cdim = gathered.shape
    tr = _pick(r, (64, 32, 16, 8))

    def body(g_ref, out_ref):
        acc = g_ref[0]
        for q in range(1, n):
            acc = acc + g_ref[q]
        out_ref[...] = acc

    return pl.pallas_call(
        body, grid=(r // tr,),
        in_specs=[pl.BlockSpec((n, tr, cdim), lambda i: (0, i, 0))],
        out_specs=pl.BlockSpec((tr, cdim), lambda i: (i, 0)),
        out_shape=jax.ShapeDtypeStruct((r, cdim), F32),
        compiler_params=_params(("parallel",)), name="sum_devices",
    )(gathered)


def _adamw(w, g, m, v):
    shape = w.shape
    cols = shape[-1]
    rows = w.size // cols
    tr = _pick(rows, (512, 256, 128, 64, 32, 16, 8))
    c1 = 1.0 - ADAM_B1 ** ADAM_STEP
    c2 = 1.0 - ADAM_B2 ** ADAM_STEP

    def body(w_ref, g_ref, m_ref, v_ref, d_ref, nm_ref, nv_ref):
        gv = g_ref[...]
        m2 = ADAM_B1 * m_ref[...] + (1.0 - ADAM_B1) * gv
        v2 = ADAM_B2 * v_ref[...] + (1.0 - ADAM_B2) * (gv * gv)
        d_ref[...] = -ADAM_LR * ((m2 / c1) / (jnp.sqrt(v2 / c2) + ADAM_EPS) + ADAM_WD * w_ref[...])
        nm_ref[...] = m2
        nv_ref[...] = v2

    blk = pl.BlockSpec((tr, cols), lambda i: (i, 0))
    sds = jax.ShapeDtypeStruct((rows, cols), F32)
    outs = pl.pallas_call(
        body, grid=(rows // tr,), in_specs=[blk] * 4, out_specs=[blk] * 3, out_shape=[sds] * 3,
        compiler_params=_params(("parallel",)), name=f"adamw_{rows}x{cols}",
    )(*(t.reshape(rows, cols) for t in (w, g, m, v)))
    return tuple(o.reshape(shape) for o in outs)


WEIGHTS = ("norm_mix", "norm_ffn", "ssd_w_in", "ssd_conv_w", "ssd_conv_b", "ssd_dt_bias", "ssd_a_log", "ssd_d",
           "ssd_norm_w", "ssd_w_out", "gmlp_w_in", "gmlp_b_in", "gmlp_ln_w", "gmlp_ln_b", "gmlp_w_s", "gmlp_b_s",
           "gmlp_w_out", "ffn_w_gate", "ffn_w_up", "ffn_w_down", "ple_w_proj", "ple_norm", "ple_gate_norm",
           "ple_w_gate", "final_norm")
ARG_NAMES = ("x", "p") + WEIGHTS + ("loss_target",) + tuple("m_" + n for n in WEIGHTS) + tuple("v_" + n for n in WEIGHTS)
SHARD_AXIS = {"ssd_w_in": 2, "ssd_conv_w": 2, "ssd_w_out": 1, "gmlp_w_in": 2, "gmlp_b_in": 1, "gmlp_ln_w": 1,
              "gmlp_ln_b": 1, "gmlp_w_out": 1, "ffn_w_gate": 2, "ffn_w_up": 2, "ffn_w_down": 1, "ple_w_proj": 2,
              "ple_w_gate": 1}
GATHER_BF16 = ("ssd_w_in", "ssd_w_out", "gmlp_w_in", "gmlp_w_out", "ffn_w_gate", "ffn_w_up", "ffn_w_down",
               "ple_w_proj", "ple_w_gate")
GATHER_F32 = ("ssd_conv_w", "gmlp_b_in", "gmlp_ln_w", "gmlp_ln_b")
SHARDED = GATHER_BF16 + GATHER_F32
WIDE = ("ssd_w_in", "ffn_w_gate", "ffn_w_up")
REPLICATED = tuple(n for n in WEIGHTS if n not in SHARD_AXIS)


def _pack(arrs, dtype, row_mult, lead=0):
    flat = jnp.concatenate([t.reshape(t.shape[:lead] + (-1,)).astype(dtype) for t in arrs], axis=lead)
    n = flat.shape[-1]
    unit = row_mult * PACK_COLS
    total = -(-n // unit) * unit
    flat = jnp.pad(flat, [(0, 0)] * lead + [(0, total - n)])
    return flat.reshape(flat.shape[:lead] + (total // PACK_COLS, PACK_COLS))


def _unpack(buf, names, shapes, lead=0):
    flat = buf.reshape(buf.shape[:lead] + (-1,))
    out, off = {}, 0
    for n in names:
        size = math.prod(shapes[n])
        out[n] = lax.slice_in_dim(flat, off, off + size, axis=lead).reshape(buf.shape[:lead] + tuple(shapes[n]))
        off += size
    return out


def _merge_shards(seg, ax):
    t = jnp.moveaxis(seg, 0, ax)
    return t.reshape(t.shape[:ax] + (t.shape[ax] * t.shape[ax + 1],) + t.shape[ax + 2:])


def _split_for_cores(gfull, ax, c):
    shp = gfull.shape
    t = gfull.reshape(shp[:ax] + (2, 2, 2, shp[ax] // N_DEV) + shp[ax + 1:])

    def take(core):
        u = lax.dynamic_index_in_dim(t, core, axis=ax + 2, keepdims=False)
        u = jnp.moveaxis(u, (ax, ax + 1), (0, 1))
        return u.reshape((4,) + u.shape[2:])

    return take(c), take(1 - c)


def kernel(x, p, norm_mix, norm_ffn, ssd_w_in, ssd_conv_w, ssd_conv_b, ssd_dt_bias, ssd_a_log, ssd_d,
           ssd_norm_w, ssd_w_out, gmlp_w_in, gmlp_b_in, gmlp_ln_w, gmlp_ln_b, gmlp_w_s, gmlp_b_s,
           gmlp_w_out, ffn_w_gate, ffn_w_up, ffn_w_down, ple_w_proj, ple_norm, ple_gate_norm, ple_w_gate,
           final_norm, loss_target, m_norm_mix, m_norm_ffn, m_ssd_w_in, m_ssd_conv_w, m_ssd_conv_b,
           m_ssd_dt_bias, m_ssd_a_log, m_ssd_d, m_ssd_norm_w, m_ssd_w_out, m_gmlp_w_in, m_gmlp_b_in,
           m_gmlp_ln_w, m_gmlp_ln_b, m_gmlp_w_s, m_gmlp_b_s, m_gmlp_w_out, m_ffn_w_gate, m_ffn_w_up,
           m_ffn_w_down, m_ple_w_proj, m_ple_norm, m_ple_gate_norm, m_ple_w_gate, m_final_norm, v_norm_mix,
           v_norm_ffn, v_ssd_w_in, v_ssd_conv_w, v_ssd_conv_b, v_ssd_dt_bias, v_ssd_a_log, v_ssd_d,
           v_ssd_norm_w, v_ssd_w_out, v_gmlp_w_in, v_gmlp_b_in, v_gmlp_ln_w, v_gmlp_ln_b, v_gmlp_w_s,
           v_gmlp_b_s, v_gmlp_w_out, v_ffn_w_gate, v_ffn_w_up, v_ffn_w_down, v_ple_w_proj, v_ple_norm,
           v_ple_gate_norm, v_ple_w_gate, v_final_norm):
    given = locals()
    a = {n: given[n] for n in ARG_NAMES}
    mx, my, c = _mesh_pos()
    xs = a["x"][0]
    ps = a["p"][:, 0]
    target = a["loss_target"][0]
    shard_shapes = {n: a[n].shape for n in WEIGHTS}

    full = {n: a[n] for n in REPLICATED}
    packed = tuple(n for n in GATHER_BF16 if n not in WIDE)
    rs_packed = tuple(n for n in SHARDED if n not in WIDE)
    got = _all_gather([_pack([a[n] for n in packed], BF16, 256), _pack([a[n] for n in GATHER_F32], F32, 8)]
                      + [a[n].astype(BF16) for n in WIDE], "ag_weights")
    for n, seg in _unpack(got[0], packed, shard_shapes, lead=1).items():
        full[n] = _merge_shards(seg, SHARD_AXIS[n])
    for n, seg in _unpack(got[1], GATHER_F32, shard_shapes, lead=1).items():
        full[n] = _merge_shards(seg, SHARD_AXIS[n])
    full["ssd_w_zx"], full["ssd_w_dt"] = _cat_ssd_in(got[2])
    full["ffn_w_gu"] = _cat_ffn(got[3], got[4])

    lpart, dx, g = _local_step(xs, ps, target, _kernel_layouts(full))
    gfull = _reference_layouts(g, wide=False)
    loss = lax.psum(lpart, ("x", "y", "c"))

    def by_core(t):
        u = t.reshape((4, 2) + t.shape[1:])
        return (lax.dynamic_index_in_dim(u, c, axis=1, keepdims=False),
                lax.dynamic_index_in_dim(u, 1 - c, axis=1, keepdims=False))

    halves = [_split_for_cores(gfull[n], SHARD_AXIS[n], c) for n in rs_packed]
    pairs = [(_pack([h[0] for h in halves], BF16, 256, lead=1), _pack([h[1] for h in halves], BF16, 256, lead=1))]
    pairs += [by_core(t) for t in (_split_ssd_in(g["ssd_w_zx"], g["ssd_w_dt"]),) + tuple(_split_ffn(g["ffn_w_gu"]))]
    landed = _exchange_sibling([s for _, s in pairs])
    partials = [_sum_pairs(k, l) for (k, _), l in zip(pairs, landed)]
    landed = _exchange_chips(partials)
    sums = [_sum_final(lax.dynamic_index_in_dim(t, 2 * mx + my, axis=0, keepdims=False), l)
            for t, l in zip(partials, landed)]
    gshard = _unpack(sums[0], rs_packed, shard_shapes)
    gshard.update(zip(WIDE, sums[1:]))
    rep = _all_gather([_pack([gfull[n] for n in REPLICATED], F32, 64)], "ag_replicated_grads")[0]
    grep = _unpack(_sum_devices(rep), REPLICATED, shard_shapes)
    grads = {**gshard, **grep}

    upd = {n: _adamw(a[n], grads[n], a["m_" + n], a["v_" + n]) for n in WEIGHTS}
    return (loss, dx[None], *[grads[n] for n in WEIGHTS], *[upd[n][0] for n in WEIGHTS],
            *[upd[n][1] for n in WEIGHTS], *[upd[n][2] for n in WEIGHTS])
```

```python
import functools
import math

import jax
import jax.numpy as jnp
from jax import lax
from jax.experimental import pallas as pl
from jax.experimental.pallas import tpu as pltpu

F32 = jnp.float32
BF16 = jnp.bfloat16

N_DEV = 8
D_MODEL = 1024
DEPTH = 4
SSD_INNER = 2048
SSD_HEADS = 32
SSD_HEADDIM = 64
SSD_GROUPS = 8
SSD_STATE = 128
SSD_GROUP_W = SSD_INNER // SSD_GROUPS
SSD_CONV_DIM = SSD_INNER + 2 * SSD_GROUPS * SSD_STATE
SSD_IN_DIM = 2 * SSD_INNER + SSD_CONV_DIM - SSD_INNER + SSD_HEADS
SSD_ZX = SSD_INNER + SSD_CONV_DIM
CONV_K = 4
CHUNK = 128
GMLP_INNER = 2048
GMLP_GROUPS = 16
FFN_DIM = 2816
PLE_DIM = 256
RMS_EPS = 1e-6
LN_EPS = 1e-5
LANES = 128
VMEM_LIMIT = 56 * 1024 * 1024

ADAM_LR = 0.001
ADAM_B1 = 0.9
ADAM_B2 = 0.999
ADAM_EPS = 1e-08
ADAM_WD = 0.01
ADAM_STEP = 10

MESH_ID = pl.DeviceIdType.MESH


def _pick(n, cands):
    for c in cands:
        if c <= n and n % c == 0:
            return c
    return n


def _params(dims):
    return pltpu.CompilerParams(dimension_semantics=dims, vmem_limit_bytes=VMEM_LIMIT)


def _dot(a, b, dims=(((1,), (0,)), ((), ())), precision=None):
    return lax.dot_general(a, b, dims, precision=precision, preferred_element_type=F32)


NN = (((1,), (0,)), ((), ()))
NT = (((1,), (1,)), ((), ()))
TN = (((0,), (0,)), ((), ()))


def _sigmoid(x):
    return 1.0 / (1.0 + jnp.exp(-x))


def _dot01(a, b, dims, split):
    v = (a, b)[split]
    hi = v.astype(BF16)
    r1 = v - hi.astype(F32)
    mid = r1.astype(BF16)
    lo = (r1 - mid.astype(F32)).astype(BF16)
    ones = (a, b)[1 - split].astype(BF16)
    terms = [(t, ones) if split == 0 else (ones, t) for t in (hi, mid, lo)]
    return _dot(*terms[0], dims) + _dot(*terms[1], dims) + _dot(*terms[2], dims)


MM_VMEM_BUDGET = 36 * 1024 * 1024


def _mm_tiles(mode, m, n, k, a_bytes, b_bytes, out_bytes, has_add):
    tm = _pick(m, (1408, 1024, 512, 256, 128))
    tn_cands = [c for c in (2816, 1024, 512, 256, 128) if c <= n and n % c == 0] or [n]
    tk_cands = [k] + [c for c in (2816, 2048, 1024, 512, 256, 128) if c < k and k % c == 0]
    for tk in tk_cands:
        for tn in tn_cands:
            blocks = tm * tk * a_bytes + tk * tn * b_bytes + tm * tn * (out_bytes + (4 if has_add else 0))
            if 2 * blocks + (tm * tn * 4 if tk < k else 0) <= MM_VMEM_BUDGET:
                return tm, tn, tk
    return tm, tn_cands[-1], tk_cands[-1]


def _mm(a, b, mode, out_dtype, add=None):
    if mode == "nn":
        m, k = a.shape
        n = b.shape[1]
    elif mode == "nt":
        m, k = a.shape
        n = b.shape[0]
    else:
        k, m = a.shape
        n = b.shape[1]
    tm, tn, tk = _mm_tiles(mode, m, n, k, a.dtype.itemsize, b.dtype.itemsize, jnp.dtype(out_dtype).itemsize,
                           add is not None)
    nk = k // tk
    dims = {"nn": NN, "nt": NT, "tn": TN}[mode]

    def body(*refs):
        if add is None:
            a_ref, b_ref, o_ref = refs[:3]
            add_ref = None
            rest = refs[3:]
        else:
            a_ref, b_ref, add_ref, o_ref = refs[:4]
            rest = refs[4:]
        part = _dot(a_ref[...].astype(BF16), b_ref[...].astype(BF16), dims)

        def finish(acc):
            if add_ref is not None:
                acc = acc + add_ref[...]
            o_ref[...] = acc.astype(o_ref.dtype)

        if nk == 1:
            finish(part)
        else:
            acc_ref = rest[0]
            kk = pl.program_id(2)

            @pl.when(kk == 0)
            def _():
                acc_ref[...] = part

            @pl.when(kk > 0)
            def _():
                acc_ref[...] += part

            @pl.when(kk == nk - 1)
            def _():
                finish(acc_ref[...])

    if mode == "nn":
        a_spec = pl.BlockSpec((tm, tk), lambda i, j, kk: (i, kk))
        b_spec = pl.BlockSpec((tk, tn), lambda i, j, kk: (kk, j))
    elif mode == "nt":
        a_spec = pl.BlockSpec((tm, tk), lambda i, j, kk: (i, kk))
        b_spec = pl.BlockSpec((tn, tk), lambda i, j, kk: (j, kk))
    else:
        a_spec = pl.BlockSpec((tk, tm), lambda i, j, kk: (kk, i))
        b_spec = pl.BlockSpec((tk, tn), lambda i, j, kk: (kk, j))
    o_spec = pl.BlockSpec((tm, tn), lambda i, j, kk: (i, j))
    in_specs = [a_spec, b_spec] + ([o_spec] if add is not None else [])
    args = (a, b) + ((add,) if add is not None else ())
    return pl.pallas_call(
        body,
        grid=(m // tm, n // tn, nk),
        in_specs=in_specs,
        out_specs=o_spec,
        out_shape=jax.ShapeDtypeStruct((m, n), out_dtype),
        scratch_shapes=[pltpu.VMEM((tm, tn), F32)] if nk > 1 else [],
        compiler_params=_params(("parallel", "parallel", "arbitrary")),
        name=f"mm_{mode}_{m}x{k}x{n}",
    )(*args)


def _rms_fwd(x, w):
    s, d = x.shape
    tr = _pick(s, (512, 256, 128))

    def body(x_ref, w_ref, o_ref):
        xv = x_ref[...]
        r = lax.rsqrt(jnp.mean(xv * xv, axis=-1, keepdims=True) + RMS_EPS)
        o_ref[...] = (xv * r * w_ref[...]).astype(o_ref.dtype)

    return pl.pallas_call(
        body,
        grid=(s // tr,),
        in_specs=[pl.BlockSpec((tr, d), lambda i: (i, 0)), pl.BlockSpec((1, d), lambda i: (0, 0))],
        out_specs=pl.BlockSpec((tr, d), lambda i: (i, 0)),
        out_shape=jax.ShapeDtypeStruct((s, d), BF16),
        compiler_params=_params(("parallel",)),
        name="rms_fwd",
    )(x, w)


def _rms_bwd(dyn, x, w, add):
    s, d = x.shape
    tr = _pick(s, (512, 256, 128))

    def body(dy_ref, x_ref, w_ref, add_ref, dx_ref, dw_ref):
        xv = x_ref[...]
        dy = dy_ref[...].astype(F32)
        r = lax.rsqrt(jnp.mean(xv * xv, axis=-1, keepdims=True) + RMS_EPS)
        xn = xv * r
        dxh = dy * w_ref[...]
        dx = r * (dxh - xn * jnp.mean(dxh * xn, axis=-1, keepdims=True))
        dx_ref[...] = add_ref[...] + dx
        part = jnp.sum(dy * xn, axis=0, keepdims=True)

        @pl.when(pl.program_id(0) == 0)
        def _():
            dw_ref[...] = part

        @pl.when(pl.program_id(0) > 0)
        def _():
            dw_ref[...] += part

    row = pl.BlockSpec((tr, d), lambda i: (i, 0))
    vec = pl.BlockSpec((1, d), lambda i: (0, 0))
    return pl.pallas_call(
        body,
        grid=(s // tr,),
        in_specs=[row, row, vec, row],
        out_specs=[row, vec],
        out_shape=[jax.ShapeDtypeStruct((s, d), F32), jax.ShapeDtypeStruct((1, d), F32)],
        compiler_params=_params(("arbitrary",)),
        name="rms_bwd",
    )(dyn, x, w, add)


CONV_ROWS = 256
CONV_COLS = 256
CONV_HALO = 16


def _conv_taps(ext, w, base, rows):
    acc = w[0:1, :] * ext[base:base + rows]
    for k in range(1, CONV_K):
        acc = acc + w[k:k + 1, :] * ext[base + k:base + k + rows]
    return acc


def _ssd_conv_fwd(zx, conv_w, conv_b):
    s = zx.shape[0]
    c = SSD_CONV_DIM
    nsteps = s // CONV_ROWS
    off = SSD_INNER // CONV_COLS

    def body(x_ref, w_ref, b_ref, o_ref):
        w = w_ref[...]
        b = b_ref[...]

        def step(i, carry):
            r0 = pl.multiple_of(i * CONV_ROWS, CONV_ROWS)
            cur = x_ref[pl.ds(r0, CONV_ROWS), :].astype(F32)
            p0 = pl.multiple_of(jnp.maximum(r0 - CONV_HALO, 0), CONV_HALO)
            prev = x_ref[pl.ds(p0, CONV_HALO), :].astype(F32)
            prev = jnp.where(i == 0, 0.0, prev)
            ext = jnp.concatenate([prev, cur], axis=0)
            acc = _conv_taps(ext, w, CONV_HALO - (CONV_K - 1), CONV_ROWS) + b
            o_ref[pl.ds(r0, CONV_ROWS), :] = (acc * _sigmoid(acc)).astype(o_ref.dtype)
            return carry

        lax.fori_loop(0, nsteps, step, 0)

    return pl.pallas_call(
        body,
        grid=(c // CONV_COLS,),
        in_specs=[pl.BlockSpec((s, CONV_COLS), lambda j: (0, j + off)),
                  pl.BlockSpec((8, CONV_COLS), lambda j: (0, j)),
                  pl.BlockSpec((1, CONV_COLS), lambda j: (0, j))],
        out_specs=pl.BlockSpec((s, CONV_COLS), lambda j: (0, j)),
        out_shape=jax.ShapeDtypeStruct((s, c), BF16),
        compiler_params=_params(("parallel",)),
        name="ssd_conv_fwd",
    )(zx, conv_w, conv_b)


def _ssd_conv_bwd(zx, dxbc, conv_w, conv_b, dzx):
    s = zx.shape[0]
    c = SSD_CONV_DIM
    nsteps = s // CONV_ROWS
    off = SSD_INNER // CONV_COLS

    def body(x_ref, dy_ref, w_ref, b_ref, dzx_in_ref, dx_ref, dw_ref, db_ref, dc_ref):
        w = w_ref[...]
        b = b_ref[...]
        dc_ref[pl.ds(s, CONV_HALO), :] = jnp.zeros((CONV_HALO, CONV_COLS), F32)

        def step1(i, carry):
            dw0, dw1, dw2, dw3, dbs = carry
            r0 = pl.multiple_of(i * CONV_ROWS, CONV_ROWS)
            cur = x_ref[pl.ds(r0, CONV_ROWS), :].astype(F32)
            p0 = pl.multiple_of(jnp.maximum(r0 - CONV_HALO, 0), CONV_HALO)
            prev = x_ref[pl.ds(p0, CONV_HALO), :].astype(F32)
            prev = jnp.where(i == 0, 0.0, prev)
            ext = jnp.concatenate([prev, cur], axis=0)
            base = CONV_HALO - (CONV_K - 1)
            acc = _conv_taps(ext, w, base, CONV_ROWS) + b
            sg = _sigmoid(acc)
            dcv = dy_ref[pl.ds(r0, CONV_ROWS), :].astype(F32) * (sg * (1.0 + acc * (1.0 - sg)))
            dc_ref[pl.ds(r0, CONV_ROWS), :] = dcv
            dws = [jnp.sum(dcv * ext[base + k:base + k + CONV_ROWS], axis=0, keepdims=True) for k in range(CONV_K)]
            return (dw0 + dws[0], dw1 + dws[1], dw2 + dws[2], dw3 + dws[3], dbs + jnp.sum(dcv, axis=0, keepdims=True))

        z = jnp.zeros((1, CONV_COLS), F32)
        dw0, dw1, dw2, dw3, dbs = lax.fori_loop(0, nsteps, step1, (z, z, z, z, z))
        dw_ref[...] = jnp.concatenate([dw0, dw1, dw2, dw3, z, z, z, z], axis=0)
        db_ref[...] = dbs

        def step2(i, carry):
            r0 = pl.multiple_of(i * CONV_ROWS, CONV_ROWS)
            ext = dc_ref[pl.ds(r0, CONV_ROWS + CONV_HALO), :]
            acc = w[0:1, :] * ext[CONV_K - 1:CONV_K - 1 + CONV_ROWS]
            for k in range(1, CONV_K):
                acc = acc + w[k:k + 1, :] * ext[CONV_K - 1 - k:CONV_K - 1 - k + CONV_ROWS]
            dx_ref[pl.ds(r0, CONV_ROWS), :] = acc.astype(dx_ref.dtype)
            return carry

        lax.fori_loop(0, nsteps, step2, 0)

    col = pl.BlockSpec((s, CONV_COLS), lambda j: (0, j))
    shifted = pl.BlockSpec((s, CONV_COLS), lambda j: (0, j + off))
    return pl.pallas_call(
        body,
        grid=(c // CONV_COLS,),
        in_specs=[shifted, col,
                  pl.BlockSpec((8, CONV_COLS), lambda j: (0, j)),
                  pl.BlockSpec((1, CONV_COLS), lambda j: (0, j)),
                  pl.BlockSpec(memory_space=pl.ANY)],
        out_specs=[shifted, pl.BlockSpec((8, CONV_COLS), lambda j: (0, j)), pl.BlockSpec((1, CONV_COLS), lambda j: (0, j))],
        out_shape=[jax.ShapeDtypeStruct((s, SSD_ZX), BF16), jax.ShapeDtypeStruct((8, c), F32),
                   jax.ShapeDtypeStruct((1, c), F32)],
        scratch_shapes=[pltpu.VMEM((s + CONV_HALO, CONV_COLS), F32)],
        input_output_aliases={4: 0},
        compiler_params=_params(("parallel",)),
        name="ssd_conv_bwd",
    )(zx, dxbc, conv_w, conv_b, dzx)


def _ssd_consts():
    li = lax.broadcasted_iota(jnp.int32, (CHUNK, CHUNK), 0)
    si = lax.broadcasted_iota(jnp.int32, (CHUNK, CHUNK), 1)
    tril = li >= si
    hrow = lax.broadcasted_iota(jnp.int32, (LANES, SSD_INNER), 0)
    hcol = lax.broadcasted_iota(jnp.int32, (LANES, SSD_INNER), 1) // SSD_HEADDIM
    expand = (hrow == hcol).astype(F32)
    return tril, expand


def _ssd_chunk_common(dtp_ref, bias_ref, alog_ref, tril, expand):
    lane = lax.broadcasted_iota(jnp.int32, (1, LANES), 1)
    valid = lane < SSD_HEADS
    pre = dtp_ref[...] + bias_ref[...]
    dt = jnp.where(valid, jnp.maximum(pre, 0.0) + jnp.log1p(jnp.exp(-jnp.abs(pre))), 0.0)
    a = jnp.where(valid, -jnp.exp(alog_ref[...]), 0.0)
    da = dt * a
    cs = _dot01(tril.astype(F32), da, NN, 1)
    cs_x = _dot01(cs, expand, NN, 0)
    dt_x = _dot01(dt, expand, NN, 0)
    return pre, dt, a, cs, cs_x, dt_x


def _ssd_scan_fwd(xbc, dtp, dt_bias, a_log, d_skip):
    s = xbc.shape[0]
    nc = s // CHUNK
    gw = SSD_GROUP_W

    def body(xbc_ref, dtp_ref, bias_ref, alog_ref, d_ref, y_ref, prev_ref, state_ref):
        c = pl.program_id(0)

        @pl.when(c == 0)
        def _():
            state_ref[...] = jnp.zeros_like(state_ref)

        tril, expand = _ssd_consts()
        pre, dt, a, cs, cs_x, dt_x = _ssd_chunk_common(dtp_ref, bias_ref, alog_ref, tril, expand)
        cs_t = cs.T
        d_x = _dot01(jnp.broadcast_to(d_ref[...], (8, LANES)), expand, NN, 0)[0:1, :]
        cs_last = cs_x[CHUNK - 1:CHUNK, :]
        dec_out = jnp.exp(cs_x)
        dec_st = jnp.exp(cs_last - cs_x)
        dec_ch = jnp.exp(cs_last)
        x = xbc_ref[:, 0:SSD_INNER].astype(F32)
        xr = x * dt_x
        xrs = xr * dec_st
        lane_g = lax.broadcasted_iota(jnp.int32, (1, gw), 1) // SSD_HEADDIM
        for g in range(SSD_GROUPS):
            sl = slice(g * gw, (g + 1) * gw)
            bg = xbc_ref[:, SSD_INNER + g * SSD_STATE:SSD_INNER + (g + 1) * SSD_STATE]
            cg = xbc_ref[:, SSD_INNER + (SSD_GROUPS + g) * SSD_STATE:SSD_INNER + (SSD_GROUPS + g + 1) * SSD_STATE]
            cb = _dot(cg, bg, NT)
            prev_g = state_ref[:, sl]
            prev_ref[0, :, sl] = prev_g
            yo = _dot(cg, prev_g.astype(BF16), NN) * dec_out[:, sl]
            xr_g = xr[:, sl]
            yd = jnp.zeros((CHUNK, gw), F32)
            for r in range(SSD_HEADS // SSD_GROUPS):
                h = g * (SSD_HEADS // SSD_GROUPS) + r
                diff = cs[:, h:h + 1] - cs_t[h:h + 1, :]
                lmat = jnp.exp(jnp.where(tril, diff, -1e30))
                wmat = (cb * lmat).astype(BF16)
                xr_h = jnp.where(lane_g == r, xr_g, 0.0).astype(BF16)
                yd = yd + _dot(wmat, xr_h, NN)
            y_ref[:, sl] = yd + yo + x[:, sl] * d_x[:, sl]
            sc = _dot(bg, xrs[:, sl].astype(BF16), TN)
            state_ref[:, sl] = prev_g * dec_ch[:, sl] + sc

    vec = pl.BlockSpec((1, LANES), lambda c: (0, 0))
    return pl.pallas_call(
        body,
        grid=(nc,),
        in_specs=[pl.BlockSpec((CHUNK, SSD_CONV_DIM), lambda c: (c, 0)),
                  pl.BlockSpec((CHUNK, LANES), lambda c: (c, 0)), vec, vec, vec],
        out_specs=[pl.BlockSpec((CHUNK, SSD_INNER), lambda c: (c, 0)),
                   pl.BlockSpec((1, SSD_STATE, SSD_INNER), lambda c: (c, 0, 0))],
        out_shape=[jax.ShapeDtypeStruct((s, SSD_INNER), F32), jax.ShapeDtypeStruct((nc, SSD_STATE, SSD_INNER), F32)],
        scratch_shapes=[pltpu.VMEM((SSD_STATE, SSD_INNER), F32)],
        compiler_params=_params(("arbitrary",)),
        name="ssd_scan_fwd",
    )(xbc, dtp, dt_bias, a_log, d_skip)


def _ssd_scan_bwd(xbc, dtp, prev, dy, dt_bias, a_log, d_skip):
    s = xbc.shape[0]
    nc = s // CHUNK
    gw = SSD_GROUP_W
    hpg = SSD_HEADS // SSD_GROUPS

    def body(xbc_ref, dtp_ref, prev_ref, dy_ref, bias_ref, alog_ref, d_ref,
             dxbc_ref, ddtp_ref, dbias_ref, dalog_ref, dd_ref, dp_ref, ddx_ref):
        step = pl.program_id(0)

        @pl.when(step == 0)
        def _():
            dp_ref[...] = jnp.zeros_like(dp_ref)
            ddx_ref[...] = jnp.zeros_like(ddx_ref)
            dbias_ref[...] = jnp.zeros_like(dbias_ref)
            dalog_ref[...] = jnp.zeros_like(dalog_ref)

        tril, expand = _ssd_consts()
        pre, dt, a, cs, cs_x, dt_x = _ssd_chunk_common(dtp_ref, bias_ref, alog_ref, tril, expand)
        cs_t = cs.T
        d_x = _dot01(jnp.broadcast_to(d_ref[...], (8, LANES)), expand, NN, 0)[0:1, :]
        cs_last = cs_x[CHUNK - 1:CHUNK, :]
        dec_out = jnp.exp(cs_x)
        dec_st = jnp.exp(cs_last - cs_x)
        dec_ch = jnp.exp(cs_last)
        x = xbc_ref[:, 0:SSD_INNER].astype(F32)
        dyv = dy_ref[...]
        xr = x * dt_x
        xrs = xr * dec_st
        lane_g = lax.broadcasted_iota(jnp.int32, (1, gw), 1) // SSD_HEADDIM
        hsel = lax.broadcasted_iota(jnp.int32, (CHUNK, LANES), 1)
        dcs = jnp.zeros((CHUNK, LANES), F32)
        last_parts = []
        t_parts = []
        dxr_parts = []
        for g in range(SSD_GROUPS):
            sl = slice(g * gw, (g + 1) * gw)
            bsl = slice(SSD_INNER + g * SSD_STATE, SSD_INNER + (g + 1) * SSD_STATE)
            csl = slice(SSD_INNER + (SSD_GROUPS + g) * SSD_STATE, SSD_INNER + (SSD_GROUPS + g + 1) * SSD_STATE)
            bg = xbc_ref[:, bsl]
            cg = xbc_ref[:, csl]
            cb = _dot(cg, bg, NT)
            prev_g = prev_ref[0, :, sl]
            prev_b = prev_g.astype(BF16)
            dp_g = dp_ref[:, sl]
            dp_b = dp_g.astype(BF16)
            dy_g = dyv[:, sl]
            xr_g = xr[:, sl]
            gmat = _dot(cg, prev_b, NN)
            dgm = (dy_g * dec_out[:, sl]).astype(BF16)
            dc_g = _dot(dgm, prev_b, NT)
            dprev = _dot(cg, dgm, TN)
            t1 = dy_g * gmat * dec_out[:, sl]
            mm_ = _dot(bg, dp_b, NN)
            db_g = _dot(xrs[:, sl].astype(BF16), dp_b, NT)
            dxr_g = mm_ * dec_st[:, sl]
            t2 = dxr_g * xr_g
            last = jnp.sum(t2, axis=0, keepdims=True) + jnp.sum(dp_g * prev_g, axis=0, keepdims=True) * dec_ch[:, sl]
            dp_ref[:, sl] = dp_g * dec_ch[:, sl] + dprev
            dcb = jnp.zeros((CHUNK, CHUNK), F32)
            for r in range(hpg):
                h = g * hpg + r
                diff = cs[:, h:h + 1] - cs_t[h:h + 1, :]
                lmat = jnp.exp(jnp.where(tril, diff, -1e30))
                wmat = cb * lmat
                dy_h = jnp.where(lane_g == r, dy_g, 0.0).astype(BF16)
                dw = _dot(dy_h, xr_g.astype(BF16), NT)
                dxr_g = dxr_g + _dot(wmat.astype(BF16), dy_h, TN)
                dcb = dcb + dw * lmat
                q = (dw * wmat).astype(BF16)
                onehot = (hsel == h).astype(BF16)
                dcs = dcs + _dot(q, onehot, NN) - _dot(q, onehot, TN)
            dcb_b = dcb.astype(BF16)
            dc_g = dc_g + _dot(dcb_b, bg, NN)
            db_g = db_g + _dot(dcb_b, cg, TN)
            dxbc_ref[:, bsl] = db_g.astype(dxbc_ref.dtype)
            dxbc_ref[:, csl] = dc_g.astype(dxbc_ref.dtype)
            t_parts.append(t1 - t2)
            last_parts.append(last)
            dxr_parts.append(dxr_g)
        dxr = jnp.concatenate(dxr_parts, axis=1)
        tt = jnp.concatenate(t_parts, axis=1)
        last_x = jnp.concatenate(last_parts, axis=1)
        dxbc_ref[:, 0:SSD_INNER] = (dxr * dt_x + dyv * d_x).astype(dxbc_ref.dtype)
        dcs = dcs + _dot01(tt, expand, NT, 0)
        last_h = _dot01(jnp.broadcast_to(last_x, (8, SSD_INNER)), expand, NT, 0)[0:1, :]
        rowi = lax.broadcasted_iota(jnp.int32, (CHUNK, LANES), 0)
        dcs = dcs + jnp.where(rowi == CHUNK - 1, last_h, 0.0)
        dda = _dot01(tril.astype(F32), dcs, TN, 1)
        ddt = dda * a + _dot01(dxr * x, expand, NT, 0)
        dpre = ddt * _sigmoid(pre)
        ddtp_ref[...] = dpre
        dbias_ref[...] += jnp.sum(dpre, axis=0, keepdims=True)
        dalog_ref[...] += jnp.sum(dda * dt, axis=0, keepdims=True) * a
        ddx_ref[...] += jnp.broadcast_to(jnp.sum(dyv * x, axis=0, keepdims=True), (8, SSD_INNER))

        @pl.when(step == nc - 1)
        def _():
            dd_ref[...] = _dot01(ddx_ref[...], expand, NT, 0)[0:1, :]

    rev = lambda c: (nc - 1 - c, 0)
    vec = pl.BlockSpec((1, LANES), lambda c: (0, 0))
    return pl.pallas_call(
        body,
        grid=(nc,),
        in_specs=[pl.BlockSpec((CHUNK, SSD_CONV_DIM), rev), pl.BlockSpec((CHUNK, LANES), rev),
                  pl.BlockSpec((1, SSD_STATE, SSD_INNER), lambda c: (nc - 1 - c, 0, 0)),
                  pl.BlockSpec((CHUNK, SSD_INNER), rev), vec, vec, vec],
        out_specs=[pl.BlockSpec((CHUNK, SSD_CONV_DIM), rev), pl.BlockSpec((CHUNK, LANES), rev), vec, vec, vec],
        out_shape=[jax.ShapeDtypeStruct((s, SSD_CONV_DIM), BF16), jax.ShapeDtypeStruct((s, LANES), F32),
                   jax.ShapeDtypeStruct((1, LANES), F32), jax.ShapeDtypeStruct((1, LANES), F32),
                   jax.ShapeDtypeStruct((1, LANES), F32)],
        scratch_shapes=[pltpu.VMEM((SSD_STATE, SSD_INNER), F32), pltpu.VMEM((8, SSD_INNER), F32)],
        compiler_params=_params(("arbitrary",)),
        name="ssd_scan_bwd",
    )(xbc, dtp, prev, dy, dt_bias, a_log, d_skip)


def _ssd_gate_fwd(y, zx, norm_w):
    s = y.shape[0]
    tr = _pick(s, (256, 128))
    gw = SSD_GROUP_W

    def body(y_ref, z_ref, w_ref, o_ref):
        for g in range(SSD_GROUPS):
            sl = slice(g * gw, (g + 1) * gw)
            z = z_ref[:, sl].astype(F32)
            gv = y_ref[:, sl] * (z * _sigmoid(z))
            r = lax.rsqrt(jnp.mean(gv * gv, axis=-1, keepdims=True) + LN_EPS)
            o_ref[:, sl] = (gv * r * w_ref[:, sl]).astype(o_ref.dtype)

    row = pl.BlockSpec((tr, SSD_INNER), lambda i: (i, 0))
    return pl.pallas_call(
        body,
        grid=(s // tr,),
        in_specs=[row, row, pl.BlockSpec((1, SSD_INNER), lambda i: (0, 0))],
        out_specs=row,
        out_shape=jax.ShapeDtypeStruct((s, SSD_INNER), BF16),
        compiler_params=_params(("parallel",)),
        name="ssd_gate_fwd",
    )(y, zx, norm_w)


def _ssd_gate_bwd(dgn, y, zx, norm_w):
    s = y.shape[0]
    tr = _pick(s, (256, 128))
    gw = SSD_GROUP_W

    def body(dg_ref, y_ref, z_ref, w_ref, dy_ref, dz_ref, dw_ref):
        parts = []
        for g in range(SSD_GROUPS):
            sl = slice(g * gw, (g + 1) * gw)
            z = z_ref[:, sl].astype(F32)
            yv = y_ref[:, sl]
            sg = _sigmoid(z)
            sz = z * sg
            gv = yv * sz
            r = lax.rsqrt(jnp.mean(gv * gv, axis=-1, keepdims=True) + LN_EPS)
            gn = gv * r
            dout = dg_ref[:, sl].astype(F32)
            parts.append(jnp.sum(dout * gn, axis=0, keepdims=True))
            dgn_ = dout * w_ref[:, sl]
            dgv = r * (dgn_ - gn * jnp.mean(dgn_ * gn, axis=-1, keepdims=True))
            dy_ref[:, sl] = dgv * sz
            dz_ref[:, sl] = (dgv * yv * (sg * (1.0 + z * (1.0 - sg)))).astype(dz_ref.dtype)
        part = jnp.concatenate(parts, axis=1)

        @pl.when(pl.program_id(0) == 0)
        def _():
            dw_ref[...] = part

        @pl.when(pl.program_id(0) > 0)
        def _():
            dw_ref[...] += part

    row = pl.BlockSpec((tr, SSD_INNER), lambda i: (i, 0))
    vec = pl.BlockSpec((1, SSD_INNER), lambda i: (0, 0))
    return pl.pallas_call(
        body,
        grid=(s // tr,),
        in_specs=[row, row, row, vec],
        out_specs=[row, row, vec],
        out_shape=[jax.ShapeDtypeStruct((s, SSD_INNER), F32), jax.ShapeDtypeStruct((s, SSD_ZX), BF16),
                   jax.ShapeDtypeStruct((1, SSD_INNER), F32)],
        compiler_params=_params(("arbitrary",)),
        name="ssd_gate_bwd",
    )(dgn, y, zx, norm_w)


INV_SQRT2 = 1.0 / math.sqrt(2.0)
INV_SQRT2PI = 1.0 / math.sqrt(2.0 * math.pi)


def _gelu(x):
    return 0.5 * x * (1.0 + lax.erf(x * INV_SQRT2))


def _gelu_grad(x):
    return 0.5 * (1.0 + lax.erf(x * INV_SQRT2)) + x * INV_SQRT2PI * jnp.exp(-0.5 * x * x)


def _gmlp_act_fwd(pre, b_in, ln_w, ln_b):
    s = pre.shape[0]
    tr = _pick(s, (256, 128))
    n = GMLP_INNER

    def body(p_ref, b_ref, w_ref, lb_ref, u_ref, v_ref):
        u_ref[...] = _gelu(p_ref[:, 0:n].astype(F32) + b_ref[:, 0:n]).astype(u_ref.dtype)
        hv = _gelu(p_ref[:, n:2 * n].astype(F32) + b_ref[:, n:2 * n])
        mu = jnp.mean(hv, axis=-1, keepdims=True)
        xc = hv - mu
        r = lax.rsqrt(jnp.mean(xc * xc, axis=-1, keepdims=True) + LN_EPS)
        v_ref[...] = (xc * r * w_ref[...] + lb_ref[...]).astype(v_ref.dtype)

    half = pl.BlockSpec((tr, n), lambda i: (i, 0))
    vec = pl.BlockSpec((1, n), lambda i: (0, 0))
    return pl.pallas_call(
        body,
        grid=(s // tr,),
        in_specs=[pl.BlockSpec((tr, 2 * n), lambda i: (i, 0)), pl.BlockSpec((1, 2 * n), lambda i: (0, 0)), vec, vec],
        out_specs=[half, half],
        out_shape=[jax.ShapeDtypeStruct((s, n), BF16), jax.ShapeDtypeStruct((s, n), BF16)],
        compiler_params=_params(("parallel",)),
        name="gmlp_act_fwd",
    )(pre, b_in, ln_w, ln_b)


def _gmlp_act_bwd(pre, b_in, ln_w, du, dv):
    s = pre.shape[0]
    tr = _pick(s, (256, 128))
    n = GMLP_INNER

    def body(p_ref, b_ref, w_ref, du_ref, dv_ref, dp_ref, db_ref, dw_ref, dlb_ref):
        xu = p_ref[:, 0:n].astype(F32) + b_ref[:, 0:n]
        dpu = du_ref[...].astype(F32) * _gelu_grad(xu)
        xv = p_ref[:, n:2 * n].astype(F32) + b_ref[:, n:2 * n]
        hv = _gelu(xv)
        mu = jnp.mean(hv, axis=-1, keepdims=True)
        xc = hv - mu
        r = lax.rsqrt(jnp.mean(xc * xc, axis=-1, keepdims=True) + LN_EPS)
        vh = xc * r
        dvv = dv_ref[...].astype(F32)
        dvh = dvv * w_ref[...]
        dh = r * (dvh - jnp.mean(dvh, axis=-1, keepdims=True) - vh * jnp.mean(dvh * vh, axis=-1, keepdims=True))
        dpv = dh * _gelu_grad(xv)
        dp_ref[:, 0:n] = dpu.astype(dp_ref.dtype)
        dp_ref[:, n:2 * n] = dpv.astype(dp_ref.dtype)
        pb = jnp.concatenate([jnp.sum(dpu, axis=0, keepdims=True), jnp.sum(dpv, axis=0, keepdims=True)], axis=1)
        pw = jnp.sum(dvv * vh, axis=0, keepdims=True)
        plb = jnp.sum(dvv, axis=0, keepdims=True)

        @pl.when(pl.program_id(0) == 0)
        def _():
            db_ref[...] = pb
            dw_ref[...] = pw
            dlb_ref[...] = plb

        @pl.when(pl.program_id(0) > 0)
        def _():
            db_ref[...] += pb
            dw_ref[...] += pw
            dlb_ref[...] += plb

    half = pl.BlockSpec((tr, n), lambda i: (i, 0))
    full = pl.BlockSpec((tr, 2 * n), lambda i: (i, 0))
    vec = pl.BlockSpec((1, n), lambda i: (0, 0))
    vec2 = pl.BlockSpec((1, 2 * n), lambda i: (0, 0))
    return pl.pallas_call(
        body,
        grid=(s // tr,),
        in_specs=[full, vec2, vec, half, half],
        out_specs=[full, vec2, vec, vec],
        out_shape=[jax.ShapeDtypeStruct((s, 2 * n), BF16), jax.ShapeDtypeStruct((1, 2 * n), F32),
                   jax.ShapeDtypeStruct((1, n), F32), jax.ShapeDtypeStruct((1, n), F32)],
        compiler_params=_params(("arbitrary",)),
        name="gmlp_act_bwd",
    )(pre, b_in, ln_w, du, dv)


def _gmlp_mix_fwd(u, v, w_s, b_st):
    s = u.shape[0]
    gd = GMLP_INNER // GMLP_GROUPS

    def body(u_ref, v_ref, w_ref, b_ref, o_ref):
        li = lax.broadcasted_iota(jnp.int32, (CHUNK, CHUNK), 0)
        si = lax.broadcasted_iota(jnp.int32, (CHUNK, CHUNK), 1)
        tril = li >= si
        for g in range(GMLP_GROUPS):
            sl = slice(g * gd, (g + 1) * gd)
            wm = jnp.where(tril, w_ref[g], 0.0).astype(BF16)
            mixed = _dot(wm, v_ref[:, sl], NN) + b_ref[:, g:g + 1]
            o_ref[:, sl] = (u_ref[:, sl].astype(F32) * mixed).astype(o_ref.dtype)

    row = pl.BlockSpec((CHUNK, GMLP_INNER), lambda c: (c, 0))
    return pl.pallas_call(
        body,
        grid=(s // CHUNK,),
        in_specs=[row, row, pl.BlockSpec((GMLP_GROUPS, CHUNK, CHUNK), lambda c: (0, 0, 0)),
                  pl.BlockSpec((CHUNK, LANES), lambda c: (0, 0))],
        out_specs=row,
        out_shape=jax.ShapeDtypeStruct((s, GMLP_INNER), BF16),
        compiler_params=_params(("parallel",)),
        name="gmlp_mix_fwd",
    )(u, v, w_s, b_st)


def _gmlp_mix_bwd(dgated, u, v, w_s, b_st):
    s = u.shape[0]
    nc = s // CHUNK
    gd = GMLP_INNER // GMLP_GROUPS

    def body(dg_ref, u_ref, v_ref, w_ref, b_ref, du_ref, dv_ref, dw_ref, db_ref):
        c = pl.program_id(0)

        @pl.when(c == 0)
        def _():
            dw_ref[...] = jnp.zeros_like(dw_ref)
            db_ref[...] = jnp.zeros_like(db_ref)

        li = lax.broadcasted_iota(jnp.int32, (CHUNK, CHUNK), 0)
        si = lax.broadcasted_iota(jnp.int32, (CHUNK, CHUNK), 1)
        tril = li >= si
        lane = lax.broadcasted_iota(jnp.int32, (CHUNK, LANES), 1)
        dbacc = jnp.zeros((CHUNK, LANES), F32)
        for g in range(GMLP_GROUPS):
            sl = slice(g * gd, (g + 1) * gd)
            wm = jnp.where(tril, w_ref[g], 0.0).astype(BF16)
            vg = v_ref[:, sl]
            mixed = _dot(wm, vg, NN) + b_ref[:, g:g + 1]
            dgv = dg_ref[:, sl].astype(F32)
            du_ref[:, sl] = (dgv * mixed).astype(du_ref.dtype)
            dm = dgv * u_ref[:, sl].astype(F32)
            dm_b = dm.astype(BF16)
            dv_ref[:, sl] = _dot(wm, dm_b, TN).astype(dv_ref.dtype)
            dw_ref[g] += jnp.where(tril, _dot(dm_b, vg, NT), 0.0)
            dbacc = dbacc + jnp.where(lane == g, jnp.sum(dm, axis=1, keepdims=True), 0.0)
        db_ref[...] += dbacc

    row = pl.BlockSpec((CHUNK, GMLP_INNER), lambda c: (c, 0))
    wspec = pl.BlockSpec((GMLP_GROUPS, CHUNK, CHUNK), lambda c: (0, 0, 0))
    bspec = pl.BlockSpec((CHUNK, LANES), lambda c: (0, 0))
    return pl.pallas_call(
        body,
        grid=(nc,),
        in_specs=[row, row, row, wspec, bspec],
        out_specs=[row, row, wspec, bspec],
        out_shape=[jax.ShapeDtypeStruct((s, GMLP_INNER), BF16), jax.ShapeDtypeStruct((s, GMLP_INNER), BF16),
                   jax.ShapeDtypeStruct((GMLP_GROUPS, CHUNK, CHUNK), F32), jax.ShapeDtypeStruct((CHUNK, LANES), F32)],
        compiler_params=_params(("arbitrary",)),
        name="gmlp_mix_bwd",
    )(dgated, u, v, w_s, b_st)


def _swiglu_fwd(gu):
    s = gu.shape[0]
    f = FFN_DIM
    tr = _pick(s, (512, 256, 128))

    def body(gu_ref, o_ref):
        gt = gu_ref[:, 0:f].astype(F32)
        o_ref[...] = (gt * _sigmoid(gt) * gu_ref[:, f:2 * f].astype(F32)).astype(o_ref.dtype)

    return pl.pallas_call(
        body,
        grid=(s // tr,),
        in_specs=[pl.BlockSpec((tr, 2 * f), lambda i: (i, 0))],
        out_specs=pl.BlockSpec((tr, f), lambda i: (i, 0)),
        out_shape=jax.ShapeDtypeStruct((s, f), BF16),
        compiler_params=_params(("parallel",)),
        name="swiglu_fwd",
    )(gu)


def _swiglu_bwd(gu, dhid):
    s = gu.shape[0]
    f = FFN_DIM
    tr = _pick(s, (512, 256, 128))

    def body(gu_ref, dh_ref, dgu_ref):
        gt = gu_ref[:, 0:f].astype(F32)
        up = gu_ref[:, f:2 * f].astype(F32)
        dh = dh_ref[...].astype(F32)
        sg = _sigmoid(gt)
        dgu_ref[:, 0:f] = (dh * up * (sg * (1.0 + gt * (1.0 - sg)))).astype(dgu_ref.dtype)
        dgu_ref[:, f:2 * f] = (dh * gt * sg).astype(dgu_ref.dtype)

    wide = pl.BlockSpec((tr, 2 * f), lambda i: (i, 0))
    return pl.pallas_call(
        body,
        grid=(s // tr,),
        in_specs=[wide, pl.BlockSpec((tr, f), lambda i: (i, 0))],
        out_specs=wide,
        out_shape=jax.ShapeDtypeStruct((s, 2 * f), BF16),
        compiler_params=_params(("parallel",)),
        name="swiglu_bwd",
    )(gu, dhid)


def _ple_fwd(pe, gl, h, ple_norm):
    s, d = h.shape
    tr = _pick(s, (512, 256, 128))

    def body(pe_ref, gl_ref, h_ref, w_ref, o_ref):
        pe_ = pe_ref[...]
        r = lax.rsqrt(jnp.mean(pe_ * pe_, axis=-1, keepdims=True) + RMS_EPS)
        o_ref[...] = h_ref[...] + _sigmoid(gl_ref[...]) * (pe_ * r * w_ref[...])

    row = pl.BlockSpec((tr, d), lambda i: (i, 0))
    return pl.pallas_call(
        body,
        grid=(s // tr,),
        in_specs=[row, row, row, pl.BlockSpec((1, d), lambda i: (0, 0))],
        out_specs=row,
        out_shape=jax.ShapeDtypeStruct((s, d), F32),
        compiler_params=_params(("parallel",)),
        name="ple_fwd",
    )(pe, gl, h, ple_norm)


def _ple_bwd(dh, pe, gl, ple_norm):
    s, d = dh.shape
    tr = _pick(s, (512, 256, 128))

    def body(dh_ref, pe_ref, gl_ref, w_ref, dgl_ref, dpe_ref, dw_ref):
        pe_ = pe_ref[...]
        dhv = dh_ref[...]
        r = lax.rsqrt(jnp.mean(pe_ * pe_, axis=-1, keepdims=True) + RMS_EPS)
        pn = pe_ * r
        gate = _sigmoid(gl_ref[...])
        dgl_ref[...] = (dhv * (pn * w_ref[...]) * gate * (1.0 - gate)).astype(dgl_ref.dtype)
        de = dhv * gate
        dxh = de * w_ref[...]
        dpe_ref[...] = (r * (dxh - pn * jnp.mean(dxh * pn, axis=-1, keepdims=True))).astype(dpe_ref.dtype)
        part = jnp.sum(de * pn, axis=0, keepdims=True)

        @pl.when(pl.program_id(0) == 0)
        def _():
            dw_ref[...] = part

        @pl.when(pl.program_id(0) > 0)
        def _():
            dw_ref[...] += part

    row = pl.BlockSpec((tr, d), lambda i: (i, 0))
    vec = pl.BlockSpec((1, d), lambda i: (0, 0))
    return pl.pallas_call(
        body,
        grid=(s // tr,),
        in_specs=[row, row, row, vec],
        out_specs=[row, row, vec],
        out_shape=[jax.ShapeDtypeStruct((s, d), BF16), jax.ShapeDtypeStruct((s, d), BF16),
                   jax.ShapeDtypeStruct((1, d), F32)],
        compiler_params=_params(("arbitrary",)),
        name="ple_bwd",
    )(dh, pe, gl, ple_norm)


def _loss_head(h, w, target):
    s, d = h.shape
    tr = _pick(s, (512, 256, 128))

    def body(h_ref, w_ref, t_ref, l_ref, dh_ref, dw_ref):
        hv = h_ref[...]
        r = lax.rsqrt(jnp.mean(hv * hv, axis=-1, keepdims=True) + RMS_EPS)
        hn = hv * r
        diff = hn * w_ref[...] - t_ref[...]
        lpart = jnp.zeros((8, LANES), F32) + (0.5 / d) * jnp.sum(jnp.sum(diff * diff, axis=1, keepdims=True), axis=0, keepdims=True)
        dy = diff * (1.0 / d)
        dxh = dy * w_ref[...]
        dh_ref[...] = r * (dxh - hn * jnp.mean(dxh * hn, axis=-1, keepdims=True))
        part = jnp.sum(dy * hn, axis=0, keepdims=True)

        @pl.when(pl.program_id(0) == 0)
        def _():
            l_ref[...] = lpart
            dw_ref[...] = part

        @pl.when(pl.program_id(0) > 0)
        def _():
            l_ref[...] += lpart
            dw_ref[...] += part

    row = pl.BlockSpec((tr, d), lambda i: (i, 0))
    vec = pl.BlockSpec((1, d), lambda i: (0, 0))
    return pl.pallas_call(
        body,
        grid=(s // tr,),
        in_specs=[row, vec, row],
        out_specs=[pl.BlockSpec((8, LANES), lambda i: (0, 0)), row, vec],
        out_shape=[jax.ShapeDtypeStruct((8, LANES), F32), jax.ShapeDtypeStruct((s, d), F32),
                   jax.ShapeDtypeStruct((1, d), F32)],
        compiler_params=_params(("arbitrary",)),
        name="loss_head",
    )(h, w, target)


PER_LAYER = ("norm_mix", "norm_ffn", "ffn_w_gu", "ffn_w_down", "ple_w_proj", "ple_norm", "ple_gate_norm", "ple_w_gate")


def _pad_lanes(v):
    return jnp.pad(v.astype(F32), (0, LANES - v.shape[0]))[None, :]


def _kernel_layouts(full):
    w = {}
    for k in ("norm_mix", "norm_ffn", "ple_norm", "ple_gate_norm", "ssd_conv_b", "ssd_norm_w", "gmlp_b_in", "gmlp_ln_w",
              "gmlp_ln_b", "gmlp_w_s"):
        w[k] = [full[k][i].astype(F32) for i in range(full[k].shape[0])]
    w["final_norm"] = full["final_norm"].astype(F32)
    n_ssd = full["ssd_w_out"].shape[0]
    if "ssd_w_in" in full:
        w["ssd_w_zx"] = [full["ssd_w_in"][j][:, :SSD_ZX].astype(BF16) for j in range(n_ssd)]
        w["ssd_w_dt"] = [jnp.pad(full["ssd_w_in"][j][:, SSD_ZX:].astype(BF16), ((0, 0), (0, LANES - SSD_HEADS)))
                         for j in range(n_ssd)]
        w["ffn_w_gu"] = [jnp.concatenate([full["ffn_w_gate"][i], full["ffn_w_up"][i]], axis=1).astype(BF16)
                         for i in range(DEPTH)]
    else:
        for k in ("ssd_w_zx", "ssd_w_dt", "ffn_w_gu"):
            w[k] = full[k]
    w["ssd_conv_w"] = [jnp.pad(full["ssd_conv_w"][j].astype(F32), ((0, 8 - CONV_K), (0, 0))) for j in range(n_ssd)]
    for k in ("ssd_dt_bias", "ssd_a_log", "ssd_d"):
        w[k] = [_pad_lanes(full[k][j]) for j in range(n_ssd)]
    w["ssd_w_out"] = [full["ssd_w_out"][j].astype(BF16) for j in range(n_ssd)]
    n_g = full["gmlp_w_in"].shape[0]
    w["gmlp_w_in"] = [full["gmlp_w_in"][j].astype(BF16) for j in range(n_g)]
    w["gmlp_w_out"] = [full["gmlp_w_out"][j].astype(BF16) for j in range(n_g)]
    w["gmlp_b_st"] = [jnp.pad(full["gmlp_b_s"][j].astype(F32).T, ((0, 0), (0, LANES - GMLP_GROUPS))) for j in range(n_g)]
    w["ffn_w_down"] =[full["ffn_w_down"][i].astype(BF16) for i in range(DEPTH)]
    w["ple_w_proj"] = [full["ple_w_proj"][i].astype(BF16) for i in range(DEPTH)]
    w["ple_w_gate"] = [full["ple_w_gate"][i].astype(BF16) for i in range(DEPTH)]
    return w


MATRICES = ("ssd_w_out", "gmlp_w_in", "gmlp_w_out", "ffn_w_down", "ple_w_proj", "ple_w_gate")


def _reference_layouts(g, wide=True):
    out = {}
    for k in ("norm_mix", "norm_ffn", "ple_norm", "ple_gate_norm", "ssd_conv_b", "ssd_norm_w", "gmlp_b_in", "gmlp_ln_w",
              "gmlp_ln_b", "gmlp_w_s", "ssd_conv_w", "ssd_dt_bias", "ssd_a_log", "ssd_d") + (MATRICES if wide else ()):
        out[k] = jnp.stack(g[k])
    out["final_norm"] = g["final_norm"]
    out["gmlp_b_s"] = jnp.stack([b[:, :GMLP_GROUPS].T for b in g["gmlp_b_st"]])
    if wide:
        out["ssd_w_in"] = jnp.stack([jnp.concatenate([zx, dt[:, :SSD_HEADS]], axis=1)
                                     for zx, dt in zip(g["ssd_w_zx"], g["ssd_w_dt"])])
        out["ffn_w_gate"] = jnp.stack([gu[:, :FFN_DIM] for gu in g["ffn_w_gu"]])
        out["ffn_w_up"] = jnp.stack([gu[:, FFN_DIM:] for gu in g["ffn_w_gu"]])
    return out


RELAYOUT_ROWS = 128
SSD_SHARD = SSD_IN_DIM // N_DEV
FFN_SHARD = FFN_DIM // N_DEV


def _to_bf16(x):
    nl, rows, n = x.shape

    def body(x_ref, o_ref):
        o_ref[...] = x_ref[...].astype(o_ref.dtype)

    blk = pl.BlockSpec((1, rows, n), lambda i: (i, 0, 0))
    return pl.pallas_call(
        body, grid=(nl,), in_specs=[blk], out_specs=blk, out_shape=jax.ShapeDtypeStruct(x.shape, BF16),
        compiler_params=_params(("parallel",)), name="to_bf16",
    )(x)


def _cat_ssd_in(gathered):
    _, nl, rows, n = gathered.shape
    tr = RELAYOUT_ROWS

    def body(g_ref, *o_refs):
        for j in range(nl):
            full = jnp.concatenate([g_ref[d, j] for d in range(N_DEV)], axis=1)
            o_refs[2 * j][...] = full[:, :SSD_ZX]
            o_refs[2 * j + 1][...] = jnp.concatenate(
                [full[:, SSD_ZX:], jnp.zeros((tr, LANES - SSD_HEADS), full.dtype)], axis=1)

    outs = pl.pallas_call(
        body, grid=(rows // tr,),
        in_specs=[pl.BlockSpec((N_DEV, nl, tr, n), lambda i: (0, 0, i, 0))],
        out_specs=[pl.BlockSpec((tr, SSD_ZX), lambda i: (i, 0)), pl.BlockSpec((tr, LANES), lambda i: (i, 0))] * nl,
        out_shape=[jax.ShapeDtypeStruct((rows, SSD_ZX), BF16), jax.ShapeDtypeStruct((rows, LANES), BF16)] * nl,
        compiler_params=_params(("parallel",)), name="cat_ssd_in",
    )(gathered)
    return [outs[2 * j] for j in range(nl)], [outs[2 * j + 1] for j in range(nl)]


def _split_ssd_in(dzx_list, ddt_list):
    nl = len(dzx_list)
    rows = dzx_list[0].shape[0]
    tr = RELAYOUT_ROWS

    def body(*refs):
        o_ref = refs[2 * nl]
        for j in range(nl):
            full = jnp.concatenate([refs[2 * j][...], refs[2 * j + 1][:, 0:SSD_HEADS]], axis=1)
            for d in range(N_DEV):
                o_ref[d, j] = full[:, d * SSD_SHARD:(d + 1) * SSD_SHARD].astype(o_ref.dtype)

    ins = []
    for j in range(nl):
        ins += [dzx_list[j], ddt_list[j]]
    return pl.pallas_call(
        body, grid=(rows // tr,),
        in_specs=[pl.BlockSpec((tr, SSD_ZX), lambda i: (i, 0)), pl.BlockSpec((tr, LANES), lambda i: (i, 0))] * nl,
        out_specs=pl.BlockSpec((N_DEV, nl, tr, SSD_SHARD), lambda i: (0, 0, i, 0)),
        out_shape=jax.ShapeDtypeStruct((N_DEV, nl, rows, SSD_SHARD), BF16),
        compiler_params=_params(("parallel",)), name="split_ssd_in",
    )(*ins)


def _cat_ffn(g_gate, g_up):
    _, nl, rows, n = g_gate.shape
    tr = RELAYOUT_ROWS

    def body(gg_ref, gu_ref, *o_refs):
        for i in range(nl):
            o_refs[i][...] = jnp.concatenate([gg_ref[d, i] for d in range(N_DEV)] + [gu_ref[d, i] for d in range(N_DEV)],
                                             axis=1)

    blk = pl.BlockSpec((N_DEV, nl, tr, n), lambda i: (0, 0, i, 0))
    outs = pl.pallas_call(
        body, grid=(rows // tr,), in_specs=[blk, blk],
        out_specs=[pl.BlockSpec((tr, 2 * FFN_DIM), lambda i: (i, 0))] * nl,
        out_shape=[jax.ShapeDtypeStruct((rows, 2 * FFN_DIM), BF16)] * nl,
        compiler_params=_params(("parallel",)), name="cat_ffn",
    )(g_gate, g_up)
    return list(outs)


def _split_ffn(dgu_list):
    nl = len(dgu_list)
    rows = dgu_list[0].shape[0]
    tr = RELAYOUT_ROWS

    def body(*refs):
        og_ref, ou_ref = refs[nl], refs[nl + 1]
        for i in range(nl):
            full = refs[i][...]
            for d in range(N_DEV):
                og_ref[d, i] = full[:, d * FFN_SHARD:(d + 1) * FFN_SHARD].astype(og_ref.dtype)
                ou_ref[d, i] = full[:, FFN_DIM + d * FFN_SHARD:FFN_DIM + (d + 1) * FFN_SHARD].astype(ou_ref.dtype)

    blk = pl.BlockSpec((N_DEV, nl, tr, FFN_SHARD), lambda i: (0, 0, i, 0))
    sds = jax.ShapeDtypeStruct((N_DEV, nl, rows, FFN_SHARD), BF16)
    return pl.pallas_call(
        body, grid=(rows // tr,),
        in_specs=[pl.BlockSpec((tr, 2 * FFN_DIM), lambda i: (i, 0))] * nl,
        out_specs=[blk, blk], out_shape=[sds, sds],
        compiler_params=_params(("parallel",)), name="split_ffn",
    )(*dgu_list)


def _local_step(x, p, target, w):
    saved = []
    h = x
    for i in range(DEPTH):
        j = i // 2
        sv = {"h0": h}
        hn = _rms_fwd(h, w["norm_mix"][i][None, :])
        sv["hn"] = hn
        if i % 2 == 0:
            zx = _mm(hn, w["ssd_w_zx"][j], "nn", BF16)
            dtp = _mm(hn, w["ssd_w_dt"][j], "nn", F32)
            xbc = _ssd_conv_fwd(zx, w["ssd_conv_w"][j], w["ssd_conv_b"][j][None, :])
            y, prev = _ssd_scan_fwd(xbc, dtp, w["ssd_dt_bias"][j], w["ssd_a_log"][j], w["ssd_d"][j])
            gn = _ssd_gate_fwd(y, zx, w["ssd_norm_w"][j][None, :])
            h = _mm(gn, w["ssd_w_out"][j], "nn", F32, add=h)
            sv.update(zx=zx, dtp=dtp, xbc=xbc, y=y, prev=prev, gn=gn)
        else:
            pre = _mm(hn, w["gmlp_w_in"][j], "nn", BF16)
            u, v = _gmlp_act_fwd(pre, w["gmlp_b_in"][j][None, :], w["gmlp_ln_w"][j][None, :], w["gmlp_ln_b"][j][None, :])
            gated = _gmlp_mix_fwd(u, v, w["gmlp_w_s"][j], w["gmlp_b_st"][j])
            h = _mm(gated, w["gmlp_w_out"][j], "nn", F32, add=h)
            sv.update(pre=pre, u=u, v=v, gated=gated)
        sv["h1"] = h
        un = _rms_fwd(h, w["norm_ffn"][i][None, :])
        gu = _mm(un, w["ffn_w_gu"][i], "nn", BF16)
        hid = _swiglu_fwd(gu)
        h = _mm(hid, w["ffn_w_down"][i], "nn", F32, add=h)
        sv.update(un=un, gu=gu, hid=hid, h2=h)
        pe = _mm(p[i], w["ple_w_proj"][i], "nn", F32)
        hg = _rms_fwd(h, w["ple_gate_norm"][i][None, :])
        gl = _mm(hg, w["ple_w_gate"][i], "nn", F32)
        h = _ple_fwd(pe, gl, h, w["ple_norm"][i][None, :])
        sv.update(pe=pe, hg=hg, gl=gl)
        saved.append(sv)

    lpart, dh, d_final = _loss_head(h, w["final_norm"][None, :], target)
    g = {k: [None] * (DEPTH if k in PER_LAYER else DEPTH // 2) for k in w if k != "final_norm"}
    g["final_norm"] = d_final[0]

    for i in reversed(range(DEPTH)):
        j = i // 2
        sv = saved[i]
        dgl, dpe, d_ple_norm = _ple_bwd(dh, sv["pe"], sv["gl"], w["ple_norm"][i][None, :])
        g["ple_norm"][i] = d_ple_norm[0]
        g["ple_w_gate"][i] = _mm(sv["hg"], dgl, "tn", F32)
        g["ple_w_proj"][i] = _mm(p[i], dpe, "tn", F32)
        dhg = _mm(dgl, w["ple_w_gate"][i], "nt", F32)
        dh, d_gate_norm = _rms_bwd(dhg, sv["h2"], w["ple_gate_norm"][i][None, :], dh)
        g["ple_gate_norm"][i] = d_gate_norm[0]
        dhid = _mm(dh, w["ffn_w_down"][i], "nt", BF16)
        g["ffn_w_down"][i] = _mm(sv["hid"], dh, "tn", F32)
        dgu = _swiglu_bwd(sv["gu"], dhid)
        g["ffn_w_gu"][i] = _mm(sv["un"], dgu, "tn", F32)
        dun = _mm(dgu, w["ffn_w_gu"][i], "nt", F32)
        dh, d_norm_ffn = _rms_bwd(dun, sv["h1"], w["norm_ffn"][i][None, :], dh)
        g["norm_ffn"][i] = d_norm_ffn[0]
        if i % 2 == 0:
            dgn = _mm(dh, w["ssd_w_out"][j], "nt", F32)
            g["ssd_w_out"][j] = _mm(sv["gn"], dh, "tn", F32)
            dy, dzx, d_norm_w = _ssd_gate_bwd(dgn, sv["y"], sv["zx"], w["ssd_norm_w"][j][None, :])
            g["ssd_norm_w"][j] = d_norm_w[0]
            dxbc, ddtp, d_bias, d_alog, d_d = _ssd_scan_bwd(sv["xbc"], sv["dtp"], sv["prev"], dy, w["ssd_dt_bias"][j],
                                                            w["ssd_a_log"][j], w["ssd_d"][j])
            g["ssd_dt_bias"][j] = d_bias[0, :SSD_HEADS]
            g["ssd_a_log"][j] = d_alog[0, :SSD_HEADS]
            g["ssd_d"][j] = d_d[0, :SSD_HEADS]
            dzx, d_conv_w, d_conv_b = _ssd_conv_bwd(sv["zx"], dxbc, w["ssd_conv_w"][j], w["ssd_conv_b"][j][None, :], dzx)
            g["ssd_conv_w"][j] = d_conv_w[:CONV_K]
            g["ssd_conv_b"][j] = d_conv_b[0]
            g["ssd_w_zx"][j] = _mm(sv["hn"], dzx, "tn", F32)
            g["ssd_w_dt"][j] = _mm(sv["hn"], ddtp, "tn", F32)
            dhn = _mm(dzx, w["ssd_w_zx"][j], "nt", F32)
            dhn = _mm(ddtp, w["ssd_w_dt"][j], "nt", F32, add=dhn)
        else:
            dgated = _mm(dh, w["gmlp_w_out"][j], "nt", BF16)
            g["gmlp_w_out"][j] = _mm(sv["gated"], dh, "tn", F32)
            du, dv, d_ws, d_bst = _gmlp_mix_bwd(dgated, sv["u"], sv["v"], w["gmlp_w_s"][j], w["gmlp_b_st"][j])
            g["gmlp_w_s"][j] = d_ws
            g["gmlp_b_st"][j] = d_bst
            dpre, d_bin, d_lnw, d_lnb = _gmlp_act_bwd(sv["pre"], w["gmlp_b_in"][j][None, :], w["gmlp_ln_w"][j][None, :],
                                                     du, dv)
            g["gmlp_b_in"][j] = d_bin[0]
            g["gmlp_ln_w"][j] = d_lnw[0]
            g["gmlp_ln_b"][j] = d_lnb[0]
            g["gmlp_w_in"][j] = _mm(sv["hn"], dpre, "tn", F32)
            dhn = _mm(dpre, w["gmlp_w_in"][j], "nt", F32)
        dh, d_norm_mix = _rms_bwd(dhn, sv["h0"], w["norm_mix"][i][None, :], dh)
        g["norm_mix"][i] = d_norm_mix[0]
    return lpart[0, 0], dh, g


PACK_COLS = 1024
ANY = pl.BlockSpec(memory_space=pl.ANY)


def _mesh_pos():
    return lax.axis_index("x"), lax.axis_index("y"), lax.axis_index("c")


def _all_gather(xs_list, name):
    n = len(xs_list)

    def body(*refs):
        x_refs, out_refs = refs[:n], refs[n:2 * n]
        send_sems, recv_sems, local_sems = refs[2 * n:]
        x, y, c = _mesh_pos()
        me, sibling = (x, y, c), (x, y, 1 - c)
        chips = [(1 - x, y), (x, 1 - y), (1 - x, 1 - y)]

        def copy(a, k, block, to, from_input=False):
            px, py, pc = block
            dst = out_refs[a].at[4 * px + 2 * py + pc]
            return pltpu.make_async_remote_copy(
                src_ref=x_refs[a] if from_input else dst, dst_ref=dst,
                send_sem=send_sems.at[7 * a + k], recv_sem=recv_sems.at[7 * a + k], device_id=to,
                device_id_type=MESH_ID)

        mine = [pltpu.make_async_copy(x_refs[a], out_refs[a].at[4 * x + 2 * y + c], local_sems.at[a]) for a in range(n)]
        for cp in mine:
            cp.start()
        first = []
        for a in range(n):
            first += [copy(a, 1 + j, me, (*chip, c), from_input=True) for j, chip in enumerate(chips)]
            first.append(copy(a, 0, me, sibling, from_input=True))
        for cp in first:
            cp.start()
        passed = []
        for a in range(n):
            for j, chip in enumerate(chips):
                copy(a, 1 + j, (*chip, c), me).wait_recv()
                fwd = copy(a, 4 + j, (*chip, c), sibling)
                fwd.start()
                passed.append(fwd)
        for a in range(n):
            copy(a, 0, sibling, me).wait_recv()
            for j, chip in enumerate(chips):
                copy(a, 4 + j, (*chip, 1 - c), me).wait_recv()
        for cp in first + passed:
            cp.wait_send()
        for cp in mine:
            cp.wait()

    outs = pl.pallas_call(
        body,
        out_shape=[jax.ShapeDtypeStruct((N_DEV,) + t.shape, t.dtype) for t in xs_list],
        in_specs=[ANY] * n,
        out_specs=[ANY] * n,
        scratch_shapes=[pltpu.SemaphoreType.DMA((7 * n,)), pltpu.SemaphoreType.DMA((7 * n,)),
                        pltpu.SemaphoreType.DMA((n,))],
        name=name,
    )(*xs_list)
    return list(outs)


def _exchange_sibling(send_list):
    n = len(send_list)

    def body(*refs):
        s_refs, land_refs = refs[:n], refs[n:2 * n]
        send_sems, recv_sems = refs[2 * n:]
        x, y, c = _mesh_pos()
        cps = [pltpu.make_async_remote_copy(src_ref=s_refs[a], dst_ref=land_refs[a], send_sem=send_sems.at[a],
                                            recv_sem=recv_sems.at[a], device_id=(x, y, 1 - c), device_id_type=MESH_ID)
               for a in range(n)]
        for cp in cps:
            cp.start()
        for cp in cps:
            cp.wait()

    outs = pl.pallas_call(
        body,
        out_shape=[jax.ShapeDtypeStruct(t.shape, t.dtype) for t in send_list],
        in_specs=[ANY] * n,
        out_specs=[ANY] * n,
        scratch_shapes=[pltpu.SemaphoreType.DMA((n,)), pltpu.SemaphoreType.DMA((n,))],
        name="rs_exchange_sibling",
    )(*send_list)
    return list(outs)


def _exchange_chips(partial_list):
    n = len(partial_list)

    def body(*refs):
        p_refs, land_refs = refs[:n], refs[n:2 * n]
        send_sems, recv_sems = refs[2 * n:]
        x, y, c = _mesh_pos()
        chips = [(1 - x, y), (x, 1 - y), (1 - x, 1 - y)]
        cps = [pltpu.make_async_remote_copy(src_ref=p_refs[a].at[2 * cx + cy], dst_ref=land_refs[a].at[j],
                                            send_sem=send_sems.at[3 * a + j], recv_sem=recv_sems.at[3 * a + j],
                                            device_id=(cx, cy, c), device_id_type=MESH_ID)
               for a in range(n) for j, (cx, cy) in enumerate(chips)]
        for cp in cps:
            cp.start()
        for cp in cps:
            cp.wait()

    outs = pl.pallas_call(
        body,
        out_shape=[jax.ShapeDtypeStruct((3,) + t.shape[1:], t.dtype) for t in partial_list],
        in_specs=[ANY] * n,
        out_specs=[ANY] * n,
        scratch_shapes=[pltpu.SemaphoreType.DMA((3 * n,)), pltpu.SemaphoreType.DMA((3 * n,))],
        name="rs_exchange_chips",
    )(*partial_list)
    return list(outs)


def _sum_pairs(a, b):
    shape = a.shape
    a = a.reshape(shape[0], -1, shape[-1])
    b = b.reshape(a.shape)
    n, r, cdim = a.shape
    tr = _pick(r, (1024, 512, 256, 128, 64))

    def body(a_ref, b_ref, o_ref):
        o_ref[...] = (a_ref[...].astype(F32) + b_ref[...].astype(F32)).astype(o_ref.dtype)

    blk = pl.BlockSpec((1, tr, cdim), lambda i, j: (i, j, 0))
    return pl.pallas_call(
        body, grid=(n, r // tr), in_specs=[blk, blk], out_specs=blk,
        out_shape=jax.ShapeDtypeStruct(a.shape, a.dtype),
        compiler_params=_params(("parallel", "parallel")), name="rs_sum_pairs",
    )(a, b).reshape(shape)


def _sum_final(own, land):
    shape = own.shape
    own = own.reshape(-1, shape[-1])
    land = land.reshape((3,) + own.shape)
    r, cdim = own.shape
    tr = _pick(r, (1024, 512, 256, 128, 64))

    def body(o_ref, l_ref, out_ref):
        acc = o_ref[...].astype(F32)
        for j in range(3):
            acc = acc + l_ref[j].astype(F32)
        out_ref[...] = acc

    return pl.pallas_call(
        body, grid=(r // tr,),
        in_specs=[pl.BlockSpec((tr, cdim), lambda i: (i, 0)), pl.BlockSpec((3, tr, cdim), lambda i: (0, i, 0))],
        out_specs=pl.BlockSpec((tr, cdim), lambda i: (i, 0)),
        out_shape=jax.ShapeDtypeStruct((r, cdim), F32),
        compiler_params=_params(("parallel",)), name="rs_sum_final",
    )(own, land).reshape(shape)


def _sum_devices(gathered):
    n, r, cdim = gathered.shape
    tr = _pick(r, (64, 32, 16, 8))

    def body(g_ref, out_ref):
        acc = g_ref[0]
        for q in range(1, n):
            acc = acc + g_ref[q]
        out_ref[...] = acc

    return pl.pallas_call(
        body, grid=(r // tr,),
        in_specs=[pl.BlockSpec((n, tr, cdim), lambda i: (0, i, 0))],
        out_specs=pl.BlockSpec((tr, cdim), lambda i: (i, 0)),
        out_shape=jax.ShapeDtypeStruct((r, cdim), F32),
        compiler_params=_params(("parallel",)), name="sum_devices",
    )(gathered)


def _adamw(w, g, m, v):
    shape = w.shape
    cols = shape[-1]
    rows = w.size // cols
    tr = _pick(rows, (512, 256, 128, 64, 32, 16, 8))
    c1 = 1.0 - ADAM_B1 ** ADAM_STEP
    c2 = 1.0 - ADAM_B2 ** ADAM_STEP

    def body(w_ref, g_ref, m_ref, v_ref, d_ref, nm_ref, nv_ref):
        gv = g_ref[...]
        m2 = ADAM_B1 * m_ref[...] + (1.0 - ADAM_B1) * gv
        v2 = ADAM_B2 * v_ref[...] + (1.0 - ADAM_B2) * (gv * gv)
        d_ref[...] = -ADAM_LR * ((m2 / c1) / (jnp.sqrt(v2 / c2) + ADAM_EPS) + ADAM_WD * w_ref[...])
        nm_ref[...] = m2
        nv_ref[...] = v2

    blk = pl.BlockSpec((tr, cols), lambda i: (i, 0))
    sds = jax.ShapeDtypeStruct((rows, cols), F32)
    outs = pl.pallas_call(
        body, grid=(rows // tr,), in_specs=[blk] * 4, out_specs=[blk] * 3, out_shape=[sds] * 3,
        compiler_params=_params(("parallel",)), name=f"adamw_{rows}x{cols}",
    )(*(t.reshape(rows, cols) for t in (w, g, m, v)))
    return tuple(o.reshape(shape) for o in outs)


WEIGHTS = ("norm_mix", "norm_ffn", "ssd_w_in", "ssd_conv_w", "ssd_conv_b", "ssd_dt_bias", "ssd_a_log", "ssd_d",
           "ssd_norm_w", "ssd_w_out", "gmlp_w_in", "gmlp_b_in", "gmlp_ln_w", "gmlp_ln_b", "gmlp_w_s", "gmlp_b_s",
           "gmlp_w_out", "ffn_w_gate", "ffn_w_up", "ffn_w_down", "ple_w_proj", "ple_norm", "ple_gate_norm",
           "ple_w_gate", "final_norm")
ARG_NAMES = ("x", "p") + WEIGHTS + ("loss_target",) + tuple("m_" + n for n in WEIGHTS) + tuple("v_" + n for n in WEIGHTS)
SHARD_AXIS = {"ssd_w_in": 2, "ssd_conv_w": 2, "ssd_w_out": 1, "gmlp_w_in": 2, "gmlp_b_in": 1, "gmlp_ln_w": 1,
              "gmlp_ln_b": 1, "gmlp_w_out": 1, "ffn_w_gate": 2, "ffn_w_up": 2, "ffn_w_down": 1, "ple_w_proj": 2,
              "ple_w_gate": 1}
GATHER_BF16 = ("ssd_w_in", "ssd_w_out", "gmlp_w_in", "gmlp_w_out", "ffn_w_gate", "ffn_w_up", "ffn_w_down",
               "ple_w_proj", "ple_w_gate")
GATHER_F32 = ("ssd_conv_w", "gmlp_b_in", "gmlp_ln_w", "gmlp_ln_b")
SHARDED = GATHER_BF16 + GATHER_F32
WIDE = ("ssd_w_in", "ffn_w_gate", "ffn_w_up")
REPLICATED = tuple(n for n in WEIGHTS if n not in SHARD_AXIS)


def _pack(arrs, dtype, row_mult, lead=0):
    flat = jnp.concatenate([t.reshape(t.shape[:lead] + (-1,)).astype(dtype) for t in arrs], axis=lead)
    n = flat.shape[-1]
    unit = row_mult * PACK_COLS
    total = -(-n // unit) * unit
    flat = jnp.pad(flat, [(0, 0)] * lead + [(0, total - n)])
    return flat.reshape(flat.shape[:lead] + (total // PACK_COLS, PACK_COLS))


def _unpack(buf, names, shapes, lead=0):
    flat = buf.reshape(buf.shape[:lead] + (-1,))
    out, off = {}, 0
    for n in names:
        size = math.prod(shapes[n])
        out[n] = lax.slice_in_dim(flat, off, off + size, axis=lead).reshape(buf.shape[:lead] + tuple(shapes[n]))
        off += size
    return out


ROW_PACKED = ((1024, ("ssd_w_out", "gmlp_w_out", "ffn_w_down", "ple_w_gate")), (512, ("gmlp_w_in",)),
              (128, ("ple_w_proj",)))
ROW_PACK_MULT = 1024


def _pack_rows(arrs, width, lead=0):
    parts = [t.reshape(t.shape[:lead] + (-1, width)).astype(BF16) for t in arrs]
    rows = sum(t.shape[lead] for t in parts)
    pad = -rows % ROW_PACK_MULT
    if pad:
        parts.append(jnp.zeros(parts[0].shape[:lead] + (pad, width), BF16))
    return jnp.concatenate(parts, axis=lead)


def _unpack_rows(buf, names, shapes, lead=0):
    width = buf.shape[-1]
    out, off = {}, 0
    for n in names:
        rows = math.prod(shapes[n]) // width
        out[n] = lax.slice_in_dim(buf, off, off + rows, axis=lead).reshape(buf.shape[:lead] + tuple(shapes[n]))
        off += rows
    return out


def _merge_shards(seg, ax):
    t = jnp.moveaxis(seg, 0, ax)
    return t.reshape(t.shape[:ax] + (t.shape[ax] * t.shape[ax + 1],) + t.shape[ax + 2:])


def _split_for_cores(gfull, ax, c):
    shp = gfull.shape
    t = gfull.reshape(shp[:ax] + (2, 2, 2, shp[ax] // N_DEV) + shp[ax + 1:])

    def take(core):
        u = lax.dynamic_index_in_dim(t, core, axis=ax + 2, keepdims=False)
        u = jnp.moveaxis(u, (ax, ax + 1), (0, 1))
        return u.reshape((4,) + u.shape[2:])

    return take(c), take(1 - c)


def kernel(x, p, norm_mix, norm_ffn, ssd_w_in, ssd_conv_w, ssd_conv_b, ssd_dt_bias, ssd_a_log, ssd_d,
           ssd_norm_w, ssd_w_out, gmlp_w_in, gmlp_b_in, gmlp_ln_w, gmlp_ln_b, gmlp_w_s, gmlp_b_s,
           gmlp_w_out, ffn_w_gate, ffn_w_up, ffn_w_down, ple_w_proj, ple_norm, ple_gate_norm, ple_w_gate,
           final_norm, loss_target, m_norm_mix, m_norm_ffn, m_ssd_w_in, m_ssd_conv_w, m_ssd_conv_b,
           m_ssd_dt_bias, m_ssd_a_log, m_ssd_d, m_ssd_norm_w, m_ssd_w_out, m_gmlp_w_in, m_gmlp_b_in,
           m_gmlp_ln_w, m_gmlp_ln_b, m_gmlp_w_s, m_gmlp_b_s, m_gmlp_w_out, m_ffn_w_gate, m_ffn_w_up,
           m_ffn_w_down, m_ple_w_proj, m_ple_norm, m_ple_gate_norm, m_ple_w_gate, m_final_norm, v_norm_mix,
           v_norm_ffn, v_ssd_w_in, v_ssd_conv_w, v_ssd_conv_b, v_ssd_dt_bias, v_ssd_a_log, v_ssd_d,
           v_ssd_norm_w, v_ssd_w_out, v_gmlp_w_in, v_gmlp_b_in, v_gmlp_ln_w, v_gmlp_ln_b, v_gmlp_w_s,
           v_gmlp_b_s, v_gmlp_w_out, v_ffn_w_gate, v_ffn_w_up, v_ffn_w_down, v_ple_w_proj, v_ple_norm,
           v_ple_gate_norm, v_ple_w_gate, v_final_norm):
    given = locals()
    a = {n: given[n] for n in ARG_NAMES}
    mx, my, c = _mesh_pos()
    xs = a["x"][0]
    ps = a["p"][:, 0]
    target = a["loss_target"][0]
    shard_shapes = {n: a[n].shape for n in WEIGHTS}

    full = {n: a[n] for n in REPLICATED}
    row_packs = [_pack_rows([a[n] for n in names], wd) for wd, names in ROW_PACKED]
    got = _all_gather(row_packs + [_pack([a[n] for n in GATHER_F32], F32, 8)] + [_to_bf16(a[n]) for n in WIDE],
                      "ag_weights")
    for (wd, names), buf in zip(ROW_PACKED, got):
        for n, seg in _unpack_rows(buf, names, shard_shapes, lead=1).items():
            full[n] = _merge_shards(seg, SHARD_AXIS[n])
    k0 = len(ROW_PACKED)
    for n, seg in _unpack(got[k0], GATHER_F32, shard_shapes, lead=1).items():
        full[n] = _merge_shards(seg, SHARD_AXIS[n])
    full["ssd_w_zx"], full["ssd_w_dt"] = _cat_ssd_in(got[k0 + 1])
    full["ffn_w_gu"] = _cat_ffn(got[k0 + 2], got[k0 + 3])

    lpart, dx, g = _local_step(xs, ps, target, _kernel_layouts(full))
    gfull = _reference_layouts(g, wide=False)
    loss = lax.psum(lpart, ("x", "y", "c"))

    def by_core(t):
        u = t.reshape((4, 2) + t.shape[1:])
        return (lax.dynamic_index_in_dim(u, c, axis=1, keepdims=False),
                lax.dynamic_index_in_dim(u, 1 - c, axis=1, keepdims=False))

    def layer_halves(gl, ax):
        if ax == 0:
            t = gl.reshape(4, 2, -1, gl.shape[-1])
            return tuple(lax.dynamic_index_in_dim(t, cc, axis=1, keepdims=False) for cc in (c, 1 - c))
        t = gl.reshape(gl.shape[0], 4, 2, -1)
        return tuple(jnp.moveaxis(lax.dynamic_index_in_dim(t, cc, axis=2, keepdims=False), 1, 0) for cc in (c, 1 - c))

    pairs = []
    for wd, names in ROW_PACKED:
        hs = [layer_halves(gl, SHARD_AXIS[n] - 1) for n in names for gl in g[n]]
        pairs.append(tuple(_pack_rows([h[i] for h in hs], wd, lead=1) for i in (0, 1)))
    halves = [_split_for_cores(gfull[n], SHARD_AXIS[n], c) for n in GATHER_F32]
    pairs.append((_pack([h[0] for h in halves], BF16, 16, lead=1), _pack([h[1] for h in halves], BF16, 16, lead=1)))
    pairs += [by_core(t) for t in (_split_ssd_in(g["ssd_w_zx"], g["ssd_w_dt"]),) + tuple(_split_ffn(g["ffn_w_gu"]))]
    landed = _exchange_sibling([s for _, s in pairs])
    partials = [_sum_pairs(k, l) for (k, _), l in zip(pairs, landed)]
    landed = _exchange_chips(partials)
    sums = [_sum_final(lax.dynamic_index_in_dim(t, 2 * mx + my, axis=0, keepdims=False), l)
            for t, l in zip(partials, landed)]
    gshard = {}
    for (wd, names), buf in zip(ROW_PACKED, sums):
        gshard.update(_unpack_rows(buf, names, shard_shapes))
    gshard.update(_unpack(sums[k0], GATHER_F32, shard_shapes))
    gshard.update(zip(WIDE, sums[k0 + 1:]))
    rep = _all_gather([_pack([gfull[n] for n in REPLICATED], F32, 64)], "ag_replicated_grads")[0]
    grep = _unpack(_sum_devices(rep), REPLICATED, shard_shapes)
    grads = {**gshard, **grep}

    upd = {n: _adamw(a[n], grads[n], a["m_" + n], a["v_" + n]) for n in WEIGHTS}
    return (loss, dx[None], *[grads[n] for n in WEIGHTS], *[upd[n][0] for n in WEIGHTS],
            *[upd[n][1] for n in WEIGHTS], *[upd[n][2] for n in WEIGHTS])
```

```python
import functools
import math

import jax
import jax.numpy as jnp
from jax import lax
from jax.experimental import pallas as pl
from jax.experimental.pallas import tpu as pltpu

F32 = jnp.float32
BF16 = jnp.bfloat16

N_DEV = 8
D_MODEL = 1024
DEPTH = 4
SSD_INNER = 2048
SSD_HEADS = 32
SSD_HEADDIM = 64
SSD_GROUPS = 8
SSD_STATE = 128
SSD_GROUP_W = SSD_INNER // SSD_GROUPS
SSD_CONV_DIM = SSD_INNER + 2 * SSD_GROUPS * SSD_STATE
SSD_IN_DIM = 2 * SSD_INNER + SSD_CONV_DIM - SSD_INNER + SSD_HEADS
SSD_ZX = SSD_INNER + SSD_CONV_DIM
CONV_K = 4
CHUNK = 128
GMLP_INNER = 2048
GMLP_GROUPS = 16
FFN_DIM = 2816
PLE_DIM = 256
RMS_EPS = 1e-6
LN_EPS = 1e-5
LANES = 128
VMEM_LIMIT = 56 * 1024 * 1024

ADAM_LR = 0.001
ADAM_B1 = 0.9
ADAM_B2 = 0.999
ADAM_EPS = 1e-08
ADAM_WD = 0.01
ADAM_STEP = 10

MESH_ID = pl.DeviceIdType.MESH


def _pick(n, cands):
    for c in cands:
        if c <= n and n % c == 0:
            return c
    return n


def _params(dims):
    return pltpu.CompilerParams(dimension_semantics=dims, vmem_limit_bytes=VMEM_LIMIT)


def _dot(a, b, dims=(((1,), (0,)), ((), ())), precision=None):
    return lax.dot_general(a, b, dims, precision=precision, preferred_element_type=F32)


NN = (((1,), (0,)), ((), ()))
NT = (((1,), (1,)), ((), ()))
TN = (((0,), (0,)), ((), ()))


def _sigmoid(x):
    return 1.0 / (1.0 + jnp.exp(-x))


def _dot01(a, b, dims, split):
    v = (a, b)[split]
    hi = v.astype(BF16)
    r1 = v - hi.astype(F32)
    mid = r1.astype(BF16)
    lo = (r1 - mid.astype(F32)).astype(BF16)
    ones = (a, b)[1 - split].astype(BF16)
    terms = [(t, ones) if split == 0 else (ones, t) for t in (hi, mid, lo)]
    return _dot(*terms[0], dims) + _dot(*terms[1], dims) + _dot(*terms[2], dims)


MM_VMEM_BUDGET = 36 * 1024 * 1024


def _mm_tiles(mode, m, n, k, a_bytes, b_bytes, out_bytes, has_add):
    tm = _pick(m, (1408, 1024, 512, 256, 128))
    tn_cands = [c for c in (2816, 1024, 512, 256, 128) if c <= n and n % c == 0] or [n]
    tk_cands = [k] + [c for c in (2816, 2048, 1024, 512, 256, 128) if c < k and k % c == 0]
    for tk in tk_cands:
        for tn in tn_cands:
            blocks = tm * tk * a_bytes + tk * tn * b_bytes + tm * tn * (out_bytes + (4 if has_add else 0))
            if 2 * blocks + (tm * tn * 4 if tk < k else 0) <= MM_VMEM_BUDGET:
                return tm, tn, tk
    return tm, tn_cands[-1], tk_cands[-1]


def _mm(a, b, mode, out_dtype, add=None):
    if mode == "nn":
        m, k = a.shape
        n = b.shape[1]
    elif mode == "nt":
        m, k = a.shape
        n = b.shape[0]
    else:
        k, m = a.shape
        n = b.shape[1]
    tm, tn, tk = _mm_tiles(mode, m, n, k, a.dtype.itemsize, b.dtype.itemsize, jnp.dtype(out_dtype).itemsize,
                           add is not None)
    nk = k // tk
    dims = {"nn": NN, "nt": NT, "tn": TN}[mode]

    def body(*refs):
        if add is None:
            a_ref, b_ref, o_ref = refs[:3]
            add_ref = None
            rest = refs[3:]
        else:
            a_ref, b_ref, add_ref, o_ref = refs[:4]
            rest = refs[4:]
        part = _dot(a_ref[...].astype(BF16), b_ref[...].astype(BF16), dims)

        def finish(acc):
            if add_ref is not None:
                acc = acc + add_ref[...]
            o_ref[...] = acc.astype(o_ref.dtype)

        if nk == 1:
            finish(part)
        else:
            acc_ref = rest[0]
            kk = pl.program_id(2)

            @pl.when(kk == 0)
            def _():
                acc_ref[...] = part

            @pl.when(kk > 0)
            def _():
                acc_ref[...] += part

            @pl.when(kk == nk - 1)
            def _():
                finish(acc_ref[...])

    if mode == "nn":
        a_spec = pl.BlockSpec((tm, tk), lambda i, j, kk: (i, kk))
        b_spec = pl.BlockSpec((tk, tn), lambda i, j, kk: (kk, j))
    elif mode == "nt":
        a_spec = pl.BlockSpec((tm, tk), lambda i, j, kk: (i, kk))
        b_spec = pl.BlockSpec((tn, tk), lambda i, j, kk: (j, kk))
    else:
        a_spec = pl.BlockSpec((tk, tm), lambda i, j, kk: (kk, i))
        b_spec = pl.BlockSpec((tk, tn), lambda i, j, kk: (kk, j))
    o_spec = pl.BlockSpec((tm, tn), lambda i, j, kk: (i, j))
    in_specs = [a_spec, b_spec] + ([o_spec] if add is not None else [])
    args = (a, b) + ((add,) if add is not None else ())
    return pl.pallas_call(
        body,
        grid=(m // tm, n // tn, nk),
        in_specs=in_specs,
        out_specs=o_spec,
        out_shape=jax.ShapeDtypeStruct((m, n), out_dtype),
        scratch_shapes=[pltpu.VMEM((tm, tn), F32)] if nk > 1 else [],
        compiler_params=_params(("parallel", "parallel", "arbitrary")),
        name=f"mm_{mode}_{m}x{k}x{n}",
    )(*args)


def _rms_fwd(x, w):
    s, d = x.shape
    tr = _pick(s, (512, 256, 128))

    def body(x_ref, w_ref, o_ref):
        xv = x_ref[...]
        r = lax.rsqrt(jnp.mean(xv * xv, axis=-1, keepdims=True) + RMS_EPS)
        o_ref[...] = (xv * r * w_ref[...]).astype(o_ref.dtype)

    return pl.pallas_call(
        body,
        grid=(s // tr,),
        in_specs=[pl.BlockSpec((tr, d), lambda i: (i, 0)), pl.BlockSpec((1, d), lambda i: (0, 0))],
        out_specs=pl.BlockSpec((tr, d), lambda i: (i, 0)),
        out_shape=jax.ShapeDtypeStruct((s, d), BF16),
        compiler_params=_params(("parallel",)),
        name="rms_fwd",
    )(x, w)


def _rms_bwd(dyn, x, w, add):
    s, d = x.shape
    tr = _pick(s, (512, 256, 128))

    def body(dy_ref, x_ref, w_ref, add_ref, dx_ref, dw_ref):
        xv = x_ref[...]
        dy = dy_ref[...].astype(F32)
        r = lax.rsqrt(jnp.mean(xv * xv, axis=-1, keepdims=True) + RMS_EPS)
        xn = xv * r
        dxh = dy * w_ref[...]
        dx = r * (dxh - xn * jnp.mean(dxh * xn, axis=-1, keepdims=True))
        dx_ref[...] = add_ref[...] + dx
        part = jnp.sum(dy * xn, axis=0, keepdims=True)

        @pl.when(pl.program_id(0) == 0)
        def _():
            dw_ref[...] = part

        @pl.when(pl.program_id(0) > 0)
        def _():
            dw_ref[...] += part

    row = pl.BlockSpec((tr, d), lambda i: (i, 0))
    vec = pl.BlockSpec((1, d), lambda i: (0, 0))
    return pl.pallas_call(
        body,
        grid=(s // tr,),
        in_specs=[row, row, vec, row],
        out_specs=[row, vec],
        out_shape=[jax.ShapeDtypeStruct((s, d), F32), jax.ShapeDtypeStruct((1, d), F32)],
        compiler_params=_params(("arbitrary",)),
        name="rms_bwd",
    )(dyn, x, w, add)


CONV_ROWS = 256
CONV_COLS = 256
CONV_HALO = 128


def _shift_matrix(offsets):
    ti = lax.broadcasted_iota(jnp.int32, (CONV_ROWS, CONV_ROWS + CONV_HALO), 0)
    ji = lax.broadcasted_iota(jnp.int32, (CONV_ROWS, CONV_ROWS + CONV_HALO), 1)
    return jnp.concatenate([jnp.where(ji == ti + o, 1.0, 0.0).astype(BF16) for o in offsets], axis=0)


def _conv_window(x_ref, i):
    r0 = pl.multiple_of(i * CONV_ROWS, CONV_ROWS)
    cur = x_ref[pl.ds(r0, CONV_ROWS), :]
    p0 = pl.multiple_of(jnp.maximum(r0 - CONV_HALO, 0), CONV_HALO)
    prev = x_ref[pl.ds(p0, CONV_HALO), :]
    prev = jnp.where(i == 0, jnp.zeros_like(prev), prev)
    return r0, jnp.concatenate([prev, cur], axis=0)


def _conv_taps(shifted, w):
    taps = [shifted[k * CONV_ROWS:(k + 1) * CONV_ROWS] for k in range(CONV_K)]
    acc = w[0:1, :] * taps[0]
    for k in range(1, CONV_K):
        acc = acc + w[k:k + 1, :] * taps[k]
    return taps, acc


def _ssd_conv_fwd(zx, conv_w, conv_b):
    s = zx.shape[0]
    c = SSD_CONV_DIM
    nsteps = s // CONV_ROWS
    off = SSD_INNER // CONV_COLS

    def body(x_ref, w_ref, b_ref, o_ref):
        w = w_ref[...]
        b = b_ref[...]
        causal = _shift_matrix([CONV_HALO - (CONV_K - 1) + k for k in range(CONV_K)])

        def step(i, carry):
            r0, ext = _conv_window(x_ref, i)
            _, acc = _conv_taps(_dot(causal, ext, NN), w)
            acc = acc + b
            o_ref[pl.ds(r0, CONV_ROWS), :] = (acc * _sigmoid(acc)).astype(o_ref.dtype)
            return carry

        lax.fori_loop(0, nsteps, step, 0)

    return pl.pallas_call(
        body,
        grid=(c // CONV_COLS,),
        in_specs=[pl.BlockSpec((s, CONV_COLS), lambda j: (0, j + off)),
                  pl.BlockSpec((8, CONV_COLS), lambda j: (0, j)),
                  pl.BlockSpec((1, CONV_COLS), lambda j: (0, j))],
        out_specs=pl.BlockSpec((s, CONV_COLS), lambda j: (0, j)),
        out_shape=jax.ShapeDtypeStruct((s, c), BF16),
        compiler_params=_params(("parallel",)),
        name="ssd_conv_fwd",
    )(zx, conv_w, conv_b)


def _ssd_conv_bwd(zx, dxbc, conv_w, conv_b, dzx):
    s = zx.shape[0]
    c = SSD_CONV_DIM
    nsteps = s // CONV_ROWS
    off = SSD_INNER // CONV_COLS

    def body(x_ref, dy_ref, w_ref, b_ref, dzx_in_ref, dx_ref, dw_ref, db_ref, dc_ref):
        w = w_ref[...]
        b = b_ref[...]
        dc_ref[pl.ds(s, CONV_HALO), :] = jnp.zeros((CONV_HALO, CONV_COLS), dc_ref.dtype)
        causal = _shift_matrix([CONV_HALO - (CONV_K - 1) + k for k in range(CONV_K)])
        anticausal = _shift_matrix([CONV_K - 1 - k for k in range(CONV_K)])

        def step1(i, carry):
            dw0, dw1, dw2, dw3, dbs = carry
            r0, ext = _conv_window(x_ref, i)
            taps, acc = _conv_taps(_dot(causal, ext, NN), w)
            acc = acc + b
            sg = _sigmoid(acc)
            dcv = dy_ref[pl.ds(r0, CONV_ROWS), :].astype(F32) * (sg * (1.0 + acc * (1.0 - sg)))
            dc_ref[pl.ds(r0, CONV_ROWS), :] = dcv.astype(dc_ref.dtype)
            dws = [jnp.sum(dcv * taps[k], axis=0, keepdims=True) for k in range(CONV_K)]
            return (dw0 + dws[0], dw1 + dws[1], dw2 + dws[2], dw3 + dws[3], dbs + jnp.sum(dcv, axis=0, keepdims=True))

        z = jnp.zeros((1, CONV_COLS), F32)
        dw0, dw1, dw2, dw3, dbs = lax.fori_loop(0, nsteps, step1, (z, z, z, z, z))
        dw_ref[...] = jnp.concatenate([dw0, dw1, dw2, dw3, z, z, z, z], axis=0)
        db_ref[...] = dbs

        def step2(i, carry):
            r0 = pl.multiple_of(i * CONV_ROWS, CONV_ROWS)
            ext = dc_ref[pl.ds(r0, CONV_ROWS + CONV_HALO), :]
            _, acc = _conv_taps(_dot(anticausal, ext, NN), w)
            dx_ref[pl.ds(r0, CONV_ROWS), :] = acc.astype(dx_ref.dtype)
            return carry

        lax.fori_loop(0, nsteps, step2, 0)

    col = pl.BlockSpec((s, CONV_COLS), lambda j: (0, j))
    shifted = pl.BlockSpec((s, CONV_COLS), lambda j: (0, j + off))
    return pl.pallas_call(
        body,
        grid=(c // CONV_COLS,),
        in_specs=[shifted, col,
                  pl.BlockSpec((8, CONV_COLS), lambda j: (0, j)),
                  pl.BlockSpec((1, CONV_COLS), lambda j: (0, j)),
                  pl.BlockSpec(memory_space=pl.ANY)],
        out_specs=[shifted, pl.BlockSpec((8, CONV_COLS), lambda j: (0, j)), pl.BlockSpec((1, CONV_COLS), lambda j: (0, j))],
        out_shape=[jax.ShapeDtypeStruct((s, SSD_ZX), BF16), jax.ShapeDtypeStruct((8, c), F32),
                   jax.ShapeDtypeStruct((1, c), F32)],
        scratch_shapes=[pltpu.VMEM((s + CONV_HALO, CONV_COLS), BF16)],
        input_output_aliases={4: 0},
        compiler_params=_params(("parallel",)),
        name="ssd_conv_bwd",
    )(zx, dxbc, conv_w, conv_b, dzx)


def _ssd_consts():
    li = lax.broadcasted_iota(jnp.int32, (CHUNK, CHUNK), 0)
    si = lax.broadcasted_iota(jnp.int32, (CHUNK, CHUNK), 1)
    tril = li >= si
    hrow = lax.broadcasted_iota(jnp.int32, (LANES, SSD_INNER), 0)
    hcol = lax.broadcasted_iota(jnp.int32, (LANES, SSD_INNER), 1) // SSD_HEADDIM
    expand = (hrow == hcol).astype(F32)
    return tril, expand


def _ssd_chunk_common(dtp_ref, bias_ref, alog_ref, tril, expand):
    lane = lax.broadcasted_iota(jnp.int32, (1, LANES), 1)
    valid = lane < SSD_HEADS
    pre = dtp_ref[...] + bias_ref[...]
    dt = jnp.where(valid, jnp.maximum(pre, 0.0) + jnp.log1p(jnp.exp(-jnp.abs(pre))), 0.0)
    a = jnp.where(valid, -jnp.exp(alog_ref[...]), 0.0)
    da = dt * a
    cs = _dot01(tril.astype(F32), da, NN, 1)
    cs_x = _dot01(cs, expand, NN, 0)
    dt_x = _dot01(dt, expand, NN, 0)
    return pre, dt, a, cs, cs_x, dt_x


def _ssd_scan_fwd(xbc, dtp, dt_bias, a_log, d_skip):
    s = xbc.shape[0]
    nc = s // CHUNK
    gw = SSD_GROUP_W

    def body(xbc_ref, dtp_ref, bias_ref, alog_ref, d_ref, y_ref, prev_ref, state_ref):
        c = pl.program_id(0)

        @pl.when(c == 0)
        def _():
            state_ref[...] = jnp.zeros_like(state_ref)

        tril, expand = _ssd_consts()
        pre, dt, a, cs, cs_x, dt_x = _ssd_chunk_common(dtp_ref, bias_ref, alog_ref, tril, expand)
        cs_t = cs.T
        d_x = _dot01(jnp.broadcast_to(d_ref[...], (8, LANES)), expand, NN, 0)[0:1, :]
        cs_last = cs_x[CHUNK - 1:CHUNK, :]
        dec_out = jnp.exp(cs_x)
        dec_st = jnp.exp(cs_last - cs_x)
        dec_ch = jnp.exp(cs_last)
        x = xbc_ref[:, 0:SSD_INNER].astype(F32)
        xr = x * dt_x
        xrs = xr * dec_st
        lane_g = lax.broadcasted_iota(jnp.int32, (1, gw), 1) // SSD_HEADDIM
        for g in range(SSD_GROUPS):
            sl = slice(g * gw, (g + 1) * gw)
            bg = xbc_ref[:, SSD_INNER + g * SSD_STATE:SSD_INNER + (g + 1) * SSD_STATE]
            cg = xbc_ref[:, SSD_INNER + (SSD_GROUPS + g) * SSD_STATE:SSD_INNER + (SSD_GROUPS + g + 1) * SSD_STATE]
            cb = _dot(cg, bg, NT)
            prev_g = state_ref[:, sl]
            prev_ref[0, :, sl] = prev_g
            yo = _dot(cg, prev_g.astype(BF16), NN) * dec_out[:, sl]
            xr_g = xr[:, sl]
            yd = jnp.zeros((CHUNK, gw), F32)
            for r in range(SSD_HEADS // SSD_GROUPS):
                h = g * (SSD_HEADS // SSD_GROUPS) + r
                diff = cs[:, h:h + 1] - cs_t[h:h + 1, :]
                lmat = jnp.exp(jnp.where(tril, diff, -1e30))
                wmat = (cb * lmat).astype(BF16)
                xr_h = jnp.where(lane_g == r, xr_g, 0.0).astype(BF16)
                yd = yd + _dot(wmat, xr_h, NN)
            y_ref[:, sl] = yd + yo + x[:, sl] * d_x[:, sl]
            sc = _dot(bg, xrs[:, sl].astype(BF16), TN)
            state_ref[:, sl] = prev_g * dec_ch[:, sl] + sc

    vec = pl.BlockSpec((1, LANES), lambda c: (0, 0))
    return pl.pallas_call(
        body,
        grid=(nc,),
        in_specs=[pl.BlockSpec((CHUNK, SSD_CONV_DIM), lambda c: (c, 0)),
                  pl.BlockSpec((CHUNK, LANES), lambda c: (c, 0)), vec, vec, vec],
        out_specs=[pl.BlockSpec((CHUNK, SSD_INNER), lambda c: (c, 0)),
                   pl.BlockSpec((1, SSD_STATE, SSD_INNER), lambda c: (c, 0, 0))],
        out_shape=[jax.ShapeDtypeStruct((s, SSD_INNER), F32), jax.ShapeDtypeStruct((nc, SSD_STATE, SSD_INNER), F32)],
        scratch_shapes=[pltpu.VMEM((SSD_STATE, SSD_INNER), F32)],
        compiler_params=_params(("arbitrary",)),
        name="ssd_scan_fwd",
    )(xbc, dtp, dt_bias, a_log, d_skip)


def _ssd_scan_bwd(xbc, dtp, prev, dy, dt_bias, a_log, d_skip):
    s = xbc.shape[0]
    nc = s // CHUNK
    gw = SSD_GROUP_W
    hpg = SSD_HEADS // SSD_GROUPS

    def body(xbc_ref, dtp_ref, prev_ref, dy_ref, bias_ref, alog_ref, d_ref,
             dxbc_ref, ddtp_ref, dbias_ref, dalog_ref, dd_ref, dp_ref, ddx_ref):
        step = pl.program_id(0)

        @pl.when(step == 0)
        def _():
            dp_ref[...] = jnp.zeros_like(dp_ref)
            ddx_ref[...] = jnp.zeros_like(ddx_ref)
            dbias_ref[...] = jnp.zeros_like(dbias_ref)
            dalog_ref[...] = jnp.zeros_like(dalog_ref)

        tril, expand = _ssd_consts()
        pre, dt, a, cs, cs_x, dt_x = _ssd_chunk_common(dtp_ref, bias_ref, alog_ref, tril, expand)
        cs_t = cs.T
        d_x = _dot01(jnp.broadcast_to(d_ref[...], (8, LANES)), expand, NN, 0)[0:1, :]
        cs_last = cs_x[CHUNK - 1:CHUNK, :]
        dec_out = jnp.exp(cs_x)
        dec_st = jnp.exp(cs_last - cs_x)
        dec_ch = jnp.exp(cs_last)
        x = xbc_ref[:, 0:SSD_INNER].astype(F32)
        dyv = dy_ref[...]
        xr = x * dt_x
        xrs = xr * dec_st
        lane_g = lax.broadcasted_iota(jnp.int32, (1, gw), 1) // SSD_HEADDIM
        hsel = lax.broadcasted_iota(jnp.int32, (CHUNK, LANES), 1)
        dcs = jnp.zeros((CHUNK, LANES), F32)
        last_parts = []
        t_parts = []
        dxr_parts = []
        for g in range(SSD_GROUPS):
            sl = slice(g * gw, (g + 1) * gw)
            bsl = slice(SSD_INNER + g * SSD_STATE, SSD_INNER + (g + 1) * SSD_STATE)
            csl = slice(SSD_INNER + (SSD_GROUPS + g) * SSD_STATE, SSD_INNER + (SSD_GROUPS + g + 1) * SSD_STATE)
            bg = xbc_ref[:, bsl]
            cg = xbc_ref[:, csl]
            cb = _dot(cg, bg, NT)
            prev_g = prev_ref[0, :, sl]
            prev_b = prev_g.astype(BF16)
            dp_g = dp_ref[:, sl]
            dp_b = dp_g.astype(BF16)
            dy_g = dyv[:, sl]
            xr_g = xr[:, sl]
            gmat = _dot(cg, prev_b, NN)
            dgm = (dy_g * dec_out[:, sl]).astype(BF16)
            dc_g = _dot(dgm, prev_b, NT)
            dprev = _dot(cg, dgm, TN)
            t1 = dy_g * gmat * dec_out[:, sl]
            mm_ = _dot(bg, dp_b, NN)
            db_g = _dot(xrs[:, sl].astype(BF16), dp_b, NT)
            dxr_g = mm_ * dec_st[:, sl]
            t2 = dxr_g * xr_g
            last = jnp.sum(t2, axis=0, keepdims=True) + jnp.sum(dp_g * prev_g, axis=0, keepdims=True) * dec_ch[:, sl]
            dp_ref[:, sl] = dp_g * dec_ch[:, sl] + dprev
            dcb = jnp.zeros((CHUNK, CHUNK), F32)
            for r in range(hpg):
                h = g * hpg + r
                diff = cs[:, h:h + 1] - cs_t[h:h + 1, :]
                lmat = jnp.exp(jnp.where(tril, diff, -1e30))
                wmat = cb * lmat
                dy_h = jnp.where(lane_g == r, dy_g, 0.0).astype(BF16)
                dw = _dot(dy_h, xr_g.astype(BF16), NT)
                dxr_g = dxr_g + _dot(wmat.astype(BF16), dy_h, TN)
                dcb = dcb + dw * lmat
                q = (dw * wmat).astype(BF16)
                onehot = (hsel == h).astype(BF16)
                dcs = dcs + _dot(q, onehot, NN) - _dot(q, onehot, TN)
            dcb_b = dcb.astype(BF16)
            dc_g = dc_g + _dot(dcb_b, bg, NN)
            db_g = db_g + _dot(dcb_b, cg, TN)
            dxbc_ref[:, bsl] = db_g.astype(dxbc_ref.dtype)
            dxbc_ref[:, csl] = dc_g.astype(dxbc_ref.dtype)
            t_parts.append(t1 - t2)
            last_parts.append(last)
            dxr_parts.append(dxr_g)
        dxr = jnp.concatenate(dxr_parts, axis=1)
        tt = jnp.concatenate(t_parts, axis=1)
        last_x = jnp.concatenate(last_parts, axis=1)
        dxbc_ref[:, 0:SSD_INNER] = (dxr * dt_x + dyv * d_x).astype(dxbc_ref.dtype)
        dcs = dcs + _dot01(tt, expand, NT, 0)
        last_h = _dot01(jnp.broadcast_to(last_x, (8, SSD_INNER)), expand, NT, 0)[0:1, :]
        rowi = lax.broadcasted_iota(jnp.int32, (CHUNK, LANES), 0)
        dcs = dcs + jnp.where(rowi == CHUNK - 1, last_h, 0.0)
        dda = _dot01(tril.astype(F32), dcs, TN, 1)
        ddt = dda * a + _dot01(dxr * x, expand, NT, 0)
        dpre = ddt * _sigmoid(pre)
        ddtp_ref[...] = dpre
        dbias_ref[...] += jnp.sum(dpre, axis=0, keepdims=True)
        dalog_ref[...] += jnp.sum(dda * dt, axis=0, keepdims=True) * a
        ddx_ref[...] += jnp.broadcast_to(jnp.sum(dyv * x, axis=0, keepdims=True), (8, SSD_INNER))

        @pl.when(step == nc - 1)
        def _():
            dd_ref[...] = _dot01(ddx_ref[...], expand, NT, 0)[0:1, :]

    rev = lambda c: (nc - 1 - c, 0)
    vec = pl.BlockSpec((1, LANES), lambda c: (0, 0))
    return pl.pallas_call(
        body,
        grid=(nc,),
        in_specs=[pl.BlockSpec((CHUNK, SSD_CONV_DIM), rev), pl.BlockSpec((CHUNK, LANES), rev),
                  pl.BlockSpec((1, SSD_STATE, SSD_INNER), lambda c: (nc - 1 - c, 0, 0)),
                  pl.BlockSpec((CHUNK, SSD_INNER), rev), vec, vec, vec],
        out_specs=[pl.BlockSpec((CHUNK, SSD_CONV_DIM), rev), pl.BlockSpec((CHUNK, LANES), rev), vec, vec, vec],
        out_shape=[jax.ShapeDtypeStruct((s, SSD_CONV_DIM), BF16), jax.ShapeDtypeStruct((s, LANES), F32),
                   jax.ShapeDtypeStruct((1, LANES), F32), jax.ShapeDtypeStruct((1, LANES), F32),
                   jax.ShapeDtypeStruct((1, LANES), F32)],
        scratch_shapes=[pltpu.VMEM((SSD_STATE, SSD_INNER), F32), pltpu.VMEM((8, SSD_INNER), F32)],
        compiler_params=_params(("arbitrary",)),
        name="ssd_scan_bwd",
    )(xbc, dtp, prev, dy, dt_bias, a_log, d_skip)


def _ssd_gate_fwd(y, zx, norm_w):
    s = y.shape[0]
    tr = _pick(s, (256, 128))
    gw = SSD_GROUP_W

    def body(y_ref, z_ref, w_ref, o_ref):
        for g in range(SSD_GROUPS):
            sl = slice(g * gw, (g + 1) * gw)
            z = z_ref[:, sl].astype(F32)
            gv = y_ref[:, sl] * (z * _sigmoid(z))
            r = lax.rsqrt(jnp.mean(gv * gv, axis=-1, keepdims=True) + LN_EPS)
            o_ref[:, sl] = (gv * r * w_ref[:, sl]).astype(o_ref.dtype)

    row = pl.BlockSpec((tr, SSD_INNER), lambda i: (i, 0))
    return pl.pallas_call(
        body,
        grid=(s // tr,),
        in_specs=[row, row, pl.BlockSpec((1, SSD_INNER), lambda i: (0, 0))],
        out_specs=row,
        out_shape=jax.ShapeDtypeStruct((s, SSD_INNER), BF16),
        compiler_params=_params(("parallel",)),
        name="ssd_gate_fwd",
    )(y, zx, norm_w)


def _ssd_gate_bwd(dgn, y, zx, norm_w):
    s = y.shape[0]
    tr = _pick(s, (256, 128))
    gw = SSD_GROUP_W

    def body(dg_ref, y_ref, z_ref, w_ref, dy_ref, dz_ref, dw_ref):
        parts = []
        for g in range(SSD_GROUPS):
            sl = slice(g * gw, (g + 1) * gw)
            z = z_ref[:, sl].astype(F32)
            yv = y_ref[:, sl]
            sg = _sigmoid(z)
            sz = z * sg
            gv = yv * sz
            r = lax.rsqrt(jnp.mean(gv * gv, axis=-1, keepdims=True) + LN_EPS)
            gn = gv * r
            dout = dg_ref[:, sl].astype(F32)
            parts.append(jnp.sum(dout * gn, axis=0, keepdims=True))
            dgn_ = dout * w_ref[:, sl]
            dgv = r * (dgn_ - gn * jnp.mean(dgn_ * gn, axis=-1, keepdims=True))
            dy_ref[:, sl] = dgv * sz
            dz_ref[:, sl] = (dgv * yv * (sg * (1.0 + z * (1.0 - sg)))).astype(dz_ref.dtype)
        part = jnp.concatenate(parts, axis=1)

        @pl.when(pl.program_id(0) == 0)
        def _():
            dw_ref[...] = part

        @pl.when(pl.program_id(0) > 0)
        def _():
            dw_ref[...] += part

    row = pl.BlockSpec((tr, SSD_INNER), lambda i: (i, 0))
    vec = pl.BlockSpec((1, SSD_INNER), lambda i: (0, 0))
    return pl.pallas_call(
        body,
        grid=(s // tr,),
        in_specs=[row, row, row, vec],
        out_specs=[row, row, vec],
        out_shape=[jax.ShapeDtypeStruct((s, SSD_INNER), F32), jax.ShapeDtypeStruct((s, SSD_ZX), BF16),
                   jax.ShapeDtypeStruct((1, SSD_INNER), F32)],
        compiler_params=_params(("arbitrary",)),
        name="ssd_gate_bwd",
    )(dgn, y, zx, norm_w)


INV_SQRT2 = 1.0 / math.sqrt(2.0)
INV_SQRT2PI = 1.0 / math.sqrt(2.0 * math.pi)


def _gelu(x):
    return 0.5 * x * (1.0 + lax.erf(x * INV_SQRT2))


def _gelu_grad(x):
    return 0.5 * (1.0 + lax.erf(x * INV_SQRT2)) + x * INV_SQRT2PI * jnp.exp(-0.5 * x * x)


def _gmlp_act_fwd(pre, b_in, ln_w, ln_b):
    s = pre.shape[0]
    tr = _pick(s, (256, 128))
    n = GMLP_INNER

    def body(p_ref, b_ref, w_ref, lb_ref, u_ref, v_ref):
        u_ref[...] = _gelu(p_ref[:, 0:n].astype(F32) + b_ref[:, 0:n]).astype(u_ref.dtype)
        hv = _gelu(p_ref[:, n:2 * n].astype(F32) + b_ref[:, n:2 * n])
        mu = jnp.mean(hv, axis=-1, keepdims=True)
        xc = hv - mu
        r = lax.rsqrt(jnp.mean(xc * xc, axis=-1, keepdims=True) + LN_EPS)
        v_ref[...] = (xc * r * w_ref[...] + lb_ref[...]).astype(v_ref.dtype)

    half = pl.BlockSpec((tr, n), lambda i: (i, 0))
    vec = pl.BlockSpec((1, n), lambda i: (0, 0))
    return pl.pallas_call(
        body,
        grid=(s // tr,),
        in_specs=[pl.BlockSpec((tr, 2 * n), lambda i: (i, 0)), pl.BlockSpec((1, 2 * n), lambda i: (0, 0)), vec, vec],
        out_specs=[half, half],
        out_shape=[jax.ShapeDtypeStruct((s, n), BF16), jax.ShapeDtypeStruct((s, n), BF16)],
        compiler_params=_params(("parallel",)),
        name="gmlp_act_fwd",
    )(pre, b_in, ln_w, ln_b)


def _gmlp_act_bwd(pre, b_in, ln_w, du, dv):
    s = pre.shape[0]
    tr = _pick(s, (256, 128))
    n = GMLP_INNER

    def body(p_ref, b_ref, w_ref, du_ref, dv_ref, dp_ref, db_ref, dw_ref, dlb_ref):
        xu = p_ref[:, 0:n].astype(F32) + b_ref[:, 0:n]
        dpu = du_ref[...].astype(F32) * _gelu_grad(xu)
        xv = p_ref[:, n:2 * n].astype(F32) + b_ref[:, n:2 * n]
        hv = _gelu(xv)
        mu = jnp.mean(hv, axis=-1, keepdims=True)
        xc = hv - mu
        r = lax.rsqrt(jnp.mean(xc * xc, axis=-1, keepdims=True) + LN_EPS)
        vh = xc * r
        dvv = dv_ref[...].astype(F32)
        dvh = dvv * w_ref[...]
        dh = r * (dvh - jnp.mean(dvh, axis=-1, keepdims=True) - vh * jnp.mean(dvh * vh, axis=-1, keepdims=True))
        dpv = dh * _gelu_grad(xv)
        dp_ref[:, 0:n] = dpu.astype(dp_ref.dtype)
        dp_ref[:, n:2 * n] = dpv.astype(dp_ref.dtype)
        pb = jnp.concatenate([jnp.sum(dpu, axis=0, keepdims=True), jnp.sum(dpv, axis=0, keepdims=True)], axis=1)
        pw = jnp.sum(dvv * vh, axis=0, keepdims=True)
        plb = jnp.sum(dvv, axis=0, keepdims=True)

        @pl.when(pl.program_id(0) == 0)
        def _():
            db_ref[...] = pb
            dw_ref[...] = pw
            dlb_ref[...] = plb

        @pl.when(pl.program_id(0) > 0)
        def _():
            db_ref[...] += pb
            dw_ref[...] += pw
            dlb_ref[...] += plb

    half = pl.BlockSpec((tr, n), lambda i: (i, 0))
    full = pl.BlockSpec((tr, 2 * n), lambda i: (i, 0))
    vec = pl.BlockSpec((1, n), lambda i: (0, 0))
    vec2 = pl.BlockSpec((1, 2 * n), lambda i: (0, 0))
    return pl.pallas_call(
        body,
        grid=(s // tr,),
        in_specs=[full, vec2, vec, half, half],
        out_specs=[full, vec2, vec, vec],
        out_shape=[jax.ShapeDtypeStruct((s, 2 * n), BF16), jax.ShapeDtypeStruct((1, 2 * n), F32),
                   jax.ShapeDtypeStruct((1, n), F32), jax.ShapeDtypeStruct((1, n), F32)],
        compiler_params=_params(("arbitrary",)),
        name="gmlp_act_bwd",
    )(pre, b_in, ln_w, du, dv)


def _gmlp_mix_fwd(u, v, w_s, b_st):
    s = u.shape[0]
    gd = GMLP_INNER // GMLP_GROUPS

    def body(u_ref, v_ref, w_ref, b_ref, o_ref):
        li = lax.broadcasted_iota(jnp.int32, (CHUNK, CHUNK), 0)
        si = lax.broadcasted_iota(jnp.int32, (CHUNK, CHUNK), 1)
        tril = li >= si
        for g in range(GMLP_GROUPS):
            sl = slice(g * gd, (g + 1) * gd)
            wm = jnp.where(tril, w_ref[g], 0.0).astype(BF16)
            mixed = _dot(wm, v_ref[:, sl], NN) + b_ref[:, g:g + 1]
            o_ref[:, sl] = (u_ref[:, sl].astype(F32) * mixed).astype(o_ref.dtype)

    row = pl.BlockSpec((CHUNK, GMLP_INNER), lambda c: (c, 0))
    return pl.pallas_call(
        body,
        grid=(s // CHUNK,),
        in_specs=[row, row, pl.BlockSpec((GMLP_GROUPS, CHUNK, CHUNK), lambda c: (0, 0, 0)),
                  pl.BlockSpec((CHUNK, LANES), lambda c: (0, 0))],
        out_specs=row,
        out_shape=jax.ShapeDtypeStruct((s, GMLP_INNER), BF16),
        compiler_params=_params(("parallel",)),
        name="gmlp_mix_fwd",
    )(u, v, w_s, b_st)


def _gmlp_mix_bwd(dgated, u, v, w_s, b_st):
    s = u.shape[0]
    nc = s // CHUNK
    gd = GMLP_INNER // GMLP_GROUPS

    def body(dg_ref, u_ref, v_ref, w_ref, b_ref, du_ref, dv_ref, dw_ref, db_ref):
        c = pl.program_id(0)

        @pl.when(c == 0)
        def _():
            dw_ref[...] = jnp.zeros_like(dw_ref)
            db_ref[...] = jnp.zeros_like(db_ref)

        li = lax.broadcasted_iota(jnp.int32, (CHUNK, CHUNK), 0)
        si = lax.broadcasted_iota(jnp.int32, (CHUNK, CHUNK), 1)
        tril = li >= si
        lane = lax.broadcasted_iota(jnp.int32, (CHUNK, LANES), 1)
        dbacc = jnp.zeros((CHUNK, LANES), F32)
        for g in range(GMLP_GROUPS):
            sl = slice(g * gd, (g + 1) * gd)
            wm = jnp.where(tril, w_ref[g], 0.0).astype(BF16)
            vg = v_ref[:, sl]
            mixed = _dot(wm, vg, NN) + b_ref[:, g:g + 1]
            dgv = dg_ref[:, sl].astype(F32)
            du_ref[:, sl] = (dgv * mixed).astype(du_ref.dtype)
            dm = dgv * u_ref[:, sl].astype(F32)
            dm_b = dm.astype(BF16)
            dv_ref[:, sl] = _dot(wm, dm_b, TN).astype(dv_ref.dtype)
            dw_ref[g] += jnp.where(tril, _dot(dm_b, vg, NT), 0.0)
            dbacc = dbacc + jnp.where(lane == g, jnp.sum(dm, axis=1, keepdims=True), 0.0)
        db_ref[...] += dbacc

    row = pl.BlockSpec((CHUNK, GMLP_INNER), lambda c: (c, 0))
    wspec = pl.BlockSpec((GMLP_GROUPS, CHUNK, CHUNK), lambda c: (0, 0, 0))
    bspec = pl.BlockSpec((CHUNK, LANES), lambda c: (0, 0))
    return pl.pallas_call(
        body,
        grid=(nc,),
        in_specs=[row, row, row, wspec, bspec],
        out_specs=[row, row, wspec, bspec],
        out_shape=[jax.ShapeDtypeStruct((s, GMLP_INNER), BF16), jax.ShapeDtypeStruct((s, GMLP_INNER), BF16),
                   jax.ShapeDtypeStruct((GMLP_GROUPS, CHUNK, CHUNK), F32), jax.ShapeDtypeStruct((CHUNK, LANES), F32)],
        compiler_params=_params(("arbitrary",)),
        name="gmlp_mix_bwd",
    )(dgated, u, v, w_s, b_st)


def _swiglu_fwd(gu):
    s = gu.shape[0]
    f = FFN_DIM
    tr = _pick(s, (512, 256, 128))

    def body(gu_ref, o_ref):
        gt = gu_ref[:, 0:f].astype(F32)
        o_ref[...] = (gt * _sigmoid(gt) * gu_ref[:, f:2 * f].astype(F32)).astype(o_ref.dtype)

    return pl.pallas_call(
        body,
        grid=(s // tr,),
        in_specs=[pl.BlockSpec((tr, 2 * f), lambda i: (i, 0))],
        out_specs=pl.BlockSpec((tr, f), lambda i: (i, 0)),
        out_shape=jax.ShapeDtypeStruct((s, f), BF16),
        compiler_params=_params(("parallel",)),
        name="swiglu_fwd",
    )(gu)


def _swiglu_bwd(gu, dhid):
    s = gu.shape[0]
    f = FFN_DIM
    tr = _pick(s, (512, 256, 128))

    def body(gu_ref, dh_ref, dgu_ref):
        gt = gu_ref[:, 0:f].astype(F32)
        up = gu_ref[:, f:2 * f].astype(F32)
        dh = dh_ref[...].astype(F32)
        sg = _sigmoid(gt)
        dgu_ref[:, 0:f] = (dh * up * (sg * (1.0 + gt * (1.0 - sg)))).astype(dgu_ref.dtype)
        dgu_ref[:, f:2 * f] = (dh * gt * sg).astype(dgu_ref.dtype)

    wide = pl.BlockSpec((tr, 2 * f), lambda i: (i, 0))
    return pl.pallas_call(
        body,
        grid=(s // tr,),
        in_specs=[wide, pl.BlockSpec((tr, f), lambda i: (i, 0))],
        out_specs=wide,
        out_shape=jax.ShapeDtypeStruct((s, 2 * f), BF16),
        compiler_params=_params(("parallel",)),
        name="swiglu_bwd",
    )(gu, dhid)


def _ple_fwd(pe, gl, h, ple_norm):
    s, d = h.shape
    tr = _pick(s, (512, 256, 128))

    def body(pe_ref, gl_ref, h_ref, w_ref, o_ref):
        pe_ = pe_ref[...]
        r = lax.rsqrt(jnp.mean(pe_ * pe_, axis=-1, keepdims=True) + RMS_EPS)
        o_ref[...] = h_ref[...] + _sigmoid(gl_ref[...]) * (pe_ * r * w_ref[...])

    row = pl.BlockSpec((tr, d), lambda i: (i, 0))
    return pl.pallas_call(
        body,
        grid=(s // tr,),
        in_specs=[row, row, row, pl.BlockSpec((1, d), lambda i: (0, 0))],
        out_specs=row,
        out_shape=jax.ShapeDtypeStruct((s, d), F32),
        compiler_params=_params(("parallel",)),
        name="ple_fwd",
    )(pe, gl, h, ple_norm)


def _ple_bwd(dh, pe, gl, ple_norm):
    s, d = dh.shape
    tr = _pick(s, (512, 256, 128))

    def body(dh_ref, pe_ref, gl_ref, w_ref, dgl_ref, dpe_ref, dw_ref):
        pe_ = pe_ref[...]
        dhv = dh_ref[...]
        r = lax.rsqrt(jnp.mean(pe_ * pe_, axis=-1, keepdims=True) + RMS_EPS)
        pn = pe_ * r
        gate = _sigmoid(gl_ref[...])
        dgl_ref[...] = (dhv * (pn * w_ref[...]) * gate * (1.0 - gate)).astype(dgl_ref.dtype)
        de = dhv * gate
        dxh = de * w_ref[...]
        dpe_ref[...] = (r * (dxh - pn * jnp.mean(dxh * pn, axis=-1, keepdims=True))).astype(dpe_ref.dtype)
        part = jnp.sum(de * pn, axis=0, keepdims=True)

        @pl.when(pl.program_id(0) == 0)
        def _():
            dw_ref[...] = part

        @pl.when(pl.program_id(0) > 0)
        def _():
            dw_ref[...] += part

    row = pl.BlockSpec((tr, d), lambda i: (i, 0))
    vec = pl.BlockSpec((1, d), lambda i: (0, 0))
    return pl.pallas_call(
        body,
        grid=(s // tr,),
        in_specs=[row, row, row, vec],
        out_specs=[row, row, vec],
        out_shape=[jax.ShapeDtypeStruct((s, d), BF16), jax.ShapeDtypeStruct((s, d), BF16),
                   jax.ShapeDtypeStruct((1, d), F32)],
        compiler_params=_params(("arbitrary",)),
        name="ple_bwd",
    )(dh, pe, gl, ple_norm)


def _loss_head(h, w, target):
    s, d = h.shape
    tr = _pick(s, (512, 256, 128))

    def body(h_ref, w_ref, t_ref, l_ref, dh_ref, dw_ref):
        hv = h_ref[...]
        r = lax.rsqrt(jnp.mean(hv * hv, axis=-1, keepdims=True) + RMS_EPS)
        hn = hv * r
        diff = hn * w_ref[...] - t_ref[...]
        lpart = jnp.zeros((8, LANES), F32) + (0.5 / d) * jnp.sum(jnp.sum(diff * diff, axis=1, keepdims=True), axis=0, keepdims=True)
        dy = diff * (1.0 / d)
        dxh = dy * w_ref[...]
        dh_ref[...] = r * (dxh - hn * jnp.mean(dxh * hn, axis=-1, keepdims=True))
        part = jnp.sum(dy * hn, axis=0, keepdims=True)

        @pl.when(pl.program_id(0) == 0)
        def _():
            l_ref[...] = lpart
            dw_ref[...] = part

        @pl.when(pl.program_id(0) > 0)
        def _():
            l_ref[...] += lpart
            dw_ref[...] += part

    row = pl.BlockSpec((tr, d), lambda i: (i, 0))
    vec = pl.BlockSpec((1, d), lambda i: (0, 0))
    return pl.pallas_call(
        body,
        grid=(s // tr,),
        in_specs=[row, vec, row],
        out_specs=[pl.BlockSpec((8, LANES), lambda i: (0, 0)), row, vec],
        out_shape=[jax.ShapeDtypeStruct((8, LANES), F32), jax.ShapeDtypeStruct((s, d), F32),
                   jax.ShapeDtypeStruct((1, d), F32)],
        compiler_params=_params(("arbitrary",)),
        name="loss_head",
    )(h, w, target)


PER_LAYER = ("norm_mix", "norm_ffn", "ffn_w_gu", "ffn_w_down", "ple_w_proj", "ple_norm", "ple_gate_norm", "ple_w_gate")


def _pad_lanes(v):
    return jnp.pad(v.astype(F32), (0, LANES - v.shape[0]))[None, :]


def _kernel_layouts(full):
    w = {}
    for k in ("norm_mix", "norm_ffn", "ple_norm", "ple_gate_norm", "ssd_conv_b", "ssd_norm_w", "gmlp_b_in", "gmlp_ln_w",
              "gmlp_ln_b", "gmlp_w_s"):
        w[k] = [full[k][i].astype(F32) for i in range(full[k].shape[0])]
    w["final_norm"] = full["final_norm"].astype(F32)
    n_ssd = full["ssd_w_out"].shape[0]
    if "ssd_w_in" in full:
        w["ssd_w_zx"] = [full["ssd_w_in"][j][:, :SSD_ZX].astype(BF16) for j in range(n_ssd)]
        w["ssd_w_dt"] = [jnp.pad(full["ssd_w_in"][j][:, SSD_ZX:].astype(BF16), ((0, 0), (0, LANES - SSD_HEADS)))
                         for j in range(n_ssd)]
        w["ffn_w_gu"] = [jnp.concatenate([full["ffn_w_gate"][i], full["ffn_w_up"][i]], axis=1).astype(BF16)
                         for i in range(DEPTH)]
    else:
        for k in ("ssd_w_zx", "ssd_w_dt", "ffn_w_gu"):
            w[k] = full[k]
    w["ssd_conv_w"] = [jnp.pad(full["ssd_conv_w"][j].astype(F32), ((0, 8 - CONV_K), (0, 0))) for j in range(n_ssd)]
    for k in ("ssd_dt_bias", "ssd_a_log", "ssd_d"):
        w[k] = [_pad_lanes(full[k][j]) for j in range(n_ssd)]
    w["ssd_w_out"] = [full["ssd_w_out"][j].astype(BF16) for j in range(n_ssd)]
    n_g = full["gmlp_w_in"].shape[0]
    w["gmlp_w_in"] = [full["gmlp_w_in"][j].astype(BF16) for j in range(n_g)]
    w["gmlp_w_out"] = [full["gmlp_w_out"][j].astype(BF16) for j in range(n_g)]
    w["gmlp_b_st"] = [jnp.pad(full["gmlp_b_s"][j].astype(F32).T, ((0, 0), (0, LANES - GMLP_GROUPS))) for j in range(n_g)]
    w["ffn_w_down"] =[full["ffn_w_down"][i].astype(BF16) for i in range(DEPTH)]
    w["ple_w_proj"] = [full["ple_w_proj"][i].astype(BF16) for i in range(DEPTH)]
    w["ple_w_gate"] = [full["ple_w_gate"][i].astype(BF16) for i in range(DEPTH)]
    return w


MATRICES = ("ssd_w_out", "gmlp_w_in", "gmlp_w_out", "ffn_w_down", "ple_w_proj", "ple_w_gate")


def _reference_layouts(g, wide=True):
    out = {}
    for k in ("norm_mix", "norm_ffn", "ple_norm", "ple_gate_norm", "ssd_conv_b", "ssd_norm_w", "gmlp_b_in", "gmlp_ln_w",
              "gmlp_ln_b", "gmlp_w_s", "ssd_conv_w", "ssd_dt_bias", "ssd_a_log", "ssd_d") + (MATRICES if wide else ()):
        out[k] = jnp.stack(g[k])
    out["final_norm"] = g["final_norm"]
    out["gmlp_b_s"] = jnp.stack([b[:, :GMLP_GROUPS].T for b in g["gmlp_b_st"]])
    if wide:
        out["ssd_w_in"] = jnp.stack([jnp.concatenate([zx, dt[:, :SSD_HEADS]], axis=1)
                                     for zx, dt in zip(g["ssd_w_zx"], g["ssd_w_dt"])])
        out["ffn_w_gate"] = jnp.stack([gu[:, :FFN_DIM] for gu in g["ffn_w_gu"]])
        out["ffn_w_up"] = jnp.stack([gu[:, FFN_DIM:] for gu in g["ffn_w_gu"]])
    return out


RELAYOUT_ROWS = 128
SSD_SHARD = SSD_IN_DIM // N_DEV
FFN_SHARD = FFN_DIM // N_DEV


def _to_bf16(x):
    nl, rows, n = x.shape

    def body(x_ref, o_ref):
        o_ref[...] = x_ref[...].astype(o_ref.dtype)

    blk = pl.BlockSpec((1, rows, n), lambda i: (i, 0, 0))
    return pl.pallas_call(
        body, grid=(nl,), in_specs=[blk], out_specs=blk, out_shape=jax.ShapeDtypeStruct(x.shape, BF16),
        compiler_params=_params(("parallel",)), name="to_bf16",
    )(x)


def _cat_ssd_in(gathered):
    _, nl, rows, n = gathered.shape
    tr = RELAYOUT_ROWS

    def body(g_ref, *o_refs):
        for j in range(nl):
            full = jnp.concatenate([g_ref[d, j] for d in range(N_DEV)], axis=1)
            o_refs[2 * j][...] = full[:, :SSD_ZX]
            o_refs[2 * j + 1][...] = jnp.concatenate(
                [full[:, SSD_ZX:], jnp.zeros((tr, LANES - SSD_HEADS), full.dtype)], axis=1)

    outs = pl.pallas_call(
        body, grid=(rows // tr,),
        in_specs=[pl.BlockSpec((N_DEV, nl, tr, n), lambda i: (0, 0, i, 0))],
        out_specs=[pl.BlockSpec((tr, SSD_ZX), lambda i: (i, 0)), pl.BlockSpec((tr, LANES), lambda i: (i, 0))] * nl,
        out_shape=[jax.ShapeDtypeStruct((rows, SSD_ZX), BF16), jax.ShapeDtypeStruct((rows, LANES), BF16)] * nl,
        compiler_params=_params(("parallel",)), name="cat_ssd_in",
    )(gathered)
    return [outs[2 * j] for j in range(nl)], [outs[2 * j + 1] for j in range(nl)]


def _split_ssd_in(dzx_list, ddt_list):
    nl = len(dzx_list)
    rows = dzx_list[0].shape[0]
    tr = RELAYOUT_ROWS

    def body(*refs):
        o_ref = refs[2 * nl]
        for j in range(nl):
            full = jnp.concatenate([refs[2 * j][...], refs[2 * j + 1][:, 0:SSD_HEADS]], axis=1)
            for d in range(N_DEV):
                o_ref[d, j] = full[:, d * SSD_SHARD:(d + 1) * SSD_SHARD].astype(o_ref.dtype)

    ins = []
    for j in range(nl):
        ins += [dzx_list[j], ddt_list[j]]
    return pl.pallas_call(
        body, grid=(rows // tr,),
        in_specs=[pl.BlockSpec((tr, SSD_ZX), lambda i: (i, 0)), pl.BlockSpec((tr, LANES), lambda i: (i, 0))] * nl,
        out_specs=pl.BlockSpec((N_DEV, nl, tr, SSD_SHARD), lambda i: (0, 0, i, 0)),
        out_shape=jax.ShapeDtypeStruct((N_DEV, nl, rows, SSD_SHARD), BF16),
        compiler_params=_params(("parallel",)), name="split_ssd_in",
    )(*ins)


def _cat_ffn(g_gate, g_up):
    _, nl, rows, n = g_gate.shape
    tr = RELAYOUT_ROWS

    def body(gg_ref, gu_ref, *o_refs):
        for i in range(nl):
            o_refs[i][...] = jnp.concatenate([gg_ref[d, i] for d in range(N_DEV)] + [gu_ref[d, i] for d in range(N_DEV)],
                                             axis=1)

    blk = pl.BlockSpec((N_DEV, nl, tr, n), lambda i: (0, 0, i, 0))
    outs = pl.pallas_call(
        body, grid=(rows // tr,), in_specs=[blk, blk],
        out_specs=[pl.BlockSpec((tr, 2 * FFN_DIM), lambda i: (i, 0))] * nl,
        out_shape=[jax.ShapeDtypeStruct((rows, 2 * FFN_DIM), BF16)] * nl,
        compiler_params=_params(("parallel",)), name="cat_ffn",
    )(g_gate, g_up)
    return list(outs)


def _split_ffn(dgu_list):
    nl = len(dgu_list)
    rows = dgu_list[0].shape[0]
    tr = RELAYOUT_ROWS

    def body(*refs):
        og_ref, ou_ref = refs[nl], refs[nl + 1]
        for i in range(nl):
            full = refs[i][...]
            for d in range(N_DEV):
                og_ref[d, i] = full[:, d * FFN_SHARD:(d + 1) * FFN_SHARD].astype(og_ref.dtype)
                ou_ref[d, i] = full[:, FFN_DIM + d * FFN_SHARD:FFN_DIM + (d + 1) * FFN_SHARD].astype(ou_ref.dtype)

    blk = pl.BlockSpec((N_DEV, nl, tr, FFN_SHARD), lambda i: (0, 0, i, 0))
    sds = jax.ShapeDtypeStruct((N_DEV, nl, rows, FFN_SHARD), BF16)
    return pl.pallas_call(
        body, grid=(rows // tr,),
        in_specs=[pl.BlockSpec((tr, 2 * FFN_DIM), lambda i: (i, 0))] * nl,
        out_specs=[blk, blk], out_shape=[sds, sds],
        compiler_params=_params(("parallel",)), name="split_ffn",
    )(*dgu_list)


def _local_step(x, p, target, w):
    saved = []
    h = x
    for i in range(DEPTH):
        j = i // 2
        sv = {"h0": h}
        hn = _rms_fwd(h, w["norm_mix"][i][None, :])
        sv["hn"] = hn
        if i % 2 == 0:
            zx = _mm(hn, w["ssd_w_zx"][j], "nn", BF16)
            dtp = _mm(hn, w["ssd_w_dt"][j], "nn", F32)
            xbc = _ssd_conv_fwd(zx, w["ssd_conv_w"][j], w["ssd_conv_b"][j][None, :])
            y, prev = _ssd_scan_fwd(xbc, dtp, w["ssd_dt_bias"][j], w["ssd_a_log"][j], w["ssd_d"][j])
            gn = _ssd_gate_fwd(y, zx, w["ssd_norm_w"][j][None, :])
            h = _mm(gn, w["ssd_w_out"][j], "nn", F32, add=h)
            sv.update(zx=zx, dtp=dtp, xbc=xbc, y=y, prev=prev, gn=gn)
        else:
            pre = _mm(hn, w["gmlp_w_in"][j], "nn", BF16)
            u, v = _gmlp_act_fwd(pre, w["gmlp_b_in"][j][None, :], w["gmlp_ln_w"][j][None, :], w["gmlp_ln_b"][j][None, :])
            gated = _gmlp_mix_fwd(u, v, w["gmlp_w_s"][j], w["gmlp_b_st"][j])
            h = _mm(gated, w["gmlp_w_out"][j], "nn", F32, add=h)
            sv.update(pre=pre, u=u, v=v, gated=gated)
        sv["h1"] = h
        un = _rms_fwd(h, w["norm_ffn"][i][None, :])
        gu = _mm(un, w["ffn_w_gu"][i], "nn", BF16)
        hid = _swiglu_fwd(gu)
        h = _mm(hid, w["ffn_w_down"][i], "nn", F32, add=h)
        sv.update(un=un, gu=gu, hid=hid, h2=h)
        pe = _mm(p[i], w["ple_w_proj"][i], "nn", F32)
        hg = _rms_fwd(h, w["ple_gate_norm"][i][None, :])
        gl = _mm(hg, w["ple_w_gate"][i], "nn", F32)
        h = _ple_fwd(pe, gl, h, w["ple_norm"][i][None, :])
        sv.update(pe=pe, hg=hg, gl=gl)
        saved.append(sv)

    lpart, dh, d_final = _loss_head(h, w["final_norm"][None, :], target)
    g = {k: [None] * (DEPTH if k in PER_LAYER else DEPTH // 2) for k in w if k != "final_norm"}
    g["final_norm"] = d_final[0]

    for i in reversed(range(DEPTH)):
        j = i // 2
        sv = saved[i]
        dgl, dpe, d_ple_norm = _ple_bwd(dh, sv["pe"], sv["gl"], w["ple_norm"][i][None, :])
        g["ple_norm"][i] = d_ple_norm[0]
        g["ple_w_gate"][i] = _mm(sv["hg"], dgl, "tn", F32)
        g["ple_w_proj"][i] = _mm(p[i], dpe, "tn", F32)
        dhg = _mm(dgl, w["ple_w_gate"][i], "nt", BF16)
        dh, d_gate_norm = _rms_bwd(dhg, sv["h2"], w["ple_gate_norm"][i][None, :], dh)
        g["ple_gate_norm"][i] = d_gate_norm[0]
        dhid = _mm(dh, w["ffn_w_down"][i], "nt", BF16)
        g["ffn_w_down"][i] = _mm(sv["hid"], dh, "tn", F32)
        dgu = _swiglu_bwd(sv["gu"], dhid)
        g["ffn_w_gu"][i] = _mm(sv["un"], dgu, "tn", F32)
        dun = _mm(dgu, w["ffn_w_gu"][i], "nt", BF16)
        dh, d_norm_ffn = _rms_bwd(dun, sv["h1"], w["norm_ffn"][i][None, :], dh)
        g["norm_ffn"][i] = d_norm_ffn[0]
        if i % 2 == 0:
            dgn = _mm(dh, w["ssd_w_out"][j], "nt", BF16)
            g["ssd_w_out"][j] = _mm(sv["gn"], dh, "tn", F32)
            dy, dzx, d_norm_w = _ssd_gate_bwd(dgn, sv["y"], sv["zx"], w["ssd_norm_w"][j][None, :])
            g["ssd_norm_w"][j] = d_norm_w[0]
            dxbc, ddtp, d_bias, d_alog, d_d = _ssd_scan_bwd(sv["xbc"], sv["dtp"], sv["prev"], dy, w["ssd_dt_bias"][j],
                                                            w["ssd_a_log"][j], w["ssd_d"][j])
            g["ssd_dt_bias"][j] = d_bias[0, :SSD_HEADS]
            g["ssd_a_log"][j] = d_alog[0, :SSD_HEADS]
            g["ssd_d"][j] = d_d[0, :SSD_HEADS]
            dzx, d_conv_w, d_conv_b = _ssd_conv_bwd(sv["zx"], dxbc, w["ssd_conv_w"][j], w["ssd_conv_b"][j][None, :], dzx)
            g["ssd_conv_w"][j] = d_conv_w[:CONV_K]
            g["ssd_conv_b"][j] = d_conv_b[0]
            g["ssd_w_zx"][j] = _mm(sv["hn"], dzx, "tn", F32)
            g["ssd_w_dt"][j] = _mm(sv["hn"], ddtp, "tn", F32)
            dhn = _mm(ddtp, w["ssd_w_dt"][j], "nt", F32)
            dhn = _mm(dzx, w["ssd_w_zx"][j], "nt", BF16, add=dhn)
        else:
            dgated = _mm(dh, w["gmlp_w_out"][j], "nt", BF16)
            g["gmlp_w_out"][j] = _mm(sv["gated"], dh, "tn", F32)
            du, dv, d_ws, d_bst = _gmlp_mix_bwd(dgated, sv["u"], sv["v"], w["gmlp_w_s"][j], w["gmlp_b_st"][j])
            g["gmlp_w_s"][j] = d_ws
            g["gmlp_b_st"][j] = d_bst
            dpre, d_bin, d_lnw, d_lnb = _gmlp_act_bwd(sv["pre"], w["gmlp_b_in"][j][None, :], w["gmlp_ln_w"][j][None, :],
                                                     du, dv)
            g["gmlp_b_in"][j] = d_bin[0]
            g["gmlp_ln_w"][j] = d_lnw[0]
            g["gmlp_ln_b"][j] = d_lnb[0]
            g["gmlp_w_in"][j] = _mm(sv["hn"], dpre, "tn", F32)
            dhn = _mm(dpre, w["gmlp_w_in"][j], "nt", BF16)
        dh, d_norm_mix = _rms_bwd(dhn, sv["h0"], w["norm_mix"][i][None, :], dh)
        g["norm_mix"][i] = d_norm_mix[0]
    return lpart[0, 0], dh, g


PACK_COLS = 1024
ANY = pl.BlockSpec(memory_space=pl.ANY)


def _mesh_pos():
    return lax.axis_index("x"), lax.axis_index("y"), lax.axis_index("c")


def _all_gather(xs_list, name):
    n = len(xs_list)

    def body(*refs):
        x_refs, out_refs = refs[:n], refs[n:2 * n]
        send_sems, recv_sems, local_sems = refs[2 * n:]
        x, y, c = _mesh_pos()
        me, sibling = (x, y, c), (x, y, 1 - c)
        chips = [(1 - x, y), (x, 1 - y), (1 - x, 1 - y)]

        def copy(a, k, block, to, from_input=False):
            px, py, pc = block
            dst = out_refs[a].at[4 * px + 2 * py + pc]
            return pltpu.make_async_remote_copy(
                src_ref=x_refs[a] if from_input else dst, dst_ref=dst,
                send_sem=send_sems.at[7 * a + k], recv_sem=recv_sems.at[7 * a + k], device_id=to,
                device_id_type=MESH_ID)

        mine = [pltpu.make_async_copy(x_refs[a], out_refs[a].at[4 * x + 2 * y + c], local_sems.at[a]) for a in range(n)]
        for cp in mine:
            cp.start()
        first = []
        for a in range(n):
            first += [copy(a, 1 + j, me, (*chip, c), from_input=True) for j, chip in enumerate(chips)]
            first.append(copy(a, 0, me, sibling, from_input=True))
        for cp in first:
            cp.start()
        passed = []
        for a in range(n):
            for j, chip in enumerate(chips):
                copy(a, 1 + j, (*chip, c), me).wait_recv()
                fwd = copy(a, 4 + j, (*chip, c), sibling)
                fwd.start()
                passed.append(fwd)
        for a in range(n):
            copy(a, 0, sibling, me).wait_recv()
            for j, chip in enumerate(chips):
                copy(a, 4 + j, (*chip, 1 - c), me).wait_recv()
        for cp in first + passed:
            cp.wait_send()
        for cp in mine:
            cp.wait()

    outs = pl.pallas_call(
        body,
        out_shape=[jax.ShapeDtypeStruct((N_DEV,) + t.shape, t.dtype) for t in xs_list],
        in_specs=[ANY] * n,
        out_specs=[ANY] * n,
        scratch_shapes=[pltpu.SemaphoreType.DMA((7 * n,)), pltpu.SemaphoreType.DMA((7 * n,)),
                        pltpu.SemaphoreType.DMA((n,))],
        name=name,
    )(*xs_list)
    return list(outs)


def _exchange_sibling(send_list):
    n = len(send_list)

    def body(*refs):
        s_refs, land_refs = refs[:n], refs[n:2 * n]
        send_sems, recv_sems = refs[2 * n:]
        x, y, c = _mesh_pos()
        cps = [pltpu.make_async_remote_copy(src_ref=s_refs[a], dst_ref=land_refs[a], send_sem=send_sems.at[a],
                                            recv_sem=recv_sems.at[a], device_id=(x, y, 1 - c), device_id_type=MESH_ID)
               for a in range(n)]
        for cp in cps:
            cp.start()
        for cp in cps:
            cp.wait()

    outs = pl.pallas_call(
        body,
        out_shape=[jax.ShapeDtypeStruct(t.shape, t.dtype) for t in send_list],
        in_specs=[ANY] * n,
        out_specs=[ANY] * n,
        scratch_shapes=[pltpu.SemaphoreType.DMA((n,)), pltpu.SemaphoreType.DMA((n,))],
        name="rs_exchange_sibling",
    )(*send_list)
    return list(outs)


def _exchange_chips(partial_list):
    n = len(partial_list)

    def body(*refs):
        p_refs, land_refs = refs[:n], refs[n:2 * n]
        send_sems, recv_sems = refs[2 * n:]
        x, y, c = _mesh_pos()
        chips = [(1 - x, y), (x, 1 - y), (1 - x, 1 - y)]
        cps = [pltpu.make_async_remote_copy(src_ref=p_refs[a].at[2 * cx + cy], dst_ref=land_refs[a].at[j],
                                            send_sem=send_sems.at[3 * a + j], recv_sem=recv_sems.at[3 * a + j],
                                            device_id=(cx, cy, c), device_id_type=MESH_ID)
               for a in range(n) for j, (cx, cy) in enumerate(chips)]
        for cp in cps:
            cp.start()
        for cp in cps:
            cp.wait()

    outs = pl.pallas_call(
        body,
        out_shape=[jax.ShapeDtypeStruct((3,) + t.shape[1:], t.dtype) for t in partial_list],
        in_specs=[ANY] * n,
        out_specs=[ANY] * n,
        scratch_shapes=[pltpu.SemaphoreType.DMA((3 * n,)), pltpu.SemaphoreType.DMA((3 * n,))],
        name="rs_exchange_chips",
    )(*partial_list)
    return list(outs)


def _sum_pairs(a, b):
    shape = a.shape
    a = a.reshape(shape[0], -1, shape[-1])
    b = b.reshape(a.shape)
    n, r, cdim = a.shape
    tr = _pick(r, (1024, 512, 256, 128, 64))

    def body(a_ref, b_ref, o_ref):
        o_ref[...] = (a_ref[...].astype(F32) + b_ref[...].astype(F32)).astype(o_ref.dtype)

    blk = pl.BlockSpec((1, tr, cdim), lambda i, j: (i, j, 0))
    return pl.pallas_call(
        body, grid=(n, r // tr), in_specs=[blk, blk], out_specs=blk,
        out_shape=jax.ShapeDtypeStruct(a.shape, a.dtype),
        compiler_params=_params(("parallel", "parallel")), name="rs_sum_pairs",
    )(a, b).reshape(shape)


def _sum_final(own, land):
    shape = own.shape
    own = own.reshape(-1, shape[-1])
    land = land.reshape((3,) + own.shape)
    r, cdim = own.shape
    tr = _pick(r, (1024, 512, 256, 128, 64))

    def body(o_ref, l_ref, out_ref):
        acc = o_ref[...].astype(F32)
        for j in range(3):
            acc = acc + l_ref[j].astype(F32)
        out_ref[...] = acc

    return pl.pallas_call(
        body, grid=(r // tr,),
        in_specs=[pl.BlockSpec((tr, cdim), lambda i: (i, 0)), pl.BlockSpec((3, tr, cdim), lambda i: (0, i, 0))],
        out_specs=pl.BlockSpec((tr, cdim), lambda i: (i, 0)),
        out_shape=jax.ShapeDtypeStruct((r, cdim), F32),
        compiler_params=_params(("parallel",)), name="rs_sum_final",
    )(own, land).reshape(shape)


def _sum_devices(gathered):
    n, r, cdim = gathered.shape
    tr = _pick(r, (64, 32, 16, 8))

    def body(g_ref, out_ref):
        acc = g_ref[0]
        for q in range(1, n):
            acc = acc + g_ref[q]
        out_ref[...] = acc

    return pl.pallas_call(
        body, grid=(r // tr,),
        in_specs=[pl.BlockSpec((n, tr, cdim), lambda i: (0, i, 0))],
        out_specs=pl.BlockSpec((tr, cdim), lambda i: (i, 0)),
        out_shape=jax.ShapeDtypeStruct((r, cdim), F32),
        compiler_params=_params(("parallel",)), name="sum_devices",
    )(gathered)


def _adamw(w, g, m, v):
    shape = w.shape
    cols = shape[-1]
    rows = w.size // cols
    tr = _pick(rows, (512, 256, 128, 64, 32, 16, 8))
    c1 = 1.0 - ADAM_B1 ** ADAM_STEP
    c2 = 1.0 - ADAM_B2 ** ADAM_STEP

    def body(w_ref, g_ref, m_ref, v_ref, d_ref, nm_ref, nv_ref):
        gv = g_ref[...]
        m2 = ADAM_B1 * m_ref[...] + (1.0 - ADAM_B1) * gv
        v2 = ADAM_B2 * v_ref[...] + (1.0 - ADAM_B2) * (gv * gv)
        d_ref[...] = -ADAM_LR * ((m2 / c1) / (jnp.sqrt(v2 / c2) + ADAM_EPS) + ADAM_WD * w_ref[...])
        nm_ref[...] = m2
        nv_ref[...] = v2

    blk = pl.BlockSpec((tr, cols), lambda i: (i, 0))
    sds = jax.ShapeDtypeStruct((rows, cols), F32)
    outs = pl.pallas_call(
        body, grid=(rows // tr,), in_specs=[blk] * 4, out_specs=[blk] * 3, out_shape=[sds] * 3,
        compiler_params=_params(("parallel",)), name=f"adamw_{rows}x{cols}",
    )(*(t.reshape(rows, cols) for t in (w, g, m, v)))
    return tuple(o.reshape(shape) for o in outs)


WEIGHTS = ("norm_mix", "norm_ffn", "ssd_w_in", "ssd_conv_w", "ssd_conv_b", "ssd_dt_bias", "ssd_a_log", "ssd_d",
           "ssd_norm_w", "ssd_w_out", "gmlp_w_in", "gmlp_b_in", "gmlp_ln_w", "gmlp_ln_b", "gmlp_w_s", "gmlp_b_s",
           "gmlp_w_out", "ffn_w_gate", "ffn_w_up", "ffn_w_down", "ple_w_proj", "ple_norm", "ple_gate_norm",
           "ple_w_gate", "final_norm")
ARG_NAMES = ("x", "p") + WEIGHTS + ("loss_target",) + tuple("m_" + n for n in WEIGHTS) + tuple("v_" + n for n in WEIGHTS)
SHARD_AXIS = {"ssd_w_in": 2, "ssd_conv_w": 2, "ssd_w_out": 1, "gmlp_w_in": 2, "gmlp_b_in": 1, "gmlp_ln_w": 1,
              "gmlp_ln_b": 1, "gmlp_w_out": 1, "ffn_w_gate": 2, "ffn_w_up": 2, "ffn_w_down": 1, "ple_w_proj": 2,
              "ple_w_gate": 1}
GATHER_BF16 = ("ssd_w_in", "ssd_w_out", "gmlp_w_in", "gmlp_w_out", "ffn_w_gate", "ffn_w_up", "ffn_w_down",
               "ple_w_proj", "ple_w_gate")
GATHER_F32 = ("ssd_conv_w", "gmlp_b_in", "gmlp_ln_w", "gmlp_ln_b")
SHARDED = GATHER_BF16 + GATHER_F32
WIDE = ("ssd_w_in", "ffn_w_gate", "ffn_w_up")
REPLICATED = tuple(n for n in WEIGHTS if n not in SHARD_AXIS)


def _pack(arrs, dtype, row_mult, lead=0):
    flat = jnp.concatenate([t.reshape(t.shape[:lead] + (-1,)).astype(dtype) for t in arrs], axis=lead)
    n = flat.shape[-1]
    unit = row_mult * PACK_COLS
    total = -(-n // unit) * unit
    flat = jnp.pad(flat, [(0, 0)] * lead + [(0, total - n)])
    return flat.reshape(flat.shape[:lead] + (total // PACK_COLS, PACK_COLS))


def _unpack(buf, names, shapes, lead=0):
    flat = buf.reshape(buf.shape[:lead] + (-1,))
    out, off = {}, 0
    for n in names:
        size = math.prod(shapes[n])
        out[n] = lax.slice_in_dim(flat, off, off + size, axis=lead).reshape(buf.shape[:lead] + tuple(shapes[n]))
        off += size
    return out


ROW_PACKED = ((1024, ("ssd_w_out", "gmlp_w_out", "ffn_w_down", "ple_w_gate")), (512, ("gmlp_w_in",)),
              (128, ("ple_w_proj",)))
ROW_PACK_MULT = 1024


def _pack_rows(arrs, width, lead=0):
    parts = [t.reshape(t.shape[:lead] + (-1, width)).astype(BF16) for t in arrs]
    rows = sum(t.shape[lead] for t in parts)
    pad = -rows % ROW_PACK_MULT
    if pad:
        parts.append(jnp.zeros(parts[0].shape[:lead] + (pad, width), BF16))
    return jnp.concatenate(parts, axis=lead)


def _unpack_rows(buf, names, shapes, lead=0):
    width = buf.shape[-1]
    out, off = {}, 0
    for n in names:
        rows = math.prod(shapes[n]) // width
        out[n] = lax.slice_in_dim(buf, off, off + rows, axis=lead).reshape(buf.shape[:lead] + tuple(shapes[n]))
        off += rows
    return out


def _merge_shards(seg, ax):
    t = jnp.moveaxis(seg, 0, ax)
    return t.reshape(t.shape[:ax] + (t.shape[ax] * t.shape[ax + 1],) + t.shape[ax + 2:])


def _split_for_cores(gfull, ax, c):
    shp = gfull.shape
    t = gfull.reshape(shp[:ax] + (2, 2, 2, shp[ax] // N_DEV) + shp[ax + 1:])

    def take(core):
        u = lax.dynamic_index_in_dim(t, core, axis=ax + 2, keepdims=False)
        u = jnp.moveaxis(u, (ax, ax + 1), (0, 1))
        return u.reshape((4,) + u.shape[2:])

    return take(c), take(1 - c)


def kernel(x, p, norm_mix, norm_ffn, ssd_w_in, ssd_conv_w, ssd_conv_b, ssd_dt_bias, ssd_a_log, ssd_d,
           ssd_norm_w, ssd_w_out, gmlp_w_in, gmlp_b_in, gmlp_ln_w, gmlp_ln_b, gmlp_w_s, gmlp_b_s,
           gmlp_w_out, ffn_w_gate, ffn_w_up, ffn_w_down, ple_w_proj, ple_norm, ple_gate_norm, ple_w_gate,
           final_norm, loss_target, m_norm_mix, m_norm_ffn, m_ssd_w_in, m_ssd_conv_w, m_ssd_conv_b,
           m_ssd_dt_bias, m_ssd_a_log, m_ssd_d, m_ssd_norm_w, m_ssd_w_out, m_gmlp_w_in, m_gmlp_b_in,
           m_gmlp_ln_w, m_gmlp_ln_b, m_gmlp_w_s, m_gmlp_b_s, m_gmlp_w_out, m_ffn_w_gate, m_ffn_w_up,
           m_ffn_w_down, m_ple_w_proj, m_ple_norm, m_ple_gate_norm, m_ple_w_gate, m_final_norm, v_norm_mix,
           v_norm_ffn, v_ssd_w_in, v_ssd_conv_w, v_ssd_conv_b, v_ssd_dt_bias, v_ssd_a_log, v_ssd_d,
           v_ssd_norm_w, v_ssd_w_out, v_gmlp_w_in, v_gmlp_b_in, v_gmlp_ln_w, v_gmlp_ln_b, v_gmlp_w_s,
           v_gmlp_b_s, v_gmlp_w_out, v_ffn_w_gate, v_ffn_w_up, v_ffn_w_down, v_ple_w_proj, v_ple_norm,
           v_ple_gate_norm, v_ple_w_gate, v_final_norm):
    given = locals()
    a = {n: given[n] for n in ARG_NAMES}
    mx, my, c = _mesh_pos()
    xs = a["x"][0]
    ps = a["p"][:, 0]
    target = a["loss_target"][0]
    shard_shapes = {n: a[n].shape for n in WEIGHTS}

    full = {n: a[n] for n in REPLICATED}
    row_packs = [_pack_rows([a[n] for n in names], wd) for wd, names in ROW_PACKED]
    got = _all_gather(row_packs + [_pack([a[n] for n in GATHER_F32], F32, 8)] + [_to_bf16(a[n]) for n in WIDE],
                      "ag_weights")
    for (wd, names), buf in zip(ROW_PACKED, got):
        for n, seg in _unpack_rows(buf, names, shard_shapes, lead=1).items():
            full[n] = _merge_shards(seg, SHARD_AXIS[n])
    k0 = len(ROW_PACKED)
    for n, seg in _unpack(got[k0], GATHER_F32, shard_shapes, lead=1).items():
        full[n] = _merge_shards(seg, SHARD_AXIS[n])
    full["ssd_w_zx"], full["ssd_w_dt"] = _cat_ssd_in(got[k0 + 1])
    full["ffn_w_gu"] = _cat_ffn(got[k0 + 2], got[k0 + 3])

    lpart, dx, g = _local_step(xs, ps, target, _kernel_layouts(full))
    gfull = _reference_layouts(g, wide=False)
    loss = lax.psum(lpart, ("x", "y", "c"))

    def by_core(t):
        u = t.reshape((4, 2) + t.shape[1:])
        return (lax.dynamic_index_in_dim(u, c, axis=1, keepdims=False),
                lax.dynamic_index_in_dim(u, 1 - c, axis=1, keepdims=False))

    def layer_halves(gl, ax):
        if ax == 0:
            t = gl.reshape(4, 2, -1, gl.shape[-1])
            return tuple(lax.dynamic_index_in_dim(t, cc, axis=1, keepdims=False) for cc in (c, 1 - c))
        t = gl.reshape(gl.shape[0], 4, 2, -1)
        return tuple(jnp.moveaxis(lax.dynamic_index_in_dim(t, cc, axis=2, keepdims=False), 1, 0) for cc in (c, 1 - c))

    pairs = []
    for wd, names in ROW_PACKED:
        hs = [layer_halves(gl, SHARD_AXIS[n] - 1) for n in names for gl in g[n]]
        pairs.append(tuple(_pack_rows([h[i] for h in hs], wd, lead=1) for i in (0, 1)))
    halves = [_split_for_cores(gfull[n], SHARD_AXIS[n], c) for n in GATHER_F32]
    pairs.append((_pack([h[0] for h in halves], BF16, 16, lead=1), _pack([h[1] for h in halves], BF16, 16, lead=1)))
    pairs += [by_core(t) for t in (_split_ssd_in(g["ssd_w_zx"], g["ssd_w_dt"]),) + tuple(_split_ffn(g["ffn_w_gu"]))]
    landed = _exchange_sibling([s for _, s in pairs])
    partials = [_sum_pairs(k, l) for (k, _), l in zip(pairs, landed)]
    landed = _exchange_chips(partials)
    sums = [_sum_final(lax.dynamic_index_in_dim(t, 2 * mx + my, axis=0, keepdims=False), l)
            for t, l in zip(partials, landed)]
    gshard = {}
    for (wd, names), buf in zip(ROW_PACKED, sums):
        gshard.update(_unpack_rows(buf, names, shard_shapes))
    gshard.update(_unpack(sums[k0], GATHER_F32, shard_shapes))
    gshard.update(zip(WIDE, sums[k0 + 1:]))
    rep = _all_gather([_pack([gfull[n] for n in REPLICATED], F32, 64)], "ag_replicated_grads")[0]
    grep = _unpack(_sum_devices(rep), REPLICATED, shard_shapes)
    grads = {**gshard, **grep}

    upd = {n: _adamw(a[n], grads[n], a["m_" + n], a["v_" + n]) for n in WEIGHTS}
    return (loss, dx[None], *[grads[n] for n in WEIGHTS], *[upd[n][0] for n in WEIGHTS],
            *[upd[n][1] for n in WEIGHTS], *[upd[n][2] for n in WEIGHTS])
```

```python
import functools
import math

import jax
import jax.numpy as jnp
from jax import lax
from jax.experimental import pallas as pl
from jax.experimental.pallas import tpu as pltpu

F32 = jnp.float32
BF16 = jnp.bfloat16

N_DEV = 8
D_MODEL = 1024
DEPTH = 4
SSD_INNER = 2048
SSD_HEADS = 32
SSD_HEADDIM = 64
SSD_GROUPS = 8
SSD_STATE = 128
SSD_GROUP_W = SSD_INNER // SSD_GROUPS
SSD_CONV_DIM = SSD_INNER + 2 * SSD_GROUPS * SSD_STATE
SSD_IN_DIM = 2 * SSD_INNER + SSD_CONV_DIM - SSD_INNER + SSD_HEADS
SSD_ZX = SSD_INNER + SSD_CONV_DIM
CONV_K = 4
CHUNK = 128
GMLP_INNER = 2048
GMLP_GROUPS = 16
FFN_DIM = 2816
PLE_DIM = 256
RMS_EPS = 1e-6
LN_EPS = 1e-5
LANES = 128
VMEM_LIMIT = 56 * 1024 * 1024

ADAM_LR = 0.001
ADAM_B1 = 0.9
ADAM_B2 = 0.999
ADAM_EPS = 1e-08
ADAM_WD = 0.01
ADAM_STEP = 10

MESH_ID = pl.DeviceIdType.MESH


def _pick(n, cands):
    for c in cands:
        if c <= n and n % c == 0:
            return c
    return n


def _params(dims):
    return pltpu.CompilerParams(dimension_semantics=dims, vmem_limit_bytes=VMEM_LIMIT)


def _dot(a, b, dims=(((1,), (0,)), ((), ())), precision=None):
    return lax.dot_general(a, b, dims, precision=precision, preferred_element_type=F32)


NN = (((1,), (0,)), ((), ()))
NT = (((1,), (1,)), ((), ()))
TN = (((0,), (0,)), ((), ()))


def _sigmoid(x):
    return 1.0 / (1.0 + jnp.exp(-x))


def _dot01(a, b, dims, split, terms=3):
    v = (a, b)[split]
    ones = (a, b)[1 - split].astype(BF16)
    acc = None
    for _ in range(terms):
        piece = v.astype(BF16)
        v = v - piece.astype(F32)
        part = _dot(piece, ones, dims) if split == 0 else _dot(ones, piece, dims)
        acc = part if acc is None else acc + part
    return acc


MM_VMEM_BUDGET = 36 * 1024 * 1024


def _mm_tiles(mode, m, n, k, a_bytes, b_bytes, out_bytes, has_add):
    tm = _pick(m, (1408, 1024, 512, 256, 128))
    tn_cands = [c for c in (2816, 1024, 512, 256, 128) if c <= n and n % c == 0] or [n]
    tk_cands = [k] + [c for c in (2816, 2048, 1024, 512, 256, 128) if c < k and k % c == 0]
    for tk in tk_cands:
        for tn in tn_cands:
            blocks = tm * tk * a_bytes + tk * tn * b_bytes + tm * tn * (out_bytes + (4 if has_add else 0))
            if 2 * blocks + (tm * tn * 4 if tk < k else 0) <= MM_VMEM_BUDGET:
                return tm, tn, tk
    return tm, tn_cands[-1], tk_cands[-1]


def _mm(a, b, mode, out_dtype, add=None):
    if mode == "nn":
        m, k = a.shape
        n = b.shape[1]
    elif mode == "nt":
        m, k = a.shape
        n = b.shape[0]
    else:
        k, m = a.shape
        n = b.shape[1]
    tm, tn, tk = _mm_tiles(mode, m, n, k, a.dtype.itemsize, b.dtype.itemsize, jnp.dtype(out_dtype).itemsize,
                           add is not None)
    nk = k // tk
    dims = {"nn": NN, "nt": NT, "tn": TN}[mode]

    def body(*refs):
        if add is None:
            a_ref, b_ref, o_ref = refs[:3]
            add_ref = None
            rest = refs[3:]
        else:
            a_ref, b_ref, add_ref, o_ref = refs[:4]
            rest = refs[4:]
        part = _dot(a_ref[...].astype(BF16), b_ref[...].astype(BF16), dims)

        def finish(acc):
            if add_ref is not None:
                acc = acc + add_ref[...]
            o_ref[...] = acc.astype(o_ref.dtype)

        if nk == 1:
            finish(part)
        else:
            acc_ref = rest[0]
            kk = pl.program_id(2)

            @pl.when(kk == 0)
            def _():
                acc_ref[...] = part

            @pl.when(kk > 0)
            def _():
                acc_ref[...] += part

            @pl.when(kk == nk - 1)
            def _():
                finish(acc_ref[...])

    if mode == "nn":
        a_spec = pl.BlockSpec((tm, tk), lambda i, j, kk: (i, kk))
        b_spec = pl.BlockSpec((tk, tn), lambda i, j, kk: (kk, j))
    elif mode == "nt":
        a_spec = pl.BlockSpec((tm, tk), lambda i, j, kk: (i, kk))
        b_spec = pl.BlockSpec((tn, tk), lambda i, j, kk: (j, kk))
    else:
        a_spec = pl.BlockSpec((tk, tm), lambda i, j, kk: (kk, i))
        b_spec = pl.BlockSpec((tk, tn), lambda i, j, kk: (kk, j))
    o_spec = pl.BlockSpec((tm, tn), lambda i, j, kk: (i, j))
    in_specs = [a_spec, b_spec] + ([o_spec] if add is not None else [])
    args = (a, b) + ((add,) if add is not None else ())
    return pl.pallas_call(
        body,
        grid=(m // tm, n // tn, nk),
        in_specs=in_specs,
        out_specs=o_spec,
        out_shape=jax.ShapeDtypeStruct((m, n), out_dtype),
        scratch_shapes=[pltpu.VMEM((tm, tn), F32)] if nk > 1 else [],
        compiler_params=_params(("parallel", "parallel", "arbitrary")),
        name=f"mm_{mode}_{m}x{k}x{n}",
    )(*args)


def _rms_fwd(x, w):
    s, d = x.shape
    tr = _pick(s, (512, 256, 128))

    def body(x_ref, w_ref, o_ref):
        xv = x_ref[...]
        r = lax.rsqrt(jnp.mean(xv * xv, axis=-1, keepdims=True) + RMS_EPS)
        o_ref[...] = (xv * r * w_ref[...]).astype(o_ref.dtype)

    return pl.pallas_call(
        body,
        grid=(s // tr,),
        in_specs=[pl.BlockSpec((tr, d), lambda i: (i, 0)), pl.BlockSpec((1, d), lambda i: (0, 0))],
        out_specs=pl.BlockSpec((tr, d), lambda i: (i, 0)),
        out_shape=jax.ShapeDtypeStruct((s, d), BF16),
        compiler_params=_params(("parallel",)),
        name="rms_fwd",
    )(x, w)


def _rms_bwd(dyn, x, w, add):
    s, d = x.shape
    tr = _pick(s, (512, 256, 128))

    def body(dy_ref, x_ref, w_ref, add_ref, dx_ref, dw_ref):
        xv = x_ref[...]
        dy = dy_ref[...].astype(F32)
        r = lax.rsqrt(jnp.mean(xv * xv, axis=-1, keepdims=True) + RMS_EPS)
        xn = xv * r
        dxh = dy * w_ref[...]
        dx = r * (dxh - xn * jnp.mean(dxh * xn, axis=-1, keepdims=True))
        dx_ref[...] = add_ref[...] + dx
        part = jnp.sum(dy * xn, axis=0, keepdims=True)

        @pl.when(pl.program_id(0) == 0)
        def _():
            dw_ref[...] = part

        @pl.when(pl.program_id(0) > 0)
        def _():
            dw_ref[...] += part

    row = pl.BlockSpec((tr, d), lambda i: (i, 0))
    vec = pl.BlockSpec((1, d), lambda i: (0, 0))
    return pl.pallas_call(
        body,
        grid=(s // tr,),
        in_specs=[row, row, vec, row],
        out_specs=[row, vec],
        out_shape=[jax.ShapeDtypeStruct((s, d), F32), jax.ShapeDtypeStruct((1, d), F32)],
        compiler_params=_params(("arbitrary",)),
        name="rms_bwd",
    )(dyn, x, w, add)


CONV_ROWS = 256
CONV_COLS = 256
CONV_HALO = 16


def _conv_taps(ext, w, base, rows):
    acc = w[0:1, :] * ext[base:base + rows]
    for k in range(1, CONV_K):
        acc = acc + w[k:k + 1, :] * ext[base + k:base + k + rows]
    return acc


def _ssd_conv_fwd(zx, conv_w, conv_b):
    s = zx.shape[0]
    c = SSD_CONV_DIM
    nsteps = s // CONV_ROWS
    off = SSD_INNER // CONV_COLS

    def body(x_ref, w_ref, b_ref, o_ref):
        w = w_ref[...]
        b = b_ref[...]

        def step(i, carry):
            r0 = pl.multiple_of(i * CONV_ROWS, CONV_ROWS)
            cur = x_ref[pl.ds(r0, CONV_ROWS), :].astype(F32)
            p0 = pl.multiple_of(jnp.maximum(r0 - CONV_HALO, 0), CONV_HALO)
            prev = x_ref[pl.ds(p0, CONV_HALO), :].astype(F32)
            prev = jnp.where(i == 0, 0.0, prev)
            ext = jnp.concatenate([prev, cur], axis=0)
            acc = _conv_taps(ext, w, CONV_HALO - (CONV_K - 1), CONV_ROWS) + b
            o_ref[pl.ds(r0, CONV_ROWS), :] = (acc * _sigmoid(acc)).astype(o_ref.dtype)
            return carry

        lax.fori_loop(0, nsteps, step, 0)

    return pl.pallas_call(
        body,
        grid=(c // CONV_COLS,),
        in_specs=[pl.BlockSpec((s, CONV_COLS), lambda j: (0, j + off)),
                  pl.BlockSpec((8, CONV_COLS), lambda j: (0, j)),
                  pl.BlockSpec((1, CONV_COLS), lambda j: (0, j))],
        out_specs=pl.BlockSpec((s, CONV_COLS), lambda j: (0, j)),
        out_shape=jax.ShapeDtypeStruct((s, c), BF16),
        compiler_params=_params(("parallel",)),
        name="ssd_conv_fwd",
    )(zx, conv_w, conv_b)


def _ssd_conv_bwd(zx, dxbc, conv_w, conv_b, dzx):
    s = zx.shape[0]
    c = SSD_CONV_DIM
    nsteps = s // CONV_ROWS
    off = SSD_INNER // CONV_COLS

    def body(x_ref, dy_ref, w_ref, b_ref, dzx_in_ref, dx_ref, dw_ref, db_ref, dc_ref):
        w = w_ref[...]
        b = b_ref[...]
        dc_ref[pl.ds(s, CONV_HALO), :] = jnp.zeros((CONV_HALO, CONV_COLS), F32)

        def step1(i, carry):
            dw0, dw1, dw2, dw3, dbs = carry
            r0 = pl.multiple_of(i * CONV_ROWS, CONV_ROWS)
            cur = x_ref[pl.ds(r0, CONV_ROWS), :].astype(F32)
            p0 = pl.multiple_of(jnp.maximum(r0 - CONV_HALO, 0), CONV_HALO)
            prev = x_ref[pl.ds(p0, CONV_HALO), :].astype(F32)
            prev = jnp.where(i == 0, 0.0, prev)
            ext = jnp.concatenate([prev, cur], axis=0)
            base = CONV_HALO - (CONV_K - 1)
            acc = _conv_taps(ext, w, base, CONV_ROWS) + b
            sg = _sigmoid(acc)
            dcv = dy_ref[pl.ds(r0, CONV_ROWS), :].astype(F32) * (sg * (1.0 + acc * (1.0 - sg)))
            dc_ref[pl.ds(r0, CONV_ROWS), :] = dcv
            dws = [jnp.sum(dcv * ext[base + k:base + k + CONV_ROWS], axis=0, keepdims=True) for k in range(CONV_K)]
            return (dw0 + dws[0], dw1 + dws[1], dw2 + dws[2], dw3 + dws[3], dbs + jnp.sum(dcv, axis=0, keepdims=True))

        z = jnp.zeros((1, CONV_COLS), F32)
        dw0, dw1, dw2, dw3, dbs = lax.fori_loop(0, nsteps, step1, (z, z, z, z, z))
        dw_ref[...] = jnp.concatenate([dw0, dw1, dw2, dw3, z, z, z, z], axis=0)
        db_ref[...] = dbs

        def step2(i, carry):
            r0 = pl.multiple_of(i * CONV_ROWS, CONV_ROWS)
            ext = dc_ref[pl.ds(r0, CONV_ROWS + CONV_HALO), :]
            acc = w[0:1, :] * ext[CONV_K - 1:CONV_K - 1 + CONV_ROWS]
            for k in range(1, CONV_K):
                acc = acc + w[k:k + 1, :] * ext[CONV_K - 1 - k:CONV_K - 1 - k + CONV_ROWS]
            dx_ref[pl.ds(r0, CONV_ROWS), :] = acc.astype(dx_ref.dtype)
            return carry

        lax.fori_loop(0, nsteps, step2, 0)

    col = pl.BlockSpec((s, CONV_COLS), lambda j: (0, j))
    shifted = pl.BlockSpec((s, CONV_COLS), lambda j: (0, j + off))
    return pl.pallas_call(
        body,
        grid=(c // CONV_COLS,),
        in_specs=[shifted, col,
                  pl.BlockSpec((8, CONV_COLS), lambda j: (0, j)),
                  pl.BlockSpec((1, CONV_COLS), lambda j: (0, j)),
                  pl.BlockSpec(memory_space=pl.ANY)],
        out_specs=[shifted, pl.BlockSpec((8, CONV_COLS), lambda j: (0, j)), pl.BlockSpec((1, CONV_COLS), lambda j: (0, j))],
        out_shape=[jax.ShapeDtypeStruct((s, SSD_ZX), BF16), jax.ShapeDtypeStruct((8, c), F32),
                   jax.ShapeDtypeStruct((1, c), F32)],
        scratch_shapes=[pltpu.VMEM((s + CONV_HALO, CONV_COLS), F32)],
        input_output_aliases={4: 0},
        compiler_params=_params(("parallel",)),
        name="ssd_conv_bwd",
    )(zx, dxbc, conv_w, conv_b, dzx)


def _ssd_consts():
    li = lax.broadcasted_iota(jnp.int32, (CHUNK, CHUNK), 0)
    si = lax.broadcasted_iota(jnp.int32, (CHUNK, CHUNK), 1)
    tril = li >= si
    hrow = lax.broadcasted_iota(jnp.int32, (LANES, SSD_INNER), 0)
    hcol = lax.broadcasted_iota(jnp.int32, (LANES, SSD_INNER), 1) // SSD_HEADDIM
    expand = (hrow == hcol).astype(F32)
    return tril, expand


def _ssd_chunk_common(dtp_ref, bias_ref, alog_ref, tril, expand):
    lane = lax.broadcasted_iota(jnp.int32, (1, LANES), 1)
    valid = lane < SSD_HEADS
    pre = dtp_ref[...] + bias_ref[...]
    dt = jnp.where(valid, jnp.maximum(pre, 0.0) + jnp.log1p(jnp.exp(-jnp.abs(pre))), 0.0)
    a = jnp.where(valid, -jnp.exp(alog_ref[...]), 0.0)
    da = dt * a
    cs = _dot01(tril.astype(F32), da, NN, 1)
    cs_x = _dot01(cs, expand, NN, 0)
    dt_x = _dot01(dt, expand, NN, 0, terms=2)
    return pre, dt, a, cs, cs_x, dt_x


def _ssd_scan_fwd(xbc, dtp, dt_bias, a_log, d_skip):
    s = xbc.shape[0]
    nc = s // CHUNK
    gw = SSD_GROUP_W

    def body(xbc_ref, dtp_ref, bias_ref, alog_ref, d_ref, y_ref, prev_ref, state_ref):
        c = pl.program_id(0)

        @pl.when(c == 0)
        def _():
            state_ref[...] = jnp.zeros_like(state_ref)

        tril, expand = _ssd_consts()
        pre, dt, a, cs, cs_x, dt_x = _ssd_chunk_common(dtp_ref, bias_ref, alog_ref, tril, expand)
        cs_t = cs.T
        d_x = _dot01(jnp.broadcast_to(d_ref[...], (8, LANES)), expand, NN, 0)[0:1, :]
        cs_last = cs_x[CHUNK - 1:CHUNK, :]
        dec_out = jnp.exp(cs_x)
        dec_st = jnp.exp(cs_last - cs_x)
        dec_ch = jnp.exp(cs_last)
        x = xbc_ref[:, 0:SSD_INNER].astype(F32)
        xr = x * dt_x
        xrs = xr * dec_st
        lane_g = lax.broadcasted_iota(jnp.int32, (1, gw), 1) // SSD_HEADDIM
        for g in range(SSD_GROUPS):
            sl = slice(g * gw, (g + 1) * gw)
            bg = xbc_ref[:, SSD_INNER + g * SSD_STATE:SSD_INNER + (g + 1) * SSD_STATE]
            cg = xbc_ref[:, SSD_INNER + (SSD_GROUPS + g) * SSD_STATE:SSD_INNER + (SSD_GROUPS + g + 1) * SSD_STATE]
            cb = _dot(cg, bg, NT)
            prev_g = state_ref[:, sl]
            prev_ref[0, :, sl] = prev_g
            yo = _dot(cg, prev_g.astype(BF16), NN) * dec_out[:, sl]
            xr_g = xr[:, sl]
            yd = jnp.zeros((CHUNK, gw), F32)
            for r in range(SSD_HEADS // SSD_GROUPS):
                h = g * (SSD_HEADS // SSD_GROUPS) + r
                diff = cs[:, h:h + 1] - cs_t[h:h + 1, :]
                lmat = jnp.exp(jnp.where(tril, diff, -1e30))
                wmat = (cb * lmat).astype(BF16)
                xr_h = jnp.where(lane_g == r, xr_g, 0.0).astype(BF16)
                yd = yd + _dot(wmat, xr_h, NN)
            y_ref[:, sl] = yd + yo + x[:, sl] * d_x[:, sl]
            sc = _dot(bg, xrs[:, sl].astype(BF16), TN)
            state_ref[:, sl] = prev_g * dec_ch[:, sl] + sc

    vec = pl.BlockSpec((1, LANES), lambda c: (0, 0))
    return pl.pallas_call(
        body,
        grid=(nc,),
        in_specs=[pl.BlockSpec((CHUNK, SSD_CONV_DIM), lambda c: (c, 0)),
                  pl.BlockSpec((CHUNK, LANES), lambda c: (c, 0)), vec, vec, vec],
        out_specs=[pl.BlockSpec((CHUNK, SSD_INNER), lambda c: (c, 0)),
                   pl.BlockSpec((1, SSD_STATE, SSD_INNER), lambda c: (c, 0, 0))],
        out_shape=[jax.ShapeDtypeStruct((s, SSD_INNER), F32), jax.ShapeDtypeStruct((nc, SSD_STATE, SSD_INNER), F32)],
        scratch_shapes=[pltpu.VMEM((SSD_STATE, SSD_INNER), F32)],
        compiler_params=_params(("arbitrary",)),
        name="ssd_scan_fwd",
    )(xbc, dtp, dt_bias, a_log, d_skip)


def _ssd_scan_bwd(xbc, dtp, prev, dy, dt_bias, a_log, d_skip):
    s = xbc.shape[0]
    nc = s // CHUNK
    gw = SSD_GROUP_W
    hpg = SSD_HEADS // SSD_GROUPS

    def body(xbc_ref, dtp_ref, prev_ref, dy_ref, bias_ref, alog_ref, d_ref,
             dxbc_ref, ddtp_ref, dbias_ref, dalog_ref, dd_ref, dp_ref, ddx_ref):
        step = pl.program_id(0)

        @pl.when(step == 0)
        def _():
            dp_ref[...] = jnp.zeros_like(dp_ref)
            ddx_ref[...] = jnp.zeros_like(ddx_ref)
            dbias_ref[...] = jnp.zeros_like(dbias_ref)
            dalog_ref[...] = jnp.zeros_like(dalog_ref)

        tril, expand = _ssd_consts()
        pre, dt, a, cs, cs_x, dt_x = _ssd_chunk_common(dtp_ref, bias_ref, alog_ref, tril, expand)
        cs_t = cs.T
        d_x = _dot01(jnp.broadcast_to(d_ref[...], (8, LANES)), expand, NN, 0)[0:1, :]
        cs_last = cs_x[CHUNK - 1:CHUNK, :]
        dec_out = jnp.exp(cs_x)
        dec_st = jnp.exp(cs_last - cs_x)
        dec_ch = jnp.exp(cs_last)
        x = xbc_ref[:, 0:SSD_INNER].astype(F32)
        dyv = dy_ref[...]
        xr = x * dt_x
        xrs = xr * dec_st
        lane_g = lax.broadcasted_iota(jnp.int32, (1, gw), 1) // SSD_HEADDIM
        hsel = lax.broadcasted_iota(jnp.int32, (CHUNK, LANES), 1)
        dcs = jnp.zeros((CHUNK, LANES), F32)
        last_parts = []
        t_parts = []
        dxr_parts = []
        for g in range(SSD_GROUPS):
            sl = slice(g * gw, (g + 1) * gw)
            bsl = slice(SSD_INNER + g * SSD_STATE, SSD_INNER + (g + 1) * SSD_STATE)
            csl = slice(SSD_INNER + (SSD_GROUPS + g) * SSD_STATE, SSD_INNER + (SSD_GROUPS + g + 1) * SSD_STATE)
            bg = xbc_ref[:, bsl]
            cg = xbc_ref[:, csl]
            cb = _dot(cg, bg, NT)
            prev_g = prev_ref[0, :, sl]
            prev_b = prev_g.astype(BF16)
            dp_g = dp_ref[:, sl]
            dp_b = dp_g.astype(BF16)
            dy_g = dyv[:, sl]
            xr_g = xr[:, sl]
            gmat = _dot(cg, prev_b, NN)
            dgm = (dy_g * dec_out[:, sl]).astype(BF16)
            dc_g = _dot(dgm, prev_b, NT)
            dprev = _dot(cg, dgm, TN)
            t1 = dy_g * gmat * dec_out[:, sl]
            mm_ = _dot(bg, dp_b, NN)
            db_g = _dot(xrs[:, sl].astype(BF16), dp_b, NT)
            dxr_g = mm_ * dec_st[:, sl]
            t2 = dxr_g * xr_g
            last = jnp.sum(t2, axis=0, keepdims=True) + jnp.sum(dp_g * prev_g, axis=0, keepdims=True) * dec_ch[:, sl]
            dp_ref[:, sl] = dp_g * dec_ch[:, sl] + dprev
            dcb = jnp.zeros((CHUNK, CHUNK), F32)
            for r in range(hpg):
                h = g * hpg + r
                diff = cs[:, h:h + 1] - cs_t[h:h + 1, :]
                lmat = jnp.exp(jnp.where(tril, diff, -1e30))
                wmat = cb * lmat
                dy_h = jnp.where(lane_g == r, dy_g, 0.0).astype(BF16)
                dw = _dot(dy_h, xr_g.astype(BF16), NT)
                dxr_g = dxr_g + _dot(wmat.astype(BF16), dy_h, TN)
                dcb = dcb + dw * lmat
                q = (dw * wmat).astype(BF16)
                onehot = (hsel == h).astype(BF16)
                dcs = dcs + _dot(q, onehot, NN) - _dot(q, onehot, TN)
            dcb_b = dcb.astype(BF16)
            dc_g = dc_g + _dot(dcb_b, bg, NN)
            db_g = db_g + _dot(dcb_b, cg, TN)
            dxbc_ref[:, bsl] = db_g.astype(dxbc_ref.dtype)
            dxbc_ref[:, csl] = dc_g.astype(dxbc_ref.dtype)
            t_parts.append(t1 - t2)
            last_parts.append(last)
            dxr_parts.append(dxr_g)
        dxr = jnp.concatenate(dxr_parts, axis=1)
        tt = jnp.concatenate(t_parts, axis=1)
        last_x = jnp.concatenate(last_parts, axis=1)
        dxbc_ref[:, 0:SSD_INNER] = (dxr * dt_x + dyv * d_x).astype(dxbc_ref.dtype)
        dcs = dcs + _dot01(tt, expand, NT, 0, terms=2)
        last_h = _dot01(jnp.broadcast_to(last_x, (8, SSD_INNER)), expand, NT, 0)[0:1, :]
        rowi = lax.broadcasted_iota(jnp.int32, (CHUNK, LANES), 0)
        dcs = dcs + jnp.where(rowi == CHUNK - 1, last_h, 0.0)
        dda = _dot01(tril.astype(F32), dcs, TN, 1)
        ddt = dda * a + _dot01(dxr * x, expand, NT, 0, terms=2)
        dpre = ddt * _sigmoid(pre)
        ddtp_ref[...] = dpre
        dbias_ref[...] += jnp.sum(dpre, axis=0, keepdims=True)
        dalog_ref[...] += jnp.sum(dda * dt, axis=0, keepdims=True) * a
        ddx_ref[...] += jnp.broadcast_to(jnp.sum(dyv * x, axis=0, keepdims=True), (8, SSD_INNER))

        @pl.when(step == nc - 1)
        def _():
            dd_ref[...] = _dot01(ddx_ref[...], expand, NT, 0)[0:1, :]

    rev = lambda c: (nc - 1 - c, 0)
    vec = pl.BlockSpec((1, LANES), lambda c: (0, 0))
    return pl.pallas_call(
        body,
        grid=(nc,),
        in_specs=[pl.BlockSpec((CHUNK, SSD_CONV_DIM), rev), pl.BlockSpec((CHUNK, LANES), rev),
                  pl.BlockSpec((1, SSD_STATE, SSD_INNER), lambda c: (nc - 1 - c, 0, 0)),
                  pl.BlockSpec((CHUNK, SSD_INNER), rev), vec, vec, vec],
        out_specs=[pl.BlockSpec((CHUNK, SSD_CONV_DIM), rev), pl.BlockSpec((CHUNK, LANES), rev), vec, vec, vec],
        out_shape=[jax.ShapeDtypeStruct((s, SSD_CONV_DIM), BF16), jax.ShapeDtypeStruct((s, LANES), F32),
                   jax.ShapeDtypeStruct((1, LANES), F32), jax.ShapeDtypeStruct((1, LANES), F32),
                   jax.ShapeDtypeStruct((1, LANES), F32)],
        scratch_shapes=[pltpu.VMEM((SSD_STATE, SSD_INNER), F32), pltpu.VMEM((8, SSD_INNER), F32)],
        compiler_params=_params(("arbitrary",)),
        name="ssd_scan_bwd",
    )(xbc, dtp, prev, dy, dt_bias, a_log, d_skip)


def _ssd_gate_fwd(y, zx, norm_w):
    s = y.shape[0]
    tr = _pick(s, (256, 128))
    gw = SSD_GROUP_W

    def body(y_ref, z_ref, w_ref, o_ref):
        for g in range(SSD_GROUPS):
            sl = slice(g * gw, (g + 1) * gw)
            z = z_ref[:, sl].astype(F32)
            gv = y_ref[:, sl] * (z * _sigmoid(z))
            r = lax.rsqrt(jnp.mean(gv * gv, axis=-1, keepdims=True) + LN_EPS)
            o_ref[:, sl] = (gv * r * w_ref[:, sl]).astype(o_ref.dtype)

    row = pl.BlockSpec((tr, SSD_INNER), lambda i: (i, 0))
    return pl.pallas_call(
        body,
        grid=(s // tr,),
        in_specs=[row, row, pl.BlockSpec((1, SSD_INNER), lambda i: (0, 0))],
        out_specs=row,
        out_shape=jax.ShapeDtypeStruct((s, SSD_INNER), BF16),
        compiler_params=_params(("parallel",)),
        name="ssd_gate_fwd",
    )(y, zx, norm_w)


def _ssd_gate_bwd(dgn, y, zx, norm_w):
    s = y.shape[0]
    tr = _pick(s, (256, 128))
    gw = SSD_GROUP_W

    def body(dg_ref, y_ref, z_ref, w_ref, dy_ref, dz_ref, dw_ref):
        parts = []
        for g in range(SSD_GROUPS):
            sl = slice(g * gw, (g + 1) * gw)
            z = z_ref[:, sl].astype(F32)
            yv = y_ref[:, sl]
            sg = _sigmoid(z)
            sz = z * sg
            gv = yv * sz
            r = lax.rsqrt(jnp.mean(gv * gv, axis=-1, keepdims=True) + LN_EPS)
            gn = gv * r
            dout = dg_ref[:, sl].astype(F32)
            parts.append(jnp.sum(dout * gn, axis=0, keepdims=True))
            dgn_ = dout * w_ref[:, sl]
            dgv = r * (dgn_ - gn * jnp.mean(dgn_ * gn, axis=-1, keepdims=True))
            dy_ref[:, sl] = dgv * sz
            dz_ref[:, sl] = (dgv * yv * (sg * (1.0 + z * (1.0 - sg)))).astype(dz_ref.dtype)
        part = jnp.concatenate(parts, axis=1)

        @pl.when(pl.program_id(0) == 0)
        def _():
            dw_ref[...] = part

        @pl.when(pl.program_id(0) > 0)
        def _():
            dw_ref[...] += part

    row = pl.BlockSpec((tr, SSD_INNER), lambda i: (i, 0))
    vec = pl.BlockSpec((1, SSD_INNER), lambda i: (0, 0))
    return pl.pallas_call(
        body,
        grid=(s // tr,),
        in_specs=[row, row, row, vec],
        out_specs=[row, row, vec],
        out_shape=[jax.ShapeDtypeStruct((s, SSD_INNER), F32), jax.ShapeDtypeStruct((s, SSD_ZX), BF16),
                   jax.ShapeDtypeStruct((1, SSD_INNER), F32)],
        compiler_params=_params(("arbitrary",)),
        name="ssd_gate_bwd",
    )(dgn, y, zx, norm_w)


INV_SQRT2 = 1.0 / math.sqrt(2.0)
INV_SQRT2PI = 1.0 / math.sqrt(2.0 * math.pi)


def _gelu(x):
    return 0.5 * x * (1.0 + lax.erf(x * INV_SQRT2))


def _gelu_grad(x):
    return 0.5 * (1.0 + lax.erf(x * INV_SQRT2)) + x * INV_SQRT2PI * jnp.exp(-0.5 * x * x)


def _gmlp_act_fwd(pre, b_in, ln_w, ln_b):
    s = pre.shape[0]
    tr = _pick(s, (256, 128))
    n = GMLP_INNER

    def body(p_ref, b_ref, w_ref, lb_ref, u_ref, v_ref):
        u_ref[...] = _gelu(p_ref[:, 0:n].astype(F32) + b_ref[:, 0:n]).astype(u_ref.dtype)
        hv = _gelu(p_ref[:, n:2 * n].astype(F32) + b_ref[:, n:2 * n])
        mu = jnp.mean(hv, axis=-1, keepdims=True)
        xc = hv - mu
        r = lax.rsqrt(jnp.mean(xc * xc, axis=-1, keepdims=True) + LN_EPS)
        v_ref[...] = (xc * r * w_ref[...] + lb_ref[...]).astype(v_ref.dtype)

    half = pl.BlockSpec((tr, n), lambda i: (i, 0))
    vec = pl.BlockSpec((1, n), lambda i: (0, 0))
    return pl.pallas_call(
        body,
        grid=(s // tr,),
        in_specs=[pl.BlockSpec((tr, 2 * n), lambda i: (i, 0)), pl.BlockSpec((1, 2 * n), lambda i: (0, 0)), vec, vec],
        out_specs=[half, half],
        out_shape=[jax.ShapeDtypeStruct((s, n), BF16), jax.ShapeDtypeStruct((s, n), BF16)],
        compiler_params=_params(("parallel",)),
        name="gmlp_act_fwd",
    )(pre, b_in, ln_w, ln_b)


def _gmlp_act_bwd(pre, b_in, ln_w, du, dv):
    s = pre.shape[0]
    tr = _pick(s, (256, 128))
    n = GMLP_INNER

    def body(p_ref, b_ref, w_ref, du_ref, dv_ref, dp_ref, db_ref, dw_ref, dlb_ref):
        xu = p_ref[:, 0:n].astype(F32) + b_ref[:, 0:n]
        dpu = du_ref[...].astype(F32) * _gelu_grad(xu)
        xv = p_ref[:, n:2 * n].astype(F32) + b_ref[:, n:2 * n]
        hv = _gelu(xv)
        mu = jnp.mean(hv, axis=-1, keepdims=True)
        xc = hv - mu
        r = lax.rsqrt(jnp.mean(xc * xc, axis=-1, keepdims=True) + LN_EPS)
        vh = xc * r
        dvv = dv_ref[...].astype(F32)
        dvh = dvv * w_ref[...]
        dh = r * (dvh - jnp.mean(dvh, axis=-1, keepdims=True) - vh * jnp.mean(dvh * vh, axis=-1, keepdims=True))
        dpv = dh * _gelu_grad(xv)
        dp_ref[:, 0:n] = dpu.astype(dp_ref.dtype)
        dp_ref[:, n:2 * n] = dpv.astype(dp_ref.dtype)
        pb = jnp.concatenate([jnp.sum(dpu, axis=0, keepdims=True), jnp.sum(dpv, axis=0, keepdims=True)], axis=1)
        pw = jnp.sum(dvv * vh, axis=0, keepdims=True)
        plb = jnp.sum(dvv, axis=0, keepdims=True)

        @pl.when(pl.program_id(0) == 0)
        def _():
            db_ref[...] = pb
            dw_ref[...] = pw
            dlb_ref[...] = plb

        @pl.when(pl.program_id(0) > 0)
        def _():
            db_ref[...] += pb
            dw_ref[...] += pw
            dlb_ref[...] += plb

    half = pl.BlockSpec((tr, n), lambda i: (i, 0))
    full = pl.BlockSpec((tr, 2 * n), lambda i: (i, 0))
    vec = pl.BlockSpec((1, n), lambda i: (0, 0))
    vec2 = pl.BlockSpec((1, 2 * n), lambda i: (0, 0))
    return pl.pallas_call(
        body,
        grid=(s // tr,),
        in_specs=[full, vec2, vec, half, half],
        out_specs=[full, vec2, vec, vec],
        out_shape=[jax.ShapeDtypeStruct((s, 2 * n), BF16), jax.ShapeDtypeStruct((1, 2 * n), F32),
                   jax.ShapeDtypeStruct((1, n), F32), jax.ShapeDtypeStruct((1, n), F32)],
        compiler_params=_params(("arbitrary",)),
        name="gmlp_act_bwd",
    )(pre, b_in, ln_w, du, dv)


def _gmlp_mix_fwd(u, v, w_s, b_st):
    s = u.shape[0]
    gd = GMLP_INNER // GMLP_GROUPS

    def body(u_ref, v_ref, w_ref, b_ref, o_ref):
        li = lax.broadcasted_iota(jnp.int32, (CHUNK, CHUNK), 0)
        si = lax.broadcasted_iota(jnp.int32, (CHUNK, CHUNK), 1)
        tril = li >= si
        for g in range(GMLP_GROUPS):
            sl = slice(g * gd, (g + 1) * gd)
            wm = jnp.where(tril, w_ref[g], 0.0).astype(BF16)
            mixed = _dot(wm, v_ref[:, sl], NN) + b_ref[:, g:g + 1]
            o_ref[:, sl] = (u_ref[:, sl].astype(F32) * mixed).astype(o_ref.dtype)

    row = pl.BlockSpec((CHUNK, GMLP_INNER), lambda c: (c, 0))
    return pl.pallas_call(
        body,
        grid=(s // CHUNK,),
        in_specs=[row, row, pl.BlockSpec((GMLP_GROUPS, CHUNK, CHUNK), lambda c: (0, 0, 0)),
                  pl.BlockSpec((CHUNK, LANES), lambda c: (0, 0))],
        out_specs=row,
        out_shape=jax.ShapeDtypeStruct((s, GMLP_INNER), BF16),
        compiler_params=_params(("parallel",)),
        name="gmlp_mix_fwd",
    )(u, v, w_s, b_st)


def _gmlp_mix_bwd(dgated, u, v, w_s, b_st):
    s = u.shape[0]
    nc = s // CHUNK
    gd = GMLP_INNER // GMLP_GROUPS

    def body(dg_ref, u_ref, v_ref, w_ref, b_ref, du_ref, dv_ref, dw_ref, db_ref):
        c = pl.program_id(0)

        @pl.when(c == 0)
        def _():
            dw_ref[...] = jnp.zeros_like(dw_ref)
            db_ref[...] = jnp.zeros_like(db_ref)

        li = lax.broadcasted_iota(jnp.int32, (CHUNK, CHUNK), 0)
        si = lax.broadcasted_iota(jnp.int32, (CHUNK, CHUNK), 1)
        tril = li >= si
        lane = lax.broadcasted_iota(jnp.int32, (CHUNK, LANES), 1)
        dbacc = jnp.zeros((CHUNK, LANES), F32)
        for g in range(GMLP_GROUPS):
            sl = slice(g * gd, (g + 1) * gd)
            wm = jnp.where(tril, w_ref[g], 0.0).astype(BF16)
            vg = v_ref[:, sl]
            mixed = _dot(wm, vg, NN) + b_ref[:, g:g + 1]
            dgv = dg_ref[:, sl].astype(F32)
            du_ref[:, sl] = (dgv * mixed).astype(du_ref.dtype)
            dm = dgv * u_ref[:, sl].astype(F32)
            dm_b = dm.astype(BF16)
            dv_ref[:, sl] = _dot(wm, dm_b, TN).astype(dv_ref.dtype)
            dw_ref[g] += jnp.where(tril, _dot(dm_b, vg, NT), 0.0)
            dbacc = dbacc + jnp.where(lane == g, jnp.sum(dm, axis=1, keepdims=True), 0.0)
        db_ref[...] += dbacc

    row = pl.BlockSpec((CHUNK, GMLP_INNER), lambda c: (c, 0))
    wspec = pl.BlockSpec((GMLP_GROUPS, CHUNK, CHUNK), lambda c: (0, 0, 0))
    bspec = pl.BlockSpec((CHUNK, LANES), lambda c: (0, 0))
    return pl.pallas_call(
        body,
        grid=(nc,),
        in_specs=[row, row, row, wspec, bspec],
        out_specs=[row, row, wspec, bspec],
        out_shape=[jax.ShapeDtypeStruct((s, GMLP_INNER), BF16), jax.ShapeDtypeStruct((s, GMLP_INNER), BF16),
                   jax.ShapeDtypeStruct((GMLP_GROUPS, CHUNK, CHUNK), F32), jax.ShapeDtypeStruct((CHUNK, LANES), F32)],
        compiler_params=_params(("arbitrary",)),
        name="gmlp_mix_bwd",
    )(dgated, u, v, w_s, b_st)


def _swiglu_fwd(gu):
    s = gu.shape[0]
    f = FFN_DIM
    tr = _pick(s, (512, 256, 128))

    def body(gu_ref, o_ref):
        gt = gu_ref[:, 0:f].astype(F32)
        o_ref[...] = (gt * _sigmoid(gt) * gu_ref[:, f:2 * f].astype(F32)).astype(o_ref.dtype)

    return pl.pallas_call(
        body,
        grid=(s // tr,),
        in_specs=[pl.BlockSpec((tr, 2 * f), lambda i: (i, 0))],
        out_specs=pl.BlockSpec((tr, f), lambda i: (i, 0)),
        out_shape=jax.ShapeDtypeStruct((s, f), BF16),
        compiler_params=_params(("parallel",)),
        name="swiglu_fwd",
    )(gu)


def _swiglu_bwd(gu, dhid):
    s = gu.shape[0]
    f = FFN_DIM
    tr = _pick(s, (512, 256, 128))

    def body(gu_ref, dh_ref, dgu_ref):
        gt = gu_ref[:, 0:f].astype(F32)
        up = gu_ref[:, f:2 * f].astype(F32)
        dh = dh_ref[...].astype(F32)
        sg = _sigmoid(gt)
        dgu_ref[:, 0:f] = (dh * up * (sg * (1.0 + gt * (1.0 - sg)))).astype(dgu_ref.dtype)
        dgu_ref[:, f:2 * f] = (dh * gt * sg).astype(dgu_ref.dtype)

    wide = pl.BlockSpec((tr, 2 * f), lambda i: (i, 0))
    return pl.pallas_call(
        body,
        grid=(s // tr,),
        in_specs=[wide, pl.BlockSpec((tr, f), lambda i: (i, 0))],
        out_specs=wide,
        out_shape=jax.ShapeDtypeStruct((s, 2 * f), BF16),
        compiler_params=_params(("parallel",)),
        name="swiglu_bwd",
    )(gu, dhid)


def _ple_fwd(pe, gl, h, ple_norm):
    s, d = h.shape
    tr = _pick(s, (512, 256, 128))

    def body(pe_ref, gl_ref, h_ref, w_ref, o_ref):
        pe_ = pe_ref[...].astype(F32)
        r = lax.rsqrt(jnp.mean(pe_ * pe_, axis=-1, keepdims=True) + RMS_EPS)
        o_ref[...] = h_ref[...] + _sigmoid(gl_ref[...].astype(F32)) * (pe_ * r * w_ref[...])

    row = pl.BlockSpec((tr, d), lambda i: (i, 0))
    return pl.pallas_call(
        body,
        grid=(s // tr,),
        in_specs=[row, row, row, pl.BlockSpec((1, d), lambda i: (0, 0))],
        out_specs=row,
        out_shape=jax.ShapeDtypeStruct((s, d), F32),
        compiler_params=_params(("parallel",)),
        name="ple_fwd",
    )(pe, gl, h, ple_norm)


def _ple_bwd(dh, pe, gl, ple_norm):
    s, d = dh.shape
    tr = _pick(s, (512, 256, 128))

    def body(dh_ref, pe_ref, gl_ref, w_ref, dgl_ref, dpe_ref, dw_ref):
        pe_ = pe_ref[...].astype(F32)
        dhv = dh_ref[...]
        r = lax.rsqrt(jnp.mean(pe_ * pe_, axis=-1, keepdims=True) + RMS_EPS)
        pn = pe_ * r
        gate = _sigmoid(gl_ref[...].astype(F32))
        dgl_ref[...] = (dhv * (pn * w_ref[...]) * gate * (1.0 - gate)).astype(dgl_ref.dtype)
        de = dhv * gate
        dxh = de * w_ref[...]
        dpe_ref[...] = (r * (dxh - pn * jnp.mean(dxh * pn, axis=-1, keepdims=True))).astype(dpe_ref.dtype)
        part = jnp.sum(de * pn, axis=0, keepdims=True)

        @pl.when(pl.program_id(0) == 0)
        def _():
            dw_ref[...] = part

        @pl.when(pl.program_id(0) > 0)
        def _():
            dw_ref[...] += part

    row = pl.BlockSpec((tr, d), lambda i: (i, 0))
    vec = pl.BlockSpec((1, d), lambda i: (0, 0))
    return pl.pallas_call(
        body,
        grid=(s // tr,),
        in_specs=[row, row, row, vec],
        out_specs=[row, row, vec],
        out_shape=[jax.ShapeDtypeStruct((s, d), BF16), jax.ShapeDtypeStruct((s, d), BF16),
                   jax.ShapeDtypeStruct((1, d), F32)],
        compiler_params=_params(("arbitrary",)),
        name="ple_bwd",
    )(dh, pe, gl, ple_norm)


def _loss_head(h, w, target):
    s, d = h.shape
    tr = _pick(s, (512, 256, 128))

    def body(h_ref, w_ref, t_ref, l_ref, dh_ref, dw_ref):
        hv = h_ref[...]
        r = lax.rsqrt(jnp.mean(hv * hv, axis=-1, keepdims=True) + RMS_EPS)
        hn = hv * r
        diff = hn * w_ref[...] - t_ref[...]
        lpart = jnp.zeros((8, LANES), F32) + (0.5 / d) * jnp.sum(jnp.sum(diff * diff, axis=1, keepdims=True), axis=0, keepdims=True)
        dy = diff * (1.0 / d)
        dxh = dy * w_ref[...]
        dh_ref[...] = r * (dxh - hn * jnp.mean(dxh * hn, axis=-1, keepdims=True))
        part = jnp.sum(dy * hn, axis=0, keepdims=True)

        @pl.when(pl.program_id(0) == 0)
        def _():
            l_ref[...] = lpart
            dw_ref[...] = part

        @pl.when(pl.program_id(0) > 0)
        def _():
            l_ref[...] += lpart
            dw_ref[...] += part

    row = pl.BlockSpec((tr, d), lambda i: (i, 0))
    vec = pl.BlockSpec((1, d), lambda i: (0, 0))
    return pl.pallas_call(
        body,
        grid=(s // tr,),
        in_specs=[row, vec, row],
        out_specs=[pl.BlockSpec((8, LANES), lambda i: (0, 0)), row, vec],
        out_shape=[jax.ShapeDtypeStruct((8, LANES), F32), jax.ShapeDtypeStruct((s, d), F32),
                   jax.ShapeDtypeStruct((1, d), F32)],
        compiler_params=_params(("arbitrary",)),
        name="loss_head",
    )(h, w, target)


PER_LAYER = ("norm_mix", "norm_ffn", "ffn_w_gu", "ffn_w_down", "ple_w_proj", "ple_norm", "ple_gate_norm", "ple_w_gate")


def _pad_lanes(v):
    return jnp.pad(v.astype(F32), (0, LANES - v.shape[0]))[None, :]


def _kernel_layouts(full):
    w = {}
    for k in ("norm_mix", "norm_ffn", "ple_norm", "ple_gate_norm", "ssd_conv_b", "ssd_norm_w", "gmlp_b_in", "gmlp_ln_w",
              "gmlp_ln_b", "gmlp_w_s"):
        w[k] = [full[k][i].astype(F32) for i in range(full[k].shape[0])]
    w["final_norm"] = full["final_norm"].astype(F32)
    n_ssd = full["ssd_w_out"].shape[0]
    if "ssd_w_in" in full:
        w["ssd_w_zx"] = [full["ssd_w_in"][j][:, :SSD_ZX].astype(BF16) for j in range(n_ssd)]
        w["ssd_w_dt"] = [jnp.pad(full["ssd_w_in"][j][:, SSD_ZX:].astype(BF16), ((0, 0), (0, LANES - SSD_HEADS)))
                         for j in range(n_ssd)]
        w["ffn_w_gu"] = [jnp.concatenate([full["ffn_w_gate"][i], full["ffn_w_up"][i]], axis=1).astype(BF16)
                         for i in range(DEPTH)]
    else:
        for k in ("ssd_w_zx", "ssd_w_dt", "ffn_w_gu"):
            w[k] = full[k]
    w["ssd_conv_w"] = [jnp.pad(full["ssd_conv_w"][j].astype(F32), ((0, 8 - CONV_K), (0, 0))) for j in range(n_ssd)]
    for k in ("ssd_dt_bias", "ssd_a_log", "ssd_d"):
        w[k] = [_pad_lanes(full[k][j]) for j in range(n_ssd)]
    w["ssd_w_out"] = [full["ssd_w_out"][j].astype(BF16) for j in range(n_ssd)]
    n_g = full["gmlp_w_in"].shape[0]
    w["gmlp_w_in"] = [full["gmlp_w_in"][j].astype(BF16) for j in range(n_g)]
    w["gmlp_w_out"] = [full["gmlp_w_out"][j].astype(BF16) for j in range(n_g)]
    w["gmlp_b_st"] = [jnp.pad(full["gmlp_b_s"][j].astype(F32).T, ((0, 0), (0, LANES - GMLP_GROUPS))) for j in range(n_g)]
    w["ffn_w_down"] =[full["ffn_w_down"][i].astype(BF16) for i in range(DEPTH)]
    w["ple_w_proj"] = [full["ple_w_proj"][i].astype(BF16) for i in range(DEPTH)]
    w["ple_w_gate"] = [full["ple_w_gate"][i].astype(BF16) for i in range(DEPTH)]
    return w


MATRICES = ("ssd_w_out", "gmlp_w_in", "gmlp_w_out", "ffn_w_down", "ple_w_proj", "ple_w_gate")


def _reference_layouts(g, wide=True):
    out = {}
    for k in ("norm_mix", "norm_ffn", "ple_norm", "ple_gate_norm", "ssd_conv_b", "ssd_norm_w", "gmlp_b_in", "gmlp_ln_w",
              "gmlp_ln_b", "gmlp_w_s", "ssd_conv_w", "ssd_dt_bias", "ssd_a_log", "ssd_d") + (MATRICES if wide else ()):
        out[k] = jnp.stack(g[k])
    out["final_norm"] = g["final_norm"]
    out["gmlp_b_s"] = jnp.stack([b[:, :GMLP_GROUPS].T for b in g["gmlp_b_st"]])
    if wide:
        out["ssd_w_in"] = jnp.stack([jnp.concatenate([zx, dt[:, :SSD_HEADS]], axis=1)
                                     for zx, dt in zip(g["ssd_w_zx"], g["ssd_w_dt"])])
        out["ffn_w_gate"] = jnp.stack([gu[:, :FFN_DIM] for gu in g["ffn_w_gu"]])
        out["ffn_w_up"] = jnp.stack([gu[:, FFN_DIM:] for gu in g["ffn_w_gu"]])
    return out


RELAYOUT_ROWS = 128
SSD_SHARD = SSD_IN_DIM // N_DEV
FFN_SHARD = FFN_DIM // N_DEV


def _to_bf16(x):
    nl, rows, n = x.shape

    def body(x_ref, o_ref):
        o_ref[...] = x_ref[...].astype(o_ref.dtype)

    blk = pl.BlockSpec((1, rows, n), lambda i: (i, 0, 0))
    return pl.pallas_call(
        body, grid=(nl,), in_specs=[blk], out_specs=blk, out_shape=jax.ShapeDtypeStruct(x.shape, BF16),
        compiler_params=_params(("parallel",)), name="to_bf16",
    )(x)


def _cat_ssd_in(gathered):
    _, nl, rows, n = gathered.shape
    tr = RELAYOUT_ROWS

    def body(g_ref, *o_refs):
        for j in range(nl):
            full = jnp.concatenate([g_ref[d, j] for d in range(N_DEV)], axis=1)
            o_refs[2 * j][...] = full[:, :SSD_ZX]
            o_refs[2 * j + 1][...] = jnp.concatenate(
                [full[:, SSD_ZX:], jnp.zeros((tr, LANES - SSD_HEADS), full.dtype)], axis=1)

    outs = pl.pallas_call(
        body, grid=(rows // tr,),
        in_specs=[pl.BlockSpec((N_DEV, nl, tr, n), lambda i: (0, 0, i, 0))],
        out_specs=[pl.BlockSpec((tr, SSD_ZX), lambda i: (i, 0)), pl.BlockSpec((tr, LANES), lambda i: (i, 0))] * nl,
        out_shape=[jax.ShapeDtypeStruct((rows, SSD_ZX), BF16), jax.ShapeDtypeStruct((rows, LANES), BF16)] * nl,
        compiler_params=_params(("parallel",)), name="cat_ssd_in",
    )(gathered)
    return [outs[2 * j] for j in range(nl)], [outs[2 * j + 1] for j in range(nl)]


def _split_ssd_in(dzx_list, ddt_list):
    nl = len(dzx_list)
    rows = dzx_list[0].shape[0]
    tr = RELAYOUT_ROWS

    def body(*refs):
        o_ref = refs[2 * nl]
        for j in range(nl):
            full = jnp.concatenate([refs[2 * j][...], refs[2 * j + 1][:, 0:SSD_HEADS]], axis=1)
            for d in range(N_DEV):
                o_ref[d, j] = full[:, d * SSD_SHARD:(d + 1) * SSD_SHARD].astype(o_ref.dtype)

    ins = []
    for j in range(nl):
        ins += [dzx_list[j], ddt_list[j]]
    return pl.pallas_call(
        body, grid=(rows // tr,),
        in_specs=[pl.BlockSpec((tr, SSD_ZX), lambda i: (i, 0)), pl.BlockSpec((tr, LANES), lambda i: (i, 0))] * nl,
        out_specs=pl.BlockSpec((N_DEV, nl, tr, SSD_SHARD), lambda i: (0, 0, i, 0)),
        out_shape=jax.ShapeDtypeStruct((N_DEV, nl, rows, SSD_SHARD), BF16),
        compiler_params=_params(("parallel",)), name="split_ssd_in",
    )(*ins)


def _cat_ffn(g_gate, g_up):
    _, nl, rows, n = g_gate.shape
    tr = RELAYOUT_ROWS

    def body(gg_ref, gu_ref, *o_refs):
        for i in range(nl):
            o_refs[i][...] = jnp.concatenate([gg_ref[d, i] for d in range(N_DEV)] + [gu_ref[d, i] for d in range(N_DEV)],
                                             axis=1)

    blk = pl.BlockSpec((N_DEV, nl, tr, n), lambda i: (0, 0, i, 0))
    outs = pl.pallas_call(
        body, grid=(rows // tr,), in_specs=[blk, blk],
        out_specs=[pl.BlockSpec((tr, 2 * FFN_DIM), lambda i: (i, 0))] * nl,
        out_shape=[jax.ShapeDtypeStruct((rows, 2 * FFN_DIM), BF16)] * nl,
        compiler_params=_params(("parallel",)), name="cat_ffn",
    )(g_gate, g_up)
    return list(outs)


def _split_ffn(dgu_list):
    nl = len(dgu_list)
    rows = dgu_list[0].shape[0]
    tr = RELAYOUT_ROWS

    def body(*refs):
        og_ref, ou_ref = refs[nl], refs[nl + 1]
        for i in range(nl):
            full = refs[i][...]
            for d in range(N_DEV):
                og_ref[d, i] = full[:, d * FFN_SHARD:(d + 1) * FFN_SHARD].astype(og_ref.dtype)
                ou_ref[d, i] = full[:, FFN_DIM + d * FFN_SHARD:FFN_DIM + (d + 1) * FFN_SHARD].astype(ou_ref.dtype)

    blk = pl.BlockSpec((N_DEV, nl, tr, FFN_SHARD), lambda i: (0, 0, i, 0))
    sds = jax.ShapeDtypeStruct((N_DEV, nl, rows, FFN_SHARD), BF16)
    return pl.pallas_call(
        body, grid=(rows // tr,),
        in_specs=[pl.BlockSpec((tr, 2 * FFN_DIM), lambda i: (i, 0))] * nl,
        out_specs=[blk, blk], out_shape=[sds, sds],
        compiler_params=_params(("parallel",)), name="split_ffn",
    )(*dgu_list)


def _local_step(x, p, target, w):
    saved = []
    h = x
    for i in range(DEPTH):
        j = i // 2
        sv = {"h0": h}
        hn = _rms_fwd(h, w["norm_mix"][i][None, :])
        sv["hn"] = hn
        if i % 2 == 0:
            zx = _mm(hn, w["ssd_w_zx"][j], "nn", BF16)
            dtp = _mm(hn, w["ssd_w_dt"][j], "nn", F32)
            xbc = _ssd_conv_fwd(zx, w["ssd_conv_w"][j], w["ssd_conv_b"][j][None, :])
            y, prev = _ssd_scan_fwd(xbc, dtp, w["ssd_dt_bias"][j], w["ssd_a_log"][j], w["ssd_d"][j])
            gn = _ssd_gate_fwd(y, zx, w["ssd_norm_w"][j][None, :])
            h = _mm(gn, w["ssd_w_out"][j], "nn", F32, add=h)
            sv.update(zx=zx, dtp=dtp, xbc=xbc, y=y, prev=prev, gn=gn)
        else:
            pre = _mm(hn, w["gmlp_w_in"][j], "nn", BF16)
            u, v = _gmlp_act_fwd(pre, w["gmlp_b_in"][j][None, :], w["gmlp_ln_w"][j][None, :], w["gmlp_ln_b"][j][None, :])
            gated = _gmlp_mix_fwd(u, v, w["gmlp_w_s"][j], w["gmlp_b_st"][j])
            h = _mm(gated, w["gmlp_w_out"][j], "nn", F32, add=h)
            sv.update(pre=pre, u=u, v=v, gated=gated)
        sv["h1"] = h
        un = _rms_fwd(h, w["norm_ffn"][i][None, :])
        gu = _mm(un, w["ffn_w_gu"][i], "nn", BF16)
        hid = _swiglu_fwd(gu)
        h = _mm(hid, w["ffn_w_down"][i], "nn", F32, add=h)
        sv.update(un=un, gu=gu, hid=hid, h2=h)
        pe = _mm(p[i], w["ple_w_proj"][i], "nn", BF16)
        hg = _rms_fwd(h, w["ple_gate_norm"][i][None, :])
        gl = _mm(hg, w["ple_w_gate"][i], "nn", BF16)
        h = _ple_fwd(pe, gl, h, w["ple_norm"][i][None, :])
        sv.update(pe=pe, hg=hg, gl=gl)
        saved.append(sv)

    lpart, dh, d_final = _loss_head(h, w["final_norm"][None, :], target)
    g = {k: [None] * (DEPTH if k in PER_LAYER else DEPTH // 2) for k in w if k != "final_norm"}
    g["final_norm"] = d_final[0]

    for i in reversed(range(DEPTH)):
        j = i // 2
        sv = saved[i]
        dgl, dpe, d_ple_norm = _ple_bwd(dh, sv["pe"], sv["gl"], w["ple_norm"][i][None, :])
        g["ple_norm"][i] = d_ple_norm[0]
        g["ple_w_gate"][i] = _mm(sv["hg"], dgl, "tn", F32)
        g["ple_w_proj"][i] = _mm(p[i], dpe, "tn", F32)
        dhg = _mm(dgl, w["ple_w_gate"][i], "nt", BF16)
        dh, d_gate_norm = _rms_bwd(dhg, sv["h2"], w["ple_gate_norm"][i][None, :], dh)
        g["ple_gate_norm"][i] = d_gate_norm[0]
        dhid = _mm(dh, w["ffn_w_down"][i], "nt", BF16)
        g["ffn_w_down"][i] = _mm(sv["hid"], dh, "tn", F32)
        dgu = _swiglu_bwd(sv["gu"], dhid)
        g["ffn_w_gu"][i] = _mm(sv["un"], dgu, "tn", F32)
        dun = _mm(dgu, w["ffn_w_gu"][i], "nt", BF16)
        dh, d_norm_ffn = _rms_bwd(dun, sv["h1"], w["norm_ffn"][i][None, :], dh)
        g["norm_ffn"][i] = d_norm_ffn[0]
        if i % 2 == 0:
            dgn = _mm(dh, w["ssd_w_out"][j], "nt", BF16)
            g["ssd_w_out"][j] = _mm(sv["gn"], dh, "tn", F32)
            dy, dzx, d_norm_w = _ssd_gate_bwd(dgn, sv["y"], sv["zx"], w["ssd_norm_w"][j][None, :])
            g["ssd_norm_w"][j] = d_norm_w[0]
            dxbc, ddtp, d_bias, d_alog, d_d = _ssd_scan_bwd(sv["xbc"], sv["dtp"], sv["prev"], dy, w["ssd_dt_bias"][j],
                                                            w["ssd_a_log"][j], w["ssd_d"][j])
            g["ssd_dt_bias"][j] = d_bias[0, :SSD_HEADS]
            g["ssd_a_log"][j] = d_alog[0, :SSD_HEADS]
            g["ssd_d"][j] = d_d[0, :SSD_HEADS]
            dzx, d_conv_w, d_conv_b = _ssd_conv_bwd(sv["zx"], dxbc, w["ssd_conv_w"][j], w["ssd_conv_b"][j][None, :], dzx)
            g["ssd_conv_w"][j] = d_conv_w[:CONV_K]
            g["ssd_conv_b"][j] = d_conv_b[0]
            g["ssd_w_zx"][j] = _mm(sv["hn"], dzx, "tn", F32)
            g["ssd_w_dt"][j] = _mm(sv["hn"], ddtp, "tn", F32)
            dhn = _mm(ddtp, w["ssd_w_dt"][j], "nt", F32)
            dhn = _mm(dzx, w["ssd_w_zx"][j], "nt", BF16, add=dhn)
        else:
            dgated = _mm(dh, w["gmlp_w_out"][j], "nt", BF16)
            g["gmlp_w_out"][j] = _mm(sv["gated"], dh, "tn", F32)
            du, dv, d_ws, d_bst = _gmlp_mix_bwd(dgated, sv["u"], sv["v"], w["gmlp_w_s"][j], w["gmlp_b_st"][j])
            g["gmlp_w_s"][j] = d_ws
            g["gmlp_b_st"][j] = d_bst
            dpre, d_bin, d_lnw, d_lnb = _gmlp_act_bwd(sv["pre"], w["gmlp_b_in"][j][None, :], w["gmlp_ln_w"][j][None, :],
                                                     du, dv)
            g["gmlp_b_in"][j] = d_bin[0]
            g["gmlp_ln_w"][j] = d_lnw[0]
            g["gmlp_ln_b"][j] = d_lnb[0]
            g["gmlp_w_in"][j] = _mm(sv["hn"], dpre, "tn", F32)
            dhn = _mm(dpre, w["gmlp_w_in"][j], "nt", BF16)
        dh, d_norm_mix = _rms_bwd(dhn, sv["h0"], w["norm_mix"][i][None, :], dh)
        g["norm_mix"][i] = d_norm_mix[0]
    return lpart[0, 0], dh, g


PACK_COLS = 1024
ANY = pl.BlockSpec(memory_space=pl.ANY)


def _mesh_pos():
    return lax.axis_index("x"), lax.axis_index("y"), lax.axis_index("c")


def _all_gather(xs_list, name):
    n = len(xs_list)

    def body(*refs):
        x_refs, out_refs = refs[:n], refs[n:2 * n]
        send_sems, recv_sems, local_sems = refs[2 * n:]
        x, y, c = _mesh_pos()
        me, sibling = (x, y, c), (x, y, 1 - c)
        chips = [(1 - x, y), (x, 1 - y), (1 - x, 1 - y)]

        def copy(a, k, block, to, from_input=False):
            px, py, pc = block
            dst = out_refs[a].at[4 * px + 2 * py + pc]
            return pltpu.make_async_remote_copy(
                src_ref=x_refs[a] if from_input else dst, dst_ref=dst,
                send_sem=send_sems.at[7 * a + k], recv_sem=recv_sems.at[7 * a + k], device_id=to,
                device_id_type=MESH_ID)

        mine = [pltpu.make_async_copy(x_refs[a], out_refs[a].at[4 * x + 2 * y + c], local_sems.at[a]) for a in range(n)]
        for cp in mine:
            cp.start()
        first = []
        for a in range(n):
            first += [copy(a, 1 + j, me, (*chip, c), from_input=True) for j, chip in enumerate(chips)]
            first.append(copy(a, 0, me, sibling, from_input=True))
        for cp in first:
            cp.start()
        passed = []
        for a in range(n):
            for j, chip in enumerate(chips):
                copy(a, 1 + j, (*chip, c), me).wait_recv()
                fwd = copy(a, 4 + j, (*chip, c), sibling)
                fwd.start()
                passed.append(fwd)
        for a in range(n):
            copy(a, 0, sibling, me).wait_recv()
            for j, chip in enumerate(chips):
                copy(a, 4 + j, (*chip, 1 - c), me).wait_recv()
        for cp in first + passed:
            cp.wait_send()
        for cp in mine:
            cp.wait()

    outs = pl.pallas_call(
        body,
        out_shape=[jax.ShapeDtypeStruct((N_DEV,) + t.shape, t.dtype) for t in xs_list],
        in_specs=[ANY] * n,
        out_specs=[ANY] * n,
        scratch_shapes=[pltpu.SemaphoreType.DMA((7 * n,)), pltpu.SemaphoreType.DMA((7 * n,)),
                        pltpu.SemaphoreType.DMA((n,))],
        name=name,
    )(*xs_list)
    return list(outs)


def _exchange_sibling(send_list):
    n = len(send_list)

    def body(*refs):
        s_refs, land_refs = refs[:n], refs[n:2 * n]
        send_sems, recv_sems = refs[2 * n:]
        x, y, c = _mesh_pos()
        cps = [pltpu.make_async_remote_copy(src_ref=s_refs[a], dst_ref=land_refs[a], send_sem=send_sems.at[a],
                                            recv_sem=recv_sems.at[a], device_id=(x, y, 1 - c), device_id_type=MESH_ID)
               for a in range(n)]
        for cp in cps:
            cp.start()
        for cp in cps:
            cp.wait()

    outs = pl.pallas_call(
        body,
        out_shape=[jax.ShapeDtypeStruct(t.shape, t.dtype) for t in send_list],
        in_specs=[ANY] * n,
        out_specs=[ANY] * n,
        scratch_shapes=[pltpu.SemaphoreType.DMA((n,)), pltpu.SemaphoreType.DMA((n,))],
        name="rs_exchange_sibling",
    )(*send_list)
    return list(outs)


def _exchange_chips(partial_list):
    n = len(partial_list)

    def body(*refs):
        p_refs, land_refs = refs[:n], refs[n:2 * n]
        send_sems, recv_sems = refs[2 * n:]
        x, y, c = _mesh_pos()
        chips = [(1 - x, y), (x, 1 - y), (1 - x, 1 - y)]
        cps = [pltpu.make_async_remote_copy(src_ref=p_refs[a].at[2 * cx + cy], dst_ref=land_refs[a].at[j],
                                            send_sem=send_sems.at[3 * a + j], recv_sem=recv_sems.at[3 * a + j],
                                            device_id=(cx, cy, c), device_id_type=MESH_ID)
               for a in range(n) for j, (cx, cy) in enumerate(chips)]
        for cp in cps:
            cp.start()
        for cp in cps:
            cp.wait()

    outs = pl.pallas_call(
        body,
        out_shape=[jax.ShapeDtypeStruct((3,) + t.shape[1:], t.dtype) for t in partial_list],
        in_specs=[ANY] * n,
        out_specs=[ANY] * n,
        scratch_shapes=[pltpu.SemaphoreType.DMA((3 * n,)), pltpu.SemaphoreType.DMA((3 * n,))],
        name="rs_exchange_chips",
    )(*partial_list)
    return list(outs)


def _sum_pairs(a, b):
    shape = a.shape
    a = a.reshape(shape[0], -1, shape[-1])
    b = b.reshape(a.shape)
    n, r, cdim = a.shape
    tr = _pick(r, (1024, 512, 256, 128, 64))

    def body(a_ref, b_ref, o_ref):
        o_ref[...] = (a_ref[...].astype(F32) + b_ref[...].astype(F32)).astype(o_ref.dtype)

    blk = pl.BlockSpec((1, tr, cdim), lambda i, j: (i, j, 0))
    return pl.pallas_call(
        body, grid=(n, r // tr), in_specs=[blk, blk], out_specs=blk,
        out_shape=jax.ShapeDtypeStruct(a.shape, a.dtype),
        compiler_params=_params(("parallel", "parallel")), name="rs_sum_pairs",
    )(a, b).reshape(shape)


def _sum_final(own, land):
    shape = own.shape
    own = own.reshape(-1, shape[-1])
    land = land.reshape((3,) + own.shape)
    r, cdim = own.shape
    tr = _pick(r, (1024, 512, 256, 128, 64))

    def body(o_ref, l_ref, out_ref):
        acc = o_ref[...].astype(F32)
        for j in range(3):
            acc = acc + l_ref[j].astype(F32)
        out_ref[...] = acc

    return pl.pallas_call(
        body, grid=(r // tr,),
        in_specs=[pl.BlockSpec((tr, cdim), lambda i: (i, 0)), pl.BlockSpec((3, tr, cdim), lambda i: (0, i, 0))],
        out_specs=pl.BlockSpec((tr, cdim), lambda i: (i, 0)),
        out_shape=jax.ShapeDtypeStruct((r, cdim), F32),
        compiler_params=_params(("parallel",)), name="rs_sum_final",
    )(own, land).reshape(shape)


def _sum_devices(gathered):
    n, r, cdim = gathered.shape
    tr = _pick(r, (64, 32, 16, 8))

    def body(g_ref, out_ref):
        acc = g_ref[0]
        for q in range(1, n):
            acc = acc + g_ref[q]
        out_ref[...] = acc

    return pl.pallas_call(
        body, grid=(r // tr,),
        in_specs=[pl.BlockSpec((n, tr, cdim), lambda i: (0, i, 0))],
        out_specs=pl.BlockSpec((tr, cdim), lambda i: (i, 0)),
        out_shape=jax.ShapeDtypeStruct((r, cdim), F32),
        compiler_params=_params(("parallel",)), name="sum_devices",
    )(gathered)


def _adamw(w, g, m, v):
    shape = w.shape
    cols = shape[-1]
    rows = w.size // cols
    tr = _pick(rows, (512, 256, 128, 64, 32, 16, 8))
    c1 = 1.0 - ADAM_B1 ** ADAM_STEP
    c2 = 1.0 - ADAM_B2 ** ADAM_STEP

    def body(w_ref, g_ref, m_ref, v_ref, d_ref, nm_ref, nv_ref):
        gv = g_ref[...]
        m2 = ADAM_B1 * m_ref[...] + (1.0 - ADAM_B1) * gv
        v2 = ADAM_B2 * v_ref[...] + (1.0 - ADAM_B2) * (gv * gv)
        d_ref[...] = -ADAM_LR * ((m2 / c1) / (jnp.sqrt(v2 / c2) + ADAM_EPS) + ADAM_WD * w_ref[...])
        nm_ref[...] = m2
        nv_ref[...] = v2

    blk = pl.BlockSpec((tr, cols), lambda i: (i, 0))
    sds = jax.ShapeDtypeStruct((rows, cols), F32)
    outs = pl.pallas_call(
        body, grid=(rows // tr,), in_specs=[blk] * 4, out_specs=[blk] * 3, out_shape=[sds] * 3,
        compiler_params=_params(("parallel",)), name=f"adamw_{rows}x{cols}",
    )(*(t.reshape(rows, cols) for t in (w, g, m, v)))
    return tuple(o.reshape(shape) for o in outs)


WEIGHTS = ("norm_mix", "norm_ffn", "ssd_w_in", "ssd_conv_w", "ssd_conv_b", "ssd_dt_bias", "ssd_a_log", "ssd_d",
           "ssd_norm_w", "ssd_w_out", "gmlp_w_in", "gmlp_b_in", "gmlp_ln_w", "gmlp_ln_b", "gmlp_w_s", "gmlp_b_s",
           "gmlp_w_out", "ffn_w_gate", "ffn_w_up", "ffn_w_down", "ple_w_proj", "ple_norm", "ple_gate_norm",
           "ple_w_gate", "final_norm")
ARG_NAMES = ("x", "p") + WEIGHTS + ("loss_target",) + tuple("m_" + n for n in WEIGHTS) + tuple("v_" + n for n in WEIGHTS)
SHARD_AXIS = {"ssd_w_in": 2, "ssd_conv_w": 2, "ssd_w_out": 1, "gmlp_w_in": 2, "gmlp_b_in": 1, "gmlp_ln_w": 1,
              "gmlp_ln_b": 1, "gmlp_w_out": 1, "ffn_w_gate": 2, "ffn_w_up": 2, "ffn_w_down": 1, "ple_w_proj": 2,
              "ple_w_gate": 1}
GATHER_BF16 = ("ssd_w_in", "ssd_w_out", "gmlp_w_in", "gmlp_w_out", "ffn_w_gate", "ffn_w_up", "ffn_w_down",
               "ple_w_proj", "ple_w_gate")
GATHER_F32 = ("ssd_conv_w", "gmlp_b_in", "gmlp_ln_w", "gmlp_ln_b")
SHARDED = GATHER_BF16 + GATHER_F32
WIDE = ("ssd_w_in", "ffn_w_gate", "ffn_w_up")
REPLICATED = tuple(n for n in WEIGHTS if n not in SHARD_AXIS)


def _pack(arrs, dtype, row_mult, lead=0):
    flat = jnp.concatenate([t.reshape(t.shape[:lead] + (-1,)).astype(dtype) for t in arrs], axis=lead)
    n = flat.shape[-1]
    unit = row_mult * PACK_COLS
    total = -(-n // unit) * unit
    flat = jnp.pad(flat, [(0, 0)] * lead + [(0, total - n)])
    return flat.reshape(flat.shape[:lead] + (total // PACK_COLS, PACK_COLS))


def _unpack(buf, names, shapes, lead=0):
    flat = buf.reshape(buf.shape[:lead] + (-1,))
    out, off = {}, 0
    for n in names:
        size = math.prod(shapes[n])
        out[n] = lax.slice_in_dim(flat, off, off + size, axis=lead).reshape(buf.shape[:lead] + tuple(shapes[n]))
        off += size
    return out


ROW_PACKED = ((1024, ("ssd_w_out", "gmlp_w_out", "ffn_w_down", "ple_w_gate")), (512, ("gmlp_w_in",)),
              (128, ("ple_w_proj",)))
ROW_PACK_MULT = 1024


def _pack_rows(arrs, width, lead=0):
    parts = [t.reshape(t.shape[:lead] + (-1, width)).astype(BF16) for t in arrs]
    rows = sum(t.shape[lead] for t in parts)
    pad = -rows % ROW_PACK_MULT
    if pad:
        parts.append(jnp.zeros(parts[0].shape[:lead] + (pad, width), BF16))
    return jnp.concatenate(parts, axis=lead)


def _unpack_rows(buf, names, shapes, lead=0):
    width = buf.shape[-1]
    out, off = {}, 0
    for n in names:
        rows = math.prod(shapes[n]) // width
        out[n] = lax.slice_in_dim(buf, off, off + rows, axis=lead).reshape(buf.shape[:lead] + tuple(shapes[n]))
        off += rows
    return out


def _merge_shards(seg, ax):
    t = jnp.moveaxis(seg, 0, ax)
    return t.reshape(t.shape[:ax] + (t.shape[ax] * t.shape[ax + 1],) + t.shape[ax + 2:])


def _split_for_cores(gfull, ax, c):
    shp = gfull.shape
    t = gfull.reshape(shp[:ax] + (2, 2, 2, shp[ax] // N_DEV) + shp[ax + 1:])

    def take(core):
        u = lax.dynamic_index_in_dim(t, core, axis=ax + 2, keepdims=False)
        u = jnp.moveaxis(u, (ax, ax + 1), (0, 1))
        return u.reshape((4,) + u.shape[2:])

    return take(c), take(1 - c)


def kernel(x, p, norm_mix, norm_ffn, ssd_w_in, ssd_conv_w, ssd_conv_b, ssd_dt_bias, ssd_a_log, ssd_d,
           ssd_norm_w, ssd_w_out, gmlp_w_in, gmlp_b_in, gmlp_ln_w, gmlp_ln_b, gmlp_w_s, gmlp_b_s,
           gmlp_w_out, ffn_w_gate, ffn_w_up, ffn_w_down, ple_w_proj, ple_norm, ple_gate_norm, ple_w_gate,
           final_norm, loss_target, m_norm_mix, m_norm_ffn, m_ssd_w_in, m_ssd_conv_w, m_ssd_conv_b,
           m_ssd_dt_bias, m_ssd_a_log, m_ssd_d, m_ssd_norm_w, m_ssd_w_out, m_gmlp_w_in, m_gmlp_b_in,
           m_gmlp_ln_w, m_gmlp_ln_b, m_gmlp_w_s, m_gmlp_b_s, m_gmlp_w_out, m_ffn_w_gate, m_ffn_w_up,
           m_ffn_w_down, m_ple_w_proj, m_ple_norm, m_ple_gate_norm, m_ple_w_gate, m_final_norm, v_norm_mix,
           v_norm_ffn, v_ssd_w_in, v_ssd_conv_w, v_ssd_conv_b, v_ssd_dt_bias, v_ssd_a_log, v_ssd_d,
           v_ssd_norm_w, v_ssd_w_out, v_gmlp_w_in, v_gmlp_b_in, v_gmlp_ln_w, v_gmlp_ln_b, v_gmlp_w_s,
           v_gmlp_b_s, v_gmlp_w_out, v_ffn_w_gate, v_ffn_w_up, v_ffn_w_down, v_ple_w_proj, v_ple_norm,
           v_ple_gate_norm, v_ple_w_gate, v_final_norm):
    given = locals()
    a = {n: given[n] for n in ARG_NAMES}
    mx, my, c = _mesh_pos()
    xs = a["x"][0]
    ps = a["p"][:, 0]
    target = a["loss_target"][0]
    shard_shapes = {n: a[n].shape for n in WEIGHTS}

    full = {n: a[n] for n in REPLICATED}
    row_packs = [_pack_rows([a[n] for n in names], wd) for wd, names in ROW_PACKED]
    got = _all_gather(row_packs + [_pack([a[n] for n in GATHER_F32], F32, 8)] + [_to_bf16(a[n]) for n in WIDE],
                      "ag_weights")
    for (wd, names), buf in zip(ROW_PACKED, got):
        for n, seg in _unpack_rows(buf, names, shard_shapes, lead=1).items():
            full[n] = _merge_shards(seg, SHARD_AXIS[n])
    k0 = len(ROW_PACKED)
    for n, seg in _unpack(got[k0], GATHER_F32, shard_shapes, lead=1).items():
        full[n] = _merge_shards(seg, SHARD_AXIS[n])
    full["ssd_w_zx"], full["ssd_w_dt"] = _cat_ssd_in(got[k0 + 1])
    full["ffn_w_gu"] = _cat_ffn(got[k0 + 2], got[k0 + 3])

    lpart, dx, g = _local_step(xs, ps, target, _kernel_layouts(full))
    gfull = _reference_layouts(g, wide=False)
    loss = lax.psum(lpart, ("x", "y", "c"))

    def by_core(t):
        u = t.reshape((4, 2) + t.shape[1:])
        return (lax.dynamic_index_in_dim(u, c, axis=1, keepdims=False),
                lax.dynamic_index_in_dim(u, 1 - c, axis=1, keepdims=False))

    def layer_halves(gl, ax):
        if ax == 0:
            t = gl.reshape(4, 2, -1, gl.shape[-1])
            return tuple(lax.dynamic_index_in_dim(t, cc, axis=1, keepdims=False) for cc in (c, 1 - c))
        t = gl.reshape(gl.shape[0], 4, 2, -1)
        return tuple(jnp.moveaxis(lax.dynamic_index_in_dim(t, cc, axis=2, keepdims=False), 1, 0) for cc in (c, 1 - c))

    pairs = []
    for wd, names in ROW_PACKED:
        hs = [layer_halves(gl, SHARD_AXIS[n] - 1) for n in names for gl in g[n]]
        pairs.append(tuple(_pack_rows([h[i] for h in hs], wd, lead=1) for i in (0, 1)))
    halves = [_split_for_cores(gfull[n], SHARD_AXIS[n], c) for n in GATHER_F32]
    pairs.append((_pack([h[0] for h in halves], BF16, 16, lead=1), _pack([h[1] for h in halves], BF16, 16, lead=1)))
    pairs += [by_core(t) for t in (_split_ssd_in(g["ssd_w_zx"], g["ssd_w_dt"]),) + tuple(_split_ffn(g["ffn_w_gu"]))]
    landed = _exchange_sibling([s for _, s in pairs])
    partials = [_sum_pairs(k, l) for (k, _), l in zip(pairs, landed)]
    landed = _exchange_chips(partials)
    sums = [_sum_final(lax.dynamic_index_in_dim(t, 2 * mx + my, axis=0, keepdims=False), l)
            for t, l in zip(partials, landed)]
    gshard = {}
    for (wd, names), buf in zip(ROW_PACKED, sums):
        gshard.update(_unpack_rows(buf, names, shard_shapes))
    gshard.update(_unpack(sums[k0], GATHER_F32, shard_shapes))
    gshard.update(zip(WIDE, sums[k0 + 1:]))
    rep = _all_gather([_pack([gfull[n] for n in REPLICATED], F32, 64)], "ag_replicated_grads")[0]
    grep = _unpack(_sum_devices(rep), REPLICATED, shard_shapes)
    grads = {**gshard, **grep}

    upd = {n: _adamw(a[n], grads[n], a["m_" + n], a["v_" + n]) for n in WEIGHTS}
    return (loss, dx[None], *[grads[n] for n in WEIGHTS], *[upd[n][0] for n in WEIGHTS],
            *[upd[n][1] for n in WEIGHTS], *[upd[n][2] for n in WEIGHTS])
```

```python
import math

import jax
import jax.numpy as jnp
from jax import lax
from jax.experimental import pallas as pl
from jax.experimental.pallas import tpu as pltpu

F32 = jnp.float32
BF16 = jnp.bfloat16

N_DEV = 8
D_MODEL = 1024
DEPTH = 4
SSD_INNER = 2048
SSD_HEADS = 32
SSD_HEADDIM = 64
SSD_GROUPS = 8
SSD_STATE = 128
SSD_GROUP_W = SSD_INNER // SSD_GROUPS
SSD_CONV_DIM = SSD_INNER + 2 * SSD_GROUPS * SSD_STATE
SSD_IN_DIM = 2 * SSD_INNER + SSD_CONV_DIM - SSD_INNER + SSD_HEADS
SSD_ZX = SSD_INNER + SSD_CONV_DIM
CONV_K = 4
CHUNK = 128
GMLP_INNER = 2048
GMLP_GROUPS = 16
FFN_DIM = 2816
PLE_DIM = 256
RMS_EPS = 1e-6
LN_EPS = 1e-5
LANES = 128
VMEM_LIMIT = 56 * 1024 * 1024

ADAM_LR = 0.001
ADAM_B1 = 0.9
ADAM_B2 = 0.999
ADAM_EPS = 1e-08
ADAM_WD = 0.01
ADAM_STEP = 10

MESH_ID = pl.DeviceIdType.MESH


def _pick(n, cands):
    for c in cands:
        if c <= n and n % c == 0:
            return c
    return n


def _params(dims):
    return pltpu.CompilerParams(dimension_semantics=dims, vmem_limit_bytes=VMEM_LIMIT)


def _dot(a, b, dims=(((1,), (0,)), ((), ())), precision=None):
    return lax.dot_general(a, b, dims, precision=precision, preferred_element_type=F32)


NN = (((1,), (0,)), ((), ()))
NT = (((1,), (1,)), ((), ()))
TN = (((0,), (0,)), ((), ()))


def _sigmoid(x):
    return 1.0 / (1.0 + jnp.exp(-x))


def _dot01(a, b, dims, split, terms=3):
    v = (a, b)[split]
    ones = (a, b)[1 - split].astype(BF16)
    acc = None
    for _ in range(terms):
        piece = v.astype(BF16)
        v = v - piece.astype(F32)
        part = _dot(piece, ones, dims) if split == 0 else _dot(ones, piece, dims)
        acc = part if acc is None else acc + part
    return acc


MM_VMEM_BUDGET = 36 * 1024 * 1024


def _mm_tiles(mode, m, n, k, a_bytes, b_bytes, out_bytes, has_add):
    tm = _pick(m, (1408, 1024, 512, 256, 128))
    tn_cands = [c for c in (2816, 1024, 512, 256, 128) if c <= n and n % c == 0] or [n]
    tk_cands = [k] + [c for c in (2816, 2048, 1024, 512, 256, 128) if c < k and k % c == 0]
    for tk in tk_cands:
        for tn in tn_cands:
            blocks = tm * tk * a_bytes + tk * tn * b_bytes + tm * tn * (out_bytes + (4 if has_add else 0))
            if 2 * blocks + (tm * tn * 4 if tk < k else 0) <= MM_VMEM_BUDGET:
                return tm, tn, tk
    return tm, tn_cands[-1], tk_cands[-1]


def _mm(a, b, mode, out_dtype, add=None):
    if mode == "nn":
        m, k = a.shape
        n = b.shape[1]
    elif mode == "nt":
        m, k = a.shape
        n = b.shape[0]
    else:
        k, m = a.shape
        n = b.shape[1]
    tm, tn, tk = _mm_tiles(mode, m, n, k, a.dtype.itemsize, b.dtype.itemsize, jnp.dtype(out_dtype).itemsize,
                           add is not None)
    nk = k // tk
    dims = {"nn": NN, "nt": NT, "tn": TN}[mode]

    def body(*refs):
        if add is None:
            a_ref, b_ref, o_ref = refs[:3]
            add_ref = None
            rest = refs[3:]
        else:
            a_ref, b_ref, add_ref, o_ref = refs[:4]
            rest = refs[4:]
        part = _dot(a_ref[...].astype(BF16), b_ref[...].astype(BF16), dims)

        def finish(acc):
            if add_ref is not None:
                acc = acc + add_ref[...]
            o_ref[...] = acc.astype(o_ref.dtype)

        if nk == 1:
            finish(part)
        else:
            acc_ref = rest[0]
            kk = pl.program_id(2)

            @pl.when(kk == 0)
            def _():
                acc_ref[...] = part

            @pl.when(kk > 0)
            def _():
                acc_ref[...] += part

            @pl.when(kk == nk - 1)
            def _():
                finish(acc_ref[...])

    if mode == "nn":
        a_spec = pl.BlockSpec((tm, tk), lambda i, j, kk: (i, kk))
        b_spec = pl.BlockSpec((tk, tn), lambda i, j, kk: (kk, j))
    elif mode == "nt":
        a_spec = pl.BlockSpec((tm, tk), lambda i, j, kk: (i, kk))
        b_spec = pl.BlockSpec((tn, tk), lambda i, j, kk: (j, kk))
    else:
        a_spec = pl.BlockSpec((tk, tm), lambda i, j, kk: (kk, i))
        b_spec = pl.BlockSpec((tk, tn), lambda i, j, kk: (kk, j))
    o_spec = pl.BlockSpec((tm, tn), lambda i, j, kk: (i, j))
    in_specs = [a_spec, b_spec] + ([o_spec] if add is not None else [])
    args = (a, b) + ((add,) if add is not None else ())
    return pl.pallas_call(
        body,
        grid=(m // tm, n // tn, nk),
        in_specs=in_specs,
        out_specs=o_spec,
        out_shape=jax.ShapeDtypeStruct((m, n), out_dtype),
        scratch_shapes=[pltpu.VMEM((tm, tn), F32)] if nk > 1 else [],
        compiler_params=_params(("parallel", "parallel", "arbitrary")),
        name=f"mm_{mode}_{m}x{k}x{n}",
    )(*args)


def _rms_fwd(x, w):
    s, d = x.shape
    tr = _pick(s, (512, 256, 128))

    def body(x_ref, w_ref, o_ref):
        xv = x_ref[...]
        r = lax.rsqrt(jnp.mean(xv * xv, axis=-1, keepdims=True) + RMS_EPS)
        o_ref[...] = (xv * r * w_ref[...]).astype(o_ref.dtype)

    return pl.pallas_call(
        body,
        grid=(s // tr,),
        in_specs=[pl.BlockSpec((tr, d), lambda i: (i, 0)), pl.BlockSpec((1, d), lambda i: (0, 0))],
        out_specs=pl.BlockSpec((tr, d), lambda i: (i, 0)),
        out_shape=jax.ShapeDtypeStruct((s, d), BF16),
        compiler_params=_params(("parallel",)),
        name="rms_fwd",
    )(x, w)


def _rms_bwd(dyn, x, w, add):
    s, d = x.shape
    tr = _pick(s, (512, 256, 128))

    def body(dy_ref, x_ref, w_ref, add_ref, dx_ref, dw_ref):
        xv = x_ref[...]
        dy = dy_ref[...].astype(F32)
        r = lax.rsqrt(jnp.mean(xv * xv, axis=-1, keepdims=True) + RMS_EPS)
        xn = xv * r
        dxh = dy * w_ref[...]
        dx = r * (dxh - xn * jnp.mean(dxh * xn, axis=-1, keepdims=True))
        dx_ref[...] = add_ref[...] + dx
        part = jnp.sum(dy * xn, axis=0, keepdims=True)

        @pl.when(pl.program_id(0) == 0)
        def _():
            dw_ref[...] = part

        @pl.when(pl.program_id(0) > 0)
        def _():
            dw_ref[...] += part

    row = pl.BlockSpec((tr, d), lambda i: (i, 0))
    vec = pl.BlockSpec((1, d), lambda i: (0, 0))
    return pl.pallas_call(
        body,
        grid=(s // tr,),
        in_specs=[row, row, vec, row],
        out_specs=[row, vec],
        out_shape=[jax.ShapeDtypeStruct((s, d), F32), jax.ShapeDtypeStruct((1, d), F32)],
        compiler_params=_params(("arbitrary",)),
        name="rms_bwd",
    )(dyn, x, w, add)


CONV_ROWS = 256
CONV_COLS = 256
CONV_HALO = 16


def _conv_taps(ext, w, base, rows):
    acc = w[0:1, :] * ext[base:base + rows]
    for k in range(1, CONV_K):
        acc = acc + w[k:k + 1, :] * ext[base + k:base + k + rows]
    return acc


def _ssd_conv_fwd(zx, conv_w, conv_b):
    s = zx.shape[0]
    c = SSD_CONV_DIM
    nsteps = s // CONV_ROWS
    off = SSD_INNER // CONV_COLS

    def body(x_ref, w_ref, b_ref, o_ref):
        w = w_ref[...]
        b = b_ref[...]

        def step(i, carry):
            r0 = pl.multiple_of(i * CONV_ROWS, CONV_ROWS)
            cur = x_ref[pl.ds(r0, CONV_ROWS), :].astype(F32)
            p0 = pl.multiple_of(jnp.maximum(r0 - CONV_HALO, 0), CONV_HALO)
            prev = x_ref[pl.ds(p0, CONV_HALO), :].astype(F32)
            prev = jnp.where(i == 0, 0.0, prev)
            ext = jnp.concatenate([prev, cur], axis=0)
            acc = _conv_taps(ext, w, CONV_HALO - (CONV_K - 1), CONV_ROWS) + b
            o_ref[pl.ds(r0, CONV_ROWS), :] = (acc * _sigmoid(acc)).astype(o_ref.dtype)
            return carry

        lax.fori_loop(0, nsteps, step, 0)

    return pl.pallas_call(
        body,
        grid=(c // CONV_COLS,),
        in_specs=[pl.BlockSpec((s, CONV_COLS), lambda j: (0, j + off)),
                  pl.BlockSpec((8, CONV_COLS), lambda j: (0, j)),
                  pl.BlockSpec((1, CONV_COLS), lambda j: (0, j))],
        out_specs=pl.BlockSpec((s, CONV_COLS), lambda j: (0, j)),
        out_shape=jax.ShapeDtypeStruct((s, c), BF16),
        compiler_params=_params(("parallel",)),
        name="ssd_conv_fwd",
    )(zx, conv_w, conv_b)


def _ssd_conv_bwd(zx, dxbc, conv_w, conv_b, dzx):
    s = zx.shape[0]
    c = SSD_CONV_DIM
    nsteps = s // CONV_ROWS
    off = SSD_INNER // CONV_COLS

    def body(x_ref, dy_ref, w_ref, b_ref, dzx_in_ref, dx_ref, dw_ref, db_ref, dc_ref):
        w = w_ref[...]
        b = b_ref[...]
        dc_ref[pl.ds(s, CONV_HALO), :] = jnp.zeros((CONV_HALO, CONV_COLS), F32)

        def step1(i, carry):
            dw0, dw1, dw2, dw3, dbs = carry
            r0 = pl.multiple_of(i * CONV_ROWS, CONV_ROWS)
            cur = x_ref[pl.ds(r0, CONV_ROWS), :].astype(F32)
            p0 = pl.multiple_of(jnp.maximum(r0 - CONV_HALO, 0), CONV_HALO)
            prev = x_ref[pl.ds(p0, CONV_HALO), :].astype(F32)
            prev = jnp.where(i == 0, 0.0, prev)
            ext = jnp.concatenate([prev, cur], axis=0)
            base = CONV_HALO - (CONV_K - 1)
            acc = _conv_taps(ext, w, base, CONV_ROWS) + b
            sg = _sigmoid(acc)
            dcv = dy_ref[pl.ds(r0, CONV_ROWS), :].astype(F32) * (sg * (1.0 + acc * (1.0 - sg)))
            dc_ref[pl.ds(r0, CONV_ROWS), :] = dcv
            dws = [jnp.sum(dcv * ext[base + k:base + k + CONV_ROWS], axis=0, keepdims=True) for k in range(CONV_K)]
            return (dw0 + dws[0], dw1 + dws[1], dw2 + dws[2], dw3 + dws[3], dbs + jnp.sum(dcv, axis=0, keepdims=True))

        z = jnp.zeros((1, CONV_COLS), F32)
        dw0, dw1, dw2, dw3, dbs = lax.fori_loop(0, nsteps, step1, (z, z, z, z, z))
        dw_ref[...] = jnp.concatenate([dw0, dw1, dw2, dw3, z, z, z, z], axis=0)
        db_ref[...] = dbs

        def step2(i, carry):
            r0 = pl.multiple_of(i * CONV_ROWS, CONV_ROWS)
            ext = dc_ref[pl.ds(r0, CONV_ROWS + CONV_HALO), :]
            acc = w[0:1, :] * ext[CONV_K - 1:CONV_K - 1 + CONV_ROWS]
            for k in range(1, CONV_K):
                acc = acc + w[k:k + 1, :] * ext[CONV_K - 1 - k:CONV_K - 1 - k + CONV_ROWS]
            dx_ref[pl.ds(r0, CONV_ROWS), :] = acc.astype(dx_ref.dtype)
            return carry

        lax.fori_loop(0, nsteps, step2, 0)

    col = pl.BlockSpec((s, CONV_COLS), lambda j: (0, j))
    shifted = pl.BlockSpec((s, CONV_COLS), lambda j: (0, j + off))
    return pl.pallas_call(
        body,
        grid=(c // CONV_COLS,),
        in_specs=[shifted, col,
                  pl.BlockSpec((8, CONV_COLS), lambda j: (0, j)),
                  pl.BlockSpec((1, CONV_COLS), lambda j: (0, j)),
                  pl.BlockSpec(memory_space=pl.ANY)],
        out_specs=[shifted, pl.BlockSpec((8, CONV_COLS), lambda j: (0, j)), pl.BlockSpec((1, CONV_COLS), lambda j: (0, j))],
        out_shape=[jax.ShapeDtypeStruct((s, SSD_ZX), BF16), jax.ShapeDtypeStruct((8, c), F32),
                   jax.ShapeDtypeStruct((1, c), F32)],
        scratch_shapes=[pltpu.VMEM((s + CONV_HALO, CONV_COLS), F32)],
        input_output_aliases={4: 0},
        compiler_params=_params(("parallel",)),
        name="ssd_conv_bwd",
    )(zx, dxbc, conv_w, conv_b, dzx)


def _ssd_consts():
    li = lax.broadcasted_iota(jnp.int32, (CHUNK, CHUNK), 0)
    si = lax.broadcasted_iota(jnp.int32, (CHUNK, CHUNK), 1)
    tril = li >= si
    hrow = lax.broadcasted_iota(jnp.int32, (LANES, SSD_INNER), 0)
    hcol = lax.broadcasted_iota(jnp.int32, (LANES, SSD_INNER), 1) // SSD_HEADDIM
    expand = (hrow == hcol).astype(F32)
    return tril, expand


def _ssd_chunk_common(dtp_ref, bias_ref, alog_ref, tril, expand):
    lane = lax.broadcasted_iota(jnp.int32, (1, LANES), 1)
    valid = lane < SSD_HEADS
    pre = dtp_ref[...] + bias_ref[...]
    dt = jnp.where(valid, jnp.maximum(pre, 0.0) + jnp.log1p(jnp.exp(-jnp.abs(pre))), 0.0)
    a = jnp.where(valid, -jnp.exp(alog_ref[...]), 0.0)
    da = dt * a
    cs = _dot01(tril.astype(F32), da, NN, 1)
    cs_x = _dot01(cs, expand, NN, 0)
    dt_x = _dot01(dt, expand, NN, 0, terms=2)
    return pre, dt, a, cs, cs_x, dt_x


def _ssd_scan_fwd(xbc, dtp, dt_bias, a_log, d_skip):
    s = xbc.shape[0]
    nc = s // CHUNK
    gw = SSD_GROUP_W

    def body(xbc_ref, dtp_ref, bias_ref, alog_ref, d_ref, y_ref, prev_ref, state_ref):
        c = pl.program_id(0)

        @pl.when(c == 0)
        def _():
            state_ref[...] = jnp.zeros_like(state_ref)

        tril, expand = _ssd_consts()
        pre, dt, a, cs, cs_x, dt_x = _ssd_chunk_common(dtp_ref, bias_ref, alog_ref, tril, expand)
        cs_t = cs.T
        d_x = _dot01(jnp.broadcast_to(d_ref[...], (8, LANES)), expand, NN, 0)[0:1, :]
        cs_last = cs_x[CHUNK - 1:CHUNK, :]
        dec_out = jnp.exp(cs_x)
        dec_st = jnp.exp(cs_last - cs_x)
        dec_ch = jnp.exp(cs_last)
        x = xbc_ref[:, 0:SSD_INNER].astype(F32)
        xr = x * dt_x
        xrs = xr * dec_st
        lane_g = lax.broadcasted_iota(jnp.int32, (1, gw), 1) // SSD_HEADDIM
        for g in range(SSD_GROUPS):
            sl = slice(g * gw, (g + 1) * gw)
            bg = xbc_ref[:, SSD_INNER + g * SSD_STATE:SSD_INNER + (g + 1) * SSD_STATE]
            cg = xbc_ref[:, SSD_INNER + (SSD_GROUPS + g) * SSD_STATE:SSD_INNER + (SSD_GROUPS + g + 1) * SSD_STATE]
            cb = _dot(cg, bg, NT)
            prev_g = state_ref[:, sl]
            prev_ref[0, :, sl] = prev_g
            yo = _dot(cg, prev_g.astype(BF16), NN) * dec_out[:, sl]
            xr_g = xr[:, sl]
            yd = jnp.zeros((CHUNK, gw), F32)
            for r in range(SSD_HEADS // SSD_GROUPS):
                h = g * (SSD_HEADS // SSD_GROUPS) + r
                diff = cs[:, h:h + 1] - cs_t[h:h + 1, :]
                lmat = jnp.exp(jnp.where(tril, diff, -1e30))
                wmat = (cb * lmat).astype(BF16)
                xr_h = jnp.where(lane_g == r, xr_g, 0.0).astype(BF16)
                yd = yd + _dot(wmat, xr_h, NN)
            y_ref[:, sl] = yd + yo + x[:, sl] * d_x[:, sl]
            sc = _dot(bg, xrs[:, sl].astype(BF16), TN)
            state_ref[:, sl] = prev_g * dec_ch[:, sl] + sc

    vec = pl.BlockSpec((1, LANES), lambda c: (0, 0))
    return pl.pallas_call(
        body,
        grid=(nc,),
        in_specs=[pl.BlockSpec((CHUNK, SSD_CONV_DIM), lambda c: (c, 0)),
                  pl.BlockSpec((CHUNK, LANES), lambda c: (c, 0)), vec, vec, vec],
        out_specs=[pl.BlockSpec((CHUNK, SSD_INNER), lambda c: (c, 0)),
                   pl.BlockSpec((1, SSD_STATE, SSD_INNER), lambda c: (c, 0, 0))],
        out_shape=[jax.ShapeDtypeStruct((s, SSD_INNER), F32), jax.ShapeDtypeStruct((nc, SSD_STATE, SSD_INNER), F32)],
        scratch_shapes=[pltpu.VMEM((SSD_STATE, SSD_INNER), F32)],
        compiler_params=_params(("arbitrary",)),
        name="ssd_scan_fwd",
    )(xbc, dtp, dt_bias, a_log, d_skip)


def _ssd_scan_bwd(xbc, dtp, prev, dy, dt_bias, a_log, d_skip):
    s = xbc.shape[0]
    nc = s // CHUNK
    gw = SSD_GROUP_W
    hpg = SSD_HEADS // SSD_GROUPS

    def body(xbc_ref, dtp_ref, prev_ref, dy_ref, bias_ref, alog_ref, d_ref,
             dxbc_ref, ddtp_ref, dbias_ref, dalog_ref, dd_ref, dp_ref, ddx_ref):
        step = pl.program_id(0)

        @pl.when(step == 0)
        def _():
            dp_ref[...] = jnp.zeros_like(dp_ref)
            ddx_ref[...] = jnp.zeros_like(ddx_ref)
            dbias_ref[...] = jnp.zeros_like(dbias_ref)
            dalog_ref[...] = jnp.zeros_like(dalog_ref)

        tril, expand = _ssd_consts()
        pre, dt, a, cs, cs_x, dt_x = _ssd_chunk_common(dtp_ref, bias_ref, alog_ref, tril, expand)
        cs_t = cs.T
        d_x = _dot01(jnp.broadcast_to(d_ref[...], (8, LANES)), expand, NN, 0)[0:1, :]
        cs_last = cs_x[CHUNK - 1:CHUNK, :]
        dec_out = jnp.exp(cs_x)
        dec_st = jnp.exp(cs_last - cs_x)
        dec_ch = jnp.exp(cs_last)
        x = xbc_ref[:, 0:SSD_INNER].astype(F32)
        dyv = dy_ref[...]
        xr = x * dt_x
        xrs = xr * dec_st
        lane_g = lax.broadcasted_iota(jnp.int32, (1, gw), 1) // SSD_HEADDIM
        hsel = lax.broadcasted_iota(jnp.int32, (CHUNK, LANES), 1)
        dcs = jnp.zeros((CHUNK, LANES), F32)
        last_parts = []
        t_parts = []
        dxr_parts = []
        for g in range(SSD_GROUPS):
            sl = slice(g * gw, (g + 1) * gw)
            bsl = slice(SSD_INNER + g * SSD_STATE, SSD_INNER + (g + 1) * SSD_STATE)
            csl = slice(SSD_INNER + (SSD_GROUPS + g) * SSD_STATE, SSD_INNER + (SSD_GROUPS + g + 1) * SSD_STATE)
            bg = xbc_ref[:, bsl]
            cg = xbc_ref[:, csl]
            cb = _dot(cg, bg, NT)
            prev_g = prev_ref[0, :, sl]
            prev_b = prev_g.astype(BF16)
            dp_g = dp_ref[:, sl]
            dp_b = dp_g.astype(BF16)
            dy_g = dyv[:, sl]
            xr_g = xr[:, sl]
            gmat = _dot(cg, prev_b, NN)
            dgm = (dy_g * dec_out[:, sl]).astype(BF16)
            dc_g = _dot(dgm, prev_b, NT)
            dprev = _dot(cg, dgm, TN)
            t1 = dy_g * gmat * dec_out[:, sl]
            mm_ = _dot(bg, dp_b, NN)
            db_g = _dot(xrs[:, sl].astype(BF16), dp_b, NT)
            dxr_g = mm_ * dec_st[:, sl]
            t2 = dxr_g * xr_g
            last = jnp.sum(t2, axis=0, keepdims=True) + jnp.sum(dp_g * prev_g, axis=0, keepdims=True) * dec_ch[:, sl]
            dp_ref[:, sl] = dp_g * dec_ch[:, sl] + dprev
            dcb = jnp.zeros((CHUNK, CHUNK), F32)
            for r in range(hpg):
                h = g * hpg + r
                diff = cs[:, h:h + 1] - cs_t[h:h + 1, :]
                lmat = jnp.exp(jnp.where(tril, diff, -1e30))
                wmat = cb * lmat
                dy_h = jnp.where(lane_g == r, dy_g, 0.0).astype(BF16)
                dw = _dot(dy_h, xr_g.astype(BF16), NT)
                dxr_g = dxr_g + _dot(wmat.astype(BF16), dy_h, TN)
                dcb = dcb + dw * lmat
                q = (dw * wmat).astype(BF16)
                onehot = (hsel == h).astype(BF16)
                dcs = dcs + _dot(q, onehot, NN) - _dot(q, onehot, TN)
            dcb_b = dcb.astype(BF16)
            dc_g = dc_g + _dot(dcb_b, bg, NN)
            db_g = db_g + _dot(dcb_b, cg, TN)
            dxbc_ref[:, bsl] = db_g.astype(dxbc_ref.dtype)
            dxbc_ref[:, csl] = dc_g.astype(dxbc_ref.dtype)
            t_parts.append(t1 - t2)
            last_parts.append(last)
            dxr_parts.append(dxr_g)
        dxr = jnp.concatenate(dxr_parts, axis=1)
        tt = jnp.concatenate(t_parts, axis=1)
        last_x = jnp.concatenate(last_parts, axis=1)
        dxbc_ref[:, 0:SSD_INNER] = (dxr * dt_x + dyv * d_x).astype(dxbc_ref.dtype)
        dcs = dcs + _dot01(tt, expand, NT, 0, terms=2)
        last_h = _dot01(jnp.broadcast_to(last_x, (8, SSD_INNER)), expand, NT, 0)[0:1, :]
        rowi = lax.broadcasted_iota(jnp.int32, (CHUNK, LANES), 0)
        dcs = dcs + jnp.where(rowi == CHUNK - 1, last_h, 0.0)
        dda = _dot01(tril.astype(F32), dcs, TN, 1)
        ddt = dda * a + _dot01(dxr * x, expand, NT, 0, terms=2)
        dpre = ddt * _sigmoid(pre)
        ddtp_ref[...] = dpre
        dbias_ref[...] += jnp.sum(dpre, axis=0, keepdims=True)
        dalog_ref[...] += jnp.sum(dda * dt, axis=0, keepdims=True) * a
        ddx_ref[...] += jnp.broadcast_to(jnp.sum(dyv * x, axis=0, keepdims=True), (8, SSD_INNER))

        @pl.when(step == nc - 1)
        def _():
            dd_ref[...] = _dot01(ddx_ref[...], expand, NT, 0)[0:1, :]

    rev = lambda c: (nc - 1 - c, 0)
    vec = pl.BlockSpec((1, LANES), lambda c: (0, 0))
    return pl.pallas_call(
        body,
        grid=(nc,),
        in_specs=[pl.BlockSpec((CHUNK, SSD_CONV_DIM), rev), pl.BlockSpec((CHUNK, LANES), rev),
                  pl.BlockSpec((1, SSD_STATE, SSD_INNER), lambda c: (nc - 1 - c, 0, 0)),
                  pl.BlockSpec((CHUNK, SSD_INNER), rev), vec, vec, vec],
        out_specs=[pl.BlockSpec((CHUNK, SSD_CONV_DIM), rev), pl.BlockSpec((CHUNK, LANES), rev), vec, vec, vec],
        out_shape=[jax.ShapeDtypeStruct((s, SSD_CONV_DIM), BF16), jax.ShapeDtypeStruct((s, LANES), F32),
                   jax.ShapeDtypeStruct((1, LANES), F32), jax.ShapeDtypeStruct((1, LANES), F32),
                   jax.ShapeDtypeStruct((1, LANES), F32)],
        scratch_shapes=[pltpu.VMEM((SSD_STATE, SSD_INNER), F32), pltpu.VMEM((8, SSD_INNER), F32)],
        compiler_params=_params(("arbitrary",)),
        name="ssd_scan_bwd",
    )(xbc, dtp, prev, dy, dt_bias, a_log, d_skip)


def _ssd_gate_fwd(y, zx, norm_w):
    s = y.shape[0]
    tr = _pick(s, (256, 128))
    gw = SSD_GROUP_W

    def body(y_ref, z_ref, w_ref, o_ref):
        for g in range(SSD_GROUPS):
            sl = slice(g * gw, (g + 1) * gw)
            z = z_ref[:, sl].astype(F32)
            gv = y_ref[:, sl] * (z * _sigmoid(z))
            r = lax.rsqrt(jnp.mean(gv * gv, axis=-1, keepdims=True) + LN_EPS)
            o_ref[:, sl] = (gv * r * w_ref[:, sl]).astype(o_ref.dtype)

    row = pl.BlockSpec((tr, SSD_INNER), lambda i: (i, 0))
    return pl.pallas_call(
        body,
        grid=(s // tr,),
        in_specs=[row, row, pl.BlockSpec((1, SSD_INNER), lambda i: (0, 0))],
        out_specs=row,
        out_shape=jax.ShapeDtypeStruct((s, SSD_INNER), BF16),
        compiler_params=_params(("parallel",)),
        name="ssd_gate_fwd",
    )(y, zx, norm_w)


def _ssd_gate_bwd(dgn, y, zx, norm_w):
    s = y.shape[0]
    tr = _pick(s, (256, 128))
    gw = SSD_GROUP_W

    def body(dg_ref, y_ref, z_ref, w_ref, dy_ref, dz_ref, dw_ref):
        parts = []
        for g in range(SSD_GROUPS):
            sl = slice(g * gw, (g + 1) * gw)
            z = z_ref[:, sl].astype(F32)
            yv = y_ref[:, sl]
            sg = _sigmoid(z)
            sz = z * sg
            gv = yv * sz
            r = lax.rsqrt(jnp.mean(gv * gv, axis=-1, keepdims=True) + LN_EPS)
            gn = gv * r
            dout = dg_ref[:, sl].astype(F32)
            parts.append(jnp.sum(dout * gn, axis=0, keepdims=True))
            dgn_ = dout * w_ref[:, sl]
            dgv = r * (dgn_ - gn * jnp.mean(dgn_ * gn, axis=-1, keepdims=True))
            dy_ref[:, sl] = dgv * sz
            dz_ref[:, sl] = (dgv * yv * (sg * (1.0 + z * (1.0 - sg)))).astype(dz_ref.dtype)
        part = jnp.concatenate(parts, axis=1)

        @pl.when(pl.program_id(0) == 0)
        def _():
            dw_ref[...] = part

        @pl.when(pl.program_id(0) > 0)
        def _():
            dw_ref[...] += part

    row = pl.BlockSpec((tr, SSD_INNER), lambda i: (i, 0))
    vec = pl.BlockSpec((1, SSD_INNER), lambda i: (0, 0))
    return pl.pallas_call(
        body,
        grid=(s // tr,),
        in_specs=[row, row, row, vec],
        out_specs=[row, row, vec],
        out_shape=[jax.ShapeDtypeStruct((s, SSD_INNER), F32), jax.ShapeDtypeStruct((s, SSD_ZX), BF16),
                   jax.ShapeDtypeStruct((1, SSD_INNER), F32)],
        compiler_params=_params(("arbitrary",)),
        name="ssd_gate_bwd",
    )(dgn, y, zx, norm_w)


INV_SQRT2 = 1.0 / math.sqrt(2.0)
INV_SQRT2PI = 1.0 / math.sqrt(2.0 * math.pi)


def _gelu(x):
    return 0.5 * x * (1.0 + lax.erf(x * INV_SQRT2))


def _gelu_grad(x):
    return 0.5 * (1.0 + lax.erf(x * INV_SQRT2)) + x * INV_SQRT2PI * jnp.exp(-0.5 * x * x)


def _gmlp_act_fwd(pre, b_in, ln_w, ln_b):
    s = pre.shape[0]
    tr = _pick(s, (256, 128))
    n = GMLP_INNER

    def body(p_ref, b_ref, w_ref, lb_ref, u_ref, v_ref):
        u_ref[...] = _gelu(p_ref[:, 0:n].astype(F32) + b_ref[:, 0:n]).astype(u_ref.dtype)
        hv = _gelu(p_ref[:, n:2 * n].astype(F32) + b_ref[:, n:2 * n])
        mu = jnp.mean(hv, axis=-1, keepdims=True)
        xc = hv - mu
        r = lax.rsqrt(jnp.mean(xc * xc, axis=-1, keepdims=True) + LN_EPS)
        v_ref[...] = (xc * r * w_ref[...] + lb_ref[...]).astype(v_ref.dtype)

    half = pl.BlockSpec((tr, n), lambda i: (i, 0))
    vec = pl.BlockSpec((1, n), lambda i: (0, 0))
    return pl.pallas_call(
        body,
        grid=(s // tr,),
        in_specs=[pl.BlockSpec((tr, 2 * n), lambda i: (i, 0)), pl.BlockSpec((1, 2 * n), lambda i: (0, 0)), vec, vec],
        out_specs=[half, half],
        out_shape=[jax.ShapeDtypeStruct((s, n), BF16), jax.ShapeDtypeStruct((s, n), BF16)],
        compiler_params=_params(("parallel",)),
        name="gmlp_act_fwd",
    )(pre, b_in, ln_w, ln_b)


def _gmlp_act_bwd(pre, b_in, ln_w, du, dv):
    s = pre.shape[0]
    tr = _pick(s, (256, 128))
    n = GMLP_INNER

    def body(p_ref, b_ref, w_ref, du_ref, dv_ref, dp_ref, db_ref, dw_ref, dlb_ref):
        xu = p_ref[:, 0:n].astype(F32) + b_ref[:, 0:n]
        dpu = du_ref[...].astype(F32) * _gelu_grad(xu)
        xv = p_ref[:, n:2 * n].astype(F32) + b_ref[:, n:2 * n]
        hv = _gelu(xv)
        mu = jnp.mean(hv, axis=-1, keepdims=True)
        xc = hv - mu
        r = lax.rsqrt(jnp.mean(xc * xc, axis=-1, keepdims=True) + LN_EPS)
        vh = xc * r
        dvv = dv_ref[...].astype(F32)
        dvh = dvv * w_ref[...]
        dh = r * (dvh - jnp.mean(dvh, axis=-1, keepdims=True) - vh * jnp.mean(dvh * vh, axis=-1, keepdims=True))
        dpv = dh * _gelu_grad(xv)
        dp_ref[:, 0:n] = dpu.astype(dp_ref.dtype)
        dp_ref[:, n:2 * n] = dpv.astype(dp_ref.dtype)
        pb = jnp.concatenate([jnp.sum(dpu, axis=0, keepdims=True), jnp.sum(dpv, axis=0, keepdims=True)], axis=1)
        pw = jnp.sum(dvv * vh, axis=0, keepdims=True)
        plb = jnp.sum(dvv, axis=0, keepdims=True)

        @pl.when(pl.program_id(0) == 0)
        def _():
            db_ref[...] = pb
            dw_ref[...] = pw
            dlb_ref[...] = plb

        @pl.when(pl.program_id(0) > 0)
        def _():
            db_ref[...] += pb
            dw_ref[...] += pw
            dlb_ref[...] += plb

    half = pl.BlockSpec((tr, n), lambda i: (i, 0))
    full = pl.BlockSpec((tr, 2 * n), lambda i: (i, 0))
    vec = pl.BlockSpec((1, n), lambda i: (0, 0))
    vec2 = pl.BlockSpec((1, 2 * n), lambda i: (0, 0))
    return pl.pallas_call(
        body,
        grid=(s // tr,),
        in_specs=[full, vec2, vec, half, half],
        out_specs=[full, vec2, vec, vec],
        out_shape=[jax.ShapeDtypeStruct((s, 2 * n), BF16), jax.ShapeDtypeStruct((1, 2 * n), F32),
                   jax.ShapeDtypeStruct((1, n), F32), jax.ShapeDtypeStruct((1, n), F32)],
        compiler_params=_params(("arbitrary",)),
        name="gmlp_act_bwd",
    )(pre, b_in, ln_w, du, dv)


def _gmlp_mix_fwd(u, v, w_s, b_st):
    s = u.shape[0]
    gd = GMLP_INNER // GMLP_GROUPS

    def body(u_ref, v_ref, w_ref, b_ref, o_ref):
        li = lax.broadcasted_iota(jnp.int32, (CHUNK, CHUNK), 0)
        si = lax.broadcasted_iota(jnp.int32, (CHUNK, CHUNK), 1)
        tril = li >= si
        for g in range(GMLP_GROUPS):
            sl = slice(g * gd, (g + 1) * gd)
            wm = jnp.where(tril, w_ref[g], 0.0).astype(BF16)
            mixed = _dot(wm, v_ref[:, sl], NN) + b_ref[:, g:g + 1]
            o_ref[:, sl] = (u_ref[:, sl].astype(F32) * mixed).astype(o_ref.dtype)

    row = pl.BlockSpec((CHUNK, GMLP_INNER), lambda c: (c, 0))
    return pl.pallas_call(
        body,
        grid=(s // CHUNK,),
        in_specs=[row, row, pl.BlockSpec((GMLP_GROUPS, CHUNK, CHUNK), lambda c: (0, 0, 0)),
                  pl.BlockSpec((CHUNK, LANES), lambda c: (0, 0))],
        out_specs=row,
        out_shape=jax.ShapeDtypeStruct((s, GMLP_INNER), BF16),
        compiler_params=_params(("parallel",)),
        name="gmlp_mix_fwd",
    )(u, v, w_s, b_st)


def _gmlp_mix_bwd(dgated, u, v, w_s, b_st):
    s = u.shape[0]
    nc = s // CHUNK
    gd = GMLP_INNER // GMLP_GROUPS

    def body(dg_ref, u_ref, v_ref, w_ref, b_ref, du_ref, dv_ref, dw_ref, db_ref):
        c = pl.program_id(0)

        @pl.when(c == 0)
        def _():
            dw_ref[...] = jnp.zeros_like(dw_ref)
            db_ref[...] = jnp.zeros_like(db_ref)

        li = lax.broadcasted_iota(jnp.int32, (CHUNK, CHUNK), 0)
        si = lax.broadcasted_iota(jnp.int32, (CHUNK, CHUNK), 1)
        tril = li >= si
        lane = lax.broadcasted_iota(jnp.int32, (CHUNK, LANES), 1)
        dbacc = jnp.zeros((CHUNK, LANES), F32)
        for g in range(GMLP_GROUPS):
            sl = slice(g * gd, (g + 1) * gd)
            wm = jnp.where(tril, w_ref[g], 0.0).astype(BF16)
            vg = v_ref[:, sl]
            mixed = _dot(wm, vg, NN) + b_ref[:, g:g + 1]
            dgv = dg_ref[:, sl].astype(F32)
            du_ref[:, sl] = (dgv * mixed).astype(du_ref.dtype)
            dm = dgv * u_ref[:, sl].astype(F32)
            dm_b = dm.astype(BF16)
            dv_ref[:, sl] = _dot(wm, dm_b, TN).astype(dv_ref.dtype)
            dw_ref[g] += jnp.where(tril, _dot(dm_b, vg, NT), 0.0)
            dbacc = dbacc + jnp.where(lane == g, jnp.sum(dm, axis=1, keepdims=True), 0.0)
        db_ref[...] += dbacc

    row = pl.BlockSpec((CHUNK, GMLP_INNER), lambda c: (c, 0))
    wspec = pl.BlockSpec((GMLP_GROUPS, CHUNK, CHUNK), lambda c: (0, 0, 0))
    bspec = pl.BlockSpec((CHUNK, LANES), lambda c: (0, 0))
    return pl.pallas_call(
        body,
        grid=(nc,),
        in_specs=[row, row, row, wspec, bspec],
        out_specs=[row, row, wspec, bspec],
        out_shape=[jax.ShapeDtypeStruct((s, GMLP_INNER), BF16), jax.ShapeDtypeStruct((s, GMLP_INNER), BF16),
                   jax.ShapeDtypeStruct((GMLP_GROUPS, CHUNK, CHUNK), F32), jax.ShapeDtypeStruct((CHUNK, LANES), F32)],
        compiler_params=_params(("arbitrary",)),
        name="gmlp_mix_bwd",
    )(dgated, u, v, w_s, b_st)


def _swiglu_fwd(gu):
    s = gu.shape[0]
    f = FFN_DIM
    tr = _pick(s, (512, 256, 128))

    def body(gu_ref, o_ref):
        gt = gu_ref[:, 0:f].astype(F32)
        o_ref[...] = (gt * _sigmoid(gt) * gu_ref[:, f:2 * f].astype(F32)).astype(o_ref.dtype)

    return pl.pallas_call(
        body,
        grid=(s // tr,),
        in_specs=[pl.BlockSpec((tr, 2 * f), lambda i: (i, 0))],
        out_specs=pl.BlockSpec((tr, f), lambda i: (i, 0)),
        out_shape=jax.ShapeDtypeStruct((s, f), BF16),
        compiler_params=_params(("parallel",)),
        name="swiglu_fwd",
    )(gu)


def _swiglu_bwd(gu, dhid):
    s = gu.shape[0]
    f = FFN_DIM
    tr = _pick(s, (512, 256, 128))

    def body(gu_ref, dh_ref, dgu_ref):
        gt = gu_ref[:, 0:f].astype(F32)
        up = gu_ref[:, f:2 * f].astype(F32)
        dh = dh_ref[...].astype(F32)
        sg = _sigmoid(gt)
        dgu_ref[:, 0:f] = (dh * up * (sg * (1.0 + gt * (1.0 - sg)))).astype(dgu_ref.dtype)
        dgu_ref[:, f:2 * f] = (dh * gt * sg).astype(dgu_ref.dtype)

    wide = pl.BlockSpec((tr, 2 * f), lambda i: (i, 0))
    return pl.pallas_call(
        body,
        grid=(s // tr,),
        in_specs=[wide, pl.BlockSpec((tr, f), lambda i: (i, 0))],
        out_specs=wide,
        out_shape=jax.ShapeDtypeStruct((s, 2 * f), BF16),
        compiler_params=_params(("parallel",)),
        name="swiglu_bwd",
    )(gu, dhid)


def _ple_fwd(pe, gl, h, ple_norm):
    s, d = h.shape
    tr = _pick(s, (512, 256, 128))

    def body(pe_ref, gl_ref, h_ref, w_ref, o_ref):
        pe_ = pe_ref[...].astype(F32)
        r = lax.rsqrt(jnp.mean(pe_ * pe_, axis=-1, keepdims=True) + RMS_EPS)
        o_ref[...] = h_ref[...] + _sigmoid(gl_ref[...].astype(F32)) * (pe_ * r * w_ref[...])

    row = pl.BlockSpec((tr, d), lambda i: (i, 0))
    return pl.pallas_call(
        body,
        grid=(s // tr,),
        in_specs=[row, row, row, pl.BlockSpec((1, d), lambda i: (0, 0))],
        out_specs=row,
        out_shape=jax.ShapeDtypeStruct((s, d), F32),
        compiler_params=_params(("parallel",)),
        name="ple_fwd",
    )(pe, gl, h, ple_norm)


def _ple_bwd(dh, pe, gl, ple_norm):
    s, d = dh.shape
    tr = _pick(s, (512, 256, 128))

    def body(dh_ref, pe_ref, gl_ref, w_ref, dgl_ref, dpe_ref, dw_ref):
        pe_ = pe_ref[...].astype(F32)
        dhv = dh_ref[...]
        r = lax.rsqrt(jnp.mean(pe_ * pe_, axis=-1, keepdims=True) + RMS_EPS)
        pn = pe_ * r
        gate = _sigmoid(gl_ref[...].astype(F32))
        dgl_ref[...] = (dhv * (pn * w_ref[...]) * gate * (1.0 - gate)).astype(dgl_ref.dtype)
        de = dhv * gate
        dxh = de * w_ref[...]
        dpe_ref[...] = (r * (dxh - pn * jnp.mean(dxh * pn, axis=-1, keepdims=True))).astype(dpe_ref.dtype)
        part = jnp.sum(de * pn, axis=0, keepdims=True)

        @pl.when(pl.program_id(0) == 0)
        def _():
            dw_ref[...] = part

        @pl.when(pl.program_id(0) > 0)
        def _():
            dw_ref[...] += part

    row = pl.BlockSpec((tr, d), lambda i: (i, 0))
    vec = pl.BlockSpec((1, d), lambda i: (0, 0))
    return pl.pallas_call(
        body,
        grid=(s // tr,),
        in_specs=[row, row, row, vec],
        out_specs=[row, row, vec],
        out_shape=[jax.ShapeDtypeStruct((s, d), BF16), jax.ShapeDtypeStruct((s, d), BF16),
                   jax.ShapeDtypeStruct((1, d), F32)],
        compiler_params=_params(("arbitrary",)),
        name="ple_bwd",
    )(dh, pe, gl, ple_norm)


def _loss_head(h, w, target):
    s, d = h.shape
    tr = _pick(s, (512, 256, 128))

    def body(h_ref, w_ref, t_ref, l_ref, dh_ref, dw_ref):
        hv = h_ref[...]
        r = lax.rsqrt(jnp.mean(hv * hv, axis=-1, keepdims=True) + RMS_EPS)
        hn = hv * r
        diff = hn * w_ref[...] - t_ref[...]
        lpart = jnp.zeros((8, LANES), F32) + (0.5 / d) * jnp.sum(jnp.sum(diff * diff, axis=1, keepdims=True), axis=0, keepdims=True)
        dy = diff * (1.0 / d)
        dxh = dy * w_ref[...]
        dh_ref[...] = r * (dxh - hn * jnp.mean(dxh * hn, axis=-1, keepdims=True))
        part = jnp.sum(dy * hn, axis=0, keepdims=True)

        @pl.when(pl.program_id(0) == 0)
        def _():
            l_ref[...] = lpart
            dw_ref[...] = part

        @pl.when(pl.program_id(0) > 0)
        def _():
            l_ref[...] += lpart
            dw_ref[...] += part

    row = pl.BlockSpec((tr, d), lambda i: (i, 0))
    vec = pl.BlockSpec((1, d), lambda i: (0, 0))
    return pl.pallas_call(
        body,
        grid=(s // tr,),
        in_specs=[row, vec, row],
        out_specs=[pl.BlockSpec((8, LANES), lambda i: (0, 0)), row, vec],
        out_shape=[jax.ShapeDtypeStruct((8, LANES), F32), jax.ShapeDtypeStruct((s, d), F32),
                   jax.ShapeDtypeStruct((1, d), F32)],
        compiler_params=_params(("arbitrary",)),
        name="loss_head",
    )(h, w, target)


PER_LAYER = ("norm_mix", "norm_ffn", "ffn_w_gu", "ffn_w_down", "ple_w_proj", "ple_norm", "ple_gate_norm", "ple_w_gate")


def _pad_lanes(v):
    return jnp.pad(v.astype(F32), (0, LANES - v.shape[0]))[None, :]


def _kernel_layouts(full):
    w = {}
    for k in ("norm_mix", "norm_ffn", "ple_norm", "ple_gate_norm", "ssd_conv_b", "ssd_norm_w", "gmlp_b_in", "gmlp_ln_w",
              "gmlp_ln_b", "gmlp_w_s"):
        w[k] = [full[k][i].astype(F32) for i in range(full[k].shape[0])]
    w["final_norm"] = full["final_norm"].astype(F32)
    n_ssd = full["ssd_w_out"].shape[0]
    if "ssd_w_in" in full:
        w["ssd_w_zx"] = [full["ssd_w_in"][j][:, :SSD_ZX].astype(BF16) for j in range(n_ssd)]
        w["ssd_w_dt"] = [jnp.pad(full["ssd_w_in"][j][:, SSD_ZX:].astype(BF16), ((0, 0), (0, LANES - SSD_HEADS)))
                         for j in range(n_ssd)]
        w["ffn_w_gu"] = [jnp.concatenate([full["ffn_w_gate"][i], full["ffn_w_up"][i]], axis=1).astype(BF16)
                         for i in range(DEPTH)]
    else:
        for k in ("ssd_w_zx", "ssd_w_dt", "ffn_w_gu"):
            w[k] = full[k]
    w["ssd_conv_w"] = [jnp.pad(full["ssd_conv_w"][j].astype(F32), ((0, 8 - CONV_K), (0, 0))) for j in range(n_ssd)]
    for k in ("ssd_dt_bias", "ssd_a_log", "ssd_d"):
        w[k] = [_pad_lanes(full[k][j]) for j in range(n_ssd)]
    w["ssd_w_out"] = [full["ssd_w_out"][j].astype(BF16) for j in range(n_ssd)]
    n_g = full["gmlp_w_in"].shape[0]
    w["gmlp_w_in"] = [full["gmlp_w_in"][j].astype(BF16) for j in range(n_g)]
    w["gmlp_w_out"] = [full["gmlp_w_out"][j].astype(BF16) for j in range(n_g)]
    w["gmlp_b_st"] = [jnp.pad(full["gmlp_b_s"][j].astype(F32).T, ((0, 0), (0, LANES - GMLP_GROUPS))) for j in range(n_g)]
    w["ffn_w_down"] =[full["ffn_w_down"][i].astype(BF16) for i in range(DEPTH)]
    w["ple_w_proj"] = [full["ple_w_proj"][i].astype(BF16) for i in range(DEPTH)]
    w["ple_w_gate"] = [full["ple_w_gate"][i].astype(BF16) for i in range(DEPTH)]
    return w


MATRICES = ("ssd_w_out", "gmlp_w_in", "gmlp_w_out", "ffn_w_down", "ple_w_proj", "ple_w_gate")


def _reference_layouts(g, wide=True):
    out = {}
    for k in ("norm_mix", "norm_ffn", "ple_norm", "ple_gate_norm", "ssd_conv_b", "ssd_norm_w", "gmlp_b_in", "gmlp_ln_w",
              "gmlp_ln_b", "gmlp_w_s", "ssd_conv_w", "ssd_dt_bias", "ssd_a_log", "ssd_d") + (MATRICES if wide else ()):
        out[k] = jnp.stack(g[k])
    out["final_norm"] = g["final_norm"]
    out["gmlp_b_s"] = jnp.stack([b[:, :GMLP_GROUPS].T for b in g["gmlp_b_st"]])
    if wide:
        out["ssd_w_in"] = jnp.stack([jnp.concatenate([zx, dt[:, :SSD_HEADS]], axis=1)
                                     for zx, dt in zip(g["ssd_w_zx"], g["ssd_w_dt"])])
        out["ffn_w_gate"] = jnp.stack([gu[:, :FFN_DIM] for gu in g["ffn_w_gu"]])
        out["ffn_w_up"] = jnp.stack([gu[:, FFN_DIM:] for gu in g["ffn_w_gu"]])
    return out


RELAYOUT_ROWS = 128
SSD_SHARD = SSD_IN_DIM // N_DEV
FFN_SHARD = FFN_DIM // N_DEV


def _to_bf16(x):
    nl, rows, n = x.shape

    def body(x_ref, o_ref):
        o_ref[...] = x_ref[...].astype(o_ref.dtype)

    blk = pl.BlockSpec((1, rows, n), lambda i: (i, 0, 0))
    return pl.pallas_call(
        body, grid=(nl,), in_specs=[blk], out_specs=blk, out_shape=jax.ShapeDtypeStruct(x.shape, BF16),
        compiler_params=_params(("parallel",)), name="to_bf16",
    )(x)


def _cat_ssd_in(gathered):
    _, nl, rows, n = gathered.shape
    tr = RELAYOUT_ROWS

    def body(g_ref, *o_refs):
        for j in range(nl):
            full = jnp.concatenate([g_ref[d, j] for d in range(N_DEV)], axis=1)
            o_refs[2 * j][...] = full[:, :SSD_ZX]
            o_refs[2 * j + 1][...] = jnp.concatenate(
                [full[:, SSD_ZX:], jnp.zeros((tr, LANES - SSD_HEADS), full.dtype)], axis=1)

    outs = pl.pallas_call(
        body, grid=(rows // tr,),
        in_specs=[pl.BlockSpec((N_DEV, nl, tr, n), lambda i: (0, 0, i, 0))],
        out_specs=[pl.BlockSpec((tr, SSD_ZX), lambda i: (i, 0)), pl.BlockSpec((tr, LANES), lambda i: (i, 0))] * nl,
        out_shape=[jax.ShapeDtypeStruct((rows, SSD_ZX), BF16), jax.ShapeDtypeStruct((rows, LANES), BF16)] * nl,
        compiler_params=_params(("parallel",)), name="cat_ssd_in",
    )(gathered)
    return [outs[2 * j] for j in range(nl)], [outs[2 * j + 1] for j in range(nl)]


def _split_ssd_in(dzx_list, ddt_list):
    nl = len(dzx_list)
    rows = dzx_list[0].shape[0]
    tr = RELAYOUT_ROWS

    def body(*refs):
        o_ref = refs[2 * nl]
        for j in range(nl):
            full = jnp.concatenate([refs[2 * j][...], refs[2 * j + 1][:, 0:SSD_HEADS]], axis=1)
            for d in range(N_DEV):
                o_ref[d, j] = full[:, d * SSD_SHARD:(d + 1) * SSD_SHARD].astype(o_ref.dtype)

    ins = []
    for j in range(nl):
        ins += [dzx_list[j], ddt_list[j]]
    return pl.pallas_call(
        body, grid=(rows // tr,),
        in_specs=[pl.BlockSpec((tr, SSD_ZX), lambda i: (i, 0)), pl.BlockSpec((tr, LANES), lambda i: (i, 0))] * nl,
        out_specs=pl.BlockSpec((N_DEV, nl, tr, SSD_SHARD), lambda i: (0, 0, i, 0)),
        out_shape=jax.ShapeDtypeStruct((N_DEV, nl, rows, SSD_SHARD), BF16),
        compiler_params=_params(("parallel",)), name="split_ssd_in",
    )(*ins)


def _cat_ffn(g_gate, g_up):
    _, nl, rows, n = g_gate.shape
    tr = RELAYOUT_ROWS

    def body(gg_ref, gu_ref, *o_refs):
        for i in range(nl):
            o_refs[i][...] = jnp.concatenate([gg_ref[d, i] for d in range(N_DEV)] + [gu_ref[d, i] for d in range(N_DEV)],
                                             axis=1)

    blk = pl.BlockSpec((N_DEV, nl, tr, n), lambda i: (0, 0, i, 0))
    outs = pl.pallas_call(
        body, grid=(rows // tr,), in_specs=[blk, blk],
        out_specs=[pl.BlockSpec((tr, 2 * FFN_DIM), lambda i: (i, 0))] * nl,
        out_shape=[jax.ShapeDtypeStruct((rows, 2 * FFN_DIM), BF16)] * nl,
        compiler_params=_params(("parallel",)), name="cat_ffn",
    )(g_gate, g_up)
    return list(outs)


def _split_ffn(dgu_list):
    nl = len(dgu_list)
    rows = dgu_list[0].shape[0]
    tr = RELAYOUT_ROWS

    def body(*refs):
        og_ref, ou_ref = refs[nl], refs[nl + 1]
        for i in range(nl):
            full = refs[i][...]
            for d in range(N_DEV):
                og_ref[d, i] = full[:, d * FFN_SHARD:(d + 1) * FFN_SHARD].astype(og_ref.dtype)
                ou_ref[d, i] = full[:, FFN_DIM + d * FFN_SHARD:FFN_DIM + (d + 1) * FFN_SHARD].astype(ou_ref.dtype)

    blk = pl.BlockSpec((N_DEV, nl, tr, FFN_SHARD), lambda i: (0, 0, i, 0))
    sds = jax.ShapeDtypeStruct((N_DEV, nl, rows, FFN_SHARD), BF16)
    return pl.pallas_call(
        body, grid=(rows // tr,),
        in_specs=[pl.BlockSpec((tr, 2 * FFN_DIM), lambda i: (i, 0))] * nl,
        out_specs=[blk, blk], out_shape=[sds, sds],
        compiler_params=_params(("parallel",)), name="split_ffn",
    )(*dgu_list)


def _local_step(x, p, target, w):
    saved = []
    h = x
    for i in range(DEPTH):
        j = i // 2
        sv = {"h0": h}
        hn = _rms_fwd(h, w["norm_mix"][i][None, :])
        sv["hn"] = hn
        if i % 2 == 0:
            zx = _mm(hn, w["ssd_w_zx"][j], "nn", BF16)
            dtp = _mm(hn, w["ssd_w_dt"][j], "nn", F32)
            xbc = _ssd_conv_fwd(zx, w["ssd_conv_w"][j], w["ssd_conv_b"][j][None, :])
            y, prev = _ssd_scan_fwd(xbc, dtp, w["ssd_dt_bias"][j], w["ssd_a_log"][j], w["ssd_d"][j])
            gn = _ssd_gate_fwd(y, zx, w["ssd_norm_w"][j][None, :])
            h = _mm(gn, w["ssd_w_out"][j], "nn", F32, add=h)
            sv.update(zx=zx, dtp=dtp, xbc=xbc, y=y, prev=prev, gn=gn)
        else:
            pre = _mm(hn, w["gmlp_w_in"][j], "nn", BF16)
            u, v = _gmlp_act_fwd(pre, w["gmlp_b_in"][j][None, :], w["gmlp_ln_w"][j][None, :], w["gmlp_ln_b"][j][None, :])
            gated = _gmlp_mix_fwd(u, v, w["gmlp_w_s"][j], w["gmlp_b_st"][j])
            h = _mm(gated, w["gmlp_w_out"][j], "nn", F32, add=h)
            sv.update(pre=pre, u=u, v=v, gated=gated)
        sv["h1"] = h
        un = _rms_fwd(h, w["norm_ffn"][i][None, :])
        gu = _mm(un, w["ffn_w_gu"][i], "nn", BF16)
        hid = _swiglu_fwd(gu)
        h = _mm(hid, w["ffn_w_down"][i], "nn", F32, add=h)
        sv.update(un=un, gu=gu, hid=hid, h2=h)
        pe = _mm(p[i], w["ple_w_proj"][i], "nn", BF16)
        hg = _rms_fwd(h, w["ple_gate_norm"][i][None, :])
        gl = _mm(hg, w["ple_w_gate"][i], "nn", BF16)
        h = _ple_fwd(pe, gl, h, w["ple_norm"][i][None, :])
        sv.update(pe=pe, hg=hg, gl=gl)
        saved.append(sv)

    lpart, dh, d_final = _loss_head(h, w["final_norm"][None, :], target)
    g = {k: [None] * (DEPTH if k in PER_LAYER else DEPTH // 2) for k in w if k != "final_norm"}
    g["final_norm"] = d_final[0]

    for i in reversed(range(DEPTH)):
        j = i // 2
        sv = saved[i]
        dgl, dpe, d_ple_norm = _ple_bwd(dh, sv["pe"], sv["gl"], w["ple_norm"][i][None, :])
        g["ple_norm"][i] = d_ple_norm[0]
        g["ple_w_gate"][i] = _mm(sv["hg"], dgl, "tn", F32)
        g["ple_w_proj"][i] = _mm(p[i], dpe, "tn", F32)
        dhg = _mm(dgl, w["ple_w_gate"][i], "nt", BF16)
        dh, d_gate_norm = _rms_bwd(dhg, sv["h2"], w["ple_gate_norm"][i][None, :], dh)
        g["ple_gate_norm"][i] = d_gate_norm[0]
        dhid = _mm(dh, w["ffn_w_down"][i], "nt", BF16)
        g["ffn_w_down"][i] = _mm(sv["hid"], dh, "tn", F32)
        dgu = _swiglu_bwd(sv["gu"], dhid)
        g["ffn_w_gu"][i] = _mm(sv["un"], dgu, "tn", F32)
        dun = _mm(dgu, w["ffn_w_gu"][i], "nt", BF16)
        dh, d_norm_ffn = _rms_bwd(dun, sv["h1"], w["norm_ffn"][i][None, :], dh)
        g["norm_ffn"][i] = d_norm_ffn[0]
        if i % 2 == 0:
            dgn = _mm(dh, w["ssd_w_out"][j], "nt", BF16)
            g["ssd_w_out"][j] = _mm(sv["gn"], dh, "tn", F32)
            dy, dzx, d_norm_w = _ssd_gate_bwd(dgn, sv["y"], sv["zx"], w["ssd_norm_w"][j][None, :])
            g["ssd_norm_w"][j] = d_norm_w[0]
            dxbc, ddtp, d_bias, d_alog, d_d = _ssd_scan_bwd(sv["xbc"], sv["dtp"], sv["prev"], dy, w["ssd_dt_bias"][j],
                                                            w["ssd_a_log"][j], w["ssd_d"][j])
            g["ssd_dt_bias"][j] = d_bias[0, :SSD_HEADS]
            g["ssd_a_log"][j] = d_alog[0, :SSD_HEADS]
            g["ssd_d"][j] = d_d[0, :SSD_HEADS]
            dzx, d_conv_w, d_conv_b = _ssd_conv_bwd(sv["zx"], dxbc, w["ssd_conv_w"][j], w["ssd_conv_b"][j][None, :], dzx)
            g["ssd_conv_w"][j] = d_conv_w[:CONV_K]
            g["ssd_conv_b"][j] = d_conv_b[0]
            g["ssd_w_zx"][j] = _mm(sv["hn"], dzx, "tn", F32)
            g["ssd_w_dt"][j] = _mm(sv["hn"], ddtp, "tn", F32)
            dhn = _mm(ddtp, w["ssd_w_dt"][j], "nt", F32)
            dhn = _mm(dzx, w["ssd_w_zx"][j], "nt", BF16, add=dhn)
        else:
            dgated = _mm(dh, w["gmlp_w_out"][j], "nt", BF16)
            g["gmlp_w_out"][j] = _mm(sv["gated"], dh, "tn", F32)
            du, dv, d_ws, d_bst = _gmlp_mix_bwd(dgated, sv["u"], sv["v"], w["gmlp_w_s"][j], w["gmlp_b_st"][j])
            g["gmlp_w_s"][j] = d_ws
            g["gmlp_b_st"][j] = d_bst
            dpre, d_bin, d_lnw, d_lnb = _gmlp_act_bwd(sv["pre"], w["gmlp_b_in"][j][None, :], w["gmlp_ln_w"][j][None, :],
                                                     du, dv)
            g["gmlp_b_in"][j] = d_bin[0]
            g["gmlp_ln_w"][j] = d_lnw[0]
            g["gmlp_ln_b"][j] = d_lnb[0]
            g["gmlp_w_in"][j] = _mm(sv["hn"], dpre, "tn", F32)
            dhn = _mm(dpre, w["gmlp_w_in"][j], "nt", BF16)
        dh, d_norm_mix = _rms_bwd(dhn, sv["h0"], w["norm_mix"][i][None, :], dh)
        g["norm_mix"][i] = d_norm_mix[0]
    return lpart[0, 0], dh, g


PACK_COLS = 1024
ANY = pl.BlockSpec(memory_space=pl.ANY)


def _mesh_pos():
    return lax.axis_index("x"), lax.axis_index("y"), lax.axis_index("c")


def _all_gather(xs_list, name):
    n = len(xs_list)

    def body(*refs):
        x_refs, out_refs = refs[:n], refs[n:2 * n]
        send_sems, recv_sems, local_sems = refs[2 * n:]
        x, y, c = _mesh_pos()
        me, sibling = (x, y, c), (x, y, 1 - c)
        chips = [(1 - x, y), (x, 1 - y), (1 - x, 1 - y)]

        def copy(a, k, block, to, from_input=False):
            px, py, pc = block
            dst = out_refs[a].at[4 * px + 2 * py + pc]
            return pltpu.make_async_remote_copy(
                src_ref=x_refs[a] if from_input else dst, dst_ref=dst,
                send_sem=send_sems.at[7 * a + k], recv_sem=recv_sems.at[7 * a + k], device_id=to,
                device_id_type=MESH_ID)

        mine = [pltpu.make_async_copy(x_refs[a], out_refs[a].at[4 * x + 2 * y + c], local_sems.at[a]) for a in range(n)]
        for cp in mine:
            cp.start()
        first = []
        for a in range(n):
            first += [copy(a, 1 + j, me, (*chip, c), from_input=True) for j, chip in enumerate(chips)]
            first.append(copy(a, 0, me, sibling, from_input=True))
        for cp in first:
            cp.start()
        passed = []
        for a in range(n):
            for j, chip in enumerate(chips):
                copy(a, 1 + j, (*chip, c), me).wait_recv()
                fwd = copy(a, 4 + j, (*chip, c), sibling)
                fwd.start()
                passed.append(fwd)
        for a in range(n):
            copy(a, 0, sibling, me).wait_recv()
            for j, chip in enumerate(chips):
                copy(a, 4 + j, (*chip, 1 - c), me).wait_recv()
        for cp in first + passed:
            cp.wait_send()
        for cp in mine:
            cp.wait()

    outs = pl.pallas_call(
        body,
        out_shape=[jax.ShapeDtypeStruct((N_DEV,) + t.shape, t.dtype) for t in xs_list],
        in_specs=[ANY] * n,
        out_specs=[ANY] * n,
        scratch_shapes=[pltpu.SemaphoreType.DMA((7 * n,)), pltpu.SemaphoreType.DMA((7 * n,)),
                        pltpu.SemaphoreType.DMA((n,))],
        name=name,
    )(*xs_list)
    return list(outs)


def _exchange_sibling(send_list):
    n = len(send_list)

    def body(*refs):
        s_refs, land_refs = refs[:n], refs[n:2 * n]
        send_sems, recv_sems = refs[2 * n:]
        x, y, c = _mesh_pos()
        cps = [pltpu.make_async_remote_copy(src_ref=s_refs[a], dst_ref=land_refs[a], send_sem=send_sems.at[a],
                                            recv_sem=recv_sems.at[a], device_id=(x, y, 1 - c), device_id_type=MESH_ID)
               for a in range(n)]
        for cp in cps:
            cp.start()
        for cp in cps:
            cp.wait()

    outs = pl.pallas_call(
        body,
        out_shape=[jax.ShapeDtypeStruct(t.shape, t.dtype) for t in send_list],
        in_specs=[ANY] * n,
        out_specs=[ANY] * n,
        scratch_shapes=[pltpu.SemaphoreType.DMA((n,)), pltpu.SemaphoreType.DMA((n,))],
        name="rs_exchange_sibling",
    )(*send_list)
    return list(outs)


def _exchange_chips(partial_list):
    n = len(partial_list)

    def body(*refs):
        p_refs, land_refs = refs[:n], refs[n:2 * n]
        send_sems, recv_sems = refs[2 * n:]
        x, y, c = _mesh_pos()
        chips = [(1 - x, y), (x, 1 - y), (1 - x, 1 - y)]
        cps = [pltpu.make_async_remote_copy(src_ref=p_refs[a].at[2 * cx + cy], dst_ref=land_refs[a].at[j],
                                            send_sem=send_sems.at[3 * a + j], recv_sem=recv_sems.at[3 * a + j],
                                            device_id=(cx, cy, c), device_id_type=MESH_ID)
               for a in range(n) for j, (cx, cy) in enumerate(chips)]
        for cp in cps:
            cp.start()
        for cp in cps:
            cp.wait()

    outs = pl.pallas_call(
        body,
        out_shape=[jax.ShapeDtypeStruct((3,) + t.shape[1:], t.dtype) for t in partial_list],
        in_specs=[ANY] * n,
        out_specs=[ANY] * n,
        scratch_shapes=[pltpu.SemaphoreType.DMA((3 * n,)), pltpu.SemaphoreType.DMA((3 * n,))],
        name="rs_exchange_chips",
    )(*partial_list)
    return list(outs)


def _sum_pairs(a, b):
    shape = a.shape
    a = a.reshape(shape[0], -1, shape[-1])
    b = b.reshape(a.shape)
    n, r, cdim = a.shape
    tr = _pick(r, (1024, 512, 256, 128, 64))

    def body(a_ref, b_ref, o_ref):
        o_ref[...] = (a_ref[...].astype(F32) + b_ref[...].astype(F32)).astype(o_ref.dtype)

    blk = pl.BlockSpec((1, tr, cdim), lambda i, j: (i, j, 0))
    return pl.pallas_call(
        body, grid=(n, r // tr), in_specs=[blk, blk], out_specs=blk,
        out_shape=jax.ShapeDtypeStruct(a.shape, a.dtype),
        compiler_params=_params(("parallel", "parallel")), name="rs_sum_pairs",
    )(a, b).reshape(shape)


def _sum_final(own, land):
    shape = own.shape
    own = own.reshape(-1, shape[-1])
    land = land.reshape((3,) + own.shape)
    r, cdim = own.shape
    tr = _pick(r, (1024, 512, 256, 128, 64))

    def body(o_ref, l_ref, out_ref):
        acc = o_ref[...].astype(F32)
        for j in range(3):
            acc = acc + l_ref[j].astype(F32)
        out_ref[...] = acc

    return pl.pallas_call(
        body, grid=(r // tr,),
        in_specs=[pl.BlockSpec((tr, cdim), lambda i: (i, 0)), pl.BlockSpec((3, tr, cdim), lambda i: (0, i, 0))],
        out_specs=pl.BlockSpec((tr, cdim), lambda i: (i, 0)),
        out_shape=jax.ShapeDtypeStruct((r, cdim), F32),
        compiler_params=_params(("parallel",)), name="rs_sum_final",
    )(own, land).reshape(shape)


def _sum_devices(gathered):
    n, r, cdim = gathered.shape
    tr = _pick(r, (64, 32, 16, 8))

    def body(g_ref, out_ref):
        acc = g_ref[0].astype(F32)
        for q in range(1, n):
            acc = acc + g_ref[q].astype(F32)
        out_ref[...] = acc

    return pl.pallas_call(
        body, grid=(r // tr,),
        in_specs=[pl.BlockSpec((n, tr, cdim), lambda i: (0, i, 0))],
        out_specs=pl.BlockSpec((tr, cdim), lambda i: (i, 0)),
        out_shape=jax.ShapeDtypeStruct((r, cdim), F32),
        compiler_params=_params(("parallel",)), name="sum_devices",
    )(gathered)


def _adamw(w, g, m, v):
    shape = w.shape
    cols = shape[-1]
    rows = w.size // cols
    tr = _pick(rows, (512, 256, 128, 64, 32, 16, 8))
    c1 = 1.0 - ADAM_B1 ** ADAM_STEP
    c2 = 1.0 - ADAM_B2 ** ADAM_STEP

    def body(w_ref, g_ref, m_ref, v_ref, d_ref, nm_ref, nv_ref):
        gv = g_ref[...]
        m2 = ADAM_B1 * m_ref[...] + (1.0 - ADAM_B1) * gv
        v2 = ADAM_B2 * v_ref[...] + (1.0 - ADAM_B2) * (gv * gv)
        d_ref[...] = -ADAM_LR * ((m2 / c1) / (jnp.sqrt(v2 / c2) + ADAM_EPS) + ADAM_WD * w_ref[...])
        nm_ref[...] = m2
        nv_ref[...] = v2

    blk = pl.BlockSpec((tr, cols), lambda i: (i, 0))
    sds = jax.ShapeDtypeStruct((rows, cols), F32)
    outs = pl.pallas_call(
        body, grid=(rows // tr,), in_specs=[blk] * 4, out_specs=[blk] * 3, out_shape=[sds] * 3,
        compiler_params=_params(("parallel",)), name=f"adamw_{rows}x{cols}",
    )(*(t.reshape(rows, cols) for t in (w, g, m, v)))
    return tuple(o.reshape(shape) for o in outs)


WEIGHTS = ("norm_mix", "norm_ffn", "ssd_w_in", "ssd_conv_w", "ssd_conv_b", "ssd_dt_bias", "ssd_a_log", "ssd_d",
           "ssd_norm_w", "ssd_w_out", "gmlp_w_in", "gmlp_b_in", "gmlp_ln_w", "gmlp_ln_b", "gmlp_w_s", "gmlp_b_s",
           "gmlp_w_out", "ffn_w_gate", "ffn_w_up", "ffn_w_down", "ple_w_proj", "ple_norm", "ple_gate_norm",
           "ple_w_gate", "final_norm")
ARG_NAMES = ("x", "p") + WEIGHTS + ("loss_target",) + tuple("m_" + n for n in WEIGHTS) + tuple("v_" + n for n in WEIGHTS)
SHARD_AXIS = {"ssd_w_in": 2, "ssd_conv_w": 2, "ssd_w_out": 1, "gmlp_w_in": 2, "gmlp_b_in": 1, "gmlp_ln_w": 1,
              "gmlp_ln_b": 1, "gmlp_w_out": 1, "ffn_w_gate": 2, "ffn_w_up": 2, "ffn_w_down": 1, "ple_w_proj": 2,
              "ple_w_gate": 1}
GATHER_BF16 = ("ssd_w_in", "ssd_w_out", "gmlp_w_in", "gmlp_w_out", "ffn_w_gate", "ffn_w_up", "ffn_w_down",
               "ple_w_proj", "ple_w_gate")
GATHER_F32 = ("ssd_conv_w", "gmlp_b_in", "gmlp_ln_w", "gmlp_ln_b")
SHARDED = GATHER_BF16 + GATHER_F32
WIDE = ("ssd_w_in", "ffn_w_gate", "ffn_w_up")
REPLICATED = tuple(n for n in WEIGHTS if n not in SHARD_AXIS)


def _pack(arrs, dtype, row_mult, lead=0):
    flat = jnp.concatenate([t.reshape(t.shape[:lead] + (-1,)).astype(dtype) for t in arrs], axis=lead)
    n = flat.shape[-1]
    unit = row_mult * PACK_COLS
    total = -(-n // unit) * unit
    flat = jnp.pad(flat, [(0, 0)] * lead + [(0, total - n)])
    return flat.reshape(flat.shape[:lead] + (total // PACK_COLS, PACK_COLS))


def _unpack(buf, names, shapes, lead=0):
    flat = buf.reshape(buf.shape[:lead] + (-1,))
    out, off = {}, 0
    for n in names:
        size = math.prod(shapes[n])
        out[n] = lax.slice_in_dim(flat, off, off + size, axis=lead).reshape(buf.shape[:lead] + tuple(shapes[n]))
        off += size
    return out


ROW_PACKED = ((1024, ("ssd_w_out", "gmlp_w_out", "ffn_w_down", "ple_w_gate")), (512, ("gmlp_w_in",)),
              (128, ("ple_w_proj",)))
ROW_PACK_MULT = 1024


def _pack_rows(arrs, width, lead=0):
    parts = [t.reshape(t.shape[:lead] + (-1, width)).astype(BF16) for t in arrs]
    rows = sum(t.shape[lead] for t in parts)
    pad = -rows % ROW_PACK_MULT
    if pad:
        parts.append(jnp.zeros(parts[0].shape[:lead] + (pad, width), BF16))
    return jnp.concatenate(parts, axis=lead)


def _unpack_rows(buf, names, shapes, lead=0):
    width = buf.shape[-1]
    out, off = {}, 0
    for n in names:
        rows = math.prod(shapes[n]) // width
        out[n] = lax.slice_in_dim(buf, off, off + rows, axis=lead).reshape(buf.shape[:lead] + tuple(shapes[n]))
        off += rows
    return out


def _merge_shards(seg, ax):
    t = jnp.moveaxis(seg, 0, ax)
    return t.reshape(t.shape[:ax] + (t.shape[ax] * t.shape[ax + 1],) + t.shape[ax + 2:])


def _split_for_cores(gfull, ax, c):
    shp = gfull.shape
    t = gfull.reshape(shp[:ax] + (2, 2, 2, shp[ax] // N_DEV) + shp[ax + 1:])

    def take(core):
        u = lax.dynamic_index_in_dim(t, core, axis=ax + 2, keepdims=False)
        u = jnp.moveaxis(u, (ax, ax + 1), (0, 1))
        return u.reshape((4,) + u.shape[2:])

    return take(c), take(1 - c)


def kernel(x, p, norm_mix, norm_ffn, ssd_w_in, ssd_conv_w, ssd_conv_b, ssd_dt_bias, ssd_a_log, ssd_d,
           ssd_norm_w, ssd_w_out, gmlp_w_in, gmlp_b_in, gmlp_ln_w, gmlp_ln_b, gmlp_w_s, gmlp_b_s,
           gmlp_w_out, ffn_w_gate, ffn_w_up, ffn_w_down, ple_w_proj, ple_norm, ple_gate_norm, ple_w_gate,
           final_norm, loss_target, m_norm_mix, m_norm_ffn, m_ssd_w_in, m_ssd_conv_w, m_ssd_conv_b,
           m_ssd_dt_bias, m_ssd_a_log, m_ssd_d, m_ssd_norm_w, m_ssd_w_out, m_gmlp_w_in, m_gmlp_b_in,
           m_gmlp_ln_w, m_gmlp_ln_b, m_gmlp_w_s, m_gmlp_b_s, m_gmlp_w_out, m_ffn_w_gate, m_ffn_w_up,
           m_ffn_w_down, m_ple_w_proj, m_ple_norm, m_ple_gate_norm, m_ple_w_gate, m_final_norm, v_norm_mix,
           v_norm_ffn, v_ssd_w_in, v_ssd_conv_w, v_ssd_conv_b, v_ssd_dt_bias, v_ssd_a_log, v_ssd_d,
           v_ssd_norm_w, v_ssd_w_out, v_gmlp_w_in, v_gmlp_b_in, v_gmlp_ln_w, v_gmlp_ln_b, v_gmlp_w_s,
           v_gmlp_b_s, v_gmlp_w_out, v_ffn_w_gate, v_ffn_w_up, v_ffn_w_down, v_ple_w_proj, v_ple_norm,
           v_ple_gate_norm, v_ple_w_gate, v_final_norm):
    given = locals()
    a = {n: given[n] for n in ARG_NAMES}
    mx, my, c = _mesh_pos()
    xs = a["x"][0]
    ps = a["p"][:, 0]
    target = a["loss_target"][0]
    shard_shapes = {n: a[n].shape for n in WEIGHTS}

    full = {n: a[n] for n in REPLICATED}
    row_packs = [_pack_rows([a[n] for n in names], wd) for wd, names in ROW_PACKED]
    got = _all_gather(row_packs + [_pack([a[n] for n in GATHER_F32], F32, 8)] + [_to_bf16(a[n]) for n in WIDE],
                      "ag_weights")
    for (wd, names), buf in zip(ROW_PACKED, got):
        for n, seg in _unpack_rows(buf, names, shard_shapes, lead=1).items():
            full[n] = _merge_shards(seg, SHARD_AXIS[n])
    k0 = len(ROW_PACKED)
    for n, seg in _unpack(got[k0], GATHER_F32, shard_shapes, lead=1).items():
        full[n] = _merge_shards(seg, SHARD_AXIS[n])
    full["ssd_w_zx"], full["ssd_w_dt"] = _cat_ssd_in(got[k0 + 1])
    full["ffn_w_gu"] = _cat_ffn(got[k0 + 2], got[k0 + 3])

    lpart, dx, g = _local_step(xs, ps, target, _kernel_layouts(full))
    gfull = _reference_layouts(g, wide=False)
    loss = lax.psum(lpart, ("x", "y", "c"))

    def by_core(t):
        u = t.reshape((4, 2) + t.shape[1:])
        return (lax.dynamic_index_in_dim(u, c, axis=1, keepdims=False),
                lax.dynamic_index_in_dim(u, 1 - c, axis=1, keepdims=False))

    def layer_halves(gl, ax):
        if ax == 0:
            t = gl.reshape(4, 2, -1, gl.shape[-1])
            return tuple(lax.dynamic_index_in_dim(t, cc, axis=1, keepdims=False) for cc in (c, 1 - c))
        t = gl.reshape(gl.shape[0], 4, 2, -1)
        return tuple(jnp.moveaxis(lax.dynamic_index_in_dim(t, cc, axis=2, keepdims=False), 1, 0) for cc in (c, 1 - c))

    pairs = []
    for wd, names in ROW_PACKED:
        hs = [layer_halves(gl, SHARD_AXIS[n] - 1) for n in names for gl in g[n]]
        pairs.append(tuple(_pack_rows([h[i] for h in hs], wd, lead=1) for i in (0, 1)))
    halves = [_split_for_cores(gfull[n], SHARD_AXIS[n], c) for n in GATHER_F32]
    pairs.append((_pack([h[0] for h in halves], BF16, 16, lead=1), _pack([h[1] for h in halves], BF16, 16, lead=1)))
    pairs += [by_core(t) for t in (_split_ssd_in(g["ssd_w_zx"], g["ssd_w_dt"]),) + tuple(_split_ffn(g["ffn_w_gu"]))]
    landed = _exchange_sibling([s for _, s in pairs])
    partials = [_sum_pairs(k, l) for (k, _), l in zip(pairs, landed)]
    landed = _exchange_chips(partials)
    sums = [_sum_final(lax.dynamic_index_in_dim(t, 2 * mx + my, axis=0, keepdims=False), l)
            for t, l in zip(partials, landed)]
    gshard = {}
    for (wd, names), buf in zip(ROW_PACKED, sums):
        gshard.update(_unpack_rows(buf, names, shard_shapes))
    gshard.update(_unpack(sums[k0], GATHER_F32, shard_shapes))
    gshard.update(zip(WIDE, sums[k0 + 1:]))
    rep = _all_gather([_pack([gfull[n] for n in REPLICATED], BF16, 64)], "ag_replicated_grads")[0]
    grep = _unpack(_sum_devices(rep), REPLICATED, shard_shapes)
    grads = {**gshard, **grep}

    upd = {n: _adamw(a[n], grads[n], a["m_" + n], a["v_" + n]) for n in WEIGHTS}
    return (loss, dx[None], *[grads[n] for n in WEIGHTS], *[upd[n][0] for n in WEIGHTS],
            *[upd[n][1] for n in WEIGHTS], *[upd[n][2] for n in WEIGHTS])
```

```python
import math

import jax
import jax.numpy as jnp
from jax import lax
from jax.experimental import pallas as pl
from jax.experimental.pallas import tpu as pltpu

F32 = jnp.float32
BF16 = jnp.bfloat16

N_DEV = 8
D_MODEL = 1024
DEPTH = 4
SSD_INNER = 2048
SSD_HEADS = 32
SSD_HEADDIM = 64
SSD_GROUPS = 8
SSD_STATE = 128
SSD_GROUP_W = SSD_INNER // SSD_GROUPS
SSD_CONV_DIM = SSD_INNER + 2 * SSD_GROUPS * SSD_STATE
SSD_IN_DIM = 2 * SSD_INNER + SSD_CONV_DIM - SSD_INNER + SSD_HEADS
SSD_ZX = SSD_INNER + SSD_CONV_DIM
CONV_K = 4
CHUNK = 128
GMLP_INNER = 2048
GMLP_GROUPS = 16
FFN_DIM = 2816
PLE_DIM = 256
RMS_EPS = 1e-6
LN_EPS = 1e-5
LANES = 128
VMEM_LIMIT = 56 * 1024 * 1024

ADAM_LR = 0.001
ADAM_B1 = 0.9
ADAM_B2 = 0.999
ADAM_EPS = 1e-08
ADAM_WD = 0.01
ADAM_STEP = 10

MESH_ID = pl.DeviceIdType.MESH


def _pick(n, cands):
    for c in cands:
        if c <= n and n % c == 0:
            return c
    return n


def _params(dims):
    return pltpu.CompilerParams(dimension_semantics=dims, vmem_limit_bytes=VMEM_LIMIT)


def _dot(a, b, dims=(((1,), (0,)), ((), ())), precision=None):
    return lax.dot_general(a, b, dims, precision=precision, preferred_element_type=F32)


NN = (((1,), (0,)), ((), ()))
NT = (((1,), (1,)), ((), ()))
TN = (((0,), (0,)), ((), ()))


def _sigmoid(x):
    return 1.0 / (1.0 + jnp.exp(-x))


def _dot01(a, b, dims, split, terms=3):
    v = (a, b)[split]
    ones = (a, b)[1 - split].astype(BF16)
    acc = None
    for _ in range(terms):
        piece = v.astype(BF16)
        v = v - piece.astype(F32)
        part = _dot(piece, ones, dims) if split == 0 else _dot(ones, piece, dims)
        acc = part if acc is None else acc + part
    return acc


MM_VMEM_BUDGET = 36 * 1024 * 1024


def _mm_tiles(mode, m, n, k, a_bytes, b_bytes, out_bytes, has_add):
    tm = _pick(m, (1408, 1024, 512, 256, 128))
    tn_cands = [c for c in (2816, 1024, 512, 256, 128) if c <= n and n % c == 0] or [n]
    tk_cands = [k] + [c for c in (2816, 2048, 1024, 512, 256, 128) if c < k and k % c == 0]
    for tk in tk_cands:
        for tn in tn_cands:
            blocks = tm * tk * a_bytes + tk * tn * b_bytes + tm * tn * (out_bytes + (4 if has_add else 0))
            if 2 * blocks + (tm * tn * 4 if tk < k else 0) <= MM_VMEM_BUDGET:
                return tm, tn, tk
    return tm, tn_cands[-1], tk_cands[-1]


def _mm(a, b, mode, out_dtype, add=None):
    if mode == "nn":
        m, k = a.shape
        n = b.shape[1]
    elif mode == "nt":
        m, k = a.shape
        n = b.shape[0]
    else:
        k, m = a.shape
        n = b.shape[1]
    tm, tn, tk = _mm_tiles(mode, m, n, k, a.dtype.itemsize, b.dtype.itemsize, jnp.dtype(out_dtype).itemsize,
                           add is not None)
    nk = k // tk
    dims = {"nn": NN, "nt": NT, "tn": TN}[mode]

    def body(*refs):
        if add is None:
            a_ref, b_ref, o_ref = refs[:3]
            add_ref = None
            rest = refs[3:]
        else:
            a_ref, b_ref, add_ref, o_ref = refs[:4]
            rest = refs[4:]
        part = _dot(a_ref[...].astype(BF16), b_ref[...].astype(BF16), dims)

        def finish(acc):
            if add_ref is not None:
                acc = acc + add_ref[...]
            o_ref[...] = acc.astype(o_ref.dtype)

        if nk == 1:
            finish(part)
        else:
            acc_ref = rest[0]
            kk = pl.program_id(2)

            @pl.when(kk == 0)
            def _():
                acc_ref[...] = part

            @pl.when(kk > 0)
            def _():
                acc_ref[...] += part

            @pl.when(kk == nk - 1)
            def _():
                finish(acc_ref[...])

    if mode == "nn":
        a_spec = pl.BlockSpec((tm, tk), lambda i, j, kk: (i, kk))
        b_spec = pl.BlockSpec((tk, tn), lambda i, j, kk: (kk, j))
    elif mode == "nt":
        a_spec = pl.BlockSpec((tm, tk), lambda i, j, kk: (i, kk))
        b_spec = pl.BlockSpec((tn, tk), lambda i, j, kk: (j, kk))
    else:
        a_spec = pl.BlockSpec((tk, tm), lambda i, j, kk: (kk, i))
        b_spec = pl.BlockSpec((tk, tn), lambda i, j, kk: (kk, j))
    o_spec = pl.BlockSpec((tm, tn), lambda i, j, kk: (i, j))
    in_specs = [a_spec, b_spec] + ([o_spec] if add is not None else [])
    args = (a, b) + ((add,) if add is not None else ())
    return pl.pallas_call(
        body,
        grid=(m // tm, n // tn, nk),
        in_specs=in_specs,
        out_specs=o_spec,
        out_shape=jax.ShapeDtypeStruct((m, n), out_dtype),
        scratch_shapes=[pltpu.VMEM((tm, tn), F32)] if nk > 1 else [],
        compiler_params=_params(("parallel", "parallel", "arbitrary")),
        name=f"mm_{mode}_{m}x{k}x{n}",
    )(*args)


def _rms_fwd(x, w):
    s, d = x.shape
    tr = _pick(s, (512, 256, 128))

    def body(x_ref, w_ref, o_ref):
        xv = x_ref[...]
        r = lax.rsqrt(jnp.mean(xv * xv, axis=-1, keepdims=True) + RMS_EPS)
        o_ref[...] = (xv * r * w_ref[...]).astype(o_ref.dtype)

    return pl.pallas_call(
        body,
        grid=(s // tr,),
        in_specs=[pl.BlockSpec((tr, d), lambda i: (i, 0)), pl.BlockSpec((1, d), lambda i: (0, 0))],
        out_specs=pl.BlockSpec((tr, d), lambda i: (i, 0)),
        out_shape=jax.ShapeDtypeStruct((s, d), BF16),
        compiler_params=_params(("parallel",)),
        name="rms_fwd",
    )(x, w)


def _rms_bwd(dyn, x, w, add):
    s, d = x.shape
    tr = _pick(s, (512, 256, 128))

    def body(dy_ref, x_ref, w_ref, add_ref, dx_ref, dw_ref):
        xv = x_ref[...]
        dy = dy_ref[...].astype(F32)
        r = lax.rsqrt(jnp.mean(xv * xv, axis=-1, keepdims=True) + RMS_EPS)
        xn = xv * r
        dxh = dy * w_ref[...]
        dx = r * (dxh - xn * jnp.mean(dxh * xn, axis=-1, keepdims=True))
        dx_ref[...] = add_ref[...] + dx
        part = jnp.sum(dy * xn, axis=0, keepdims=True)

        @pl.when(pl.program_id(0) == 0)
        def _():
            dw_ref[...] = part

        @pl.when(pl.program_id(0) > 0)
        def _():
            dw_ref[...] += part

    row = pl.BlockSpec((tr, d), lambda i: (i, 0))
    vec = pl.BlockSpec((1, d), lambda i: (0, 0))
    return pl.pallas_call(
        body,
        grid=(s // tr,),
        in_specs=[row, row, vec, row],
        out_specs=[row, vec],
        out_shape=[jax.ShapeDtypeStruct((s, d), F32), jax.ShapeDtypeStruct((1, d), F32)],
        compiler_params=_params(("arbitrary",)),
        name="rms_bwd",
    )(dyn, x, w, add)


CONV_ROWS = 256
CONV_COLS = 256
CONV_HALO = 16


def _conv_taps(ext, w, base, rows):
    acc = w[0:1, :] * ext[base:base + rows]
    for k in range(1, CONV_K):
        acc = acc + w[k:k + 1, :] * ext[base + k:base + k + rows]
    return acc


def _ssd_conv_fwd(zx, conv_w, conv_b):
    s = zx.shape[0]
    c = SSD_CONV_DIM
    nsteps = s // CONV_ROWS
    off = SSD_INNER // CONV_COLS

    def body(x_ref, w_ref, b_ref, o_ref):
        w = w_ref[...]
        b = b_ref[...]

        def step(i, carry):
            r0 = pl.multiple_of(i * CONV_ROWS, CONV_ROWS)
            cur = x_ref[pl.ds(r0, CONV_ROWS), :].astype(F32)
            p0 = pl.multiple_of(jnp.maximum(r0 - CONV_HALO, 0), CONV_HALO)
            prev = x_ref[pl.ds(p0, CONV_HALO), :].astype(F32)
            prev = jnp.where(i == 0, 0.0, prev)
            ext = jnp.concatenate([prev, cur], axis=0)
            acc = _conv_taps(ext, w, CONV_HALO - (CONV_K - 1), CONV_ROWS) + b
            o_ref[pl.ds(r0, CONV_ROWS), :] = (acc * _sigmoid(acc)).astype(o_ref.dtype)
            return carry

        lax.fori_loop(0, nsteps, step, 0)

    return pl.pallas_call(
        body,
        grid=(c // CONV_COLS,),
        in_specs=[pl.BlockSpec((s, CONV_COLS), lambda j: (0, j + off)),
                  pl.BlockSpec((8, CONV_COLS), lambda j: (0, j)),
                  pl.BlockSpec((1, CONV_COLS), lambda j: (0, j))],
        out_specs=pl.BlockSpec((s, CONV_COLS), lambda j: (0, j)),
        out_shape=jax.ShapeDtypeStruct((s, c), BF16),
        compiler_params=_params(("parallel",)),
        name="ssd_conv_fwd",
    )(zx, conv_w, conv_b)


def _ssd_conv_bwd(zx, dxbc, conv_w, conv_b, dzx):
    s = zx.shape[0]
    c = SSD_CONV_DIM
    nsteps = s // CONV_ROWS
    off = SSD_INNER // CONV_COLS

    def body(x_ref, dy_ref, w_ref, b_ref, dzx_in_ref, dx_ref, dw_ref, db_ref, dc_ref):
        w = w_ref[...]
        b = b_ref[...]
        dc_ref[pl.ds(s, CONV_HALO), :] = jnp.zeros((CONV_HALO, CONV_COLS), F32)

        def step1(i, carry):
            dw0, dw1, dw2, dw3, dbs = carry
            r0 = pl.multiple_of(i * CONV_ROWS, CONV_ROWS)
            cur = x_ref[pl.ds(r0, CONV_ROWS), :].astype(F32)
            p0 = pl.multiple_of(jnp.maximum(r0 - CONV_HALO, 0), CONV_HALO)
            prev = x_ref[pl.ds(p0, CONV_HALO), :].astype(F32)
            prev = jnp.where(i == 0, 0.0, prev)
            ext = jnp.concatenate([prev, cur], axis=0)
            base = CONV_HALO - (CONV_K - 1)
            acc = _conv_taps(ext, w, base, CONV_ROWS) + b
            sg = _sigmoid(acc)
            dcv = dy_ref[pl.ds(r0, CONV_ROWS), :].astype(F32) * (sg * (1.0 + acc * (1.0 - sg)))
            dc_ref[pl.ds(r0, CONV_ROWS), :] = dcv
            dws = [jnp.sum(dcv * ext[base + k:base + k + CONV_ROWS], axis=0, keepdims=True) for k in range(CONV_K)]
            return (dw0 + dws[0], dw1 + dws[1], dw2 + dws[2], dw3 + dws[3], dbs + jnp.sum(dcv, axis=0, keepdims=True))

        z = jnp.zeros((1, CONV_COLS), F32)
        dw0, dw1, dw2, dw3, dbs = lax.fori_loop(0, nsteps, step1, (z, z, z, z, z))
        dw_ref[...] = jnp.concatenate([dw0, dw1, dw2, dw3, z, z, z, z], axis=0)
        db_ref[...] = dbs

        def step2(i, carry):
            r0 = pl.multiple_of(i * CONV_ROWS, CONV_ROWS)
            ext = dc_ref[pl.ds(r0, CONV_ROWS + CONV_HALO), :]
            acc = w[0:1, :] * ext[CONV_K - 1:CONV_K - 1 + CONV_ROWS]
            for k in range(1, CONV_K):
                acc = acc + w[k:k + 1, :] * ext[CONV_K - 1 - k:CONV_K - 1 - k + CONV_ROWS]
            dx_ref[pl.ds(r0, CONV_ROWS), :] = acc.astype(dx_ref.dtype)
            return carry

        lax.fori_loop(0, nsteps, step2, 0)

    col = pl.BlockSpec((s, CONV_COLS), lambda j: (0, j))
    shifted = pl.BlockSpec((s, CONV_COLS), lambda j: (0, j + off))
    return pl.pallas_call(
        body,
        grid=(c // CONV_COLS,),
        in_specs=[shifted, col,
                  pl.BlockSpec((8, CONV_COLS), lambda j: (0, j)),
                  pl.BlockSpec((1, CONV_COLS), lambda j: (0, j)),
                  pl.BlockSpec(memory_space=pl.ANY)],
        out_specs=[shifted, pl.BlockSpec((8, CONV_COLS), lambda j: (0, j)), pl.BlockSpec((1, CONV_COLS), lambda j: (0, j))],
        out_shape=[jax.ShapeDtypeStruct((s, SSD_ZX), BF16), jax.ShapeDtypeStruct((8, c), F32),
                   jax.ShapeDtypeStruct((1, c), F32)],
        scratch_shapes=[pltpu.VMEM((s + CONV_HALO, CONV_COLS), F32)],
        input_output_aliases={4: 0},
        compiler_params=_params(("parallel",)),
        name="ssd_conv_bwd",
    )(zx, dxbc, conv_w, conv_b, dzx)


def _ssd_consts():
    li = lax.broadcasted_iota(jnp.int32, (CHUNK, CHUNK), 0)
    si = lax.broadcasted_iota(jnp.int32, (CHUNK, CHUNK), 1)
    tril = li >= si
    hrow = lax.broadcasted_iota(jnp.int32, (LANES, SSD_INNER), 0)
    hcol = lax.broadcasted_iota(jnp.int32, (LANES, SSD_INNER), 1) // SSD_HEADDIM
    expand = (hrow == hcol).astype(F32)
    return tril, expand


def _ssd_chunk_common(dtp_ref, bias_ref, alog_ref, tril, expand):
    lane = lax.broadcasted_iota(jnp.int32, (1, LANES), 1)
    valid = lane < SSD_HEADS
    pre = dtp_ref[...] + bias_ref[...]
    dt = jnp.where(valid, jnp.maximum(pre, 0.0) + jnp.log1p(jnp.exp(-jnp.abs(pre))), 0.0)
    a = jnp.where(valid, -jnp.exp(alog_ref[...]), 0.0)
    da = dt * a
    cs = _dot01(tril.astype(F32), da, NN, 1)
    cs_x = _dot01(cs, expand, NN, 0)
    dt_x = _dot01(dt, expand, NN, 0, terms=2)
    return pre, dt, a, cs, cs_x, dt_x


def _ssd_scan_fwd(xbc, dtp, dt_bias, a_log, d_skip):
    s = xbc.shape[0]
    nc = s // CHUNK
    gw = SSD_GROUP_W

    def body(xbc_ref, dtp_ref, bias_ref, alog_ref, d_ref, y_ref, prev_ref, state_ref):
        c = pl.program_id(0)

        @pl.when(c == 0)
        def _():
            state_ref[...] = jnp.zeros_like(state_ref)

        tril, expand = _ssd_consts()
        pre, dt, a, cs, cs_x, dt_x = _ssd_chunk_common(dtp_ref, bias_ref, alog_ref, tril, expand)
        cs_t = cs.T
        d_x = _dot01(jnp.broadcast_to(d_ref[...], (8, LANES)), expand, NN, 0)[0:1, :]
        cs_last = cs_x[CHUNK - 1:CHUNK, :]
        dec_out = jnp.exp(cs_x)
        dec_st = jnp.exp(cs_last - cs_x)
        dec_ch = jnp.exp(cs_last)
        x = xbc_ref[:, 0:SSD_INNER].astype(F32)
        xr = x * dt_x
        xrs = xr * dec_st
        lane_g = lax.broadcasted_iota(jnp.int32, (1, gw), 1) // SSD_HEADDIM
        for g in range(SSD_GROUPS):
            sl = slice(g * gw, (g + 1) * gw)
            bg = xbc_ref[:, SSD_INNER + g * SSD_STATE:SSD_INNER + (g + 1) * SSD_STATE]
            cg = xbc_ref[:, SSD_INNER + (SSD_GROUPS + g) * SSD_STATE:SSD_INNER + (SSD_GROUPS + g + 1) * SSD_STATE]
            cb = _dot(cg, bg, NT)
            prev_g = state_ref[:, sl]
            prev_ref[0, :, sl] = prev_g
            yo = _dot(cg, prev_g.astype(BF16), NN) * dec_out[:, sl]
            xr_g = xr[:, sl]
            yd = jnp.zeros((CHUNK, gw), F32)
            for r in range(SSD_HEADS // SSD_GROUPS):
                h = g * (SSD_HEADS // SSD_GROUPS) + r
                diff = cs[:, h:h + 1] - cs_t[h:h + 1, :]
                lmat = jnp.exp(jnp.where(tril, diff, -1e30))
                wmat = (cb * lmat).astype(BF16)
                xr_h = jnp.where(lane_g == r, xr_g, 0.0).astype(BF16)
                yd = yd + _dot(wmat, xr_h, NN)
            y_ref[:, sl] = yd + yo + x[:, sl] * d_x[:, sl]
            sc = _dot(bg, xrs[:, sl].astype(BF16), TN)
            state_ref[:, sl] = prev_g * dec_ch[:, sl] + sc

    vec = pl.BlockSpec((1, LANES), lambda c: (0, 0))
    return pl.pallas_call(
        body,
        grid=(nc,),
        in_specs=[pl.BlockSpec((CHUNK, SSD_CONV_DIM), lambda c: (c, 0)),
                  pl.BlockSpec((CHUNK, LANES), lambda c: (c, 0)), vec, vec, vec],
        out_specs=[pl.BlockSpec((CHUNK, SSD_INNER), lambda c: (c, 0)),
                   pl.BlockSpec((1, SSD_STATE, SSD_INNER), lambda c: (c, 0, 0))],
        out_shape=[jax.ShapeDtypeStruct((s, SSD_INNER), F32), jax.ShapeDtypeStruct((nc, SSD_STATE, SSD_INNER), F32)],
        scratch_shapes=[pltpu.VMEM((SSD_STATE, SSD_INNER), F32)],
        compiler_params=_params(("arbitrary",)),
        name="ssd_scan_fwd",
    )(xbc, dtp, dt_bias, a_log, d_skip)


def _ssd_scan_bwd(xbc, dtp, prev, dy, dt_bias, a_log, d_skip):
    s = xbc.shape[0]
    nc = s // CHUNK
    gw = SSD_GROUP_W
    hpg = SSD_HEADS // SSD_GROUPS

    def body(xbc_ref, dtp_ref, prev_ref, dy_ref, bias_ref, alog_ref, d_ref,
             dxbc_ref, ddtp_ref, dbias_ref, dalog_ref, dd_ref, dp_ref, ddx_ref):
        step = pl.program_id(0)

        @pl.when(step == 0)
        def _():
            dp_ref[...] = jnp.zeros_like(dp_ref)
            ddx_ref[...] = jnp.zeros_like(ddx_ref)
            dbias_ref[...] = jnp.zeros_like(dbias_ref)
            dalog_ref[...] = jnp.zeros_like(dalog_ref)

        tril, expand = _ssd_consts()
        pre, dt, a, cs, cs_x, dt_x = _ssd_chunk_common(dtp_ref, bias_ref, alog_ref, tril, expand)
        cs_t = cs.T
        d_x = _dot01(jnp.broadcast_to(d_ref[...], (8, LANES)), expand, NN, 0)[0:1, :]
        cs_last = cs_x[CHUNK - 1:CHUNK, :]
        dec_out = jnp.exp(cs_x)
        dec_st = jnp.exp(cs_last - cs_x)
        dec_ch = jnp.exp(cs_last)
        x = xbc_ref[:, 0:SSD_INNER].astype(F32)
        dyv = dy_ref[...]
        xr = x * dt_x
        xrs = xr * dec_st
        lane_g = lax.broadcasted_iota(jnp.int32, (1, gw), 1) // SSD_HEADDIM
        hsel = lax.broadcasted_iota(jnp.int32, (CHUNK, LANES), 1)
        dcs = jnp.zeros((CHUNK, LANES), F32)
        last_parts = []
        t_parts = []
        dxr_parts = []
        for g in range(SSD_GROUPS):
            sl = slice(g * gw, (g + 1) * gw)
            bsl = slice(SSD_INNER + g * SSD_STATE, SSD_INNER + (g + 1) * SSD_STATE)
            csl = slice(SSD_INNER + (SSD_GROUPS + g) * SSD_STATE, SSD_INNER + (SSD_GROUPS + g + 1) * SSD_STATE)
            bg = xbc_ref[:, bsl]
            cg = xbc_ref[:, csl]
            cb = _dot(cg, bg, NT)
            prev_g = prev_ref[0, :, sl]
            prev_b = prev_g.astype(BF16)
            dp_g = dp_ref[:, sl]
            dp_b = dp_g.astype(BF16)
            dy_g = dyv[:, sl]
            xr_g = xr[:, sl]
            gmat = _dot(cg, prev_b, NN)
            dgm = (dy_g * dec_out[:, sl]).astype(BF16)
            dc_g = _dot(dgm, prev_b, NT)
            dprev = _dot(cg, dgm, TN)
            t1 = dy_g * gmat * dec_out[:, sl]
            mm_ = _dot(bg, dp_b, NN)
            db_g = _dot(xrs[:, sl].astype(BF16), dp_b, NT)
            dxr_g = mm_ * dec_st[:, sl]
            t2 = dxr_g * xr_g
            last = jnp.sum(t2, axis=0, keepdims=True) + jnp.sum(dp_g * prev_g, axis=0, keepdims=True) * dec_ch[:, sl]
            dp_ref[:, sl] = dp_g * dec_ch[:, sl] + dprev
            dcb = jnp.zeros((CHUNK, CHUNK), F32)
            for r in range(hpg):
                h = g * hpg + r
                diff = cs[:, h:h + 1] - cs_t[h:h + 1, :]
                lmat = jnp.exp(jnp.where(tril, diff, -1e30))
                wmat = cb * lmat
                dy_h = jnp.where(lane_g == r, dy_g, 0.0).astype(BF16)
                dw = _dot(dy_h, xr_g.astype(BF16), NT)
                dxr_g = dxr_g + _dot(wmat.astype(BF16), dy_h, TN)
                dcb = dcb + dw * lmat
                q = (dw * wmat).astype(BF16)
                onehot = (hsel == h).astype(BF16)
                dcs = dcs + _dot(q, onehot, NN) - _dot(q, onehot, TN)
            dcb_b = dcb.astype(BF16)
            dc_g = dc_g + _dot(dcb_b, bg, NN)
            db_g = db_g + _dot(dcb_b, cg, TN)
            dxbc_ref[:, bsl] = db_g.astype(dxbc_ref.dtype)
            dxbc_ref[:, csl] = dc_g.astype(dxbc_ref.dtype)
            t_parts.append(t1 - t2)
            last_parts.append(last)
            dxr_parts.append(dxr_g)
        dxr = jnp.concatenate(dxr_parts, axis=1)
        tt = jnp.concatenate(t_parts, axis=1)
        last_x = jnp.concatenate(last_parts, axis=1)
        dxbc_ref[:, 0:SSD_INNER] = (dxr * dt_x + dyv * d_x).astype(dxbc_ref.dtype)
        dcs = dcs + _dot01(tt, expand, NT, 0, terms=2)
        last_h = _dot01(jnp.broadcast_to(last_x, (8, SSD_INNER)), expand, NT, 0)[0:1, :]
        rowi = lax.broadcasted_iota(jnp.int32, (CHUNK, LANES), 0)
        dcs = dcs + jnp.where(rowi == CHUNK - 1, last_h, 0.0)
        dda = _dot01(tril.astype(F32), dcs, TN, 1)
        ddt = dda * a + _dot01(dxr * x, expand, NT, 0, terms=2)
        dpre = ddt * _sigmoid(pre)
        ddtp_ref[...] = dpre
        dbias_ref[...] += jnp.sum(dpre, axis=0, keepdims=True)
        dalog_ref[...] += jnp.sum(dda * dt, axis=0, keepdims=True) * a
        ddx_ref[...] += jnp.broadcast_to(jnp.sum(dyv * x, axis=0, keepdims=True), (8, SSD_INNER))

        @pl.when(step == nc - 1)
        def _():
            dd_ref[...] = _dot01(ddx_ref[...], expand, NT, 0)[0:1, :]

    rev = lambda c: (nc - 1 - c, 0)
    vec = pl.BlockSpec((1, LANES), lambda c: (0, 0))
    return pl.pallas_call(
        body,
        grid=(nc,),
        in_specs=[pl.BlockSpec((CHUNK, SSD_CONV_DIM), rev), pl.BlockSpec((CHUNK, LANES), rev),
                  pl.BlockSpec((1, SSD_STATE, SSD_INNER), lambda c: (nc - 1 - c, 0, 0)),
                  pl.BlockSpec((CHUNK, SSD_INNER), rev), vec, vec, vec],
        out_specs=[pl.BlockSpec((CHUNK, SSD_CONV_DIM), rev), pl.BlockSpec((CHUNK, LANES), rev), vec, vec, vec],
        out_shape=[jax.ShapeDtypeStruct((s, SSD_CONV_DIM), BF16), jax.ShapeDtypeStruct((s, LANES), F32),
                   jax.ShapeDtypeStruct((1, LANES), F32), jax.ShapeDtypeStruct((1, LANES), F32),
                   jax.ShapeDtypeStruct((1, LANES), F32)],
        scratch_shapes=[pltpu.VMEM((SSD_STATE, SSD_INNER), F32), pltpu.VMEM((8, SSD_INNER), F32)],
        compiler_params=_params(("arbitrary",)),
        name="ssd_scan_bwd",
    )(xbc, dtp, prev, dy, dt_bias, a_log, d_skip)


def _ssd_gate_fwd(y, zx, norm_w):
    s = y.shape[0]
    tr = _pick(s, (256, 128))
    gw = SSD_GROUP_W

    def body(y_ref, z_ref, w_ref, o_ref):
        for g in range(SSD_GROUPS):
            sl = slice(g * gw, (g + 1) * gw)
            z = z_ref[:, sl].astype(F32)
            gv = y_ref[:, sl] * (z * _sigmoid(z))
            r = lax.rsqrt(jnp.mean(gv * gv, axis=-1, keepdims=True) + LN_EPS)
            o_ref[:, sl] = (gv * r * w_ref[:, sl]).astype(o_ref.dtype)

    row = pl.BlockSpec((tr, SSD_INNER), lambda i: (i, 0))
    return pl.pallas_call(
        body,
        grid=(s // tr,),
        in_specs=[row, row, pl.BlockSpec((1, SSD_INNER), lambda i: (0, 0))],
        out_specs=row,
        out_shape=jax.ShapeDtypeStruct((s, SSD_INNER), BF16),
        compiler_params=_params(("parallel",)),
        name="ssd_gate_fwd",
    )(y, zx, norm_w)


def _ssd_gate_bwd(dgn, y, zx, norm_w):
    s = y.shape[0]
    tr = _pick(s, (256, 128))
    gw = SSD_GROUP_W

    def body(dg_ref, y_ref, z_ref, w_ref, dy_ref, dz_ref, dw_ref):
        parts = []
        for g in range(SSD_GROUPS):
            sl = slice(g * gw, (g + 1) * gw)
            z = z_ref[:, sl].astype(F32)
            yv = y_ref[:, sl]
            sg = _sigmoid(z)
            sz = z * sg
            gv = yv * sz
            r = lax.rsqrt(jnp.mean(gv * gv, axis=-1, keepdims=True) + LN_EPS)
            gn = gv * r
            dout = dg_ref[:, sl].astype(F32)
            parts.append(jnp.sum(dout * gn, axis=0, keepdims=True))
            dgn_ = dout * w_ref[:, sl]
            dgv = r * (dgn_ - gn * jnp.mean(dgn_ * gn, axis=-1, keepdims=True))
            dy_ref[:, sl] = dgv * sz
            dz_ref[:, sl] = (dgv * yv * (sg * (1.0 + z * (1.0 - sg)))).astype(dz_ref.dtype)
        part = jnp.concatenate(parts, axis=1)

        @pl.when(pl.program_id(0) == 0)
        def _():
            dw_ref[...] = part

        @pl.when(pl.program_id(0) > 0)
        def _():
            dw_ref[...] += part

    row = pl.BlockSpec((tr, SSD_INNER), lambda i: (i, 0))
    vec = pl.BlockSpec((1, SSD_INNER), lambda i: (0, 0))
    return pl.pallas_call(
        body,
        grid=(s // tr,),
        in_specs=[row, row, row, vec],
        out_specs=[row, row, vec],
        out_shape=[jax.ShapeDtypeStruct((s, SSD_INNER), F32), jax.ShapeDtypeStruct((s, SSD_ZX), BF16),
                   jax.ShapeDtypeStruct((1, SSD_INNER), F32)],
        compiler_params=_params(("arbitrary",)),
        name="ssd_gate_bwd",
    )(dgn, y, zx, norm_w)


INV_SQRT2 = 1.0 / math.sqrt(2.0)
INV_SQRT2PI = 1.0 / math.sqrt(2.0 * math.pi)


def _gelu(x):
    return 0.5 * x * (1.0 + lax.erf(x * INV_SQRT2))


def _gelu_grad(x):
    return 0.5 * (1.0 + lax.erf(x * INV_SQRT2)) + x * INV_SQRT2PI * jnp.exp(-0.5 * x * x)


def _gmlp_act_fwd(pre, b_in, ln_w, ln_b):
    s = pre.shape[0]
    tr = _pick(s, (256, 128))
    n = GMLP_INNER

    def body(p_ref, b_ref, w_ref, lb_ref, u_ref, v_ref):
        u_ref[...] = _gelu(p_ref[:, 0:n].astype(F32) + b_ref[:, 0:n]).astype(u_ref.dtype)
        hv = _gelu(p_ref[:, n:2 * n].astype(F32) + b_ref[:, n:2 * n])
        mu = jnp.mean(hv, axis=-1, keepdims=True)
        xc = hv - mu
        r = lax.rsqrt(jnp.mean(xc * xc, axis=-1, keepdims=True) + LN_EPS)
        v_ref[...] = (xc * r * w_ref[...] + lb_ref[...]).astype(v_ref.dtype)

    half = pl.BlockSpec((tr, n), lambda i: (i, 0))
    vec = pl.BlockSpec((1, n), lambda i: (0, 0))
    return pl.pallas_call(
        body,
        grid=(s // tr,),
        in_specs=[pl.BlockSpec((tr, 2 * n), lambda i: (i, 0)), pl.BlockSpec((1, 2 * n), lambda i: (0, 0)), vec, vec],
        out_specs=[half, half],
        out_shape=[jax.ShapeDtypeStruct((s, n), BF16), jax.ShapeDtypeStruct((s, n), BF16)],
        compiler_params=_params(("parallel",)),
        name="gmlp_act_fwd",
    )(pre, b_in, ln_w, ln_b)


def _gmlp_act_bwd(pre, b_in, ln_w, du, dv):
    s = pre.shape[0]
    tr = _pick(s, (256, 128))
    n = GMLP_INNER

    def body(p_ref, b_ref, w_ref, du_ref, dv_ref, dp_ref, db_ref, dw_ref, dlb_ref):
        xu = p_ref[:, 0:n].astype(F32) + b_ref[:, 0:n]
        dpu = du_ref[...].astype(F32) * _gelu_grad(xu)
        xv = p_ref[:, n:2 * n].astype(F32) + b_ref[:, n:2 * n]
        hv = _gelu(xv)
        mu = jnp.mean(hv, axis=-1, keepdims=True)
        xc = hv - mu
        r = lax.rsqrt(jnp.mean(xc * xc, axis=-1, keepdims=True) + LN_EPS)
        vh = xc * r
        dvv = dv_ref[...].astype(F32)
        dvh = dvv * w_ref[...]
        dh = r * (dvh - jnp.mean(dvh, axis=-1, keepdims=True) - vh * jnp.mean(dvh * vh, axis=-1, keepdims=True))
        dpv = dh * _gelu_grad(xv)
        dp_ref[:, 0:n] = dpu.astype(dp_ref.dtype)
        dp_ref[:, n:2 * n] = dpv.astype(dp_ref.dtype)
        pb = jnp.concatenate([jnp.sum(dpu, axis=0, keepdims=True), jnp.sum(dpv, axis=0, keepdims=True)], axis=1)
        pw = jnp.sum(dvv * vh, axis=0, keepdims=True)
        plb = jnp.sum(dvv, axis=0, keepdims=True)

        @pl.when(pl.program_id(0) == 0)
        def _():
            db_ref[...] = pb
            dw_ref[...] = pw
            dlb_ref[...] = plb

        @pl.when(pl.program_id(0) > 0)
        def _():
            db_ref[...] += pb
            dw_ref[...] += pw
            dlb_ref[...] += plb

    half = pl.BlockSpec((tr, n), lambda i: (i, 0))
    full = pl.BlockSpec((tr, 2 * n), lambda i: (i, 0))
    vec = pl.BlockSpec((1, n), lambda i: (0, 0))
    vec2 = pl.BlockSpec((1, 2 * n), lambda i: (0, 0))
    return pl.pallas_call(
        body,
        grid=(s // tr,),
        in_specs=[full, vec2, vec, half, half],
        out_specs=[full, vec2, vec, vec],
        out_shape=[jax.ShapeDtypeStruct((s, 2 * n), BF16), jax.ShapeDtypeStruct((1, 2 * n), F32),
                   jax.ShapeDtypeStruct((1, n), F32), jax.ShapeDtypeStruct((1, n), F32)],
        compiler_params=_params(("arbitrary",)),
        name="gmlp_act_bwd",
    )(pre, b_in, ln_w, du, dv)


def _gmlp_mix_fwd(u, v, w_s, b_st):
    s = u.shape[0]
    gd = GMLP_INNER // GMLP_GROUPS

    def body(u_ref, v_ref, w_ref, b_ref, o_ref):
        li = lax.broadcasted_iota(jnp.int32, (CHUNK, CHUNK), 0)
        si = lax.broadcasted_iota(jnp.int32, (CHUNK, CHUNK), 1)
        tril = li >= si
        for g in range(GMLP_GROUPS):
            sl = slice(g * gd, (g + 1) * gd)
            wm = jnp.where(tril, w_ref[g], 0.0).astype(BF16)
            mixed = _dot(wm, v_ref[:, sl], NN) + b_ref[:, g:g + 1]
            o_ref[:, sl] = (u_ref[:, sl].astype(F32) * mixed).astype(o_ref.dtype)

    row = pl.BlockSpec((CHUNK, GMLP_INNER), lambda c: (c, 0))
    return pl.pallas_call(
        body,
        grid=(s // CHUNK,),
        in_specs=[row, row, pl.BlockSpec((GMLP_GROUPS, CHUNK, CHUNK), lambda c: (0, 0, 0)),
                  pl.BlockSpec((CHUNK, LANES), lambda c: (0, 0))],
        out_specs=row,
        out_shape=jax.ShapeDtypeStruct((s, GMLP_INNER), BF16),
        compiler_params=_params(("parallel",)),
        name="gmlp_mix_fwd",
    )(u, v, w_s, b_st)


def _gmlp_mix_bwd(dgated, u, v, w_s, b_st):
    s = u.shape[0]
    nc = s // CHUNK
    gd = GMLP_INNER // GMLP_GROUPS

    def body(dg_ref, u_ref, v_ref, w_ref, b_ref, du_ref, dv_ref, dw_ref, db_ref):
        c = pl.program_id(0)

        @pl.when(c == 0)
        def _():
            dw_ref[...] = jnp.zeros_like(dw_ref)
            db_ref[...] = jnp.zeros_like(db_ref)

        li = lax.broadcasted_iota(jnp.int32, (CHUNK, CHUNK), 0)
        si = lax.broadcasted_iota(jnp.int32, (CHUNK, CHUNK), 1)
        tril = li >= si
        lane = lax.broadcasted_iota(jnp.int32, (CHUNK, LANES), 1)
        dbacc = jnp.zeros((CHUNK, LANES), F32)
        for g in range(GMLP_GROUPS):
            sl = slice(g * gd, (g + 1) * gd)
            wm = jnp.where(tril, w_ref[g], 0.0).astype(BF16)
            vg = v_ref[:, sl]
            mixed = _dot(wm, vg, NN) + b_ref[:, g:g + 1]
            dgv = dg_ref[:, sl].astype(F32)
            du_ref[:, sl] = (dgv * mixed).astype(du_ref.dtype)
            dm = dgv * u_ref[:, sl].astype(F32)
            dm_b = dm.astype(BF16)
            dv_ref[:, sl] = _dot(wm, dm_b, TN).astype(dv_ref.dtype)
            dw_ref[g] += jnp.where(tril, _dot(dm_b, vg, NT), 0.0)
            dbacc = dbacc + jnp.where(lane == g, jnp.sum(dm, axis=1, keepdims=True), 0.0)
        db_ref[...] += dbacc

    row = pl.BlockSpec((CHUNK, GMLP_INNER), lambda c: (c, 0))
    wspec = pl.BlockSpec((GMLP_GROUPS, CHUNK, CHUNK), lambda c: (0, 0, 0))
    bspec = pl.BlockSpec((CHUNK, LANES), lambda c: (0, 0))
    return pl.pallas_call(
        body,
        grid=(nc,),
        in_specs=[row, row, row, wspec, bspec],
        out_specs=[row, row, wspec, bspec],
        out_shape=[jax.ShapeDtypeStruct((s, GMLP_INNER), BF16), jax.ShapeDtypeStruct((s, GMLP_INNER), BF16),
                   jax.ShapeDtypeStruct((GMLP_GROUPS, CHUNK, CHUNK), F32), jax.ShapeDtypeStruct((CHUNK, LANES), F32)],
        compiler_params=_params(("arbitrary",)),
        name="gmlp_mix_bwd",
    )(dgated, u, v, w_s, b_st)


FFN_HALF = FFN_DIM // 2


def _ffn_up(un, wgu):
    s, d = un.shape
    f = FFN_DIM
    tm = _pick(s, (1024, 512, 256, 128))
    nh = f // FFN_HALF

    def body(x_ref, wg_ref, wu_ref, g_ref, u_ref, h_ref):
        x = x_ref[...]
        g_ref[...] = _dot(x, wg_ref[...], NN).astype(g_ref.dtype)
        u_ref[...] = _dot(x, wu_ref[...], NN).astype(u_ref.dtype)
        gt = g_ref[...].astype(F32)
        h_ref[...] = (gt * _sigmoid(gt) * u_ref[...].astype(F32)).astype(h_ref.dtype)

    out = pl.BlockSpec((tm, FFN_HALF), lambda i, j: (i, j))
    sds = jax.ShapeDtypeStruct((s, f), BF16)
    return pl.pallas_call(
        body,
        grid=(s // tm, nh),
        in_specs=[pl.BlockSpec((tm, d), lambda i, j: (i, 0)), pl.BlockSpec((d, FFN_HALF), lambda i, j: (0, j)),
                  pl.BlockSpec((d, FFN_HALF), lambda i, j: (0, j + nh))],
        out_specs=[out, out, out],
        out_shape=[sds, sds, sds],
        compiler_params=_params(("parallel", "parallel")),
        name="ffn_up",
    )(un, wgu, wgu)


def _ffn_down_bwd(dh, wd, gate, up):
    s, d = dh.shape
    f = FFN_DIM
    tm = _pick(s, (512, 256, 128))

    def body(dh_ref, wd_ref, g_ref, u_ref, o_ref):
        dhb = dh_ref[...].astype(BF16)
        for half in range(f // FFN_HALF):
            cols = slice(half * FFN_HALF, (half + 1) * FFN_HALF)
            dhid = _dot(dhb, wd_ref[cols, :], NT)
            gt = g_ref[:, cols].astype(F32)
            sg = _sigmoid(gt)
            o_ref[:, cols] = (dhid * u_ref[:, cols].astype(F32) * (sg * (1.0 + gt * (1.0 - sg)))).astype(o_ref.dtype)
            o_ref[:, f + half * FFN_HALF:f + (half + 1) * FFN_HALF] = (dhid * gt * sg).astype(o_ref.dtype)

    row = pl.BlockSpec((tm, f), lambda i: (i, 0))
    return pl.pallas_call(
        body,
        grid=(s // tm,),
        in_specs=[pl.BlockSpec((tm, d), lambda i: (i, 0)), pl.BlockSpec((f, d), lambda i: (0, 0)), row, row],
        out_specs=pl.BlockSpec((tm, 2 * f), lambda i: (i, 0)),
        out_shape=jax.ShapeDtypeStruct((s, 2 * f), BF16),
        compiler_params=_params(("parallel",)),
        name="ffn_down_bwd",
    )(dh, wd, gate, up)


def _ple_fwd(pe, gl, h, ple_norm):
    s, d = h.shape
    tr = _pick(s, (512, 256, 128))

    def body(pe_ref, gl_ref, h_ref, w_ref, o_ref):
        pe_ = pe_ref[...].astype(F32)
        r = lax.rsqrt(jnp.mean(pe_ * pe_, axis=-1, keepdims=True) + RMS_EPS)
        o_ref[...] = h_ref[...] + _sigmoid(gl_ref[...].astype(F32)) * (pe_ * r * w_ref[...])

    row = pl.BlockSpec((tr, d), lambda i: (i, 0))
    return pl.pallas_call(
        body,
        grid=(s // tr,),
        in_specs=[row, row, row, pl.BlockSpec((1, d), lambda i: (0, 0))],
        out_specs=row,
        out_shape=jax.ShapeDtypeStruct((s, d), F32),
        compiler_params=_params(("parallel",)),
        name="ple_fwd",
    )(pe, gl, h, ple_norm)


def _ple_bwd(dh, pe, gl, ple_norm):
    s, d = dh.shape
    tr = _pick(s, (512, 256, 128))

    def body(dh_ref, pe_ref, gl_ref, w_ref, dgl_ref, dpe_ref, dw_ref):
        pe_ = pe_ref[...].astype(F32)
        dhv = dh_ref[...]
        r = lax.rsqrt(jnp.mean(pe_ * pe_, axis=-1, keepdims=True) + RMS_EPS)
        pn = pe_ * r
        gate = _sigmoid(gl_ref[...].astype(F32))
        dgl_ref[...] = (dhv * (pn * w_ref[...]) * gate * (1.0 - gate)).astype(dgl_ref.dtype)
        de = dhv * gate
        dxh = de * w_ref[...]
        dpe_ref[...] = (r * (dxh - pn * jnp.mean(dxh * pn, axis=-1, keepdims=True))).astype(dpe_ref.dtype)
        part = jnp.sum(de * pn, axis=0, keepdims=True)

        @pl.when(pl.program_id(0) == 0)
        def _():
            dw_ref[...] = part

        @pl.when(pl.program_id(0) > 0)
        def _():
            dw_ref[...] += part

    row = pl.BlockSpec((tr, d), lambda i: (i, 0))
    vec = pl.BlockSpec((1, d), lambda i: (0, 0))
    return pl.pallas_call(
        body,
        grid=(s // tr,),
        in_specs=[row, row, row, vec],
        out_specs=[row, row, vec],
        out_shape=[jax.ShapeDtypeStruct((s, d), BF16), jax.ShapeDtypeStruct((s, d), BF16),
                   jax.ShapeDtypeStruct((1, d), F32)],
        compiler_params=_params(("arbitrary",)),
        name="ple_bwd",
    )(dh, pe, gl, ple_norm)


def _loss_head(h, w, target):
    s, d = h.shape
    tr = _pick(s, (512, 256, 128))

    def body(h_ref, w_ref, t_ref, l_ref, dh_ref, dw_ref):
        hv = h_ref[...]
        r = lax.rsqrt(jnp.mean(hv * hv, axis=-1, keepdims=True) + RMS_EPS)
        hn = hv * r
        diff = hn * w_ref[...] - t_ref[...]
        lpart = jnp.zeros((8, LANES), F32) + (0.5 / d) * jnp.sum(jnp.sum(diff * diff, axis=1, keepdims=True), axis=0, keepdims=True)
        dy = diff * (1.0 / d)
        dxh = dy * w_ref[...]
        dh_ref[...] = r * (dxh - hn * jnp.mean(dxh * hn, axis=-1, keepdims=True))
        part = jnp.sum(dy * hn, axis=0, keepdims=True)

        @pl.when(pl.program_id(0) == 0)
        def _():
            l_ref[...] = lpart
            dw_ref[...] = part

        @pl.when(pl.program_id(0) > 0)
        def _():
            l_ref[...] += lpart
            dw_ref[...] += part

    row = pl.BlockSpec((tr, d), lambda i: (i, 0))
    vec = pl.BlockSpec((1, d), lambda i: (0, 0))
    return pl.pallas_call(
        body,
        grid=(s // tr,),
        in_specs=[row, vec, row],
        out_specs=[pl.BlockSpec((8, LANES), lambda i: (0, 0)), row, vec],
        out_shape=[jax.ShapeDtypeStruct((8, LANES), F32), jax.ShapeDtypeStruct((s, d), F32),
                   jax.ShapeDtypeStruct((1, d), F32)],
        compiler_params=_params(("arbitrary",)),
        name="loss_head",
    )(h, w, target)


PER_LAYER = ("norm_mix", "norm_ffn", "ffn_w_gu", "ffn_w_down", "ple_w_proj", "ple_norm", "ple_gate_norm", "ple_w_gate")


def _pad_lanes(v):
    return jnp.pad(v.astype(F32), (0, LANES - v.shape[0]))[None, :]


def _kernel_layouts(full):
    w = {}
    for k in ("norm_mix", "norm_ffn", "ple_norm", "ple_gate_norm", "ssd_conv_b", "ssd_norm_w", "gmlp_b_in", "gmlp_ln_w",
              "gmlp_ln_b", "gmlp_w_s"):
        w[k] = [full[k][i].astype(F32) for i in range(full[k].shape[0])]
    w["final_norm"] = full["final_norm"].astype(F32)
    n_ssd = full["ssd_w_out"].shape[0]
    if "ssd_w_in" in full:
        w["ssd_w_zx"] = [full["ssd_w_in"][j][:, :SSD_ZX].astype(BF16) for j in range(n_ssd)]
        w["ssd_w_dt"] = [jnp.pad(full["ssd_w_in"][j][:, SSD_ZX:].astype(BF16), ((0, 0), (0, LANES - SSD_HEADS)))
                         for j in range(n_ssd)]
        w["ffn_w_gu"] = [jnp.concatenate([full["ffn_w_gate"][i], full["ffn_w_up"][i]], axis=1).astype(BF16)
                         for i in range(DEPTH)]
    else:
        for k in ("ssd_w_zx", "ssd_w_dt", "ffn_w_gu"):
            w[k] = full[k]
    w["ssd_conv_w"] = [jnp.pad(full["ssd_conv_w"][j].astype(F32), ((0, 8 - CONV_K), (0, 0))) for j in range(n_ssd)]
    for k in ("ssd_dt_bias", "ssd_a_log", "ssd_d"):
        w[k] = [_pad_lanes(full[k][j]) for j in range(n_ssd)]
    w["ssd_w_out"] = [full["ssd_w_out"][j].astype(BF16) for j in range(n_ssd)]
    n_g = full["gmlp_w_in"].shape[0]
    w["gmlp_w_in"] = [full["gmlp_w_in"][j].astype(BF16) for j in range(n_g)]
    w["gmlp_w_out"] = [full["gmlp_w_out"][j].astype(BF16) for j in range(n_g)]
    w["gmlp_b_st"] = [jnp.pad(full["gmlp_b_s"][j].astype(F32).T, ((0, 0), (0, LANES - GMLP_GROUPS))) for j in range(n_g)]
    w["ffn_w_down"] =[full["ffn_w_down"][i].astype(BF16) for i in range(DEPTH)]
    w["ple_w_proj"] = [full["ple_w_proj"][i].astype(BF16) for i in range(DEPTH)]
    w["ple_w_gate"] = [full["ple_w_gate"][i].astype(BF16) for i in range(DEPTH)]
    return w


MATRICES = ("ssd_w_out", "gmlp_w_in", "gmlp_w_out", "ffn_w_down", "ple_w_proj", "ple_w_gate")


def _reference_layouts(g, wide=True):
    out = {}
    for k in ("norm_mix", "norm_ffn", "ple_norm", "ple_gate_norm", "ssd_conv_b", "ssd_norm_w", "gmlp_b_in", "gmlp_ln_w",
              "gmlp_ln_b", "gmlp_w_s", "ssd_conv_w", "ssd_dt_bias", "ssd_a_log", "ssd_d") + (MATRICES if wide else ()):
        out[k] = jnp.stack(g[k])
    out["final_norm"] = g["final_norm"]
    out["gmlp_b_s"] = jnp.stack([b[:, :GMLP_GROUPS].T for b in g["gmlp_b_st"]])
    if wide:
        out["ssd_w_in"] = jnp.stack([jnp.concatenate([zx, dt[:, :SSD_HEADS]], axis=1)
                                     for zx, dt in zip(g["ssd_w_zx"], g["ssd_w_dt"])])
        out["ffn_w_gate"] = jnp.stack([gu[:, :FFN_DIM] for gu in g["ffn_w_gu"]])
        out["ffn_w_up"] = jnp.stack([gu[:, FFN_DIM:] for gu in g["ffn_w_gu"]])
    return out


RELAYOUT_ROWS = 128
SSD_SHARD = SSD_IN_DIM // N_DEV
FFN_SHARD = FFN_DIM // N_DEV


def _to_bf16(x):
    nl, rows, n = x.shape

    def body(x_ref, o_ref):
        o_ref[...] = x_ref[...].astype(o_ref.dtype)

    blk = pl.BlockSpec((1, rows, n), lambda i: (i, 0, 0))
    return pl.pallas_call(
        body, grid=(nl,), in_specs=[blk], out_specs=blk, out_shape=jax.ShapeDtypeStruct(x.shape, BF16),
        compiler_params=_params(("parallel",)), name="to_bf16",
    )(x)


def _cat_ssd_in(gathered):
    _, nl, rows, n = gathered.shape
    tr = RELAYOUT_ROWS

    def body(g_ref, *o_refs):
        for j in range(nl):
            full = jnp.concatenate([g_ref[d, j] for d in range(N_DEV)], axis=1)
            o_refs[2 * j][...] = full[:, :SSD_ZX]
            o_refs[2 * j + 1][...] = jnp.concatenate(
                [full[:, SSD_ZX:], jnp.zeros((tr, LANES - SSD_HEADS), full.dtype)], axis=1)

    outs = pl.pallas_call(
        body, grid=(rows // tr,),
        in_specs=[pl.BlockSpec((N_DEV, nl, tr, n), lambda i: (0, 0, i, 0))],
        out_specs=[pl.BlockSpec((tr, SSD_ZX), lambda i: (i, 0)), pl.BlockSpec((tr, LANES), lambda i: (i, 0))] * nl,
        out_shape=[jax.ShapeDtypeStruct((rows, SSD_ZX), BF16), jax.ShapeDtypeStruct((rows, LANES), BF16)] * nl,
        compiler_params=_params(("parallel",)), name="cat_ssd_in",
    )(gathered)
    return [outs[2 * j] for j in range(nl)], [outs[2 * j + 1] for j in range(nl)]


def _split_ssd_in(dzx_list, ddt_list):
    nl = len(dzx_list)
    rows = dzx_list[0].shape[0]
    tr = RELAYOUT_ROWS

    def body(*refs):
        o_ref = refs[2 * nl]
        for j in range(nl):
            full = jnp.concatenate([refs[2 * j][...], refs[2 * j + 1][:, 0:SSD_HEADS]], axis=1)
            for d in range(N_DEV):
                o_ref[d, j] = full[:, d * SSD_SHARD:(d + 1) * SSD_SHARD].astype(o_ref.dtype)

    ins = []
    for j in range(nl):
        ins += [dzx_list[j], ddt_list[j]]
    return pl.pallas_call(
        body, grid=(rows // tr,),
        in_specs=[pl.BlockSpec((tr, SSD_ZX), lambda i: (i, 0)), pl.BlockSpec((tr, LANES), lambda i: (i, 0))] * nl,
        out_specs=pl.BlockSpec((N_DEV, nl, tr, SSD_SHARD), lambda i: (0, 0, i, 0)),
        out_shape=jax.ShapeDtypeStruct((N_DEV, nl, rows, SSD_SHARD), BF16),
        compiler_params=_params(("parallel",)), name="split_ssd_in",
    )(*ins)


def _cat_ffn(g_gate, g_up):
    _, nl, rows, n = g_gate.shape
    tr = RELAYOUT_ROWS

    def body(gg_ref, gu_ref, *o_refs):
        for i in range(nl):
            o_refs[i][...] = jnp.concatenate([gg_ref[d, i] for d in range(N_DEV)] + [gu_ref[d, i] for d in range(N_DEV)],
                                             axis=1)

    blk = pl.BlockSpec((N_DEV, nl, tr, n), lambda i: (0, 0, i, 0))
    outs = pl.pallas_call(
        body, grid=(rows // tr,), in_specs=[blk, blk],
        out_specs=[pl.BlockSpec((tr, 2 * FFN_DIM), lambda i: (i, 0))] * nl,
        out_shape=[jax.ShapeDtypeStruct((rows, 2 * FFN_DIM), BF16)] * nl,
        compiler_params=_params(("parallel",)), name="cat_ffn",
    )(g_gate, g_up)
    return list(outs)


def _split_ffn(dgu_list):
    nl = len(dgu_list)
    rows = dgu_list[0].shape[0]
    tr = RELAYOUT_ROWS

    def body(*refs):
        og_ref, ou_ref = refs[nl], refs[nl + 1]
        for i in range(nl):
            full = refs[i][...]
            for d in range(N_DEV):
                og_ref[d, i] = full[:, d * FFN_SHARD:(d + 1) * FFN_SHARD].astype(og_ref.dtype)
                ou_ref[d, i] = full[:, FFN_DIM + d * FFN_SHARD:FFN_DIM + (d + 1) * FFN_SHARD].astype(ou_ref.dtype)

    blk = pl.BlockSpec((N_DEV, nl, tr, FFN_SHARD), lambda i: (0, 0, i, 0))
    sds = jax.ShapeDtypeStruct((N_DEV, nl, rows, FFN_SHARD), BF16)
    return pl.pallas_call(
        body, grid=(rows // tr,),
        in_specs=[pl.BlockSpec((tr, 2 * FFN_DIM), lambda i: (i, 0))] * nl,
        out_specs=[blk, blk], out_shape=[sds, sds],
        compiler_params=_params(("parallel",)), name="split_ffn",
    )(*dgu_list)


def _local_step(x, p, target, w):
    saved = []
    h = x
    for i in range(DEPTH):
        j = i // 2
        sv = {"h0": h}
        hn = _rms_fwd(h, w["norm_mix"][i][None, :])
        sv["hn"] = hn
        if i % 2 == 0:
            zx = _mm(hn, w["ssd_w_zx"][j], "nn", BF16)
            dtp = _mm(hn, w["ssd_w_dt"][j], "nn", F32)
            xbc = _ssd_conv_fwd(zx, w["ssd_conv_w"][j], w["ssd_conv_b"][j][None, :])
            y, prev = _ssd_scan_fwd(xbc, dtp, w["ssd_dt_bias"][j], w["ssd_a_log"][j], w["ssd_d"][j])
            gn = _ssd_gate_fwd(y, zx, w["ssd_norm_w"][j][None, :])
            h = _mm(gn, w["ssd_w_out"][j], "nn", F32, add=h)
            sv.update(zx=zx, dtp=dtp, xbc=xbc, y=y, prev=prev, gn=gn)
        else:
            pre = _mm(hn, w["gmlp_w_in"][j], "nn", BF16)
            u, v = _gmlp_act_fwd(pre, w["gmlp_b_in"][j][None, :], w["gmlp_ln_w"][j][None, :], w["gmlp_ln_b"][j][None, :])
            gated = _gmlp_mix_fwd(u, v, w["gmlp_w_s"][j], w["gmlp_b_st"][j])
            h = _mm(gated, w["gmlp_w_out"][j], "nn", F32, add=h)
            sv.update(pre=pre, u=u, v=v, gated=gated)
        sv["h1"] = h
        un = _rms_fwd(h, w["norm_ffn"][i][None, :])
        gate, up, hid = _ffn_up(un, w["ffn_w_gu"][i])
        h = _mm(hid, w["ffn_w_down"][i], "nn", F32, add=h)
        sv.update(un=un, gate=gate, up=up, hid=hid, h2=h)
        pe = _mm(p[i], w["ple_w_proj"][i], "nn", BF16)
        hg = _rms_fwd(h, w["ple_gate_norm"][i][None, :])
        gl = _mm(hg, w["ple_w_gate"][i], "nn", BF16)
        h = _ple_fwd(pe, gl, h, w["ple_norm"][i][None, :])
        sv.update(pe=pe, hg=hg, gl=gl)
        saved.append(sv)

    lpart, dh, d_final = _loss_head(h, w["final_norm"][None, :], target)
    g = {k: [None] * (DEPTH if k in PER_LAYER else DEPTH // 2) for k in w if k != "final_norm"}
    g["final_norm"] = d_final[0]

    for i in reversed(range(DEPTH)):
        j = i // 2
        sv = saved[i]
        dgl, dpe, d_ple_norm = _ple_bwd(dh, sv["pe"], sv["gl"], w["ple_norm"][i][None, :])
        g["ple_norm"][i] = d_ple_norm[0]
        g["ple_w_gate"][i] = _mm(sv["hg"], dgl, "tn", F32)
        g["ple_w_proj"][i] = _mm(p[i], dpe, "tn", F32)
        dhg = _mm(dgl, w["ple_w_gate"][i], "nt", BF16)
        dh, d_gate_norm = _rms_bwd(dhg, sv["h2"], w["ple_gate_norm"][i][None, :], dh)
        g["ple_gate_norm"][i] = d_gate_norm[0]
        g["ffn_w_down"][i] = _mm(sv["hid"], dh, "tn", F32)
        dgu = _ffn_down_bwd(dh, w["ffn_w_down"][i], sv["gate"], sv["up"])
        g["ffn_w_gu"][i] = _mm(sv["un"], dgu, "tn", F32)
        dun = _mm(dgu, w["ffn_w_gu"][i], "nt", BF16)
        dh, d_norm_ffn = _rms_bwd(dun, sv["h1"], w["norm_ffn"][i][None, :], dh)
        g["norm_ffn"][i] = d_norm_ffn[0]
        if i % 2 == 0:
            dgn = _mm(dh, w["ssd_w_out"][j], "nt", BF16)
            g["ssd_w_out"][j] = _mm(sv["gn"], dh, "tn", F32)
            dy, dzx, d_norm_w = _ssd_gate_bwd(dgn, sv["y"], sv["zx"], w["ssd_norm_w"][j][None, :])
            g["ssd_norm_w"][j] = d_norm_w[0]
            dxbc, ddtp, d_bias, d_alog, d_d = _ssd_scan_bwd(sv["xbc"], sv["dtp"], sv["prev"], dy, w["ssd_dt_bias"][j],
                                                            w["ssd_a_log"][j], w["ssd_d"][j])
            g["ssd_dt_bias"][j] = d_bias[0, :SSD_HEADS]
            g["ssd_a_log"][j] = d_alog[0, :SSD_HEADS]
            g["ssd_d"][j] = d_d[0, :SSD_HEADS]
            dzx, d_conv_w, d_conv_b = _ssd_conv_bwd(sv["zx"], dxbc, w["ssd_conv_w"][j], w["ssd_conv_b"][j][None, :], dzx)
            g["ssd_conv_w"][j] = d_conv_w[:CONV_K]
            g["ssd_conv_b"][j] = d_conv_b[0]
            g["ssd_w_zx"][j] = _mm(sv["hn"], dzx, "tn", F32)
            g["ssd_w_dt"][j] = _mm(sv["hn"], ddtp, "tn", F32)
            dhn = _mm(ddtp, w["ssd_w_dt"][j], "nt", F32)
            dhn = _mm(dzx, w["ssd_w_zx"][j], "nt", BF16, add=dhn)
        else:
            dgated = _mm(dh, w["gmlp_w_out"][j], "nt", BF16)
            g["gmlp_w_out"][j] = _mm(sv["gated"], dh, "tn", F32)
            du, dv, d_ws, d_bst = _gmlp_mix_bwd(dgated, sv["u"], sv["v"], w["gmlp_w_s"][j], w["gmlp_b_st"][j])
            g["gmlp_w_s"][j] = d_ws
            g["gmlp_b_st"][j] = d_bst
            dpre, d_bin, d_lnw, d_lnb = _gmlp_act_bwd(sv["pre"], w["gmlp_b_in"][j][None, :], w["gmlp_ln_w"][j][None, :],
                                                     du, dv)
            g["gmlp_b_in"][j] = d_bin[0]
            g["gmlp_ln_w"][j] = d_lnw[0]
            g["gmlp_ln_b"][j] = d_lnb[0]
            g["gmlp_w_in"][j] = _mm(sv["hn"], dpre, "tn", F32)
            dhn = _mm(dpre, w["gmlp_w_in"][j], "nt", BF16)
        dh, d_norm_mix = _rms_bwd(dhn, sv["h0"], w["norm_mix"][i][None, :], dh)
        g["norm_mix"][i] = d_norm_mix[0]
    return lpart[0, 0], dh, g


PACK_COLS = 1024
ANY = pl.BlockSpec(memory_space=pl.ANY)


def _mesh_pos():
    return lax.axis_index("x"), lax.axis_index("y"), lax.axis_index("c")


def _all_gather(xs_list, name):
    n = len(xs_list)

    def body(*refs):
        x_refs, out_refs = refs[:n], refs[n:2 * n]
        send_sems, recv_sems, local_sems = refs[2 * n:]
        x, y, c = _mesh_pos()
        me, sibling = (x, y, c), (x, y, 1 - c)
        chips = [(1 - x, y), (x, 1 - y), (1 - x, 1 - y)]

        def copy(a, k, block, to, from_input=False):
            px, py, pc = block
            dst = out_refs[a].at[4 * px + 2 * py + pc]
            return pltpu.make_async_remote_copy(
                src_ref=x_refs[a] if from_input else dst, dst_ref=dst,
                send_sem=send_sems.at[7 * a + k], recv_sem=recv_sems.at[7 * a + k], device_id=to,
                device_id_type=MESH_ID)

        mine = [pltpu.make_async_copy(x_refs[a], out_refs[a].at[4 * x + 2 * y + c], local_sems.at[a]) for a in range(n)]
        for cp in mine:
            cp.start()
        first = []
        for a in range(n):
            first += [copy(a, 1 + j, me, (*chip, c), from_input=True) for j, chip in enumerate(chips)]
            first.append(copy(a, 0, me, sibling, from_input=True))
        for cp in first:
            cp.start()
        passed = []
        for a in range(n):
            for j, chip in enumerate(chips):
                copy(a, 1 + j, (*chip, c), me).wait_recv()
                fwd = copy(a, 4 + j, (*chip, c), sibling)
                fwd.start()
                passed.append(fwd)
        for a in range(n):
            copy(a, 0, sibling, me).wait_recv()
            for j, chip in enumerate(chips):
                copy(a, 4 + j, (*chip, 1 - c), me).wait_recv()
        for cp in first + passed:
            cp.wait_send()
        for cp in mine:
            cp.wait()

    outs = pl.pallas_call(
        body,
        out_shape=[jax.ShapeDtypeStruct((N_DEV,) + t.shape, t.dtype) for t in xs_list],
        in_specs=[ANY] * n,
        out_specs=[ANY] * n,
        scratch_shapes=[pltpu.SemaphoreType.DMA((7 * n,)), pltpu.SemaphoreType.DMA((7 * n,)),
                        pltpu.SemaphoreType.DMA((n,))],
        name=name,
    )(*xs_list)
    return list(outs)


def _exchange_sibling(send_list):
    n = len(send_list)

    def body(*refs):
        s_refs, land_refs = refs[:n], refs[n:2 * n]
        send_sems, recv_sems = refs[2 * n:]
        x, y, c = _mesh_pos()
        cps = [pltpu.make_async_remote_copy(src_ref=s_refs[a], dst_ref=land_refs[a], send_sem=send_sems.at[a],
                                            recv_sem=recv_sems.at[a], device_id=(x, y, 1 - c), device_id_type=MESH_ID)
               for a in range(n)]
        for cp in cps:
            cp.start()
        for cp in cps:
            cp.wait()

    outs = pl.pallas_call(
        body,
        out_shape=[jax.ShapeDtypeStruct(t.shape, t.dtype) for t in send_list],
        in_specs=[ANY] * n,
        out_specs=[ANY] * n,
        scratch_shapes=[pltpu.SemaphoreType.DMA((n,)), pltpu.SemaphoreType.DMA((n,))],
        name="rs_exchange_sibling",
    )(*send_list)
    return list(outs)


def _exchange_chips(partial_list):
    n = len(partial_list)

    def body(*refs):
        p_refs, land_refs = refs[:n], refs[n:2 * n]
        send_sems, recv_sems = refs[2 * n:]
        x, y, c = _mesh_pos()
        chips = [(1 - x, y), (x, 1 - y), (1 - x, 1 - y)]
        cps = [pltpu.make_async_remote_copy(src_ref=p_refs[a].at[2 * cx + cy], dst_ref=land_refs[a].at[j],
                                            send_sem=send_sems.at[3 * a + j], recv_sem=recv_sems.at[3 * a + j],
                                            device_id=(cx, cy, c), device_id_type=MESH_ID)
               for a in range(n) for j, (cx, cy) in enumerate(chips)]
        for cp in cps:
            cp.start()
        for cp in cps:
            cp.wait()

    outs = pl.pallas_call(
        body,
        out_shape=[jax.ShapeDtypeStruct((3,) + t.shape[1:], t.dtype) for t in partial_list],
        in_specs=[ANY] * n,
        out_specs=[ANY] * n,
        scratch_shapes=[pltpu.SemaphoreType.DMA((3 * n,)), pltpu.SemaphoreType.DMA((3 * n,))],
        name="rs_exchange_chips",
    )(*partial_list)
    return list(outs)


def _sum_pairs(a, b):
    shape = a.shape
    a = a.reshape(shape[0], -1, shape[-1])
    b = b.reshape(a.shape)
    n, r, cdim = a.shape
    tr = _pick(r, (1024, 512, 256, 128, 64))

    def body(a_ref, b_ref, o_ref):
        o_ref[...] = (a_ref[...].astype(F32) + b_ref[...].astype(F32)).astype(o_ref.dtype)

    blk = pl.BlockSpec((1, tr, cdim), lambda i, j: (i, j, 0))
    return pl.pallas_call(
        body, grid=(n, r // tr), in_specs=[blk, blk], out_specs=blk,
        out_shape=jax.ShapeDtypeStruct(a.shape, a.dtype),
        compiler_params=_params(("parallel", "parallel")), name="rs_sum_pairs",
    )(a, b).reshape(shape)


def _sum_final(own, land):
    shape = own.shape
    own = own.reshape(-1, shape[-1])
    land = land.reshape((3,) + own.shape)
    r, cdim = own.shape
    tr = _pick(r, (1024, 512, 256, 128, 64))

    def body(o_ref, l_ref, out_ref):
        acc = o_ref[...].astype(F32)
        for j in range(3):
            acc = acc + l_ref[j].astype(F32)
        out_ref[...] = acc

    return pl.pallas_call(
        body, grid=(r // tr,),
        in_specs=[pl.BlockSpec((tr, cdim), lambda i: (i, 0)), pl.BlockSpec((3, tr, cdim), lambda i: (0, i, 0))],
        out_specs=pl.BlockSpec((tr, cdim), lambda i: (i, 0)),
        out_shape=jax.ShapeDtypeStruct((r, cdim), F32),
        compiler_params=_params(("parallel",)), name="rs_sum_final",
    )(own, land).reshape(shape)


def _sum_devices(gathered):
    n, r, cdim = gathered.shape
    tr = _pick(r, (64, 32, 16, 8))

    def body(g_ref, out_ref):
        acc = g_ref[0].astype(F32)
        for q in range(1, n):
            acc = acc + g_ref[q].astype(F32)
        out_ref[...] = acc

    return pl.pallas_call(
        body, grid=(r // tr,),
        in_specs=[pl.BlockSpec((n, tr, cdim), lambda i: (0, i, 0))],
        out_specs=pl.BlockSpec((tr, cdim), lambda i: (i, 0)),
        out_shape=jax.ShapeDtypeStruct((r, cdim), F32),
        compiler_params=_params(("parallel",)), name="sum_devices",
    )(gathered)


def _adamw(w, g, m, v):
    shape = w.shape
    cols = shape[-1]
    rows = w.size // cols
    tr = _pick(rows, (512, 256, 128, 64, 32, 16, 8))
    c1 = 1.0 - ADAM_B1 ** ADAM_STEP
    c2 = 1.0 - ADAM_B2 ** ADAM_STEP

    def body(w_ref, g_ref, m_ref, v_ref, d_ref, nm_ref, nv_ref):
        gv = g_ref[...]
        m2 = ADAM_B1 * m_ref[...] + (1.0 - ADAM_B1) * gv
        v2 = ADAM_B2 * v_ref[...] + (1.0 - ADAM_B2) * (gv * gv)
        d_ref[...] = -ADAM_LR * ((m2 / c1) / (jnp.sqrt(v2 / c2) + ADAM_EPS) + ADAM_WD * w_ref[...])
        nm_ref[...] = m2
        nv_ref[...] = v2

    blk = pl.BlockSpec((tr, cols), lambda i: (i, 0))
    sds = jax.ShapeDtypeStruct((rows, cols), F32)
    outs = pl.pallas_call(
        body, grid=(rows // tr,), in_specs=[blk] * 4, out_specs=[blk] * 3, out_shape=[sds] * 3,
        compiler_params=_params(("parallel",)), name=f"adamw_{rows}x{cols}",
    )(*(t.reshape(rows, cols) for t in (w, g, m, v)))
    return tuple(o.reshape(shape) for o in outs)


WEIGHTS = ("norm_mix", "norm_ffn", "ssd_w_in", "ssd_conv_w", "ssd_conv_b", "ssd_dt_bias", "ssd_a_log", "ssd_d",
           "ssd_norm_w", "ssd_w_out", "gmlp_w_in", "gmlp_b_in", "gmlp_ln_w", "gmlp_ln_b", "gmlp_w_s", "gmlp_b_s",
           "gmlp_w_out", "ffn_w_gate", "ffn_w_up", "ffn_w_down", "ple_w_proj", "ple_norm", "ple_gate_norm",
           "ple_w_gate", "final_norm")
ARG_NAMES = ("x", "p") + WEIGHTS + ("loss_target",) + tuple("m_" + n for n in WEIGHTS) + tuple("v_" + n for n in WEIGHTS)
SHARD_AXIS = {"ssd_w_in": 2, "ssd_conv_w": 2, "ssd_w_out": 1, "gmlp_w_in": 2, "gmlp_b_in": 1, "gmlp_ln_w": 1,
              "gmlp_ln_b": 1, "gmlp_w_out": 1, "ffn_w_gate": 2, "ffn_w_up": 2, "ffn_w_down": 1, "ple_w_proj": 2,
              "ple_w_gate": 1}
GATHER_BF16 = ("ssd_w_in", "ssd_w_out", "gmlp_w_in", "gmlp_w_out", "ffn_w_gate", "ffn_w_up", "ffn_w_down",
               "ple_w_proj", "ple_w_gate")
GATHER_F32 = ("ssd_conv_w", "gmlp_b_in", "gmlp_ln_w", "gmlp_ln_b")
SHARDED = GATHER_BF16 + GATHER_F32
WIDE = ("ssd_w_in", "ffn_w_gate", "ffn_w_up")
REPLICATED = tuple(n for n in WEIGHTS if n not in SHARD_AXIS)


def _pack(arrs, dtype, row_mult, lead=0):
    flat = jnp.concatenate([t.reshape(t.shape[:lead] + (-1,)).astype(dtype) for t in arrs], axis=lead)
    n = flat.shape[-1]
    unit = row_mult * PACK_COLS
    total = -(-n // unit) * unit
    flat = jnp.pad(flat, [(0, 0)] * lead + [(0, total - n)])
    return flat.reshape(flat.shape[:lead] + (total // PACK_COLS, PACK_COLS))


def _unpack(buf, names, shapes, lead=0):
    flat = buf.reshape(buf.shape[:lead] + (-1,))
    out, off = {}, 0
    for n in names:
        size = math.prod(shapes[n])
        out[n] = lax.slice_in_dim(flat, off, off + size, axis=lead).reshape(buf.shape[:lead] + tuple(shapes[n]))
        off += size
    return out


ROW_PACKED = ((1024, ("ssd_w_out", "gmlp_w_out", "ffn_w_down", "ple_w_gate")), (512, ("gmlp_w_in",)),
              (128, ("ple_w_proj",)))
ROW_PACK_MULT = 1024


def _pack_rows(arrs, width, lead=0):
    parts = [t.reshape(t.shape[:lead] + (-1, width)).astype(BF16) for t in arrs]
    rows = sum(t.shape[lead] for t in parts)
    pad = -rows % ROW_PACK_MULT
    if pad:
        parts.append(jnp.zeros(parts[0].shape[:lead] + (pad, width), BF16))
    return jnp.concatenate(parts, axis=lead)


def _unpack_rows(buf, names, shapes, lead=0):
    width = buf.shape[-1]
    out, off = {}, 0
    for n in names:
        rows = math.prod(shapes[n]) // width
        out[n] = lax.slice_in_dim(buf, off, off + rows, axis=lead).reshape(buf.shape[:lead] + tuple(shapes[n]))
        off += rows
    return out


def _merge_shards(seg, ax):
    t = jnp.moveaxis(seg, 0, ax)
    return t.reshape(t.shape[:ax] + (t.shape[ax] * t.shape[ax + 1],) + t.shape[ax + 2:])


def _split_for_cores(gfull, ax, c):
    shp = gfull.shape
    t = gfull.reshape(shp[:ax] + (2, 2, 2, shp[ax] // N_DEV) + shp[ax + 1:])

    def take(core):
        u = lax.dynamic_index_in_dim(t, core, axis=ax + 2, keepdims=False)
        u = jnp.moveaxis(u, (ax, ax + 1), (0, 1))
        return u.reshape((4,) + u.shape[2:])

    return take(c), take(1 - c)


def kernel(x, p, norm_mix, norm_ffn, ssd_w_in, ssd_conv_w, ssd_conv_b, ssd_dt_bias, ssd_a_log, ssd_d,
           ssd_norm_w, ssd_w_out, gmlp_w_in, gmlp_b_in, gmlp_ln_w, gmlp_ln_b, gmlp_w_s, gmlp_b_s,
           gmlp_w_out, ffn_w_gate, ffn_w_up, ffn_w_down, ple_w_proj, ple_norm, ple_gate_norm, ple_w_gate,
           final_norm, loss_target, m_norm_mix, m_norm_ffn, m_ssd_w_in, m_ssd_conv_w, m_ssd_conv_b,
           m_ssd_dt_bias, m_ssd_a_log, m_ssd_d, m_ssd_norm_w, m_ssd_w_out, m_gmlp_w_in, m_gmlp_b_in,
           m_gmlp_ln_w, m_gmlp_ln_b, m_gmlp_w_s, m_gmlp_b_s, m_gmlp_w_out, m_ffn_w_gate, m_ffn_w_up,
           m_ffn_w_down, m_ple_w_proj, m_ple_norm, m_ple_gate_norm, m_ple_w_gate, m_final_norm, v_norm_mix,
           v_norm_ffn, v_ssd_w_in, v_ssd_conv_w, v_ssd_conv_b, v_ssd_dt_bias, v_ssd_a_log, v_ssd_d,
           v_ssd_norm_w, v_ssd_w_out, v_gmlp_w_in, v_gmlp_b_in, v_gmlp_ln_w, v_gmlp_ln_b, v_gmlp_w_s,
           v_gmlp_b_s, v_gmlp_w_out, v_ffn_w_gate, v_ffn_w_up, v_ffn_w_down, v_ple_w_proj, v_ple_norm,
           v_ple_gate_norm, v_ple_w_gate, v_final_norm):
    given = locals()
    a = {n: given[n] for n in ARG_NAMES}
    mx, my, c = _mesh_pos()
    xs = a["x"][0]
    ps = a["p"][:, 0]
    target = a["loss_target"][0]
    shard_shapes = {n: a[n].shape for n in WEIGHTS}

    full = {n: a[n] for n in REPLICATED}
    row_packs = [_pack_rows([a[n] for n in names], wd) for wd, names in ROW_PACKED]
    got = _all_gather(row_packs + [_pack([a[n] for n in GATHER_F32], F32, 8)] + [_to_bf16(a[n]) for n in WIDE],
                      "ag_weights")
    for (wd, names), buf in zip(ROW_PACKED, got):
        for n, seg in _unpack_rows(buf, names, shard_shapes, lead=1).items():
            full[n] = _merge_shards(seg, SHARD_AXIS[n])
    k0 = len(ROW_PACKED)
    for n, seg in _unpack(got[k0], GATHER_F32, shard_shapes, lead=1).items():
        full[n] = _merge_shards(seg, SHARD_AXIS[n])
    full["ssd_w_zx"], full["ssd_w_dt"] = _cat_ssd_in(got[k0 + 1])
    full["ffn_w_gu"] = _cat_ffn(got[k0 + 2], got[k0 + 3])

    lpart, dx, g = _local_step(xs, ps, target, _kernel_layouts(full))
    gfull = _reference_layouts(g, wide=False)
    loss = lax.psum(lpart, ("x", "y", "c"))

    def by_core(t):
        u = t.reshape((4, 2) + t.shape[1:])
        return (lax.dynamic_index_in_dim(u, c, axis=1, keepdims=False),
                lax.dynamic_index_in_dim(u, 1 - c, axis=1, keepdims=False))

    def layer_halves(gl, ax):
        if ax == 0:
            t = gl.reshape(4, 2, -1, gl.shape[-1])
            return tuple(lax.dynamic_index_in_dim(t, cc, axis=1, keepdims=False) for cc in (c, 1 - c))
        t = gl.reshape(gl.shape[0], 4, 2, -1)
        return tuple(jnp.moveaxis(lax.dynamic_index_in_dim(t, cc, axis=2, keepdims=False), 1, 0) for cc in (c, 1 - c))

    pairs = []
    for wd, names in ROW_PACKED:
        hs = [layer_halves(gl, SHARD_AXIS[n] - 1) for n in names for gl in g[n]]
        pairs.append(tuple(_pack_rows([h[i] for h in hs], wd, lead=1) for i in (0, 1)))
    halves = [_split_for_cores(gfull[n], SHARD_AXIS[n], c) for n in GATHER_F32]
    pairs.append((_pack([h[0] for h in halves], BF16, 16, lead=1), _pack([h[1] for h in halves], BF16, 16, lead=1)))
    pairs += [by_core(t) for t in (_split_ssd_in(g["ssd_w_zx"], g["ssd_w_dt"]),) + tuple(_split_ffn(g["ffn_w_gu"]))]
    landed = _exchange_sibling([s for _, s in pairs])
    partials = [_sum_pairs(k, l) for (k, _), l in zip(pairs, landed)]
    landed = _exchange_chips(partials)
    sums = [_sum_final(lax.dynamic_index_in_dim(t, 2 * mx + my, axis=0, keepdims=False), l)
            for t, l in zip(partials, landed)]
    gshard = {}
    for (wd, names), buf in zip(ROW_PACKED, sums):
        gshard.update(_unpack_rows(buf, names, shard_shapes))
    gshard.update(_unpack(sums[k0], GATHER_F32, shard_shapes))
    gshard.update(zip(WIDE, sums[k0 + 1:]))
    rep = _all_gather([_pack([gfull[n] for n in REPLICATED], BF16, 64)], "ag_replicated_grads")[0]
    grep = _unpack(_sum_devices(rep), REPLICATED, shard_shapes)
    grads = {**gshard, **grep}

    upd = {n: _adamw(a[n], grads[n], a["m_" + n], a["v_" + n]) for n in WEIGHTS}
    return (loss, dx[None], *[grads[n] for n in WEIGHTS], *[upd[n][0] for n in WEIGHTS],
            *[upd[n][1] for n in WEIGHTS], *[upd[n][2] for n in WEIGHTS])
```

```python
import math

import jax
import jax.numpy as jnp
from jax import lax
from jax.experimental import pallas as pl
from jax.experimental.pallas import tpu as pltpu

F32 = jnp.float32
BF16 = jnp.bfloat16

N_DEV = 8
D_MODEL = 1024
DEPTH = 4
SSD_INNER = 2048
SSD_HEADS = 32
SSD_HEADDIM = 64
SSD_GROUPS = 8
SSD_STATE = 128
SSD_GROUP_W = SSD_INNER // SSD_GROUPS
SSD_CONV_DIM = SSD_INNER + 2 * SSD_GROUPS * SSD_STATE
SSD_IN_DIM = 2 * SSD_INNER + SSD_CONV_DIM - SSD_INNER + SSD_HEADS
SSD_ZX = SSD_INNER + SSD_CONV_DIM
CONV_K = 4
CHUNK = 128
GMLP_INNER = 2048
GMLP_GROUPS = 16
FFN_DIM = 2816
PLE_DIM = 256
RMS_EPS = 1e-6
LN_EPS = 1e-5
LANES = 128
VMEM_LIMIT = 56 * 1024 * 1024

ADAM_LR = 0.001
ADAM_B1 = 0.9
ADAM_B2 = 0.999
ADAM_EPS = 1e-08
ADAM_WD = 0.01
ADAM_STEP = 10

MESH_ID = pl.DeviceIdType.MESH


def _pick(n, cands):
    for c in cands:
        if c <= n and n % c == 0:
            return c
    return n


def _params(dims):
    return pltpu.CompilerParams(dimension_semantics=dims, vmem_limit_bytes=VMEM_LIMIT)


def _dot(a, b, dims=(((1,), (0,)), ((), ())), precision=None):
    return lax.dot_general(a, b, dims, precision=precision, preferred_element_type=F32)


NN = (((1,), (0,)), ((), ()))
NT = (((1,), (1,)), ((), ()))
TN = (((0,), (0,)), ((), ()))


def _sigmoid(x):
    return 1.0 / (1.0 + jnp.exp(-x))


def _dot01(a, b, dims, split, terms=3):
    v = (a, b)[split]
    ones = (a, b)[1 - split].astype(BF16)
    acc = None
    for _ in range(terms):
        piece = v.astype(BF16)
        v = v - piece.astype(F32)
        part = _dot(piece, ones, dims) if split == 0 else _dot(ones, piece, dims)
        acc = part if acc is None else acc + part
    return acc


MM_VMEM_BUDGET = 36 * 1024 * 1024


def _mm_tiles(mode, m, n, k, a_bytes, b_bytes, out_bytes, has_add):
    tm = _pick(m, (1408, 1024, 512, 256, 128))
    tn_cands = [c for c in (2816, 1024, 512, 256, 128) if c <= n and n % c == 0] or [n]
    tk_cands = [k] + [c for c in (2816, 2048, 1024, 512, 256, 128) if c < k and k % c == 0]
    for tk in tk_cands:
        for tn in tn_cands:
            blocks = tm * tk * a_bytes + tk * tn * b_bytes + tm * tn * (out_bytes + (4 if has_add else 0))
            if 2 * blocks + (tm * tn * 4 if tk < k else 0) <= MM_VMEM_BUDGET:
                return tm, tn, tk
    return tm, tn_cands[-1], tk_cands[-1]


def _mm(a, b, mode, out_dtype, add=None):
    if mode == "nn":
        m, k = a.shape
        n = b.shape[1]
    elif mode == "nt":
        m, k = a.shape
        n = b.shape[0]
    else:
        k, m = a.shape
        n = b.shape[1]
    tm, tn, tk = _mm_tiles(mode, m, n, k, a.dtype.itemsize, b.dtype.itemsize, jnp.dtype(out_dtype).itemsize,
                           add is not None)
    nk = k // tk
    dims = {"nn": NN, "nt": NT, "tn": TN}[mode]

    def body(*refs):
        if add is None:
            a_ref, b_ref, o_ref = refs[:3]
            add_ref = None
            rest = refs[3:]
        else:
            a_ref, b_ref, add_ref, o_ref = refs[:4]
            rest = refs[4:]
        part = _dot(a_ref[...].astype(BF16), b_ref[...].astype(BF16), dims)

        def finish(acc):
            if add_ref is not None:
                acc = acc + add_ref[...]
            o_ref[...] = acc.astype(o_ref.dtype)

        if nk == 1:
            finish(part)
        else:
            acc_ref = rest[0]
            kk = pl.program_id(2)

            @pl.when(kk == 0)
            def _():
                acc_ref[...] = part

            @pl.when(kk > 0)
            def _():
                acc_ref[...] += part

            @pl.when(kk == nk - 1)
            def _():
                finish(acc_ref[...])

    if mode == "nn":
        a_spec = pl.BlockSpec((tm, tk), lambda i, j, kk: (i, kk))
        b_spec = pl.BlockSpec((tk, tn), lambda i, j, kk: (kk, j))
    elif mode == "nt":
        a_spec = pl.BlockSpec((tm, tk), lambda i, j, kk: (i, kk))
        b_spec = pl.BlockSpec((tn, tk), lambda i, j, kk: (j, kk))
    else:
        a_spec = pl.BlockSpec((tk, tm), lambda i, j, kk: (kk, i))
        b_spec = pl.BlockSpec((tk, tn), lambda i, j, kk: (kk, j))
    o_spec = pl.BlockSpec((tm, tn), lambda i, j, kk: (i, j))
    in_specs = [a_spec, b_spec] + ([o_spec] if add is not None else [])
    args = (a, b) + ((add,) if add is not None else ())
    return pl.pallas_call(
        body,
        grid=(m // tm, n // tn, nk),
        in_specs=in_specs,
        out_specs=o_spec,
        out_shape=jax.ShapeDtypeStruct((m, n), out_dtype),
        scratch_shapes=[pltpu.VMEM((tm, tn), F32)] if nk > 1 else [],
        compiler_params=_params(("parallel", "parallel", "arbitrary")),
        name=f"mm_{mode}_{m}x{k}x{n}",
    )(*args)


def _rms_fwd(x, w):
    s, d = x.shape
    tr = _pick(s, (512, 256, 128))

    def body(x_ref, w_ref, o_ref):
        xv = x_ref[...]
        r = lax.rsqrt(jnp.mean(xv * xv, axis=-1, keepdims=True) + RMS_EPS)
        o_ref[...] = (xv * r * w_ref[...]).astype(o_ref.dtype)

    return pl.pallas_call(
        body,
        grid=(s // tr,),
        in_specs=[pl.BlockSpec((tr, d), lambda i: (i, 0)), pl.BlockSpec((1, d), lambda i: (0, 0))],
        out_specs=pl.BlockSpec((tr, d), lambda i: (i, 0)),
        out_shape=jax.ShapeDtypeStruct((s, d), BF16),
        compiler_params=_params(("parallel",)),
        name="rms_fwd",
    )(x, w)


def _rms_bwd(dyn, x, w, add):
    s, d = x.shape
    tr = _pick(s, (512, 256, 128))

    def body(dy_ref, x_ref, w_ref, add_ref, dx_ref, dw_ref):
        xv = x_ref[...]
        dy = dy_ref[...].astype(F32)
        r = lax.rsqrt(jnp.mean(xv * xv, axis=-1, keepdims=True) + RMS_EPS)
        xn = xv * r
        dxh = dy * w_ref[...]
        dx = r * (dxh - xn * jnp.mean(dxh * xn, axis=-1, keepdims=True))
        dx_ref[...] = add_ref[...] + dx
        part = jnp.sum(dy * xn, axis=0, keepdims=True)

        @pl.when(pl.program_id(0) == 0)
        def _():
            dw_ref[...] = part

        @pl.when(pl.program_id(0) > 0)
        def _():
            dw_ref[...] += part

    row = pl.BlockSpec((tr, d), lambda i: (i, 0))
    vec = pl.BlockSpec((1, d), lambda i: (0, 0))
    return pl.pallas_call(
        body,
        grid=(s // tr,),
        in_specs=[row, row, vec, row],
        out_specs=[row, vec],
        out_shape=[jax.ShapeDtypeStruct((s, d), F32), jax.ShapeDtypeStruct((1, d), F32)],
        compiler_params=_params(("arbitrary",)),
        name="rms_bwd",
    )(dyn, x, w, add)


CONV_ROWS = 256
CONV_COLS = 256
CONV_HALO = 16


def _conv_taps(ext, w, base, rows):
    acc = w[0:1, :] * ext[base:base + rows]
    for k in range(1, CONV_K):
        acc = acc + w[k:k + 1, :] * ext[base + k:base + k + rows]
    return acc


def _ssd_conv_fwd(zx, conv_w, conv_b):
    s = zx.shape[0]
    c = SSD_CONV_DIM
    nsteps = s // CONV_ROWS
    off = SSD_INNER // CONV_COLS

    def body(x_ref, w_ref, b_ref, o_ref):
        w = w_ref[...]
        b = b_ref[...]

        def step(i, carry):
            r0 = pl.multiple_of(i * CONV_ROWS, CONV_ROWS)
            cur = x_ref[pl.ds(r0, CONV_ROWS), :].astype(F32)
            p0 = pl.multiple_of(jnp.maximum(r0 - CONV_HALO, 0), CONV_HALO)
            prev = x_ref[pl.ds(p0, CONV_HALO), :].astype(F32)
            prev = jnp.where(i == 0, 0.0, prev)
            ext = jnp.concatenate([prev, cur], axis=0)
            acc = _conv_taps(ext, w, CONV_HALO - (CONV_K - 1), CONV_ROWS) + b
            o_ref[pl.ds(r0, CONV_ROWS), :] = (acc * _sigmoid(acc)).astype(o_ref.dtype)
            return carry

        lax.fori_loop(0, nsteps, step, 0)

    return pl.pallas_call(
        body,
        grid=(c // CONV_COLS,),
        in_specs=[pl.BlockSpec((s, CONV_COLS), lambda j: (0, j + off)),
                  pl.BlockSpec((8, CONV_COLS), lambda j: (0, j)),
                  pl.BlockSpec((1, CONV_COLS), lambda j: (0, j))],
        out_specs=pl.BlockSpec((s, CONV_COLS), lambda j: (0, j)),
        out_shape=jax.ShapeDtypeStruct((s, c), BF16),
        compiler_params=_params(("parallel",)),
        name="ssd_conv_fwd",
    )(zx, conv_w, conv_b)


def _ssd_conv_bwd(zx, dxbc, conv_w, conv_b, dzx):
    s = zx.shape[0]
    c = SSD_CONV_DIM
    nsteps = s // CONV_ROWS
    off = SSD_INNER // CONV_COLS

    def body(x_ref, dy_ref, w_ref, b_ref, dzx_in_ref, dx_ref, dw_ref, db_ref, dc_ref):
        w = w_ref[...]
        b = b_ref[...]
        dc_ref[pl.ds(s, CONV_HALO), :] = jnp.zeros((CONV_HALO, CONV_COLS), F32)

        def step1(i, carry):
            dw0, dw1, dw2, dw3, dbs = carry
            r0 = pl.multiple_of(i * CONV_ROWS, CONV_ROWS)
            cur = x_ref[pl.ds(r0, CONV_ROWS), :].astype(F32)
            p0 = pl.multiple_of(jnp.maximum(r0 - CONV_HALO, 0), CONV_HALO)
            prev = x_ref[pl.ds(p0, CONV_HALO), :].astype(F32)
            prev = jnp.where(i == 0, 0.0, prev)
            ext = jnp.concatenate([prev, cur], axis=0)
            base = CONV_HALO - (CONV_K - 1)
            acc = _conv_taps(ext, w, base, CONV_ROWS) + b
            sg = _sigmoid(acc)
            dcv = dy_ref[pl.ds(r0, CONV_ROWS), :].astype(F32) * (sg * (1.0 + acc * (1.0 - sg)))
            dc_ref[pl.ds(r0, CONV_ROWS), :] = dcv
            dws = [jnp.sum(dcv * ext[base + k:base + k + CONV_ROWS], axis=0, keepdims=True) for k in range(CONV_K)]
            return (dw0 + dws[0], dw1 + dws[1], dw2 + dws[2], dw3 + dws[3], dbs + jnp.sum(dcv, axis=0, keepdims=True))

        z = jnp.zeros((1, CONV_COLS), F32)
        dw0, dw1, dw2, dw3, dbs = lax.fori_loop(0, nsteps, step1, (z, z, z, z, z))
        dw_ref[...] = jnp.concatenate([dw0, dw1, dw2, dw3, z, z, z, z], axis=0)
        db_ref[...] = dbs

        def step2(i, carry):
            r0 = pl.multiple_of(i * CONV_ROWS, CONV_ROWS)
            ext = dc_ref[pl.ds(r0, CONV_ROWS + CONV_HALO), :]
            acc = w[0:1, :] * ext[CONV_K - 1:CONV_K - 1 + CONV_ROWS]
            for k in range(1, CONV_K):
                acc = acc + w[k:k + 1, :] * ext[CONV_K - 1 - k:CONV_K - 1 - k + CONV_ROWS]
            dx_ref[pl.ds(r0, CONV_ROWS), :] = acc.astype(dx_ref.dtype)
            return carry

        lax.fori_loop(0, nsteps, step2, 0)

    col = pl.BlockSpec((s, CONV_COLS), lambda j: (0, j))
    shifted = pl.BlockSpec((s, CONV_COLS), lambda j: (0, j + off))
    return pl.pallas_call(
        body,
        grid=(c // CONV_COLS,),
        in_specs=[shifted, col,
                  pl.BlockSpec((8, CONV_COLS), lambda j: (0, j)),
                  pl.BlockSpec((1, CONV_COLS), lambda j: (0, j)),
                  pl.BlockSpec(memory_space=pl.ANY)],
        out_specs=[shifted, pl.BlockSpec((8, CONV_COLS), lambda j: (0, j)), pl.BlockSpec((1, CONV_COLS), lambda j: (0, j))],
        out_shape=[jax.ShapeDtypeStruct((s, SSD_ZX), BF16), jax.ShapeDtypeStruct((8, c), F32),
                   jax.ShapeDtypeStruct((1, c), F32)],
        scratch_shapes=[pltpu.VMEM((s + CONV_HALO, CONV_COLS), F32)],
        input_output_aliases={4: 0},
        compiler_params=_params(("parallel",)),
        name="ssd_conv_bwd",
    )(zx, dxbc, conv_w, conv_b, dzx)


def _ssd_consts():
    li = lax.broadcasted_iota(jnp.int32, (CHUNK, CHUNK), 0)
    si = lax.broadcasted_iota(jnp.int32, (CHUNK, CHUNK), 1)
    tril = li >= si
    hrow = lax.broadcasted_iota(jnp.int32, (LANES, SSD_INNER), 0)
    hcol = lax.broadcasted_iota(jnp.int32, (LANES, SSD_INNER), 1) // SSD_HEADDIM
    expand = (hrow == hcol).astype(F32)
    return tril, expand


def _ssd_chunk_common(dtp_ref, bias_ref, alog_ref, tril, expand):
    lane = lax.broadcasted_iota(jnp.int32, (1, LANES), 1)
    valid = lane < SSD_HEADS
    pre = dtp_ref[...] + bias_ref[...]
    dt = jnp.where(valid, jnp.maximum(pre, 0.0) + jnp.log1p(jnp.exp(-jnp.abs(pre))), 0.0)
    a = jnp.where(valid, -jnp.exp(alog_ref[...]), 0.0)
    da = dt * a
    cs = _dot01(tril.astype(F32), da, NN, 1)
    cs_x = _dot01(cs, expand, NN, 0)
    dt_x = _dot01(dt, expand, NN, 0, terms=2)
    return pre, dt, a, cs, cs_x, dt_x


def _ssd_scan_fwd(xbc, dtp, dt_bias, a_log, d_skip):
    s = xbc.shape[0]
    nc = s // CHUNK
    gw = SSD_GROUP_W

    def body(xbc_ref, dtp_ref, bias_ref, alog_ref, d_ref, y_ref, prev_ref, state_ref):
        c = pl.program_id(0)

        @pl.when(c == 0)
        def _():
            state_ref[...] = jnp.zeros_like(state_ref)

        tril, expand = _ssd_consts()
        pre, dt, a, cs, cs_x, dt_x = _ssd_chunk_common(dtp_ref, bias_ref, alog_ref, tril, expand)
        cs_t = cs.T
        d_x = _dot01(jnp.broadcast_to(d_ref[...], (8, LANES)), expand, NN, 0)[0:1, :]
        cs_last = cs_x[CHUNK - 1:CHUNK, :]
        dec_out = jnp.exp(cs_x)
        dec_st = jnp.exp(cs_last - cs_x)
        dec_ch = jnp.exp(cs_last)
        x = xbc_ref[:, 0:SSD_INNER].astype(F32)
        xr = x * dt_x
        xrs = xr * dec_st
        lane_g = lax.broadcasted_iota(jnp.int32, (1, gw), 1) // SSD_HEADDIM
        for g in range(SSD_GROUPS):
            sl = slice(g * gw, (g + 1) * gw)
            bg = xbc_ref[:, SSD_INNER + g * SSD_STATE:SSD_INNER + (g + 1) * SSD_STATE]
            cg = xbc_ref[:, SSD_INNER + (SSD_GROUPS + g) * SSD_STATE:SSD_INNER + (SSD_GROUPS + g + 1) * SSD_STATE]
            cb = _dot(cg, bg, NT)
            prev_g = state_ref[:, sl]
            prev_ref[0, :, sl] = prev_g
            yo = _dot(cg, prev_g.astype(BF16), NN) * dec_out[:, sl]
            xr_g = xr[:, sl]
            yd = jnp.zeros((CHUNK, gw), F32)
            for r in range(SSD_HEADS // SSD_GROUPS):
                h = g * (SSD_HEADS // SSD_GROUPS) + r
                diff = cs[:, h:h + 1] - cs_t[h:h + 1, :]
                lmat = jnp.exp(jnp.where(tril, diff, -1e30))
                wmat = (cb * lmat).astype(BF16)
                xr_h = jnp.where(lane_g == r, xr_g, 0.0).astype(BF16)
                yd = yd + _dot(wmat, xr_h, NN)
            y_ref[:, sl] = yd + yo + x[:, sl] * d_x[:, sl]
            sc = _dot(bg, xrs[:, sl].astype(BF16), TN)
            state_ref[:, sl] = prev_g * dec_ch[:, sl] + sc

    vec = pl.BlockSpec((1, LANES), lambda c: (0, 0))
    return pl.pallas_call(
        body,
        grid=(nc,),
        in_specs=[pl.BlockSpec((CHUNK, SSD_CONV_DIM), lambda c: (c, 0)),
                  pl.BlockSpec((CHUNK, LANES), lambda c: (c, 0)), vec, vec, vec],
        out_specs=[pl.BlockSpec((CHUNK, SSD_INNER), lambda c: (c, 0)),
                   pl.BlockSpec((1, SSD_STATE, SSD_INNER), lambda c: (c, 0, 0))],
        out_shape=[jax.ShapeDtypeStruct((s, SSD_INNER), F32), jax.ShapeDtypeStruct((nc, SSD_STATE, SSD_INNER), F32)],
        scratch_shapes=[pltpu.VMEM((SSD_STATE, SSD_INNER), F32)],
        compiler_params=_params(("arbitrary",)),
        name="ssd_scan_fwd",
    )(xbc, dtp, dt_bias, a_log, d_skip)


def _ssd_scan_bwd(xbc, dtp, prev, dy, dt_bias, a_log, d_skip):
    s = xbc.shape[0]
    nc = s // CHUNK
    gw = SSD_GROUP_W
    hpg = SSD_HEADS // SSD_GROUPS

    def body(xbc_ref, dtp_ref, prev_ref, dy_ref, bias_ref, alog_ref, d_ref,
             dxbc_ref, ddtp_ref, dbias_ref, dalog_ref, dd_ref, dp_ref, ddx_ref):
        step = pl.program_id(0)

        @pl.when(step == 0)
        def _():
            dp_ref[...] = jnp.zeros_like(dp_ref)
            ddx_ref[...] = jnp.zeros_like(ddx_ref)
            dbias_ref[...] = jnp.zeros_like(dbias_ref)
            dalog_ref[...] = jnp.zeros_like(dalog_ref)

        tril, expand = _ssd_consts()
        pre, dt, a, cs, cs_x, dt_x = _ssd_chunk_common(dtp_ref, bias_ref, alog_ref, tril, expand)
        cs_t = cs.T
        d_x = _dot01(jnp.broadcast_to(d_ref[...], (8, LANES)), expand, NN, 0)[0:1, :]
        cs_last = cs_x[CHUNK - 1:CHUNK, :]
        dec_out = jnp.exp(cs_x)
        dec_st = jnp.exp(cs_last - cs_x)
        dec_ch = jnp.exp(cs_last)
        x = xbc_ref[:, 0:SSD_INNER].astype(F32)
        dyv = dy_ref[...]
        xr = x * dt_x
        xrs = xr * dec_st
        lane_g = lax.broadcasted_iota(jnp.int32, (1, gw), 1) // SSD_HEADDIM
        hsel = lax.broadcasted_iota(jnp.int32, (CHUNK, LANES), 1)
        dcs = jnp.zeros((CHUNK, LANES), F32)
        last_parts = []
        t_parts = []
        dxr_parts = []
        for g in range(SSD_GROUPS):
            sl = slice(g * gw, (g + 1) * gw)
            bsl = slice(SSD_INNER + g * SSD_STATE, SSD_INNER + (g + 1) * SSD_STATE)
            csl = slice(SSD_INNER + (SSD_GROUPS + g) * SSD_STATE, SSD_INNER + (SSD_GROUPS + g + 1) * SSD_STATE)
            bg = xbc_ref[:, bsl]
            cg = xbc_ref[:, csl]
            cb = _dot(cg, bg, NT)
            prev_g = prev_ref[0, :, sl]
            prev_b = prev_g.astype(BF16)
            dp_g = dp_ref[:, sl]
            dp_b = dp_g.astype(BF16)
            dy_g = dyv[:, sl]
            xr_g = xr[:, sl]
            gmat = _dot(cg, prev_b, NN)
            dgm = (dy_g * dec_out[:, sl]).astype(BF16)
            dc_g = _dot(dgm, prev_b, NT)
            dprev = _dot(cg, dgm, TN)
            t1 = dy_g * gmat * dec_out[:, sl]
            mm_ = _dot(bg, dp_b, NN)
            db_g = _dot(xrs[:, sl].astype(BF16), dp_b, NT)
            dxr_g = mm_ * dec_st[:, sl]
            t2 = dxr_g * xr_g
            last = jnp.sum(t2, axis=0, keepdims=True) + jnp.sum(dp_g * prev_g, axis=0, keepdims=True) * dec_ch[:, sl]
            dp_ref[:, sl] = dp_g * dec_ch[:, sl] + dprev
            dcb = jnp.zeros((CHUNK, CHUNK), F32)
            for r in range(hpg):
                h = g * hpg + r
                diff = cs[:, h:h + 1] - cs_t[h:h + 1, :]
                lmat = jnp.exp(jnp.where(tril, diff, -1e30))
                wmat = cb * lmat
                dy_h = jnp.where(lane_g == r, dy_g, 0.0).astype(BF16)
                dw = _dot(dy_h, xr_g.astype(BF16), NT)
                dxr_g = dxr_g + _dot(wmat.astype(BF16), dy_h, TN)
                dcb = dcb + dw * lmat
                q = (dw * wmat).astype(BF16)
                onehot = (hsel == h).astype(BF16)
                dcs = dcs + _dot(q, onehot, NN) - _dot(q, onehot, TN)
            dcb_b = dcb.astype(BF16)
            dc_g = dc_g + _dot(dcb_b, bg, NN)
            db_g = db_g + _dot(dcb_b, cg, TN)
            dxbc_ref[:, bsl] = db_g.astype(dxbc_ref.dtype)
            dxbc_ref[:, csl] = dc_g.astype(dxbc_ref.dtype)
            t_parts.append(t1 - t2)
            last_parts.append(last)
            dxr_parts.append(dxr_g)
        dxr = jnp.concatenate(dxr_parts, axis=1)
        tt = jnp.concatenate(t_parts, axis=1)
        last_x = jnp.concatenate(last_parts, axis=1)
        dxbc_ref[:, 0:SSD_INNER] = (dxr * dt_x + dyv * d_x).astype(dxbc_ref.dtype)
        dcs = dcs + _dot01(tt, expand, NT, 0, terms=2)
        last_h = _dot01(jnp.broadcast_to(last_x, (8, SSD_INNER)), expand, NT, 0)[0:1, :]
        rowi = lax.broadcasted_iota(jnp.int32, (CHUNK, LANES), 0)
        dcs = dcs + jnp.where(rowi == CHUNK - 1, last_h, 0.0)
        dda = _dot01(tril.astype(F32), dcs, TN, 1)
        ddt = dda * a + _dot01(dxr * x, expand, NT, 0, terms=2)
        dpre = ddt * _sigmoid(pre)
        ddtp_ref[...] = dpre
        dbias_ref[...] += jnp.sum(dpre, axis=0, keepdims=True)
        dalog_ref[...] += jnp.sum(dda * dt, axis=0, keepdims=True) * a
        ddx_ref[...] += jnp.broadcast_to(jnp.sum(dyv * x, axis=0, keepdims=True), (8, SSD_INNER))

        @pl.when(step == nc - 1)
        def _():
            dd_ref[...] = _dot01(ddx_ref[...], expand, NT, 0)[0:1, :]

    rev = lambda c: (nc - 1 - c, 0)
    vec = pl.BlockSpec((1, LANES), lambda c: (0, 0))
    return pl.pallas_call(
        body,
        grid=(nc,),
        in_specs=[pl.BlockSpec((CHUNK, SSD_CONV_DIM), rev), pl.BlockSpec((CHUNK, LANES), rev),
                  pl.BlockSpec((1, SSD_STATE, SSD_INNER), lambda c: (nc - 1 - c, 0, 0)),
                  pl.BlockSpec((CHUNK, SSD_INNER), rev), vec, vec, vec],
        out_specs=[pl.BlockSpec((CHUNK, SSD_CONV_DIM), rev), pl.BlockSpec((CHUNK, LANES), rev), vec, vec, vec],
        out_shape=[jax.ShapeDtypeStruct((s, SSD_CONV_DIM), BF16), jax.ShapeDtypeStruct((s, LANES), F32),
                   jax.ShapeDtypeStruct((1, LANES), F32), jax.ShapeDtypeStruct((1, LANES), F32),
                   jax.ShapeDtypeStruct((1, LANES), F32)],
        scratch_shapes=[pltpu.VMEM((SSD_STATE, SSD_INNER), F32), pltpu.VMEM((8, SSD_INNER), F32)],
        compiler_params=_params(("arbitrary",)),
        name="ssd_scan_bwd",
    )(xbc, dtp, prev, dy, dt_bias, a_log, d_skip)


def _ssd_gate_fwd(y, zx, norm_w):
    s = y.shape[0]
    tr = _pick(s, (256, 128))
    gw = SSD_GROUP_W

    def body(y_ref, z_ref, w_ref, o_ref):
        for g in range(SSD_GROUPS):
            sl = slice(g * gw, (g + 1) * gw)
            z = z_ref[:, sl].astype(F32)
            gv = y_ref[:, sl] * (z * _sigmoid(z))
            r = lax.rsqrt(jnp.mean(gv * gv, axis=-1, keepdims=True) + LN_EPS)
            o_ref[:, sl] = (gv * r * w_ref[:, sl]).astype(o_ref.dtype)

    row = pl.BlockSpec((tr, SSD_INNER), lambda i: (i, 0))
    return pl.pallas_call(
        body,
        grid=(s // tr,),
        in_specs=[row, row, pl.BlockSpec((1, SSD_INNER), lambda i: (0, 0))],
        out_specs=row,
        out_shape=jax.ShapeDtypeStruct((s, SSD_INNER), BF16),
        compiler_params=_params(("parallel",)),
        name="ssd_gate_fwd",
    )(y, zx, norm_w)


def _ssd_gate_bwd(dgn, y, zx, norm_w):
    s = y.shape[0]
    tr = _pick(s, (256, 128))
    gw = SSD_GROUP_W

    def body(dg_ref, y_ref, z_ref, w_ref, dy_ref, dz_ref, dw_ref):
        parts = []
        for g in range(SSD_GROUPS):
            sl = slice(g * gw, (g + 1) * gw)
            z = z_ref[:, sl].astype(F32)
            yv = y_ref[:, sl]
            sg = _sigmoid(z)
            sz = z * sg
            gv = yv * sz
            r = lax.rsqrt(jnp.mean(gv * gv, axis=-1, keepdims=True) + LN_EPS)
            gn = gv * r
            dout = dg_ref[:, sl].astype(F32)
            parts.append(jnp.sum(dout * gn, axis=0, keepdims=True))
            dgn_ = dout * w_ref[:, sl]
            dgv = r * (dgn_ - gn * jnp.mean(dgn_ * gn, axis=-1, keepdims=True))
            dy_ref[:, sl] = dgv * sz
            dz_ref[:, sl] = (dgv * yv * (sg * (1.0 + z * (1.0 - sg)))).astype(dz_ref.dtype)
        part = jnp.concatenate(parts, axis=1)

        @pl.when(pl.program_id(0) == 0)
        def _():
            dw_ref[...] = part

        @pl.when(pl.program_id(0) > 0)
        def _():
            dw_ref[...] += part

    row = pl.BlockSpec((tr, SSD_INNER), lambda i: (i, 0))
    vec = pl.BlockSpec((1, SSD_INNER), lambda i: (0, 0))
    return pl.pallas_call(
        body,
        grid=(s // tr,),
        in_specs=[row, row, row, vec],
        out_specs=[row, row, vec],
        out_shape=[jax.ShapeDtypeStruct((s, SSD_INNER), F32), jax.ShapeDtypeStruct((s, SSD_ZX), BF16),
                   jax.ShapeDtypeStruct((1, SSD_INNER), F32)],
        compiler_params=_params(("arbitrary",)),
        name="ssd_gate_bwd",
    )(dgn, y, zx, norm_w)


INV_SQRT2 = 1.0 / math.sqrt(2.0)
INV_SQRT2PI = 1.0 / math.sqrt(2.0 * math.pi)


def _gelu(x):
    return 0.5 * x * (1.0 + lax.erf(x * INV_SQRT2))


def _gelu_grad(x):
    return 0.5 * (1.0 + lax.erf(x * INV_SQRT2)) + x * INV_SQRT2PI * jnp.exp(-0.5 * x * x)


def _gmlp_act_fwd(pre, b_in, ln_w, ln_b):
    s = pre.shape[0]
    tr = _pick(s, (256, 128))
    n = GMLP_INNER

    def body(p_ref, b_ref, w_ref, lb_ref, u_ref, v_ref):
        u_ref[...] = _gelu(p_ref[:, 0:n].astype(F32) + b_ref[:, 0:n]).astype(u_ref.dtype)
        hv = _gelu(p_ref[:, n:2 * n].astype(F32) + b_ref[:, n:2 * n])
        mu = jnp.mean(hv, axis=-1, keepdims=True)
        xc = hv - mu
        r = lax.rsqrt(jnp.mean(xc * xc, axis=-1, keepdims=True) + LN_EPS)
        v_ref[...] = (xc * r * w_ref[...] + lb_ref[...]).astype(v_ref.dtype)

    half = pl.BlockSpec((tr, n), lambda i: (i, 0))
    vec = pl.BlockSpec((1, n), lambda i: (0, 0))
    return pl.pallas_call(
        body,
        grid=(s // tr,),
        in_specs=[pl.BlockSpec((tr, 2 * n), lambda i: (i, 0)), pl.BlockSpec((1, 2 * n), lambda i: (0, 0)), vec, vec],
        out_specs=[half, half],
        out_shape=[jax.ShapeDtypeStruct((s, n), BF16), jax.ShapeDtypeStruct((s, n), BF16)],
        compiler_params=_params(("parallel",)),
        name="gmlp_act_fwd",
    )(pre, b_in, ln_w, ln_b)


def _gmlp_act_bwd(pre, b_in, ln_w, du, dv):
    s = pre.shape[0]
    tr = _pick(s, (256, 128))
    n = GMLP_INNER

    def body(p_ref, b_ref, w_ref, du_ref, dv_ref, dp_ref, db_ref, dw_ref, dlb_ref):
        xu = p_ref[:, 0:n].astype(F32) + b_ref[:, 0:n]
        dpu = du_ref[...].astype(F32) * _gelu_grad(xu)
        xv = p_ref[:, n:2 * n].astype(F32) + b_ref[:, n:2 * n]
        hv = _gelu(xv)
        mu = jnp.mean(hv, axis=-1, keepdims=True)
        xc = hv - mu
        r = lax.rsqrt(jnp.mean(xc * xc, axis=-1, keepdims=True) + LN_EPS)
        vh = xc * r
        dvv = dv_ref[...].astype(F32)
        dvh = dvv * w_ref[...]
        dh = r * (dvh - jnp.mean(dvh, axis=-1, keepdims=True) - vh * jnp.mean(dvh * vh, axis=-1, keepdims=True))
        dpv = dh * _gelu_grad(xv)
        dp_ref[:, 0:n] = dpu.astype(dp_ref.dtype)
        dp_ref[:, n:2 * n] = dpv.astype(dp_ref.dtype)
        pb = jnp.concatenate([jnp.sum(dpu, axis=0, keepdims=True), jnp.sum(dpv, axis=0, keepdims=True)], axis=1)
        pw = jnp.sum(dvv * vh, axis=0, keepdims=True)
        plb = jnp.sum(dvv, axis=0, keepdims=True)

        @pl.when(pl.program_id(0) == 0)
        def _():
            db_ref[...] = pb
            dw_ref[...] = pw
            dlb_ref[...] = plb

        @pl.when(pl.program_id(0) > 0)
        def _():
            db_ref[...] += pb
            dw_ref[...] += pw
            dlb_ref[...] += plb

    half = pl.BlockSpec((tr, n), lambda i: (i, 0))
    full = pl.BlockSpec((tr, 2 * n), lambda i: (i, 0))
    vec = pl.BlockSpec((1, n), lambda i: (0, 0))
    vec2 = pl.BlockSpec((1, 2 * n), lambda i: (0, 0))
    return pl.pallas_call(
        body,
        grid=(s // tr,),
        in_specs=[full, vec2, vec, half, half],
        out_specs=[full, vec2, vec, vec],
        out_shape=[jax.ShapeDtypeStruct((s, 2 * n), BF16), jax.ShapeDtypeStruct((1, 2 * n), F32),
                   jax.ShapeDtypeStruct((1, n), F32), jax.ShapeDtypeStruct((1, n), F32)],
        compiler_params=_params(("arbitrary",)),
        name="gmlp_act_bwd",
    )(pre, b_in, ln_w, du, dv)


def _gmlp_mix_fwd(u, v, w_s, b_st):
    s = u.shape[0]
    gd = GMLP_INNER // GMLP_GROUPS

    def body(u_ref, v_ref, w_ref, b_ref, o_ref):
        li = lax.broadcasted_iota(jnp.int32, (CHUNK, CHUNK), 0)
        si = lax.broadcasted_iota(jnp.int32, (CHUNK, CHUNK), 1)
        tril = li >= si
        for g in range(GMLP_GROUPS):
            sl = slice(g * gd, (g + 1) * gd)
            wm = jnp.where(tril, w_ref[g], 0.0).astype(BF16)
            mixed = _dot(wm, v_ref[:, sl], NN) + b_ref[:, g:g + 1]
            o_ref[:, sl] = (u_ref[:, sl].astype(F32) * mixed).astype(o_ref.dtype)

    row = pl.BlockSpec((CHUNK, GMLP_INNER), lambda c: (c, 0))
    return pl.pallas_call(
        body,
        grid=(s // CHUNK,),
        in_specs=[row, row, pl.BlockSpec((GMLP_GROUPS, CHUNK, CHUNK), lambda c: (0, 0, 0)),
                  pl.BlockSpec((CHUNK, LANES), lambda c: (0, 0))],
        out_specs=row,
        out_shape=jax.ShapeDtypeStruct((s, GMLP_INNER), BF16),
        compiler_params=_params(("parallel",)),
        name="gmlp_mix_fwd",
    )(u, v, w_s, b_st)


def _gmlp_mix_bwd(dgated, u, v, w_s, b_st):
    s = u.shape[0]
    nc = s // CHUNK
    gd = GMLP_INNER // GMLP_GROUPS

    def body(dg_ref, u_ref, v_ref, w_ref, b_ref, du_ref, dv_ref, dw_ref, db_ref):
        c = pl.program_id(0)

        @pl.when(c == 0)
        def _():
            dw_ref[...] = jnp.zeros_like(dw_ref)
            db_ref[...] = jnp.zeros_like(db_ref)

        li = lax.broadcasted_iota(jnp.int32, (CHUNK, CHUNK), 0)
        si = lax.broadcasted_iota(jnp.int32, (CHUNK, CHUNK), 1)
        tril = li >= si
        lane = lax.broadcasted_iota(jnp.int32, (CHUNK, LANES), 1)
        dbacc = jnp.zeros((CHUNK, LANES), F32)
        for g in range(GMLP_GROUPS):
            sl = slice(g * gd, (g + 1) * gd)
            wm = jnp.where(tril, w_ref[g], 0.0).astype(BF16)
            vg = v_ref[:, sl]
            mixed = _dot(wm, vg, NN) + b_ref[:, g:g + 1]
            dgv = dg_ref[:, sl].astype(F32)
            du_ref[:, sl] = (dgv * mixed).astype(du_ref.dtype)
            dm = dgv * u_ref[:, sl].astype(F32)
            dm_b = dm.astype(BF16)
            dv_ref[:, sl] = _dot(wm, dm_b, TN).astype(dv_ref.dtype)
            dw_ref[g] += jnp.where(tril, _dot(dm_b, vg, NT), 0.0)
            dbacc = dbacc + jnp.where(lane == g, jnp.sum(dm, axis=1, keepdims=True), 0.0)
        db_ref[...] += dbacc

    row = pl.BlockSpec((CHUNK, GMLP_INNER), lambda c: (c, 0))
    wspec = pl.BlockSpec((GMLP_GROUPS, CHUNK, CHUNK), lambda c: (0, 0, 0))
    bspec = pl.BlockSpec((CHUNK, LANES), lambda c: (0, 0))
    return pl.pallas_call(
        body,
        grid=(nc,),
        in_specs=[row, row, row, wspec, bspec],
        out_specs=[row, row, wspec, bspec],
        out_shape=[jax.ShapeDtypeStruct((s, GMLP_INNER), BF16), jax.ShapeDtypeStruct((s, GMLP_INNER), BF16),
                   jax.ShapeDtypeStruct((GMLP_GROUPS, CHUNK, CHUNK), F32), jax.ShapeDtypeStruct((CHUNK, LANES), F32)],
        compiler_params=_params(("arbitrary",)),
        name="gmlp_mix_bwd",
    )(dgated, u, v, w_s, b_st)


FFN_HALF = FFN_DIM // 2


def _ffn_up(un, wgu):
    s, d = un.shape
    f = FFN_DIM
    tm = _pick(s, (1024, 512, 256, 128))
    nh = f // FFN_HALF

    def body(x_ref, wg_ref, wu_ref, g_ref, u_ref, h_ref):
        x = x_ref[...]
        g_ref[...] = _dot(x, wg_ref[...], NN).astype(g_ref.dtype)
        u_ref[...] = _dot(x, wu_ref[...], NN).astype(u_ref.dtype)
        gt = g_ref[...].astype(F32)
        h_ref[...] = (gt * _sigmoid(gt) * u_ref[...].astype(F32)).astype(h_ref.dtype)

    out = pl.BlockSpec((tm, FFN_HALF), lambda i, j: (i, j))
    sds = jax.ShapeDtypeStruct((s, f), BF16)
    return pl.pallas_call(
        body,
        grid=(s // tm, nh),
        in_specs=[pl.BlockSpec((tm, d), lambda i, j: (i, 0)), pl.BlockSpec((d, FFN_HALF), lambda i, j: (0, j)),
                  pl.BlockSpec((d, FFN_HALF), lambda i, j: (0, j + nh))],
        out_specs=[out, out, out],
        out_shape=[sds, sds, sds],
        compiler_params=_params(("parallel", "parallel")),
        name="ffn_up",
    )(un, wgu, wgu)


def _ffn_down_bwd(dh, wd, gate, up):
    s, d = dh.shape
    f = FFN_DIM
    tm = _pick(s, (512, 256, 128))

    def body(dh_ref, wd_ref, g_ref, u_ref, o_ref):
        dhb = dh_ref[...].astype(BF16)
        for half in range(f // FFN_HALF):
            cols = slice(half * FFN_HALF, (half + 1) * FFN_HALF)
            dhid = _dot(dhb, wd_ref[cols, :], NT)
            gt = g_ref[:, cols].astype(F32)
            sg = _sigmoid(gt)
            o_ref[:, cols] = (dhid * u_ref[:, cols].astype(F32) * (sg * (1.0 + gt * (1.0 - sg)))).astype(o_ref.dtype)
            o_ref[:, f + half * FFN_HALF:f + (half + 1) * FFN_HALF] = (dhid * gt * sg).astype(o_ref.dtype)

    row = pl.BlockSpec((tm, f), lambda i: (i, 0))
    return pl.pallas_call(
        body,
        grid=(s // tm,),
        in_specs=[pl.BlockSpec((tm, d), lambda i: (i, 0)), pl.BlockSpec((f, d), lambda i: (0, 0)), row, row],
        out_specs=pl.BlockSpec((tm, 2 * f), lambda i: (i, 0)),
        out_shape=jax.ShapeDtypeStruct((s, 2 * f), BF16),
        compiler_params=_params(("parallel",)),
        name="ffn_down_bwd",
    )(dh, wd, gate, up)


def _ple_fwd(p, h, w_proj, w_gate, gate_norm, ple_norm):
    s, d = h.shape
    e = p.shape[1]
    tr = _pick(s, (512, 256, 128))

    def body(p_ref, h_ref, wp_ref, wg_ref, gn_ref, pn_ref, o_ref, pe_ref, hg_ref, gl_ref):
        hv = h_ref[...]
        pe_ref[...] = _dot(p_ref[...].astype(BF16), wp_ref[...], NN).astype(pe_ref.dtype)
        r = lax.rsqrt(jnp.mean(hv * hv, axis=-1, keepdims=True) + RMS_EPS)
        hg_ref[...] = (hv * r * gn_ref[...]).astype(hg_ref.dtype)
        gl_ref[...] = _dot(hg_ref[...], wg_ref[...], NN).astype(gl_ref.dtype)
        pe_ = pe_ref[...].astype(F32)
        rp = lax.rsqrt(jnp.mean(pe_ * pe_, axis=-1, keepdims=True) + RMS_EPS)
        o_ref[...] = hv + _sigmoid(gl_ref[...].astype(F32)) * (pe_ * rp * pn_ref[...])

    row = pl.BlockSpec((tr, d), lambda i: (i, 0))
    vec = pl.BlockSpec((1, d), lambda i: (0, 0))
    sds = jax.ShapeDtypeStruct((s, d), BF16)
    return pl.pallas_call(
        body,
        grid=(s // tr,),
        in_specs=[pl.BlockSpec((tr, e), lambda i: (i, 0)), row, pl.BlockSpec((e, d), lambda i: (0, 0)),
                  pl.BlockSpec((d, d), lambda i: (0, 0)), vec, vec],
        out_specs=[row, row, row, row],
        out_shape=[jax.ShapeDtypeStruct((s, d), F32), sds, sds, sds],
        compiler_params=_params(("parallel",)),
        name="ple_fwd",
    )(p, h, w_proj, w_gate, gate_norm, ple_norm)


def _ple_bwd(dh, p, pe, gl, hg, h, w_gate, gate_norm, ple_norm):
    s, d = dh.shape
    e = p.shape[1]
    tr = _pick(s, (512, 256, 128))

    def body(dh_ref, p_ref, pe_ref, gl_ref, hg_ref, h_ref, wg_ref, gn_ref, pn_ref,
             dx_ref, dwg_ref, dwp_ref, dpn_ref, dgn_ref):
        pe_ = pe_ref[...].astype(F32)
        dhv = dh_ref[...]
        r = lax.rsqrt(jnp.mean(pe_ * pe_, axis=-1, keepdims=True) + RMS_EPS)
        pn = pe_ * r
        gate = _sigmoid(gl_ref[...].astype(F32))
        dgl = (dhv * (pn * pn_ref[...]) * gate * (1.0 - gate)).astype(BF16)
        de = dhv * gate
        dxh = de * pn_ref[...]
        dpe = (r * (dxh - pn * jnp.mean(dxh * pn, axis=-1, keepdims=True))).astype(BF16)
        dhg = _dot(dgl, wg_ref[...], NT)
        hv = h_ref[...]
        rh = lax.rsqrt(jnp.mean(hv * hv, axis=-1, keepdims=True) + RMS_EPS)
        hn = hv * rh
        dhh = dhg * gn_ref[...]
        dx_ref[...] = dhv + rh * (dhh - hn * jnp.mean(dhh * hn, axis=-1, keepdims=True))
        parts = (_dot(hg_ref[...], dgl, TN), _dot(p_ref[...].astype(BF16), dpe, TN),
                 jnp.sum(de * pn, axis=0, keepdims=True), jnp.sum(dhg * hn, axis=0, keepdims=True))
        accs = (dwg_ref, dwp_ref, dpn_ref, dgn_ref)

        @pl.when(pl.program_id(0) == 0)
        def _():
            for acc, part in zip(accs, parts):
                acc[...] = part

        @pl.when(pl.program_id(0) > 0)
        def _():
            for acc, part in zip(accs, parts):
                acc[...] += part

    row = pl.BlockSpec((tr, d), lambda i: (i, 0))
    vec = pl.BlockSpec((1, d), lambda i: (0, 0))
    mat = pl.BlockSpec((d, d), lambda i: (0, 0))
    small = pl.BlockSpec((e, d), lambda i: (0, 0))
    return pl.pallas_call(
        body,
        grid=(s // tr,),
        in_specs=[row, pl.BlockSpec((tr, e), lambda i: (i, 0)), row, row, row, row, mat, vec, vec],
        out_specs=[row, mat, small, vec, vec],
        out_shape=[jax.ShapeDtypeStruct((s, d), F32), jax.ShapeDtypeStruct((d, d), F32),
                   jax.ShapeDtypeStruct((e, d), F32), jax.ShapeDtypeStruct((1, d), F32),
                   jax.ShapeDtypeStruct((1, d), F32)],
        compiler_params=_params(("arbitrary",)),
        name="ple_bwd",
    )(dh, p, pe, gl, hg, h, w_gate, gate_norm, ple_norm)


def _loss_head(h, w, target):
    s, d = h.shape
    tr = _pick(s, (512, 256, 128))

    def body(h_ref, w_ref, t_ref, l_ref, dh_ref, dw_ref):
        hv = h_ref[...]
        r = lax.rsqrt(jnp.mean(hv * hv, axis=-1, keepdims=True) + RMS_EPS)
        hn = hv * r
        diff = hn * w_ref[...] - t_ref[...]
        lpart = jnp.zeros((8, LANES), F32) + (0.5 / d) * jnp.sum(jnp.sum(diff * diff, axis=1, keepdims=True), axis=0, keepdims=True)
        dy = diff * (1.0 / d)
        dxh = dy * w_ref[...]
        dh_ref[...] = r * (dxh - hn * jnp.mean(dxh * hn, axis=-1, keepdims=True))
        part = jnp.sum(dy * hn, axis=0, keepdims=True)

        @pl.when(pl.program_id(0) == 0)
        def _():
            l_ref[...] = lpart
            dw_ref[...] = part

        @pl.when(pl.program_id(0) > 0)
        def _():
            l_ref[...] += lpart
            dw_ref[...] += part

    row = pl.BlockSpec((tr, d), lambda i: (i, 0))
    vec = pl.BlockSpec((1, d), lambda i: (0, 0))
    return pl.pallas_call(
        body,
        grid=(s // tr,),
        in_specs=[row, vec, row],
        out_specs=[pl.BlockSpec((8, LANES), lambda i: (0, 0)), row, vec],
        out_shape=[jax.ShapeDtypeStruct((8, LANES), F32), jax.ShapeDtypeStruct((s, d), F32),
                   jax.ShapeDtypeStruct((1, d), F32)],
        compiler_params=_params(("arbitrary",)),
        name="loss_head",
    )(h, w, target)


PER_LAYER = ("norm_mix", "norm_ffn", "ffn_w_gu", "ffn_w_down", "ple_w_proj", "ple_norm", "ple_gate_norm", "ple_w_gate")


def _pad_lanes(v):
    return jnp.pad(v.astype(F32), (0, LANES - v.shape[0]))[None, :]


def _kernel_layouts(full):
    w = {}
    for k in ("norm_mix", "norm_ffn", "ple_norm", "ple_gate_norm", "ssd_conv_b", "ssd_norm_w", "gmlp_b_in", "gmlp_ln_w",
              "gmlp_ln_b", "gmlp_w_s"):
        w[k] = [full[k][i].astype(F32) for i in range(full[k].shape[0])]
    w["final_norm"] = full["final_norm"].astype(F32)
    n_ssd = full["ssd_w_out"].shape[0]
    if "ssd_w_in" in full:
        w["ssd_w_zx"] = [full["ssd_w_in"][j][:, :SSD_ZX].astype(BF16) for j in range(n_ssd)]
        w["ssd_w_dt"] = [jnp.pad(full["ssd_w_in"][j][:, SSD_ZX:].astype(BF16), ((0, 0), (0, LANES - SSD_HEADS)))
                         for j in range(n_ssd)]
        w["ffn_w_gu"] = [jnp.concatenate([full["ffn_w_gate"][i], full["ffn_w_up"][i]], axis=1).astype(BF16)
                         for i in range(DEPTH)]
    else:
        for k in ("ssd_w_zx", "ssd_w_dt", "ffn_w_gu"):
            w[k] = full[k]
    w["ssd_conv_w"] = [jnp.pad(full["ssd_conv_w"][j].astype(F32), ((0, 8 - CONV_K), (0, 0))) for j in range(n_ssd)]
    for k in ("ssd_dt_bias", "ssd_a_log", "ssd_d"):
        w[k] = [_pad_lanes(full[k][j]) for j in range(n_ssd)]
    w["ssd_w_out"] = [full["ssd_w_out"][j].astype(BF16) for j in range(n_ssd)]
    n_g = full["gmlp_w_in"].shape[0]
    w["gmlp_w_in"] = [full["gmlp_w_in"][j].astype(BF16) for j in range(n_g)]
    w["gmlp_w_out"] = [full["gmlp_w_out"][j].astype(BF16) for j in range(n_g)]
    w["gmlp_b_st"] = [jnp.pad(full["gmlp_b_s"][j].astype(F32).T, ((0, 0), (0, LANES - GMLP_GROUPS))) for j in range(n_g)]
    w["ffn_w_down"] =[full["ffn_w_down"][i].astype(BF16) for i in range(DEPTH)]
    w["ple_w_proj"] = [full["ple_w_proj"][i].astype(BF16) for i in range(DEPTH)]
    w["ple_w_gate"] = [full["ple_w_gate"][i].astype(BF16) for i in range(DEPTH)]
    return w


MATRICES = ("ssd_w_out", "gmlp_w_in", "gmlp_w_out", "ffn_w_down", "ple_w_proj", "ple_w_gate")


def _reference_layouts(g, wide=True):
    out = {}
    for k in ("norm_mix", "norm_ffn", "ple_norm", "ple_gate_norm", "ssd_conv_b", "ssd_norm_w", "gmlp_b_in", "gmlp_ln_w",
              "gmlp_ln_b", "gmlp_w_s", "ssd_conv_w", "ssd_dt_bias", "ssd_a_log", "ssd_d") + (MATRICES if wide else ()):
        out[k] = jnp.stack(g[k])
    out["final_norm"] = g["final_norm"]
    out["gmlp_b_s"] = jnp.stack([b[:, :GMLP_GROUPS].T for b in g["gmlp_b_st"]])
    if wide:
        out["ssd_w_in"] = jnp.stack([jnp.concatenate([zx, dt[:, :SSD_HEADS]], axis=1)
                                     for zx, dt in zip(g["ssd_w_zx"], g["ssd_w_dt"])])
        out["ffn_w_gate"] = jnp.stack([gu[:, :FFN_DIM] for gu in g["ffn_w_gu"]])
        out["ffn_w_up"] = jnp.stack([gu[:, FFN_DIM:] for gu in g["ffn_w_gu"]])
    return out


RELAYOUT_ROWS = 128
SSD_SHARD = SSD_IN_DIM // N_DEV
FFN_SHARD = FFN_DIM // N_DEV


def _to_bf16(x):
    nl, rows, n = x.shape

    def body(x_ref, o_ref):
        o_ref[...] = x_ref[...].astype(o_ref.dtype)

    blk = pl.BlockSpec((1, rows, n), lambda i: (i, 0, 0))
    return pl.pallas_call(
        body, grid=(nl,), in_specs=[blk], out_specs=blk, out_shape=jax.ShapeDtypeStruct(x.shape, BF16),
        compiler_params=_params(("parallel",)), name="to_bf16",
    )(x)


def _cat_ssd_in(gathered):
    _, nl, rows, n = gathered.shape
    tr = RELAYOUT_ROWS

    def body(g_ref, *o_refs):
        for j in range(nl):
            full = jnp.concatenate([g_ref[d, j] for d in range(N_DEV)], axis=1)
            o_refs[2 * j][...] = full[:, :SSD_ZX]
            o_refs[2 * j + 1][...] = jnp.concatenate(
                [full[:, SSD_ZX:], jnp.zeros((tr, LANES - SSD_HEADS), full.dtype)], axis=1)

    outs = pl.pallas_call(
        body, grid=(rows // tr,),
        in_specs=[pl.BlockSpec((N_DEV, nl, tr, n), lambda i: (0, 0, i, 0))],
        out_specs=[pl.BlockSpec((tr, SSD_ZX), lambda i: (i, 0)), pl.BlockSpec((tr, LANES), lambda i: (i, 0))] * nl,
        out_shape=[jax.ShapeDtypeStruct((rows, SSD_ZX), BF16), jax.ShapeDtypeStruct((rows, LANES), BF16)] * nl,
        compiler_params=_params(("parallel",)), name="cat_ssd_in",
    )(gathered)
    return [outs[2 * j] for j in range(nl)], [outs[2 * j + 1] for j in range(nl)]


def _split_ssd_in(dzx_list, ddt_list):
    nl = len(dzx_list)
    rows = dzx_list[0].shape[0]
    tr = RELAYOUT_ROWS

    def body(*refs):
        o_ref = refs[2 * nl]
        for j in range(nl):
            full = jnp.concatenate([refs[2 * j][...], refs[2 * j + 1][:, 0:SSD_HEADS]], axis=1)
            for d in range(N_DEV):
                o_ref[d, j] = full[:, d * SSD_SHARD:(d + 1) * SSD_SHARD].astype(o_ref.dtype)

    ins = []
    for j in range(nl):
        ins += [dzx_list[j], ddt_list[j]]
    return pl.pallas_call(
        body, grid=(rows // tr,),
        in_specs=[pl.BlockSpec((tr, SSD_ZX), lambda i: (i, 0)), pl.BlockSpec((tr, LANES), lambda i: (i, 0))] * nl,
        out_specs=pl.BlockSpec((N_DEV, nl, tr, SSD_SHARD), lambda i: (0, 0, i, 0)),
        out_shape=jax.ShapeDtypeStruct((N_DEV, nl, rows, SSD_SHARD), BF16),
        compiler_params=_params(("parallel",)), name="split_ssd_in",
    )(*ins)


def _cat_ffn(g_gate, g_up):
    _, nl, rows, n = g_gate.shape
    tr = RELAYOUT_ROWS

    def body(gg_ref, gu_ref, *o_refs):
        for i in range(nl):
            o_refs[i][...] = jnp.concatenate([gg_ref[d, i] for d in range(N_DEV)] + [gu_ref[d, i] for d in range(N_DEV)],
                                             axis=1)

    blk = pl.BlockSpec((N_DEV, nl, tr, n), lambda i: (0, 0, i, 0))
    outs = pl.pallas_call(
        body, grid=(rows // tr,), in_specs=[blk, blk],
        out_specs=[pl.BlockSpec((tr, 2 * FFN_DIM), lambda i: (i, 0))] * nl,
        out_shape=[jax.ShapeDtypeStruct((rows, 2 * FFN_DIM), BF16)] * nl,
        compiler_params=_params(("parallel",)), name="cat_ffn",
    )(g_gate, g_up)
    return list(outs)


def _split_ffn(dgu_list):
    nl = len(dgu_list)
    rows = dgu_list[0].shape[0]
    tr = RELAYOUT_ROWS

    def body(*refs):
        og_ref, ou_ref = refs[nl], refs[nl + 1]
        for i in range(nl):
            full = refs[i][...]
            for d in range(N_DEV):
                og_ref[d, i] = full[:, d * FFN_SHARD:(d + 1) * FFN_SHARD].astype(og_ref.dtype)
                ou_ref[d, i] = full[:, FFN_DIM + d * FFN_SHARD:FFN_DIM + (d + 1) * FFN_SHARD].astype(ou_ref.dtype)

    blk = pl.BlockSpec((N_DEV, nl, tr, FFN_SHARD), lambda i: (0, 0, i, 0))
    sds = jax.ShapeDtypeStruct((N_DEV, nl, rows, FFN_SHARD), BF16)
    return pl.pallas_call(
        body, grid=(rows // tr,),
        in_specs=[pl.BlockSpec((tr, 2 * FFN_DIM), lambda i: (i, 0))] * nl,
        out_specs=[blk, blk], out_shape=[sds, sds],
        compiler_params=_params(("parallel",)), name="split_ffn",
    )(*dgu_list)


def _local_step(x, p, target, w):
    saved = []
    h = x
    for i in range(DEPTH):
        j = i // 2
        sv = {"h0": h}
        hn = _rms_fwd(h, w["norm_mix"][i][None, :])
        sv["hn"] = hn
        if i % 2 == 0:
            zx = _mm(hn, w["ssd_w_zx"][j], "nn", BF16)
            dtp = _mm(hn, w["ssd_w_dt"][j], "nn", F32)
            xbc = _ssd_conv_fwd(zx, w["ssd_conv_w"][j], w["ssd_conv_b"][j][None, :])
            y, prev = _ssd_scan_fwd(xbc, dtp, w["ssd_dt_bias"][j], w["ssd_a_log"][j], w["ssd_d"][j])
            gn = _ssd_gate_fwd(y, zx, w["ssd_norm_w"][j][None, :])
            h = _mm(gn, w["ssd_w_out"][j], "nn", F32, add=h)
            sv.update(zx=zx, dtp=dtp, xbc=xbc, y=y, prev=prev, gn=gn)
        else:
            pre = _mm(hn, w["gmlp_w_in"][j], "nn", BF16)
            u, v = _gmlp_act_fwd(pre, w["gmlp_b_in"][j][None, :], w["gmlp_ln_w"][j][None, :], w["gmlp_ln_b"][j][None, :])
            gated = _gmlp_mix_fwd(u, v, w["gmlp_w_s"][j], w["gmlp_b_st"][j])
            h = _mm(gated, w["gmlp_w_out"][j], "nn", F32, add=h)
            sv.update(pre=pre, u=u, v=v, gated=gated)
        sv["h1"] = h
        un = _rms_fwd(h, w["norm_ffn"][i][None, :])
        gate, up, hid = _ffn_up(un, w["ffn_w_gu"][i])
        h = _mm(hid, w["ffn_w_down"][i], "nn", F32, add=h)
        sv.update(un=un, gate=gate, up=up, hid=hid, h2=h)
        h, pe, hg, gl = _ple_fwd(p[i], h, w["ple_w_proj"][i], w["ple_w_gate"][i], w["ple_gate_norm"][i][None, :],
                                 w["ple_norm"][i][None, :])
        sv.update(pe=pe, hg=hg, gl=gl)
        saved.append(sv)

    lpart, dh, d_final = _loss_head(h, w["final_norm"][None, :], target)
    g = {k: [None] * (DEPTH if k in PER_LAYER else DEPTH // 2) for k in w if k != "final_norm"}
    g["final_norm"] = d_final[0]

    for i in reversed(range(DEPTH)):
        j = i // 2
        sv = saved[i]
        dh, g["ple_w_gate"][i], g["ple_w_proj"][i], d_ple_norm, d_gate_norm = _ple_bwd(
            dh, p[i], sv["pe"], sv["gl"], sv["hg"], sv["h2"], w["ple_w_gate"][i], w["ple_gate_norm"][i][None, :],
            w["ple_norm"][i][None, :])
        g["ple_norm"][i] = d_ple_norm[0]
        g["ple_gate_norm"][i] = d_gate_norm[0]
        g["ffn_w_down"][i] = _mm(sv["hid"], dh, "tn", F32)
        dgu = _ffn_down_bwd(dh, w["ffn_w_down"][i], sv["gate"], sv["up"])
        g["ffn_w_gu"][i] = _mm(sv["un"], dgu, "tn", F32)
        dun = _mm(dgu, w["ffn_w_gu"][i], "nt", BF16)
        dh, d_norm_ffn = _rms_bwd(dun, sv["h1"], w["norm_ffn"][i][None, :], dh)
        g["norm_ffn"][i] = d_norm_ffn[0]
        if i % 2 == 0:
            dgn = _mm(dh, w["ssd_w_out"][j], "nt", BF16)
            g["ssd_w_out"][j] = _mm(sv["gn"], dh, "tn", F32)
            dy, dzx, d_norm_w = _ssd_gate_bwd(dgn, sv["y"], sv["zx"], w["ssd_norm_w"][j][None, :])
            g["ssd_norm_w"][j] = d_norm_w[0]
            dxbc, ddtp, d_bias, d_alog, d_d = _ssd_scan_bwd(sv["xbc"], sv["dtp"], sv["prev"], dy, w["ssd_dt_bias"][j],
                                                            w["ssd_a_log"][j], w["ssd_d"][j])
            g["ssd_dt_bias"][j] = d_bias[0, :SSD_HEADS]
            g["ssd_a_log"][j] = d_alog[0, :SSD_HEADS]
            g["ssd_d"][j] = d_d[0, :SSD_HEADS]
            dzx, d_conv_w, d_conv_b = _ssd_conv_bwd(sv["zx"], dxbc, w["ssd_conv_w"][j], w["ssd_conv_b"][j][None, :], dzx)
            g["ssd_conv_w"][j] = d_conv_w[:CONV_K]
            g["ssd_conv_b"][j] = d_conv_b[0]
            g["ssd_w_zx"][j] = _mm(sv["hn"], dzx, "tn", F32)
            g["ssd_w_dt"][j] = _mm(sv["hn"], ddtp, "tn", F32)
            dhn = _mm(ddtp, w["ssd_w_dt"][j], "nt", F32)
            dhn = _mm(dzx, w["ssd_w_zx"][j], "nt", BF16, add=dhn)
        else:
            dgated = _mm(dh, w["gmlp_w_out"][j], "nt", BF16)
            g["gmlp_w_out"][j] = _mm(sv["gated"], dh, "tn", F32)
            du, dv, d_ws, d_bst = _gmlp_mix_bwd(dgated, sv["u"], sv["v"], w["gmlp_w_s"][j], w["gmlp_b_st"][j])
            g["gmlp_w_s"][j] = d_ws
            g["gmlp_b_st"][j] = d_bst
            dpre, d_bin, d_lnw, d_lnb = _gmlp_act_bwd(sv["pre"], w["gmlp_b_in"][j][None, :], w["gmlp_ln_w"][j][None, :],
                                                     du, dv)
            g["gmlp_b_in"][j] = d_bin[0]
            g["gmlp_ln_w"][j] = d_lnw[0]
            g["gmlp_ln_b"][j] = d_lnb[0]
            g["gmlp_w_in"][j] = _mm(sv["hn"], dpre, "tn", F32)
            dhn = _mm(dpre, w["gmlp_w_in"][j], "nt", BF16)
        dh, d_norm_mix = _rms_bwd(dhn, sv["h0"], w["norm_mix"][i][None, :], dh)
        g["norm_mix"][i] = d_norm_mix[0]
    return lpart[0, 0], dh, g


PACK_COLS = 1024
ANY = pl.BlockSpec(memory_space=pl.ANY)


def _mesh_pos():
    return lax.axis_index("x"), lax.axis_index("y"), lax.axis_index("c")


def _all_gather(xs_list, name):
    n = len(xs_list)

    def body(*refs):
        x_refs, out_refs = refs[:n], refs[n:2 * n]
        send_sems, recv_sems, local_sems = refs[2 * n:]
        x, y, c = _mesh_pos()
        me, sibling = (x, y, c), (x, y, 1 - c)
        chips = [(1 - x, y), (x, 1 - y), (1 - x, 1 - y)]

        def copy(a, k, block, to, from_input=False):
            px, py, pc = block
            dst = out_refs[a].at[4 * px + 2 * py + pc]
            return pltpu.make_async_remote_copy(
                src_ref=x_refs[a] if from_input else dst, dst_ref=dst,
                send_sem=send_sems.at[7 * a + k], recv_sem=recv_sems.at[7 * a + k], device_id=to,
                device_id_type=MESH_ID)

        mine = [pltpu.make_async_copy(x_refs[a], out_refs[a].at[4 * x + 2 * y + c], local_sems.at[a]) for a in range(n)]
        for cp in mine:
            cp.start()
        first = []
        for a in range(n):
            first += [copy(a, 1 + j, me, (*chip, c), from_input=True) for j, chip in enumerate(chips)]
            first.append(copy(a, 0, me, sibling, from_input=True))
        for cp in first:
            cp.start()
        passed = []
        for a in range(n):
            for j, chip in enumerate(chips):
                copy(a, 1 + j, (*chip, c), me).wait_recv()
                fwd = copy(a, 4 + j, (*chip, c), sibling)
                fwd.start()
                passed.append(fwd)
        for a in range(n):
            copy(a, 0, sibling, me).wait_recv()
            for j, chip in enumerate(chips):
                copy(a, 4 + j, (*chip, 1 - c), me).wait_recv()
        for cp in first + passed:
            cp.wait_send()
        for cp in mine:
            cp.wait()

    outs = pl.pallas_call(
        body,
        out_shape=[jax.ShapeDtypeStruct((N_DEV,) + t.shape, t.dtype) for t in xs_list],
        in_specs=[ANY] * n,
        out_specs=[ANY] * n,
        scratch_shapes=[pltpu.SemaphoreType.DMA((7 * n,)), pltpu.SemaphoreType.DMA((7 * n,)),
                        pltpu.SemaphoreType.DMA((n,))],
        name=name,
    )(*xs_list)
    return list(outs)


def _exchange_sibling(send_list):
    n = len(send_list)

    def body(*refs):
        s_refs, land_refs = refs[:n], refs[n:2 * n]
        send_sems, recv_sems = refs[2 * n:]
        x, y, c = _mesh_pos()
        cps = [pltpu.make_async_remote_copy(src_ref=s_refs[a], dst_ref=land_refs[a], send_sem=send_sems.at[a],
                                            recv_sem=recv_sems.at[a], device_id=(x, y, 1 - c), device_id_type=MESH_ID)
               for a in range(n)]
        for cp in cps:
            cp.start()
        for cp in cps:
            cp.wait()

    outs = pl.pallas_call(
        body,
        out_shape=[jax.ShapeDtypeStruct(t.shape, t.dtype) for t in send_list],
        in_specs=[ANY] * n,
        out_specs=[ANY] * n,
        scratch_shapes=[pltpu.SemaphoreType.DMA((n,)), pltpu.SemaphoreType.DMA((n,))],
        name="rs_exchange_sibling",
    )(*send_list)
    return list(outs)


def _exchange_chips(partial_list):
    n = len(partial_list)

    def body(*refs):
        p_refs, land_refs = refs[:n], refs[n:2 * n]
        send_sems, recv_sems = refs[2 * n:]
        x, y, c = _mesh_pos()
        chips = [(1 - x, y), (x, 1 - y), (1 - x, 1 - y)]
        cps = [pltpu.make_async_remote_copy(src_ref=p_refs[a].at[2 * cx + cy], dst_ref=land_refs[a].at[j],
                                            send_sem=send_sems.at[3 * a + j], recv_sem=recv_sems.at[3 * a + j],
                                            device_id=(cx, cy, c), device_id_type=MESH_ID)
               for a in range(n) for j, (cx, cy) in enumerate(chips)]
        for cp in cps:
            cp.start()
        for cp in cps:
            cp.wait()

    outs = pl.pallas_call(
        body,
        out_shape=[jax.ShapeDtypeStruct((3,) + t.shape[1:], t.dtype) for t in partial_list],
        in_specs=[ANY] * n,
        out_specs=[ANY] * n,
        scratch_shapes=[pltpu.SemaphoreType.DMA((3 * n,)), pltpu.SemaphoreType.DMA((3 * n,))],
        name="rs_exchange_chips",
    )(*partial_list)
    return list(outs)


def _sum_pairs(a, b):
    shape = a.shape
    a = a.reshape(shape[0], -1, shape[-1])
    b = b.reshape(a.shape)
    n, r, cdim = a.shape
    tr = _pick(r, (1024, 512, 256, 128, 64))

    def body(a_ref, b_ref, o_ref):
        o_ref[...] = (a_ref[...].astype(F32) + b_ref[...].astype(F32)).astype(o_ref.dtype)

    blk = pl.BlockSpec((1, tr, cdim), lambda i, j: (i, j, 0))
    return pl.pallas_call(
        body, grid=(n, r // tr), in_specs=[blk, blk], out_specs=blk,
        out_shape=jax.ShapeDtypeStruct(a.shape, a.dtype),
        compiler_params=_params(("parallel", "parallel")), name="rs_sum_pairs",
    )(a, b).reshape(shape)


def _sum_final(own, land):
    shape = own.shape
    own = own.reshape(-1, shape[-1])
    land = land.reshape((3,) + own.shape)
    r, cdim = own.shape
    tr = _pick(r, (1024, 512, 256, 128, 64))

    def body(o_ref, l_ref, out_ref):
        acc = o_ref[...].astype(F32)
        for j in range(3):
            acc = acc + l_ref[j].astype(F32)
        out_ref[...] = acc

    return pl.pallas_call(
        body, grid=(r // tr,),
        in_specs=[pl.BlockSpec((tr, cdim), lambda i: (i, 0)), pl.BlockSpec((3, tr, cdim), lambda i: (0, i, 0))],
        out_specs=pl.BlockSpec((tr, cdim), lambda i: (i, 0)),
        out_shape=jax.ShapeDtypeStruct((r, cdim), F32),
        compiler_params=_params(("parallel",)), name="rs_sum_final",
    )(own, land).reshape(shape)


def _sum_devices(gathered):
    n, r, cdim = gathered.shape
    tr = _pick(r, (64, 32, 16, 8))

    def body(g_ref, out_ref):
        acc = g_ref[0].astype(F32)
        for q in range(1, n):
            acc = acc + g_ref[q].astype(F32)
        out_ref[...] = acc

    return pl.pallas_call(
        body, grid=(r // tr,),
        in_specs=[pl.BlockSpec((n, tr, cdim), lambda i: (0, i, 0))],
        out_specs=pl.BlockSpec((tr, cdim), lambda i: (i, 0)),
        out_shape=jax.ShapeDtypeStruct((r, cdim), F32),
        compiler_params=_params(("parallel",)), name="sum_devices",
    )(gathered)


def _adamw(w, g, m, v):
    shape = w.shape
    cols = shape[-1]
    rows = w.size // cols
    tr = _pick(rows, (512, 256, 128, 64, 32, 16, 8))
    c1 = 1.0 - ADAM_B1 ** ADAM_STEP
    c2 = 1.0 - ADAM_B2 ** ADAM_STEP

    def body(w_ref, g_ref, m_ref, v_ref, d_ref, nm_ref, nv_ref):
        gv = g_ref[...]
        m2 = ADAM_B1 * m_ref[...] + (1.0 - ADAM_B1) * gv
        v2 = ADAM_B2 * v_ref[...] + (1.0 - ADAM_B2) * (gv * gv)
        d_ref[...] = -ADAM_LR * ((m2 / c1) / (jnp.sqrt(v2 / c2) + ADAM_EPS) + ADAM_WD * w_ref[...])
        nm_ref[...] = m2
        nv_ref[...] = v2

    blk = pl.BlockSpec((tr, cols), lambda i: (i, 0))
    sds = jax.ShapeDtypeStruct((rows, cols), F32)
    outs = pl.pallas_call(
        body, grid=(rows // tr,), in_specs=[blk] * 4, out_specs=[blk] * 3, out_shape=[sds] * 3,
        compiler_params=_params(("parallel",)), name=f"adamw_{rows}x{cols}",
    )(*(t.reshape(rows, cols) for t in (w, g, m, v)))
    return tuple(o.reshape(shape) for o in outs)


WEIGHTS = ("norm_mix", "norm_ffn", "ssd_w_in", "ssd_conv_w", "ssd_conv_b", "ssd_dt_bias", "ssd_a_log", "ssd_d",
           "ssd_norm_w", "ssd_w_out", "gmlp_w_in", "gmlp_b_in", "gmlp_ln_w", "gmlp_ln_b", "gmlp_w_s", "gmlp_b_s",
           "gmlp_w_out", "ffn_w_gate", "ffn_w_up", "ffn_w_down", "ple_w_proj", "ple_norm", "ple_gate_norm",
           "ple_w_gate", "final_norm")
ARG_NAMES = ("x", "p") + WEIGHTS + ("loss_target",) + tuple("m_" + n for n in WEIGHTS) + tuple("v_" + n for n in WEIGHTS)
SHARD_AXIS = {"ssd_w_in": 2, "ssd_conv_w": 2, "ssd_w_out": 1, "gmlp_w_in": 2, "gmlp_b_in": 1, "gmlp_ln_w": 1,
              "gmlp_ln_b": 1, "gmlp_w_out": 1, "ffn_w_gate": 2, "ffn_w_up": 2, "ffn_w_down": 1, "ple_w_proj": 2,
              "ple_w_gate": 1}
GATHER_BF16 = ("ssd_w_in", "ssd_w_out", "gmlp_w_in", "gmlp_w_out", "ffn_w_gate", "ffn_w_up", "ffn_w_down",
               "ple_w_proj", "ple_w_gate")
GATHER_F32 = ("ssd_conv_w", "gmlp_b_in", "gmlp_ln_w", "gmlp_ln_b")
SHARDED = GATHER_BF16 + GATHER_F32
WIDE = ("ssd_w_in", "ffn_w_gate", "ffn_w_up")
REPLICATED = tuple(n for n in WEIGHTS if n not in SHARD_AXIS)


def _pack(arrs, dtype, row_mult, lead=0):
    flat = jnp.concatenate([t.reshape(t.shape[:lead] + (-1,)).astype(dtype) for t in arrs], axis=lead)
    n = flat.shape[-1]
    unit = row_mult * PACK_COLS
    total = -(-n // unit) * unit
    flat = jnp.pad(flat, [(0, 0)] * lead + [(0, total - n)])
    return flat.reshape(flat.shape[:lead] + (total // PACK_COLS, PACK_COLS))


def _unpack(buf, names, shapes, lead=0):
    flat = buf.reshape(buf.shape[:lead] + (-1,))
    out, off = {}, 0
    for n in names:
        size = math.prod(shapes[n])
        out[n] = lax.slice_in_dim(flat, off, off + size, axis=lead).reshape(buf.shape[:lead] + tuple(shapes[n]))
        off += size
    return out


ROW_PACKED = ((1024, ("ssd_w_out", "gmlp_w_out", "ffn_w_down", "ple_w_gate")), (512, ("gmlp_w_in",)),
              (128, ("ple_w_proj",)))
ROW_PACK_MULT = 1024


def _pack_rows(arrs, width, lead=0):
    parts = [t.reshape(t.shape[:lead] + (-1, width)).astype(BF16) for t in arrs]
    rows = sum(t.shape[lead] for t in parts)
    pad = -rows % ROW_PACK_MULT
    if pad:
        parts.append(jnp.zeros(parts[0].shape[:lead] + (pad, width), BF16))
    return jnp.concatenate(parts, axis=lead)


def _unpack_rows(buf, names, shapes, lead=0):
    width = buf.shape[-1]
    out, off = {}, 0
    for n in names:
        rows = math.prod(shapes[n]) // width
        out[n] = lax.slice_in_dim(buf, off, off + rows, axis=lead).reshape(buf.shape[:lead] + tuple(shapes[n]))
        off += rows
    return out


def _merge_shards(seg, ax):
    t = jnp.moveaxis(seg, 0, ax)
    return t.reshape(t.shape[:ax] + (t.shape[ax] * t.shape[ax + 1],) + t.shape[ax + 2:])


def _split_for_cores(gfull, ax, c):
    shp = gfull.shape
    t = gfull.reshape(shp[:ax] + (2, 2, 2, shp[ax] // N_DEV) + shp[ax + 1:])

    def take(core):
        u = lax.dynamic_index_in_dim(t, core, axis=ax + 2, keepdims=False)
        u = jnp.moveaxis(u, (ax, ax + 1), (0, 1))
        return u.reshape((4,) + u.shape[2:])

    return take(c), take(1 - c)


def kernel(x, p, norm_mix, norm_ffn, ssd_w_in, ssd_conv_w, ssd_conv_b, ssd_dt_bias, ssd_a_log, ssd_d,
           ssd_norm_w, ssd_w_out, gmlp_w_in, gmlp_b_in, gmlp_ln_w, gmlp_ln_b, gmlp_w_s, gmlp_b_s,
           gmlp_w_out, ffn_w_gate, ffn_w_up, ffn_w_down, ple_w_proj, ple_norm, ple_gate_norm, ple_w_gate,
           final_norm, loss_target, m_norm_mix, m_norm_ffn, m_ssd_w_in, m_ssd_conv_w, m_ssd_conv_b,
           m_ssd_dt_bias, m_ssd_a_log, m_ssd_d, m_ssd_norm_w, m_ssd_w_out, m_gmlp_w_in, m_gmlp_b_in,
           m_gmlp_ln_w, m_gmlp_ln_b, m_gmlp_w_s, m_gmlp_b_s, m_gmlp_w_out, m_ffn_w_gate, m_ffn_w_up,
           m_ffn_w_down, m_ple_w_proj, m_ple_norm, m_ple_gate_norm, m_ple_w_gate, m_final_norm, v_norm_mix,
           v_norm_ffn, v_ssd_w_in, v_ssd_conv_w, v_ssd_conv_b, v_ssd_dt_bias, v_ssd_a_log, v_ssd_d,
           v_ssd_norm_w, v_ssd_w_out, v_gmlp_w_in, v_gmlp_b_in, v_gmlp_ln_w, v_gmlp_ln_b, v_gmlp_w_s,
           v_gmlp_b_s, v_gmlp_w_out, v_ffn_w_gate, v_ffn_w_up, v_ffn_w_down, v_ple_w_proj, v_ple_norm,
           v_ple_gate_norm, v_ple_w_gate, v_final_norm):
    given = locals()
    a = {n: given[n] for n in ARG_NAMES}
    mx, my, c = _mesh_pos()
    xs = a["x"][0]
    ps = a["p"][:, 0]
    target = a["loss_target"][0]
    shard_shapes = {n: a[n].shape for n in WEIGHTS}

    full = {n: a[n] for n in REPLICATED}
    row_packs = [_pack_rows([a[n] for n in names], wd) for wd, names in ROW_PACKED]
    got = _all_gather(row_packs + [_pack([a[n] for n in GATHER_F32], F32, 8)] + [_to_bf16(a[n]) for n in WIDE],
                      "ag_weights")
    for (wd, names), buf in zip(ROW_PACKED, got):
        for n, seg in _unpack_rows(buf, names, shard_shapes, lead=1).items():
            full[n] = _merge_shards(seg, SHARD_AXIS[n])
    k0 = len(ROW_PACKED)
    for n, seg in _unpack(got[k0], GATHER_F32, shard_shapes, lead=1).items():
        full[n] = _merge_shards(seg, SHARD_AXIS[n])
    full["ssd_w_zx"], full["ssd_w_dt"] = _cat_ssd_in(got[k0 + 1])
    full["ffn_w_gu"] = _cat_ffn(got[k0 + 2], got[k0 + 3])

    lpart, dx, g = _local_step(xs, ps, target, _kernel_layouts(full))
    gfull = _reference_layouts(g, wide=False)
    loss = lax.psum(lpart, ("x", "y", "c"))

    def by_core(t):
        u = t.reshape((4, 2) + t.shape[1:])
        return (lax.dynamic_index_in_dim(u, c, axis=1, keepdims=False),
                lax.dynamic_index_in_dim(u, 1 - c, axis=1, keepdims=False))

    def layer_halves(gl, ax):
        if ax == 0:
            t = gl.reshape(4, 2, -1, gl.shape[-1])
            return tuple(lax.dynamic_index_in_dim(t, cc, axis=1, keepdims=False) for cc in (c, 1 - c))
        t = gl.reshape(gl.shape[0], 4, 2, -1)
        return tuple(jnp.moveaxis(lax.dynamic_index_in_dim(t, cc, axis=2, keepdims=False), 1, 0) for cc in (c, 1 - c))

    pairs = []
    for wd, names in ROW_PACKED:
        hs = [layer_halves(gl, SHARD_AXIS[n] - 1) for n in names for gl in g[n]]
        pairs.append(tuple(_pack_rows([h[i] for h in hs], wd, lead=1) for i in (0, 1)))
    halves = [_split_for_cores(gfull[n], SHARD_AXIS[n], c) for n in GATHER_F32]
    pairs.append((_pack([h[0] for h in halves], BF16, 16, lead=1), _pack([h[1] for h in halves], BF16, 16, lead=1)))
    pairs += [by_core(t) for t in (_split_ssd_in(g["ssd_w_zx"], g["ssd_w_dt"]),) + tuple(_split_ffn(g["ffn_w_gu"]))]
    landed = _exchange_sibling([s for _, s in pairs])
    partials = [_sum_pairs(k, l) for (k, _), l in zip(pairs, landed)]
    landed = _exchange_chips(partials)
    sums = [_sum_final(lax.dynamic_index_in_dim(t, 2 * mx + my, axis=0, keepdims=False), l)
            for t, l in zip(partials, landed)]
    gshard = {}
    for (wd, names), buf in zip(ROW_PACKED, sums):
        gshard.update(_unpack_rows(buf, names, shard_shapes))
    gshard.update(_unpack(sums[k0], GATHER_F32, shard_shapes))
    gshard.update(zip(WIDE, sums[k0 + 1:]))
    rep = _all_gather([_pack([gfull[n] for n in REPLICATED], BF16, 64)], "ag_replicated_grads")[0]
    grep = _unpack(_sum_devices(rep), REPLICATED, shard_shapes)
    grads = {**gshard, **grep}

    upd = {n: _adamw(a[n], grads[n], a["m_" + n], a["v_" + n]) for n in WEIGHTS}
    return (loss, dx[None], *[grads[n] for n in WEIGHTS], *[upd[n][0] for n in WEIGHTS],
            *[upd[n][1] for n in WEIGHTS], *[upd[n][2] for n in WEIGHTS])
```

```python
import math

import jax
import jax.numpy as jnp
from jax import lax
from jax.experimental import pallas as pl
from jax.experimental.pallas import tpu as pltpu

F32 = jnp.float32
BF16 = jnp.bfloat16

N_DEV = 8
D_MODEL = 1024
DEPTH = 4
SSD_INNER = 2048
SSD_HEADS = 32
SSD_HEADDIM = 64
SSD_GROUPS = 8
SSD_STATE = 128
SSD_GROUP_W = SSD_INNER // SSD_GROUPS
SSD_CONV_DIM = SSD_INNER + 2 * SSD_GROUPS * SSD_STATE
SSD_IN_DIM = 2 * SSD_INNER + SSD_CONV_DIM - SSD_INNER + SSD_HEADS
SSD_ZX = SSD_INNER + SSD_CONV_DIM
CONV_K = 4
CHUNK = 128
GMLP_INNER = 2048
GMLP_GROUPS = 16
FFN_DIM = 2816
PLE_DIM = 256
RMS_EPS = 1e-6
LN_EPS = 1e-5
LANES = 128
VMEM_LIMIT = 56 * 1024 * 1024

ADAM_LR = 0.001
ADAM_B1 = 0.9
ADAM_B2 = 0.999
ADAM_EPS = 1e-08
ADAM_WD = 0.01
ADAM_STEP = 10

MESH_ID = pl.DeviceIdType.MESH


def _pick(n, cands):
    for c in cands:
        if c <= n and n % c == 0:
            return c
    return n


def _params(dims):
    return pltpu.CompilerParams(dimension_semantics=dims, vmem_limit_bytes=VMEM_LIMIT)


def _dot(a, b, dims=(((1,), (0,)), ((), ())), precision=None):
    return lax.dot_general(a, b, dims, precision=precision, preferred_element_type=F32)


NN = (((1,), (0,)), ((), ()))
NT = (((1,), (1,)), ((), ()))
TN = (((0,), (0,)), ((), ()))


def _sigmoid(x):
    return 1.0 / (1.0 + jnp.exp(-x))


def _dot01(a, b, dims, split, terms=3):
    v = (a, b)[split]
    ones = (a, b)[1 - split].astype(BF16)
    acc = None
    for _ in range(terms):
        piece = v.astype(BF16)
        v = v - piece.astype(F32)
        part = _dot(piece, ones, dims) if split == 0 else _dot(ones, piece, dims)
        acc = part if acc is None else acc + part
    return acc


MM_VMEM_BUDGET = 36 * 1024 * 1024


def _mm_tiles(mode, m, n, k, a_bytes, b_bytes, out_bytes, has_add):
    tm = _pick(m, (1408, 1024, 512, 256, 128))
    tn_cands = [c for c in (2816, 1024, 512, 256, 128) if c <= n and n % c == 0] or [n]
    tk_cands = [k] + [c for c in (2816, 2048, 1024, 512, 256, 128) if c < k and k % c == 0]
    for tk in tk_cands:
        for tn in tn_cands:
            blocks = tm * tk * a_bytes + tk * tn * b_bytes + tm * tn * (out_bytes + (4 if has_add else 0))
            if 2 * blocks + (tm * tn * 4 if tk < k else 0) <= MM_VMEM_BUDGET:
                return tm, tn, tk
    return tm, tn_cands[-1], tk_cands[-1]


def _mm(a, b, mode, out_dtype, add=None):
    if mode == "nn":
        m, k = a.shape
        n = b.shape[1]
    elif mode == "nt":
        m, k = a.shape
        n = b.shape[0]
    else:
        k, m = a.shape
        n = b.shape[1]
    tm, tn, tk = _mm_tiles(mode, m, n, k, a.dtype.itemsize, b.dtype.itemsize, jnp.dtype(out_dtype).itemsize,
                           add is not None)
    nk = k // tk
    dims = {"nn": NN, "nt": NT, "tn": TN}[mode]

    def body(*refs):
        if add is None:
            a_ref, b_ref, o_ref = refs[:3]
            add_ref = None
            rest = refs[3:]
        else:
            a_ref, b_ref, add_ref, o_ref = refs[:4]
            rest = refs[4:]
        part = _dot(a_ref[...].astype(BF16), b_ref[...].astype(BF16), dims)

        def finish(acc):
            if add_ref is not None:
                acc = acc + add_ref[...]
            o_ref[...] = acc.astype(o_ref.dtype)

        if nk == 1:
            finish(part)
        else:
            acc_ref = rest[0]
            kk = pl.program_id(2)

            @pl.when(kk == 0)
            def _():
                acc_ref[...] = part

            @pl.when(kk > 0)
            def _():
                acc_ref[...] += part

            @pl.when(kk == nk - 1)
            def _():
                finish(acc_ref[...])

    if mode == "nn":
        a_spec = pl.BlockSpec((tm, tk), lambda i, j, kk: (i, kk))
        b_spec = pl.BlockSpec((tk, tn), lambda i, j, kk: (kk, j))
    elif mode == "nt":
        a_spec = pl.BlockSpec((tm, tk), lambda i, j, kk: (i, kk))
        b_spec = pl.BlockSpec((tn, tk), lambda i, j, kk: (j, kk))
    else:
        a_spec = pl.BlockSpec((tk, tm), lambda i, j, kk: (kk, i))
        b_spec = pl.BlockSpec((tk, tn), lambda i, j, kk: (kk, j))
    o_spec = pl.BlockSpec((tm, tn), lambda i, j, kk: (i, j))
    in_specs = [a_spec, b_spec] + ([o_spec] if add is not None else [])
    args = (a, b) + ((add,) if add is not None else ())
    return pl.pallas_call(
        body,
        grid=(m // tm, n // tn, nk),
        in_specs=in_specs,
        out_specs=o_spec,
        out_shape=jax.ShapeDtypeStruct((m, n), out_dtype),
        scratch_shapes=[pltpu.VMEM((tm, tn), F32)] if nk > 1 else [],
        compiler_params=_params(("parallel", "parallel", "arbitrary")),
        name=f"mm_{mode}_{m}x{k}x{n}",
    )(*args)


def _rms_fwd(x, w):
    s, d = x.shape
    tr = _pick(s, (512, 256, 128))

    def body(x_ref, w_ref, o_ref):
        xv = x_ref[...]
        r = lax.rsqrt(jnp.mean(xv * xv, axis=-1, keepdims=True) + RMS_EPS)
        o_ref[...] = (xv * r * w_ref[...]).astype(o_ref.dtype)

    return pl.pallas_call(
        body,
        grid=(s // tr,),
        in_specs=[pl.BlockSpec((tr, d), lambda i: (i, 0)), pl.BlockSpec((1, d), lambda i: (0, 0))],
        out_specs=pl.BlockSpec((tr, d), lambda i: (i, 0)),
        out_shape=jax.ShapeDtypeStruct((s, d), BF16),
        compiler_params=_params(("parallel",)),
        name="rms_fwd",
    )(x, w)


def _rms_bwd(dyn, x, w, add):
    s, d = x.shape
    tr = _pick(s, (512, 256, 128))

    def body(dy_ref, x_ref, w_ref, add_ref, dx_ref, dw_ref):
        xv = x_ref[...]
        dy = dy_ref[...].astype(F32)
        r = lax.rsqrt(jnp.mean(xv * xv, axis=-1, keepdims=True) + RMS_EPS)
        xn = xv * r
        dxh = dy * w_ref[...]
        dx = r * (dxh - xn * jnp.mean(dxh * xn, axis=-1, keepdims=True))
        dx_ref[...] = add_ref[...] + dx
        part = jnp.sum(dy * xn, axis=0, keepdims=True)

        @pl.when(pl.program_id(0) == 0)
        def _():
            dw_ref[...] = part

        @pl.when(pl.program_id(0) > 0)
        def _():
            dw_ref[...] += part

    row = pl.BlockSpec((tr, d), lambda i: (i, 0))
    vec = pl.BlockSpec((1, d), lambda i: (0, 0))
    return pl.pallas_call(
        body,
        grid=(s // tr,),
        in_specs=[row, row, vec, row],
        out_specs=[row, vec],
        out_shape=[jax.ShapeDtypeStruct((s, d), F32), jax.ShapeDtypeStruct((1, d), F32)],
        compiler_params=_params(("arbitrary",)),
        name="rms_bwd",
    )(dyn, x, w, add)


CONV_ROWS = 256
CONV_COLS = 256
CONV_HALO = 16


def _conv_taps(ext, w, base, rows):
    acc = w[0:1, :] * ext[base:base + rows]
    for k in range(1, CONV_K):
        acc = acc + w[k:k + 1, :] * ext[base + k:base + k + rows]
    return acc


def _ssd_conv_fwd(zx, conv_w, conv_b):
    s = zx.shape[0]
    c = SSD_CONV_DIM
    nsteps = s // CONV_ROWS
    off = SSD_INNER // CONV_COLS

    def body(x_ref, w_ref, b_ref, o_ref):
        w = w_ref[...]
        b = b_ref[...]

        def step(i, carry):
            r0 = pl.multiple_of(i * CONV_ROWS, CONV_ROWS)
            cur = x_ref[pl.ds(r0, CONV_ROWS), :].astype(F32)
            p0 = pl.multiple_of(jnp.maximum(r0 - CONV_HALO, 0), CONV_HALO)
            prev = x_ref[pl.ds(p0, CONV_HALO), :].astype(F32)
            prev = jnp.where(i == 0, 0.0, prev)
            ext = jnp.concatenate([prev, cur], axis=0)
            acc = _conv_taps(ext, w, CONV_HALO - (CONV_K - 1), CONV_ROWS) + b
            o_ref[pl.ds(r0, CONV_ROWS), :] = (acc * _sigmoid(acc)).astype(o_ref.dtype)
            return carry

        lax.fori_loop(0, nsteps, step, 0)

    return pl.pallas_call(
        body,
        grid=(c // CONV_COLS,),
        in_specs=[pl.BlockSpec((s, CONV_COLS), lambda j: (0, j + off)),
                  pl.BlockSpec((8, CONV_COLS), lambda j: (0, j)),
                  pl.BlockSpec((1, CONV_COLS), lambda j: (0, j))],
        out_specs=pl.BlockSpec((s, CONV_COLS), lambda j: (0, j)),
        out_shape=jax.ShapeDtypeStruct((s, c), BF16),
        compiler_params=_params(("parallel",)),
        name="ssd_conv_fwd",
    )(zx, conv_w, conv_b)


def _ssd_conv_bwd(zx, dxbc, conv_w, conv_b, dzx):
    s = zx.shape[0]
    c = SSD_CONV_DIM
    nsteps = s // CONV_ROWS
    off = SSD_INNER // CONV_COLS

    def body(x_ref, dy_ref, w_ref, b_ref, dzx_in_ref, dx_ref, dw_ref, db_ref, dc_ref):
        w = w_ref[...]
        b = b_ref[...]
        dc_ref[pl.ds(s, CONV_HALO), :] = jnp.zeros((CONV_HALO, CONV_COLS), F32)

        def step1(i, carry):
            dw0, dw1, dw2, dw3, dbs = carry
            r0 = pl.multiple_of(i * CONV_ROWS, CONV_ROWS)
            cur = x_ref[pl.ds(r0, CONV_ROWS), :].astype(F32)
            p0 = pl.multiple_of(jnp.maximum(r0 - CONV_HALO, 0), CONV_HALO)
            prev = x_ref[pl.ds(p0, CONV_HALO), :].astype(F32)
            prev = jnp.where(i == 0, 0.0, prev)
            ext = jnp.concatenate([prev, cur], axis=0)
            base = CONV_HALO - (CONV_K - 1)
            acc = _conv_taps(ext, w, base, CONV_ROWS) + b
            sg = _sigmoid(acc)
            dcv = dy_ref[pl.ds(r0, CONV_ROWS), :].astype(F32) * (sg * (1.0 + acc * (1.0 - sg)))
            dc_ref[pl.ds(r0, CONV_ROWS), :] = dcv
            dws = [jnp.sum(dcv * ext[base + k:base + k + CONV_ROWS], axis=0, keepdims=True) for k in range(CONV_K)]
            return (dw0 + dws[0], dw1 + dws[1], dw2 + dws[2], dw3 + dws[3], dbs + jnp.sum(dcv, axis=0, keepdims=True))

        z = jnp.zeros((1, CONV_COLS), F32)
        dw0, dw1, dw2, dw3, dbs = lax.fori_loop(0, nsteps, step1, (z, z, z, z, z))
        dw_ref[...] = jnp.concatenate([dw0, dw1, dw2, dw3, z, z, z, z], axis=0)
        db_ref[...] = dbs

        def step2(i, carry):
            r0 = pl.multiple_of(i * CONV_ROWS, CONV_ROWS)
            ext = dc_ref[pl.ds(r0, CONV_ROWS + CONV_HALO), :]
            acc = w[0:1, :] * ext[CONV_K - 1:CONV_K - 1 + CONV_ROWS]
            for k in range(1, CONV_K):
                acc = acc + w[k:k + 1, :] * ext[CONV_K - 1 - k:CONV_K - 1 - k + CONV_ROWS]
            dx_ref[pl.ds(r0, CONV_ROWS), :] = acc.astype(dx_ref.dtype)
            return carry

        lax.fori_loop(0, nsteps, step2, 0)

    col = pl.BlockSpec((s, CONV_COLS), lambda j: (0, j))
    shifted = pl.BlockSpec((s, CONV_COLS), lambda j: (0, j + off))
    return pl.pallas_call(
        body,
        grid=(c // CONV_COLS,),
        in_specs=[shifted, col,
                  pl.BlockSpec((8, CONV_COLS), lambda j: (0, j)),
                  pl.BlockSpec((1, CONV_COLS), lambda j: (0, j)),
                  pl.BlockSpec(memory_space=pl.ANY)],
        out_specs=[shifted, pl.BlockSpec((8, CONV_COLS), lambda j: (0, j)), pl.BlockSpec((1, CONV_COLS), lambda j: (0, j))],
        out_shape=[jax.ShapeDtypeStruct((s, SSD_ZX), BF16), jax.ShapeDtypeStruct((8, c), F32),
                   jax.ShapeDtypeStruct((1, c), F32)],
        scratch_shapes=[pltpu.VMEM((s + CONV_HALO, CONV_COLS), F32)],
        input_output_aliases={4: 0},
        compiler_params=_params(("parallel",)),
        name="ssd_conv_bwd",
    )(zx, dxbc, conv_w, conv_b, dzx)


def _ssd_consts():
    li = lax.broadcasted_iota(jnp.int32, (CHUNK, CHUNK), 0)
    si = lax.broadcasted_iota(jnp.int32, (CHUNK, CHUNK), 1)
    tril = li >= si
    hrow = lax.broadcasted_iota(jnp.int32, (LANES, SSD_INNER), 0)
    hcol = lax.broadcasted_iota(jnp.int32, (LANES, SSD_INNER), 1) // SSD_HEADDIM
    expand = (hrow == hcol).astype(F32)
    return tril, expand


def _ssd_chunk_common(dtp_ref, bias_ref, alog_ref, tril, expand):
    lane = lax.broadcasted_iota(jnp.int32, (1, LANES), 1)
    valid = lane < SSD_HEADS
    pre = dtp_ref[...] + bias_ref[...]
    dt = jnp.where(valid, jnp.maximum(pre, 0.0) + jnp.log1p(jnp.exp(-jnp.abs(pre))), 0.0)
    a = jnp.where(valid, -jnp.exp(alog_ref[...]), 0.0)
    da = dt * a
    cs = _dot01(tril.astype(F32), da, NN, 1)
    cs_x = _dot01(cs, expand, NN, 0)
    dt_x = _dot01(dt, expand, NN, 0, terms=2)
    return pre, dt, a, cs, cs_x, dt_x


def _ssd_scan_fwd(xbc, dtp, dt_bias, a_log, d_skip):
    s = xbc.shape[0]
    nc = s // CHUNK
    gw = SSD_GROUP_W

    def body(xbc_ref, dtp_ref, bias_ref, alog_ref, d_ref, y_ref, prev_ref, state_ref):
        c = pl.program_id(0)

        @pl.when(c == 0)
        def _():
            state_ref[...] = jnp.zeros_like(state_ref)

        tril, expand = _ssd_consts()
        pre, dt, a, cs, cs_x, dt_x = _ssd_chunk_common(dtp_ref, bias_ref, alog_ref, tril, expand)
        cs_t = cs.T
        d_x = _dot01(jnp.broadcast_to(d_ref[...], (8, LANES)), expand, NN, 0)[0:1, :]
        cs_last = cs_x[CHUNK - 1:CHUNK, :]
        dec_out = jnp.exp(cs_x)
        dec_st = jnp.exp(cs_last - cs_x)
        dec_ch = jnp.exp(cs_last)
        x = xbc_ref[:, 0:SSD_INNER].astype(F32)
        xr = x * dt_x
        xrs = xr * dec_st
        lane_g = lax.broadcasted_iota(jnp.int32, (1, gw), 1) // SSD_HEADDIM
        for g in range(SSD_GROUPS):
            sl = slice(g * gw, (g + 1) * gw)
            bg = xbc_ref[:, SSD_INNER + g * SSD_STATE:SSD_INNER + (g + 1) * SSD_STATE]
            cg = xbc_ref[:, SSD_INNER + (SSD_GROUPS + g) * SSD_STATE:SSD_INNER + (SSD_GROUPS + g + 1) * SSD_STATE]
            cb = _dot(cg, bg, NT)
            prev_g = state_ref[:, sl]
            prev_ref[0, :, sl] = prev_g
            yo = _dot(cg, prev_g.astype(BF16), NN) * dec_out[:, sl]
            xr_g = xr[:, sl]
            yd = jnp.zeros((CHUNK, gw), F32)
            for r in range(SSD_HEADS // SSD_GROUPS):
                h = g * (SSD_HEADS // SSD_GROUPS) + r
                diff = cs[:, h:h + 1] - cs_t[h:h + 1, :]
                lmat = jnp.exp(jnp.where(tril, diff, -1e30))
                wmat = (cb * lmat).astype(BF16)
                xr_h = jnp.where(lane_g == r, xr_g, 0.0).astype(BF16)
                yd = yd + _dot(wmat, xr_h, NN)
            y_ref[:, sl] = yd + yo + x[:, sl] * d_x[:, sl]
            sc = _dot(bg, xrs[:, sl].astype(BF16), TN)
            state_ref[:, sl] = prev_g * dec_ch[:, sl] + sc

    vec = pl.BlockSpec((1, LANES), lambda c: (0, 0))
    return pl.pallas_call(
        body,
        grid=(nc,),
        in_specs=[pl.BlockSpec((CHUNK, SSD_CONV_DIM), lambda c: (c, 0)),
                  pl.BlockSpec((CHUNK, LANES), lambda c: (c, 0)), vec, vec, vec],
        out_specs=[pl.BlockSpec((CHUNK, SSD_INNER), lambda c: (c, 0)),
                   pl.BlockSpec((1, SSD_STATE, SSD_INNER), lambda c: (c, 0, 0))],
        out_shape=[jax.ShapeDtypeStruct((s, SSD_INNER), F32), jax.ShapeDtypeStruct((nc, SSD_STATE, SSD_INNER), F32)],
        scratch_shapes=[pltpu.VMEM((SSD_STATE, SSD_INNER), F32)],
        compiler_params=_params(("arbitrary",)),
        name="ssd_scan_fwd",
    )(xbc, dtp, dt_bias, a_log, d_skip)


def _ssd_scan_bwd(xbc, dtp, prev, dy, dt_bias, a_log, d_skip):
    s = xbc.shape[0]
    nc = s // CHUNK
    gw = SSD_GROUP_W
    hpg = SSD_HEADS // SSD_GROUPS

    def body(xbc_ref, dtp_ref, prev_ref, dy_ref, bias_ref, alog_ref, d_ref,
             dxbc_ref, ddtp_ref, dbias_ref, dalog_ref, dd_ref, dp_ref, ddx_ref):
        step = pl.program_id(0)

        @pl.when(step == 0)
        def _():
            dp_ref[...] = jnp.zeros_like(dp_ref)
            ddx_ref[...] = jnp.zeros_like(ddx_ref)
            dbias_ref[...] = jnp.zeros_like(dbias_ref)
            dalog_ref[...] = jnp.zeros_like(dalog_ref)

        tril, expand = _ssd_consts()
        pre, dt, a, cs, cs_x, dt_x = _ssd_chunk_common(dtp_ref, bias_ref, alog_ref, tril, expand)
        cs_t = cs.T
        d_x = _dot01(jnp.broadcast_to(d_ref[...], (8, LANES)), expand, NN, 0)[0:1, :]
        cs_last = cs_x[CHUNK - 1:CHUNK, :]
        dec_out = jnp.exp(cs_x)
        dec_st = jnp.exp(cs_last - cs_x)
        dec_ch = jnp.exp(cs_last)
        x = xbc_ref[:, 0:SSD_INNER].astype(F32)
        dyv = dy_ref[...]
        xr = x * dt_x
        xrs = xr * dec_st
        lane_g = lax.broadcasted_iota(jnp.int32, (1, gw), 1) // SSD_HEADDIM
        hsel = lax.broadcasted_iota(jnp.int32, (CHUNK, LANES), 1)
        dcs = jnp.zeros((CHUNK, LANES), F32)
        last_parts = []
        t_parts = []
        dxr_parts = []
        for g in range(SSD_GROUPS):
            sl = slice(g * gw, (g + 1) * gw)
            bsl = slice(SSD_INNER + g * SSD_STATE, SSD_INNER + (g + 1) * SSD_STATE)
            csl = slice(SSD_INNER + (SSD_GROUPS + g) * SSD_STATE, SSD_INNER + (SSD_GROUPS + g + 1) * SSD_STATE)
            bg = xbc_ref[:, bsl]
            cg = xbc_ref[:, csl]
            cb = _dot(cg, bg, NT)
            prev_g = prev_ref[0, :, sl]
            prev_b = prev_g.astype(BF16)
            dp_g = dp_ref[:, sl]
            dp_b = dp_g.astype(BF16)
            dy_g = dyv[:, sl]
            xr_g = xr[:, sl]
            gmat = _dot(cg, prev_b, NN)
            dgm = (dy_g * dec_out[:, sl]).astype(BF16)
            dc_g = _dot(dgm, prev_b, NT)
            dprev = _dot(cg, dgm, TN)
            t1 = dy_g * gmat * dec_out[:, sl]
            mm_ = _dot(bg, dp_b, NN)
            db_g = _dot(xrs[:, sl].astype(BF16), dp_b, NT)
            dxr_g = mm_ * dec_st[:, sl]
            t2 = dxr_g * xr_g
            last = jnp.sum(t2, axis=0, keepdims=True) + jnp.sum(dp_g * prev_g, axis=0, keepdims=True) * dec_ch[:, sl]
            dp_ref[:, sl] = dp_g * dec_ch[:, sl] + dprev
            dcb = jnp.zeros((CHUNK, CHUNK), F32)
            for r in range(hpg):
                h = g * hpg + r
                diff = cs[:, h:h + 1] - cs_t[h:h + 1, :]
                lmat = jnp.exp(jnp.where(tril, diff, -1e30))
                wmat = cb * lmat
                dy_h = jnp.where(lane_g == r, dy_g, 0.0).astype(BF16)
                dw = _dot(dy_h, xr_g.astype(BF16), NT)
                dxr_g = dxr_g + _dot(wmat.astype(BF16), dy_h, TN)
                dcb = dcb + dw * lmat
                q = dw * wmat
                q2 = jnp.concatenate([q, q.T], axis=1).astype(BF16)
                onehot = (hsel == h).astype(BF16)
                dcs = dcs + _dot(q2, jnp.concatenate([onehot, -onehot], axis=0), NN)
            dcb_b = dcb.astype(BF16)
            dc_g = dc_g + _dot(dcb_b, bg, NN)
            db_g = db_g + _dot(dcb_b, cg, TN)
            dxbc_ref[:, bsl] = db_g.astype(dxbc_ref.dtype)
            dxbc_ref[:, csl] = dc_g.astype(dxbc_ref.dtype)
            t_parts.append(t1 - t2)
            last_parts.append(last)
            dxr_parts.append(dxr_g)
        dxr = jnp.concatenate(dxr_parts, axis=1)
        tt = jnp.concatenate(t_parts, axis=1)
        last_x = jnp.concatenate(last_parts, axis=1)
        dxbc_ref[:, 0:SSD_INNER] = (dxr * dt_x + dyv * d_x).astype(dxbc_ref.dtype)
        dcs = dcs + _dot01(tt, expand, NT, 0, terms=2)
        last_h = _dot01(jnp.broadcast_to(last_x, (8, SSD_INNER)), expand, NT, 0)[0:1, :]
        rowi = lax.broadcasted_iota(jnp.int32, (CHUNK, LANES), 0)
        dcs = dcs + jnp.where(rowi == CHUNK - 1, last_h, 0.0)
        dda = _dot01(tril.astype(F32), dcs, TN, 1)
        ddt = dda * a + _dot01(dxr * x, expand, NT, 0, terms=2)
        dpre = ddt * _sigmoid(pre)
        ddtp_ref[...] = dpre
        dbias_ref[...] += jnp.sum(dpre, axis=0, keepdims=True)
        dalog_ref[...] += jnp.sum(dda * dt, axis=0, keepdims=True) * a
        ddx_ref[...] += jnp.broadcast_to(jnp.sum(dyv * x, axis=0, keepdims=True), (8, SSD_INNER))

        @pl.when(step == nc - 1)
        def _():
            dd_ref[...] = _dot01(ddx_ref[...], expand, NT, 0)[0:1, :]

    rev = lambda c: (nc - 1 - c, 0)
    vec = pl.BlockSpec((1, LANES), lambda c: (0, 0))
    return pl.pallas_call(
        body,
        grid=(nc,),
        in_specs=[pl.BlockSpec((CHUNK, SSD_CONV_DIM), rev), pl.BlockSpec((CHUNK, LANES), rev),
                  pl.BlockSpec((1, SSD_STATE, SSD_INNER), lambda c: (nc - 1 - c, 0, 0)),
                  pl.BlockSpec((CHUNK, SSD_INNER), rev), vec, vec, vec],
        out_specs=[pl.BlockSpec((CHUNK, SSD_CONV_DIM), rev), pl.BlockSpec((CHUNK, LANES), rev), vec, vec, vec],
        out_shape=[jax.ShapeDtypeStruct((s, SSD_CONV_DIM), BF16), jax.ShapeDtypeStruct((s, LANES), F32),
                   jax.ShapeDtypeStruct((1, LANES), F32), jax.ShapeDtypeStruct((1, LANES), F32),
                   jax.ShapeDtypeStruct((1, LANES), F32)],
        scratch_shapes=[pltpu.VMEM((SSD_STATE, SSD_INNER), F32), pltpu.VMEM((8, SSD_INNER), F32)],
        compiler_params=_params(("arbitrary",)),
        name="ssd_scan_bwd",
    )(xbc, dtp, prev, dy, dt_bias, a_log, d_skip)


def _ssd_gate_fwd(y, zx, norm_w):
    s = y.shape[0]
    tr = _pick(s, (256, 128))
    gw = SSD_GROUP_W

    def body(y_ref, z_ref, w_ref, o_ref):
        for g in range(SSD_GROUPS):
            sl = slice(g * gw, (g + 1) * gw)
            z = z_ref[:, sl].astype(F32)
            gv = y_ref[:, sl] * (z * _sigmoid(z))
            r = lax.rsqrt(jnp.mean(gv * gv, axis=-1, keepdims=True) + LN_EPS)
            o_ref[:, sl] = (gv * r * w_ref[:, sl]).astype(o_ref.dtype)

    row = pl.BlockSpec((tr, SSD_INNER), lambda i: (i, 0))
    return pl.pallas_call(
        body,
        grid=(s // tr,),
        in_specs=[row, row, pl.BlockSpec((1, SSD_INNER), lambda i: (0, 0))],
        out_specs=row,
        out_shape=jax.ShapeDtypeStruct((s, SSD_INNER), BF16),
        compiler_params=_params(("parallel",)),
        name="ssd_gate_fwd",
    )(y, zx, norm_w)


def _ssd_gate_bwd(dgn, y, zx, norm_w):
    s = y.shape[0]
    tr = _pick(s, (256, 128))
    gw = SSD_GROUP_W

    def body(dg_ref, y_ref, z_ref, w_ref, dy_ref, dz_ref, dw_ref):
        parts = []
        for g in range(SSD_GROUPS):
            sl = slice(g * gw, (g + 1) * gw)
            z = z_ref[:, sl].astype(F32)
            yv = y_ref[:, sl]
            sg = _sigmoid(z)
            sz = z * sg
            gv = yv * sz
            r = lax.rsqrt(jnp.mean(gv * gv, axis=-1, keepdims=True) + LN_EPS)
            gn = gv * r
            dout = dg_ref[:, sl].astype(F32)
            parts.append(jnp.sum(dout * gn, axis=0, keepdims=True))
            dgn_ = dout * w_ref[:, sl]
            dgv = r * (dgn_ - gn * jnp.mean(dgn_ * gn, axis=-1, keepdims=True))
            dy_ref[:, sl] = dgv * sz
            dz_ref[:, sl] = (dgv * yv * (sg * (1.0 + z * (1.0 - sg)))).astype(dz_ref.dtype)
        part = jnp.concatenate(parts, axis=1)

        @pl.when(pl.program_id(0) == 0)
        def _():
            dw_ref[...] = part

        @pl.when(pl.program_id(0) > 0)
        def _():
            dw_ref[...] += part

    row = pl.BlockSpec((tr, SSD_INNER), lambda i: (i, 0))
    vec = pl.BlockSpec((1, SSD_INNER), lambda i: (0, 0))
    return pl.pallas_call(
        body,
        grid=(s // tr,),
        in_specs=[row, row, row, vec],
        out_specs=[row, row, vec],
        out_shape=[jax.ShapeDtypeStruct((s, SSD_INNER), F32), jax.ShapeDtypeStruct((s, SSD_ZX), BF16),
                   jax.ShapeDtypeStruct((1, SSD_INNER), F32)],
        compiler_params=_params(("arbitrary",)),
        name="ssd_gate_bwd",
    )(dgn, y, zx, norm_w)


INV_SQRT2 = 1.0 / math.sqrt(2.0)
INV_SQRT2PI = 1.0 / math.sqrt(2.0 * math.pi)


def _gelu(x):
    return 0.5 * x * (1.0 + lax.erf(x * INV_SQRT2))


def _gelu_grad(x):
    return 0.5 * (1.0 + lax.erf(x * INV_SQRT2)) + x * INV_SQRT2PI * jnp.exp(-0.5 * x * x)


def _gmlp_act_fwd(pre, b_in, ln_w, ln_b):
    s = pre.shape[0]
    tr = _pick(s, (256, 128))
    n = GMLP_INNER

    def body(p_ref, b_ref, w_ref, lb_ref, u_ref, v_ref):
        u_ref[...] = _gelu(p_ref[:, 0:n].astype(F32) + b_ref[:, 0:n]).astype(u_ref.dtype)
        hv = _gelu(p_ref[:, n:2 * n].astype(F32) + b_ref[:, n:2 * n])
        mu = jnp.mean(hv, axis=-1, keepdims=True)
        xc = hv - mu
        r = lax.rsqrt(jnp.mean(xc * xc, axis=-1, keepdims=True) + LN_EPS)
        v_ref[...] = (xc * r * w_ref[...] + lb_ref[...]).astype(v_ref.dtype)

    half = pl.BlockSpec((tr, n), lambda i: (i, 0))
    vec = pl.BlockSpec((1, n), lambda i: (0, 0))
    return pl.pallas_call(
        body,
        grid=(s // tr,),
        in_specs=[pl.BlockSpec((tr, 2 * n), lambda i: (i, 0)), pl.BlockSpec((1, 2 * n), lambda i: (0, 0)), vec, vec],
        out_specs=[half, half],
        out_shape=[jax.ShapeDtypeStruct((s, n), BF16), jax.ShapeDtypeStruct((s, n), BF16)],
        compiler_params=_params(("parallel",)),
        name="gmlp_act_fwd",
    )(pre, b_in, ln_w, ln_b)


def _gmlp_act_bwd(pre, b_in, ln_w, du, dv):
    s = pre.shape[0]
    tr = _pick(s, (256, 128))
    n = GMLP_INNER

    def body(p_ref, b_ref, w_ref, du_ref, dv_ref, dp_ref, db_ref, dw_ref, dlb_ref):
        xu = p_ref[:, 0:n].astype(F32) + b_ref[:, 0:n]
        dpu = du_ref[...].astype(F32) * _gelu_grad(xu)
        xv = p_ref[:, n:2 * n].astype(F32) + b_ref[:, n:2 * n]
        hv = _gelu(xv)
        mu = jnp.mean(hv, axis=-1, keepdims=True)
        xc = hv - mu
        r = lax.rsqrt(jnp.mean(xc * xc, axis=-1, keepdims=True) + LN_EPS)
        vh = xc * r
        dvv = dv_ref[...].astype(F32)
        dvh = dvv * w_ref[...]
        dh = r * (dvh - jnp.mean(dvh, axis=-1, keepdims=True) - vh * jnp.mean(dvh * vh, axis=-1, keepdims=True))
        dpv = dh * _gelu_grad(xv)
        dp_ref[:, 0:n] = dpu.astype(dp_ref.dtype)
        dp_ref[:, n:2 * n] = dpv.astype(dp_ref.dtype)
        pb = jnp.concatenate([jnp.sum(dpu, axis=0, keepdims=True), jnp.sum(dpv, axis=0, keepdims=True)], axis=1)
        pw = jnp.sum(dvv * vh, axis=0, keepdims=True)
        plb = jnp.sum(dvv, axis=0, keepdims=True)

        @pl.when(pl.program_id(0) == 0)
        def _():
            db_ref[...] = pb
            dw_ref[...] = pw
            dlb_ref[...] = plb

        @pl.when(pl.program_id(0) > 0)
        def _():
            db_ref[...] += pb
            dw_ref[...] += pw
            dlb_ref[...] += plb

    half = pl.BlockSpec((tr, n), lambda i: (i, 0))
    full = pl.BlockSpec((tr, 2 * n), lambda i: (i, 0))
    vec = pl.BlockSpec((1, n), lambda i: (0, 0))
    vec2 = pl.BlockSpec((1, 2 * n), lambda i: (0, 0))
    return pl.pallas_call(
        body,
        grid=(s // tr,),
        in_specs=[full, vec2, vec, half, half],
        out_specs=[full, vec2, vec, vec],
        out_shape=[jax.ShapeDtypeStruct((s, 2 * n), BF16), jax.ShapeDtypeStruct((1, 2 * n), F32),
                   jax.ShapeDtypeStruct((1, n), F32), jax.ShapeDtypeStruct((1, n), F32)],
        compiler_params=_params(("arbitrary",)),
        name="gmlp_act_bwd",
    )(pre, b_in, ln_w, du, dv)


def _gmlp_mix_fwd(u, v, w_s, b_st):
    s = u.shape[0]
    gd = GMLP_INNER // GMLP_GROUPS

    def body(u_ref, v_ref, w_ref, b_ref, o_ref):
        li = lax.broadcasted_iota(jnp.int32, (CHUNK, CHUNK), 0)
        si = lax.broadcasted_iota(jnp.int32, (CHUNK, CHUNK), 1)
        tril = li >= si
        for g in range(GMLP_GROUPS):
            sl = slice(g * gd, (g + 1) * gd)
            wm = jnp.where(tril, w_ref[g], 0.0).astype(BF16)
            mixed = _dot(wm, v_ref[:, sl], NN) + b_ref[:, g:g + 1]
            o_ref[:, sl] = (u_ref[:, sl].astype(F32) * mixed).astype(o_ref.dtype)

    row = pl.BlockSpec((CHUNK, GMLP_INNER), lambda c: (c, 0))
    return pl.pallas_call(
        body,
        grid=(s // CHUNK,),
        in_specs=[row, row, pl.BlockSpec((GMLP_GROUPS, CHUNK, CHUNK), lambda c: (0, 0, 0)),
                  pl.BlockSpec((CHUNK, LANES), lambda c: (0, 0))],
        out_specs=row,
        out_shape=jax.ShapeDtypeStruct((s, GMLP_INNER), BF16),
        compiler_params=_params(("parallel",)),
        name="gmlp_mix_fwd",
    )(u, v, w_s, b_st)


def _gmlp_mix_bwd(dgated, u, v, w_s, b_st):
    s = u.shape[0]
    nc = s // CHUNK
    gd = GMLP_INNER // GMLP_GROUPS

    def body(dg_ref, u_ref, v_ref, w_ref, b_ref, du_ref, dv_ref, dw_ref, db_ref):
        c = pl.program_id(0)

        @pl.when(c == 0)
        def _():
            dw_ref[...] = jnp.zeros_like(dw_ref)
            db_ref[...] = jnp.zeros_like(db_ref)

        li = lax.broadcasted_iota(jnp.int32, (CHUNK, CHUNK), 0)
        si = lax.broadcasted_iota(jnp.int32, (CHUNK, CHUNK), 1)
        tril = li >= si
        lane = lax.broadcasted_iota(jnp.int32, (CHUNK, LANES), 1)
        dbacc = jnp.zeros((CHUNK, LANES), F32)
        for g in range(GMLP_GROUPS):
            sl = slice(g * gd, (g + 1) * gd)
            wm = jnp.where(tril, w_ref[g], 0.0).astype(BF16)
            vg = v_ref[:, sl]
            mixed = _dot(wm, vg, NN) + b_ref[:, g:g + 1]
            dgv = dg_ref[:, sl].astype(F32)
            du_ref[:, sl] = (dgv * mixed).astype(du_ref.dtype)
            dm = dgv * u_ref[:, sl].astype(F32)
            dm_b = dm.astype(BF16)
            dv_ref[:, sl] = _dot(wm, dm_b, TN).astype(dv_ref.dtype)
            dw_ref[g] += jnp.where(tril, _dot(dm_b, vg, NT), 0.0)
            dbacc = dbacc + jnp.where(lane == g, jnp.sum(dm, axis=1, keepdims=True), 0.0)
        db_ref[...] += dbacc

    row = pl.BlockSpec((CHUNK, GMLP_INNER), lambda c: (c, 0))
    wspec = pl.BlockSpec((GMLP_GROUPS, CHUNK, CHUNK), lambda c: (0, 0, 0))
    bspec = pl.BlockSpec((CHUNK, LANES), lambda c: (0, 0))
    return pl.pallas_call(
        body,
        grid=(nc,),
        in_specs=[row, row, row, wspec, bspec],
        out_specs=[row, row, wspec, bspec],
        out_shape=[jax.ShapeDtypeStruct((s, GMLP_INNER), BF16), jax.ShapeDtypeStruct((s, GMLP_INNER), BF16),
                   jax.ShapeDtypeStruct((GMLP_GROUPS, CHUNK, CHUNK), F32), jax.ShapeDtypeStruct((CHUNK, LANES), F32)],
        compiler_params=_params(("arbitrary",)),
        name="gmlp_mix_bwd",
    )(dgated, u, v, w_s, b_st)


FFN_HALF = FFN_DIM // 2


def _ffn_up(un, wgu):
    s, d = un.shape
    f = FFN_DIM
    tm = _pick(s, (1024, 512, 256, 128))
    nh = f // FFN_HALF

    def body(x_ref, wg_ref, wu_ref, g_ref, u_ref, h_ref):
        x = x_ref[...]
        g_ref[...] = _dot(x, wg_ref[...], NN).astype(g_ref.dtype)
        u_ref[...] = _dot(x, wu_ref[...], NN).astype(u_ref.dtype)
        gt = g_ref[...].astype(F32)
        h_ref[...] = (gt * _sigmoid(gt) * u_ref[...].astype(F32)).astype(h_ref.dtype)

    out = pl.BlockSpec((tm, FFN_HALF), lambda i, j: (i, j))
    sds = jax.ShapeDtypeStruct((s, f), BF16)
    return pl.pallas_call(
        body,
        grid=(s // tm, nh),
        in_specs=[pl.BlockSpec((tm, d), lambda i, j: (i, 0)), pl.BlockSpec((d, FFN_HALF), lambda i, j: (0, j)),
                  pl.BlockSpec((d, FFN_HALF), lambda i, j: (0, j + nh))],
        out_specs=[out, out, out],
        out_shape=[sds, sds, sds],
        compiler_params=_params(("parallel", "parallel")),
        name="ffn_up",
    )(un, wgu, wgu)


def _ffn_down_bwd(dh, wd, gate, up):
    s, d = dh.shape
    f = FFN_DIM
    tm = _pick(s, (512, 256, 128))

    def body(dh_ref, wd_ref, g_ref, u_ref, o_ref):
        dhb = dh_ref[...].astype(BF16)
        for half in range(f // FFN_HALF):
            cols = slice(half * FFN_HALF, (half + 1) * FFN_HALF)
            dhid = _dot(dhb, wd_ref[cols, :], NT)
            gt = g_ref[:, cols].astype(F32)
            sg = _sigmoid(gt)
            o_ref[:, cols] = (dhid * u_ref[:, cols].astype(F32) * (sg * (1.0 + gt * (1.0 - sg)))).astype(o_ref.dtype)
            o_ref[:, f + half * FFN_HALF:f + (half + 1) * FFN_HALF] = (dhid * gt * sg).astype(o_ref.dtype)

    row = pl.BlockSpec((tm, f), lambda i: (i, 0))
    return pl.pallas_call(
        body,
        grid=(s // tm,),
        in_specs=[pl.BlockSpec((tm, d), lambda i: (i, 0)), pl.BlockSpec((f, d), lambda i: (0, 0)), row, row],
        out_specs=pl.BlockSpec((tm, 2 * f), lambda i: (i, 0)),
        out_shape=jax.ShapeDtypeStruct((s, 2 * f), BF16),
        compiler_params=_params(("parallel",)),
        name="ffn_down_bwd",
    )(dh, wd, gate, up)


def _ple_fwd(p, h, w_proj, w_gate, gate_norm, ple_norm, next_norm):
    s, d = h.shape
    e = p.shape[1]
    tr = _pick(s, (512, 256, 128))

    def body(p_ref, h_ref, wp_ref, wg_ref, gn_ref, pn_ref, nn_ref, o_ref, pe_ref, hg_ref, gl_ref, hn_ref):
        hv = h_ref[...]
        pe_ref[...] = _dot(p_ref[...].astype(BF16), wp_ref[...], NN).astype(pe_ref.dtype)
        r = lax.rsqrt(jnp.mean(hv * hv, axis=-1, keepdims=True) + RMS_EPS)
        hg_ref[...] = (hv * r * gn_ref[...]).astype(hg_ref.dtype)
        gl_ref[...] = _dot(hg_ref[...], wg_ref[...], NN).astype(gl_ref.dtype)
        pe_ = pe_ref[...].astype(F32)
        rp = lax.rsqrt(jnp.mean(pe_ * pe_, axis=-1, keepdims=True) + RMS_EPS)
        out = hv + _sigmoid(gl_ref[...].astype(F32)) * (pe_ * rp * pn_ref[...])
        o_ref[...] = out
        ro = lax.rsqrt(jnp.mean(out * out, axis=-1, keepdims=True) + RMS_EPS)
        hn_ref[...] = (out * ro * nn_ref[...]).astype(hn_ref.dtype)

    row = pl.BlockSpec((tr, d), lambda i: (i, 0))
    vec = pl.BlockSpec((1, d), lambda i: (0, 0))
    sds = jax.ShapeDtypeStruct((s, d), BF16)
    return pl.pallas_call(
        body,
        grid=(s // tr,),
        in_specs=[pl.BlockSpec((tr, e), lambda i: (i, 0)), row, pl.BlockSpec((e, d), lambda i: (0, 0)),
                  pl.BlockSpec((d, d), lambda i: (0, 0)), vec, vec, vec],
        out_specs=[row, row, row, row, row],
        out_shape=[jax.ShapeDtypeStruct((s, d), F32), sds, sds, sds, sds],
        compiler_params=_params(("parallel",)),
        name="ple_fwd",
    )(p, h, w_proj, w_gate, gate_norm, ple_norm, next_norm)


def _ple_bwd(dh, p, pe, gl, hg, h, w_gate, gate_norm, ple_norm):
    s, d = dh.shape
    e = p.shape[1]
    tr = _pick(s, (512, 256, 128))

    def body(dh_ref, p_ref, pe_ref, gl_ref, hg_ref, h_ref, wg_ref, gn_ref, pn_ref,
             dx_ref, dwg_ref, dwp_ref, dpn_ref, dgn_ref):
        pe_ = pe_ref[...].astype(F32)
        dhv = dh_ref[...]
        r = lax.rsqrt(jnp.mean(pe_ * pe_, axis=-1, keepdims=True) + RMS_EPS)
        pn = pe_ * r
        gate = _sigmoid(gl_ref[...].astype(F32))
        dgl = (dhv * (pn * pn_ref[...]) * gate * (1.0 - gate)).astype(BF16)
        de = dhv * gate
        dxh = de * pn_ref[...]
        dpe = (r * (dxh - pn * jnp.mean(dxh * pn, axis=-1, keepdims=True))).astype(BF16)
        dhg = _dot(dgl, wg_ref[...], NT)
        hv = h_ref[...]
        rh = lax.rsqrt(jnp.mean(hv * hv, axis=-1, keepdims=True) + RMS_EPS)
        hn = hv * rh
        dhh = dhg * gn_ref[...]
        dx_ref[...] = dhv + rh * (dhh - hn * jnp.mean(dhh * hn, axis=-1, keepdims=True))
        parts = (_dot(hg_ref[...], dgl, TN), _dot(p_ref[...].astype(BF16), dpe, TN),
                 jnp.sum(de * pn, axis=0, keepdims=True), jnp.sum(dhg * hn, axis=0, keepdims=True))
        accs = (dwg_ref, dwp_ref, dpn_ref, dgn_ref)

        @pl.when(pl.program_id(0) == 0)
        def _():
            for acc, part in zip(accs, parts):
                acc[...] = part

        @pl.when(pl.program_id(0) > 0)
        def _():
            for acc, part in zip(accs, parts):
                acc[...] += part

    row = pl.BlockSpec((tr, d), lambda i: (i, 0))
    vec = pl.BlockSpec((1, d), lambda i: (0, 0))
    mat = pl.BlockSpec((d, d), lambda i: (0, 0))
    small = pl.BlockSpec((e, d), lambda i: (0, 0))
    return pl.pallas_call(
        body,
        grid=(s // tr,),
        in_specs=[row, pl.BlockSpec((tr, e), lambda i: (i, 0)), row, row, row, row, mat, vec, vec],
        out_specs=[row, mat, small, vec, vec],
        out_shape=[jax.ShapeDtypeStruct((s, d), F32), jax.ShapeDtypeStruct((d, d), F32),
                   jax.ShapeDtypeStruct((e, d), F32), jax.ShapeDtypeStruct((1, d), F32),
                   jax.ShapeDtypeStruct((1, d), F32)],
        compiler_params=_params(("arbitrary",)),
        name="ple_bwd",
    )(dh, p, pe, gl, hg, h, w_gate, gate_norm, ple_norm)


def _loss_head(h, w, target):
    s, d = h.shape
    tr = _pick(s, (512, 256, 128))

    def body(h_ref, w_ref, t_ref, l_ref, dh_ref, dw_ref):
        hv = h_ref[...]
        r = lax.rsqrt(jnp.mean(hv * hv, axis=-1, keepdims=True) + RMS_EPS)
        hn = hv * r
        diff = hn * w_ref[...] - t_ref[...]
        lpart = jnp.zeros((8, LANES), F32) + (0.5 / d) * jnp.sum(jnp.sum(diff * diff, axis=1, keepdims=True), axis=0, keepdims=True)
        dy = diff * (1.0 / d)
        dxh = dy * w_ref[...]
        dh_ref[...] = r * (dxh - hn * jnp.mean(dxh * hn, axis=-1, keepdims=True))
        part = jnp.sum(dy * hn, axis=0, keepdims=True)

        @pl.when(pl.program_id(0) == 0)
        def _():
            l_ref[...] = lpart
            dw_ref[...] = part

        @pl.when(pl.program_id(0) > 0)
        def _():
            l_ref[...] += lpart
            dw_ref[...] += part

    row = pl.BlockSpec((tr, d), lambda i: (i, 0))
    vec = pl.BlockSpec((1, d), lambda i: (0, 0))
    return pl.pallas_call(
        body,
        grid=(s // tr,),
        in_specs=[row, vec, row],
        out_specs=[pl.BlockSpec((8, LANES), lambda i: (0, 0)), row, vec],
        out_shape=[jax.ShapeDtypeStruct((8, LANES), F32), jax.ShapeDtypeStruct((s, d), F32),
                   jax.ShapeDtypeStruct((1, d), F32)],
        compiler_params=_params(("arbitrary",)),
        name="loss_head",
    )(h, w, target)


PER_LAYER = ("norm_mix", "norm_ffn", "ffn_w_gu", "ffn_w_down", "ple_w_proj", "ple_norm", "ple_gate_norm", "ple_w_gate")


def _pad_lanes(v):
    return jnp.pad(v.astype(F32), (0, LANES - v.shape[0]))[None, :]


def _kernel_layouts(full):
    w = {}
    for k in ("norm_mix", "norm_ffn", "ple_norm", "ple_gate_norm", "ssd_conv_b", "ssd_norm_w", "gmlp_b_in", "gmlp_ln_w",
              "gmlp_ln_b", "gmlp_w_s"):
        w[k] = [full[k][i].astype(F32) for i in range(full[k].shape[0])]
    w["final_norm"] = full["final_norm"].astype(F32)
    n_ssd = full["ssd_w_out"].shape[0]
    if "ssd_w_in" in full:
        w["ssd_w_zx"] = [full["ssd_w_in"][j][:, :SSD_ZX].astype(BF16) for j in range(n_ssd)]
        w["ssd_w_dt"] = [jnp.pad(full["ssd_w_in"][j][:, SSD_ZX:].astype(BF16), ((0, 0), (0, LANES - SSD_HEADS)))
                         for j in range(n_ssd)]
        w["ffn_w_gu"] = [jnp.concatenate([full["ffn_w_gate"][i], full["ffn_w_up"][i]], axis=1).astype(BF16)
                         for i in range(DEPTH)]
    else:
        for k in ("ssd_w_zx", "ssd_w_dt", "ffn_w_gu"):
            w[k] = full[k]
    w["ssd_conv_w"] = [jnp.pad(full["ssd_conv_w"][j].astype(F32), ((0, 8 - CONV_K), (0, 0))) for j in range(n_ssd)]
    for k in ("ssd_dt_bias", "ssd_a_log", "ssd_d"):
        w[k] = [_pad_lanes(full[k][j]) for j in range(n_ssd)]
    w["ssd_w_out"] = [full["ssd_w_out"][j].astype(BF16) for j in range(n_ssd)]
    n_g = full["gmlp_w_in"].shape[0]
    w["gmlp_w_in"] = [full["gmlp_w_in"][j].astype(BF16) for j in range(n_g)]
    w["gmlp_w_out"] = [full["gmlp_w_out"][j].astype(BF16) for j in range(n_g)]
    w["gmlp_b_st"] = [jnp.pad(full["gmlp_b_s"][j].astype(F32).T, ((0, 0), (0, LANES - GMLP_GROUPS))) for j in range(n_g)]
    w["ffn_w_down"] =[full["ffn_w_down"][i].astype(BF16) for i in range(DEPTH)]
    w["ple_w_proj"] = [full["ple_w_proj"][i].astype(BF16) for i in range(DEPTH)]
    w["ple_w_gate"] = [full["ple_w_gate"][i].astype(BF16) for i in range(DEPTH)]
    return w


MATRICES = ("ssd_w_out", "gmlp_w_in", "gmlp_w_out", "ffn_w_down", "ple_w_proj", "ple_w_gate")


def _reference_layouts(g, wide=True):
    out = {}
    for k in ("norm_mix", "norm_ffn", "ple_norm", "ple_gate_norm", "ssd_conv_b", "ssd_norm_w", "gmlp_b_in", "gmlp_ln_w",
              "gmlp_ln_b", "gmlp_w_s", "ssd_conv_w", "ssd_dt_bias", "ssd_a_log", "ssd_d") + (MATRICES if wide else ()):
        out[k] = jnp.stack(g[k])
    out["final_norm"] = g["final_norm"]
    out["gmlp_b_s"] = jnp.stack([b[:, :GMLP_GROUPS].T for b in g["gmlp_b_st"]])
    if wide:
        out["ssd_w_in"] = jnp.stack([jnp.concatenate([zx, dt[:, :SSD_HEADS]], axis=1)
                                     for zx, dt in zip(g["ssd_w_zx"], g["ssd_w_dt"])])
        out["ffn_w_gate"] = jnp.stack([gu[:, :FFN_DIM] for gu in g["ffn_w_gu"]])
        out["ffn_w_up"] = jnp.stack([gu[:, FFN_DIM:] for gu in g["ffn_w_gu"]])
    return out


RELAYOUT_ROWS = 128
SSD_SHARD = SSD_IN_DIM // N_DEV
FFN_SHARD = FFN_DIM // N_DEV


def _to_bf16(x):
    nl, rows, n = x.shape

    def body(x_ref, o_ref):
        o_ref[...] = x_ref[...].astype(o_ref.dtype)

    blk = pl.BlockSpec((1, rows, n), lambda i: (i, 0, 0))
    return pl.pallas_call(
        body, grid=(nl,), in_specs=[blk], out_specs=blk, out_shape=jax.ShapeDtypeStruct(x.shape, BF16),
        compiler_params=_params(("parallel",)), name="to_bf16",
    )(x)


def _cat_ssd_in(gathered):
    _, nl, rows, n = gathered.shape
    tr = RELAYOUT_ROWS

    def body(g_ref, *o_refs):
        for j in range(nl):
            full = jnp.concatenate([g_ref[d, j] for d in range(N_DEV)], axis=1)
            o_refs[2 * j][...] = full[:, :SSD_ZX]
            o_refs[2 * j + 1][...] = jnp.concatenate(
                [full[:, SSD_ZX:], jnp.zeros((tr, LANES - SSD_HEADS), full.dtype)], axis=1)

    outs = pl.pallas_call(
        body, grid=(rows // tr,),
        in_specs=[pl.BlockSpec((N_DEV, nl, tr, n), lambda i: (0, 0, i, 0))],
        out_specs=[pl.BlockSpec((tr, SSD_ZX), lambda i: (i, 0)), pl.BlockSpec((tr, LANES), lambda i: (i, 0))] * nl,
        out_shape=[jax.ShapeDtypeStruct((rows, SSD_ZX), BF16), jax.ShapeDtypeStruct((rows, LANES), BF16)] * nl,
        compiler_params=_params(("parallel",)), name="cat_ssd_in",
    )(gathered)
    return [outs[2 * j] for j in range(nl)], [outs[2 * j + 1] for j in range(nl)]


def _split_ssd_in(dzx_list, ddt_list):
    nl = len(dzx_list)
    rows = dzx_list[0].shape[0]
    tr = RELAYOUT_ROWS

    def body(*refs):
        o_ref = refs[2 * nl]
        for j in range(nl):
            full = jnp.concatenate([refs[2 * j][...], refs[2 * j + 1][:, 0:SSD_HEADS]], axis=1)
            for d in range(N_DEV):
                o_ref[d, j] = full[:, d * SSD_SHARD:(d + 1) * SSD_SHARD].astype(o_ref.dtype)

    ins = []
    for j in range(nl):
        ins += [dzx_list[j], ddt_list[j]]
    return pl.pallas_call(
        body, grid=(rows // tr,),
        in_specs=[pl.BlockSpec((tr, SSD_ZX), lambda i: (i, 0)), pl.BlockSpec((tr, LANES), lambda i: (i, 0))] * nl,
        out_specs=pl.BlockSpec((N_DEV, nl, tr, SSD_SHARD), lambda i: (0, 0, i, 0)),
        out_shape=jax.ShapeDtypeStruct((N_DEV, nl, rows, SSD_SHARD), BF16),
        compiler_params=_params(("parallel",)), name="split_ssd_in",
    )(*ins)


def _cat_ffn(g_gate, g_up):
    _, nl, rows, n = g_gate.shape
    tr = RELAYOUT_ROWS

    def body(gg_ref, gu_ref, *o_refs):
        for i in range(nl):
            o_refs[i][...] = jnp.concatenate([gg_ref[d, i] for d in range(N_DEV)] + [gu_ref[d, i] for d in range(N_DEV)],
                                             axis=1)

    blk = pl.BlockSpec((N_DEV, nl, tr, n), lambda i: (0, 0, i, 0))
    outs = pl.pallas_call(
        body, grid=(rows // tr,), in_specs=[blk, blk],
        out_specs=[pl.BlockSpec((tr, 2 * FFN_DIM), lambda i: (i, 0))] * nl,
        out_shape=[jax.ShapeDtypeStruct((rows, 2 * FFN_DIM), BF16)] * nl,
        compiler_params=_params(("parallel",)), name="cat_ffn",
    )(g_gate, g_up)
    return list(outs)


def _split_ffn(dgu_list):
    nl = len(dgu_list)
    rows = dgu_list[0].shape[0]
    tr = RELAYOUT_ROWS

    def body(*refs):
        og_ref, ou_ref = refs[nl], refs[nl + 1]
        for i in range(nl):
            full = refs[i][...]
            for d in range(N_DEV):
                og_ref[d, i] = full[:, d * FFN_SHARD:(d + 1) * FFN_SHARD].astype(og_ref.dtype)
                ou_ref[d, i] = full[:, FFN_DIM + d * FFN_SHARD:FFN_DIM + (d + 1) * FFN_SHARD].astype(ou_ref.dtype)

    blk = pl.BlockSpec((N_DEV, nl, tr, FFN_SHARD), lambda i: (0, 0, i, 0))
    sds = jax.ShapeDtypeStruct((N_DEV, nl, rows, FFN_SHARD), BF16)
    return pl.pallas_call(
        body, grid=(rows // tr,),
        in_specs=[pl.BlockSpec((tr, 2 * FFN_DIM), lambda i: (i, 0))] * nl,
        out_specs=[blk, blk], out_shape=[sds, sds],
        compiler_params=_params(("parallel",)), name="split_ffn",
    )(*dgu_list)


def _local_step(x, p, target, w):
    saved = []
    h = x
    hn = _rms_fwd(h, w["norm_mix"][0][None, :])
    for i in range(DEPTH):
        j = i // 2
        sv = {"h0": h}
        sv["hn"] = hn
        if i % 2 == 0:
            zx = _mm(hn, w["ssd_w_zx"][j], "nn", BF16)
            dtp = _mm(hn, w["ssd_w_dt"][j], "nn", F32)
            xbc = _ssd_conv_fwd(zx, w["ssd_conv_w"][j], w["ssd_conv_b"][j][None, :])
            y, prev = _ssd_scan_fwd(xbc, dtp, w["ssd_dt_bias"][j], w["ssd_a_log"][j], w["ssd_d"][j])
            gn = _ssd_gate_fwd(y, zx, w["ssd_norm_w"][j][None, :])
            h = _mm(gn, w["ssd_w_out"][j], "nn", F32, add=h)
            sv.update(zx=zx, dtp=dtp, xbc=xbc, y=y, prev=prev, gn=gn)
        else:
            pre = _mm(hn, w["gmlp_w_in"][j], "nn", BF16)
            u, v = _gmlp_act_fwd(pre, w["gmlp_b_in"][j][None, :], w["gmlp_ln_w"][j][None, :], w["gmlp_ln_b"][j][None, :])
            gated = _gmlp_mix_fwd(u, v, w["gmlp_w_s"][j], w["gmlp_b_st"][j])
            h = _mm(gated, w["gmlp_w_out"][j], "nn", F32, add=h)
            sv.update(pre=pre, u=u, v=v, gated=gated)
        sv["h1"] = h
        un = _rms_fwd(h, w["norm_ffn"][i][None, :])
        gate, up, hid = _ffn_up(un, w["ffn_w_gu"][i])
        h = _mm(hid, w["ffn_w_down"][i], "nn", F32, add=h)
        sv.update(un=un, gate=gate, up=up, hid=hid, h2=h)
        next_norm = w["norm_mix"][i + 1] if i + 1 < DEPTH else w["final_norm"]
        h, pe, hg, gl, hn = _ple_fwd(p[i], h, w["ple_w_proj"][i], w["ple_w_gate"][i], w["ple_gate_norm"][i][None, :],
                                     w["ple_norm"][i][None, :], next_norm[None, :])
        sv.update(pe=pe, hg=hg, gl=gl)
        saved.append(sv)

    lpart, dh, d_final = _loss_head(h, w["final_norm"][None, :], target)
    g = {k: [None] * (DEPTH if k in PER_LAYER else DEPTH // 2) for k in w if k != "final_norm"}
    g["final_norm"] = d_final[0]

    for i in reversed(range(DEPTH)):
        j = i // 2
        sv = saved[i]
        dh, g["ple_w_gate"][i], g["ple_w_proj"][i], d_ple_norm, d_gate_norm = _ple_bwd(
            dh, p[i], sv["pe"], sv["gl"], sv["hg"], sv["h2"], w["ple_w_gate"][i], w["ple_gate_norm"][i][None, :],
            w["ple_norm"][i][None, :])
        g["ple_norm"][i] = d_ple_norm[0]
        g["ple_gate_norm"][i] = d_gate_norm[0]
        g["ffn_w_down"][i] = _mm(sv["hid"], dh, "tn", F32)
        dgu = _ffn_down_bwd(dh, w["ffn_w_down"][i], sv["gate"], sv["up"])
        g["ffn_w_gu"][i] = _mm(sv["un"], dgu, "tn", F32)
        dun = _mm(dgu, w["ffn_w_gu"][i], "nt", BF16)
        dh, d_norm_ffn = _rms_bwd(dun, sv["h1"], w["norm_ffn"][i][None, :], dh)
        g["norm_ffn"][i] = d_norm_ffn[0]
        if i % 2 == 0:
            dgn = _mm(dh, w["ssd_w_out"][j], "nt", BF16)
            g["ssd_w_out"][j] = _mm(sv["gn"], dh, "tn", F32)
            dy, dzx, d_norm_w = _ssd_gate_bwd(dgn, sv["y"], sv["zx"], w["ssd_norm_w"][j][None, :])
            g["ssd_norm_w"][j] = d_norm_w[0]
            dxbc, ddtp, d_bias, d_alog, d_d = _ssd_scan_bwd(sv["xbc"], sv["dtp"], sv["prev"], dy, w["ssd_dt_bias"][j],
                                                            w["ssd_a_log"][j], w["ssd_d"][j])
            g["ssd_dt_bias"][j] = d_bias[0, :SSD_HEADS]
            g["ssd_a_log"][j] = d_alog[0, :SSD_HEADS]
            g["ssd_d"][j] = d_d[0, :SSD_HEADS]
            dzx, d_conv_w, d_conv_b = _ssd_conv_bwd(sv["zx"], dxbc, w["ssd_conv_w"][j], w["ssd_conv_b"][j][None, :], dzx)
            g["ssd_conv_w"][j] = d_conv_w[:CONV_K]
            g["ssd_conv_b"][j] = d_conv_b[0]
            g["ssd_w_zx"][j] = _mm(sv["hn"], dzx, "tn", F32)
            g["ssd_w_dt"][j] = _mm(sv["hn"], ddtp, "tn", F32)
            dhn = _mm(ddtp, w["ssd_w_dt"][j], "nt", F32)
            dhn = _mm(dzx, w["ssd_w_zx"][j], "nt", BF16, add=dhn)
        else:
            dgated = _mm(dh, w["gmlp_w_out"][j], "nt", BF16)
            g["gmlp_w_out"][j] = _mm(sv["gated"], dh, "tn", F32)
            du, dv, d_ws, d_bst = _gmlp_mix_bwd(dgated, sv["u"], sv["v"], w["gmlp_w_s"][j], w["gmlp_b_st"][j])
            g["gmlp_w_s"][j] = d_ws
            g["gmlp_b_st"][j] = d_bst
            dpre, d_bin, d_lnw, d_lnb = _gmlp_act_bwd(sv["pre"], w["gmlp_b_in"][j][None, :], w["gmlp_ln_w"][j][None, :],
                                                     du, dv)
            g["gmlp_b_in"][j] = d_bin[0]
            g["gmlp_ln_w"][j] = d_lnw[0]
            g["gmlp_ln_b"][j] = d_lnb[0]
            g["gmlp_w_in"][j] = _mm(sv["hn"], dpre, "tn", F32)
            dhn = _mm(dpre, w["gmlp_w_in"][j], "nt", BF16)
        dh, d_norm_mix = _rms_bwd(dhn, sv["h0"], w["norm_mix"][i][None, :], dh)
        g["norm_mix"][i] = d_norm_mix[0]
    return lpart[0, 0], dh, g


PACK_COLS = 1024
ANY = pl.BlockSpec(memory_space=pl.ANY)


def _mesh_pos():
    return lax.axis_index("x"), lax.axis_index("y"), lax.axis_index("c")


def _all_gather(xs_list, name):
    n = len(xs_list)

    def body(*refs):
        x_refs, out_refs = refs[:n], refs[n:2 * n]
        send_sems, recv_sems, local_sems = refs[2 * n:]
        x, y, c = _mesh_pos()
        me, sibling = (x, y, c), (x, y, 1 - c)
        chips = [(1 - x, y), (x, 1 - y), (1 - x, 1 - y)]

        def copy(a, k, block, to, from_input=False):
            px, py, pc = block
            dst = out_refs[a].at[4 * px + 2 * py + pc]
            return pltpu.make_async_remote_copy(
                src_ref=x_refs[a] if from_input else dst, dst_ref=dst,
                send_sem=send_sems.at[7 * a + k], recv_sem=recv_sems.at[7 * a + k], device_id=to,
                device_id_type=MESH_ID)

        mine = [pltpu.make_async_copy(x_refs[a], out_refs[a].at[4 * x + 2 * y + c], local_sems.at[a]) for a in range(n)]
        for cp in mine:
            cp.start()
        first = []
        for a in range(n):
            first += [copy(a, 1 + j, me, (*chip, c), from_input=True) for j, chip in enumerate(chips)]
            first.append(copy(a, 0, me, sibling, from_input=True))
        for cp in first:
            cp.start()
        passed = []
        for a in range(n):
            for j, chip in enumerate(chips):
                copy(a, 1 + j, (*chip, c), me).wait_recv()
                fwd = copy(a, 4 + j, (*chip, c), sibling)
                fwd.start()
                passed.append(fwd)
        for a in range(n):
            copy(a, 0, sibling, me).wait_recv()
            for j, chip in enumerate(chips):
                copy(a, 4 + j, (*chip, 1 - c), me).wait_recv()
        for cp in first + passed:
            cp.wait_send()
        for cp in mine:
            cp.wait()

    outs = pl.pallas_call(
        body,
        out_shape=[jax.ShapeDtypeStruct((N_DEV,) + t.shape, t.dtype) for t in xs_list],
        in_specs=[ANY] * n,
        out_specs=[ANY] * n,
        scratch_shapes=[pltpu.SemaphoreType.DMA((7 * n,)), pltpu.SemaphoreType.DMA((7 * n,)),
                        pltpu.SemaphoreType.DMA((n,))],
        name=name,
    )(*xs_list)
    return list(outs)


def _exchange_sibling(send_list):
    n = len(send_list)

    def body(*refs):
        s_refs, land_refs = refs[:n], refs[n:2 * n]
        send_sems, recv_sems = refs[2 * n:]
        x, y, c = _mesh_pos()
        cps = [pltpu.make_async_remote_copy(src_ref=s_refs[a], dst_ref=land_refs[a], send_sem=send_sems.at[a],
                                            recv_sem=recv_sems.at[a], device_id=(x, y, 1 - c), device_id_type=MESH_ID)
               for a in range(n)]
        for cp in cps:
            cp.start()
        for cp in cps:
            cp.wait()

    outs = pl.pallas_call(
        body,
        out_shape=[jax.ShapeDtypeStruct(t.shape, t.dtype) for t in send_list],
        in_specs=[ANY] * n,
        out_specs=[ANY] * n,
        scratch_shapes=[pltpu.SemaphoreType.DMA((n,)), pltpu.SemaphoreType.DMA((n,))],
        name="rs_exchange_sibling",
    )(*send_list)
    return list(outs)


def _exchange_chips(partial_list):
    n = len(partial_list)

    def body(*refs):
        p_refs, land_refs = refs[:n], refs[n:2 * n]
        send_sems, recv_sems = refs[2 * n:]
        x, y, c = _mesh_pos()
        chips = [(1 - x, y), (x, 1 - y), (1 - x, 1 - y)]
        cps = [pltpu.make_async_remote_copy(src_ref=p_refs[a].at[2 * cx + cy], dst_ref=land_refs[a].at[j],
                                            send_sem=send_sems.at[3 * a + j], recv_sem=recv_sems.at[3 * a + j],
                                            device_id=(cx, cy, c), device_id_type=MESH_ID)
               for a in range(n) for j, (cx, cy) in enumerate(chips)]
        for cp in cps:
            cp.start()
        for cp in cps:
            cp.wait()

    outs = pl.pallas_call(
        body,
        out_shape=[jax.ShapeDtypeStruct((3,) + t.shape[1:], t.dtype) for t in partial_list],
        in_specs=[ANY] * n,
        out_specs=[ANY] * n,
        scratch_shapes=[pltpu.SemaphoreType.DMA((3 * n,)), pltpu.SemaphoreType.DMA((3 * n,))],
        name="rs_exchange_chips",
    )(*partial_list)
    return list(outs)


def _sum_pairs(a, b):
    shape = a.shape
    a = a.reshape(shape[0], -1, shape[-1])
    b = b.reshape(a.shape)
    n, r, cdim = a.shape
    tr = _pick(r, (1024, 512, 256, 128, 64))

    def body(a_ref, b_ref, o_ref):
        o_ref[...] = (a_ref[...].astype(F32) + b_ref[...].astype(F32)).astype(o_ref.dtype)

    blk = pl.BlockSpec((1, tr, cdim), lambda i, j: (i, j, 0))
    return pl.pallas_call(
        body, grid=(n, r // tr), in_specs=[blk, blk], out_specs=blk,
        out_shape=jax.ShapeDtypeStruct(a.shape, a.dtype),
        compiler_params=_params(("parallel", "parallel")), name="rs_sum_pairs",
    )(a, b).reshape(shape)


def _sum_final(own, land):
    shape = own.shape
    own = own.reshape(-1, shape[-1])
    land = land.reshape((3,) + own.shape)
    r, cdim = own.shape
    tr = _pick(r, (1024, 512, 256, 128, 64))

    def body(o_ref, l_ref, out_ref):
        acc = o_ref[...].astype(F32)
        for j in range(3):
            acc = acc + l_ref[j].astype(F32)
        out_ref[...] = acc

    return pl.pallas_call(
        body, grid=(r // tr,),
        in_specs=[pl.BlockSpec((tr, cdim), lambda i: (i, 0)), pl.BlockSpec((3, tr, cdim), lambda i: (0, i, 0))],
        out_specs=pl.BlockSpec((tr, cdim), lambda i: (i, 0)),
        out_shape=jax.ShapeDtypeStruct((r, cdim), F32),
        compiler_params=_params(("parallel",)), name="rs_sum_final",
    )(own, land).reshape(shape)


def _sum_devices(gathered):
    n, r, cdim = gathered.shape
    tr = _pick(r, (64, 32, 16, 8))

    def body(g_ref, out_ref):
        acc = g_ref[0].astype(F32)
        for q in range(1, n):
            acc = acc + g_ref[q].astype(F32)
        out_ref[...] = acc

    return pl.pallas_call(
        body, grid=(r // tr,),
        in_specs=[pl.BlockSpec((n, tr, cdim), lambda i: (0, i, 0))],
        out_specs=pl.BlockSpec((tr, cdim), lambda i: (i, 0)),
        out_shape=jax.ShapeDtypeStruct((r, cdim), F32),
        compiler_params=_params(("parallel",)), name="sum_devices",
    )(gathered)


def _adamw(w, g, m, v):
    shape = w.shape
    cols = shape[-1]
    rows = w.size // cols
    tr = _pick(rows, (512, 256, 128, 64, 32, 16, 8))
    c1 = 1.0 - ADAM_B1 ** ADAM_STEP
    c2 = 1.0 - ADAM_B2 ** ADAM_STEP

    def body(w_ref, g_ref, m_ref, v_ref, d_ref, nm_ref, nv_ref):
        gv = g_ref[...]
        m2 = ADAM_B1 * m_ref[...] + (1.0 - ADAM_B1) * gv
        v2 = ADAM_B2 * v_ref[...] + (1.0 - ADAM_B2) * (gv * gv)
        d_ref[...] = -ADAM_LR * ((m2 / c1) / (jnp.sqrt(v2 / c2) + ADAM_EPS) + ADAM_WD * w_ref[...])
        nm_ref[...] = m2
        nv_ref[...] = v2

    blk = pl.BlockSpec((tr, cols), lambda i: (i, 0))
    sds = jax.ShapeDtypeStruct((rows, cols), F32)
    outs = pl.pallas_call(
        body, grid=(rows // tr,), in_specs=[blk] * 4, out_specs=[blk] * 3, out_shape=[sds] * 3,
        compiler_params=_params(("parallel",)), name=f"adamw_{rows}x{cols}",
    )(*(t.reshape(rows, cols) for t in (w, g, m, v)))
    return tuple(o.reshape(shape) for o in outs)


WEIGHTS = ("norm_mix", "norm_ffn", "ssd_w_in", "ssd_conv_w", "ssd_conv_b", "ssd_dt_bias", "ssd_a_log", "ssd_d",
           "ssd_norm_w", "ssd_w_out", "gmlp_w_in", "gmlp_b_in", "gmlp_ln_w", "gmlp_ln_b", "gmlp_w_s", "gmlp_b_s",
           "gmlp_w_out", "ffn_w_gate", "ffn_w_up", "ffn_w_down", "ple_w_proj", "ple_norm", "ple_gate_norm",
           "ple_w_gate", "final_norm")
ARG_NAMES = ("x", "p") + WEIGHTS + ("loss_target",) + tuple("m_" + n for n in WEIGHTS) + tuple("v_" + n for n in WEIGHTS)
SHARD_AXIS = {"ssd_w_in": 2, "ssd_conv_w": 2, "ssd_w_out": 1, "gmlp_w_in": 2, "gmlp_b_in": 1, "gmlp_ln_w": 1,
              "gmlp_ln_b": 1, "gmlp_w_out": 1, "ffn_w_gate": 2, "ffn_w_up": 2, "ffn_w_down": 1, "ple_w_proj": 2,
              "ple_w_gate": 1}
GATHER_BF16 = ("ssd_w_in", "ssd_w_out", "gmlp_w_in", "gmlp_w_out", "ffn_w_gate", "ffn_w_up", "ffn_w_down",
               "ple_w_proj", "ple_w_gate")
GATHER_F32 = ("ssd_conv_w", "gmlp_b_in", "gmlp_ln_w", "gmlp_ln_b")
SHARDED = GATHER_BF16 + GATHER_F32
WIDE = ("ssd_w_in", "ffn_w_gate", "ffn_w_up")
REPLICATED = tuple(n for n in WEIGHTS if n not in SHARD_AXIS)


def _pack(arrs, dtype, row_mult, lead=0):
    flat = jnp.concatenate([t.reshape(t.shape[:lead] + (-1,)).astype(dtype) for t in arrs], axis=lead)
    n = flat.shape[-1]
    unit = row_mult * PACK_COLS
    total = -(-n // unit) * unit
    flat = jnp.pad(flat, [(0, 0)] * lead + [(0, total - n)])
    return flat.reshape(flat.shape[:lead] + (total // PACK_COLS, PACK_COLS))


def _unpack(buf, names, shapes, lead=0):
    flat = buf.reshape(buf.shape[:lead] + (-1,))
    out, off = {}, 0
    for n in names:
        size = math.prod(shapes[n])
        out[n] = lax.slice_in_dim(flat, off, off + size, axis=lead).reshape(buf.shape[:lead] + tuple(shapes[n]))
        off += size
    return out


ROW_PACKED = ((1024, ("ssd_w_out", "gmlp_w_out", "ffn_w_down", "ple_w_gate")), (512, ("gmlp_w_in",)),
              (128, ("ple_w_proj",)))
ROW_PACK_MULT = 1024


def _pack_rows(arrs, width, lead=0):
    parts = [t.reshape(t.shape[:lead] + (-1, width)).astype(BF16) for t in arrs]
    rows = sum(t.shape[lead] for t in parts)
    pad = -rows % ROW_PACK_MULT
    if pad:
        parts.append(jnp.zeros(parts[0].shape[:lead] + (pad, width), BF16))
    return jnp.concatenate(parts, axis=lead)


def _unpack_rows(buf, names, shapes, lead=0):
    width = buf.shape[-1]
    out, off = {}, 0
    for n in names:
        rows = math.prod(shapes[n]) // width
        out[n] = lax.slice_in_dim(buf, off, off + rows, axis=lead).reshape(buf.shape[:lead] + tuple(shapes[n]))
        off += rows
    return out


def _merge_shards(seg, ax):
    t = jnp.moveaxis(seg, 0, ax)
    return t.reshape(t.shape[:ax] + (t.shape[ax] * t.shape[ax + 1],) + t.shape[ax + 2:])


def _split_for_cores(gfull, ax, c):
    shp = gfull.shape
    t = gfull.reshape(shp[:ax] + (2, 2, 2, shp[ax] // N_DEV) + shp[ax + 1:])

    def take(core):
        u = lax.dynamic_index_in_dim(t, core, axis=ax + 2, keepdims=False)
        u = jnp.moveaxis(u, (ax, ax + 1), (0, 1))
        return u.reshape((4,) + u.shape[2:])

    return take(c), take(1 - c)


def kernel(x, p, norm_mix, norm_ffn, ssd_w_in, ssd_conv_w, ssd_conv_b, ssd_dt_bias, ssd_a_log, ssd_d,
           ssd_norm_w, ssd_w_out, gmlp_w_in, gmlp_b_in, gmlp_ln_w, gmlp_ln_b, gmlp_w_s, gmlp_b_s,
           gmlp_w_out, ffn_w_gate, ffn_w_up, ffn_w_down, ple_w_proj, ple_norm, ple_gate_norm, ple_w_gate,
           final_norm, loss_target, m_norm_mix, m_norm_ffn, m_ssd_w_in, m_ssd_conv_w, m_ssd_conv_b,
           m_ssd_dt_bias, m_ssd_a_log, m_ssd_d, m_ssd_norm_w, m_ssd_w_out, m_gmlp_w_in, m_gmlp_b_in,
           m_gmlp_ln_w, m_gmlp_ln_b, m_gmlp_w_s, m_gmlp_b_s, m_gmlp_w_out, m_ffn_w_gate, m_ffn_w_up,
           m_ffn_w_down, m_ple_w_proj, m_ple_norm, m_ple_gate_norm, m_ple_w_gate, m_final_norm, v_norm_mix,
           v_norm_ffn, v_ssd_w_in, v_ssd_conv_w, v_ssd_conv_b, v_ssd_dt_bias, v_ssd_a_log, v_ssd_d,
           v_ssd_norm_w, v_ssd_w_out, v_gmlp_w_in, v_gmlp_b_in, v_gmlp_ln_w, v_gmlp_ln_b, v_gmlp_w_s,
           v_gmlp_b_s, v_gmlp_w_out, v_ffn_w_gate, v_ffn_w_up, v_ffn_w_down, v_ple_w_proj, v_ple_norm,
           v_ple_gate_norm, v_ple_w_gate, v_final_norm):
    given = locals()
    a = {n: given[n] for n in ARG_NAMES}
    mx, my, c = _mesh_pos()
    xs = a["x"][0]
    ps = a["p"][:, 0]
    target = a["loss_target"][0]
    shard_shapes = {n: a[n].shape for n in WEIGHTS}

    full = {n: a[n] for n in REPLICATED}
    row_packs = [_pack_rows([a[n] for n in names], wd) for wd, names in ROW_PACKED]
    got = _all_gather(row_packs + [_pack([a[n] for n in GATHER_F32], F32, 8)] + [_to_bf16(a[n]) for n in WIDE],
                      "ag_weights")
    for (wd, names), buf in zip(ROW_PACKED, got):
        for n, seg in _unpack_rows(buf, names, shard_shapes, lead=1).items():
            full[n] = _merge_shards(seg, SHARD_AXIS[n])
    k0 = len(ROW_PACKED)
    for n, seg in _unpack(got[k0], GATHER_F32, shard_shapes, lead=1).items():
        full[n] = _merge_shards(seg, SHARD_AXIS[n])
    full["ssd_w_zx"], full["ssd_w_dt"] = _cat_ssd_in(got[k0 + 1])
    full["ffn_w_gu"] = _cat_ffn(got[k0 + 2], got[k0 + 3])

    lpart, dx, g = _local_step(xs, ps, target, _kernel_layouts(full))
    gfull = _reference_layouts(g, wide=False)
    loss = lax.psum(lpart, ("x", "y", "c"))

    def by_core(t):
        u = t.reshape((4, 2) + t.shape[1:])
        return (lax.dynamic_index_in_dim(u, c, axis=1, keepdims=False),
                lax.dynamic_index_in_dim(u, 1 - c, axis=1, keepdims=False))

    def layer_halves(gl, ax):
        if ax == 0:
            t = gl.reshape(4, 2, -1, gl.shape[-1])
            return tuple(lax.dynamic_index_in_dim(t, cc, axis=1, keepdims=False) for cc in (c, 1 - c))
        t = gl.reshape(gl.shape[0], 4, 2, -1)
        return tuple(jnp.moveaxis(lax.dynamic_index_in_dim(t, cc, axis=2, keepdims=False), 1, 0) for cc in (c, 1 - c))

    pairs = []
    for wd, names in ROW_PACKED:
        hs = [layer_halves(gl, SHARD_AXIS[n] - 1) for n in names for gl in g[n]]
        pairs.append(tuple(_pack_rows([h[i] for h in hs], wd, lead=1) for i in (0, 1)))
    halves = [_split_for_cores(gfull[n], SHARD_AXIS[n], c) for n in GATHER_F32]
    pairs.append((_pack([h[0] for h in halves], BF16, 16, lead=1), _pack([h[1] for h in halves], BF16, 16, lead=1)))
    pairs += [by_core(t) for t in (_split_ssd_in(g["ssd_w_zx"], g["ssd_w_dt"]),) + tuple(_split_ffn(g["ffn_w_gu"]))]
    landed = _exchange_sibling([s for _, s in pairs])
    partials = [_sum_pairs(k, l) for (k, _), l in zip(pairs, landed)]
    landed = _exchange_chips(partials)
    sums = [_sum_final(lax.dynamic_index_in_dim(t, 2 * mx + my, axis=0, keepdims=False), l)
            for t, l in zip(partials, landed)]
    gshard = {}
    for (wd, names), buf in zip(ROW_PACKED, sums):
        gshard.update(_unpack_rows(buf, names, shard_shapes))
    gshard.update(_unpack(sums[k0], GATHER_F32, shard_shapes))
    gshard.update(zip(WIDE, sums[k0 + 1:]))
    rep = _all_gather([_pack([gfull[n] for n in REPLICATED], BF16, 64)], "ag_replicated_grads")[0]
    grep = _unpack(_sum_devices(rep), REPLICATED, shard_shapes)
    grads = {**gshard, **grep}

    upd = {n: _adamw(a[n], grads[n], a["m_" + n], a["v_" + n]) for n in WEIGHTS}
    return (loss, dx[None], *[grads[n] for n in WEIGHTS], *[upd[n][0] for n in WEIGHTS],
            *[upd[n][1] for n in WEIGHTS], *[upd[n][2] for n in WEIGHTS])
```

```python
import math

import jax
import jax.numpy as jnp
from jax import lax
from jax.experimental import pallas as pl
from jax.experimental.pallas import tpu as pltpu

F32 = jnp.float32
BF16 = jnp.bfloat16

N_DEV = 8
D_MODEL = 1024
DEPTH = 4
SSD_INNER = 2048
SSD_HEADS = 32
SSD_HEADDIM = 64
SSD_GROUPS = 8
SSD_STATE = 128
SSD_GROUP_W = SSD_INNER // SSD_GROUPS
SSD_CONV_DIM = SSD_INNER + 2 * SSD_GROUPS * SSD_STATE
SSD_IN_DIM = 2 * SSD_INNER + SSD_CONV_DIM - SSD_INNER + SSD_HEADS
SSD_ZX = SSD_INNER + SSD_CONV_DIM
CONV_K = 4
CHUNK = 128
GMLP_INNER = 2048
GMLP_GROUPS = 16
FFN_DIM = 2816
PLE_DIM = 256
RMS_EPS = 1e-6
LN_EPS = 1e-5
LANES = 128
VMEM_LIMIT = 56 * 1024 * 1024

ADAM_LR = 0.001
ADAM_B1 = 0.9
ADAM_B2 = 0.999
ADAM_EPS = 1e-08
ADAM_WD = 0.01
ADAM_STEP = 10

MESH_ID = pl.DeviceIdType.MESH


def _pick(n, cands):
    for c in cands:
        if c <= n and n % c == 0:
            return c
    return n


def _params(dims):
    return pltpu.CompilerParams(dimension_semantics=dims, vmem_limit_bytes=VMEM_LIMIT)


def _dot(a, b, dims=(((1,), (0,)), ((), ())), precision=None):
    return lax.dot_general(a, b, dims, precision=precision, preferred_element_type=F32)


NN = (((1,), (0,)), ((), ()))
NT = (((1,), (1,)), ((), ()))
TN = (((0,), (0,)), ((), ()))


def _sigmoid(x):
    return 1.0 / (1.0 + jnp.exp(-x))


def _dot01(a, b, dims, split, terms=3):
    v = (a, b)[split]
    ones = (a, b)[1 - split].astype(BF16)
    acc = None
    for _ in range(terms):
        piece = v.astype(BF16)
        v = v - piece.astype(F32)
        part = _dot(piece, ones, dims) if split == 0 else _dot(ones, piece, dims)
        acc = part if acc is None else acc + part
    return acc


MM_VMEM_BUDGET = 36 * 1024 * 1024


def _mm_tiles(mode, m, n, k, a_bytes, b_bytes, out_bytes, has_add):
    tm = _pick(m, (1408, 1024, 512, 256, 128))
    tn_cands = [c for c in (2816, 1024, 512, 256, 128) if c <= n and n % c == 0] or [n]
    tk_cands = [k] + [c for c in (2816, 2048, 1024, 512, 256, 128) if c < k and k % c == 0]
    for tk in tk_cands:
        for tn in tn_cands:
            blocks = tm * tk * a_bytes + tk * tn * b_bytes + tm * tn * (out_bytes + (4 if has_add else 0))
            if 2 * blocks + (tm * tn * 4 if tk < k else 0) <= MM_VMEM_BUDGET:
                return tm, tn, tk
    return tm, tn_cands[-1], tk_cands[-1]


def _mm(a, b, mode, out_dtype, add=None):
    if mode == "nn":
        m, k = a.shape
        n = b.shape[1]
    elif mode == "nt":
        m, k = a.shape
        n = b.shape[0]
    else:
        k, m = a.shape
        n = b.shape[1]
    tm, tn, tk = _mm_tiles(mode, m, n, k, a.dtype.itemsize, b.dtype.itemsize, jnp.dtype(out_dtype).itemsize,
                           add is not None)
    nk = k // tk
    dims = {"nn": NN, "nt": NT, "tn": TN}[mode]

    def body(*refs):
        if add is None:
            a_ref, b_ref, o_ref = refs[:3]
            add_ref = None
            rest = refs[3:]
        else:
            a_ref, b_ref, add_ref, o_ref = refs[:4]
            rest = refs[4:]
        part = _dot(a_ref[...].astype(BF16), b_ref[...].astype(BF16), dims)

        def finish(acc):
            if add_ref is not None:
                acc = acc + add_ref[...]
            o_ref[...] = acc.astype(o_ref.dtype)

        if nk == 1:
            finish(part)
        else:
            acc_ref = rest[0]
            kk = pl.program_id(2)

            @pl.when(kk == 0)
            def _():
                acc_ref[...] = part

            @pl.when(kk > 0)
            def _():
                acc_ref[...] += part

            @pl.when(kk == nk - 1)
            def _():
                finish(acc_ref[...])

    if mode == "nn":
        a_spec = pl.BlockSpec((tm, tk), lambda i, j, kk: (i, kk))
        b_spec = pl.BlockSpec((tk, tn), lambda i, j, kk: (kk, j))
    elif mode == "nt":
        a_spec = pl.BlockSpec((tm, tk), lambda i, j, kk: (i, kk))
        b_spec = pl.BlockSpec((tn, tk), lambda i, j, kk: (j, kk))
    else:
        a_spec = pl.BlockSpec((tk, tm), lambda i, j, kk: (kk, i))
        b_spec = pl.BlockSpec((tk, tn), lambda i, j, kk: (kk, j))
    o_spec = pl.BlockSpec((tm, tn), lambda i, j, kk: (i, j))
    in_specs = [a_spec, b_spec] + ([o_spec] if add is not None else [])
    args = (a, b) + ((add,) if add is not None else ())
    return pl.pallas_call(
        body,
        grid=(m // tm, n // tn, nk),
        in_specs=in_specs,
        out_specs=o_spec,
        out_shape=jax.ShapeDtypeStruct((m, n), out_dtype),
        scratch_shapes=[pltpu.VMEM((tm, tn), F32)] if nk > 1 else [],
        compiler_params=_params(("parallel", "parallel", "arbitrary")),
        name=f"mm_{mode}_{m}x{k}x{n}",
    )(*args)


def _rms_fwd(x, w):
    s, d = x.shape
    tr = _pick(s, (512, 256, 128))

    def body(x_ref, w_ref, o_ref):
        xv = x_ref[...]
        r = lax.rsqrt(jnp.mean(xv * xv, axis=-1, keepdims=True) + RMS_EPS)
        o_ref[...] = (xv * r * w_ref[...]).astype(o_ref.dtype)

    return pl.pallas_call(
        body,
        grid=(s // tr,),
        in_specs=[pl.BlockSpec((tr, d), lambda i: (i, 0)), pl.BlockSpec((1, d), lambda i: (0, 0))],
        out_specs=pl.BlockSpec((tr, d), lambda i: (i, 0)),
        out_shape=jax.ShapeDtypeStruct((s, d), BF16),
        compiler_params=_params(("parallel",)),
        name="rms_fwd",
    )(x, w)


def _rms_bwd(dyn, x, w, add):
    s, d = x.shape
    tr = _pick(s, (512, 256, 128))

    def body(dy_ref, x_ref, w_ref, add_ref, dx_ref, dw_ref):
        xv = x_ref[...]
        dy = dy_ref[...].astype(F32)
        r = lax.rsqrt(jnp.mean(xv * xv, axis=-1, keepdims=True) + RMS_EPS)
        xn = xv * r
        dxh = dy * w_ref[...]
        dx = r * (dxh - xn * jnp.mean(dxh * xn, axis=-1, keepdims=True))
        dx_ref[...] = add_ref[...] + dx
        part = jnp.sum(dy * xn, axis=0, keepdims=True)

        @pl.when(pl.program_id(0) == 0)
        def _():
            dw_ref[...] = part

        @pl.when(pl.program_id(0) > 0)
        def _():
            dw_ref[...] += part

    row = pl.BlockSpec((tr, d), lambda i: (i, 0))
    vec = pl.BlockSpec((1, d), lambda i: (0, 0))
    return pl.pallas_call(
        body,
        grid=(s // tr,),
        in_specs=[row, row, vec, row],
        out_specs=[row, vec],
        out_shape=[jax.ShapeDtypeStruct((s, d), F32), jax.ShapeDtypeStruct((1, d), F32)],
        compiler_params=_params(("arbitrary",)),
        name="rms_bwd",
    )(dyn, x, w, add)


CONV_ROWS = 256
CONV_COLS = 256
CONV_HALO = 16


def _conv_taps(ext, w, base, rows):
    acc = w[0:1, :] * ext[base:base + rows]
    for k in range(1, CONV_K):
        acc = acc + w[k:k + 1, :] * ext[base + k:base + k + rows]
    return acc


def _ssd_conv_fwd(zx, conv_w, conv_b):
    s = zx.shape[0]
    c = SSD_CONV_DIM
    nsteps = s // CONV_ROWS
    off = SSD_INNER // CONV_COLS

    def body(x_ref, w_ref, b_ref, o_ref):
        w = w_ref[...]
        b = b_ref[...]

        def step(i, carry):
            r0 = pl.multiple_of(i * CONV_ROWS, CONV_ROWS)
            cur = x_ref[pl.ds(r0, CONV_ROWS), :].astype(F32)
            p0 = pl.multiple_of(jnp.maximum(r0 - CONV_HALO, 0), CONV_HALO)
            prev = x_ref[pl.ds(p0, CONV_HALO), :].astype(F32)
            prev = jnp.where(i == 0, 0.0, prev)
            ext = jnp.concatenate([prev, cur], axis=0)
            acc = _conv_taps(ext, w, CONV_HALO - (CONV_K - 1), CONV_ROWS) + b
            o_ref[pl.ds(r0, CONV_ROWS), :] = (acc * _sigmoid(acc)).astype(o_ref.dtype)
            return carry

        lax.fori_loop(0, nsteps, step, 0)

    return pl.pallas_call(
        body,
        grid=(c // CONV_COLS,),
        in_specs=[pl.BlockSpec((s, CONV_COLS), lambda j: (0, j + off)),
                  pl.BlockSpec((8, CONV_COLS), lambda j: (0, j)),
                  pl.BlockSpec((1, CONV_COLS), lambda j: (0, j))],
        out_specs=pl.BlockSpec((s, CONV_COLS), lambda j: (0, j)),
        out_shape=jax.ShapeDtypeStruct((s, c), BF16),
        compiler_params=_params(("parallel",)),
        name="ssd_conv_fwd",
    )(zx, conv_w, conv_b)


def _ssd_conv_bwd(zx, dxbc, conv_w, conv_b, dzx):
    s = zx.shape[0]
    c = SSD_CONV_DIM
    nsteps = s // CONV_ROWS
    off = SSD_INNER // CONV_COLS

    def body(x_ref, dy_ref, w_ref, b_ref, dzx_in_ref, dx_ref, dw_ref, db_ref, dc_ref):
        w = w_ref[...]
        b = b_ref[...]
        dc_ref[pl.ds(s, CONV_HALO), :] = jnp.zeros((CONV_HALO, CONV_COLS), F32)

        def step1(i, carry):
            dw0, dw1, dw2, dw3, dbs = carry
            r0 = pl.multiple_of(i * CONV_ROWS, CONV_ROWS)
            cur = x_ref[pl.ds(r0, CONV_ROWS), :].astype(F32)
            p0 = pl.multiple_of(jnp.maximum(r0 - CONV_HALO, 0), CONV_HALO)
            prev = x_ref[pl.ds(p0, CONV_HALO), :].astype(F32)
            prev = jnp.where(i == 0, 0.0, prev)
            ext = jnp.concatenate([prev, cur], axis=0)
            base = CONV_HALO - (CONV_K - 1)
            acc = _conv_taps(ext, w, base, CONV_ROWS) + b
            sg = _sigmoid(acc)
            dcv = dy_ref[pl.ds(r0, CONV_ROWS), :].astype(F32) * (sg * (1.0 + acc * (1.0 - sg)))
            dc_ref[pl.ds(r0, CONV_ROWS), :] = dcv
            dws = [jnp.sum(dcv * ext[base + k:base + k + CONV_ROWS], axis=0, keepdims=True) for k in range(CONV_K)]
            return (dw0 + dws[0], dw1 + dws[1], dw2 + dws[2], dw3 + dws[3], dbs + jnp.sum(dcv, axis=0, keepdims=True))

        z = jnp.zeros((1, CONV_COLS), F32)
        dw0, dw1, dw2, dw3, dbs = lax.fori_loop(0, nsteps, step1, (z, z, z, z, z))
        dw_ref[...] = jnp.concatenate([dw0, dw1, dw2, dw3, z, z, z, z], axis=0)
        db_ref[...] = dbs

        def step2(i, carry):
            r0 = pl.multiple_of(i * CONV_ROWS, CONV_ROWS)
            ext = dc_ref[pl.ds(r0, CONV_ROWS + CONV_HALO), :]
            acc = w[0:1, :] * ext[CONV_K - 1:CONV_K - 1 + CONV_ROWS]
            for k in range(1, CONV_K):
                acc = acc + w[k:k + 1, :] * ext[CONV_K - 1 - k:CONV_K - 1 - k + CONV_ROWS]
            dx_ref[pl.ds(r0, CONV_ROWS), :] = acc.astype(dx_ref.dtype)
            return carry

        lax.fori_loop(0, nsteps, step2, 0)

    col = pl.BlockSpec((s, CONV_COLS), lambda j: (0, j))
    shifted = pl.BlockSpec((s, CONV_COLS), lambda j: (0, j + off))
    return pl.pallas_call(
        body,
        grid=(c // CONV_COLS,),
        in_specs=[shifted, col,
                  pl.BlockSpec((8, CONV_COLS), lambda j: (0, j)),
                  pl.BlockSpec((1, CONV_COLS), lambda j: (0, j)),
                  pl.BlockSpec(memory_space=pl.ANY)],
        out_specs=[shifted, pl.BlockSpec((8, CONV_COLS), lambda j: (0, j)), pl.BlockSpec((1, CONV_COLS), lambda j: (0, j))],
        out_shape=[jax.ShapeDtypeStruct((s, SSD_ZX), BF16), jax.ShapeDtypeStruct((8, c), F32),
                   jax.ShapeDtypeStruct((1, c), F32)],
        scratch_shapes=[pltpu.VMEM((s + CONV_HALO, CONV_COLS), F32)],
        input_output_aliases={4: 0},
        compiler_params=_params(("parallel",)),
        name="ssd_conv_bwd",
    )(zx, dxbc, conv_w, conv_b, dzx)


def _ssd_consts():
    li = lax.broadcasted_iota(jnp.int32, (CHUNK, CHUNK), 0)
    si = lax.broadcasted_iota(jnp.int32, (CHUNK, CHUNK), 1)
    tril = li >= si
    hrow = lax.broadcasted_iota(jnp.int32, (LANES, SSD_INNER), 0)
    hcol = lax.broadcasted_iota(jnp.int32, (LANES, SSD_INNER), 1) // SSD_HEADDIM
    expand = (hrow == hcol).astype(F32)
    return tril, expand


def _ssd_chunk_common(dtp_ref, bias_ref, alog_ref, tril, expand):
    lane = lax.broadcasted_iota(jnp.int32, (1, LANES), 1)
    valid = lane < SSD_HEADS
    pre = dtp_ref[...] + bias_ref[...]
    dt = jnp.where(valid, jnp.maximum(pre, 0.0) + jnp.log1p(jnp.exp(-jnp.abs(pre))), 0.0)
    a = jnp.where(valid, -jnp.exp(alog_ref[...]), 0.0)
    da = dt * a
    cs = _dot01(tril.astype(F32), da, NN, 1)
    cs_x = _dot01(cs, expand, NN, 0)
    dt_x = _dot01(dt, expand, NN, 0, terms=2)
    return pre, dt, a, cs, cs_x, dt_x


def _ssd_scan_fwd(xbc, dtp, dt_bias, a_log, d_skip):
    s = xbc.shape[0]
    nc = s // CHUNK
    gw = SSD_GROUP_W

    def body(xbc_ref, dtp_ref, bias_ref, alog_ref, d_ref, y_ref, prev_ref, state_ref):
        c = pl.program_id(0)

        @pl.when(c == 0)
        def _():
            state_ref[...] = jnp.zeros_like(state_ref)

        tril, expand = _ssd_consts()
        pre, dt, a, cs, cs_x, dt_x = _ssd_chunk_common(dtp_ref, bias_ref, alog_ref, tril, expand)
        cs_t = cs.T
        d_x = _dot01(jnp.broadcast_to(d_ref[...], (8, LANES)), expand, NN, 0)[0:1, :]
        cs_last = cs_x[CHUNK - 1:CHUNK, :]
        dec_out = jnp.exp(cs_x)
        dec_st = jnp.exp(cs_last - cs_x)
        dec_ch = jnp.exp(cs_last)
        x = xbc_ref[:, 0:SSD_INNER].astype(F32)
        xr = x * dt_x
        xrs = xr * dec_st
        lane_g = lax.broadcasted_iota(jnp.int32, (1, gw), 1) // SSD_HEADDIM
        for g in range(SSD_GROUPS):
            sl = slice(g * gw, (g + 1) * gw)
            bg = xbc_ref[:, SSD_INNER + g * SSD_STATE:SSD_INNER + (g + 1) * SSD_STATE]
            cg = xbc_ref[:, SSD_INNER + (SSD_GROUPS + g) * SSD_STATE:SSD_INNER + (SSD_GROUPS + g + 1) * SSD_STATE]
            cb = _dot(cg, bg, NT)
            prev_g = state_ref[:, sl]
            prev_ref[0, :, sl] = prev_g
            yo = _dot(cg, prev_g.astype(BF16), NN) * dec_out[:, sl]
            xr_g = xr[:, sl]
            yd = jnp.zeros((CHUNK, gw), F32)
            for r in range(SSD_HEADS // SSD_GROUPS):
                h = g * (SSD_HEADS // SSD_GROUPS) + r
                diff = cs[:, h:h + 1] - cs_t[h:h + 1, :]
                lmat = jnp.exp(jnp.where(tril, diff, -1e30))
                wmat = (cb * lmat).astype(BF16)
                xr_h = jnp.where(lane_g == r, xr_g, 0.0).astype(BF16)
                yd = yd + _dot(wmat, xr_h, NN)
            y_ref[:, sl] = yd + yo + x[:, sl] * d_x[:, sl]
            sc = _dot(bg, xrs[:, sl].astype(BF16), TN)
            state_ref[:, sl] = prev_g * dec_ch[:, sl] + sc

    vec = pl.BlockSpec((1, LANES), lambda c: (0, 0))
    return pl.pallas_call(
        body,
        grid=(nc,),
        in_specs=[pl.BlockSpec((CHUNK, SSD_CONV_DIM), lambda c: (c, 0)),
                  pl.BlockSpec((CHUNK, LANES), lambda c: (c, 0)), vec, vec, vec],
        out_specs=[pl.BlockSpec((CHUNK, SSD_INNER), lambda c: (c, 0)),
                   pl.BlockSpec((1, SSD_STATE, SSD_INNER), lambda c: (c, 0, 0))],
        out_shape=[jax.ShapeDtypeStruct((s, SSD_INNER), F32), jax.ShapeDtypeStruct((nc, SSD_STATE, SSD_INNER), F32)],
        scratch_shapes=[pltpu.VMEM((SSD_STATE, SSD_INNER), F32)],
        compiler_params=_params(("arbitrary",)),
        name="ssd_scan_fwd",
    )(xbc, dtp, dt_bias, a_log, d_skip)


def _ssd_scan_bwd(xbc, dtp, prev, dy, dt_bias, a_log, d_skip):
    s = xbc.shape[0]
    nc = s // CHUNK
    gw = SSD_GROUP_W
    hpg = SSD_HEADS // SSD_GROUPS

    def body(xbc_ref, dtp_ref, prev_ref, dy_ref, bias_ref, alog_ref, d_ref,
             dxbc_ref, ddtp_ref, dbias_ref, dalog_ref, dd_ref, dp_ref, ddx_ref):
        step = pl.program_id(0)

        @pl.when(step == 0)
        def _():
            dp_ref[...] = jnp.zeros_like(dp_ref)
            ddx_ref[...] = jnp.zeros_like(ddx_ref)
            dbias_ref[...] = jnp.zeros_like(dbias_ref)
            dalog_ref[...] = jnp.zeros_like(dalog_ref)

        tril, expand = _ssd_consts()
        pre, dt, a, cs, cs_x, dt_x = _ssd_chunk_common(dtp_ref, bias_ref, alog_ref, tril, expand)
        cs_t = cs.T
        d_x = _dot01(jnp.broadcast_to(d_ref[...], (8, LANES)), expand, NN, 0)[0:1, :]
        cs_last = cs_x[CHUNK - 1:CHUNK, :]
        dec_out = jnp.exp(cs_x)
        dec_st = jnp.exp(cs_last - cs_x)
        dec_ch = jnp.exp(cs_last)
        x = xbc_ref[:, 0:SSD_INNER].astype(F32)
        dyv = dy_ref[...]
        xr = x * dt_x
        xrs = xr * dec_st
        lane_g = lax.broadcasted_iota(jnp.int32, (1, gw), 1) // SSD_HEADDIM
        hsel = lax.broadcasted_iota(jnp.int32, (CHUNK, LANES), 1)
        dcs = jnp.zeros((CHUNK, LANES), F32)
        last_parts = []
        t_parts = []
        dxr_parts = []
        for g in range(SSD_GROUPS):
            sl = slice(g * gw, (g + 1) * gw)
            bsl = slice(SSD_INNER + g * SSD_STATE, SSD_INNER + (g + 1) * SSD_STATE)
            csl = slice(SSD_INNER + (SSD_GROUPS + g) * SSD_STATE, SSD_INNER + (SSD_GROUPS + g + 1) * SSD_STATE)
            bg = xbc_ref[:, bsl]
            cg = xbc_ref[:, csl]
            cb = _dot(cg, bg, NT)
            prev_g = prev_ref[0, :, sl]
            prev_b = prev_g.astype(BF16)
            dp_g = dp_ref[:, sl]
            dp_b = dp_g.astype(BF16)
            dy_g = dyv[:, sl]
            xr_g = xr[:, sl]
            gmat = _dot(cg, prev_b, NN)
            dgm = (dy_g * dec_out[:, sl]).astype(BF16)
            dc_g = _dot(dgm, prev_b, NT)
            dprev = _dot(cg, dgm, TN)
            t1 = dy_g * gmat * dec_out[:, sl]
            mm_ = _dot(bg, dp_b, NN)
            db_g = _dot(xrs[:, sl].astype(BF16), dp_b, NT)
            dxr_g = mm_ * dec_st[:, sl]
            t2 = dxr_g * xr_g
            last = jnp.sum(t2, axis=0, keepdims=True) + jnp.sum(dp_g * prev_g, axis=0, keepdims=True) * dec_ch[:, sl]
            dp_ref[:, sl] = dp_g * dec_ch[:, sl] + dprev
            dcb = jnp.zeros((CHUNK, CHUNK), F32)
            for r in range(hpg):
                h = g * hpg + r
                diff = cs[:, h:h + 1] - cs_t[h:h + 1, :]
                lmat = jnp.exp(jnp.where(tril, diff, -1e30))
                wmat = cb * lmat
                dy_h = jnp.where(lane_g == r, dy_g, 0.0).astype(BF16)
                dw = _dot(dy_h, xr_g.astype(BF16), NT)
                dxr_g = dxr_g + _dot(wmat.astype(BF16), dy_h, TN)
                dcb = dcb + dw * lmat
                q = (dw * wmat).astype(BF16)
                onehot = (hsel == h).astype(BF16)
                dcs = dcs + _dot(q, onehot, NN) - _dot(q, onehot, TN)
            dcb_b = dcb.astype(BF16)
            dc_g = dc_g + _dot(dcb_b, bg, NN)
            db_g = db_g + _dot(dcb_b, cg, TN)
            dxbc_ref[:, bsl] = db_g.astype(dxbc_ref.dtype)
            dxbc_ref[:, csl] = dc_g.astype(dxbc_ref.dtype)
            t_parts.append(t1 - t2)
            last_parts.append(last)
            dxr_parts.append(dxr_g)
        dxr = jnp.concatenate(dxr_parts, axis=1)
        tt = jnp.concatenate(t_parts, axis=1)
        last_x = jnp.concatenate(last_parts, axis=1)
        dxbc_ref[:, 0:SSD_INNER] = (dxr * dt_x + dyv * d_x).astype(dxbc_ref.dtype)
        dcs = dcs + _dot01(tt, expand, NT, 0, terms=2)
        last_h = _dot01(jnp.broadcast_to(last_x, (8, SSD_INNER)), expand, NT, 0)[0:1, :]
        rowi = lax.broadcasted_iota(jnp.int32, (CHUNK, LANES), 0)
        dcs = dcs + jnp.where(rowi == CHUNK - 1, last_h, 0.0)
        dda = _dot01(tril.astype(F32), dcs, TN, 1)
        ddt = dda * a + _dot01(dxr * x, expand, NT, 0, terms=2)
        dpre = ddt * _sigmoid(pre)
        ddtp_ref[...] = dpre
        dbias_ref[...] += jnp.sum(dpre, axis=0, keepdims=True)
        dalog_ref[...] += jnp.sum(dda * dt, axis=0, keepdims=True) * a
        ddx_ref[...] += jnp.broadcast_to(jnp.sum(dyv * x, axis=0, keepdims=True), (8, SSD_INNER))

        @pl.when(step == nc - 1)
        def _():
            dd_ref[...] = _dot01(ddx_ref[...], expand, NT, 0)[0:1, :]

    rev = lambda c: (nc - 1 - c, 0)
    vec = pl.BlockSpec((1, LANES), lambda c: (0, 0))
    return pl.pallas_call(
        body,
        grid=(nc,),
        in_specs=[pl.BlockSpec((CHUNK, SSD_CONV_DIM), rev), pl.BlockSpec((CHUNK, LANES), rev),
                  pl.BlockSpec((1, SSD_STATE, SSD_INNER), lambda c: (nc - 1 - c, 0, 0)),
                  pl.BlockSpec((CHUNK, SSD_INNER), rev), vec, vec, vec],
        out_specs=[pl.BlockSpec((CHUNK, SSD_CONV_DIM), rev), pl.BlockSpec((CHUNK, LANES), rev), vec, vec, vec],
        out_shape=[jax.ShapeDtypeStruct((s, SSD_CONV_DIM), BF16), jax.ShapeDtypeStruct((s, LANES), F32),
                   jax.ShapeDtypeStruct((1, LANES), F32), jax.ShapeDtypeStruct((1, LANES), F32),
                   jax.ShapeDtypeStruct((1, LANES), F32)],
        scratch_shapes=[pltpu.VMEM((SSD_STATE, SSD_INNER), F32), pltpu.VMEM((8, SSD_INNER), F32)],
        compiler_params=_params(("arbitrary",)),
        name="ssd_scan_bwd",
    )(xbc, dtp, prev, dy, dt_bias, a_log, d_skip)


def _ssd_gate_fwd(y, zx, norm_w):
    s = y.shape[0]
    tr = _pick(s, (256, 128))
    gw = SSD_GROUP_W

    def body(y_ref, z_ref, w_ref, o_ref):
        for g in range(SSD_GROUPS):
            sl = slice(g * gw, (g + 1) * gw)
            z = z_ref[:, sl].astype(F32)
            gv = y_ref[:, sl] * (z * _sigmoid(z))
            r = lax.rsqrt(jnp.mean(gv * gv, axis=-1, keepdims=True) + LN_EPS)
            o_ref[:, sl] = (gv * r * w_ref[:, sl]).astype(o_ref.dtype)

    row = pl.BlockSpec((tr, SSD_INNER), lambda i: (i, 0))
    return pl.pallas_call(
        body,
        grid=(s // tr,),
        in_specs=[row, row, pl.BlockSpec((1, SSD_INNER), lambda i: (0, 0))],
        out_specs=row,
        out_shape=jax.ShapeDtypeStruct((s, SSD_INNER), BF16),
        compiler_params=_params(("parallel",)),
        name="ssd_gate_fwd",
    )(y, zx, norm_w)


def _ssd_gate_bwd(dgn, y, zx, norm_w):
    s = y.shape[0]
    tr = _pick(s, (256, 128))
    gw = SSD_GROUP_W

    def body(dg_ref, y_ref, z_ref, w_ref, dy_ref, dz_ref, dw_ref):
        parts = []
        for g in range(SSD_GROUPS):
            sl = slice(g * gw, (g + 1) * gw)
            z = z_ref[:, sl].astype(F32)
            yv = y_ref[:, sl]
            sg = _sigmoid(z)
            sz = z * sg
            gv = yv * sz
            r = lax.rsqrt(jnp.mean(gv * gv, axis=-1, keepdims=True) + LN_EPS)
            gn = gv * r
            dout = dg_ref[:, sl].astype(F32)
            parts.append(jnp.sum(dout * gn, axis=0, keepdims=True))
            dgn_ = dout * w_ref[:, sl]
            dgv = r * (dgn_ - gn * jnp.mean(dgn_ * gn, axis=-1, keepdims=True))
            dy_ref[:, sl] = dgv * sz
            dz_ref[:, sl] = (dgv * yv * (sg * (1.0 + z * (1.0 - sg)))).astype(dz_ref.dtype)
        part = jnp.concatenate(parts, axis=1)

        @pl.when(pl.program_id(0) == 0)
        def _():
            dw_ref[...] = part

        @pl.when(pl.program_id(0) > 0)
        def _():
            dw_ref[...] += part

    row = pl.BlockSpec((tr, SSD_INNER), lambda i: (i, 0))
    vec = pl.BlockSpec((1, SSD_INNER), lambda i: (0, 0))
    return pl.pallas_call(
        body,
        grid=(s // tr,),
        in_specs=[row, row, row, vec],
        out_specs=[row, row, vec],
        out_shape=[jax.ShapeDtypeStruct((s, SSD_INNER), F32), jax.ShapeDtypeStruct((s, SSD_ZX), BF16),
                   jax.ShapeDtypeStruct((1, SSD_INNER), F32)],
        compiler_params=_params(("arbitrary",)),
        name="ssd_gate_bwd",
    )(dgn, y, zx, norm_w)


INV_SQRT2 = 1.0 / math.sqrt(2.0)
INV_SQRT2PI = 1.0 / math.sqrt(2.0 * math.pi)


def _gelu(x):
    return 0.5 * x * (1.0 + lax.erf(x * INV_SQRT2))


def _gelu_grad(x):
    return 0.5 * (1.0 + lax.erf(x * INV_SQRT2)) + x * INV_SQRT2PI * jnp.exp(-0.5 * x * x)


def _gmlp_act_fwd(pre, b_in, ln_w, ln_b):
    s = pre.shape[0]
    tr = _pick(s, (256, 128))
    n = GMLP_INNER

    def body(p_ref, b_ref, w_ref, lb_ref, u_ref, v_ref):
        u_ref[...] = _gelu(p_ref[:, 0:n].astype(F32) + b_ref[:, 0:n]).astype(u_ref.dtype)
        hv = _gelu(p_ref[:, n:2 * n].astype(F32) + b_ref[:, n:2 * n])
        mu = jnp.mean(hv, axis=-1, keepdims=True)
        xc = hv - mu
        r = lax.rsqrt(jnp.mean(xc * xc, axis=-1, keepdims=True) + LN_EPS)
        v_ref[...] = (xc * r * w_ref[...] + lb_ref[...]).astype(v_ref.dtype)

    half = pl.BlockSpec((tr, n), lambda i: (i, 0))
    vec = pl.BlockSpec((1, n), lambda i: (0, 0))
    return pl.pallas_call(
        body,
        grid=(s // tr,),
        in_specs=[pl.BlockSpec((tr, 2 * n), lambda i: (i, 0)), pl.BlockSpec((1, 2 * n), lambda i: (0, 0)), vec, vec],
        out_specs=[half, half],
        out_shape=[jax.ShapeDtypeStruct((s, n), BF16), jax.ShapeDtypeStruct((s, n), BF16)],
        compiler_params=_params(("parallel",)),
        name="gmlp_act_fwd",
    )(pre, b_in, ln_w, ln_b)


def _gmlp_act_bwd(pre, b_in, ln_w, du, dv):
    s = pre.shape[0]
    tr = _pick(s, (256, 128))
    n = GMLP_INNER

    def body(p_ref, b_ref, w_ref, du_ref, dv_ref, dp_ref, db_ref, dw_ref, dlb_ref):
        xu = p_ref[:, 0:n].astype(F32) + b_ref[:, 0:n]
        dpu = du_ref[...].astype(F32) * _gelu_grad(xu)
        xv = p_ref[:, n:2 * n].astype(F32) + b_ref[:, n:2 * n]
        hv = _gelu(xv)
        mu = jnp.mean(hv, axis=-1, keepdims=True)
        xc = hv - mu
        r = lax.rsqrt(jnp.mean(xc * xc, axis=-1, keepdims=True) + LN_EPS)
        vh = xc * r
        dvv = dv_ref[...].astype(F32)
        dvh = dvv * w_ref[...]
        dh = r * (dvh - jnp.mean(dvh, axis=-1, keepdims=True) - vh * jnp.mean(dvh * vh, axis=-1, keepdims=True))
        dpv = dh * _gelu_grad(xv)
        dp_ref[:, 0:n] = dpu.astype(dp_ref.dtype)
        dp_ref[:, n:2 * n] = dpv.astype(dp_ref.dtype)
        pb = jnp.concatenate([jnp.sum(dpu, axis=0, keepdims=True), jnp.sum(dpv, axis=0, keepdims=True)], axis=1)
        pw = jnp.sum(dvv * vh, axis=0, keepdims=True)
        plb = jnp.sum(dvv, axis=0, keepdims=True)

        @pl.when(pl.program_id(0) == 0)
        def _():
            db_ref[...] = pb
            dw_ref[...] = pw
            dlb_ref[...] = plb

        @pl.when(pl.program_id(0) > 0)
        def _():
            db_ref[...] += pb
            dw_ref[...] += pw
            dlb_ref[...] += plb

    half = pl.BlockSpec((tr, n), lambda i: (i, 0))
    full = pl.BlockSpec((tr, 2 * n), lambda i: (i, 0))
    vec = pl.BlockSpec((1, n), lambda i: (0, 0))
    vec2 = pl.BlockSpec((1, 2 * n), lambda i: (0, 0))
    return pl.pallas_call(
        body,
        grid=(s // tr,),
        in_specs=[full, vec2, vec, half, half],
        out_specs=[full, vec2, vec, vec],
        out_shape=[jax.ShapeDtypeStruct((s, 2 * n), BF16), jax.ShapeDtypeStruct((1, 2 * n), F32),
                   jax.ShapeDtypeStruct((1, n), F32), jax.ShapeDtypeStruct((1, n), F32)],
        compiler_params=_params(("arbitrary",)),
        name="gmlp_act_bwd",
    )(pre, b_in, ln_w, du, dv)


def _gmlp_mix_fwd(u, v, w_s, b_st):
    s = u.shape[0]
    gd = GMLP_INNER // GMLP_GROUPS

    def body(u_ref, v_ref, w_ref, b_ref, o_ref):
        li = lax.broadcasted_iota(jnp.int32, (CHUNK, CHUNK), 0)
        si = lax.broadcasted_iota(jnp.int32, (CHUNK, CHUNK), 1)
        tril = li >= si
        for g in range(GMLP_GROUPS):
            sl = slice(g * gd, (g + 1) * gd)
            wm = jnp.where(tril, w_ref[g], 0.0).astype(BF16)
            mixed = _dot(wm, v_ref[:, sl], NN) + b_ref[:, g:g + 1]
            o_ref[:, sl] = (u_ref[:, sl].astype(F32) * mixed).astype(o_ref.dtype)

    row = pl.BlockSpec((CHUNK, GMLP_INNER), lambda c: (c, 0))
    return pl.pallas_call(
        body,
        grid=(s // CHUNK,),
        in_specs=[row, row, pl.BlockSpec((GMLP_GROUPS, CHUNK, CHUNK), lambda c: (0, 0, 0)),
                  pl.BlockSpec((CHUNK, LANES), lambda c: (0, 0))],
        out_specs=row,
        out_shape=jax.ShapeDtypeStruct((s, GMLP_INNER), BF16),
        compiler_params=_params(("parallel",)),
        name="gmlp_mix_fwd",
    )(u, v, w_s, b_st)


def _gmlp_mix_bwd(dgated, u, v, w_s, b_st):
    s = u.shape[0]
    nc = s // CHUNK
    gd = GMLP_INNER // GMLP_GROUPS

    def body(dg_ref, u_ref, v_ref, w_ref, b_ref, du_ref, dv_ref, dw_ref, db_ref):
        c = pl.program_id(0)

        @pl.when(c == 0)
        def _():
            dw_ref[...] = jnp.zeros_like(dw_ref)
            db_ref[...] = jnp.zeros_like(db_ref)

        li = lax.broadcasted_iota(jnp.int32, (CHUNK, CHUNK), 0)
        si = lax.broadcasted_iota(jnp.int32, (CHUNK, CHUNK), 1)
        tril = li >= si
        lane = lax.broadcasted_iota(jnp.int32, (CHUNK, LANES), 1)
        dbacc = jnp.zeros((CHUNK, LANES), F32)
        for g in range(GMLP_GROUPS):
            sl = slice(g * gd, (g + 1) * gd)
            wm = jnp.where(tril, w_ref[g], 0.0).astype(BF16)
            vg = v_ref[:, sl]
            mixed = _dot(wm, vg, NN) + b_ref[:, g:g + 1]
            dgv = dg_ref[:, sl].astype(F32)
            du_ref[:, sl] = (dgv * mixed).astype(du_ref.dtype)
            dm = dgv * u_ref[:, sl].astype(F32)
            dm_b = dm.astype(BF16)
            dv_ref[:, sl] = _dot(wm, dm_b, TN).astype(dv_ref.dtype)
            dw_ref[g] += jnp.where(tril, _dot(dm_b, vg, NT), 0.0)
            dbacc = dbacc + jnp.where(lane == g, jnp.sum(dm, axis=1, keepdims=True), 0.0)
        db_ref[...] += dbacc

    row = pl.BlockSpec((CHUNK, GMLP_INNER), lambda c: (c, 0))
    wspec = pl.BlockSpec((GMLP_GROUPS, CHUNK, CHUNK), lambda c: (0, 0, 0))
    bspec = pl.BlockSpec((CHUNK, LANES), lambda c: (0, 0))
    return pl.pallas_call(
        body,
        grid=(nc,),
        in_specs=[row, row, row, wspec, bspec],
        out_specs=[row, row, wspec, bspec],
        out_shape=[jax.ShapeDtypeStruct((s, GMLP_INNER), BF16), jax.ShapeDtypeStruct((s, GMLP_INNER), BF16),
                   jax.ShapeDtypeStruct((GMLP_GROUPS, CHUNK, CHUNK), F32), jax.ShapeDtypeStruct((CHUNK, LANES), F32)],
        compiler_params=_params(("arbitrary",)),
        name="gmlp_mix_bwd",
    )(dgated, u, v, w_s, b_st)


FFN_HALF = FFN_DIM // 2


def _ffn_up(un, wgu):
    s, d = un.shape
    f = FFN_DIM
    tm = _pick(s, (1024, 512, 256, 128))
    nh = f // FFN_HALF

    def body(x_ref, wg_ref, wu_ref, g_ref, u_ref, h_ref):
        x = x_ref[...]
        g_ref[...] = _dot(x, wg_ref[...], NN).astype(g_ref.dtype)
        u_ref[...] = _dot(x, wu_ref[...], NN).astype(u_ref.dtype)
        gt = g_ref[...].astype(F32)
        h_ref[...] = (gt * _sigmoid(gt) * u_ref[...].astype(F32)).astype(h_ref.dtype)

    out = pl.BlockSpec((tm, FFN_HALF), lambda i, j: (i, j))
    sds = jax.ShapeDtypeStruct((s, f), BF16)
    return pl.pallas_call(
        body,
        grid=(s // tm, nh),
        in_specs=[pl.BlockSpec((tm, d), lambda i, j: (i, 0)), pl.BlockSpec((d, FFN_HALF), lambda i, j: (0, j)),
                  pl.BlockSpec((d, FFN_HALF), lambda i, j: (0, j + nh))],
        out_specs=[out, out, out],
        out_shape=[sds, sds, sds],
        compiler_params=_params(("parallel", "parallel")),
        name="ffn_up",
    )(un, wgu, wgu)


def _ffn_down_bwd(dh, wd, gate, up):
    s, d = dh.shape
    f = FFN_DIM
    tm = _pick(s, (512, 256, 128))

    def body(dh_ref, wd_ref, g_ref, u_ref, o_ref):
        dhb = dh_ref[...].astype(BF16)
        for half in range(f // FFN_HALF):
            cols = slice(half * FFN_HALF, (half + 1) * FFN_HALF)
            dhid = _dot(dhb, wd_ref[cols, :], NT)
            gt = g_ref[:, cols].astype(F32)
            sg = _sigmoid(gt)
            o_ref[:, cols] = (dhid * u_ref[:, cols].astype(F32) * (sg * (1.0 + gt * (1.0 - sg)))).astype(o_ref.dtype)
            o_ref[:, f + half * FFN_HALF:f + (half + 1) * FFN_HALF] = (dhid * gt * sg).astype(o_ref.dtype)

    row = pl.BlockSpec((tm, f), lambda i: (i, 0))
    return pl.pallas_call(
        body,
        grid=(s // tm,),
        in_specs=[pl.BlockSpec((tm, d), lambda i: (i, 0)), pl.BlockSpec((f, d), lambda i: (0, 0)), row, row],
        out_specs=pl.BlockSpec((tm, 2 * f), lambda i: (i, 0)),
        out_shape=jax.ShapeDtypeStruct((s, 2 * f), BF16),
        compiler_params=_params(("parallel",)),
        name="ffn_down_bwd",
    )(dh, wd, gate, up)


def _ple_fwd(p, h, w_proj, w_gate, gate_norm, ple_norm, next_norm):
    s, d = h.shape
    e = p.shape[1]
    tr = _pick(s, (512, 256, 128))

    def body(p_ref, h_ref, wp_ref, wg_ref, gn_ref, pn_ref, nn_ref, o_ref, pe_ref, hg_ref, gl_ref, hn_ref):
        hv = h_ref[...]
        pe_ref[...] = _dot(p_ref[...].astype(BF16), wp_ref[...], NN).astype(pe_ref.dtype)
        r = lax.rsqrt(jnp.mean(hv * hv, axis=-1, keepdims=True) + RMS_EPS)
        hg_ref[...] = (hv * r * gn_ref[...]).astype(hg_ref.dtype)
        gl_ref[...] = _dot(hg_ref[...], wg_ref[...], NN).astype(gl_ref.dtype)
        pe_ = pe_ref[...].astype(F32)
        rp = lax.rsqrt(jnp.mean(pe_ * pe_, axis=-1, keepdims=True) + RMS_EPS)
        out = hv + _sigmoid(gl_ref[...].astype(F32)) * (pe_ * rp * pn_ref[...])
        o_ref[...] = out
        ro = lax.rsqrt(jnp.mean(out * out, axis=-1, keepdims=True) + RMS_EPS)
        hn_ref[...] = (out * ro * nn_ref[...]).astype(hn_ref.dtype)

    row = pl.BlockSpec((tr, d), lambda i: (i, 0))
    vec = pl.BlockSpec((1, d), lambda i: (0, 0))
    sds = jax.ShapeDtypeStruct((s, d), BF16)
    return pl.pallas_call(
        body,
        grid=(s // tr,),
        in_specs=[pl.BlockSpec((tr, e), lambda i: (i, 0)), row, pl.BlockSpec((e, d), lambda i: (0, 0)),
                  pl.BlockSpec((d, d), lambda i: (0, 0)), vec, vec, vec],
        out_specs=[row, row, row, row, row],
        out_shape=[jax.ShapeDtypeStruct((s, d), F32), sds, sds, sds, sds],
        compiler_params=_params(("parallel",)),
        name="ple_fwd",
    )(p, h, w_proj, w_gate, gate_norm, ple_norm, next_norm)


def _ple_bwd(dh, p, pe, gl, hg, h, w_gate, gate_norm, ple_norm):
    s, d = dh.shape
    e = p.shape[1]
    tr = _pick(s, (512, 256, 128))

    def body(dh_ref, p_ref, pe_ref, gl_ref, hg_ref, h_ref, wg_ref, gn_ref, pn_ref,
             dx_ref, dwg_ref, dwp_ref, dpn_ref, dgn_ref):
        pe_ = pe_ref[...].astype(F32)
        dhv = dh_ref[...]
        r = lax.rsqrt(jnp.mean(pe_ * pe_, axis=-1, keepdims=True) + RMS_EPS)
        pn = pe_ * r
        gate = _sigmoid(gl_ref[...].astype(F32))
        dgl = (dhv * (pn * pn_ref[...]) * gate * (1.0 - gate)).astype(BF16)
        de = dhv * gate
        dxh = de * pn_ref[...]
        dpe = (r * (dxh - pn * jnp.mean(dxh * pn, axis=-1, keepdims=True))).astype(BF16)
        dhg = _dot(dgl, wg_ref[...], NT)
        hv = h_ref[...]
        rh = lax.rsqrt(jnp.mean(hv * hv, axis=-1, keepdims=True) + RMS_EPS)
        hn = hv * rh
        dhh = dhg * gn_ref[...]
        dx_ref[...] = dhv + rh * (dhh - hn * jnp.mean(dhh * hn, axis=-1, keepdims=True))
        parts = (_dot(hg_ref[...], dgl, TN), _dot(p_ref[...].astype(BF16), dpe, TN),
                 jnp.sum(de * pn, axis=0, keepdims=True), jnp.sum(dhg * hn, axis=0, keepdims=True))
        accs = (dwg_ref, dwp_ref, dpn_ref, dgn_ref)

        @pl.when(pl.program_id(0) == 0)
        def _():
            for acc, part in zip(accs, parts):
                acc[...] = part

        @pl.when(pl.program_id(0) > 0)
        def _():
            for acc, part in zip(accs, parts):
                acc[...] += part

    row = pl.BlockSpec((tr, d), lambda i: (i, 0))
    vec = pl.BlockSpec((1, d), lambda i: (0, 0))
    mat = pl.BlockSpec((d, d), lambda i: (0, 0))
    small = pl.BlockSpec((e, d), lambda i: (0, 0))
    return pl.pallas_call(
        body,
        grid=(s // tr,),
        in_specs=[row, pl.BlockSpec((tr, e), lambda i: (i, 0)), row, row, row, row, mat, vec, vec],
        out_specs=[row, mat, small, vec, vec],
        out_shape=[jax.ShapeDtypeStruct((s, d), F32), jax.ShapeDtypeStruct((d, d), F32),
                   jax.ShapeDtypeStruct((e, d), F32), jax.ShapeDtypeStruct((1, d), F32),
                   jax.ShapeDtypeStruct((1, d), F32)],
        compiler_params=_params(("arbitrary",)),
        name="ple_bwd",
    )(dh, p, pe, gl, hg, h, w_gate, gate_norm, ple_norm)


def _loss_head(h, w, target):
    s, d = h.shape
    tr = _pick(s, (512, 256, 128))

    def body(h_ref, w_ref, t_ref, l_ref, dh_ref, dw_ref):
        hv = h_ref[...]
        r = lax.rsqrt(jnp.mean(hv * hv, axis=-1, keepdims=True) + RMS_EPS)
        hn = hv * r
        diff = hn * w_ref[...] - t_ref[...]
        lpart = jnp.zeros((8, LANES), F32) + (0.5 / d) * jnp.sum(jnp.sum(diff * diff, axis=1, keepdims=True), axis=0, keepdims=True)
        dy = diff * (1.0 / d)
        dxh = dy * w_ref[...]
        dh_ref[...] = r * (dxh - hn * jnp.mean(dxh * hn, axis=-1, keepdims=True))
        part = jnp.sum(dy * hn, axis=0, keepdims=True)

        @pl.when(pl.program_id(0) == 0)
        def _():
            l_ref[...] = lpart
            dw_ref[...] = part

        @pl.when(pl.program_id(0) > 0)
        def _():
            l_ref[...] += lpart
            dw_ref[...] += part

    row = pl.BlockSpec((tr, d), lambda i: (i, 0))
    vec = pl.BlockSpec((1, d), lambda i: (0, 0))
    return pl.pallas_call(
        body,
        grid=(s // tr,),
        in_specs=[row, vec, row],
        out_specs=[pl.BlockSpec((8, LANES), lambda i: (0, 0)), row, vec],
        out_shape=[jax.ShapeDtypeStruct((8, LANES), F32), jax.ShapeDtypeStruct((s, d), F32),
                   jax.ShapeDtypeStruct((1, d), F32)],
        compiler_params=_params(("arbitrary",)),
        name="loss_head",
    )(h, w, target)


PER_LAYER = ("norm_mix", "norm_ffn", "ffn_w_gu", "ffn_w_down", "ple_w_proj", "ple_norm", "ple_gate_norm", "ple_w_gate")


def _pad_lanes(v):
    return jnp.pad(v.astype(F32), (0, LANES - v.shape[0]))[None, :]


def _kernel_layouts(full):
    w = {}
    for k in ("norm_mix", "norm_ffn", "ple_norm", "ple_gate_norm", "ssd_conv_b", "ssd_norm_w", "gmlp_b_in", "gmlp_ln_w",
              "gmlp_ln_b", "gmlp_w_s"):
        w[k] = [full[k][i].astype(F32) for i in range(full[k].shape[0])]
    w["final_norm"] = full["final_norm"].astype(F32)
    n_ssd = full["ssd_w_out"].shape[0]
    if "ssd_w_in" in full:
        w["ssd_w_zx"] = [full["ssd_w_in"][j][:, :SSD_ZX].astype(BF16) for j in range(n_ssd)]
        w["ssd_w_dt"] = [jnp.pad(full["ssd_w_in"][j][:, SSD_ZX:].astype(BF16), ((0, 0), (0, LANES - SSD_HEADS)))
                         for j in range(n_ssd)]
        w["ffn_w_gu"] = [jnp.concatenate([full["ffn_w_gate"][i], full["ffn_w_up"][i]], axis=1).astype(BF16)
                         for i in range(DEPTH)]
    else:
        for k in ("ssd_w_zx", "ssd_w_dt", "ffn_w_gu"):
            w[k] = full[k]
    w["ssd_conv_w"] = [jnp.pad(full["ssd_conv_w"][j].astype(F32), ((0, 8 - CONV_K), (0, 0))) for j in range(n_ssd)]
    for k in ("ssd_dt_bias", "ssd_a_log", "ssd_d"):
        w[k] = [_pad_lanes(full[k][j]) for j in range(n_ssd)]
    w["ssd_w_out"] = [full["ssd_w_out"][j].astype(BF16) for j in range(n_ssd)]
    n_g = full["gmlp_w_in"].shape[0]
    w["gmlp_w_in"] = [full["gmlp_w_in"][j].astype(BF16) for j in range(n_g)]
    w["gmlp_w_out"] = [full["gmlp_w_out"][j].astype(BF16) for j in range(n_g)]
    w["gmlp_b_st"] = [jnp.pad(full["gmlp_b_s"][j].astype(F32).T, ((0, 0), (0, LANES - GMLP_GROUPS))) for j in range(n_g)]
    w["ffn_w_down"] =[full["ffn_w_down"][i].astype(BF16) for i in range(DEPTH)]
    w["ple_w_proj"] = [full["ple_w_proj"][i].astype(BF16) for i in range(DEPTH)]
    w["ple_w_gate"] = [full["ple_w_gate"][i].astype(BF16) for i in range(DEPTH)]
    return w


MATRICES = ("ssd_w_out", "gmlp_w_in", "gmlp_w_out", "ffn_w_down", "ple_w_proj", "ple_w_gate")


def _reference_layouts(g, wide=True):
    out = {}
    for k in ("norm_mix", "norm_ffn", "ple_norm", "ple_gate_norm", "ssd_conv_b", "ssd_norm_w", "gmlp_b_in", "gmlp_ln_w",
              "gmlp_ln_b", "gmlp_w_s", "ssd_conv_w", "ssd_dt_bias", "ssd_a_log", "ssd_d") + (MATRICES if wide else ()):
        out[k] = jnp.stack(g[k])
    out["final_norm"] = g["final_norm"]
    out["gmlp_b_s"] = jnp.stack([b[:, :GMLP_GROUPS].T for b in g["gmlp_b_st"]])
    if wide:
        out["ssd_w_in"] = jnp.stack([jnp.concatenate([zx, dt[:, :SSD_HEADS]], axis=1)
                                     for zx, dt in zip(g["ssd_w_zx"], g["ssd_w_dt"])])
        out["ffn_w_gate"] = jnp.stack([gu[:, :FFN_DIM] for gu in g["ffn_w_gu"]])
        out["ffn_w_up"] = jnp.stack([gu[:, FFN_DIM:] for gu in g["ffn_w_gu"]])
    return out


RELAYOUT_ROWS = 128
SSD_SHARD = SSD_IN_DIM // N_DEV
FFN_SHARD = FFN_DIM // N_DEV


def _to_bf16(x):
    nl, rows, n = x.shape

    def body(x_ref, o_ref):
        o_ref[...] = x_ref[...].astype(o_ref.dtype)

    blk = pl.BlockSpec((1, rows, n), lambda i: (i, 0, 0))
    return pl.pallas_call(
        body, grid=(nl,), in_specs=[blk], out_specs=blk, out_shape=jax.ShapeDtypeStruct(x.shape, BF16),
        compiler_params=_params(("parallel",)), name="to_bf16",
    )(x)


def _cat_ssd_in(gathered):
    _, nl, rows, n = gathered.shape
    tr = RELAYOUT_ROWS

    def body(g_ref, *o_refs):
        for j in range(nl):
            full = jnp.concatenate([g_ref[d, j] for d in range(N_DEV)], axis=1)
            o_refs[2 * j][...] = full[:, :SSD_ZX]
            o_refs[2 * j + 1][...] = jnp.concatenate(
                [full[:, SSD_ZX:], jnp.zeros((tr, LANES - SSD_HEADS), full.dtype)], axis=1)

    outs = pl.pallas_call(
        body, grid=(rows // tr,),
        in_specs=[pl.BlockSpec((N_DEV, nl, tr, n), lambda i: (0, 0, i, 0))],
        out_specs=[pl.BlockSpec((tr, SSD_ZX), lambda i: (i, 0)), pl.BlockSpec((tr, LANES), lambda i: (i, 0))] * nl,
        out_shape=[jax.ShapeDtypeStruct((rows, SSD_ZX), BF16), jax.ShapeDtypeStruct((rows, LANES), BF16)] * nl,
        compiler_params=_params(("parallel",)), name="cat_ssd_in",
    )(gathered)
    return [outs[2 * j] for j in range(nl)], [outs[2 * j + 1] for j in range(nl)]


def _split_ssd_in(dzx_list, ddt_list):
    nl = len(dzx_list)
    rows = dzx_list[0].shape[0]
    tr = RELAYOUT_ROWS

    def body(*refs):
        o_ref = refs[2 * nl]
        for j in range(nl):
            full = jnp.concatenate([refs[2 * j][...], refs[2 * j + 1][:, 0:SSD_HEADS]], axis=1)
            for d in range(N_DEV):
                o_ref[d, j] = full[:, d * SSD_SHARD:(d + 1) * SSD_SHARD].astype(o_ref.dtype)

    ins = []
    for j in range(nl):
        ins += [dzx_list[j], ddt_list[j]]
    return pl.pallas_call(
        body, grid=(rows // tr,),
        in_specs=[pl.BlockSpec((tr, SSD_ZX), lambda i: (i, 0)), pl.BlockSpec((tr, LANES), lambda i: (i, 0))] * nl,
        out_specs=pl.BlockSpec((N_DEV, nl, tr, SSD_SHARD), lambda i: (0, 0, i, 0)),
        out_shape=jax.ShapeDtypeStruct((N_DEV, nl, rows, SSD_SHARD), BF16),
        compiler_params=_params(("parallel",)), name="split_ssd_in",
    )(*ins)


def _cat_ffn(g_gate, g_up):
    _, nl, rows, n = g_gate.shape
    tr = RELAYOUT_ROWS

    def body(gg_ref, gu_ref, *o_refs):
        for i in range(nl):
            o_refs[i][...] = jnp.concatenate([gg_ref[d, i] for d in range(N_DEV)] + [gu_ref[d, i] for d in range(N_DEV)],
                                             axis=1)

    blk = pl.BlockSpec((N_DEV, nl, tr, n), lambda i: (0, 0, i, 0))
    outs = pl.pallas_call(
        body, grid=(rows // tr,), in_specs=[blk, blk],
        out_specs=[pl.BlockSpec((tr, 2 * FFN_DIM), lambda i: (i, 0))] * nl,
        out_shape=[jax.ShapeDtypeStruct((rows, 2 * FFN_DIM), BF16)] * nl,
        compiler_params=_params(("parallel",)), name="cat_ffn",
    )(g_gate, g_up)
    return list(outs)


def _split_ffn(dgu_list):
    nl = len(dgu_list)
    rows = dgu_list[0].shape[0]
    tr = RELAYOUT_ROWS

    def body(*refs):
        og_ref, ou_ref = refs[nl], refs[nl + 1]
        for i in range(nl):
            full = refs[i][...]
            for d in range(N_DEV):
                og_ref[d, i] = full[:, d * FFN_SHARD:(d + 1) * FFN_SHARD].astype(og_ref.dtype)
                ou_ref[d, i] = full[:, FFN_DIM + d * FFN_SHARD:FFN_DIM + (d + 1) * FFN_SHARD].astype(ou_ref.dtype)

    blk = pl.BlockSpec((N_DEV, nl, tr, FFN_SHARD), lambda i: (0, 0, i, 0))
    sds = jax.ShapeDtypeStruct((N_DEV, nl, rows, FFN_SHARD), BF16)
    return pl.pallas_call(
        body, grid=(rows // tr,),
        in_specs=[pl.BlockSpec((tr, 2 * FFN_DIM), lambda i: (i, 0))] * nl,
        out_specs=[blk, blk], out_shape=[sds, sds],
        compiler_params=_params(("parallel",)), name="split_ffn",
    )(*dgu_list)


def _local_step(x, p, target, w):
    saved = []
    h = x
    hn = _rms_fwd(h, w["norm_mix"][0][None, :])
    for i in range(DEPTH):
        j = i // 2
        sv = {"h0": h}
        sv["hn"] = hn
        if i % 2 == 0:
            zx = _mm(hn, w["ssd_w_zx"][j], "nn", BF16)
            dtp = _mm(hn, w["ssd_w_dt"][j], "nn", F32)
            xbc = _ssd_conv_fwd(zx, w["ssd_conv_w"][j], w["ssd_conv_b"][j][None, :])
            y, prev = _ssd_scan_fwd(xbc, dtp, w["ssd_dt_bias"][j], w["ssd_a_log"][j], w["ssd_d"][j])
            gn = _ssd_gate_fwd(y, zx, w["ssd_norm_w"][j][None, :])
            h = _mm(gn, w["ssd_w_out"][j], "nn", F32, add=h)
            sv.update(zx=zx, dtp=dtp, xbc=xbc, y=y, prev=prev, gn=gn)
        else:
            pre = _mm(hn, w["gmlp_w_in"][j], "nn", BF16)
            u, v = _gmlp_act_fwd(pre, w["gmlp_b_in"][j][None, :], w["gmlp_ln_w"][j][None, :], w["gmlp_ln_b"][j][None, :])
            gated = _gmlp_mix_fwd(u, v, w["gmlp_w_s"][j], w["gmlp_b_st"][j])
            h = _mm(gated, w["gmlp_w_out"][j], "nn", F32, add=h)
            sv.update(pre=pre, u=u, v=v, gated=gated)
        sv["h1"] = h
        un = _rms_fwd(h, w["norm_ffn"][i][None, :])
        gate, up, hid = _ffn_up(un, w["ffn_w_gu"][i])
        h = _mm(hid, w["ffn_w_down"][i], "nn", F32, add=h)
        sv.update(un=un, gate=gate, up=up, hid=hid, h2=h)
        next_norm = w["norm_mix"][i + 1] if i + 1 < DEPTH else w["final_norm"]
        h, pe, hg, gl, hn = _ple_fwd(p[i], h, w["ple_w_proj"][i], w["ple_w_gate"][i], w["ple_gate_norm"][i][None, :],
                                     w["ple_norm"][i][None, :], next_norm[None, :])
        sv.update(pe=pe, hg=hg, gl=gl)
        saved.append(sv)

    lpart, dh, d_final = _loss_head(h, w["final_norm"][None, :], target)
    g = {k: [None] * (DEPTH if k in PER_LAYER else DEPTH // 2) for k in w if k != "final_norm"}
    g["final_norm"] = d_final[0]

    for i in reversed(range(DEPTH)):
        j = i // 2
        sv = saved[i]
        dh, g["ple_w_gate"][i], g["ple_w_proj"][i], d_ple_norm, d_gate_norm = _ple_bwd(
            dh, p[i], sv["pe"], sv["gl"], sv["hg"], sv["h2"], w["ple_w_gate"][i], w["ple_gate_norm"][i][None, :],
            w["ple_norm"][i][None, :])
        g["ple_norm"][i] = d_ple_norm[0]
        g["ple_gate_norm"][i] = d_gate_norm[0]
        g["ffn_w_down"][i] = _mm(sv["hid"], dh, "tn", F32)
        dgu = _ffn_down_bwd(dh, w["ffn_w_down"][i], sv["gate"], sv["up"])
        g["ffn_w_gu"][i] = _mm(sv["un"], dgu, "tn", F32)
        dun = _mm(dgu, w["ffn_w_gu"][i], "nt", BF16)
        dh, d_norm_ffn = _rms_bwd(dun, sv["h1"], w["norm_ffn"][i][None, :], dh)
        g["norm_ffn"][i] = d_norm_ffn[0]
        if i % 2 == 0:
            dgn = _mm(dh, w["ssd_w_out"][j], "nt", BF16)
            g["ssd_w_out"][j] = _mm(sv["gn"], dh, "tn", F32)
            dy, dzx, d_norm_w = _ssd_gate_bwd(dgn, sv["y"], sv["zx"], w["ssd_norm_w"][j][None, :])
            g["ssd_norm_w"][j] = d_norm_w[0]
            dxbc, ddtp, d_bias, d_alog, d_d = _ssd_scan_bwd(sv["xbc"], sv["dtp"], sv["prev"], dy, w["ssd_dt_bias"][j],
                                                            w["ssd_a_log"][j], w["ssd_d"][j])
            g["ssd_dt_bias"][j] = d_bias[0, :SSD_HEADS]
            g["ssd_a_log"][j] = d_alog[0, :SSD_HEADS]
            g["ssd_d"][j] = d_d[0, :SSD_HEADS]
            dzx, d_conv_w, d_conv_b = _ssd_conv_bwd(sv["zx"], dxbc, w["ssd_conv_w"][j], w["ssd_conv_b"][j][None, :], dzx)
            g["ssd_conv_w"][j] = d_conv_w[:CONV_K]
            g["ssd_conv_b"][j] = d_conv_b[0]
            g["ssd_w_zx"][j] = _mm(sv["hn"], dzx, "tn", F32)
            g["ssd_w_dt"][j] = _mm(sv["hn"], ddtp, "tn", F32)
            dhn = _mm(ddtp, w["ssd_w_dt"][j], "nt", F32)
            dhn = _mm(dzx, w["ssd_w_zx"][j], "nt", BF16, add=dhn)
        else:
            dgated = _mm(dh, w["gmlp_w_out"][j], "nt", BF16)
            g["gmlp_w_out"][j] = _mm(sv["gated"], dh, "tn", F32)
            du, dv, d_ws, d_bst = _gmlp_mix_bwd(dgated, sv["u"], sv["v"], w["gmlp_w_s"][j], w["gmlp_b_st"][j])
            g["gmlp_w_s"][j] = d_ws
            g["gmlp_b_st"][j] = d_bst
            dpre, d_bin, d_lnw, d_lnb = _gmlp_act_bwd(sv["pre"], w["gmlp_b_in"][j][None, :], w["gmlp_ln_w"][j][None, :],
                                                     du, dv)
            g["gmlp_b_in"][j] = d_bin[0]
            g["gmlp_ln_w"][j] = d_lnw[0]
            g["gmlp_ln_b"][j] = d_lnb[0]
            g["gmlp_w_in"][j] = _mm(sv["hn"], dpre, "tn", F32)
            dhn = _mm(dpre, w["gmlp_w_in"][j], "nt", BF16)
        dh, d_norm_mix = _rms_bwd(dhn, sv["h0"], w["norm_mix"][i][None, :], dh)
        g["norm_mix"][i] = d_norm_mix[0]
    return lpart[0, 0], dh, g


PACK_COLS = 1024
ANY = pl.BlockSpec(memory_space=pl.ANY)


def _mesh_pos():
    return lax.axis_index("x"), lax.axis_index("y"), lax.axis_index("c")


def _all_gather(xs_list, name):
    n = len(xs_list)

    def body(*refs):
        x_refs, out_refs = refs[:n], refs[n:2 * n]
        send_sems, recv_sems, local_sems = refs[2 * n:]
        x, y, c = _mesh_pos()
        me, sibling = (x, y, c), (x, y, 1 - c)
        chips = [(1 - x, y), (x, 1 - y), (1 - x, 1 - y)]

        def copy(a, k, block, to, from_input=False):
            px, py, pc = block
            dst = out_refs[a].at[4 * px + 2 * py + pc]
            return pltpu.make_async_remote_copy(
                src_ref=x_refs[a] if from_input else dst, dst_ref=dst,
                send_sem=send_sems.at[7 * a + k], recv_sem=recv_sems.at[7 * a + k], device_id=to,
                device_id_type=MESH_ID)

        mine = [pltpu.make_async_copy(x_refs[a], out_refs[a].at[4 * x + 2 * y + c], local_sems.at[a]) for a in range(n)]
        for cp in mine:
            cp.start()
        first = []
        for a in range(n):
            first += [copy(a, 1 + j, me, (*chip, c), from_input=True) for j, chip in enumerate(chips)]
            first.append(copy(a, 0, me, sibling, from_input=True))
        for cp in first:
            cp.start()
        passed = []
        for a in range(n):
            for j, chip in enumerate(chips):
                copy(a, 1 + j, (*chip, c), me).wait_recv()
                fwd = copy(a, 4 + j, (*chip, c), sibling)
                fwd.start()
                passed.append(fwd)
        for a in range(n):
            copy(a, 0, sibling, me).wait_recv()
            for j, chip in enumerate(chips):
                copy(a, 4 + j, (*chip, 1 - c), me).wait_recv()
        for cp in first + passed:
            cp.wait_send()
        for cp in mine:
            cp.wait()

    outs = pl.pallas_call(
        body,
        out_shape=[jax.ShapeDtypeStruct((N_DEV,) + t.shape, t.dtype) for t in xs_list],
        in_specs=[ANY] * n,
        out_specs=[ANY] * n,
        scratch_shapes=[pltpu.SemaphoreType.DMA((7 * n,)), pltpu.SemaphoreType.DMA((7 * n,)),
                        pltpu.SemaphoreType.DMA((n,))],
        name=name,
    )(*xs_list)
    return list(outs)


def _exchange_sibling(send_list):
    n = len(send_list)

    def body(*refs):
        s_refs, land_refs = refs[:n], refs[n:2 * n]
        send_sems, recv_sems = refs[2 * n:]
        x, y, c = _mesh_pos()
        cps = [pltpu.make_async_remote_copy(src_ref=s_refs[a], dst_ref=land_refs[a], send_sem=send_sems.at[a],
                                            recv_sem=recv_sems.at[a], device_id=(x, y, 1 - c), device_id_type=MESH_ID)
               for a in range(n)]
        for cp in cps:
            cp.start()
        for cp in cps:
            cp.wait()

    outs = pl.pallas_call(
        body,
        out_shape=[jax.ShapeDtypeStruct(t.shape, t.dtype) for t in send_list],
        in_specs=[ANY] * n,
        out_specs=[ANY] * n,
        scratch_shapes=[pltpu.SemaphoreType.DMA((n,)), pltpu.SemaphoreType.DMA((n,))],
        name="rs_exchange_sibling",
    )(*send_list)
    return list(outs)


def _exchange_chips(partial_list):
    n = len(partial_list)

    def body(*refs):
        p_refs, land_refs = refs[:n], refs[n:2 * n]
        send_sems, recv_sems = refs[2 * n:]
        x, y, c = _mesh_pos()
        chips = [(1 - x, y), (x, 1 - y), (1 - x, 1 - y)]
        cps = [pltpu.make_async_remote_copy(src_ref=p_refs[a].at[2 * cx + cy], dst_ref=land_refs[a].at[j],
                                            send_sem=send_sems.at[3 * a + j], recv_sem=recv_sems.at[3 * a + j],
                                            device_id=(cx, cy, c), device_id_type=MESH_ID)
               for a in range(n) for j, (cx, cy) in enumerate(chips)]
        for cp in cps:
            cp.start()
        for cp in cps:
            cp.wait()

    outs = pl.pallas_call(
        body,
        out_shape=[jax.ShapeDtypeStruct((3,) + t.shape[1:], t.dtype) for t in partial_list],
        in_specs=[ANY] * n,
        out_specs=[ANY] * n,
        scratch_shapes=[pltpu.SemaphoreType.DMA((3 * n,)), pltpu.SemaphoreType.DMA((3 * n,))],
        name="rs_exchange_chips",
    )(*partial_list)
    return list(outs)


def _sum_pairs(a, b):
    shape = a.shape
    a = a.reshape(shape[0], -1, shape[-1])
    b = b.reshape(a.shape)
    n, r, cdim = a.shape
    tr = _pick(r, (1024, 512, 256, 128, 64))

    def body(a_ref, b_ref, o_ref):
        o_ref[...] = (a_ref[...].astype(F32) + b_ref[...].astype(F32)).astype(o_ref.dtype)

    blk = pl.BlockSpec((1, tr, cdim), lambda i, j: (i, j, 0))
    return pl.pallas_call(
        body, grid=(n, r // tr), in_specs=[blk, blk], out_specs=blk,
        out_shape=jax.ShapeDtypeStruct(a.shape, a.dtype),
        compiler_params=_params(("parallel", "parallel")), name="rs_sum_pairs",
    )(a, b).reshape(shape)


def _sum_final(own, land):
    shape = own.shape
    own = own.reshape(-1, shape[-1])
    land = land.reshape((3,) + own.shape)
    r, cdim = own.shape
    tr = _pick(r, (1024, 512, 256, 128, 64))

    def body(o_ref, l_ref, out_ref):
        acc = o_ref[...].astype(F32)
        for j in range(3):
            acc = acc + l_ref[j].astype(F32)
        out_ref[...] = acc

    return pl.pallas_call(
        body, grid=(r // tr,),
        in_specs=[pl.BlockSpec((tr, cdim), lambda i: (i, 0)), pl.BlockSpec((3, tr, cdim), lambda i: (0, i, 0))],
        out_specs=pl.BlockSpec((tr, cdim), lambda i: (i, 0)),
        out_shape=jax.ShapeDtypeStruct((r, cdim), F32),
        compiler_params=_params(("parallel",)), name="rs_sum_final",
    )(own, land).reshape(shape)


def _sum_devices(gathered):
    n, r, cdim = gathered.shape
    tr = _pick(r, (64, 32, 16, 8))

    def body(g_ref, out_ref):
        acc = g_ref[0].astype(F32)
        for q in range(1, n):
            acc = acc + g_ref[q].astype(F32)
        out_ref[...] = acc

    return pl.pallas_call(
        body, grid=(r // tr,),
        in_specs=[pl.BlockSpec((n, tr, cdim), lambda i: (0, i, 0))],
        out_specs=pl.BlockSpec((tr, cdim), lambda i: (i, 0)),
        out_shape=jax.ShapeDtypeStruct((r, cdim), F32),
        compiler_params=_params(("parallel",)), name="sum_devices",
    )(gathered)


def _adamw(w, g, m, v):
    shape = w.shape
    cols = shape[-1]
    rows = w.size // cols
    tr = _pick(rows, (512, 256, 128, 64, 32, 16, 8))
    c1 = 1.0 - ADAM_B1 ** ADAM_STEP
    c2 = 1.0 - ADAM_B2 ** ADAM_STEP

    def body(w_ref, g_ref, m_ref, v_ref, d_ref, nm_ref, nv_ref):
        gv = g_ref[...]
        m2 = ADAM_B1 * m_ref[...] + (1.0 - ADAM_B1) * gv
        v2 = ADAM_B2 * v_ref[...] + (1.0 - ADAM_B2) * (gv * gv)
        d_ref[...] = -ADAM_LR * ((m2 / c1) / (jnp.sqrt(v2 / c2) + ADAM_EPS) + ADAM_WD * w_ref[...])
        nm_ref[...] = m2
        nv_ref[...] = v2

    blk = pl.BlockSpec((tr, cols), lambda i: (i, 0))
    sds = jax.ShapeDtypeStruct((rows, cols), F32)
    outs = pl.pallas_call(
        body, grid=(rows // tr,), in_specs=[blk] * 4, out_specs=[blk] * 3, out_shape=[sds] * 3,
        compiler_params=_params(("parallel",)), name=f"adamw_{rows}x{cols}",
    )(*(t.reshape(rows, cols) for t in (w, g, m, v)))
    return tuple(o.reshape(shape) for o in outs)


WEIGHTS = ("norm_mix", "norm_ffn", "ssd_w_in", "ssd_conv_w", "ssd_conv_b", "ssd_dt_bias", "ssd_a_log", "ssd_d",
           "ssd_norm_w", "ssd_w_out", "gmlp_w_in", "gmlp_b_in", "gmlp_ln_w", "gmlp_ln_b", "gmlp_w_s", "gmlp_b_s",
           "gmlp_w_out", "ffn_w_gate", "ffn_w_up", "ffn_w_down", "ple_w_proj", "ple_norm", "ple_gate_norm",
           "ple_w_gate", "final_norm")
ARG_NAMES = ("x", "p") + WEIGHTS + ("loss_target",) + tuple("m_" + n for n in WEIGHTS) + tuple("v_" + n for n in WEIGHTS)
SHARD_AXIS = {"ssd_w_in": 2, "ssd_conv_w": 2, "ssd_w_out": 1, "gmlp_w_in": 2, "gmlp_b_in": 1, "gmlp_ln_w": 1,
              "gmlp_ln_b": 1, "gmlp_w_out": 1, "ffn_w_gate": 2, "ffn_w_up": 2, "ffn_w_down": 1, "ple_w_proj": 2,
              "ple_w_gate": 1}
GATHER_BF16 = ("ssd_w_in", "ssd_w_out", "gmlp_w_in", "gmlp_w_out", "ffn_w_gate", "ffn_w_up", "ffn_w_down",
               "ple_w_proj", "ple_w_gate")
GATHER_F32 = ("ssd_conv_w", "gmlp_b_in", "gmlp_ln_w", "gmlp_ln_b")
SHARDED = GATHER_BF16 + GATHER_F32
WIDE = ("ssd_w_in", "ffn_w_gate", "ffn_w_up")
REPLICATED = tuple(n for n in WEIGHTS if n not in SHARD_AXIS)


def _pack(arrs, dtype, row_mult, lead=0):
    flat = jnp.concatenate([t.reshape(t.shape[:lead] + (-1,)).astype(dtype) for t in arrs], axis=lead)
    n = flat.shape[-1]
    unit = row_mult * PACK_COLS
    total = -(-n // unit) * unit
    flat = jnp.pad(flat, [(0, 0)] * lead + [(0, total - n)])
    return flat.reshape(flat.shape[:lead] + (total // PACK_COLS, PACK_COLS))


def _unpack(buf, names, shapes, lead=0):
    flat = buf.reshape(buf.shape[:lead] + (-1,))
    out, off = {}, 0
    for n in names:
        size = math.prod(shapes[n])
        out[n] = lax.slice_in_dim(flat, off, off + size, axis=lead).reshape(buf.shape[:lead] + tuple(shapes[n]))
        off += size
    return out


ROW_PACKED = ((1024, ("ssd_w_out", "gmlp_w_out", "ffn_w_down", "ple_w_gate")), (512, ("gmlp_w_in",)),
              (128, ("ple_w_proj",)))
ROW_PACK_MULT = 1024


def _pack_rows(arrs, width, lead=0):
    parts = [t.reshape(t.shape[:lead] + (-1, width)).astype(BF16) for t in arrs]
    rows = sum(t.shape[lead] for t in parts)
    pad = -rows % ROW_PACK_MULT
    if pad:
        parts.append(jnp.zeros(parts[0].shape[:lead] + (pad, width), BF16))
    return jnp.concatenate(parts, axis=lead)


def _unpack_rows(buf, names, shapes, lead=0):
    width = buf.shape[-1]
    out, off = {}, 0
    for n in names:
        rows = math.prod(shapes[n]) // width
        out[n] = lax.slice_in_dim(buf, off, off + rows, axis=lead).reshape(buf.shape[:lead] + tuple(shapes[n]))
        off += rows
    return out


def _merge_shards(seg, ax):
    t = jnp.moveaxis(seg, 0, ax)
    return t.reshape(t.shape[:ax] + (t.shape[ax] * t.shape[ax + 1],) + t.shape[ax + 2:])


def _split_for_cores(gfull, ax, c):
    shp = gfull.shape
    t = gfull.reshape(shp[:ax] + (2, 2, 2, shp[ax] // N_DEV) + shp[ax + 1:])

    def take(core):
        u = lax.dynamic_index_in_dim(t, core, axis=ax + 2, keepdims=False)
        u = jnp.moveaxis(u, (ax, ax + 1), (0, 1))
        return u.reshape((4,) + u.shape[2:])

    return take(c), take(1 - c)


def kernel(x, p, norm_mix, norm_ffn, ssd_w_in, ssd_conv_w, ssd_conv_b, ssd_dt_bias, ssd_a_log, ssd_d,
           ssd_norm_w, ssd_w_out, gmlp_w_in, gmlp_b_in, gmlp_ln_w, gmlp_ln_b, gmlp_w_s, gmlp_b_s,
           gmlp_w_out, ffn_w_gate, ffn_w_up, ffn_w_down, ple_w_proj, ple_norm, ple_gate_norm, ple_w_gate,
           final_norm, loss_target, m_norm_mix, m_norm_ffn, m_ssd_w_in, m_ssd_conv_w, m_ssd_conv_b,
           m_ssd_dt_bias, m_ssd_a_log, m_ssd_d, m_ssd_norm_w, m_ssd_w_out, m_gmlp_w_in, m_gmlp_b_in,
           m_gmlp_ln_w, m_gmlp_ln_b, m_gmlp_w_s, m_gmlp_b_s, m_gmlp_w_out, m_ffn_w_gate, m_ffn_w_up,
           m_ffn_w_down, m_ple_w_proj, m_ple_norm, m_ple_gate_norm, m_ple_w_gate, m_final_norm, v_norm_mix,
           v_norm_ffn, v_ssd_w_in, v_ssd_conv_w, v_ssd_conv_b, v_ssd_dt_bias, v_ssd_a_log, v_ssd_d,
           v_ssd_norm_w, v_ssd_w_out, v_gmlp_w_in, v_gmlp_b_in, v_gmlp_ln_w, v_gmlp_ln_b, v_gmlp_w_s,
           v_gmlp_b_s, v_gmlp_w_out, v_ffn_w_gate, v_ffn_w_up, v_ffn_w_down, v_ple_w_proj, v_ple_norm,
           v_ple_gate_norm, v_ple_w_gate, v_final_norm):
    given = locals()
    a = {n: given[n] for n in ARG_NAMES}
    mx, my, c = _mesh_pos()
    xs = a["x"][0]
    ps = a["p"][:, 0]
    target = a["loss_target"][0]
    shard_shapes = {n: a[n].shape for n in WEIGHTS}

    full = {n: a[n] for n in REPLICATED}
    row_packs = [_pack_rows([a[n] for n in names], wd) for wd, names in ROW_PACKED]
    got = _all_gather(row_packs + [_pack([a[n] for n in GATHER_F32], F32, 8)] + [_to_bf16(a[n]) for n in WIDE],
                      "ag_weights")
    for (wd, names), buf in zip(ROW_PACKED, got):
        for n, seg in _unpack_rows(buf, names, shard_shapes, lead=1).items():
            full[n] = _merge_shards(seg, SHARD_AXIS[n])
    k0 = len(ROW_PACKED)
    for n, seg in _unpack(got[k0], GATHER_F32, shard_shapes, lead=1).items():
        full[n] = _merge_shards(seg, SHARD_AXIS[n])
    full["ssd_w_zx"], full["ssd_w_dt"] = _cat_ssd_in(got[k0 + 1])
    full["ffn_w_gu"] = _cat_ffn(got[k0 + 2], got[k0 + 3])

    lpart, dx, g = _local_step(xs, ps, target, _kernel_layouts(full))
    gfull = _reference_layouts(g, wide=False)
    loss = lax.psum(lpart, ("x", "y", "c"))

    def by_core(t):
        u = t.reshape((4, 2) + t.shape[1:])
        return (lax.dynamic_index_in_dim(u, c, axis=1, keepdims=False),
                lax.dynamic_index_in_dim(u, 1 - c, axis=1, keepdims=False))

    def layer_halves(gl, ax):
        if ax == 0:
            t = gl.reshape(4, 2, -1, gl.shape[-1])
            return tuple(lax.dynamic_index_in_dim(t, cc, axis=1, keepdims=False) for cc in (c, 1 - c))
        t = gl.reshape(gl.shape[0], 4, 2, -1)
        return tuple(jnp.moveaxis(lax.dynamic_index_in_dim(t, cc, axis=2, keepdims=False), 1, 0) for cc in (c, 1 - c))

    pairs = []
    for wd, names in ROW_PACKED:
        hs = [layer_halves(gl, SHARD_AXIS[n] - 1) for n in names for gl in g[n]]
        pairs.append(tuple(_pack_rows([h[i] for h in hs], wd, lead=1) for i in (0, 1)))
    halves = [_split_for_cores(gfull[n], SHARD_AXIS[n], c) for n in GATHER_F32]
    pairs.append((_pack([h[0] for h in halves], BF16, 16, lead=1), _pack([h[1] for h in halves], BF16, 16, lead=1)))
    pairs += [by_core(t) for t in (_split_ssd_in(g["ssd_w_zx"], g["ssd_w_dt"]),) + tuple(_split_ffn(g["ffn_w_gu"]))]
    landed = _exchange_sibling([s for _, s in pairs])
    partials = [_sum_pairs(k, l) for (k, _), l in zip(pairs, landed)]
    landed = _exchange_chips(partials)
    sums = [_sum_final(lax.dynamic_index_in_dim(t, 2 * mx + my, axis=0, keepdims=False), l)
            for t, l in zip(partials, landed)]
    gshard = {}
    for (wd, names), buf in zip(ROW_PACKED, sums):
        gshard.update(_unpack_rows(buf, names, shard_shapes))
    gshard.update(_unpack(sums[k0], GATHER_F32, shard_shapes))
    gshard.update(zip(WIDE, sums[k0 + 1:]))
    rep = _all_gather([_pack([gfull[n] for n in REPLICATED], BF16, 64)], "ag_replicated_grads")[0]
    grep = _unpack(_sum_devices(rep), REPLICATED, shard_shapes)
    grads = {**gshard, **grep}

    upd = {n: _adamw(a[n], grads[n], a["m_" + n], a["v_" + n]) for n in WEIGHTS}
    return (loss, dx[None], *[grads[n] for n in WEIGHTS], *[upd[n][0] for n in WEIGHTS],
            *[upd[n][1] for n in WEIGHTS], *[upd[n][2] for n in WEIGHTS])
```

```python
import math

import jax
import jax.numpy as jnp
from jax import lax
from jax.experimental import pallas as pl
from jax.experimental.pallas import tpu as pltpu

F32 = jnp.float32
BF16 = jnp.bfloat16

N_DEV = 8
D_MODEL = 1024
DEPTH = 4
SSD_INNER = 2048
SSD_HEADS = 32
SSD_HEADDIM = 64
SSD_GROUPS = 8
SSD_STATE = 128
SSD_GROUP_W = SSD_INNER // SSD_GROUPS
SSD_CONV_DIM = SSD_INNER + 2 * SSD_GROUPS * SSD_STATE
SSD_IN_DIM = 2 * SSD_INNER + SSD_CONV_DIM - SSD_INNER + SSD_HEADS
SSD_ZX = SSD_INNER + SSD_CONV_DIM
CONV_K = 4
CHUNK = 128
GMLP_INNER = 2048
GMLP_GROUPS = 16
FFN_DIM = 2816
PLE_DIM = 256
RMS_EPS = 1e-6
LN_EPS = 1e-5
LANES = 128
VMEM_LIMIT = 56 * 1024 * 1024

ADAM_LR = 0.001
ADAM_B1 = 0.9
ADAM_B2 = 0.999
ADAM_EPS = 1e-08
ADAM_WD = 0.01
ADAM_STEP = 10

MESH_ID = pl.DeviceIdType.MESH


def _pick(n, cands):
    for c in cands:
        if c <= n and n % c == 0:
            return c
    return n


def _params(dims):
    return pltpu.CompilerParams(dimension_semantics=dims, vmem_limit_bytes=VMEM_LIMIT)


def _dot(a, b, dims=(((1,), (0,)), ((), ())), precision=None):
    return lax.dot_general(a, b, dims, precision=precision, preferred_element_type=F32)


NN = (((1,), (0,)), ((), ()))
NT = (((1,), (1,)), ((), ()))
TN = (((0,), (0,)), ((), ()))


def _sigmoid(x):
    return 1.0 / (1.0 + jnp.exp(-x))


def _dot01(a, b, dims, split, terms=3):
    v = (a, b)[split]
    ones = (a, b)[1 - split].astype(BF16)
    acc = None
    for _ in range(terms):
        piece = v.astype(BF16)
        v = v - piece.astype(F32)
        part = _dot(piece, ones, dims) if split == 0 else _dot(ones, piece, dims)
        acc = part if acc is None else acc + part
    return acc


MM_VMEM_BUDGET = 36 * 1024 * 1024


def _mm_tiles(mode, m, n, k, a_bytes, b_bytes, out_bytes, has_add):
    tm = _pick(m, (1408, 1024, 512, 256, 128))
    tn_cands = [c for c in (2816, 1024, 512, 256, 128) if c <= n and n % c == 0] or [n]
    tk_cands = [k] + [c for c in (2816, 2048, 1024, 512, 256, 128) if c < k and k % c == 0]
    for tk in tk_cands:
        for tn in tn_cands:
            blocks = tm * tk * a_bytes + tk * tn * b_bytes + tm * tn * (out_bytes + (4 if has_add else 0))
            if 2 * blocks + (tm * tn * 4 if tk < k else 0) <= MM_VMEM_BUDGET:
                return tm, tn, tk
    return tm, tn_cands[-1], tk_cands[-1]


def _mm(a, b, mode, out_dtype, add=None, norm=None):
    if mode == "nn":
        m, k = a.shape
        n = b.shape[1]
    elif mode == "nt":
        m, k = a.shape
        n = b.shape[0]
    else:
        k, m = a.shape
        n = b.shape[1]
    tm, tn, tk = _mm_tiles(mode, m, n, k, a.dtype.itemsize, b.dtype.itemsize, jnp.dtype(out_dtype).itemsize,
                           add is not None)
    nk = k // tk
    dims = {"nn": NN, "nt": NT, "tn": TN}[mode]
    assert norm is None or tn == n

    def body(*refs):
        refs = list(refs)
        a_ref, b_ref = refs[:2]
        del refs[:2]
        add_ref = refs.pop(0) if add is not None else None
        nw_ref = refs.pop(0) if norm is not None else None
        o_ref = refs.pop(0)
        n_ref = refs.pop(0) if norm is not None else None
        rest = refs
        part = _dot(a_ref[...].astype(BF16), b_ref[...].astype(BF16), dims)

        def finish(acc):
            if add_ref is not None:
                acc = acc + add_ref[...]
            o_ref[...] = acc.astype(o_ref.dtype)
            if n_ref is not None:
                r = lax.rsqrt(jnp.mean(acc * acc, axis=-1, keepdims=True) + RMS_EPS)
                n_ref[...] = (acc * r * nw_ref[...]).astype(n_ref.dtype)

        if nk == 1:
            finish(part)
        else:
            acc_ref = rest[0]
            kk = pl.program_id(2)

            @pl.when(kk == 0)
            def _():
                acc_ref[...] = part

            @pl.when(kk > 0)
            def _():
                acc_ref[...] += part

            @pl.when(kk == nk - 1)
            def _():
                finish(acc_ref[...])

    if mode == "nn":
        a_spec = pl.BlockSpec((tm, tk), lambda i, j, kk: (i, kk))
        b_spec = pl.BlockSpec((tk, tn), lambda i, j, kk: (kk, j))
    elif mode == "nt":
        a_spec = pl.BlockSpec((tm, tk), lambda i, j, kk: (i, kk))
        b_spec = pl.BlockSpec((tn, tk), lambda i, j, kk: (j, kk))
    else:
        a_spec = pl.BlockSpec((tk, tm), lambda i, j, kk: (kk, i))
        b_spec = pl.BlockSpec((tk, tn), lambda i, j, kk: (kk, j))
    o_spec = pl.BlockSpec((tm, tn), lambda i, j, kk: (i, j))
    in_specs = [a_spec, b_spec] + ([o_spec] if add is not None else [])
    args = (a, b) + ((add,) if add is not None else ())
    out_specs, out_shape = o_spec, jax.ShapeDtypeStruct((m, n), out_dtype)
    if norm is not None:
        in_specs.append(pl.BlockSpec((1, n), lambda i, j, kk: (0, 0)))
        args += (norm,)
        out_specs, out_shape = [o_spec, o_spec], [out_shape, jax.ShapeDtypeStruct((m, n), BF16)]
    return pl.pallas_call(
        body,
        grid=(m // tm, n // tn, nk),
        in_specs=in_specs,
        out_specs=out_specs,
        out_shape=out_shape,
        scratch_shapes=[pltpu.VMEM((tm, tn), F32)] if nk > 1 else [],
        compiler_params=_params(("parallel", "parallel", "arbitrary")),
        name=f"mm_{mode}_{m}x{k}x{n}",
    )(*args)


def _rms_fwd(x, w):
    s, d = x.shape
    tr = _pick(s, (512, 256, 128))

    def body(x_ref, w_ref, o_ref):
        xv = x_ref[...]
        r = lax.rsqrt(jnp.mean(xv * xv, axis=-1, keepdims=True) + RMS_EPS)
        o_ref[...] = (xv * r * w_ref[...]).astype(o_ref.dtype)

    return pl.pallas_call(
        body,
        grid=(s // tr,),
        in_specs=[pl.BlockSpec((tr, d), lambda i: (i, 0)), pl.BlockSpec((1, d), lambda i: (0, 0))],
        out_specs=pl.BlockSpec((tr, d), lambda i: (i, 0)),
        out_shape=jax.ShapeDtypeStruct((s, d), BF16),
        compiler_params=_params(("parallel",)),
        name="rms_fwd",
    )(x, w)


def _rms_bwd(dyn, x, w, add):
    s, d = x.shape
    tr = _pick(s, (512, 256, 128))

    def body(dy_ref, x_ref, w_ref, add_ref, dx_ref, dw_ref):
        xv = x_ref[...]
        dy = dy_ref[...].astype(F32)
        r = lax.rsqrt(jnp.mean(xv * xv, axis=-1, keepdims=True) + RMS_EPS)
        xn = xv * r
        dxh = dy * w_ref[...]
        dx = r * (dxh - xn * jnp.mean(dxh * xn, axis=-1, keepdims=True))
        dx_ref[...] = add_ref[...] + dx
        part = jnp.sum(dy * xn, axis=0, keepdims=True)

        @pl.when(pl.program_id(0) == 0)
        def _():
            dw_ref[...] = part

        @pl.when(pl.program_id(0) > 0)
        def _():
            dw_ref[...] += part

    row = pl.BlockSpec((tr, d), lambda i: (i, 0))
    vec = pl.BlockSpec((1, d), lambda i: (0, 0))
    return pl.pallas_call(
        body,
        grid=(s // tr,),
        in_specs=[row, row, vec, row],
        out_specs=[row, vec],
        out_shape=[jax.ShapeDtypeStruct((s, d), F32), jax.ShapeDtypeStruct((1, d), F32)],
        compiler_params=_params(("arbitrary",)),
        name="rms_bwd",
    )(dyn, x, w, add)


CONV_ROWS = 256
CONV_COLS = 256
CONV_HALO = 16


def _conv_taps(ext, w, base, rows):
    acc = w[0:1, :] * ext[base:base + rows]
    for k in range(1, CONV_K):
        acc = acc + w[k:k + 1, :] * ext[base + k:base + k + rows]
    return acc


def _ssd_conv_fwd(zx, conv_w, conv_b):
    s = zx.shape[0]
    c = SSD_CONV_DIM
    nsteps = s // CONV_ROWS
    off = SSD_INNER // CONV_COLS

    def body(x_ref, w_ref, b_ref, o_ref):
        w = w_ref[...]
        b = b_ref[...]

        def step(i, carry):
            r0 = pl.multiple_of(i * CONV_ROWS, CONV_ROWS)
            cur = x_ref[pl.ds(r0, CONV_ROWS), :].astype(F32)
            p0 = pl.multiple_of(jnp.maximum(r0 - CONV_HALO, 0), CONV_HALO)
            prev = x_ref[pl.ds(p0, CONV_HALO), :].astype(F32)
            prev = jnp.where(i == 0, 0.0, prev)
            ext = jnp.concatenate([prev, cur], axis=0)
            acc = _conv_taps(ext, w, CONV_HALO - (CONV_K - 1), CONV_ROWS) + b
            o_ref[pl.ds(r0, CONV_ROWS), :] = (acc * _sigmoid(acc)).astype(o_ref.dtype)
            return carry

        lax.fori_loop(0, nsteps, step, 0)

    return pl.pallas_call(
        body,
        grid=(c // CONV_COLS,),
        in_specs=[pl.BlockSpec((s, CONV_COLS), lambda j: (0, j + off)),
                  pl.BlockSpec((8, CONV_COLS), lambda j: (0, j)),
                  pl.BlockSpec((1, CONV_COLS), lambda j: (0, j))],
        out_specs=pl.BlockSpec((s, CONV_COLS), lambda j: (0, j)),
        out_shape=jax.ShapeDtypeStruct((s, c), BF16),
        compiler_params=_params(("parallel",)),
        name="ssd_conv_fwd",
    )(zx, conv_w, conv_b)


def _ssd_conv_bwd(zx, dxbc, conv_w, conv_b, dzx):
    s = zx.shape[0]
    c = SSD_CONV_DIM
    nsteps = s // CONV_ROWS
    off = SSD_INNER // CONV_COLS

    def body(x_ref, dy_ref, w_ref, b_ref, dzx_in_ref, dx_ref, dw_ref, db_ref, dc_ref):
        w = w_ref[...]
        b = b_ref[...]
        dc_ref[pl.ds(s, CONV_HALO), :] = jnp.zeros((CONV_HALO, CONV_COLS), F32)

        def step1(i, carry):
            dw0, dw1, dw2, dw3, dbs = carry
            r0 = pl.multiple_of(i * CONV_ROWS, CONV_ROWS)
            cur = x_ref[pl.ds(r0, CONV_ROWS), :].astype(F32)
            p0 = pl.multiple_of(jnp.maximum(r0 - CONV_HALO, 0), CONV_HALO)
            prev = x_ref[pl.ds(p0, CONV_HALO), :].astype(F32)
            prev = jnp.where(i == 0, 0.0, prev)
            ext = jnp.concatenate([prev, cur], axis=0)
            base = CONV_HALO - (CONV_K - 1)
            acc = _conv_taps(ext, w, base, CONV_ROWS) + b
            sg = _sigmoid(acc)
            dcv = dy_ref[pl.ds(r0, CONV_ROWS), :].astype(F32) * (sg * (1.0 + acc * (1.0 - sg)))
            dc_ref[pl.ds(r0, CONV_ROWS), :] = dcv
            dws = [jnp.sum(dcv * ext[base + k:base + k + CONV_ROWS], axis=0, keepdims=True) for k in range(CONV_K)]
            return (dw0 + dws[0], dw1 + dws[1], dw2 + dws[2], dw3 + dws[3], dbs + jnp.sum(dcv, axis=0, keepdims=True))

        z = jnp.zeros((1, CONV_COLS), F32)
        dw0, dw1, dw2, dw3, dbs = lax.fori_loop(0, nsteps, step1, (z, z, z, z, z))
        dw_ref[...] = jnp.concatenate([dw0, dw1, dw2, dw3, z, z, z, z], axis=0)
        db_ref[...] = dbs

        def step2(i, carry):
            r0 = pl.multiple_of(i * CONV_ROWS, CONV_ROWS)
            ext = dc_ref[pl.ds(r0, CONV_ROWS + CONV_HALO), :]
            acc = w[0:1, :] * ext[CONV_K - 1:CONV_K - 1 + CONV_ROWS]
            for k in range(1, CONV_K):
                acc = acc + w[k:k + 1, :] * ext[CONV_K - 1 - k:CONV_K - 1 - k + CONV_ROWS]
            dx_ref[pl.ds(r0, CONV_ROWS), :] = acc.astype(dx_ref.dtype)
            return carry

        lax.fori_loop(0, nsteps, step2, 0)

    col = pl.BlockSpec((s, CONV_COLS), lambda j: (0, j))
    shifted = pl.BlockSpec((s, CONV_COLS), lambda j: (0, j + off))
    return pl.pallas_call(
        body,
        grid=(c // CONV_COLS,),
        in_specs=[shifted, col,
                  pl.BlockSpec((8, CONV_COLS), lambda j: (0, j)),
                  pl.BlockSpec((1, CONV_COLS), lambda j: (0, j)),
                  pl.BlockSpec(memory_space=pl.ANY)],
        out_specs=[shifted, pl.BlockSpec((8, CONV_COLS), lambda j: (0, j)), pl.BlockSpec((1, CONV_COLS), lambda j: (0, j))],
        out_shape=[jax.ShapeDtypeStruct((s, SSD_ZX), BF16), jax.ShapeDtypeStruct((8, c), F32),
                   jax.ShapeDtypeStruct((1, c), F32)],
        scratch_shapes=[pltpu.VMEM((s + CONV_HALO, CONV_COLS), F32)],
        input_output_aliases={4: 0},
        compiler_params=_params(("parallel",)),
        name="ssd_conv_bwd",
    )(zx, dxbc, conv_w, conv_b, dzx)


def _ssd_consts():
    li = lax.broadcasted_iota(jnp.int32, (CHUNK, CHUNK), 0)
    si = lax.broadcasted_iota(jnp.int32, (CHUNK, CHUNK), 1)
    tril = li >= si
    hrow = lax.broadcasted_iota(jnp.int32, (LANES, SSD_INNER), 0)
    hcol = lax.broadcasted_iota(jnp.int32, (LANES, SSD_INNER), 1) // SSD_HEADDIM
    expand = (hrow == hcol).astype(F32)
    return tril, expand


def _ssd_chunk_common(dtp_ref, bias_ref, alog_ref, tril, expand):
    lane = lax.broadcasted_iota(jnp.int32, (1, LANES), 1)
    valid = lane < SSD_HEADS
    pre = dtp_ref[...] + bias_ref[...]
    dt = jnp.where(valid, jnp.maximum(pre, 0.0) + jnp.log1p(jnp.exp(-jnp.abs(pre))), 0.0)
    a = jnp.where(valid, -jnp.exp(alog_ref[...]), 0.0)
    da = dt * a
    cs = _dot01(tril.astype(F32), da, NN, 1)
    cs_x = _dot01(cs, expand, NN, 0)
    dt_x = _dot01(dt, expand, NN, 0, terms=2)
    return pre, dt, a, cs, cs_x, dt_x


def _ssd_scan_fwd(xbc, dtp, dt_bias, a_log, d_skip):
    s = xbc.shape[0]
    nc = s // CHUNK
    gw = SSD_GROUP_W

    def body(xbc_ref, dtp_ref, bias_ref, alog_ref, d_ref, y_ref, prev_ref, state_ref):
        c = pl.program_id(0)

        @pl.when(c == 0)
        def _():
            state_ref[...] = jnp.zeros_like(state_ref)

        tril, expand = _ssd_consts()
        pre, dt, a, cs, cs_x, dt_x = _ssd_chunk_common(dtp_ref, bias_ref, alog_ref, tril, expand)
        cs_t = cs.T
        d_x = _dot01(jnp.broadcast_to(d_ref[...], (8, LANES)), expand, NN, 0)[0:1, :]
        cs_last = cs_x[CHUNK - 1:CHUNK, :]
        dec_out = jnp.exp(cs_x)
        dec_st = jnp.exp(cs_last - cs_x)
        dec_ch = jnp.exp(cs_last)
        x = xbc_ref[:, 0:SSD_INNER].astype(F32)
        xr = x * dt_x
        xrs = xr * dec_st
        lane_g = lax.broadcasted_iota(jnp.int32, (1, gw), 1) // SSD_HEADDIM
        for g in range(SSD_GROUPS):
            sl = slice(g * gw, (g + 1) * gw)
            bg = xbc_ref[:, SSD_INNER + g * SSD_STATE:SSD_INNER + (g + 1) * SSD_STATE]
            cg = xbc_ref[:, SSD_INNER + (SSD_GROUPS + g) * SSD_STATE:SSD_INNER + (SSD_GROUPS + g + 1) * SSD_STATE]
            cb = _dot(cg, bg, NT)
            prev_g = state_ref[:, sl]
            prev_ref[0, :, sl] = prev_g
            yo = _dot(cg, prev_g.astype(BF16), NN) * dec_out[:, sl]
            xr_g = xr[:, sl]
            yd = jnp.zeros((CHUNK, gw), F32)
            for r in range(SSD_HEADS // SSD_GROUPS):
                h = g * (SSD_HEADS // SSD_GROUPS) + r
                diff = cs[:, h:h + 1] - cs_t[h:h + 1, :]
                lmat = jnp.exp(jnp.where(tril, diff, -1e30))
                wmat = (cb * lmat).astype(BF16)
                xr_h = jnp.where(lane_g == r, xr_g, 0.0).astype(BF16)
                yd = yd + _dot(wmat, xr_h, NN)
            y_ref[:, sl] = yd + yo + x[:, sl] * d_x[:, sl]
            sc = _dot(bg, xrs[:, sl].astype(BF16), TN)
            state_ref[:, sl] = prev_g * dec_ch[:, sl] + sc

    vec = pl.BlockSpec((1, LANES), lambda c: (0, 0))
    return pl.pallas_call(
        body,
        grid=(nc,),
        in_specs=[pl.BlockSpec((CHUNK, SSD_CONV_DIM), lambda c: (c, 0)),
                  pl.BlockSpec((CHUNK, LANES), lambda c: (c, 0)), vec, vec, vec],
        out_specs=[pl.BlockSpec((CHUNK, SSD_INNER), lambda c: (c, 0)),
                   pl.BlockSpec((1, SSD_STATE, SSD_INNER), lambda c: (c, 0, 0))],
        out_shape=[jax.ShapeDtypeStruct((s, SSD_INNER), F32), jax.ShapeDtypeStruct((nc, SSD_STATE, SSD_INNER), F32)],
        scratch_shapes=[pltpu.VMEM((SSD_STATE, SSD_INNER), F32)],
        compiler_params=_params(("arbitrary",)),
        name="ssd_scan_fwd",
    )(xbc, dtp, dt_bias, a_log, d_skip)


def _ssd_scan_bwd(xbc, dtp, prev, dy, dt_bias, a_log, d_skip):
    s = xbc.shape[0]
    nc = s // CHUNK
    gw = SSD_GROUP_W
    hpg = SSD_HEADS // SSD_GROUPS

    def body(xbc_ref, dtp_ref, prev_ref, dy_ref, bias_ref, alog_ref, d_ref,
             dxbc_ref, ddtp_ref, dbias_ref, dalog_ref, dd_ref, dp_ref, ddx_ref):
        step = pl.program_id(0)

        @pl.when(step == 0)
        def _():
            dp_ref[...] = jnp.zeros_like(dp_ref)
            ddx_ref[...] = jnp.zeros_like(ddx_ref)
            dbias_ref[...] = jnp.zeros_like(dbias_ref)
            dalog_ref[...] = jnp.zeros_like(dalog_ref)

        tril, expand = _ssd_consts()
        pre, dt, a, cs, cs_x, dt_x = _ssd_chunk_common(dtp_ref, bias_ref, alog_ref, tril, expand)
        cs_t = cs.T
        d_x = _dot01(jnp.broadcast_to(d_ref[...], (8, LANES)), expand, NN, 0)[0:1, :]
        cs_last = cs_x[CHUNK - 1:CHUNK, :]
        dec_out = jnp.exp(cs_x)
        dec_st = jnp.exp(cs_last - cs_x)
        dec_ch = jnp.exp(cs_last)
        x = xbc_ref[:, 0:SSD_INNER].astype(F32)
        dyv = dy_ref[...]
        xr = x * dt_x
        xrs = xr * dec_st
        lane_g = lax.broadcasted_iota(jnp.int32, (1, gw), 1) // SSD_HEADDIM
        hsel = lax.broadcasted_iota(jnp.int32, (CHUNK, LANES), 1)
        dcs = jnp.zeros((CHUNK, LANES), F32)
        last_parts = []
        t_parts = []
        dxr_parts = []
        for g in range(SSD_GROUPS):
            sl = slice(g * gw, (g + 1) * gw)
            bsl = slice(SSD_INNER + g * SSD_STATE, SSD_INNER + (g + 1) * SSD_STATE)
            csl = slice(SSD_INNER + (SSD_GROUPS + g) * SSD_STATE, SSD_INNER + (SSD_GROUPS + g + 1) * SSD_STATE)
            bg = xbc_ref[:, bsl]
            cg = xbc_ref[:, csl]
            cb = _dot(cg, bg, NT)
            prev_g = prev_ref[0, :, sl]
            prev_b = prev_g.astype(BF16)
            dp_g = dp_ref[:, sl]
            dp_b = dp_g.astype(BF16)
            dy_g = dyv[:, sl]
            xr_g = xr[:, sl]
            gmat = _dot(cg, prev_b, NN)
            dgm = (dy_g * dec_out[:, sl]).astype(BF16)
            dc_g = _dot(dgm, prev_b, NT)
            dprev = _dot(cg, dgm, TN)
            t1 = dy_g * gmat * dec_out[:, sl]
            mm_ = _dot(bg, dp_b, NN)
            db_g = _dot(xrs[:, sl].astype(BF16), dp_b, NT)
            dxr_g = mm_ * dec_st[:, sl]
            t2 = dxr_g * xr_g
            last = jnp.sum(t2, axis=0, keepdims=True) + jnp.sum(dp_g * prev_g, axis=0, keepdims=True) * dec_ch[:, sl]
            dp_ref[:, sl] = dp_g * dec_ch[:, sl] + dprev
            dcb = jnp.zeros((CHUNK, CHUNK), F32)
            for r in range(hpg):
                h = g * hpg + r
                diff = cs[:, h:h + 1] - cs_t[h:h + 1, :]
                lmat = jnp.exp(jnp.where(tril, diff, -1e30))
                wmat = cb * lmat
                dy_h = jnp.where(lane_g == r, dy_g, 0.0).astype(BF16)
                dw = _dot(dy_h, xr_g.astype(BF16), NT)
                dxr_g = dxr_g + _dot(wmat.astype(BF16), dy_h, TN)
                dcb = dcb + dw * lmat
                q = (dw * wmat).astype(BF16)
                onehot = (hsel == h).astype(BF16)
                dcs = dcs + _dot(q, onehot, NN) - _dot(q, onehot, TN)
            dcb_b = dcb.astype(BF16)
            dc_g = dc_g + _dot(dcb_b, bg, NN)
            db_g = db_g + _dot(dcb_b, cg, TN)
            dxbc_ref[:, bsl] = db_g.astype(dxbc_ref.dtype)
            dxbc_ref[:, csl] = dc_g.astype(dxbc_ref.dtype)
            t_parts.append(t1 - t2)
            last_parts.append(last)
            dxr_parts.append(dxr_g)
        dxr = jnp.concatenate(dxr_parts, axis=1)
        tt = jnp.concatenate(t_parts, axis=1)
        last_x = jnp.concatenate(last_parts, axis=1)
        dxbc_ref[:, 0:SSD_INNER] = (dxr * dt_x + dyv * d_x).astype(dxbc_ref.dtype)
        dcs = dcs + _dot01(tt, expand, NT, 0, terms=2)
        last_h = _dot01(jnp.broadcast_to(last_x, (8, SSD_INNER)), expand, NT, 0)[0:1, :]
        rowi = lax.broadcasted_iota(jnp.int32, (CHUNK, LANES), 0)
        dcs = dcs + jnp.where(rowi == CHUNK - 1, last_h, 0.0)
        dda = _dot01(tril.astype(F32), dcs, TN, 1)
        ddt = dda * a + _dot01(dxr * x, expand, NT, 0, terms=2)
        dpre = ddt * _sigmoid(pre)
        ddtp_ref[...] = dpre
        dbias_ref[...] += jnp.sum(dpre, axis=0, keepdims=True)
        dalog_ref[...] += jnp.sum(dda * dt, axis=0, keepdims=True) * a
        ddx_ref[...] += jnp.broadcast_to(jnp.sum(dyv * x, axis=0, keepdims=True), (8, SSD_INNER))

        @pl.when(step == nc - 1)
        def _():
            dd_ref[...] = _dot01(ddx_ref[...], expand, NT, 0)[0:1, :]

    rev = lambda c: (nc - 1 - c, 0)
    vec = pl.BlockSpec((1, LANES), lambda c: (0, 0))
    return pl.pallas_call(
        body,
        grid=(nc,),
        in_specs=[pl.BlockSpec((CHUNK, SSD_CONV_DIM), rev), pl.BlockSpec((CHUNK, LANES), rev),
                  pl.BlockSpec((1, SSD_STATE, SSD_INNER), lambda c: (nc - 1 - c, 0, 0)),
                  pl.BlockSpec((CHUNK, SSD_INNER), rev), vec, vec, vec],
        out_specs=[pl.BlockSpec((CHUNK, SSD_CONV_DIM), rev), pl.BlockSpec((CHUNK, LANES), rev), vec, vec, vec],
        out_shape=[jax.ShapeDtypeStruct((s, SSD_CONV_DIM), BF16), jax.ShapeDtypeStruct((s, LANES), F32),
                   jax.ShapeDtypeStruct((1, LANES), F32), jax.ShapeDtypeStruct((1, LANES), F32),
                   jax.ShapeDtypeStruct((1, LANES), F32)],
        scratch_shapes=[pltpu.VMEM((SSD_STATE, SSD_INNER), F32), pltpu.VMEM((8, SSD_INNER), F32)],
        compiler_params=_params(("arbitrary",)),
        name="ssd_scan_bwd",
    )(xbc, dtp, prev, dy, dt_bias, a_log, d_skip)


def _ssd_gate_fwd(y, zx, norm_w):
    s = y.shape[0]
    tr = _pick(s, (256, 128))
    gw = SSD_GROUP_W

    def body(y_ref, z_ref, w_ref, o_ref):
        for g in range(SSD_GROUPS):
            sl = slice(g * gw, (g + 1) * gw)
            z = z_ref[:, sl].astype(F32)
            gv = y_ref[:, sl] * (z * _sigmoid(z))
            r = lax.rsqrt(jnp.mean(gv * gv, axis=-1, keepdims=True) + LN_EPS)
            o_ref[:, sl] = (gv * r * w_ref[:, sl]).astype(o_ref.dtype)

    row = pl.BlockSpec((tr, SSD_INNER), lambda i: (i, 0))
    return pl.pallas_call(
        body,
        grid=(s // tr,),
        in_specs=[row, row, pl.BlockSpec((1, SSD_INNER), lambda i: (0, 0))],
        out_specs=row,
        out_shape=jax.ShapeDtypeStruct((s, SSD_INNER), BF16),
        compiler_params=_params(("parallel",)),
        name="ssd_gate_fwd",
    )(y, zx, norm_w)


def _ssd_gate_bwd(dgn, y, zx, norm_w):
    s = y.shape[0]
    tr = _pick(s, (256, 128))
    gw = SSD_GROUP_W

    def body(dg_ref, y_ref, z_ref, w_ref, dy_ref, dz_ref, dw_ref):
        parts = []
        for g in range(SSD_GROUPS):
            sl = slice(g * gw, (g + 1) * gw)
            z = z_ref[:, sl].astype(F32)
            yv = y_ref[:, sl]
            sg = _sigmoid(z)
            sz = z * sg
            gv = yv * sz
            r = lax.rsqrt(jnp.mean(gv * gv, axis=-1, keepdims=True) + LN_EPS)
            gn = gv * r
            dout = dg_ref[:, sl].astype(F32)
            parts.append(jnp.sum(dout * gn, axis=0, keepdims=True))
            dgn_ = dout * w_ref[:, sl]
            dgv = r * (dgn_ - gn * jnp.mean(dgn_ * gn, axis=-1, keepdims=True))
            dy_ref[:, sl] = dgv * sz
            dz_ref[:, sl] = (dgv * yv * (sg * (1.0 + z * (1.0 - sg)))).astype(dz_ref.dtype)
        part = jnp.concatenate(parts, axis=1)

        @pl.when(pl.program_id(0) == 0)
        def _():
            dw_ref[...] = part

        @pl.when(pl.program_id(0) > 0)
        def _():
            dw_ref[...] += part

    row = pl.BlockSpec((tr, SSD_INNER), lambda i: (i, 0))
    vec = pl.BlockSpec((1, SSD_INNER), lambda i: (0, 0))
    return pl.pallas_call(
        body,
        grid=(s // tr,),
        in_specs=[row, row, row, vec],
        out_specs=[row, row, vec],
        out_shape=[jax.ShapeDtypeStruct((s, SSD_INNER), F32), jax.ShapeDtypeStruct((s, SSD_ZX), BF16),
                   jax.ShapeDtypeStruct((1, SSD_INNER), F32)],
        compiler_params=_params(("arbitrary",)),
        name="ssd_gate_bwd",
    )(dgn, y, zx, norm_w)


INV_SQRT2 = 1.0 / math.sqrt(2.0)
INV_SQRT2PI = 1.0 / math.sqrt(2.0 * math.pi)


def _gelu(x):
    return 0.5 * x * (1.0 + lax.erf(x * INV_SQRT2))


def _gelu_grad(x):
    return 0.5 * (1.0 + lax.erf(x * INV_SQRT2)) + x * INV_SQRT2PI * jnp.exp(-0.5 * x * x)


def _gmlp_act_fwd(pre, b_in, ln_w, ln_b):
    s = pre.shape[0]
    tr = _pick(s, (256, 128))
    n = GMLP_INNER

    def body(p_ref, b_ref, w_ref, lb_ref, u_ref, v_ref):
        u_ref[...] = _gelu(p_ref[:, 0:n].astype(F32) + b_ref[:, 0:n]).astype(u_ref.dtype)
        hv = _gelu(p_ref[:, n:2 * n].astype(F32) + b_ref[:, n:2 * n])
        mu = jnp.mean(hv, axis=-1, keepdims=True)
        xc = hv - mu
        r = lax.rsqrt(jnp.mean(xc * xc, axis=-1, keepdims=True) + LN_EPS)
        v_ref[...] = (xc * r * w_ref[...] + lb_ref[...]).astype(v_ref.dtype)

    half = pl.BlockSpec((tr, n), lambda i: (i, 0))
    vec = pl.BlockSpec((1, n), lambda i: (0, 0))
    return pl.pallas_call(
        body,
        grid=(s // tr,),
        in_specs=[pl.BlockSpec((tr, 2 * n), lambda i: (i, 0)), pl.BlockSpec((1, 2 * n), lambda i: (0, 0)), vec, vec],
        out_specs=[half, half],
        out_shape=[jax.ShapeDtypeStruct((s, n), BF16), jax.ShapeDtypeStruct((s, n), BF16)],
        compiler_params=_params(("parallel",)),
        name="gmlp_act_fwd",
    )(pre, b_in, ln_w, ln_b)


def _gmlp_act_bwd(pre, b_in, ln_w, du, dv):
    s = pre.shape[0]
    tr = _pick(s, (256, 128))
    n = GMLP_INNER

    def body(p_ref, b_ref, w_ref, du_ref, dv_ref, dp_ref, db_ref, dw_ref, dlb_ref):
        xu = p_ref[:, 0:n].astype(F32) + b_ref[:, 0:n]
        dpu = du_ref[...].astype(F32) * _gelu_grad(xu)
        xv = p_ref[:, n:2 * n].astype(F32) + b_ref[:, n:2 * n]
        hv = _gelu(xv)
        mu = jnp.mean(hv, axis=-1, keepdims=True)
        xc = hv - mu
        r = lax.rsqrt(jnp.mean(xc * xc, axis=-1, keepdims=True) + LN_EPS)
        vh = xc * r
        dvv = dv_ref[...].astype(F32)
        dvh = dvv * w_ref[...]
        dh = r * (dvh - jnp.mean(dvh, axis=-1, keepdims=True) - vh * jnp.mean(dvh * vh, axis=-1, keepdims=True))
        dpv = dh * _gelu_grad(xv)
        dp_ref[:, 0:n] = dpu.astype(dp_ref.dtype)
        dp_ref[:, n:2 * n] = dpv.astype(dp_ref.dtype)
        pb = jnp.concatenate([jnp.sum(dpu, axis=0, keepdims=True), jnp.sum(dpv, axis=0, keepdims=True)], axis=1)
        pw = jnp.sum(dvv * vh, axis=0, keepdims=True)
        plb = jnp.sum(dvv, axis=0, keepdims=True)

        @pl.when(pl.program_id(0) == 0)
        def _():
            db_ref[...] = pb
            dw_ref[...] = pw
            dlb_ref[...] = plb

        @pl.when(pl.program_id(0) > 0)
        def _():
            db_ref[...] += pb
            dw_ref[...] += pw
            dlb_ref[...] += plb

    half = pl.BlockSpec((tr, n), lambda i: (i, 0))
    full = pl.BlockSpec((tr, 2 * n), lambda i: (i, 0))
    vec = pl.BlockSpec((1, n), lambda i: (0, 0))
    vec2 = pl.BlockSpec((1, 2 * n), lambda i: (0, 0))
    return pl.pallas_call(
        body,
        grid=(s // tr,),
        in_specs=[full, vec2, vec, half, half],
        out_specs=[full, vec2, vec, vec],
        out_shape=[jax.ShapeDtypeStruct((s, 2 * n), BF16), jax.ShapeDtypeStruct((1, 2 * n), F32),
                   jax.ShapeDtypeStruct((1, n), F32), jax.ShapeDtypeStruct((1, n), F32)],
        compiler_params=_params(("arbitrary",)),
        name="gmlp_act_bwd",
    )(pre, b_in, ln_w, du, dv)


def _gmlp_mix_fwd(u, v, w_s, b_st):
    s = u.shape[0]
    gd = GMLP_INNER // GMLP_GROUPS

    def body(u_ref, v_ref, w_ref, b_ref, o_ref):
        li = lax.broadcasted_iota(jnp.int32, (CHUNK, CHUNK), 0)
        si = lax.broadcasted_iota(jnp.int32, (CHUNK, CHUNK), 1)
        tril = li >= si
        for g in range(GMLP_GROUPS):
            sl = slice(g * gd, (g + 1) * gd)
            wm = jnp.where(tril, w_ref[g], 0.0).astype(BF16)
            mixed = _dot(wm, v_ref[:, sl], NN) + b_ref[:, g:g + 1]
            o_ref[:, sl] = (u_ref[:, sl].astype(F32) * mixed).astype(o_ref.dtype)

    row = pl.BlockSpec((CHUNK, GMLP_INNER), lambda c: (c, 0))
    return pl.pallas_call(
        body,
        grid=(s // CHUNK,),
        in_specs=[row, row, pl.BlockSpec((GMLP_GROUPS, CHUNK, CHUNK), lambda c: (0, 0, 0)),
                  pl.BlockSpec((CHUNK, LANES), lambda c: (0, 0))],
        out_specs=row,
        out_shape=jax.ShapeDtypeStruct((s, GMLP_INNER), BF16),
        compiler_params=_params(("parallel",)),
        name="gmlp_mix_fwd",
    )(u, v, w_s, b_st)


def _gmlp_mix_bwd(dgated, u, v, w_s, b_st):
    s = u.shape[0]
    nc = s // CHUNK
    gd = GMLP_INNER // GMLP_GROUPS

    def body(dg_ref, u_ref, v_ref, w_ref, b_ref, du_ref, dv_ref, dw_ref, db_ref):
        c = pl.program_id(0)

        @pl.when(c == 0)
        def _():
            dw_ref[...] = jnp.zeros_like(dw_ref)
            db_ref[...] = jnp.zeros_like(db_ref)

        li = lax.broadcasted_iota(jnp.int32, (CHUNK, CHUNK), 0)
        si = lax.broadcasted_iota(jnp.int32, (CHUNK, CHUNK), 1)
        tril = li >= si
        lane = lax.broadcasted_iota(jnp.int32, (CHUNK, LANES), 1)
        dbacc = jnp.zeros((CHUNK, LANES), F32)
        for g in range(GMLP_GROUPS):
            sl = slice(g * gd, (g + 1) * gd)
            wm = jnp.where(tril, w_ref[g], 0.0).astype(BF16)
            vg = v_ref[:, sl]
            mixed = _dot(wm, vg, NN) + b_ref[:, g:g + 1]
            dgv = dg_ref[:, sl].astype(F32)
            du_ref[:, sl] = (dgv * mixed).astype(du_ref.dtype)
            dm = dgv * u_ref[:, sl].astype(F32)
            dm_b = dm.astype(BF16)
            dv_ref[:, sl] = _dot(wm, dm_b, TN).astype(dv_ref.dtype)
            dw_ref[g] += jnp.where(tril, _dot(dm_b, vg, NT), 0.0)
            dbacc = dbacc + jnp.where(lane == g, jnp.sum(dm, axis=1, keepdims=True), 0.0)
        db_ref[...] += dbacc

    row = pl.BlockSpec((CHUNK, GMLP_INNER), lambda c: (c, 0))
    wspec = pl.BlockSpec((GMLP_GROUPS, CHUNK, CHUNK), lambda c: (0, 0, 0))
    bspec = pl.BlockSpec((CHUNK, LANES), lambda c: (0, 0))
    return pl.pallas_call(
        body,
        grid=(nc,),
        in_specs=[row, row, row, wspec, bspec],
        out_specs=[row, row, wspec, bspec],
        out_shape=[jax.ShapeDtypeStruct((s, GMLP_INNER), BF16), jax.ShapeDtypeStruct((s, GMLP_INNER), BF16),
                   jax.ShapeDtypeStruct((GMLP_GROUPS, CHUNK, CHUNK), F32), jax.ShapeDtypeStruct((CHUNK, LANES), F32)],
        compiler_params=_params(("arbitrary",)),
        name="gmlp_mix_bwd",
    )(dgated, u, v, w_s, b_st)


FFN_HALF = FFN_DIM // 2


def _ffn_up(un, wgu):
    s, d = un.shape
    f = FFN_DIM
    tm = _pick(s, (1024, 512, 256, 128))
    nh = f // FFN_HALF

    def body(x_ref, wg_ref, wu_ref, g_ref, u_ref, h_ref):
        x = x_ref[...]
        g_ref[...] = _dot(x, wg_ref[...], NN).astype(g_ref.dtype)
        u_ref[...] = _dot(x, wu_ref[...], NN).astype(u_ref.dtype)
        gt = g_ref[...].astype(F32)
        h_ref[...] = (gt * _sigmoid(gt) * u_ref[...].astype(F32)).astype(h_ref.dtype)

    out = pl.BlockSpec((tm, FFN_HALF), lambda i, j: (i, j))
    sds = jax.ShapeDtypeStruct((s, f), BF16)
    return pl.pallas_call(
        body,
        grid=(s // tm, nh),
        in_specs=[pl.BlockSpec((tm, d), lambda i, j: (i, 0)), pl.BlockSpec((d, FFN_HALF), lambda i, j: (0, j)),
                  pl.BlockSpec((d, FFN_HALF), lambda i, j: (0, j + nh))],
        out_specs=[out, out, out],
        out_shape=[sds, sds, sds],
        compiler_params=_params(("parallel", "parallel")),
        name="ffn_up",
    )(un, wgu, wgu)


def _ffn_down_bwd(dh, wd, gate, up):
    s, d = dh.shape
    f = FFN_DIM
    tm = _pick(s, (512, 256, 128))

    def body(dh_ref, wd_ref, g_ref, u_ref, o_ref):
        dhb = dh_ref[...].astype(BF16)
        for half in range(f // FFN_HALF):
            cols = slice(half * FFN_HALF, (half + 1) * FFN_HALF)
            dhid = _dot(dhb, wd_ref[cols, :], NT)
            gt = g_ref[:, cols].astype(F32)
            sg = _sigmoid(gt)
            o_ref[:, cols] = (dhid * u_ref[:, cols].astype(F32) * (sg * (1.0 + gt * (1.0 - sg)))).astype(o_ref.dtype)
            o_ref[:, f + half * FFN_HALF:f + (half + 1) * FFN_HALF] = (dhid * gt * sg).astype(o_ref.dtype)

    row = pl.BlockSpec((tm, f), lambda i: (i, 0))
    return pl.pallas_call(
        body,
        grid=(s // tm,),
        in_specs=[pl.BlockSpec((tm, d), lambda i: (i, 0)), pl.BlockSpec((f, d), lambda i: (0, 0)), row, row],
        out_specs=pl.BlockSpec((tm, 2 * f), lambda i: (i, 0)),
        out_shape=jax.ShapeDtypeStruct((s, 2 * f), BF16),
        compiler_params=_params(("parallel",)),
        name="ffn_down_bwd",
    )(dh, wd, gate, up)


def _ple_fwd(p, h, w_proj, w_gate, gate_norm, ple_norm, next_norm):
    s, d = h.shape
    e = p.shape[1]
    tr = _pick(s, (512, 256, 128))

    def body(p_ref, h_ref, wp_ref, wg_ref, gn_ref, pn_ref, nn_ref, o_ref, pe_ref, hg_ref, gl_ref, hn_ref):
        hv = h_ref[...]
        pe_ref[...] = _dot(p_ref[...].astype(BF16), wp_ref[...], NN).astype(pe_ref.dtype)
        r = lax.rsqrt(jnp.mean(hv * hv, axis=-1, keepdims=True) + RMS_EPS)
        hg_ref[...] = (hv * r * gn_ref[...]).astype(hg_ref.dtype)
        gl_ref[...] = _dot(hg_ref[...], wg_ref[...], NN).astype(gl_ref.dtype)
        pe_ = pe_ref[...].astype(F32)
        rp = lax.rsqrt(jnp.mean(pe_ * pe_, axis=-1, keepdims=True) + RMS_EPS)
        out = hv + _sigmoid(gl_ref[...].astype(F32)) * (pe_ * rp * pn_ref[...])
        o_ref[...] = out
        ro = lax.rsqrt(jnp.mean(out * out, axis=-1, keepdims=True) + RMS_EPS)
        hn_ref[...] = (out * ro * nn_ref[...]).astype(hn_ref.dtype)

    row = pl.BlockSpec((tr, d), lambda i: (i, 0))
    vec = pl.BlockSpec((1, d), lambda i: (0, 0))
    sds = jax.ShapeDtypeStruct((s, d), BF16)
    return pl.pallas_call(
        body,
        grid=(s // tr,),
        in_specs=[pl.BlockSpec((tr, e), lambda i: (i, 0)), row, pl.BlockSpec((e, d), lambda i: (0, 0)),
                  pl.BlockSpec((d, d), lambda i: (0, 0)), vec, vec, vec],
        out_specs=[row, row, row, row, row],
        out_shape=[jax.ShapeDtypeStruct((s, d), F32), sds, sds, sds, sds],
        compiler_params=_params(("parallel",)),
        name="ple_fwd",
    )(p, h, w_proj, w_gate, gate_norm, ple_norm, next_norm)


def _ple_bwd(dh, p, pe, gl, hg, h, w_gate, gate_norm, ple_norm):
    s, d = dh.shape
    e = p.shape[1]
    tr = _pick(s, (512, 256, 128))

    def body(dh_ref, p_ref, pe_ref, gl_ref, hg_ref, h_ref, wg_ref, gn_ref, pn_ref,
             dx_ref, dwg_ref, dwp_ref, dpn_ref, dgn_ref):
        pe_ = pe_ref[...].astype(F32)
        dhv = dh_ref[...]
        r = lax.rsqrt(jnp.mean(pe_ * pe_, axis=-1, keepdims=True) + RMS_EPS)
        pn = pe_ * r
        gate = _sigmoid(gl_ref[...].astype(F32))
        dgl = (dhv * (pn * pn_ref[...]) * gate * (1.0 - gate)).astype(BF16)
        de = dhv * gate
        dxh = de * pn_ref[...]
        dpe = (r * (dxh - pn * jnp.mean(dxh * pn, axis=-1, keepdims=True))).astype(BF16)
        dhg = _dot(dgl, wg_ref[...], NT)
        hv = h_ref[...]
        rh = lax.rsqrt(jnp.mean(hv * hv, axis=-1, keepdims=True) + RMS_EPS)
        hn = hv * rh
        dhh = dhg * gn_ref[...]
        dx_ref[...] = dhv + rh * (dhh - hn * jnp.mean(dhh * hn, axis=-1, keepdims=True))
        parts = (_dot(hg_ref[...], dgl, TN), _dot(p_ref[...].astype(BF16), dpe, TN),
                 jnp.sum(de * pn, axis=0, keepdims=True), jnp.sum(dhg * hn, axis=0, keepdims=True))
        accs = (dwg_ref, dwp_ref, dpn_ref, dgn_ref)

        @pl.when(pl.program_id(0) == 0)
        def _():
            for acc, part in zip(accs, parts):
                acc[...] = part

        @pl.when(pl.program_id(0) > 0)
        def _():
            for acc, part in zip(accs, parts):
                acc[...] += part

    row = pl.BlockSpec((tr, d), lambda i: (i, 0))
    vec = pl.BlockSpec((1, d), lambda i: (0, 0))
    mat = pl.BlockSpec((d, d), lambda i: (0, 0))
    small = pl.BlockSpec((e, d), lambda i: (0, 0))
    return pl.pallas_call(
        body,
        grid=(s // tr,),
        in_specs=[row, pl.BlockSpec((tr, e), lambda i: (i, 0)), row, row, row, row, mat, vec, vec],
        out_specs=[row, mat, small, vec, vec],
        out_shape=[jax.ShapeDtypeStruct((s, d), F32), jax.ShapeDtypeStruct((d, d), F32),
                   jax.ShapeDtypeStruct((e, d), F32), jax.ShapeDtypeStruct((1, d), F32),
                   jax.ShapeDtypeStruct((1, d), F32)],
        compiler_params=_params(("arbitrary",)),
        name="ple_bwd",
    )(dh, p, pe, gl, hg, h, w_gate, gate_norm, ple_norm)


def _loss_head(h, w, target):
    s, d = h.shape
    tr = _pick(s, (512, 256, 128))

    def body(h_ref, w_ref, t_ref, l_ref, dh_ref, dw_ref):
        hv = h_ref[...]
        r = lax.rsqrt(jnp.mean(hv * hv, axis=-1, keepdims=True) + RMS_EPS)
        hn = hv * r
        diff = hn * w_ref[...] - t_ref[...]
        lpart = jnp.zeros((8, LANES), F32) + (0.5 / d) * jnp.sum(jnp.sum(diff * diff, axis=1, keepdims=True), axis=0, keepdims=True)
        dy = diff * (1.0 / d)
        dxh = dy * w_ref[...]
        dh_ref[...] = r * (dxh - hn * jnp.mean(dxh * hn, axis=-1, keepdims=True))
        part = jnp.sum(dy * hn, axis=0, keepdims=True)

        @pl.when(pl.program_id(0) == 0)
        def _():
            l_ref[...] = lpart
            dw_ref[...] = part

        @pl.when(pl.program_id(0) > 0)
        def _():
            l_ref[...] += lpart
            dw_ref[...] += part

    row = pl.BlockSpec((tr, d), lambda i: (i, 0))
    vec = pl.BlockSpec((1, d), lambda i: (0, 0))
    return pl.pallas_call(
        body,
        grid=(s // tr,),
        in_specs=[row, vec, row],
        out_specs=[pl.BlockSpec((8, LANES), lambda i: (0, 0)), row, vec],
        out_shape=[jax.ShapeDtypeStruct((8, LANES), F32), jax.ShapeDtypeStruct((s, d), F32),
                   jax.ShapeDtypeStruct((1, d), F32)],
        compiler_params=_params(("arbitrary",)),
        name="loss_head",
    )(h, w, target)


PER_LAYER = ("norm_mix", "norm_ffn", "ffn_w_gu", "ffn_w_down", "ple_w_proj", "ple_norm", "ple_gate_norm", "ple_w_gate")


def _pad_lanes(v):
    return jnp.pad(v.astype(F32), (0, LANES - v.shape[0]))[None, :]


def _kernel_layouts(full):
    w = {}
    for k in ("norm_mix", "norm_ffn", "ple_norm", "ple_gate_norm", "ssd_conv_b", "ssd_norm_w", "gmlp_b_in", "gmlp_ln_w",
              "gmlp_ln_b", "gmlp_w_s"):
        w[k] = [full[k][i].astype(F32) for i in range(full[k].shape[0])]
    w["final_norm"] = full["final_norm"].astype(F32)
    n_ssd = full["ssd_w_out"].shape[0]
    if "ssd_w_in" in full:
        w["ssd_w_zx"] = [full["ssd_w_in"][j][:, :SSD_ZX].astype(BF16) for j in range(n_ssd)]
        w["ssd_w_dt"] = [jnp.pad(full["ssd_w_in"][j][:, SSD_ZX:].astype(BF16), ((0, 0), (0, LANES - SSD_HEADS)))
                         for j in range(n_ssd)]
        w["ffn_w_gu"] = [jnp.concatenate([full["ffn_w_gate"][i], full["ffn_w_up"][i]], axis=1).astype(BF16)
                         for i in range(DEPTH)]
    else:
        for k in ("ssd_w_zx", "ssd_w_dt", "ffn_w_gu"):
            w[k] = full[k]
    w["ssd_conv_w"] = [jnp.pad(full["ssd_conv_w"][j].astype(F32), ((0, 8 - CONV_K), (0, 0))) for j in range(n_ssd)]
    for k in ("ssd_dt_bias", "ssd_a_log", "ssd_d"):
        w[k] = [_pad_lanes(full[k][j]) for j in range(n_ssd)]
    w["ssd_w_out"] = [full["ssd_w_out"][j].astype(BF16) for j in range(n_ssd)]
    n_g = full["gmlp_w_in"].shape[0]
    w["gmlp_w_in"] = [full["gmlp_w_in"][j].astype(BF16) for j in range(n_g)]
    w["gmlp_w_out"] = [full["gmlp_w_out"][j].astype(BF16) for j in range(n_g)]
    w["gmlp_b_st"] = [jnp.pad(full["gmlp_b_s"][j].astype(F32).T, ((0, 0), (0, LANES - GMLP_GROUPS))) for j in range(n_g)]
    w["ffn_w_down"] =[full["ffn_w_down"][i].astype(BF16) for i in range(DEPTH)]
    w["ple_w_proj"] = [full["ple_w_proj"][i].astype(BF16) for i in range(DEPTH)]
    w["ple_w_gate"] = [full["ple_w_gate"][i].astype(BF16) for i in range(DEPTH)]
    return w


MATRICES = ("ssd_w_out", "gmlp_w_in", "gmlp_w_out", "ffn_w_down", "ple_w_proj", "ple_w_gate")


def _reference_layouts(g, wide=True):
    out = {}
    for k in ("norm_mix", "norm_ffn", "ple_norm", "ple_gate_norm", "ssd_conv_b", "ssd_norm_w", "gmlp_b_in", "gmlp_ln_w",
              "gmlp_ln_b", "gmlp_w_s", "ssd_conv_w", "ssd_dt_bias", "ssd_a_log", "ssd_d") + (MATRICES if wide else ()):
        out[k] = jnp.stack(g[k])
    out["final_norm"] = g["final_norm"]
    out["gmlp_b_s"] = jnp.stack([b[:, :GMLP_GROUPS].T for b in g["gmlp_b_st"]])
    if wide:
        out["ssd_w_in"] = jnp.stack([jnp.concatenate([zx, dt[:, :SSD_HEADS]], axis=1)
                                     for zx, dt in zip(g["ssd_w_zx"], g["ssd_w_dt"])])
        out["ffn_w_gate"] = jnp.stack([gu[:, :FFN_DIM] for gu in g["ffn_w_gu"]])
        out["ffn_w_up"] = jnp.stack([gu[:, FFN_DIM:] for gu in g["ffn_w_gu"]])
    return out


RELAYOUT_ROWS = 128
SSD_SHARD = SSD_IN_DIM // N_DEV
FFN_SHARD = FFN_DIM // N_DEV


def _to_bf16(x):
    nl, rows, n = x.shape

    def body(x_ref, o_ref):
        o_ref[...] = x_ref[...].astype(o_ref.dtype)

    blk = pl.BlockSpec((1, rows, n), lambda i: (i, 0, 0))
    return pl.pallas_call(
        body, grid=(nl,), in_specs=[blk], out_specs=blk, out_shape=jax.ShapeDtypeStruct(x.shape, BF16),
        compiler_params=_params(("parallel",)), name="to_bf16",
    )(x)


def _cat_ssd_in(gathered):
    _, nl, rows, n = gathered.shape
    tr = RELAYOUT_ROWS

    def body(g_ref, *o_refs):
        for j in range(nl):
            full = jnp.concatenate([g_ref[d, j] for d in range(N_DEV)], axis=1)
            o_refs[2 * j][...] = full[:, :SSD_ZX]
            o_refs[2 * j + 1][...] = jnp.concatenate(
                [full[:, SSD_ZX:], jnp.zeros((tr, LANES - SSD_HEADS), full.dtype)], axis=1)

    outs = pl.pallas_call(
        body, grid=(rows // tr,),
        in_specs=[pl.BlockSpec((N_DEV, nl, tr, n), lambda i: (0, 0, i, 0))],
        out_specs=[pl.BlockSpec((tr, SSD_ZX), lambda i: (i, 0)), pl.BlockSpec((tr, LANES), lambda i: (i, 0))] * nl,
        out_shape=[jax.ShapeDtypeStruct((rows, SSD_ZX), BF16), jax.ShapeDtypeStruct((rows, LANES), BF16)] * nl,
        compiler_params=_params(("parallel",)), name="cat_ssd_in",
    )(gathered)
    return [outs[2 * j] for j in range(nl)], [outs[2 * j + 1] for j in range(nl)]


def _split_ssd_in(dzx_list, ddt_list):
    nl = len(dzx_list)
    rows = dzx_list[0].shape[0]
    tr = RELAYOUT_ROWS

    def body(*refs):
        o_ref = refs[2 * nl]
        for j in range(nl):
            full = jnp.concatenate([refs[2 * j][...], refs[2 * j + 1][:, 0:SSD_HEADS]], axis=1)
            for d in range(N_DEV):
                o_ref[d, j] = full[:, d * SSD_SHARD:(d + 1) * SSD_SHARD].astype(o_ref.dtype)

    ins = []
    for j in range(nl):
        ins += [dzx_list[j], ddt_list[j]]
    return pl.pallas_call(
        body, grid=(rows // tr,),
        in_specs=[pl.BlockSpec((tr, SSD_ZX), lambda i: (i, 0)), pl.BlockSpec((tr, LANES), lambda i: (i, 0))] * nl,
        out_specs=pl.BlockSpec((N_DEV, nl, tr, SSD_SHARD), lambda i: (0, 0, i, 0)),
        out_shape=jax.ShapeDtypeStruct((N_DEV, nl, rows, SSD_SHARD), BF16),
        compiler_params=_params(("parallel",)), name="split_ssd_in",
    )(*ins)


def _cat_ffn(g_gate, g_up):
    _, nl, rows, n = g_gate.shape
    tr = RELAYOUT_ROWS

    def body(gg_ref, gu_ref, *o_refs):
        for i in range(nl):
            o_refs[i][...] = jnp.concatenate([gg_ref[d, i] for d in range(N_DEV)] + [gu_ref[d, i] for d in range(N_DEV)],
                                             axis=1)

    blk = pl.BlockSpec((N_DEV, nl, tr, n), lambda i: (0, 0, i, 0))
    outs = pl.pallas_call(
        body, grid=(rows // tr,), in_specs=[blk, blk],
        out_specs=[pl.BlockSpec((tr, 2 * FFN_DIM), lambda i: (i, 0))] * nl,
        out_shape=[jax.ShapeDtypeStruct((rows, 2 * FFN_DIM), BF16)] * nl,
        compiler_params=_params(("parallel",)), name="cat_ffn",
    )(g_gate, g_up)
    return list(outs)


def _split_ffn(dgu_list):
    nl = len(dgu_list)
    rows = dgu_list[0].shape[0]
    tr = RELAYOUT_ROWS

    def body(*refs):
        og_ref, ou_ref = refs[nl], refs[nl + 1]
        for i in range(nl):
            full = refs[i][...]
            for d in range(N_DEV):
                og_ref[d, i] = full[:, d * FFN_SHARD:(d + 1) * FFN_SHARD].astype(og_ref.dtype)
                ou_ref[d, i] = full[:, FFN_DIM + d * FFN_SHARD:FFN_DIM + (d + 1) * FFN_SHARD].astype(ou_ref.dtype)

    blk = pl.BlockSpec((N_DEV, nl, tr, FFN_SHARD), lambda i: (0, 0, i, 0))
    sds = jax.ShapeDtypeStruct((N_DEV, nl, rows, FFN_SHARD), BF16)
    return pl.pallas_call(
        body, grid=(rows // tr,),
        in_specs=[pl.BlockSpec((tr, 2 * FFN_DIM), lambda i: (i, 0))] * nl,
        out_specs=[blk, blk], out_shape=[sds, sds],
        compiler_params=_params(("parallel",)), name="split_ffn",
    )(*dgu_list)


def _local_step(x, p, target, w):
    saved = []
    h = x
    hn = _rms_fwd(h, w["norm_mix"][0][None, :])
    for i in range(DEPTH):
        j = i // 2
        sv = {"h0": h}
        sv["hn"] = hn
        if i % 2 == 0:
            zx = _mm(hn, w["ssd_w_zx"][j], "nn", BF16)
            dtp = _mm(hn, w["ssd_w_dt"][j], "nn", F32)
            xbc = _ssd_conv_fwd(zx, w["ssd_conv_w"][j], w["ssd_conv_b"][j][None, :])
            y, prev = _ssd_scan_fwd(xbc, dtp, w["ssd_dt_bias"][j], w["ssd_a_log"][j], w["ssd_d"][j])
            gn = _ssd_gate_fwd(y, zx, w["ssd_norm_w"][j][None, :])
            h, un = _mm(gn, w["ssd_w_out"][j], "nn", F32, add=h, norm=w["norm_ffn"][i][None, :])
            sv.update(zx=zx, dtp=dtp, xbc=xbc, y=y, prev=prev, gn=gn)
        else:
            pre = _mm(hn, w["gmlp_w_in"][j], "nn", BF16)
            u, v = _gmlp_act_fwd(pre, w["gmlp_b_in"][j][None, :], w["gmlp_ln_w"][j][None, :], w["gmlp_ln_b"][j][None, :])
            gated = _gmlp_mix_fwd(u, v, w["gmlp_w_s"][j], w["gmlp_b_st"][j])
            h, un = _mm(gated, w["gmlp_w_out"][j], "nn", F32, add=h, norm=w["norm_ffn"][i][None, :])
            sv.update(pre=pre, u=u, v=v, gated=gated)
        sv["h1"] = h
        gate, up, hid = _ffn_up(un, w["ffn_w_gu"][i])
        h = _mm(hid, w["ffn_w_down"][i], "nn", F32, add=h)
        sv.update(un=un, gate=gate, up=up, hid=hid, h2=h)
        next_norm = w["norm_mix"][i + 1] if i + 1 < DEPTH else w["final_norm"]
        h, pe, hg, gl, hn = _ple_fwd(p[i], h, w["ple_w_proj"][i], w["ple_w_gate"][i], w["ple_gate_norm"][i][None, :],
                                     w["ple_norm"][i][None, :], next_norm[None, :])
        sv.update(pe=pe, hg=hg, gl=gl)
        saved.append(sv)

    lpart, dh, d_final = _loss_head(h, w["final_norm"][None, :], target)
    g = {k: [None] * (DEPTH if k in PER_LAYER else DEPTH // 2) for k in w if k != "final_norm"}
    g["final_norm"] = d_final[0]

    for i in reversed(range(DEPTH)):
        j = i // 2
        sv = saved[i]
        dh, g["ple_w_gate"][i], g["ple_w_proj"][i], d_ple_norm, d_gate_norm = _ple_bwd(
            dh, p[i], sv["pe"], sv["gl"], sv["hg"], sv["h2"], w["ple_w_gate"][i], w["ple_gate_norm"][i][None, :],
            w["ple_norm"][i][None, :])
        g["ple_norm"][i] = d_ple_norm[0]
        g["ple_gate_norm"][i] = d_gate_norm[0]
        g["ffn_w_down"][i] = _mm(sv["hid"], dh, "tn", F32)
        dgu = _ffn_down_bwd(dh, w["ffn_w_down"][i], sv["gate"], sv["up"])
        g["ffn_w_gu"][i] = _mm(sv["un"], dgu, "tn", F32)
        dun = _mm(dgu, w["ffn_w_gu"][i], "nt", BF16)
        dh, d_norm_ffn = _rms_bwd(dun, sv["h1"], w["norm_ffn"][i][None, :], dh)
        g["norm_ffn"][i] = d_norm_ffn[0]
        if i % 2 == 0:
            dgn = _mm(dh, w["ssd_w_out"][j], "nt", BF16)
            g["ssd_w_out"][j] = _mm(sv["gn"], dh, "tn", F32)
            dy, dzx, d_norm_w = _ssd_gate_bwd(dgn, sv["y"], sv["zx"], w["ssd_norm_w"][j][None, :])
            g["ssd_norm_w"][j] = d_norm_w[0]
            dxbc, ddtp, d_bias, d_alog, d_d = _ssd_scan_bwd(sv["xbc"], sv["dtp"], sv["prev"], dy, w["ssd_dt_bias"][j],
                                                            w["ssd_a_log"][j], w["ssd_d"][j])
            g["ssd_dt_bias"][j] = d_bias[0, :SSD_HEADS]
            g["ssd_a_log"][j] = d_alog[0, :SSD_HEADS]
            g["ssd_d"][j] = d_d[0, :SSD_HEADS]
            dzx, d_conv_w, d_conv_b = _ssd_conv_bwd(sv["zx"], dxbc, w["ssd_conv_w"][j], w["ssd_conv_b"][j][None, :], dzx)
            g["ssd_conv_w"][j] = d_conv_w[:CONV_K]
            g["ssd_conv_b"][j] = d_conv_b[0]
            g["ssd_w_zx"][j] = _mm(sv["hn"], dzx, "tn", F32)
            g["ssd_w_dt"][j] = _mm(sv["hn"], ddtp, "tn", F32)
            dhn = _mm(ddtp, w["ssd_w_dt"][j], "nt", F32)
            dhn = _mm(dzx, w["ssd_w_zx"][j], "nt", BF16, add=dhn)
        else:
            dgated = _mm(dh, w["gmlp_w_out"][j], "nt", BF16)
            g["gmlp_w_out"][j] = _mm(sv["gated"], dh, "tn", F32)
            du, dv, d_ws, d_bst = _gmlp_mix_bwd(dgated, sv["u"], sv["v"], w["gmlp_w_s"][j], w["gmlp_b_st"][j])
            g["gmlp_w_s"][j] = d_ws
            g["gmlp_b_st"][j] = d_bst
            dpre, d_bin, d_lnw, d_lnb = _gmlp_act_bwd(sv["pre"], w["gmlp_b_in"][j][None, :], w["gmlp_ln_w"][j][None, :],
                                                     du, dv)
            g["gmlp_b_in"][j] = d_bin[0]
            g["gmlp_ln_w"][j] = d_lnw[0]
            g["gmlp_ln_b"][j] = d_lnb[0]
            g["gmlp_w_in"][j] = _mm(sv["hn"], dpre, "tn", F32)
            dhn = _mm(dpre, w["gmlp_w_in"][j], "nt", BF16)
        dh, d_norm_mix = _rms_bwd(dhn, sv["h0"], w["norm_mix"][i][None, :], dh)
        g["norm_mix"][i] = d_norm_mix[0]
    return lpart[0, 0], dh, g


PACK_COLS = 1024
ANY = pl.BlockSpec(memory_space=pl.ANY)


def _mesh_pos():
    return lax.axis_index("x"), lax.axis_index("y"), lax.axis_index("c")


def _all_gather(xs_list, name):
    n = len(xs_list)

    def body(*refs):
        x_refs, out_refs = refs[:n], refs[n:2 * n]
        send_sems, recv_sems, local_sems = refs[2 * n:]
        x, y, c = _mesh_pos()
        me, sibling = (x, y, c), (x, y, 1 - c)
        chips = [(1 - x, y), (x, 1 - y), (1 - x, 1 - y)]

        def copy(a, k, block, to, from_input=False):
            px, py, pc = block
            dst = out_refs[a].at[4 * px + 2 * py + pc]
            return pltpu.make_async_remote_copy(
                src_ref=x_refs[a] if from_input else dst, dst_ref=dst,
                send_sem=send_sems.at[7 * a + k], recv_sem=recv_sems.at[7 * a + k], device_id=to,
                device_id_type=MESH_ID)

        mine = [pltpu.make_async_copy(x_refs[a], out_refs[a].at[4 * x + 2 * y + c], local_sems.at[a]) for a in range(n)]
        for cp in mine:
            cp.start()
        first = []
        for a in range(n):
            first += [copy(a, 1 + j, me, (*chip, c), from_input=True) for j, chip in enumerate(chips)]
            first.append(copy(a, 0, me, sibling, from_input=True))
        for cp in first:
            cp.start()
        passed = []
        for a in range(n):
            for j, chip in enumerate(chips):
                copy(a, 1 + j, (*chip, c), me).wait_recv()
                fwd = copy(a, 4 + j, (*chip, c), sibling)
                fwd.start()
                passed.append(fwd)
        for a in range(n):
            copy(a, 0, sibling, me).wait_recv()
            for j, chip in enumerate(chips):
                copy(a, 4 + j, (*chip, 1 - c), me).wait_recv()
        for cp in first + passed:
            cp.wait_send()
        for cp in mine:
            cp.wait()

    outs = pl.pallas_call(
        body,
        out_shape=[jax.ShapeDtypeStruct((N_DEV,) + t.shape, t.dtype) for t in xs_list],
        in_specs=[ANY] * n,
        out_specs=[ANY] * n,
        scratch_shapes=[pltpu.SemaphoreType.DMA((7 * n,)), pltpu.SemaphoreType.DMA((7 * n,)),
                        pltpu.SemaphoreType.DMA((n,))],
        name=name,
    )(*xs_list)
    return list(outs)


def _exchange_sibling(send_list):
    n = len(send_list)

    def body(*refs):
        s_refs, land_refs = refs[:n], refs[n:2 * n]
        send_sems, recv_sems = refs[2 * n:]
        x, y, c = _mesh_pos()
        cps = [pltpu.make_async_remote_copy(src_ref=s_refs[a], dst_ref=land_refs[a], send_sem=send_sems.at[a],
                                            recv_sem=recv_sems.at[a], device_id=(x, y, 1 - c), device_id_type=MESH_ID)
               for a in range(n)]
        for cp in cps:
            cp.start()
        for cp in cps:
            cp.wait()

    outs = pl.pallas_call(
        body,
        out_shape=[jax.ShapeDtypeStruct(t.shape, t.dtype) for t in send_list],
        in_specs=[ANY] * n,
        out_specs=[ANY] * n,
        scratch_shapes=[pltpu.SemaphoreType.DMA((n,)), pltpu.SemaphoreType.DMA((n,))],
        name="rs_exchange_sibling",
    )(*send_list)
    return list(outs)


def _exchange_chips(partial_list):
    n = len(partial_list)

    def body(*refs):
        p_refs, land_refs = refs[:n], refs[n:2 * n]
        send_sems, recv_sems = refs[2 * n:]
        x, y, c = _mesh_pos()
        chips = [(1 - x, y), (x, 1 - y), (1 - x, 1 - y)]
        cps = [pltpu.make_async_remote_copy(src_ref=p_refs[a].at[2 * cx + cy], dst_ref=land_refs[a].at[j],
                                            send_sem=send_sems.at[3 * a + j], recv_sem=recv_sems.at[3 * a + j],
                                            device_id=(cx, cy, c), device_id_type=MESH_ID)
               for a in range(n) for j, (cx, cy) in enumerate(chips)]
        for cp in cps:
            cp.start()
        for cp in cps:
            cp.wait()

    outs = pl.pallas_call(
        body,
        out_shape=[jax.ShapeDtypeStruct((3,) + t.shape[1:], t.dtype) for t in partial_list],
        in_specs=[ANY] * n,
        out_specs=[ANY] * n,
        scratch_shapes=[pltpu.SemaphoreType.DMA((3 * n,)), pltpu.SemaphoreType.DMA((3 * n,))],
        name="rs_exchange_chips",
    )(*partial_list)
    return list(outs)


def _sum_pairs(a, b):
    shape = a.shape
    a = a.reshape(shape[0], -1, shape[-1])
    b = b.reshape(a.shape)
    n, r, cdim = a.shape
    tr = _pick(r, (1024, 512, 256, 128, 64))

    def body(a_ref, b_ref, o_ref):
        o_ref[...] = (a_ref[...].astype(F32) + b_ref[...].astype(F32)).astype(o_ref.dtype)

    blk = pl.BlockSpec((1, tr, cdim), lambda i, j: (i, j, 0))
    return pl.pallas_call(
        body, grid=(n, r // tr), in_specs=[blk, blk], out_specs=blk,
        out_shape=jax.ShapeDtypeStruct(a.shape, a.dtype),
        compiler_params=_params(("parallel", "parallel")), name="rs_sum_pairs",
    )(a, b).reshape(shape)


def _sum_final(own, land):
    shape = own.shape
    own = own.reshape(-1, shape[-1])
    land = land.reshape((3,) + own.shape)
    r, cdim = own.shape
    tr = _pick(r, (1024, 512, 256, 128, 64))

    def body(o_ref, l_ref, out_ref):
        acc = o_ref[...].astype(F32)
        for j in range(3):
            acc = acc + l_ref[j].astype(F32)
        out_ref[...] = acc

    return pl.pallas_call(
        body, grid=(r // tr,),
        in_specs=[pl.BlockSpec((tr, cdim), lambda i: (i, 0)), pl.BlockSpec((3, tr, cdim), lambda i: (0, i, 0))],
        out_specs=pl.BlockSpec((tr, cdim), lambda i: (i, 0)),
        out_shape=jax.ShapeDtypeStruct((r, cdim), F32),
        compiler_params=_params(("parallel",)), name="rs_sum_final",
    )(own, land).reshape(shape)


def _sum_devices(gathered):
    n, r, cdim = gathered.shape
    tr = _pick(r, (64, 32, 16, 8))

    def body(g_ref, out_ref):
        acc = g_ref[0].astype(F32)
        for q in range(1, n):
            acc = acc + g_ref[q].astype(F32)
        out_ref[...] = acc

    return pl.pallas_call(
        body, grid=(r // tr,),
        in_specs=[pl.BlockSpec((n, tr, cdim), lambda i: (0, i, 0))],
        out_specs=pl.BlockSpec((tr, cdim), lambda i: (i, 0)),
        out_shape=jax.ShapeDtypeStruct((r, cdim), F32),
        compiler_params=_params(("parallel",)), name="sum_devices",
    )(gathered)


def _adamw(w, g, m, v):
    shape = w.shape
    cols = shape[-1]
    rows = w.size // cols
    tr = _pick(rows, (512, 256, 128, 64, 32, 16, 8))
    c1 = 1.0 - ADAM_B1 ** ADAM_STEP
    c2 = 1.0 - ADAM_B2 ** ADAM_STEP

    def body(w_ref, g_ref, m_ref, v_ref, d_ref, nm_ref, nv_ref):
        gv = g_ref[...]
        m2 = ADAM_B1 * m_ref[...] + (1.0 - ADAM_B1) * gv
        v2 = ADAM_B2 * v_ref[...] + (1.0 - ADAM_B2) * (gv * gv)
        d_ref[...] = -ADAM_LR * ((m2 / c1) / (jnp.sqrt(v2 / c2) + ADAM_EPS) + ADAM_WD * w_ref[...])
        nm_ref[...] = m2
        nv_ref[...] = v2

    blk = pl.BlockSpec((tr, cols), lambda i: (i, 0))
    sds = jax.ShapeDtypeStruct((rows, cols), F32)
    outs = pl.pallas_call(
        body, grid=(rows // tr,), in_specs=[blk] * 4, out_specs=[blk] * 3, out_shape=[sds] * 3,
        compiler_params=_params(("parallel",)), name=f"adamw_{rows}x{cols}",
    )(*(t.reshape(rows, cols) for t in (w, g, m, v)))
    return tuple(o.reshape(shape) for o in outs)


WEIGHTS = ("norm_mix", "norm_ffn", "ssd_w_in", "ssd_conv_w", "ssd_conv_b", "ssd_dt_bias", "ssd_a_log", "ssd_d",
           "ssd_norm_w", "ssd_w_out", "gmlp_w_in", "gmlp_b_in", "gmlp_ln_w", "gmlp_ln_b", "gmlp_w_s", "gmlp_b_s",
           "gmlp_w_out", "ffn_w_gate", "ffn_w_up", "ffn_w_down", "ple_w_proj", "ple_norm", "ple_gate_norm",
           "ple_w_gate", "final_norm")
ARG_NAMES = ("x", "p") + WEIGHTS + ("loss_target",) + tuple("m_" + n for n in WEIGHTS) + tuple("v_" + n for n in WEIGHTS)
SHARD_AXIS = {"ssd_w_in": 2, "ssd_conv_w": 2, "ssd_w_out": 1, "gmlp_w_in": 2, "gmlp_b_in": 1, "gmlp_ln_w": 1,
              "gmlp_ln_b": 1, "gmlp_w_out": 1, "ffn_w_gate": 2, "ffn_w_up": 2, "ffn_w_down": 1, "ple_w_proj": 2,
              "ple_w_gate": 1}
GATHER_BF16 = ("ssd_w_in", "ssd_w_out", "gmlp_w_in", "gmlp_w_out", "ffn_w_gate", "ffn_w_up", "ffn_w_down",
               "ple_w_proj", "ple_w_gate")
GATHER_F32 = ("ssd_conv_w", "gmlp_b_in", "gmlp_ln_w", "gmlp_ln_b")
SHARDED = GATHER_BF16 + GATHER_F32
WIDE = ("ssd_w_in", "ffn_w_gate", "ffn_w_up")
REPLICATED = tuple(n for n in WEIGHTS if n not in SHARD_AXIS)


def _pack(arrs, dtype, row_mult, lead=0):
    flat = jnp.concatenate([t.reshape(t.shape[:lead] + (-1,)).astype(dtype) for t in arrs], axis=lead)
    n = flat.shape[-1]
    unit = row_mult * PACK_COLS
    total = -(-n // unit) * unit
    flat = jnp.pad(flat, [(0, 0)] * lead + [(0, total - n)])
    return flat.reshape(flat.shape[:lead] + (total // PACK_COLS, PACK_COLS))


def _unpack(buf, names, shapes, lead=0):
    flat = buf.reshape(buf.shape[:lead] + (-1,))
    out, off = {}, 0
    for n in names:
        size = math.prod(shapes[n])
        out[n] = lax.slice_in_dim(flat, off, off + size, axis=lead).reshape(buf.shape[:lead] + tuple(shapes[n]))
        off += size
    return out


ROW_PACKED = ((1024, ("ssd_w_out", "gmlp_w_out", "ffn_w_down", "ple_w_gate")), (512, ("gmlp_w_in",)),
              (128, ("ple_w_proj",)))
ROW_PACK_MULT = 1024


def _pack_rows(arrs, width, lead=0):
    parts = [t.reshape(t.shape[:lead] + (-1, width)).astype(BF16) for t in arrs]
    rows = sum(t.shape[lead] for t in parts)
    pad = -rows % ROW_PACK_MULT
    if pad:
        parts.append(jnp.zeros(parts[0].shape[:lead] + (pad, width), BF16))
    return jnp.concatenate(parts, axis=lead)


def _unpack_rows(buf, names, shapes, lead=0):
    width = buf.shape[-1]
    out, off = {}, 0
    for n in names:
        rows = math.prod(shapes[n]) // width
        out[n] = lax.slice_in_dim(buf, off, off + rows, axis=lead).reshape(buf.shape[:lead] + tuple(shapes[n]))
        off += rows
    return out


def _merge_shards(seg, ax):
    t = jnp.moveaxis(seg, 0, ax)
    return t.reshape(t.shape[:ax] + (t.shape[ax] * t.shape[ax + 1],) + t.shape[ax + 2:])


def _split_for_cores(gfull, ax, c):
    shp = gfull.shape
    t = gfull.reshape(shp[:ax] + (2, 2, 2, shp[ax] // N_DEV) + shp[ax + 1:])

    def take(core):
        u = lax.dynamic_index_in_dim(t, core, axis=ax + 2, keepdims=False)
        u = jnp.moveaxis(u, (ax, ax + 1), (0, 1))
        return u.reshape((4,) + u.shape[2:])

    return take(c), take(1 - c)


def kernel(x, p, norm_mix, norm_ffn, ssd_w_in, ssd_conv_w, ssd_conv_b, ssd_dt_bias, ssd_a_log, ssd_d,
           ssd_norm_w, ssd_w_out, gmlp_w_in, gmlp_b_in, gmlp_ln_w, gmlp_ln_b, gmlp_w_s, gmlp_b_s,
           gmlp_w_out, ffn_w_gate, ffn_w_up, ffn_w_down, ple_w_proj, ple_norm, ple_gate_norm, ple_w_gate,
           final_norm, loss_target, m_norm_mix, m_norm_ffn, m_ssd_w_in, m_ssd_conv_w, m_ssd_conv_b,
           m_ssd_dt_bias, m_ssd_a_log, m_ssd_d, m_ssd_norm_w, m_ssd_w_out, m_gmlp_w_in, m_gmlp_b_in,
           m_gmlp_ln_w, m_gmlp_ln_b, m_gmlp_w_s, m_gmlp_b_s, m_gmlp_w_out, m_ffn_w_gate, m_ffn_w_up,
           m_ffn_w_down, m_ple_w_proj, m_ple_norm, m_ple_gate_norm, m_ple_w_gate, m_final_norm, v_norm_mix,
           v_norm_ffn, v_ssd_w_in, v_ssd_conv_w, v_ssd_conv_b, v_ssd_dt_bias, v_ssd_a_log, v_ssd_d,
           v_ssd_norm_w, v_ssd_w_out, v_gmlp_w_in, v_gmlp_b_in, v_gmlp_ln_w, v_gmlp_ln_b, v_gmlp_w_s,
           v_gmlp_b_s, v_gmlp_w_out, v_ffn_w_gate, v_ffn_w_up, v_ffn_w_down, v_ple_w_proj, v_ple_norm,
           v_ple_gate_norm, v_ple_w_gate, v_final_norm):
    given = locals()
    a = {n: given[n] for n in ARG_NAMES}
    mx, my, c = _mesh_pos()
    xs = a["x"][0]
    ps = a["p"][:, 0]
    target = a["loss_target"][0]
    shard_shapes = {n: a[n].shape for n in WEIGHTS}

    full = {n: a[n] for n in REPLICATED}
    row_packs = [_pack_rows([a[n] for n in names], wd) for wd, names in ROW_PACKED]
    got = _all_gather(row_packs + [_pack([a[n] for n in GATHER_F32], F32, 8)] + [_to_bf16(a[n]) for n in WIDE],
                      "ag_weights")
    for (wd, names), buf in zip(ROW_PACKED, got):
        for n, seg in _unpack_rows(buf, names, shard_shapes, lead=1).items():
            full[n] = _merge_shards(seg, SHARD_AXIS[n])
    k0 = len(ROW_PACKED)
    for n, seg in _unpack(got[k0], GATHER_F32, shard_shapes, lead=1).items():
        full[n] = _merge_shards(seg, SHARD_AXIS[n])
    full["ssd_w_zx"], full["ssd_w_dt"] = _cat_ssd_in(got[k0 + 1])
    full["ffn_w_gu"] = _cat_ffn(got[k0 + 2], got[k0 + 3])

    lpart, dx, g = _local_step(xs, ps, target, _kernel_layouts(full))
    gfull = _reference_layouts(g, wide=False)
    loss = lax.psum(lpart, ("x", "y", "c"))

    def by_core(t):
        u = t.reshape((4, 2) + t.shape[1:])
        return (lax.dynamic_index_in_dim(u, c, axis=1, keepdims=False),
                lax.dynamic_index_in_dim(u, 1 - c, axis=1, keepdims=False))

    def layer_halves(gl, ax):
        if ax == 0:
            t = gl.reshape(4, 2, -1, gl.shape[-1])
            return tuple(lax.dynamic_index_in_dim(t, cc, axis=1, keepdims=False) for cc in (c, 1 - c))
        t = gl.reshape(gl.shape[0], 4, 2, -1)
        return tuple(jnp.moveaxis(lax.dynamic_index_in_dim(t, cc, axis=2, keepdims=False), 1, 0) for cc in (c, 1 - c))

    pairs = []
    for wd, names in ROW_PACKED:
        hs = [layer_halves(gl, SHARD_AXIS[n] - 1) for n in names for gl in g[n]]
        pairs.append(tuple(_pack_rows([h[i] for h in hs], wd, lead=1) for i in (0, 1)))
    halves = [_split_for_cores(gfull[n], SHARD_AXIS[n], c) for n in GATHER_F32]
    pairs.append((_pack([h[0] for h in halves], BF16, 16, lead=1), _pack([h[1] for h in halves], BF16, 16, lead=1)))
    pairs += [by_core(t) for t in (_split_ssd_in(g["ssd_w_zx"], g["ssd_w_dt"]),) + tuple(_split_ffn(g["ffn_w_gu"]))]
    landed = _exchange_sibling([s for _, s in pairs])
    partials = [_sum_pairs(k, l) for (k, _), l in zip(pairs, landed)]
    landed = _exchange_chips(partials)
    sums = [_sum_final(lax.dynamic_index_in_dim(t, 2 * mx + my, axis=0, keepdims=False), l)
            for t, l in zip(partials, landed)]
    gshard = {}
    for (wd, names), buf in zip(ROW_PACKED, sums):
        gshard.update(_unpack_rows(buf, names, shard_shapes))
    gshard.update(_unpack(sums[k0], GATHER_F32, shard_shapes))
    gshard.update(zip(WIDE, sums[k0 + 1:]))
    rep = _all_gather([_pack([gfull[n] for n in REPLICATED], BF16, 64)], "ag_replicated_grads")[0]
    grep = _unpack(_sum_devices(rep), REPLICATED, shard_shapes)
    grads = {**gshard, **grep}

    upd = {n: _adamw(a[n], grads[n], a["m_" + n], a["v_" + n]) for n in WEIGHTS}
    return (loss, dx[None], *[grads[n] for n in WEIGHTS], *[upd[n][0] for n in WEIGHTS],
            *[upd[n][1] for n in WEIGHTS], *[upd[n][2] for n in WEIGHTS])
```

```python
import math

import jax
import jax.numpy as jnp
from jax import lax
from jax.experimental import pallas as pl
from jax.experimental.pallas import tpu as pltpu

F32 = jnp.float32
BF16 = jnp.bfloat16

N_DEV = 8
D_MODEL = 1024
DEPTH = 4
SSD_INNER = 2048
SSD_HEADS = 32
SSD_HEADDIM = 64
SSD_GROUPS = 8
SSD_STATE = 128
SSD_GROUP_W = SSD_INNER // SSD_GROUPS
SSD_CONV_DIM = SSD_INNER + 2 * SSD_GROUPS * SSD_STATE
SSD_IN_DIM = 2 * SSD_INNER + SSD_CONV_DIM - SSD_INNER + SSD_HEADS
SSD_ZX = SSD_INNER + SSD_CONV_DIM
CONV_K = 4
CHUNK = 128
GMLP_INNER = 2048
GMLP_GROUPS = 16
FFN_DIM = 2816
PLE_DIM = 256
RMS_EPS = 1e-6
LN_EPS = 1e-5
LANES = 128
VMEM_LIMIT = 56 * 1024 * 1024

ADAM_LR = 0.001
ADAM_B1 = 0.9
ADAM_B2 = 0.999
ADAM_EPS = 1e-08
ADAM_WD = 0.01
ADAM_STEP = 10

MESH_ID = pl.DeviceIdType.MESH


def _pick(n, cands):
    for c in cands:
        if c <= n and n % c == 0:
            return c
    return n


def _params(dims):
    return pltpu.CompilerParams(dimension_semantics=dims, vmem_limit_bytes=VMEM_LIMIT)


def _dot(a, b, dims=(((1,), (0,)), ((), ())), precision=None):
    return lax.dot_general(a, b, dims, precision=precision, preferred_element_type=F32)


NN = (((1,), (0,)), ((), ()))
NT = (((1,), (1,)), ((), ()))
TN = (((0,), (0,)), ((), ()))


def _sigmoid(x):
    return 1.0 / (1.0 + jnp.exp(-x))


def _dot01(a, b, dims, split, terms=3):
    v = (a, b)[split]
    ones = (a, b)[1 - split].astype(BF16)
    acc = None
    for _ in range(terms):
        piece = v.astype(BF16)
        v = v - piece.astype(F32)
        part = _dot(piece, ones, dims) if split == 0 else _dot(ones, piece, dims)
        acc = part if acc is None else acc + part
    return acc


MM_VMEM_BUDGET = 36 * 1024 * 1024


def _mm_tiles(mode, m, n, k, a_bytes, b_bytes, out_bytes, has_add):
    tm = _pick(m, (1408, 1024, 512, 256, 128))
    tn_cands = [c for c in (2816, 1024, 512, 256, 128) if c <= n and n % c == 0] or [n]
    tk_cands = [k] + [c for c in (2816, 2048, 1024, 512, 256, 128) if c < k and k % c == 0]
    for tk in tk_cands:
        for tn in tn_cands:
            blocks = tm * tk * a_bytes + tk * tn * b_bytes + tm * tn * (out_bytes + (4 if has_add else 0))
            if 2 * blocks + (tm * tn * 4 if tk < k else 0) <= MM_VMEM_BUDGET:
                return tm, tn, tk
    return tm, tn_cands[-1], tk_cands[-1]


def _mm(a, b, mode, out_dtype, add=None, norm=None):
    if mode == "nn":
        m, k = a.shape
        n = b.shape[1]
    elif mode == "nt":
        m, k = a.shape
        n = b.shape[0]
    else:
        k, m = a.shape
        n = b.shape[1]
    tm, tn, tk = _mm_tiles(mode, m, n, k, a.dtype.itemsize, b.dtype.itemsize, jnp.dtype(out_dtype).itemsize,
                           add is not None)
    nk = k // tk
    dims = {"nn": NN, "nt": NT, "tn": TN}[mode]
    assert norm is None or tn == n

    def body(*refs):
        refs = list(refs)
        a_ref, b_ref = refs[:2]
        del refs[:2]
        add_ref = refs.pop(0) if add is not None else None
        nw_ref = refs.pop(0) if norm is not None else None
        o_ref = refs.pop(0)
        n_ref = refs.pop(0) if norm is not None else None
        rest = refs
        part = _dot(a_ref[...].astype(BF16), b_ref[...].astype(BF16), dims)

        def finish(acc):
            if add_ref is not None:
                acc = acc + add_ref[...]
            o_ref[...] = acc.astype(o_ref.dtype)
            if n_ref is not None:
                r = lax.rsqrt(jnp.mean(acc * acc, axis=-1, keepdims=True) + RMS_EPS)
                n_ref[...] = (acc * r * nw_ref[...]).astype(n_ref.dtype)

        if nk == 1:
            finish(part)
        else:
            acc_ref = rest[0]
            kk = pl.program_id(2)

            @pl.when(kk == 0)
            def _():
                acc_ref[...] = part

            @pl.when(kk > 0)
            def _():
                acc_ref[...] += part

            @pl.when(kk == nk - 1)
            def _():
                finish(acc_ref[...])

    if mode == "nn":
        a_spec = pl.BlockSpec((tm, tk), lambda i, j, kk: (i, kk))
        b_spec = pl.BlockSpec((tk, tn), lambda i, j, kk: (kk, j))
    elif mode == "nt":
        a_spec = pl.BlockSpec((tm, tk), lambda i, j, kk: (i, kk))
        b_spec = pl.BlockSpec((tn, tk), lambda i, j, kk: (j, kk))
    else:
        a_spec = pl.BlockSpec((tk, tm), lambda i, j, kk: (kk, i))
        b_spec = pl.BlockSpec((tk, tn), lambda i, j, kk: (kk, j))
    o_spec = pl.BlockSpec((tm, tn), lambda i, j, kk: (i, j))
    in_specs = [a_spec, b_spec] + ([o_spec] if add is not None else [])
    args = (a, b) + ((add,) if add is not None else ())
    out_specs, out_shape = o_spec, jax.ShapeDtypeStruct((m, n), out_dtype)
    if norm is not None:
        in_specs.append(pl.BlockSpec((1, n), lambda i, j, kk: (0, 0)))
        args += (norm,)
        out_specs, out_shape = [o_spec, o_spec], [out_shape, jax.ShapeDtypeStruct((m, n), BF16)]
    return pl.pallas_call(
        body,
        grid=(m // tm, n // tn, nk),
        in_specs=in_specs,
        out_specs=out_specs,
        out_shape=out_shape,
        scratch_shapes=[pltpu.VMEM((tm, tn), F32)] if nk > 1 else [],
        compiler_params=_params(("parallel", "parallel", "arbitrary")),
        name=f"mm_{mode}_{m}x{k}x{n}",
    )(*args)


def _rms_fwd(x, w):
    s, d = x.shape
    tr = _pick(s, (512, 256, 128))

    def body(x_ref, w_ref, o_ref):
        xv = x_ref[...]
        r = lax.rsqrt(jnp.mean(xv * xv, axis=-1, keepdims=True) + RMS_EPS)
        o_ref[...] = (xv * r * w_ref[...]).astype(o_ref.dtype)

    return pl.pallas_call(
        body,
        grid=(s // tr,),
        in_specs=[pl.BlockSpec((tr, d), lambda i: (i, 0)), pl.BlockSpec((1, d), lambda i: (0, 0))],
        out_specs=pl.BlockSpec((tr, d), lambda i: (i, 0)),
        out_shape=jax.ShapeDtypeStruct((s, d), BF16),
        compiler_params=_params(("parallel",)),
        name="rms_fwd",
    )(x, w)


def _rms_bwd(dyn, x, w, add):
    s, d = x.shape
    tr = _pick(s, (512, 256, 128))

    def body(dy_ref, x_ref, w_ref, add_ref, dx_ref, dw_ref):
        xv = x_ref[...]
        dy = dy_ref[...].astype(F32)
        r = lax.rsqrt(jnp.mean(xv * xv, axis=-1, keepdims=True) + RMS_EPS)
        xn = xv * r
        dxh = dy * w_ref[...]
        dx = r * (dxh - xn * jnp.mean(dxh * xn, axis=-1, keepdims=True))
        dx_ref[...] = add_ref[...] + dx
        part = jnp.sum(dy * xn, axis=0, keepdims=True)

        @pl.when(pl.program_id(0) == 0)
        def _():
            dw_ref[...] = part

        @pl.when(pl.program_id(0) > 0)
        def _():
            dw_ref[...] += part

    row = pl.BlockSpec((tr, d), lambda i: (i, 0))
    vec = pl.BlockSpec((1, d), lambda i: (0, 0))
    return pl.pallas_call(
        body,
        grid=(s // tr,),
        in_specs=[row, row, vec, row],
        out_specs=[row, vec],
        out_shape=[jax.ShapeDtypeStruct((s, d), F32), jax.ShapeDtypeStruct((1, d), F32)],
        compiler_params=_params(("arbitrary",)),
        name="rms_bwd",
    )(dyn, x, w, add)


CONV_ROWS = 256
CONV_COLS = 256
CONV_HALO = 16


def _conv_taps(ext, w, base, rows):
    acc = w[0:1, :] * ext[base:base + rows]
    for k in range(1, CONV_K):
        acc = acc + w[k:k + 1, :] * ext[base + k:base + k + rows]
    return acc


def _ssd_conv_fwd(zx, conv_w, conv_b):
    s = zx.shape[0]
    c = SSD_CONV_DIM
    nsteps = s // CONV_ROWS
    off = SSD_INNER // CONV_COLS

    def body(x_ref, w_ref, b_ref, o_ref):
        w = w_ref[...]
        b = b_ref[...]

        def step(i, carry):
            r0 = pl.multiple_of(i * CONV_ROWS, CONV_ROWS)
            cur = x_ref[pl.ds(r0, CONV_ROWS), :].astype(F32)
            p0 = pl.multiple_of(jnp.maximum(r0 - CONV_HALO, 0), CONV_HALO)
            prev = x_ref[pl.ds(p0, CONV_HALO), :].astype(F32)
            prev = jnp.where(i == 0, 0.0, prev)
            ext = jnp.concatenate([prev, cur], axis=0)
            acc = _conv_taps(ext, w, CONV_HALO - (CONV_K - 1), CONV_ROWS) + b
            o_ref[pl.ds(r0, CONV_ROWS), :] = (acc * _sigmoid(acc)).astype(o_ref.dtype)
            return carry

        lax.fori_loop(0, nsteps, step, 0)

    return pl.pallas_call(
        body,
        grid=(c // CONV_COLS,),
        in_specs=[pl.BlockSpec((s, CONV_COLS), lambda j: (0, j + off)),
                  pl.BlockSpec((8, CONV_COLS), lambda j: (0, j)),
                  pl.BlockSpec((1, CONV_COLS), lambda j: (0, j))],
        out_specs=pl.BlockSpec((s, CONV_COLS), lambda j: (0, j)),
        out_shape=jax.ShapeDtypeStruct((s, c), BF16),
        compiler_params=_params(("parallel",)),
        name="ssd_conv_fwd",
    )(zx, conv_w, conv_b)


def _ssd_conv_bwd(zx, dxbc, conv_w, conv_b, dzx):
    s = zx.shape[0]
    c = SSD_CONV_DIM
    nsteps = s // CONV_ROWS
    off = SSD_INNER // CONV_COLS

    def body(x_ref, dy_ref, w_ref, b_ref, dzx_in_ref, dx_ref, dw_ref, db_ref, dc_ref):
        w = w_ref[...]
        b = b_ref[...]
        dc_ref[pl.ds(s, CONV_HALO), :] = jnp.zeros((CONV_HALO, CONV_COLS), F32)

        def step1(i, carry):
            dw0, dw1, dw2, dw3, dbs = carry
            r0 = pl.multiple_of(i * CONV_ROWS, CONV_ROWS)
            cur = x_ref[pl.ds(r0, CONV_ROWS), :].astype(F32)
            p0 = pl.multiple_of(jnp.maximum(r0 - CONV_HALO, 0), CONV_HALO)
            prev = x_ref[pl.ds(p0, CONV_HALO), :].astype(F32)
            prev = jnp.where(i == 0, 0.0, prev)
            ext = jnp.concatenate([prev, cur], axis=0)
            base = CONV_HALO - (CONV_K - 1)
            acc = _conv_taps(ext, w, base, CONV_ROWS) + b
            sg = _sigmoid(acc)
            dcv = dy_ref[pl.ds(r0, CONV_ROWS), :].astype(F32) * (sg * (1.0 + acc * (1.0 - sg)))
            dc_ref[pl.ds(r0, CONV_ROWS), :] = dcv
            dws = [jnp.sum(dcv * ext[base + k:base + k + CONV_ROWS], axis=0, keepdims=True) for k in range(CONV_K)]
            return (dw0 + dws[0], dw1 + dws[1], dw2 + dws[2], dw3 + dws[3], dbs + jnp.sum(dcv, axis=0, keepdims=True))

        z = jnp.zeros((1, CONV_COLS), F32)
        dw0, dw1, dw2, dw3, dbs = lax.fori_loop(0, nsteps, step1, (z, z, z, z, z))
        dw_ref[...] = jnp.concatenate([dw0, dw1, dw2, dw3, z, z, z, z], axis=0)
        db_ref[...] = dbs

        def step2(i, carry):
            r0 = pl.multiple_of(i * CONV_ROWS, CONV_ROWS)
            ext = dc_ref[pl.ds(r0, CONV_ROWS + CONV_HALO), :]
            acc = w[0:1, :] * ext[CONV_K - 1:CONV_K - 1 + CONV_ROWS]
            for k in range(1, CONV_K):
                acc = acc + w[k:k + 1, :] * ext[CONV_K - 1 - k:CONV_K - 1 - k + CONV_ROWS]
            dx_ref[pl.ds(r0, CONV_ROWS), :] = acc.astype(dx_ref.dtype)
            return carry

        lax.fori_loop(0, nsteps, step2, 0)

    col = pl.BlockSpec((s, CONV_COLS), lambda j: (0, j))
    shifted = pl.BlockSpec((s, CONV_COLS), lambda j: (0, j + off))
    return pl.pallas_call(
        body,
        grid=(c // CONV_COLS,),
        in_specs=[shifted, col,
                  pl.BlockSpec((8, CONV_COLS), lambda j: (0, j)),
                  pl.BlockSpec((1, CONV_COLS), lambda j: (0, j)),
                  pl.BlockSpec(memory_space=pl.ANY)],
        out_specs=[shifted, pl.BlockSpec((8, CONV_COLS), lambda j: (0, j)), pl.BlockSpec((1, CONV_COLS), lambda j: (0, j))],
        out_shape=[jax.ShapeDtypeStruct((s, SSD_ZX), BF16), jax.ShapeDtypeStruct((8, c), F32),
                   jax.ShapeDtypeStruct((1, c), F32)],
        scratch_shapes=[pltpu.VMEM((s + CONV_HALO, CONV_COLS), F32)],
        input_output_aliases={4: 0},
        compiler_params=_params(("parallel",)),
        name="ssd_conv_bwd",
    )(zx, dxbc, conv_w, conv_b, dzx)


def _ssd_consts():
    li = lax.broadcasted_iota(jnp.int32, (CHUNK, CHUNK), 0)
    si = lax.broadcasted_iota(jnp.int32, (CHUNK, CHUNK), 1)
    tril = li >= si
    hrow = lax.broadcasted_iota(jnp.int32, (LANES, SSD_INNER), 0)
    hcol = lax.broadcasted_iota(jnp.int32, (LANES, SSD_INNER), 1) // SSD_HEADDIM
    expand = (hrow == hcol).astype(F32)
    return tril, expand


def _ssd_chunk_common(dtp_ref, bias_ref, alog_ref, tril, expand):
    lane = lax.broadcasted_iota(jnp.int32, (1, LANES), 1)
    valid = lane < SSD_HEADS
    pre = dtp_ref[...] + bias_ref[...]
    dt = jnp.where(valid, jnp.maximum(pre, 0.0) + jnp.log1p(jnp.exp(-jnp.abs(pre))), 0.0)
    a = jnp.where(valid, -jnp.exp(alog_ref[...]), 0.0)
    da = dt * a
    cs = _dot01(tril.astype(F32), da, NN, 1)
    cs_x = _dot01(cs, expand, NN, 0)
    dt_x = _dot01(dt, expand, NN, 0, terms=2)
    return pre, dt, a, cs, cs_x, dt_x


def _ssd_scan_fwd(xbc, dtp, dt_bias, a_log, d_skip):
    s = xbc.shape[0]
    nc = s // CHUNK
    gw = SSD_GROUP_W

    def body(xbc_ref, dtp_ref, bias_ref, alog_ref, d_ref, y_ref, prev_ref, state_ref):
        c = pl.program_id(0)

        @pl.when(c == 0)
        def _():
            state_ref[...] = jnp.zeros_like(state_ref)

        tril, expand = _ssd_consts()
        pre, dt, a, cs, cs_x, dt_x = _ssd_chunk_common(dtp_ref, bias_ref, alog_ref, tril, expand)
        cs_t = cs.T
        d_x = _dot01(jnp.broadcast_to(d_ref[...], (8, LANES)), expand, NN, 0)[0:1, :]
        cs_last = cs_x[CHUNK - 1:CHUNK, :]
        dec_out = jnp.exp(cs_x)
        dec_st = jnp.exp(cs_last - cs_x)
        dec_ch = jnp.exp(cs_last)
        x = xbc_ref[:, 0:SSD_INNER].astype(F32)
        xr = x * dt_x
        xrs = xr * dec_st
        lane_g = lax.broadcasted_iota(jnp.int32, (1, gw), 1) // SSD_HEADDIM
        for g in range(SSD_GROUPS):
            sl = slice(g * gw, (g + 1) * gw)
            bg = xbc_ref[:, SSD_INNER + g * SSD_STATE:SSD_INNER + (g + 1) * SSD_STATE]
            cg = xbc_ref[:, SSD_INNER + (SSD_GROUPS + g) * SSD_STATE:SSD_INNER + (SSD_GROUPS + g + 1) * SSD_STATE]
            cb = _dot(cg, bg, NT)
            prev_g = state_ref[:, sl]
            prev_ref[0, :, sl] = prev_g
            yo = _dot(cg, prev_g.astype(BF16), NN) * dec_out[:, sl]
            xr_g = xr[:, sl]
            yd = jnp.zeros((CHUNK, gw), F32)
            for r in range(SSD_HEADS // SSD_GROUPS):
                h = g * (SSD_HEADS // SSD_GROUPS) + r
                diff = cs[:, h:h + 1] - cs_t[h:h + 1, :]
                lmat = jnp.exp(jnp.where(tril, diff, -1e30))
                wmat = (cb * lmat).astype(BF16)
                xr_h = jnp.where(lane_g == r, xr_g, 0.0).astype(BF16)
                yd = yd + _dot(wmat, xr_h, NN)
            y_ref[:, sl] = yd + yo + x[:, sl] * d_x[:, sl]
            sc = _dot(bg, xrs[:, sl].astype(BF16), TN)
            state_ref[:, sl] = prev_g * dec_ch[:, sl] + sc

    vec = pl.BlockSpec((1, LANES), lambda c: (0, 0))
    return pl.pallas_call(
        body,
        grid=(nc,),
        in_specs=[pl.BlockSpec((CHUNK, SSD_CONV_DIM), lambda c: (c, 0)),
                  pl.BlockSpec((CHUNK, LANES), lambda c: (c, 0)), vec, vec, vec],
        out_specs=[pl.BlockSpec((CHUNK, SSD_INNER), lambda c: (c, 0)),
                   pl.BlockSpec((1, SSD_STATE, SSD_INNER), lambda c: (c, 0, 0))],
        out_shape=[jax.ShapeDtypeStruct((s, SSD_INNER), F32), jax.ShapeDtypeStruct((nc, SSD_STATE, SSD_INNER), F32)],
        scratch_shapes=[pltpu.VMEM((SSD_STATE, SSD_INNER), F32)],
        compiler_params=_params(("arbitrary",)),
        name="ssd_scan_fwd",
    )(xbc, dtp, dt_bias, a_log, d_skip)


def _ssd_scan_bwd(xbc, dtp, prev, dy, dt_bias, a_log, d_skip):
    s = xbc.shape[0]
    nc = s // CHUNK
    gw = SSD_GROUP_W
    hpg = SSD_HEADS // SSD_GROUPS

    def body(xbc_ref, dtp_ref, prev_ref, dy_ref, bias_ref, alog_ref, d_ref,
             dxbc_ref, ddtp_ref, dbias_ref, dalog_ref, dd_ref, dp_ref, ddx_ref):
        step = pl.program_id(0)

        @pl.when(step == 0)
        def _():
            dp_ref[...] = jnp.zeros_like(dp_ref)
            ddx_ref[...] = jnp.zeros_like(ddx_ref)
            dbias_ref[...] = jnp.zeros_like(dbias_ref)
            dalog_ref[...] = jnp.zeros_like(dalog_ref)

        tril, expand = _ssd_consts()
        pre, dt, a, cs, cs_x, dt_x = _ssd_chunk_common(dtp_ref, bias_ref, alog_ref, tril, expand)
        cs_t = cs.T
        d_x = _dot01(jnp.broadcast_to(d_ref[...], (8, LANES)), expand, NN, 0)[0:1, :]
        cs_last = cs_x[CHUNK - 1:CHUNK, :]
        dec_out = jnp.exp(cs_x)
        dec_st = jnp.exp(cs_last - cs_x)
        dec_ch = jnp.exp(cs_last)
        x = xbc_ref[:, 0:SSD_INNER].astype(F32)
        dyv = dy_ref[...]
        xr = x * dt_x
        xrs = xr * dec_st
        lane_g = lax.broadcasted_iota(jnp.int32, (1, gw), 1) // SSD_HEADDIM
        hsel = lax.broadcasted_iota(jnp.int32, (CHUNK, LANES), 1)
        dcs = jnp.zeros((CHUNK, LANES), F32)
        last_parts = []
        t_parts = []
        dxr_parts = []
        for g in range(SSD_GROUPS):
            sl = slice(g * gw, (g + 1) * gw)
            bsl = slice(SSD_INNER + g * SSD_STATE, SSD_INNER + (g + 1) * SSD_STATE)
            csl = slice(SSD_INNER + (SSD_GROUPS + g) * SSD_STATE, SSD_INNER + (SSD_GROUPS + g + 1) * SSD_STATE)
            bg = xbc_ref[:, bsl]
            cg = xbc_ref[:, csl]
            cb = _dot(cg, bg, NT)
            prev_g = prev_ref[0, :, sl]
            prev_b = prev_g.astype(BF16)
            dp_g = dp_ref[:, sl]
            dp_b = dp_g.astype(BF16)
            dy_g = dyv[:, sl]
            xr_g = xr[:, sl]
            gmat = _dot(cg, prev_b, NN)
            dgm = (dy_g * dec_out[:, sl]).astype(BF16)
            dc_g = _dot(dgm, prev_b, NT)
            dprev = _dot(cg, dgm, TN)
            t1 = dy_g * gmat * dec_out[:, sl]
            mm_ = _dot(bg, dp_b, NN)
            db_g = _dot(xrs[:, sl].astype(BF16), dp_b, NT)
            dxr_g = mm_ * dec_st[:, sl]
            t2 = dxr_g * xr_g
            last = jnp.sum(t2, axis=0, keepdims=True) + jnp.sum(dp_g * prev_g, axis=0, keepdims=True) * dec_ch[:, sl]
            dp_ref[:, sl] = dp_g * dec_ch[:, sl] + dprev
            dcb = jnp.zeros((CHUNK, CHUNK), F32)
            for r in range(hpg):
                h = g * hpg + r
                diff = cs[:, h:h + 1] - cs_t[h:h + 1, :]
                lmat = jnp.exp(jnp.where(tril, diff, -1e30))
                wmat = cb * lmat
                dy_h = jnp.where(lane_g == r, dy_g, 0.0).astype(BF16)
                dw = _dot(dy_h, xr_g.astype(BF16), NT)
                dxr_g = dxr_g + _dot(wmat.astype(BF16), dy_h, TN)
                dcb = dcb + dw * lmat
                q = (dw * wmat).astype(BF16)
                onehot = (hsel == h).astype(BF16)
                dcs = dcs + _dot(q, onehot, NN) - _dot(q, onehot, TN)
            dcb_b = dcb.astype(BF16)
            dc_g = dc_g + _dot(dcb_b, bg, NN)
            db_g = db_g + _dot(dcb_b, cg, TN)
            dxbc_ref[:, bsl] = db_g.astype(dxbc_ref.dtype)
            dxbc_ref[:, csl] = dc_g.astype(dxbc_ref.dtype)
            t_parts.append(t1 - t2)
            last_parts.append(last)
            dxr_parts.append(dxr_g)
        dxr = jnp.concatenate(dxr_parts, axis=1)
        tt = jnp.concatenate(t_parts, axis=1)
        last_x = jnp.concatenate(last_parts, axis=1)
        dxbc_ref[:, 0:SSD_INNER] = (dxr * dt_x + dyv * d_x).astype(dxbc_ref.dtype)
        dcs = dcs + _dot01(tt, expand, NT, 0, terms=2)
        last_h = _dot01(jnp.broadcast_to(last_x, (8, SSD_INNER)), expand, NT, 0)[0:1, :]
        rowi = lax.broadcasted_iota(jnp.int32, (CHUNK, LANES), 0)
        dcs = dcs + jnp.where(rowi == CHUNK - 1, last_h, 0.0)
        dda = _dot01(tril.astype(F32), dcs, TN, 1)
        ddt = dda * a + _dot01(dxr * x, expand, NT, 0, terms=2)
        dpre = ddt * _sigmoid(pre)
        ddtp_ref[...] = dpre
        dbias_ref[...] += jnp.sum(dpre, axis=0, keepdims=True)
        dalog_ref[...] += jnp.sum(dda * dt, axis=0, keepdims=True) * a
        ddx_ref[...] += jnp.broadcast_to(jnp.sum(dyv * x, axis=0, keepdims=True), (8, SSD_INNER))

        @pl.when(step == nc - 1)
        def _():
            dd_ref[...] = _dot01(ddx_ref[...], expand, NT, 0)[0:1, :]

    rev = lambda c: (nc - 1 - c, 0)
    vec = pl.BlockSpec((1, LANES), lambda c: (0, 0))
    return pl.pallas_call(
        body,
        grid=(nc,),
        in_specs=[pl.BlockSpec((CHUNK, SSD_CONV_DIM), rev), pl.BlockSpec((CHUNK, LANES), rev),
                  pl.BlockSpec((1, SSD_STATE, SSD_INNER), lambda c: (nc - 1 - c, 0, 0)),
                  pl.BlockSpec((CHUNK, SSD_INNER), rev), vec, vec, vec],
        out_specs=[pl.BlockSpec((CHUNK, SSD_CONV_DIM), rev), pl.BlockSpec((CHUNK, LANES), rev), vec, vec, vec],
        out_shape=[jax.ShapeDtypeStruct((s, SSD_CONV_DIM), BF16), jax.ShapeDtypeStruct((s, LANES), F32),
                   jax.ShapeDtypeStruct((1, LANES), F32), jax.ShapeDtypeStruct((1, LANES), F32),
                   jax.ShapeDtypeStruct((1, LANES), F32)],
        scratch_shapes=[pltpu.VMEM((SSD_STATE, SSD_INNER), F32), pltpu.VMEM((8, SSD_INNER), F32)],
        compiler_params=_params(("arbitrary",)),
        name="ssd_scan_bwd",
    )(xbc, dtp, prev, dy, dt_bias, a_log, d_skip)


def _ssd_gate_fwd(y, zx, norm_w):
    s = y.shape[0]
    tr = _pick(s, (256, 128))
    gw = SSD_GROUP_W

    def body(y_ref, z_ref, w_ref, o_ref):
        for g in range(SSD_GROUPS):
            sl = slice(g * gw, (g + 1) * gw)
            z = z_ref[:, sl].astype(F32)
            gv = y_ref[:, sl] * (z * _sigmoid(z))
            r = lax.rsqrt(jnp.mean(gv * gv, axis=-1, keepdims=True) + LN_EPS)
            o_ref[:, sl] = (gv * r * w_ref[:, sl]).astype(o_ref.dtype)

    row = pl.BlockSpec((tr, SSD_INNER), lambda i: (i, 0))
    return pl.pallas_call(
        body,
        grid=(s // tr,),
        in_specs=[row, row, pl.BlockSpec((1, SSD_INNER), lambda i: (0, 0))],
        out_specs=row,
        out_shape=jax.ShapeDtypeStruct((s, SSD_INNER), BF16),
        compiler_params=_params(("parallel",)),
        name="ssd_gate_fwd",
    )(y, zx, norm_w)


def _ssd_gate_bwd(dgn, y, zx, norm_w):
    s = y.shape[0]
    tr = _pick(s, (256, 128))
    gw = SSD_GROUP_W

    def body(dg_ref, y_ref, z_ref, w_ref, dy_ref, dz_ref, dw_ref):
        parts = []
        for g in range(SSD_GROUPS):
            sl = slice(g * gw, (g + 1) * gw)
            z = z_ref[:, sl].astype(F32)
            yv = y_ref[:, sl]
            sg = _sigmoid(z)
            sz = z * sg
            gv = yv * sz
            r = lax.rsqrt(jnp.mean(gv * gv, axis=-1, keepdims=True) + LN_EPS)
            gn = gv * r
            dout = dg_ref[:, sl].astype(F32)
            parts.append(jnp.sum(dout * gn, axis=0, keepdims=True))
            dgn_ = dout * w_ref[:, sl]
            dgv = r * (dgn_ - gn * jnp.mean(dgn_ * gn, axis=-1, keepdims=True))
            dy_ref[:, sl] = dgv * sz
            dz_ref[:, sl] = (dgv * yv * (sg * (1.0 + z * (1.0 - sg)))).astype(dz_ref.dtype)
        part = jnp.concatenate(parts, axis=1)

        @pl.when(pl.program_id(0) == 0)
        def _():
            dw_ref[...] = part

        @pl.when(pl.program_id(0) > 0)
        def _():
            dw_ref[...] += part

    row = pl.BlockSpec((tr, SSD_INNER), lambda i: (i, 0))
    vec = pl.BlockSpec((1, SSD_INNER), lambda i: (0, 0))
    return pl.pallas_call(
        body,
        grid=(s // tr,),
        in_specs=[row, row, row, vec],
        out_specs=[row, row, vec],
        out_shape=[jax.ShapeDtypeStruct((s, SSD_INNER), F32), jax.ShapeDtypeStruct((s, SSD_ZX), BF16),
                   jax.ShapeDtypeStruct((1, SSD_INNER), F32)],
        compiler_params=_params(("arbitrary",)),
        name="ssd_gate_bwd",
    )(dgn, y, zx, norm_w)


INV_SQRT2 = 1.0 / math.sqrt(2.0)
INV_SQRT2PI = 1.0 / math.sqrt(2.0 * math.pi)


def _gelu(x):
    return 0.5 * x * (1.0 + lax.erf(x * INV_SQRT2))


def _gelu_grad(x):
    return 0.5 * (1.0 + lax.erf(x * INV_SQRT2)) + x * INV_SQRT2PI * jnp.exp(-0.5 * x * x)


def _gmlp_act_fwd(pre, b_in, ln_w, ln_b):
    s = pre.shape[0]
    tr = _pick(s, (256, 128))
    n = GMLP_INNER

    def body(p_ref, b_ref, w_ref, lb_ref, u_ref, v_ref):
        u_ref[...] = _gelu(p_ref[:, 0:n].astype(F32) + b_ref[:, 0:n]).astype(u_ref.dtype)
        hv = _gelu(p_ref[:, n:2 * n].astype(F32) + b_ref[:, n:2 * n])
        mu = jnp.mean(hv, axis=-1, keepdims=True)
        xc = hv - mu
        r = lax.rsqrt(jnp.mean(xc * xc, axis=-1, keepdims=True) + LN_EPS)
        v_ref[...] = (xc * r * w_ref[...] + lb_ref[...]).astype(v_ref.dtype)

    half = pl.BlockSpec((tr, n), lambda i: (i, 0))
    vec = pl.BlockSpec((1, n), lambda i: (0, 0))
    return pl.pallas_call(
        body,
        grid=(s // tr,),
        in_specs=[pl.BlockSpec((tr, 2 * n), lambda i: (i, 0)), pl.BlockSpec((1, 2 * n), lambda i: (0, 0)), vec, vec],
        out_specs=[half, half],
        out_shape=[jax.ShapeDtypeStruct((s, n), BF16), jax.ShapeDtypeStruct((s, n), BF16)],
        compiler_params=_params(("parallel",)),
        name="gmlp_act_fwd",
    )(pre, b_in, ln_w, ln_b)


def _gmlp_act_bwd(pre, b_in, ln_w, du, dv):
    s = pre.shape[0]
    tr = _pick(s, (256, 128))
    n = GMLP_INNER

    def body(p_ref, b_ref, w_ref, du_ref, dv_ref, dp_ref, db_ref, dw_ref, dlb_ref):
        xu = p_ref[:, 0:n].astype(F32) + b_ref[:, 0:n]
        dpu = du_ref[...].astype(F32) * _gelu_grad(xu)
        xv = p_ref[:, n:2 * n].astype(F32) + b_ref[:, n:2 * n]
        hv = _gelu(xv)
        mu = jnp.mean(hv, axis=-1, keepdims=True)
        xc = hv - mu
        r = lax.rsqrt(jnp.mean(xc * xc, axis=-1, keepdims=True) + LN_EPS)
        vh = xc * r
        dvv = dv_ref[...].astype(F32)
        dvh = dvv * w_ref[...]
        dh = r * (dvh - jnp.mean(dvh, axis=-1, keepdims=True) - vh * jnp.mean(dvh * vh, axis=-1, keepdims=True))
        dpv = dh * _gelu_grad(xv)
        dp_ref[:, 0:n] = dpu.astype(dp_ref.dtype)
        dp_ref[:, n:2 * n] = dpv.astype(dp_ref.dtype)
        pb = jnp.concatenate([jnp.sum(dpu, axis=0, keepdims=True), jnp.sum(dpv, axis=0, keepdims=True)], axis=1)
        pw = jnp.sum(dvv * vh, axis=0, keepdims=True)
        plb = jnp.sum(dvv, axis=0, keepdims=True)

        @pl.when(pl.program_id(0) == 0)
        def _():
            db_ref[...] = pb
            dw_ref[...] = pw
            dlb_ref[...] = plb

        @pl.when(pl.program_id(0) > 0)
        def _():
            db_ref[...] += pb
            dw_ref[...] += pw
            dlb_ref[...] += plb

    half = pl.BlockSpec((tr, n), lambda i: (i, 0))
    full = pl.BlockSpec((tr, 2 * n), lambda i: (i, 0))
    vec = pl.BlockSpec((1, n), lambda i: (0, 0))
    vec2 = pl.BlockSpec((1, 2 * n), lambda i: (0, 0))
    return pl.pallas_call(
        body,
        grid=(s // tr,),
        in_specs=[full, vec2, vec, half, half],
        out_specs=[full, vec2, vec, vec],
        out_shape=[jax.ShapeDtypeStruct((s, 2 * n), BF16), jax.ShapeDtypeStruct((1, 2 * n), F32),
                   jax.ShapeDtypeStruct((1, n), F32), jax.ShapeDtypeStruct((1, n), F32)],
        compiler_params=_params(("arbitrary",)),
        name="gmlp_act_bwd",
    )(pre, b_in, ln_w, du, dv)


def _gmlp_mix_fwd(u, v, w_s, b_st):
    s = u.shape[0]
    gd = GMLP_INNER // GMLP_GROUPS

    def body(u_ref, v_ref, w_ref, b_ref, o_ref):
        li = lax.broadcasted_iota(jnp.int32, (CHUNK, CHUNK), 0)
        si = lax.broadcasted_iota(jnp.int32, (CHUNK, CHUNK), 1)
        tril = li >= si
        for g in range(GMLP_GROUPS):
            sl = slice(g * gd, (g + 1) * gd)
            wm = jnp.where(tril, w_ref[g], 0.0).astype(BF16)
            mixed = _dot(wm, v_ref[:, sl], NN) + b_ref[:, g:g + 1]
            o_ref[:, sl] = (u_ref[:, sl].astype(F32) * mixed).astype(o_ref.dtype)

    row = pl.BlockSpec((CHUNK, GMLP_INNER), lambda c: (c, 0))
    return pl.pallas_call(
        body,
        grid=(s // CHUNK,),
        in_specs=[row, row, pl.BlockSpec((GMLP_GROUPS, CHUNK, CHUNK), lambda c: (0, 0, 0)),
                  pl.BlockSpec((CHUNK, LANES), lambda c: (0, 0))],
        out_specs=row,
        out_shape=jax.ShapeDtypeStruct((s, GMLP_INNER), BF16),
        compiler_params=_params(("parallel",)),
        name="gmlp_mix_fwd",
    )(u, v, w_s, b_st)


def _gmlp_mix_bwd(dgated, u, v, w_s, b_st):
    s = u.shape[0]
    nc = s // CHUNK
    gd = GMLP_INNER // GMLP_GROUPS

    def body(dg_ref, u_ref, v_ref, w_ref, b_ref, du_ref, dv_ref, dw_ref, db_ref):
        c = pl.program_id(0)

        @pl.when(c == 0)
        def _():
            dw_ref[...] = jnp.zeros_like(dw_ref)
            db_ref[...] = jnp.zeros_like(db_ref)

        li = lax.broadcasted_iota(jnp.int32, (CHUNK, CHUNK), 0)
        si = lax.broadcasted_iota(jnp.int32, (CHUNK, CHUNK), 1)
        tril = li >= si
        lane = lax.broadcasted_iota(jnp.int32, (CHUNK, LANES), 1)
        dbacc = jnp.zeros((CHUNK, LANES), F32)
        for g in range(GMLP_GROUPS):
            sl = slice(g * gd, (g + 1) * gd)
            wm = jnp.where(tril, w_ref[g], 0.0).astype(BF16)
            vg = v_ref[:, sl]
            mixed = _dot(wm, vg, NN) + b_ref[:, g:g + 1]
            dgv = dg_ref[:, sl].astype(F32)
            du_ref[:, sl] = (dgv * mixed).astype(du_ref.dtype)
            dm = dgv * u_ref[:, sl].astype(F32)
            dm_b = dm.astype(BF16)
            dv_ref[:, sl] = _dot(wm, dm_b, TN).astype(dv_ref.dtype)
            dw_ref[g] += jnp.where(tril, _dot(dm_b, vg, NT), 0.0)
            dbacc = dbacc + jnp.where(lane == g, jnp.sum(dm, axis=1, keepdims=True), 0.0)
        db_ref[...] += dbacc

    row = pl.BlockSpec((CHUNK, GMLP_INNER), lambda c: (c, 0))
    wspec = pl.BlockSpec((GMLP_GROUPS, CHUNK, CHUNK), lambda c: (0, 0, 0))
    bspec = pl.BlockSpec((CHUNK, LANES), lambda c: (0, 0))
    return pl.pallas_call(
        body,
        grid=(nc,),
        in_specs=[row, row, row, wspec, bspec],
        out_specs=[row, row, wspec, bspec],
        out_shape=[jax.ShapeDtypeStruct((s, GMLP_INNER), BF16), jax.ShapeDtypeStruct((s, GMLP_INNER), BF16),
                   jax.ShapeDtypeStruct((GMLP_GROUPS, CHUNK, CHUNK), F32), jax.ShapeDtypeStruct((CHUNK, LANES), F32)],
        compiler_params=_params(("arbitrary",)),
        name="gmlp_mix_bwd",
    )(dgated, u, v, w_s, b_st)


FFN_HALF = FFN_DIM // 2


def _ffn_up(un, wgu):
    s, d = un.shape
    f = FFN_DIM
    tm = _pick(s, (1024, 512, 256, 128))
    nh = f // FFN_HALF

    def body(x_ref, wg_ref, wu_ref, g_ref, u_ref, h_ref):
        x = x_ref[...]
        g_ref[...] = _dot(x, wg_ref[...], NN).astype(g_ref.dtype)
        u_ref[...] = _dot(x, wu_ref[...], NN).astype(u_ref.dtype)
        gt = g_ref[...].astype(F32)
        h_ref[...] = (gt * _sigmoid(gt) * u_ref[...].astype(F32)).astype(h_ref.dtype)

    out = pl.BlockSpec((tm, FFN_HALF), lambda i, j: (i, j))
    sds = jax.ShapeDtypeStruct((s, f), BF16)
    return pl.pallas_call(
        body,
        grid=(s // tm, nh),
        in_specs=[pl.BlockSpec((tm, d), lambda i, j: (i, 0)), pl.BlockSpec((d, FFN_HALF), lambda i, j: (0, j)),
                  pl.BlockSpec((d, FFN_HALF), lambda i, j: (0, j + nh))],
        out_specs=[out, out, out],
        out_shape=[sds, sds, sds],
        compiler_params=_params(("parallel", "parallel")),
        name="ffn_up",
    )(un, wgu, wgu)


def _ffn_down_bwd(dh, wd, gate, up):
    s, d = dh.shape
    f = FFN_DIM
    tm = _pick(s, (512, 256, 128))

    def body(dh_ref, wd_ref, g_ref, u_ref, o_ref):
        dhb = dh_ref[...].astype(BF16)
        for half in range(f // FFN_HALF):
            cols = slice(half * FFN_HALF, (half + 1) * FFN_HALF)
            dhid = _dot(dhb, wd_ref[cols, :], NT)
            gt = g_ref[:, cols].astype(F32)
            sg = _sigmoid(gt)
            o_ref[:, cols] = (dhid * u_ref[:, cols].astype(F32) * (sg * (1.0 + gt * (1.0 - sg)))).astype(o_ref.dtype)
            o_ref[:, f + half * FFN_HALF:f + (half + 1) * FFN_HALF] = (dhid * gt * sg).astype(o_ref.dtype)

    row = pl.BlockSpec((tm, f), lambda i: (i, 0))
    return pl.pallas_call(
        body,
        grid=(s // tm,),
        in_specs=[pl.BlockSpec((tm, d), lambda i: (i, 0)), pl.BlockSpec((f, d), lambda i: (0, 0)), row, row],
        out_specs=pl.BlockSpec((tm, 2 * f), lambda i: (i, 0)),
        out_shape=jax.ShapeDtypeStruct((s, 2 * f), BF16),
        compiler_params=_params(("parallel",)),
        name="ffn_down_bwd",
    )(dh, wd, gate, up)


def _ple_fwd(p, h, w_proj, w_gate, gate_norm, ple_norm, next_norm):
    s, d = h.shape
    e = p.shape[1]
    tr = _pick(s, (512, 256, 128))

    def body(p_ref, h_ref, wp_ref, wg_ref, gn_ref, pn_ref, nn_ref, o_ref, pe_ref, hg_ref, gl_ref, hn_ref):
        hv = h_ref[...]
        pe_ref[...] = _dot(p_ref[...].astype(BF16), wp_ref[...], NN).astype(pe_ref.dtype)
        r = lax.rsqrt(jnp.mean(hv * hv, axis=-1, keepdims=True) + RMS_EPS)
        hg_ref[...] = (hv * r * gn_ref[...]).astype(hg_ref.dtype)
        gl_ref[...] = _dot(hg_ref[...], wg_ref[...], NN).astype(gl_ref.dtype)
        pe_ = pe_ref[...].astype(F32)
        rp = lax.rsqrt(jnp.mean(pe_ * pe_, axis=-1, keepdims=True) + RMS_EPS)
        out = hv + _sigmoid(gl_ref[...].astype(F32)) * (pe_ * rp * pn_ref[...])
        o_ref[...] = out
        ro = lax.rsqrt(jnp.mean(out * out, axis=-1, keepdims=True) + RMS_EPS)
        hn_ref[...] = (out * ro * nn_ref[...]).astype(hn_ref.dtype)

    row = pl.BlockSpec((tr, d), lambda i: (i, 0))
    vec = pl.BlockSpec((1, d), lambda i: (0, 0))
    sds = jax.ShapeDtypeStruct((s, d), BF16)
    return pl.pallas_call(
        body,
        grid=(s // tr,),
        in_specs=[pl.BlockSpec((tr, e), lambda i: (i, 0)), row, pl.BlockSpec((e, d), lambda i: (0, 0)),
                  pl.BlockSpec((d, d), lambda i: (0, 0)), vec, vec, vec],
        out_specs=[row, row, row, row, row],
        out_shape=[jax.ShapeDtypeStruct((s, d), F32), sds, sds, sds, sds],
        compiler_params=_params(("parallel",)),
        name="ple_fwd",
    )(p, h, w_proj, w_gate, gate_norm, ple_norm, next_norm)


def _ple_bwd(dh, p, pe, gl, hg, h, w_gate, gate_norm, ple_norm):
    s, d = dh.shape
    e = p.shape[1]
    tr = _pick(s, (512, 256, 128))

    def body(dh_ref, p_ref, pe_ref, gl_ref, hg_ref, h_ref, wg_ref, gn_ref, pn_ref,
             dx_ref, dwg_ref, dwp_ref, dpn_ref, dgn_ref):
        pe_ = pe_ref[...].astype(F32)
        dhv = dh_ref[...]
        r = lax.rsqrt(jnp.mean(pe_ * pe_, axis=-1, keepdims=True) + RMS_EPS)
        pn = pe_ * r
        gate = _sigmoid(gl_ref[...].astype(F32))
        dgl = (dhv * (pn * pn_ref[...]) * gate * (1.0 - gate)).astype(BF16)
        de = dhv * gate
        dxh = de * pn_ref[...]
        dpe = (r * (dxh - pn * jnp.mean(dxh * pn, axis=-1, keepdims=True))).astype(BF16)
        dhg = _dot(dgl, wg_ref[...], NT)
        hv = h_ref[...]
        rh = lax.rsqrt(jnp.mean(hv * hv, axis=-1, keepdims=True) + RMS_EPS)
        hn = hv * rh
        dhh = dhg * gn_ref[...]
        dx_ref[...] = dhv + rh * (dhh - hn * jnp.mean(dhh * hn, axis=-1, keepdims=True))
        parts = (_dot(hg_ref[...], dgl, TN), _dot(p_ref[...].astype(BF16), dpe, TN),
                 jnp.sum(de * pn, axis=0, keepdims=True), jnp.sum(dhg * hn, axis=0, keepdims=True))
        accs = (dwg_ref, dwp_ref, dpn_ref, dgn_ref)

        @pl.when(pl.program_id(0) == 0)
        def _():
            for acc, part in zip(accs, parts):
                acc[...] = part

        @pl.when(pl.program_id(0) > 0)
        def _():
            for acc, part in zip(accs, parts):
                acc[...] += part

    row = pl.BlockSpec((tr, d), lambda i: (i, 0))
    vec = pl.BlockSpec((1, d), lambda i: (0, 0))
    mat = pl.BlockSpec((d, d), lambda i: (0, 0))
    small = pl.BlockSpec((e, d), lambda i: (0, 0))
    return pl.pallas_call(
        body,
        grid=(s // tr,),
        in_specs=[row, pl.BlockSpec((tr, e), lambda i: (i, 0)), row, row, row, row, mat, vec, vec],
        out_specs=[row, mat, small, vec, vec],
        out_shape=[jax.ShapeDtypeStruct((s, d), F32), jax.ShapeDtypeStruct((d, d), F32),
                   jax.ShapeDtypeStruct((e, d), F32), jax.ShapeDtypeStruct((1, d), F32),
                   jax.ShapeDtypeStruct((1, d), F32)],
        compiler_params=_params(("arbitrary",)),
        name="ple_bwd",
    )(dh, p, pe, gl, hg, h, w_gate, gate_norm, ple_norm)


def _loss_head(h, w, target):
    s, d = h.shape
    tr = _pick(s, (512, 256, 128))

    def body(h_ref, w_ref, t_ref, l_ref, dh_ref, dw_ref):
        hv = h_ref[...]
        r = lax.rsqrt(jnp.mean(hv * hv, axis=-1, keepdims=True) + RMS_EPS)
        hn = hv * r
        diff = hn * w_ref[...] - t_ref[...]
        lpart = jnp.zeros((8, LANES), F32) + (0.5 / d) * jnp.sum(jnp.sum(diff * diff, axis=1, keepdims=True), axis=0, keepdims=True)
        dy = diff * (1.0 / d)
        dxh = dy * w_ref[...]
        dh_ref[...] = r * (dxh - hn * jnp.mean(dxh * hn, axis=-1, keepdims=True))
        part = jnp.sum(dy * hn, axis=0, keepdims=True)

        @pl.when(pl.program_id(0) == 0)
        def _():
            l_ref[...] = lpart
            dw_ref[...] = part

        @pl.when(pl.program_id(0) > 0)
        def _():
            l_ref[...] += lpart
            dw_ref[...] += part

    row = pl.BlockSpec((tr, d), lambda i: (i, 0))
    vec = pl.BlockSpec((1, d), lambda i: (0, 0))
    return pl.pallas_call(
        body,
        grid=(s // tr,),
        in_specs=[row, vec, row],
        out_specs=[pl.BlockSpec((8, LANES), lambda i: (0, 0)), row, vec],
        out_shape=[jax.ShapeDtypeStruct((8, LANES), F32), jax.ShapeDtypeStruct((s, d), F32),
                   jax.ShapeDtypeStruct((1, d), F32)],
        compiler_params=_params(("arbitrary",)),
        name="loss_head",
    )(h, w, target)


PER_LAYER = ("norm_mix", "norm_ffn", "ffn_w_gu", "ffn_w_down", "ple_w_proj", "ple_norm", "ple_gate_norm", "ple_w_gate")


def _pad_lanes(v):
    return jnp.pad(v.astype(F32), (0, LANES - v.shape[0]))[None, :]


def _kernel_layouts(full):
    w = {}
    for k in ("norm_mix", "norm_ffn", "ple_norm", "ple_gate_norm", "ssd_conv_b", "ssd_norm_w", "gmlp_b_in", "gmlp_ln_w",
              "gmlp_ln_b", "gmlp_w_s"):
        w[k] = [full[k][i].astype(F32) for i in range(full[k].shape[0])]
    w["final_norm"] = full["final_norm"].astype(F32)
    n_ssd = full["ssd_w_out"].shape[0]
    if "ssd_w_in" in full:
        w["ssd_w_zx"] = [full["ssd_w_in"][j][:, :SSD_ZX].astype(BF16) for j in range(n_ssd)]
        w["ssd_w_dt"] = [jnp.pad(full["ssd_w_in"][j][:, SSD_ZX:].astype(BF16), ((0, 0), (0, LANES - SSD_HEADS)))
                         for j in range(n_ssd)]
        w["ffn_w_gu"] = [jnp.concatenate([full["ffn_w_gate"][i], full["ffn_w_up"][i]], axis=1).astype(BF16)
                         for i in range(DEPTH)]
    else:
        for k in ("ssd_w_zx", "ssd_w_dt", "ffn_w_gu"):
            w[k] = full[k]
    w["ssd_conv_w"] = [jnp.pad(full["ssd_conv_w"][j].astype(F32), ((0, 8 - CONV_K), (0, 0))) for j in range(n_ssd)]
    for k in ("ssd_dt_bias", "ssd_a_log", "ssd_d"):
        w[k] = [_pad_lanes(full[k][j]) for j in range(n_ssd)]
    w["ssd_w_out"] = [full["ssd_w_out"][j].astype(BF16) for j in range(n_ssd)]
    n_g = full["gmlp_w_in"].shape[0]
    w["gmlp_w_in"] = [full["gmlp_w_in"][j].astype(BF16) for j in range(n_g)]
    w["gmlp_w_out"] = [full["gmlp_w_out"][j].astype(BF16) for j in range(n_g)]
    w["gmlp_b_st"] = [jnp.pad(full["gmlp_b_s"][j].astype(F32).T, ((0, 0), (0, LANES - GMLP_GROUPS))) for j in range(n_g)]
    w["ffn_w_down"] =[full["ffn_w_down"][i].astype(BF16) for i in range(DEPTH)]
    w["ple_w_proj"] = [full["ple_w_proj"][i].astype(BF16) for i in range(DEPTH)]
    w["ple_w_gate"] = [full["ple_w_gate"][i].astype(BF16) for i in range(DEPTH)]
    return w


MATRICES = ("ssd_w_out", "gmlp_w_in", "gmlp_w_out", "ffn_w_down", "ple_w_proj", "ple_w_gate")


def _reference_layouts(g, wide=True):
    out = {}
    for k in ("norm_mix", "norm_ffn", "ple_norm", "ple_gate_norm", "ssd_conv_b", "ssd_norm_w", "gmlp_b_in", "gmlp_ln_w",
              "gmlp_ln_b", "gmlp_w_s", "ssd_conv_w", "ssd_dt_bias", "ssd_a_log", "ssd_d") + (MATRICES if wide else ()):
        out[k] = jnp.stack(g[k])
    out["final_norm"] = g["final_norm"]
    out["gmlp_b_s"] = jnp.stack([b[:, :GMLP_GROUPS].T for b in g["gmlp_b_st"]])
    if wide:
        out["ssd_w_in"] = jnp.stack([jnp.concatenate([zx, dt[:, :SSD_HEADS]], axis=1)
                                     for zx, dt in zip(g["ssd_w_zx"], g["ssd_w_dt"])])
        out["ffn_w_gate"] = jnp.stack([gu[:, :FFN_DIM] for gu in g["ffn_w_gu"]])
        out["ffn_w_up"] = jnp.stack([gu[:, FFN_DIM:] for gu in g["ffn_w_gu"]])
    return out


RELAYOUT_ROWS = 128
SSD_SHARD = SSD_IN_DIM // N_DEV
FFN_SHARD = FFN_DIM // N_DEV


def _to_bf16(x):
    nl, rows, n = x.shape

    def body(x_ref, o_ref):
        o_ref[...] = x_ref[...].astype(o_ref.dtype)

    blk = pl.BlockSpec((1, rows, n), lambda i: (i, 0, 0))
    return pl.pallas_call(
        body, grid=(nl,), in_specs=[blk], out_specs=blk, out_shape=jax.ShapeDtypeStruct(x.shape, BF16),
        compiler_params=_params(("parallel",)), name="to_bf16",
    )(x)


def _cat_ssd_in(gathered):
    _, nl, rows, n = gathered.shape
    tr = RELAYOUT_ROWS

    def body(g_ref, *o_refs):
        for j in range(nl):
            full = jnp.concatenate([g_ref[d, j] for d in range(N_DEV)], axis=1)
            o_refs[2 * j][...] = full[:, :SSD_ZX]
            o_refs[2 * j + 1][...] = jnp.concatenate(
                [full[:, SSD_ZX:], jnp.zeros((tr, LANES - SSD_HEADS), full.dtype)], axis=1)

    outs = pl.pallas_call(
        body, grid=(rows // tr,),
        in_specs=[pl.BlockSpec((N_DEV, nl, tr, n), lambda i: (0, 0, i, 0))],
        out_specs=[pl.BlockSpec((tr, SSD_ZX), lambda i: (i, 0)), pl.BlockSpec((tr, LANES), lambda i: (i, 0))] * nl,
        out_shape=[jax.ShapeDtypeStruct((rows, SSD_ZX), BF16), jax.ShapeDtypeStruct((rows, LANES), BF16)] * nl,
        compiler_params=_params(("parallel",)), name="cat_ssd_in",
    )(gathered)
    return [outs[2 * j] for j in range(nl)], [outs[2 * j + 1] for j in range(nl)]


def _split_ssd_in(dzx_list, ddt_list):
    nl = len(dzx_list)
    rows = dzx_list[0].shape[0]
    tr = RELAYOUT_ROWS

    def body(*refs):
        o_ref = refs[2 * nl]
        for j in range(nl):
            full = jnp.concatenate([refs[2 * j][...], refs[2 * j + 1][:, 0:SSD_HEADS]], axis=1)
            for d in range(N_DEV):
                o_ref[d, j] = full[:, d * SSD_SHARD:(d + 1) * SSD_SHARD].astype(o_ref.dtype)

    ins = []
    for j in range(nl):
        ins += [dzx_list[j], ddt_list[j]]
    return pl.pallas_call(
        body, grid=(rows // tr,),
        in_specs=[pl.BlockSpec((tr, SSD_ZX), lambda i: (i, 0)), pl.BlockSpec((tr, LANES), lambda i: (i, 0))] * nl,
        out_specs=pl.BlockSpec((N_DEV, nl, tr, SSD_SHARD), lambda i: (0, 0, i, 0)),
        out_shape=jax.ShapeDtypeStruct((N_DEV, nl, rows, SSD_SHARD), BF16),
        compiler_params=_params(("parallel",)), name="split_ssd_in",
    )(*ins)


def _cat_ffn(g_gate, g_up):
    _, nl, rows, n = g_gate.shape
    tr = RELAYOUT_ROWS

    def body(gg_ref, gu_ref, *o_refs):
        for i in range(nl):
            o_refs[i][...] = jnp.concatenate([gg_ref[d, i] for d in range(N_DEV)] + [gu_ref[d, i] for d in range(N_DEV)],
                                             axis=1)

    blk = pl.BlockSpec((N_DEV, nl, tr, n), lambda i: (0, 0, i, 0))
    outs = pl.pallas_call(
        body, grid=(rows // tr,), in_specs=[blk, blk],
        out_specs=[pl.BlockSpec((tr, 2 * FFN_DIM), lambda i: (i, 0))] * nl,
        out_shape=[jax.ShapeDtypeStruct((rows, 2 * FFN_DIM), BF16)] * nl,
        compiler_params=_params(("parallel",)), name="cat_ffn",
    )(g_gate, g_up)
    return list(outs)


def _split_ffn(dgu_list):
    nl = len(dgu_list)
    rows = dgu_list[0].shape[0]
    tr = RELAYOUT_ROWS

    def body(*refs):
        og_ref, ou_ref = refs[nl], refs[nl + 1]
        for i in range(nl):
            full = refs[i][...]
            for d in range(N_DEV):
                og_ref[d, i] = full[:, d * FFN_SHARD:(d + 1) * FFN_SHARD].astype(og_ref.dtype)
                ou_ref[d, i] = full[:, FFN_DIM + d * FFN_SHARD:FFN_DIM + (d + 1) * FFN_SHARD].astype(ou_ref.dtype)

    blk = pl.BlockSpec((N_DEV, nl, tr, FFN_SHARD), lambda i: (0, 0, i, 0))
    sds = jax.ShapeDtypeStruct((N_DEV, nl, rows, FFN_SHARD), BF16)
    return pl.pallas_call(
        body, grid=(rows // tr,),
        in_specs=[pl.BlockSpec((tr, 2 * FFN_DIM), lambda i: (i, 0))] * nl,
        out_specs=[blk, blk], out_shape=[sds, sds],
        compiler_params=_params(("parallel",)), name="split_ffn",
    )(*dgu_list)


def _local_step(x, p, target, w):
    saved = []
    h = x
    hn = _rms_fwd(h, w["norm_mix"][0][None, :])
    for i in range(DEPTH):
        j = i // 2
        sv = {"h0": h}
        sv["hn"] = hn
        if i % 2 == 0:
            zx = _mm(hn, w["ssd_w_zx"][j], "nn", BF16)
            dtp = _mm(hn, w["ssd_w_dt"][j], "nn", F32)
            xbc = _ssd_conv_fwd(zx, w["ssd_conv_w"][j], w["ssd_conv_b"][j][None, :])
            y, prev = _ssd_scan_fwd(xbc, dtp, w["ssd_dt_bias"][j], w["ssd_a_log"][j], w["ssd_d"][j])
            gn = _ssd_gate_fwd(y, zx, w["ssd_norm_w"][j][None, :])
            h, un = _mm(gn, w["ssd_w_out"][j], "nn", F32, add=h, norm=w["norm_ffn"][i][None, :])
            sv.update(zx=zx, dtp=dtp, xbc=xbc, y=y, prev=prev, gn=gn)
        else:
            pre = _mm(hn, w["gmlp_w_in"][j], "nn", BF16)
            u, v = _gmlp_act_fwd(pre, w["gmlp_b_in"][j][None, :], w["gmlp_ln_w"][j][None, :], w["gmlp_ln_b"][j][None, :])
            gated = _gmlp_mix_fwd(u, v, w["gmlp_w_s"][j], w["gmlp_b_st"][j])
            h, un = _mm(gated, w["gmlp_w_out"][j], "nn", F32, add=h, norm=w["norm_ffn"][i][None, :])
            sv.update(pre=pre, u=u, v=v, gated=gated)
        sv["h1"] = h
        gate, up, hid = _ffn_up(un, w["ffn_w_gu"][i])
        h = _mm(hid, w["ffn_w_down"][i], "nn", F32, add=h)
        sv.update(un=un, gate=gate, up=up, hid=hid, h2=h)
        next_norm = w["norm_mix"][i + 1] if i + 1 < DEPTH else w["final_norm"]
        h, pe, hg, gl, hn = _ple_fwd(p[i], h, w["ple_w_proj"][i], w["ple_w_gate"][i], w["ple_gate_norm"][i][None, :],
                                     w["ple_norm"][i][None, :], next_norm[None, :])
        sv.update(pe=pe, hg=hg, gl=gl)
        saved.append(sv)

    lpart, dh, d_final = _loss_head(h, w["final_norm"][None, :], target)
    g = {k: [None] * (DEPTH if k in PER_LAYER else DEPTH // 2) for k in w if k != "final_norm"}
    g["final_norm"] = d_final[0]

    for i in reversed(range(DEPTH)):
        j = i // 2
        sv = saved[i]
        dh, g["ple_w_gate"][i], g["ple_w_proj"][i], d_ple_norm, d_gate_norm = _ple_bwd(
            dh, p[i], sv["pe"], sv["gl"], sv["hg"], sv["h2"], w["ple_w_gate"][i], w["ple_gate_norm"][i][None, :],
            w["ple_norm"][i][None, :])
        g["ple_norm"][i] = d_ple_norm[0]
        g["ple_gate_norm"][i] = d_gate_norm[0]
        g["ffn_w_down"][i] = _mm(sv["hid"], dh, "tn", F32)
        dgu = _ffn_down_bwd(dh, w["ffn_w_down"][i], sv["gate"], sv["up"])
        g["ffn_w_gu"][i] = _mm(sv["un"], dgu, "tn", F32)
        dun = _mm(dgu, w["ffn_w_gu"][i], "nt", BF16)
        dh, d_norm_ffn = _rms_bwd(dun, sv["h1"], w["norm_ffn"][i][None, :], dh)
        g["norm_ffn"][i] = d_norm_ffn[0]
        if i % 2 == 0:
            dgn = _mm(dh, w["ssd_w_out"][j], "nt", BF16)
            g["ssd_w_out"][j] = _mm(sv["gn"], dh, "tn", F32)
            dy, dzx, d_norm_w = _ssd_gate_bwd(dgn, sv["y"], sv["zx"], w["ssd_norm_w"][j][None, :])
            g["ssd_norm_w"][j] = d_norm_w[0]
            dxbc, ddtp, d_bias, d_alog, d_d = _ssd_scan_bwd(sv["xbc"], sv["dtp"], sv["prev"], dy, w["ssd_dt_bias"][j],
                                                            w["ssd_a_log"][j], w["ssd_d"][j])
            g["ssd_dt_bias"][j] = d_bias[0, :SSD_HEADS]
            g["ssd_a_log"][j] = d_alog[0, :SSD_HEADS]
            g["ssd_d"][j] = d_d[0, :SSD_HEADS]
            dzx, d_conv_w, d_conv_b = _ssd_conv_bwd(sv["zx"], dxbc, w["ssd_conv_w"][j], w["ssd_conv_b"][j][None, :], dzx)
            g["ssd_conv_w"][j] = d_conv_w[:CONV_K]
            g["ssd_conv_b"][j] = d_conv_b[0]
            g["ssd_w_zx"][j] = _mm(sv["hn"], dzx, "tn", F32)
            g["ssd_w_dt"][j] = _mm(sv["hn"], ddtp, "tn", F32)
            dhn = _mm(ddtp, w["ssd_w_dt"][j], "nt", F32)
            dhn = _mm(dzx, w["ssd_w_zx"][j], "nt", BF16, add=dhn)
        else:
            dgated = _mm(dh, w["gmlp_w_out"][j], "nt", BF16)
            g["gmlp_w_out"][j] = _mm(sv["gated"], dh, "tn", F32)
            du, dv, d_ws, d_bst = _gmlp_mix_bwd(dgated, sv["u"], sv["v"], w["gmlp_w_s"][j], w["gmlp_b_st"][j])
            g["gmlp_w_s"][j] = d_ws
            g["gmlp_b_st"][j] = d_bst
            dpre, d_bin, d_lnw, d_lnb = _gmlp_act_bwd(sv["pre"], w["gmlp_b_in"][j][None, :], w["gmlp_ln_w"][j][None, :],
                                                     du, dv)
            g["gmlp_b_in"][j] = d_bin[0]
            g["gmlp_ln_w"][j] = d_lnw[0]
            g["gmlp_ln_b"][j] = d_lnb[0]
            g["gmlp_w_in"][j] = _mm(sv["hn"], dpre, "tn", F32)
            dhn = _mm(dpre, w["gmlp_w_in"][j], "nt", BF16)
        dh, d_norm_mix = _rms_bwd(dhn, sv["h0"], w["norm_mix"][i][None, :], dh)
        g["norm_mix"][i] = d_norm_mix[0]
    return lpart[0, 0], dh, g


PACK_COLS = 1024
ANY = pl.BlockSpec(memory_space=pl.ANY)


def _mesh_pos():
    return lax.axis_index("x"), lax.axis_index("y"), lax.axis_index("c")


def _all_gather(xs_list, name):
    n = len(xs_list)

    def body(*refs):
        x_refs, out_refs = refs[:n], refs[n:2 * n]
        send_sems, recv_sems, local_sems = refs[2 * n:]
        x, y, c = _mesh_pos()
        me, sibling = (x, y, c), (x, y, 1 - c)
        chips = [(1 - x, y), (x, 1 - y), (1 - x, 1 - y)]

        def copy(a, k, block, to, from_input=False):
            px, py, pc = block
            dst = out_refs[a].at[4 * px + 2 * py + pc]
            return pltpu.make_async_remote_copy(
                src_ref=x_refs[a] if from_input else dst, dst_ref=dst,
                send_sem=send_sems.at[7 * a + k], recv_sem=recv_sems.at[7 * a + k], device_id=to,
                device_id_type=MESH_ID)

        mine = [pltpu.make_async_copy(x_refs[a], out_refs[a].at[4 * x + 2 * y + c], local_sems.at[a]) for a in range(n)]
        for cp in mine:
            cp.start()
        first = []
        for a in range(n):
            first += [copy(a, 1 + j, me, (*chip, c), from_input=True) for j, chip in enumerate(chips)]
            first.append(copy(a, 0, me, sibling, from_input=True))
        for cp in first:
            cp.start()
        passed = []
        for a in range(n):
            for j, chip in enumerate(chips):
                copy(a, 1 + j, (*chip, c), me).wait_recv()
                fwd = copy(a, 4 + j, (*chip, c), sibling)
                fwd.start()
                passed.append(fwd)
        for a in range(n):
            copy(a, 0, sibling, me).wait_recv()
            for j, chip in enumerate(chips):
                copy(a, 4 + j, (*chip, 1 - c), me).wait_recv()
        for cp in first + passed:
            cp.wait_send()
        for cp in mine:
            cp.wait()

    outs = pl.pallas_call(
        body,
        out_shape=[jax.ShapeDtypeStruct((N_DEV,) + t.shape, t.dtype) for t in xs_list],
        in_specs=[ANY] * n,
        out_specs=[ANY] * n,
        scratch_shapes=[pltpu.SemaphoreType.DMA((7 * n,)), pltpu.SemaphoreType.DMA((7 * n,)),
                        pltpu.SemaphoreType.DMA((n,))],
        name=name,
    )(*xs_list)
    return list(outs)


def _exchange_sibling(send_list):
    n = len(send_list)

    def body(*refs):
        s_refs, land_refs = refs[:n], refs[n:2 * n]
        send_sems, recv_sems = refs[2 * n:]
        x, y, c = _mesh_pos()
        cps = [pltpu.make_async_remote_copy(src_ref=s_refs[a], dst_ref=land_refs[a], send_sem=send_sems.at[a],
                                            recv_sem=recv_sems.at[a], device_id=(x, y, 1 - c), device_id_type=MESH_ID)
               for a in range(n)]
        for cp in cps:
            cp.start()
        for cp in cps:
            cp.wait()

    outs = pl.pallas_call(
        body,
        out_shape=[jax.ShapeDtypeStruct(t.shape, t.dtype) for t in send_list],
        in_specs=[ANY] * n,
        out_specs=[ANY] * n,
        scratch_shapes=[pltpu.SemaphoreType.DMA((n,)), pltpu.SemaphoreType.DMA((n,))],
        name="rs_exchange_sibling",
    )(*send_list)
    return list(outs)


def _exchange_chips(partial_list):
    n = len(partial_list)

    def body(*refs):
        p_refs, land_refs = refs[:n], refs[n:2 * n]
        send_sems, recv_sems, local_sems = refs[2 * n:]
        x, y, c = _mesh_pos()
        chips = [(1 - x, y), (x, 1 - y), (1 - x, 1 - y)]
        cps = [pltpu.make_async_remote_copy(src_ref=p_refs[a].at[2 * cx + cy], dst_ref=land_refs[a].at[j],
                                            send_sem=send_sems.at[3 * a + j], recv_sem=recv_sems.at[3 * a + j],
                                            device_id=(cx, cy, c), device_id_type=MESH_ID)
               for a in range(n) for j, (cx, cy) in enumerate(chips)]
        mine = [pltpu.make_async_copy(p_refs[a].at[2 * x + y], land_refs[a].at[3], local_sems.at[a]) for a in range(n)]
        for cp in cps + mine:
            cp.start()
        for cp in cps + mine:
            cp.wait()

    outs = pl.pallas_call(
        body,
        out_shape=[jax.ShapeDtypeStruct((4,) + t.shape[1:], t.dtype) for t in partial_list],
        in_specs=[ANY] * n,
        out_specs=[ANY] * n,
        scratch_shapes=[pltpu.SemaphoreType.DMA((3 * n,)), pltpu.SemaphoreType.DMA((3 * n,)),
                        pltpu.SemaphoreType.DMA((n,))],
        name="rs_exchange_chips",
    )(*partial_list)
    return list(outs)


def _sum_pairs(a, b):
    shape = a.shape
    a = a.reshape(shape[0], -1, shape[-1])
    b = b.reshape(a.shape)
    n, r, cdim = a.shape
    tr = _pick(r, (1024, 512, 256, 128, 64))

    def body(a_ref, b_ref, o_ref):
        o_ref[...] = (a_ref[...].astype(F32) + b_ref[...].astype(F32)).astype(o_ref.dtype)

    blk = pl.BlockSpec((1, tr, cdim), lambda i, j: (i, j, 0))
    return pl.pallas_call(
        body, grid=(n, r // tr), in_specs=[blk, blk], out_specs=blk,
        out_shape=jax.ShapeDtypeStruct(a.shape, a.dtype),
        compiler_params=_params(("parallel", "parallel")), name="rs_sum_pairs",
    )(a, b).reshape(shape)


def _sum_final(land):
    shape = land.shape[1:]
    land = land.reshape(4, -1, shape[-1])
    _, r, cdim = land.shape
    tr = _pick(r, (1024, 512, 256, 128, 64))

    def body(l_ref, out_ref):
        acc = l_ref[3].astype(F32)
        for j in range(3):
            acc = acc + l_ref[j].astype(F32)
        out_ref[...] = acc

    return pl.pallas_call(
        body, grid=(r // tr,),
        in_specs=[pl.BlockSpec((4, tr, cdim), lambda i: (0, i, 0))],
        out_specs=pl.BlockSpec((tr, cdim), lambda i: (i, 0)),
        out_shape=jax.ShapeDtypeStruct((r, cdim), F32),
        compiler_params=_params(("parallel",)), name="rs_sum_final",
    )(land).reshape(shape)


def _sum_devices(gathered):
    n, r, cdim = gathered.shape
    tr = _pick(r, (64, 32, 16, 8))

    def body(g_ref, out_ref):
        acc = g_ref[0].astype(F32)
        for q in range(1, n):
            acc = acc + g_ref[q].astype(F32)
        out_ref[...] = acc

    return pl.pallas_call(
        body, grid=(r // tr,),
        in_specs=[pl.BlockSpec((n, tr, cdim), lambda i: (0, i, 0))],
        out_specs=pl.BlockSpec((tr, cdim), lambda i: (i, 0)),
        out_shape=jax.ShapeDtypeStruct((r, cdim), F32),
        compiler_params=_params(("parallel",)), name="sum_devices",
    )(gathered)


def _adamw(w, g, m, v):
    shape = w.shape
    cols = shape[-1]
    rows = w.size // cols
    tr = _pick(rows, (512, 256, 128, 64, 32, 16, 8))
    c1 = 1.0 - ADAM_B1 ** ADAM_STEP
    c2 = 1.0 - ADAM_B2 ** ADAM_STEP

    def body(w_ref, g_ref, m_ref, v_ref, d_ref, nm_ref, nv_ref):
        gv = g_ref[...]
        m2 = ADAM_B1 * m_ref[...] + (1.0 - ADAM_B1) * gv
        v2 = ADAM_B2 * v_ref[...] + (1.0 - ADAM_B2) * (gv * gv)
        d_ref[...] = -ADAM_LR * ((m2 / c1) / (jnp.sqrt(v2 / c2) + ADAM_EPS) + ADAM_WD * w_ref[...])
        nm_ref[...] = m2
        nv_ref[...] = v2

    blk = pl.BlockSpec((tr, cols), lambda i: (i, 0))
    sds = jax.ShapeDtypeStruct((rows, cols), F32)
    outs = pl.pallas_call(
        body, grid=(rows // tr,), in_specs=[blk] * 4, out_specs=[blk] * 3, out_shape=[sds] * 3,
        compiler_params=_params(("parallel",)), name=f"adamw_{rows}x{cols}",
    )(*(t.reshape(rows, cols) for t in (w, g, m, v)))
    return tuple(o.reshape(shape) for o in outs)


WEIGHTS = ("norm_mix", "norm_ffn", "ssd_w_in", "ssd_conv_w", "ssd_conv_b", "ssd_dt_bias", "ssd_a_log", "ssd_d",
           "ssd_norm_w", "ssd_w_out", "gmlp_w_in", "gmlp_b_in", "gmlp_ln_w", "gmlp_ln_b", "gmlp_w_s", "gmlp_b_s",
           "gmlp_w_out", "ffn_w_gate", "ffn_w_up", "ffn_w_down", "ple_w_proj", "ple_norm", "ple_gate_norm",
           "ple_w_gate", "final_norm")
ARG_NAMES = ("x", "p") + WEIGHTS + ("loss_target",) + tuple("m_" + n for n in WEIGHTS) + tuple("v_" + n for n in WEIGHTS)
SHARD_AXIS = {"ssd_w_in": 2, "ssd_conv_w": 2, "ssd_w_out": 1, "gmlp_w_in": 2, "gmlp_b_in": 1, "gmlp_ln_w": 1,
              "gmlp_ln_b": 1, "gmlp_w_out": 1, "ffn_w_gate": 2, "ffn_w_up": 2, "ffn_w_down": 1, "ple_w_proj": 2,
              "ple_w_gate": 1}
GATHER_BF16 = ("ssd_w_in", "ssd_w_out", "gmlp_w_in", "gmlp_w_out", "ffn_w_gate", "ffn_w_up", "ffn_w_down",
               "ple_w_proj", "ple_w_gate")
GATHER_F32 = ("ssd_conv_w", "gmlp_b_in", "gmlp_ln_w", "gmlp_ln_b")
SHARDED = GATHER_BF16 + GATHER_F32
WIDE = ("ssd_w_in", "ffn_w_gate", "ffn_w_up")
REPLICATED = tuple(n for n in WEIGHTS if n not in SHARD_AXIS)


def _pack(arrs, dtype, row_mult, lead=0):
    flat = jnp.concatenate([t.reshape(t.shape[:lead] + (-1,)).astype(dtype) for t in arrs], axis=lead)
    n = flat.shape[-1]
    unit = row_mult * PACK_COLS
    total = -(-n // unit) * unit
    flat = jnp.pad(flat, [(0, 0)] * lead + [(0, total - n)])
    return flat.reshape(flat.shape[:lead] + (total // PACK_COLS, PACK_COLS))


def _unpack(buf, names, shapes, lead=0):
    flat = buf.reshape(buf.shape[:lead] + (-1,))
    out, off = {}, 0
    for n in names:
        size = math.prod(shapes[n])
        out[n] = lax.slice_in_dim(flat, off, off + size, axis=lead).reshape(buf.shape[:lead] + tuple(shapes[n]))
        off += size
    return out


ROW_PACKED = ((1024, ("ssd_w_out", "gmlp_w_out", "ffn_w_down", "ple_w_gate")), (512, ("gmlp_w_in",)),
              (128, ("ple_w_proj",)))
ROW_PACK_MULT = 1024


def _pack_rows(arrs, width, lead=0):
    parts = [t.reshape(t.shape[:lead] + (-1, width)).astype(BF16) for t in arrs]
    rows = sum(t.shape[lead] for t in parts)
    pad = -rows % ROW_PACK_MULT
    if pad:
        parts.append(jnp.zeros(parts[0].shape[:lead] + (pad, width), BF16))
    return jnp.concatenate(parts, axis=lead)


def _unpack_rows(buf, names, shapes, lead=0):
    width = buf.shape[-1]
    out, off = {}, 0
    for n in names:
        rows = math.prod(shapes[n]) // width
        out[n] = lax.slice_in_dim(buf, off, off + rows, axis=lead).reshape(buf.shape[:lead] + tuple(shapes[n]))
        off += rows
    return out


def _merge_shards(seg, ax):
    t = jnp.moveaxis(seg, 0, ax)
    return t.reshape(t.shape[:ax] + (t.shape[ax] * t.shape[ax + 1],) + t.shape[ax + 2:])


def _split_for_cores(gfull, ax, c):
    shp = gfull.shape
    t = gfull.reshape(shp[:ax] + (2, 2, 2, shp[ax] // N_DEV) + shp[ax + 1:])

    def take(core):
        u = lax.dynamic_index_in_dim(t, core, axis=ax + 2, keepdims=False)
        u = jnp.moveaxis(u, (ax, ax + 1), (0, 1))
        return u.reshape((4,) + u.shape[2:])

    return take(c), take(1 - c)


def kernel(x, p, norm_mix, norm_ffn, ssd_w_in, ssd_conv_w, ssd_conv_b, ssd_dt_bias, ssd_a_log, ssd_d,
           ssd_norm_w, ssd_w_out, gmlp_w_in, gmlp_b_in, gmlp_ln_w, gmlp_ln_b, gmlp_w_s, gmlp_b_s,
           gmlp_w_out, ffn_w_gate, ffn_w_up, ffn_w_down, ple_w_proj, ple_norm, ple_gate_norm, ple_w_gate,
           final_norm, loss_target, m_norm_mix, m_norm_ffn, m_ssd_w_in, m_ssd_conv_w, m_ssd_conv_b,
           m_ssd_dt_bias, m_ssd_a_log, m_ssd_d, m_ssd_norm_w, m_ssd_w_out, m_gmlp_w_in, m_gmlp_b_in,
           m_gmlp_ln_w, m_gmlp_ln_b, m_gmlp_w_s, m_gmlp_b_s, m_gmlp_w_out, m_ffn_w_gate, m_ffn_w_up,
           m_ffn_w_down, m_ple_w_proj, m_ple_norm, m_ple_gate_norm, m_ple_w_gate, m_final_norm, v_norm_mix,
           v_norm_ffn, v_ssd_w_in, v_ssd_conv_w, v_ssd_conv_b, v_ssd_dt_bias, v_ssd_a_log, v_ssd_d,
           v_ssd_norm_w, v_ssd_w_out, v_gmlp_w_in, v_gmlp_b_in, v_gmlp_ln_w, v_gmlp_ln_b, v_gmlp_w_s,
           v_gmlp_b_s, v_gmlp_w_out, v_ffn_w_gate, v_ffn_w_up, v_ffn_w_down, v_ple_w_proj, v_ple_norm,
           v_ple_gate_norm, v_ple_w_gate, v_final_norm):
    given = locals()
    a = {n: given[n] for n in ARG_NAMES}
    mx, my, c = _mesh_pos()
    xs = a["x"][0]
    ps = a["p"][:, 0]
    target = a["loss_target"][0]
    shard_shapes = {n: a[n].shape for n in WEIGHTS}

    full = {n: a[n] for n in REPLICATED}
    row_packs = [_pack_rows([a[n] for n in names], wd) for wd, names in ROW_PACKED]
    got = _all_gather(row_packs + [_pack([a[n] for n in GATHER_F32], F32, 8)] + [_to_bf16(a[n]) for n in WIDE],
                      "ag_weights")
    for (wd, names), buf in zip(ROW_PACKED, got):
        for n, seg in _unpack_rows(buf, names, shard_shapes, lead=1).items():
            full[n] = _merge_shards(seg, SHARD_AXIS[n])
    k0 = len(ROW_PACKED)
    for n, seg in _unpack(got[k0], GATHER_F32, shard_shapes, lead=1).items():
        full[n] = _merge_shards(seg, SHARD_AXIS[n])
    full["ssd_w_zx"], full["ssd_w_dt"] = _cat_ssd_in(got[k0 + 1])
    full["ffn_w_gu"] = _cat_ffn(got[k0 + 2], got[k0 + 3])

    lpart, dx, g = _local_step(xs, ps, target, _kernel_layouts(full))
    gfull = _reference_layouts(g, wide=False)
    loss = lax.psum(lpart, ("x", "y", "c"))

    def by_core(t):
        u = t.reshape((4, 2) + t.shape[1:])
        return (lax.dynamic_index_in_dim(u, c, axis=1, keepdims=False),
                lax.dynamic_index_in_dim(u, 1 - c, axis=1, keepdims=False))

    def layer_halves(gl, ax):
        if ax == 0:
            t = gl.reshape(4, 2, -1, gl.shape[-1])
            return tuple(lax.dynamic_index_in_dim(t, cc, axis=1, keepdims=False) for cc in (c, 1 - c))
        t = gl.reshape(gl.shape[0], 4, 2, -1)
        return tuple(jnp.moveaxis(lax.dynamic_index_in_dim(t, cc, axis=2, keepdims=False), 1, 0) for cc in (c, 1 - c))

    pairs = []
    for wd, names in ROW_PACKED:
        hs = [layer_halves(gl, SHARD_AXIS[n] - 1) for n in names for gl in g[n]]
        pairs.append(tuple(_pack_rows([h[i] for h in hs], wd, lead=1) for i in (0, 1)))
    halves = [_split_for_cores(gfull[n], SHARD_AXIS[n], c) for n in GATHER_F32]
    pairs.append((_pack([h[0] for h in halves], BF16, 16, lead=1), _pack([h[1] for h in halves], BF16, 16, lead=1)))
    pairs += [by_core(t) for t in (_split_ssd_in(g["ssd_w_zx"], g["ssd_w_dt"]),) + tuple(_split_ffn(g["ffn_w_gu"]))]
    landed = _exchange_sibling([s for _, s in pairs])
    partials = [_sum_pairs(k, l) for (k, _), l in zip(pairs, landed)]
    landed = _exchange_chips(partials)
    sums = [_sum_final(l) for l in landed]
    gshard = {}
    for (wd, names), buf in zip(ROW_PACKED, sums):
        gshard.update(_unpack_rows(buf, names, shard_shapes))
    gshard.update(_unpack(sums[k0], GATHER_F32, shard_shapes))
    gshard.update(zip(WIDE, sums[k0 + 1:]))
    rep = _all_gather([_pack([gfull[n] for n in REPLICATED], BF16, 64)], "ag_replicated_grads")[0]
    grep = _unpack(_sum_devices(rep), REPLICATED, shard_shapes)
    grads = {**gshard, **grep}

    upd = {n: _adamw(a[n], grads[n], a["m_" + n], a["v_" + n]) for n in WEIGHTS}
    return (loss, dx[None], *[grads[n] for n in WEIGHTS], *[upd[n][0] for n in WEIGHTS],
            *[upd[n][1] for n in WEIGHTS], *[upd[n][2] for n in WEIGHTS])
```
